```python
import jax, jax.numpy as jnp
from jax import lax
import numpy as np

D_MODEL = 1024
BATCH = 4
SEQ = 4096
DEPTH = 2

HEAD_DIM = 64
N_Q_HEADS = 8
N_KV_HEADS = 2
Q_PER_KV = N_Q_HEADS // N_KV_HEADS
ATTN_WIDTH = N_Q_HEADS * HEAD_DIM
KV_WIDTH = N_KV_HEADS * HEAD_DIM
WINDOW = 128
ATTN_BLOCK = 128
ROPE_DIM = HEAD_DIM // 4
ROPE_THETA = 500000.0
N_SG_GROUPS = 8
SG_GROUP_DIM = 64
SG_WIDTH = N_SG_GROUPS * SG_GROUP_DIM
SG_CHUNK = 128
EVEN_GATE_WIDTH = ATTN_WIDTH + SG_WIDTH
EVEN_SPLITS = tuple(int(s) for s in np.cumsum([ATTN_WIDTH, KV_WIDTH, KV_WIDTH, SG_WIDTH, SG_WIDTH]))
EVEN_IN_WIDTH = EVEN_SPLITS[-1] + EVEN_GATE_WIDTH
RNN_WIDTH = D_MODEL
RNN_HEADS = 8
RNN_HEAD_DIM = RNN_WIDTH // RNN_HEADS
CONV_WIDTH = 4
CONV_PAD = (2, 1)
RG_LRU_C = 8.0
ODD_IN_WIDTH = 2 * RNN_WIDTH
N_EVEN = (DEPTH + 1) // 2
N_ODD = DEPTH // 2
DEEPNORM_ALPHA = (2 * DEPTH) ** 0.25
DEEPNORM_BETA = (8 * DEPTH) ** -0.25
LN_EPS = 1e-5
NEG_INF = -1e30

kernel_name = "hybrid_swa_gmlp_rglru_deepnorm_encoder"


def layer_norm(x, g, b):
    xf = x.astype(jnp.float32)
    mu = xf.mean(-1, keepdims=True)
    var = jnp.square(xf - mu).mean(-1, keepdims=True)
    y = (xf - mu) * lax.rsqrt(var + LN_EPS)
    return (y * g.astype(jnp.float32) + b.astype(jnp.float32)).astype(x.dtype)


def partial_rotary(t, positions):
    half = ROPE_DIM // 2
    inv_freq = jnp.power(jnp.float32(ROPE_THETA), -jnp.arange(half, dtype=jnp.float32) / half)
    ang = positions.astype(jnp.float32)[:, :, None, None] * inv_freq
    cos, sin = jnp.cos(ang), jnp.sin(ang)
    tr = t[..., :ROPE_DIM].astype(jnp.float32)
    t1, t2 = tr[..., :half], tr[..., half:]
    rot = jnp.concatenate([t1 * cos - t2 * sin, t2 * cos + t1 * sin], axis=-1)
    return jnp.concatenate([rot.astype(t.dtype), t[..., ROPE_DIM:]], axis=-1)


def windowed_gqa_sink(q, k, v, sink):
    B, S = q.shape[0], q.shape[1]
    nb = S // ATTN_BLOCK
    qb = q.reshape(B, nb, ATTN_BLOCK, N_KV_HEADS, Q_PER_KV, HEAD_DIM)

    def band(t):
        tp = jnp.pad(t, ((0, 0), (ATTN_BLOCK, ATTN_BLOCK), (0, 0), (0, 0)))
        parts = [tp[:, o * ATTN_BLOCK:o * ATTN_BLOCK + S].reshape(B, nb, ATTN_BLOCK, N_KV_HEADS, HEAD_DIM)
                 for o in range(3)]
        return jnp.concatenate(parts, axis=2)

    kb, vb = band(k), band(v)
    s = jnp.einsum('bnqhgd,bnkhd->bnhgqk', qb, kb).astype(jnp.float32) * (HEAD_DIM ** -0.5)
    qi = jnp.arange(ATTN_BLOCK)[:, None]
    kj = jnp.arange(3 * ATTN_BLOCK)[None, :]
    blk = jnp.arange(nb)[:, None, None]
    k_abs = blk * ATTN_BLOCK - ATTN_BLOCK + kj
    valid = (jnp.abs(kj - ATTN_BLOCK - qi) <= WINDOW) & (k_abs >= 0) & (k_abs < S)
    s = jnp.where(valid[None, :, None, None], s, NEG_INF)
    sink_l = sink.astype(jnp.float32).reshape(N_KV_HEADS, Q_PER_KV)[None, None, :, :, None, None]
    m = jnp.maximum(s.max(-1, keepdims=True), sink_l)
    p = jnp.exp(s - m)
    denom = p.sum(-1, keepdims=True) + jnp.exp(sink_l - m)
    o = jnp.einsum('bnhgqk,bnkhd->bnqhgd', (p / denom).astype(v.dtype), vb)
    return o.reshape(B, S, ATTN_WIDTH)


def chunked_spatial_gating(u, v, ln_g, ln_b, w_s, b_s):
    B, S = u.shape[0], u.shape[1]
    nc = S // SG_CHUNK
    vg = v.reshape(B, S, N_SG_GROUPS, SG_GROUP_DIM)
    vg = layer_norm(vg, ln_g.reshape(N_SG_GROUPS, SG_GROUP_DIM), ln_b.reshape(N_SG_GROUPS, SG_GROUP_DIM))
    vc = vg.reshape(B, nc, SG_CHUNK, N_SG_GROUPS, SG_GROUP_DIM)
    sv = jnp.einsum('gpq,bcqgd->bcpgd', w_s, vc) + b_s.T[None, None, :, :, None]
    return u * sv.reshape(B, S, SG_WIDTH)


def centred_depthwise_conv(x, w, b):
    y = lax.conv_general_dilated(x, w[:, None, :], window_strides=(1,), padding=[CONV_PAD],
                                 dimension_numbers=('NWC', 'WIO', 'NWC'),
                                 feature_group_count=x.shape[-1])
    return y + b


def rg_lru(x, w_a, b_a, w_x, b_x, lam, reverse):
    B, S = x.shape[0], x.shape[1]
    xh = x.reshape(B, S, RNN_HEADS, RNN_HEAD_DIM)
    pre_r = jnp.einsum('bshi,hij->bshj', xh, w_a).reshape(B, S, RNN_WIDTH) + b_a
    pre_i = jnp.einsum('bshi,hij->bshj', xh, w_x).reshape(B, S, RNN_WIDTH) + b_x
    rec_gate = jax.nn.sigmoid(pre_r.astype(jnp.float32))
    in_gate = jax.nn.sigmoid(pre_i.astype(jnp.float32))
    log_a = -RG_LRU_C * rec_gate * jax.nn.softplus(-lam.astype(jnp.float32))
    a = jnp.exp(log_a)
    bterm = jnp.sqrt(-jnp.expm1(2.0 * log_a)) * in_gate * x.astype(jnp.float32)

    def combine(lhs, rhs):
        a1, b1 = lhs
        a2, b2 = rhs
        return a1 * a2, a2 * b1 + b2

    _, h = lax.associative_scan(combine, (a, bterm), axis=1, reverse=reverse)
    return h


def even_mixer(h, positions, w_in, w_out, sink, sg_ln_g, sg_ln_b, sg_w, sg_b):
    B, S = h.shape[0], h.shape[1]
    q, k, v, su, sv, g = jnp.split(h @ w_in, EVEN_SPLITS, axis=-1)
    q = partial_rotary(q.reshape(B, S, N_Q_HEADS, HEAD_DIM), positions)
    k = partial_rotary(k.reshape(B, S, N_KV_HEADS, HEAD_DIM), positions)
    v = v.reshape(B, S, N_KV_HEADS, HEAD_DIM)
    y_attn = windowed_gqa_sink(q, k, v, sink)
    y_sg = chunked_spatial_gating(su, sv, sg_ln_g, sg_ln_b, sg_w, sg_b)
    y = jnp.concatenate([y_attn, y_sg], axis=-1) * jax.nn.silu(g)
    return y @ w_out


def odd_mixer(h, w_in, conv_w, conv_b, w_a, b_a, w_x, b_x, lam, w_out):
    xr, g = jnp.split(h @ w_in, 2, axis=-1)
    xr = centred_depthwise_conv(xr, conv_w, conv_b)
    y = (rg_lru(xr, w_a[0], b_a[0], w_x[0], b_x[0], lam[0], reverse=False)
         + rg_lru(xr, w_a[1], b_a[1], w_x[1], b_x[1], lam[1], reverse=True))
    y = y.astype(h.dtype) * jax.nn.silu(g)
    return y @ w_out


def setup_inputs(seed: int = 0) -> dict:
    key = jax.random.key(seed)
    ks = jax.random.split(key, 24)
    f32 = jnp.float32
    nrm = lambda k, shape, s: jax.random.normal(k, shape, f32) * s
    a_c = jax.random.uniform(ks[22], (N_ODD, 2, RNN_WIDTH), f32, minval=0.9, maxval=0.999)
    p = a_c ** (1.0 / RG_LRU_C)
    lam = jnp.log(p) - jnp.log1p(-p)
    return {
        "x": nrm(ks[0], (BATCH, SEQ, D_MODEL), 1.0),
        "c": nrm(ks[1], (BATCH, D_MODEL), 1.0),
        "positions": jnp.broadcast_to(jnp.arange(SEQ, dtype=jnp.int32), (BATCH, SEQ)),
        "ada_w": nrm(ks[2], (DEPTH, D_MODEL, 3 * D_MODEL), D_MODEL ** -0.5),
        "ada_b": nrm(ks[3], (DEPTH, 3 * D_MODEL), 0.01),
        "ln_g": 1.0 + nrm(ks[4], (DEPTH, D_MODEL), 0.02),
        "ln_b": nrm(ks[5], (DEPTH, D_MODEL), 0.02),
        "ev_w_in": nrm(ks[6], (N_EVEN, D_MODEL, EVEN_IN_WIDTH), D_MODEL ** -0.5),
        "ev_w_out": nrm(ks[7], (N_EVEN, EVEN_GATE_WIDTH, D_MODEL), DEEPNORM_BETA * EVEN_GATE_WIDTH ** -0.5),
        "ev_sink": nrm(ks[8], (N_EVEN, N_Q_HEADS), 1.0),
        "ev_sg_ln_g": 1.0 + nrm(ks[9], (N_EVEN, SG_WIDTH), 0.02),
        "ev_sg_ln_b": nrm(ks[10], (N_EVEN, SG_WIDTH), 0.02),
        "ev_sg_w": nrm(ks[11], (N_EVEN, N_SG_GROUPS, SG_CHUNK, SG_CHUNK), SG_CHUNK ** -0.5),
        "ev_sg_b": 1.0 + nrm(ks[12], (N_EVEN, N_SG_GROUPS, SG_CHUNK), 0.1),
        "od_w_in": nrm(ks[13], (N_ODD, D_MODEL, ODD_IN_WIDTH), D_MODEL ** -0.5),
        "od_conv_w": nrm(ks[14], (N_ODD, CONV_WIDTH, RNN_WIDTH), CONV_WIDTH ** -0.5),
        "od_conv_b": nrm(ks[15], (N_ODD, RNN_WIDTH), 0.01),
        "od_w_a": nrm(ks[16], (N_ODD, 2, RNN_HEADS, RNN_HEAD_DIM, RNN_HEAD_DIM), RNN_HEAD_DIM ** -0.5),
        "od_b_a": nrm(ks[17], (N_ODD, 2, RNN_WIDTH), 0.01),
        "od_w_x": nrm(ks[18], (N_ODD, 2, RNN_HEADS, RNN_HEAD_DIM, RNN_HEAD_DIM), RNN_HEAD_DIM ** -0.5),
        "od_b_x": nrm(ks[19], (N_ODD, 2, RNN_WIDTH), 0.01),
        "od_lam": lam,
        "od_w_out": nrm(ks[20], (N_ODD, RNN_WIDTH, D_MODEL), DEEPNORM_BETA * RNN_WIDTH ** -0.5),
    }


def reference(x, c, positions, ada_w, ada_b, ln_g, ln_b, ev_w_in, ev_w_out, ev_sink, ev_sg_ln_g, ev_sg_ln_b,
              ev_sg_w, ev_sg_b, od_w_in, od_conv_w, od_conv_b, od_w_a, od_b_a, od_w_x, od_b_x, od_lam, od_w_out):
    cond = jax.nn.silu(c)
    for layer in range(DEPTH):
        mod = cond @ ada_w[layer] + ada_b[layer]
        shift, scale, gate = jnp.split(mod, 3, axis=-1)
        h = x * (1.0 + scale[:, None, :]) + shift[:, None, :]
        j = layer // 2
        if layer % 2 == 0:
            y = even_mixer(h, positions, ev_w_in[j], ev_w_out[j], ev_sink[j], ev_sg_ln_g[j], ev_sg_ln_b[j],
                           ev_sg_w[j], ev_sg_b[j])
        else:
            y = odd_mixer(h, od_w_in[j], od_conv_w[j], od_conv_b[j], od_w_a[j], od_b_a[j], od_w_x[j],
                          od_b_x[j], od_lam[j], od_w_out[j])
        x = layer_norm(DEEPNORM_ALPHA * x + gate[:, None, :] * y, ln_g[layer], ln_b[layer])
    return x
```

```python
import functools

import numpy as np
import jax
import jax.numpy as jnp
from jax import lax
from jax.experimental import pallas as pl
from jax.experimental.pallas import tpu as pltpu

D_MODEL = 1024
DEPTH = 2
HEAD_DIM = 64
N_Q_HEADS = 8
N_KV_HEADS = 2
Q_PER_KV = N_Q_HEADS // N_KV_HEADS
ATTN_WIDTH = N_Q_HEADS * HEAD_DIM
KV_WIDTH = N_KV_HEADS * HEAD_DIM
ATTN_BLOCK = 128
ROPE_DIM = HEAD_DIM // 4
ROPE_HALF = ROPE_DIM // 2
ROPE_THETA = 500000.0
N_SG_GROUPS = 8
SG_GROUP_DIM = 64
SG_WIDTH = N_SG_GROUPS * SG_GROUP_DIM
SG_CHUNK = 128
EVEN_GATE_WIDTH = ATTN_WIDTH + SG_WIDTH
EVEN_IN_WIDTH = ATTN_WIDTH + 2 * KV_WIDTH + 2 * SG_WIDTH + EVEN_GATE_WIDTH
RNN_WIDTH = D_MODEL
RNN_HEADS = 8
RNN_HEAD_DIM = RNN_WIDTH // RNN_HEADS
CONV_WIDTH = 4
CONV_LEFT = 2
RG_LRU_C = 8.0
DEEPNORM_ALPHA = (2 * DEPTH) ** 0.25
LN_EPS = 1e-5
NEG_INF = -1e30

LANES = 128
SUBLANES = 8
VMEM_LIMIT = 56 * 1024 * 1024

ROW_TILE = 512
N_SEG = 32
SCAN_CHUNK = 256

BF16 = jnp.bfloat16
F32 = jnp.float32


def _sigmoid(x):
    return 0.5 * jnp.tanh(0.5 * x) + 0.5


def _layer_norm_rows(z, g, b):
    mu = jnp.mean(z, axis=-1, keepdims=True)
    d = z - mu
    var = jnp.mean(d * d, axis=-1, keepdims=True)
    return d * lax.rsqrt(var + LN_EPS) * g + b


def _params(n_axes):
    return pltpu.CompilerParams(dimension_semantics=("parallel",) * n_axes,
                                vmem_limit_bytes=VMEM_LIMIT)


def _mod_kernel(c_ref, w_ref, b_ref, o_ref):
    c = c_ref[...]
    cond = (c * _sigmoid(c)).astype(BF16)
    o_ref[...] = jnp.dot(cond, w_ref[...].astype(BF16), preferred_element_type=F32) + b_ref[...]


def _modulation(c, ada_w, ada_b):
    batch = c.shape[0]
    rows = -(-batch // SUBLANES) * SUBLANES
    c_pad = jnp.pad(c, ((0, rows - batch), (0, 0)))
    n_col = 3 * D_MODEL // D_MODEL
    out = pl.pallas_call(
        _mod_kernel,
        grid=(DEPTH, n_col),
        in_specs=[
            pl.BlockSpec((rows, D_MODEL), lambda l, j: (0, 0)),
            pl.BlockSpec((None, D_MODEL, D_MODEL), lambda l, j: (l, 0, j)),
            pl.BlockSpec((None, 1, D_MODEL), lambda l, j: (l, 0, j)),
        ],
        out_specs=pl.BlockSpec((None, rows, D_MODEL), lambda l, j: (l, 0, j)),
        out_shape=jax.ShapeDtypeStruct((DEPTH, rows, 3 * D_MODEL), F32),
        compiler_params=_params(2),
        name="adaln_mod",
    )(c_pad, ada_w, ada_b.reshape(DEPTH, 1, 3 * D_MODEL))
    return out[:, :batch]


def _even_in_kernel(x_ref, shift_ref, scale_ref, pos_ref, freq_ref, sign_ref, w_ref,
                    q_ref, k_ref, v_ref, su_ref, sv_ref, g_ref):
    h = (x_ref[...] * (1.0 + scale_ref[...]) + shift_ref[...]).astype(BF16)
    p = jnp.dot(h, w_ref[...], preferred_element_type=F32)

    ang = pos_ref[...].astype(F32) * freq_ref[...]
    cs = jnp.cos(ang)
    sn = jnp.sin(ang) * sign_ref[...]
    lane = lax.broadcasted_iota(jnp.int32, (1, LANES), 1)
    first_half = (lane % HEAD_DIM) < ROPE_HALF

    def rotary(t):
        partner = jnp.where(first_half, pltpu.roll(t, LANES - ROPE_HALF, 1), pltpu.roll(t, ROPE_HALF, 1))
        return t * cs + partner * sn

    scale = HEAD_DIM ** -0.5
    for j in range(ATTN_WIDTH // LANES):
        q_ref[:, j * LANES:(j + 1) * LANES] = (rotary(p[:, j * LANES:(j + 1) * LANES]) * scale).astype(BF16)
    o = ATTN_WIDTH
    k_ref[...] = rotary(p[:, o:o + KV_WIDTH]).astype(BF16)
    o += KV_WIDTH
    v_ref[...] = p[:, o:o + KV_WIDTH].astype(BF16)
    o += KV_WIDTH
    su_ref[...] = p[:, o:o + SG_WIDTH].astype(BF16)
    o += SG_WIDTH
    sv_ref[...] = p[:, o:o + SG_WIDTH].astype(BF16)
    o += SG_WIDTH
    g_ref[...] = p[:, o:o + EVEN_GATE_WIDTH].astype(BF16)


def _even_in_proj(x, shift, scale, positions, w_in):
    batch, seq, _ = x.shape
    lane = np.arange(LANES)
    d = lane % HEAD_DIM
    inv_freq = jnp.power(jnp.float32(ROPE_THETA), -jnp.arange(ROPE_HALF, dtype=F32) / ROPE_HALF)
    freq_lane = jnp.where(jnp.asarray(d < ROPE_DIM), inv_freq[d % ROPE_HALF], 0.0).reshape(1, LANES)
    sign_lane = jnp.asarray(np.where(d < ROPE_HALF, -1.0, np.where(d < ROPE_DIM, 1.0, 0.0)),
                            dtype=F32).reshape(1, LANES)
    row = lambda width: pl.BlockSpec((None, ROW_TILE, width), lambda b, i: (b, i, 0))
    vec = pl.BlockSpec((None, 1, D_MODEL), lambda b, i: (b, 0, 0))
    const = lambda shape: pl.BlockSpec(shape, lambda b, i: (0,) * len(shape))
    widths = (ATTN_WIDTH, KV_WIDTH, KV_WIDTH, SG_WIDTH, SG_WIDTH, EVEN_GATE_WIDTH)
    return pl.pallas_call(
        _even_in_kernel,
        grid=(batch, seq // ROW_TILE),
        in_specs=[row(D_MODEL), vec, vec, row(1), const((1, LANES)), const((1, LANES)),
                  const((D_MODEL, EVEN_IN_WIDTH))],
        out_specs=[row(w) for w in widths],
        out_shape=[jax.ShapeDtypeStruct((batch, seq, w), BF16) for w in widths],
        compiler_params=_params(2),
        name="even_in_proj",
    )(x, shift, scale, positions.reshape(batch, seq, 1), freq_lane, sign_lane, w_in.astype(BF16))


def _even_mix_kernel(sink_ref, q_ref, k_ref, v_ref, su_ref, sv_ref, g_ref, x_ref, gate_ref,
                     avg_ref, sgg_ref, sgb_ref, sgw_ref, sgbias_ref, wout_ref, lng_ref, lnb_ref,
                     o_ref, y_scr):
    seq = k_ref.shape[0]
    n_blocks = seq // ATTN_BLOCK
    sub_blocks = ROW_TILE // ATTN_BLOCK
    tile = pl.program_id(1)
    rows_g = Q_PER_KV * ATTN_BLOCK

    qi = lax.broadcasted_iota(jnp.int32, (rows_g, 3 * ATTN_BLOCK), 0) % ATTN_BLOCK
    kj = lax.broadcasted_iota(jnp.int32, (rows_g, 3 * ATTN_BLOCK), 1)
    rel = kj - ATTN_BLOCK - qi
    in_window = (rel >= -ATTN_BLOCK) & (rel <= ATTN_BLOCK)
    lane = lax.broadcasted_iota(jnp.int32, (1, LANES), 1)
    left_group = lane < SG_GROUP_DIM

    for j in range(sub_blocks):
        blk = tile * sub_blocks + j
        prev = jnp.maximum(blk - 1, 0)
        nxt = jnp.minimum(blk + 1, n_blocks - 1)
        lo = jnp.where(blk == 0, ATTN_BLOCK, 0)
        hi = jnp.where(blk == n_blocks - 1, 2 * ATTN_BLOCK, 3 * ATTN_BLOCK)
        valid = in_window & (kj >= lo) & (kj < hi)
        r0 = j * ATTN_BLOCK

        def band(ref):
            parts = [ref[pl.ds(pl.multiple_of(b * ATTN_BLOCK, ATTN_BLOCK), ATTN_BLOCK), :]
                     for b in (prev, blk, nxt)]
            return jnp.concatenate(parts, axis=0)

        k_band = band(k_ref)
        v_band = band(v_ref)
        q_blk = q_ref[r0:r0 + ATTN_BLOCK, :]
        attn_cols = []
        for hk in range(N_KV_HEADS):
            k_h = k_band[:, hk * HEAD_DIM:(hk + 1) * HEAD_DIM]
            v_h = v_band[:, hk * HEAD_DIM:(hk + 1) * HEAD_DIM]
            heads = range(hk * Q_PER_KV, (hk + 1) * Q_PER_KV)
            q_g = jnp.concatenate([q_blk[:, h * HEAD_DIM:(h + 1) * HEAD_DIM] for h in heads], axis=0)
            sink_col = jnp.concatenate([jnp.full((ATTN_BLOCK, 1), sink_ref[h], F32) for h in heads], axis=0)
            s = lax.dot_general(q_g, k_h, (((1,), (1,)), ((), ())), preferred_element_type=F32)
            s = jnp.where(valid, s, NEG_INF)
            m = jnp.maximum(jnp.max(s, axis=-1, keepdims=True), sink_col)
            p = jnp.exp(s - m)
            denom = jnp.sum(p, axis=-1, keepdims=True) + jnp.exp(sink_col - m)
            o = jnp.dot(p.astype(BF16), v_h, preferred_element_type=F32) / denom
            attn_cols += [o[i * ATTN_BLOCK:(i + 1) * ATTN_BLOCK, :] for i in range(Q_PER_KV)]
        y_attn = jnp.concatenate(attn_cols, axis=1)

        g_blk = g_ref[r0:r0 + ATTN_BLOCK, :].astype(F32)
        silu_g = g_blk * _sigmoid(g_blk)
        y_scr[r0:r0 + ATTN_BLOCK, 0:ATTN_WIDTH] = (y_attn * silu_g[:, 0:ATTN_WIDTH]).astype(BF16)

        avg = avg_ref[...]
        for c in range(SG_WIDTH // LANES):
            cols = slice(c * LANES, (c + 1) * LANES)
            v_c = sv_ref[r0:r0 + ATTN_BLOCK, cols]
            mean = jnp.dot(v_c, avg, preferred_element_type=F32)
            dev = v_c.astype(F32) - mean
            sq = dev * dev
            sq_hi = sq.astype(BF16)
            sq_lo = (sq - sq_hi.astype(F32)).astype(BF16)
            var = (jnp.dot(sq_hi, avg, preferred_element_type=F32)
                   + jnp.dot(sq_lo, avg, preferred_element_type=F32))
            vn = dev * lax.rsqrt(var + LN_EPS) * sgg_ref[:, cols] + sgb_ref[:, cols]
            stacked = jnp.concatenate([jnp.where(left_group, vn, 0.0), jnp.where(left_group, 0.0, vn)],
                                      axis=0).astype(BF16)
            mixed = jnp.dot(sgw_ref[c], stacked, preferred_element_type=F32) + sgbias_ref[:, cols]
            y_sg = su_ref[r0:r0 + ATTN_BLOCK, cols].astype(F32) * mixed
            gc = slice(ATTN_WIDTH + c * LANES, ATTN_WIDTH + (c + 1) * LANES)
            y_scr[r0:r0 + ATTN_BLOCK, gc] = (y_sg * silu_g[:, gc]).astype(BF16)

    out = jnp.dot(y_scr[...], wout_ref[...], preferred_element_type=F32)
    z = DEEPNORM_ALPHA * x_ref[...] + gate_ref[...] * out
    o_ref[...] = _layer_norm_rows(z, lng_ref[...], lnb_ref[...])


def _even_mixer(q, k, v, su, sv, g, x, gate, sink, sg_ln_g, sg_ln_b, sg_w, sg_b, w_out, ln_g, ln_b):
    batch, seq, _ = x.shape
    lane = np.arange(LANES)
    avg = jnp.asarray((lane[:, None] // SG_GROUP_DIM == lane[None, :] // SG_GROUP_DIM) / SG_GROUP_DIM, dtype=BF16)
    w_pairs = sg_w.reshape(N_SG_GROUPS // 2, 2, SG_CHUNK, SG_CHUNK).transpose(0, 2, 1, 3)
    w_pairs = w_pairs.reshape(N_SG_GROUPS // 2, SG_CHUNK, 2 * SG_CHUNK).astype(BF16)
    bias_full = jnp.repeat(sg_b.T, SG_GROUP_DIM, axis=1)
    row = lambda width: pl.BlockSpec((None, ROW_TILE, width), lambda b, i: (b, i, 0))
    full_seq = pl.BlockSpec((None, seq, KV_WIDTH), lambda b, i: (b, 0, 0))
    const = lambda shape: pl.BlockSpec(shape, lambda b, i: (0,) * len(shape))
    return pl.pallas_call(
        _even_mix_kernel,
        grid=(batch, seq // ROW_TILE),
        in_specs=[
            pl.BlockSpec(memory_space=pltpu.SMEM),
            row(ATTN_WIDTH), full_seq, full_seq, row(SG_WIDTH), row(SG_WIDTH), row(EVEN_GATE_WIDTH),
            row(D_MODEL), pl.BlockSpec((None, 1, D_MODEL), lambda b, i: (b, 0, 0)),
            const((LANES, LANES)), const((1, SG_WIDTH)), const((1, SG_WIDTH)),
            const((N_SG_GROUPS // 2, SG_CHUNK, 2 * SG_CHUNK)), const((SG_CHUNK, SG_WIDTH)),
            const((EVEN_GATE_WIDTH, D_MODEL)), const((1, D_MODEL)), const((1, D_MODEL)),
        ],
        out_specs=row(D_MODEL),
        out_shape=jax.ShapeDtypeStruct((batch, seq, D_MODEL), F32),
        scratch_shapes=[pltpu.VMEM((ROW_TILE, EVEN_GATE_WIDTH), BF16)],
        compiler_params=_params(2),
        name="even_mixer",
    )(sink, q, k, v, su, sv, g, x, gate, avg, sg_ln_g.reshape(1, -1), sg_ln_b.reshape(1, -1),
      w_pairs, bias_full, w_out.astype(BF16), ln_g.reshape(1, -1), ln_b.reshape(1, -1))


def _odd_in_kernel(x_ref, shift_ref, scale_ref, w_ref, xr_ref, g_ref):
    seg_len = xr_ref.shape[0]
    h = (x_ref[...] * (1.0 + scale_ref[...]) + shift_ref[...]).astype(BF16)
    p = jnp.dot(h, w_ref[...], preferred_element_type=F32).astype(BF16)
    for s in range(ROW_TILE // seg_len):
        rows = slice(s * seg_len, (s + 1) * seg_len)
        xr_ref[:, s * RNN_WIDTH:(s + 1) * RNN_WIDTH] = p[rows, 0:RNN_WIDTH]
        g_ref[:, s * RNN_WIDTH:(s + 1) * RNN_WIDTH] = p[rows, RNN_WIDTH:2 * RNN_WIDTH]


def _odd_in_proj(x, shift, scale, w_in):
    batch, seq, _ = x.shape
    seg_len = seq // N_SEG
    segs_per_tile = ROW_TILE // seg_len
    vec = pl.BlockSpec((None, 1, D_MODEL), lambda b, i: (b, 0, 0))
    out_spec = pl.BlockSpec((None, seg_len, segs_per_tile * RNN_WIDTH), lambda b, i: (b, 0, i))
    out_shape = jax.ShapeDtypeStruct((batch, seg_len, N_SEG * RNN_WIDTH), BF16)
    return pl.pallas_call(
        _odd_in_kernel,
        grid=(batch, seq // ROW_TILE),
        in_specs=[pl.BlockSpec((None, ROW_TILE, D_MODEL), lambda b, i: (b, i, 0)), vec, vec,
                  pl.BlockSpec((D_MODEL, 2 * RNN_WIDTH), lambda b, i: (0, 0))],
        out_specs=[out_spec, out_spec],
        out_shape=[out_shape, out_shape],
        compiler_params=_params(2),
        name="odd_in_proj",
    )(x, shift, scale, w_in.astype(BF16))


def _rglru_kernel(x_ref, g_ref, convw_ref, convb_ref, wcat_ref, bcat_ref, lam_ref, o_ref,
                  xpad, a_f, b_f, a_b, b_b, edge):
    n_rows = x_ref.shape[0]
    seg_len = n_rows // N_SEG
    pad_rows = CONV_LEFT * N_SEG
    seg = lax.broadcasted_iota(jnp.int32, (N_SEG, LANES), 0)

    def from_prev_segment(blk):
        return jnp.where(seg == 0, 0.0, pltpu.roll(blk, 1, 0))

    def from_next_segment(blk):
        return jnp.where(seg == N_SEG - 1, 0.0, pltpu.roll(blk, N_SEG - 1, 0))

    for c in range(n_rows // SCAN_CHUNK):
        xpad[pl.ds(pad_rows + c * SCAN_CHUNK, SCAN_CHUNK), :] = (
            x_ref[pl.ds(c * SCAN_CHUNK, SCAN_CHUNK), :].astype(F32))
    for t in range(CONV_LEFT):
        src = (seg_len - CONV_LEFT + t) * N_SEG
        xpad[pl.ds(t * N_SEG, N_SEG), :] = from_prev_segment(x_ref[pl.ds(src, N_SEG), :].astype(F32))
    for t in range(CONV_WIDTH - 1 - CONV_LEFT):
        xpad[pl.ds(pad_rows + n_rows + t * N_SEG, N_SEG), :] = from_next_segment(
            x_ref[pl.ds(t * N_SEG, N_SEG), :].astype(F32))

    lam = lam_ref[...]
    softplus_neg = jnp.maximum(-lam, 0.0) + jnp.log1p(jnp.exp(-jnp.abs(lam)))
    decay = -RG_LRU_C * softplus_neg

    def gates_chunk(c, carry):
        base = pl.multiple_of(c * SCAN_CHUNK, SCAN_CHUNK)
        xr = convb_ref[...] + sum(convw_ref[k:k + 1, :] * xpad[pl.ds(base + k * N_SEG, SCAN_CHUNK), :]
                                  for k in range(CONV_WIDTH))
        pre = jnp.dot(xr.astype(BF16), wcat_ref[...], preferred_element_type=F32) + bcat_ref[...]
        for d, (a_ref, b_ref) in enumerate(((a_f, b_f), (a_b, b_b))):
            o = 2 * d * LANES
            rec = _sigmoid(pre[:, o:o + LANES])
            inp = _sigmoid(pre[:, o + LANES:o + 2 * LANES])
            log_a = rec * decay[d:d + 1, :]
            a = jnp.exp(log_a)
            one_minus_a2 = -jnp.tanh(log_a) * (a * a + 1.0)
            a_ref[pl.ds(base, SCAN_CHUNK), :] = a
            b_ref[pl.ds(base, SCAN_CHUNK), :] = jnp.sqrt(one_minus_a2) * inp * xr
        return carry

    lax.fori_loop(0, n_rows // SCAN_CHUNK, gates_chunk, 0)

    def scan_step(t, carry):
        h_f, p_f, h_b, p_b = carry
        rf = pl.ds(pl.multiple_of(t * N_SEG, N_SEG), N_SEG)
        rb = pl.ds(pl.multiple_of((seg_len - 1 - t) * N_SEG, N_SEG), N_SEG)
        a = a_f[rf, :]
        h_f = a * h_f + b_f[rf, :]
        p_f = a * p_f
        b_f[rf, :] = h_f
        a_f[rf, :] = p_f
        a = a_b[rb, :]
        h_b = a * h_b + b_b[rb, :]
        p_b = a * p_b
        b_b[rb, :] = h_b
        a_b[rb, :] = p_b
        return h_f, p_f, h_b, p_b

    zeros = jnp.zeros((N_SEG, LANES), F32)
    ones = jnp.ones((N_SEG, LANES), F32)
    h_f, p_f, h_b, p_b = lax.fori_loop(0, seg_len, scan_step, (zeros, ones, zeros, ones), unroll=4)

    edge[0] = h_f
    edge[1] = p_f
    edge[2] = h_b
    edge[3] = p_b
    edge[4, 0:1, :] = jnp.zeros((1, LANES), F32)
    edge[5, N_SEG - 1:N_SEG, :] = jnp.zeros((1, LANES), F32)
    for s in range(1, N_SEG):
        edge[4, s:s + 1, :] = edge[0, s - 1:s, :] + edge[1, s - 1:s, :] * edge[4, s - 1:s, :]
        r = N_SEG - 1 - s
        edge[5, r:r + 1, :] = edge[2, r + 1:r + 2, :] + edge[3, r + 1:r + 2, :] * edge[5, r + 1:r + 2, :]
    blocks = SCAN_CHUNK // N_SEG
    in_f = jnp.concatenate([edge[4]] * blocks, axis=0)
    in_b = jnp.concatenate([edge[5]] * blocks, axis=0)

    def out_chunk(c, carry):
        rows = pl.ds(pl.multiple_of(c * SCAN_CHUNK, SCAN_CHUNK), SCAN_CHUNK)
        y = (b_f[rows, :] + a_f[rows, :] * in_f) + (b_b[rows, :] + a_b[rows, :] * in_b)
        g = g_ref[rows, :].astype(F32)
        o_ref[rows, :] = (y * (g * _sigmoid(g))).astype(BF16)
        return carry

    lax.fori_loop(0, n_rows // SCAN_CHUNK, out_chunk, 0)


def _rglru(xr, g, conv_w, conv_b, w_a, b_a, w_x, b_x, lam):
    batch, n_rows, _ = xr.shape
    w_cat = jnp.concatenate([w_a[0], w_x[0], w_a[1], w_x[1]], axis=-1).astype(BF16)
    per_head = lambda v: v.reshape(RNN_HEADS, 1, RNN_HEAD_DIM)
    b_cat = jnp.concatenate([per_head(b_a[0]), per_head(b_x[0]), per_head(b_a[1]), per_head(b_x[1])], axis=-1)
    col = lambda rows: pl.BlockSpec((rows, RNN_HEAD_DIM), lambda b, h: (0, h))
    seq_col = pl.BlockSpec((None, n_rows, RNN_HEAD_DIM), lambda b, h: (b, 0, h))
    scratch_rows = n_rows + (CONV_WIDTH - 1) * N_SEG
    return pl.pallas_call(
        _rglru_kernel,
        grid=(batch, RNN_HEADS),
        in_specs=[seq_col, seq_col, col(CONV_WIDTH), col(1),
                  pl.BlockSpec((None, RNN_HEAD_DIM, 4 * RNN_HEAD_DIM), lambda b, h: (h, 0, 0)),
                  pl.BlockSpec((None, 1, 4 * RNN_HEAD_DIM), lambda b, h: (h, 0, 0)),
                  col(2)],
        out_specs=seq_col,
        out_shape=jax.ShapeDtypeStruct((batch, n_rows, RNN_WIDTH), BF16),
        scratch_shapes=[pltpu.VMEM((scratch_rows, LANES), F32)]
                       + [pltpu.VMEM((n_rows, LANES), F32)] * 4
                       + [pltpu.VMEM((6, N_SEG, LANES), F32)],
        compiler_params=_params(2),
        name="rglru",
    )(xr, g, conv_w, conv_b.reshape(1, -1), w_cat, b_cat, lam)


def _odd_out_kernel(y_ref, x_ref, gate_ref, w_ref, lng_ref, lnb_ref, o_ref):
    segs = y_ref.shape[1] // RNN_WIDTH
    y = jnp.concatenate([y_ref[:, s * RNN_WIDTH:(s + 1) * RNN_WIDTH] for s in range(segs)], axis=0)
    out = jnp.dot(y, w_ref[...], preferred_element_type=F32)
    z = DEEPNORM_ALPHA * x_ref[...] + gate_ref[...] * out
    o_ref[...] = _layer_norm_rows(z, lng_ref[...], lnb_ref[...])


def _odd_out_proj(y, x, gate, w_out, ln_g, ln_b):
    batch, seq, _ = x.shape
    seg_len = seq // N_SEG
    segs_per_tile = ROW_TILE // seg_len
    const = lambda shape: pl.BlockSpec(shape, lambda b, i: (0,) * len(shape))
    return pl.pallas_call(
        _odd_out_kernel,
        grid=(batch, seq // ROW_TILE),
        in_specs=[pl.BlockSpec((None, seg_len, segs_per_tile * RNN_WIDTH), lambda b, i: (b, 0, i)),
                  pl.BlockSpec((None, ROW_TILE, D_MODEL), lambda b, i: (b, i, 0)),
                  pl.BlockSpec((None, 1, D_MODEL), lambda b, i: (b, 0, 0)),
                  const((RNN_WIDTH, D_MODEL)), const((1, D_MODEL)), const((1, D_MODEL))],
        out_specs=pl.BlockSpec((None, ROW_TILE, D_MODEL), lambda b, i: (b, i, 0)),
        out_shape=jax.ShapeDtypeStruct((batch, seq, D_MODEL), F32),
        compiler_params=_params(2),
        name="odd_out_proj",
    )(y, x, gate, w_out.astype(BF16), ln_g.reshape(1, -1), ln_b.reshape(1, -1))


def kernel(x, c, positions, ada_w, ada_b, ln_g, ln_b, ev_w_in, ev_w_out, ev_sink, ev_sg_ln_g, ev_sg_ln_b,
           ev_sg_w, ev_sg_b, od_w_in, od_conv_w, od_conv_b, od_w_a, od_b_a, od_w_x, od_b_x, od_lam, od_w_out):
    batch, seq, d_model = x.shape
    assert d_model == D_MODEL and seq % ROW_TILE == 0 and seq % N_SEG == 0
    assert ROW_TILE % (seq // N_SEG) == 0 and (seq // N_SEG) * N_SEG % SCAN_CHUNK == 0
    assert ada_w.shape[0] == DEPTH == 2

    mod = _modulation(c, ada_w, ada_b)
    shift, scale, gate = (mod[:, :, i * D_MODEL:(i + 1) * D_MODEL].reshape(DEPTH, batch, 1, D_MODEL)
                          for i in range(3))

    q, k, v, su, sv, g = _even_in_proj(x, shift[0], scale[0], positions, ev_w_in[0])
    x1 = _even_mixer(q, k, v, su, sv, g, x, gate[0], ev_sink[0], ev_sg_ln_g[0], ev_sg_ln_b[0],
                     ev_sg_w[0], ev_sg_b[0], ev_w_out[0], ln_g[0], ln_b[0])

    seg_len = seq // N_SEG
    xr, g1 = _odd_in_proj(x1, shift[1], scale[1], od_w_in[0])
    y = _rglru(xr.reshape(batch, seq, RNN_WIDTH), g1.reshape(batch, seq, RNN_WIDTH), od_conv_w[0],
               od_conv_b[0], od_w_a[0], od_b_a[0], od_w_x[0], od_b_x[0], od_lam[0])
    return _odd_out_proj(y.reshape(batch, seg_len, N_SEG * RNN_WIDTH), x1, gate[1], od_w_out[0],
                         ln_g[1], ln_b[1])
```

```python
import math

import numpy as np
import jax
import jax.numpy as jnp
from jax import lax
from jax.experimental import pallas as pl
from jax.experimental.pallas import tpu as pltpu

D_MODEL = 1024
DEPTH = 2
HEAD_DIM = 64
N_Q_HEADS = 8
N_KV_HEADS = 2
Q_PER_KV = N_Q_HEADS // N_KV_HEADS
ATTN_WIDTH = N_Q_HEADS * HEAD_DIM
KV_WIDTH = N_KV_HEADS * HEAD_DIM
ATTN_BLOCK = 128
ROPE_DIM = HEAD_DIM // 4
ROPE_HALF = ROPE_DIM // 2
ROPE_THETA = 500000.0
N_SG_GROUPS = 8
SG_GROUP_DIM = 64
SG_WIDTH = N_SG_GROUPS * SG_GROUP_DIM
SG_CHUNK = 128
EVEN_GATE_WIDTH = ATTN_WIDTH + SG_WIDTH
EVEN_IN_WIDTH = ATTN_WIDTH + 2 * KV_WIDTH + 2 * SG_WIDTH + EVEN_GATE_WIDTH
RNN_WIDTH = D_MODEL
RNN_HEADS = 8
RNN_HEAD_DIM = RNN_WIDTH // RNN_HEADS
CONV_WIDTH = 4
CONV_LEFT = 2
RG_LRU_C = 8.0
DEEPNORM_ALPHA = (2 * DEPTH) ** 0.25
LN_EPS = 1e-5
NEG_INF = -1e30
LOG2_E = math.log2(math.e)

LANES = 128
SUBLANES = 8
VMEM_LIMIT = 56 * 1024 * 1024

ROW_TILE = 512
N_SEG = 32
SEGS_PER_TILE = SUBLANES
SCAN_CHUNK = 256

BF16 = jnp.bfloat16
F32 = jnp.float32


def _sigmoid(x):
    return 0.5 * jnp.tanh(0.5 * x) + 0.5


def _layer_norm_rows(z, g, b):
    mu = jnp.mean(z, axis=-1, keepdims=True)
    d = z - mu
    var = jnp.mean(d * d, axis=-1, keepdims=True)
    return d * lax.rsqrt(var + LN_EPS) * g + b


def _params(*semantics):
    return pltpu.CompilerParams(dimension_semantics=semantics, vmem_limit_bytes=VMEM_LIMIT)


def _mod_kernel(c_ref, w_ref, b_ref, o_ref):
    c = c_ref[...]
    cond = (c * _sigmoid(c)).astype(BF16)
    o_ref[...] = jnp.dot(cond, w_ref[...].astype(BF16), preferred_element_type=F32) + b_ref[...]


def _modulation(c, ada_w, ada_b):
    batch = c.shape[0]
    rows = -(-batch // SUBLANES) * SUBLANES
    c_pad = jnp.pad(c, ((0, rows - batch), (0, 0)))
    n_col = 3 * D_MODEL // D_MODEL
    out = pl.pallas_call(
        _mod_kernel,
        grid=(DEPTH, n_col),
        in_specs=[
            pl.BlockSpec((rows, D_MODEL), lambda l, j: (0, 0)),
            pl.BlockSpec((None, D_MODEL, D_MODEL), lambda l, j: (l, 0, j)),
            pl.BlockSpec((None, 1, D_MODEL), lambda l, j: (l, 0, j)),
        ],
        out_specs=pl.BlockSpec((None, rows, D_MODEL), lambda l, j: (l, 0, j)),
        out_shape=jax.ShapeDtypeStruct((DEPTH, rows, 3 * D_MODEL), F32),
        compiler_params=_params("parallel", "parallel"),
        name="adaln_mod",
    )(c_pad, ada_w, ada_b.reshape(DEPTH, 1, 3 * D_MODEL))
    return out[:, :batch]


def _even_in_kernel(x_ref, shift_ref, scale_ref, pos_ref, freq_ref, sign_ref, w_ref,
                    q_ref, k_ref, v_ref, su_ref, sv_ref, g_ref):
    h = (x_ref[...] * (1.0 + scale_ref[...]) + shift_ref[...]).astype(BF16)
    p = jnp.dot(h, w_ref[...], preferred_element_type=F32)

    ang = pos_ref[...].astype(F32) * freq_ref[...]
    cs = jnp.cos(ang)
    sn = jnp.sin(ang) * sign_ref[...]
    lane = lax.broadcasted_iota(jnp.int32, (1, LANES), 1)
    first_half = (lane % HEAD_DIM) < ROPE_HALF

    def rotary(t):
        partner = jnp.where(first_half, pltpu.roll(t, LANES - ROPE_HALF, 1), pltpu.roll(t, ROPE_HALF, 1))
        return t * cs + partner * sn

    q_scale = HEAD_DIM ** -0.5 * LOG2_E
    for j in range(ATTN_WIDTH // LANES):
        q_ref[:, j * LANES:(j + 1) * LANES] = (rotary(p[:, j * LANES:(j + 1) * LANES]) * q_scale).astype(BF16)
    o = ATTN_WIDTH
    k_ref[...] = rotary(p[:, o:o + KV_WIDTH]).astype(BF16)
    o += KV_WIDTH
    v_ref[...] = p[:, o:o + KV_WIDTH].astype(BF16)
    o += KV_WIDTH
    su_ref[...] = p[:, o:o + SG_WIDTH].astype(BF16)
    o += SG_WIDTH
    sv_ref[...] = p[:, o:o + SG_WIDTH].astype(BF16)
    o += SG_WIDTH
    g_ref[...] = p[:, o:o + EVEN_GATE_WIDTH].astype(BF16)


def _even_in_proj(x, shift, scale, positions, w_in):
    batch, seq, _ = x.shape
    lane = np.arange(LANES)
    d = lane % HEAD_DIM
    inv_freq = jnp.power(jnp.float32(ROPE_THETA), -jnp.arange(ROPE_HALF, dtype=F32) / ROPE_HALF)
    freq_lane = jnp.where(jnp.asarray(d < ROPE_DIM), inv_freq[d % ROPE_HALF], 0.0).reshape(1, LANES)
    sign_lane = jnp.asarray(np.where(d < ROPE_HALF, -1.0, np.where(d < ROPE_DIM, 1.0, 0.0)),
                            dtype=F32).reshape(1, LANES)
    row = lambda width: pl.BlockSpec((None, ROW_TILE, width), lambda b, i: (b, i, 0))
    vec = pl.BlockSpec((None, 1, D_MODEL), lambda b, i: (b, 0, 0))
    const = lambda shape: pl.BlockSpec(shape, lambda b, i: (0,) * len(shape))
    widths = (ATTN_WIDTH, KV_WIDTH, KV_WIDTH, SG_WIDTH, SG_WIDTH, EVEN_GATE_WIDTH)
    return pl.pallas_call(
        _even_in_kernel,
        grid=(batch, seq // ROW_TILE),
        in_specs=[row(D_MODEL), vec, vec, row(1), const((1, LANES)), const((1, LANES)),
                  const((D_MODEL, EVEN_IN_WIDTH))],
        out_specs=[row(w) for w in widths],
        out_shape=[jax.ShapeDtypeStruct((batch, seq, w), BF16) for w in widths],
        compiler_params=_params("parallel", "parallel"),
        name="even_in_proj",
    )(x, shift, scale, positions.reshape(batch, seq, 1), freq_lane, sign_lane, w_in.astype(BF16))


def _even_mix_kernel(sink_ref, q_ref, k_ref, v_ref, su_ref, sv_ref, g_ref, x_ref, gate_ref,
                     avg_ref, sgg_ref, sgb_ref, sgw_ref, sgbias_ref, wout_ref, lng_ref, lnb_ref,
                     o_ref, kdup, vsplit, y_scr):
    seq = k_ref.shape[0]
    n_blocks = seq // ATTN_BLOCK
    sub_blocks = ROW_TILE // ATTN_BLOCK
    tile = pl.program_id(1)
    lane = lax.broadcasted_iota(jnp.int32, (1, LANES), 1)
    left = lane < HEAD_DIM

    @pl.when(tile == 0)
    def _():
        def build(c, carry):
            rows = pl.ds(pl.multiple_of(c * ROW_TILE, ROW_TILE), ROW_TILE)
            k = k_ref[rows, :].astype(F32)
            k_sw = pltpu.roll(k, HEAD_DIM, 1)
            kdup[rows, 0:LANES] = jnp.where(left, k, k_sw).astype(BF16)
            kdup[rows, LANES:2 * LANES] = jnp.where(left, k_sw, k).astype(BF16)
            v = v_ref[rows, :].astype(F32)
            v_sw = pltpu.roll(v, HEAD_DIM, 1)
            vsplit[rows, 0 * LANES:1 * LANES] = jnp.where(left, v, 0.0).astype(BF16)
            vsplit[rows, 1 * LANES:2 * LANES] = jnp.where(left, 0.0, v_sw).astype(BF16)
            vsplit[rows, 2 * LANES:3 * LANES] = jnp.where(left, v_sw, 0.0).astype(BF16)
            vsplit[rows, 3 * LANES:4 * LANES] = jnp.where(left, 0.0, v).astype(BF16)
            return carry
        lax.fori_loop(0, seq // ROW_TILE, build, 0)

    qi = lax.broadcasted_iota(jnp.int32, (ATTN_BLOCK, ATTN_BLOCK), 0)
    kj = lax.broadcasted_iota(jnp.int32, (ATTN_BLOCK, ATTN_BLOCK), 1)
    ones_cols = jnp.ones((3 * ATTN_BLOCK, LANES), BF16)
    avg = avg_ref[...]

    def sub_block(j, carry):
        blk = tile * sub_blocks + j
        prev = jnp.maximum(blk - 1, 0)
        nxt = jnp.minimum(blk + 1, n_blocks - 1)
        rows = pl.ds(pl.multiple_of(j * ATTN_BLOCK, ATTN_BLOCK), ATTN_BLOCK)
        bias_prev = jnp.where((kj >= qi) & (blk > 0), 0.0, NEG_INF)
        bias_next = jnp.where((kj <= qi) & (blk < n_blocks - 1), 0.0, NEG_INF)
        bias_prev = jnp.concatenate([bias_prev] * Q_PER_KV, axis=0)
        bias_next = jnp.concatenate([bias_next] * Q_PER_KV, axis=0)

        def band(ref):
            return jnp.concatenate(
                [ref[pl.ds(pl.multiple_of(b * ATTN_BLOCK, ATTN_BLOCK), ATTN_BLOCK), :] for b in (prev, blk, nxt)],
                axis=0)

        k_band = band(kdup)
        v_band = band(vsplit)
        g_blk = g_ref[rows, :].astype(F32)
        silu_g = g_blk * _sigmoid(g_blk)

        for hk in range(N_KV_HEADS):
            q_cols = [q_ref[rows, (2 * hk + c) * LANES:(2 * hk + c + 1) * LANES] for c in range(2)]
            zero = jnp.zeros_like(q_cols[0])
            lhs = jnp.concatenate([jnp.where(left, qc, zero) for qc in q_cols]
                                  + [jnp.where(left, zero, qc) for qc in q_cols], axis=0)
            heads = (4 * hk, 4 * hk + 2, 4 * hk + 1, 4 * hk + 3)
            sink_col = jnp.concatenate(
                [jnp.full((ATTN_BLOCK, 1), sink_ref[h] * LOG2_E, F32) for h in heads], axis=0)
            s = lax.dot_general(lhs, k_band[:, hk * LANES:(hk + 1) * LANES], (((1,), (1,)), ((), ())),
                                preferred_element_type=F32)
            s0 = s[:, 0:ATTN_BLOCK] + bias_prev
            s1 = s[:, ATTN_BLOCK:2 * ATTN_BLOCK]
            s2 = s[:, 2 * ATTN_BLOCK:3 * ATTN_BLOCK] + bias_next
            m = jnp.max(jnp.maximum(jnp.maximum(s0, s1), s2), axis=-1, keepdims=True)
            m = jnp.maximum(m, sink_col)
            p = jnp.concatenate([jnp.exp2(t - m).astype(BF16) for t in (s0, s1, s2)], axis=1)
            sink_term = jnp.exp2(sink_col - m)
            half = 2 * ATTN_BLOCK
            rhs_l = jnp.concatenate([v_band[:, (2 * hk) * LANES:(2 * hk + 1) * LANES], ones_cols], axis=1)
            rhs_r = jnp.concatenate([v_band[:, (2 * hk + 1) * LANES:(2 * hk + 2) * LANES], ones_cols], axis=1)
            o_l = jnp.dot(p[0:half], rhs_l, preferred_element_type=F32)
            o_r = jnp.dot(p[half:2 * half], rhs_r, preferred_element_type=F32)
            inv_l = 1.0 / (o_l[:, LANES:2 * LANES] + sink_term[0:half])
            inv_r = 1.0 / (o_r[:, LANES:2 * LANES] + sink_term[half:2 * half])
            pair = o_l[:, 0:LANES] * inv_l + o_r[:, 0:LANES] * inv_r
            for c in range(2):
                cols = slice((2 * hk + c) * LANES, (2 * hk + c + 1) * LANES)
                y_scr[rows, cols] = (pair[c * ATTN_BLOCK:(c + 1) * ATTN_BLOCK] * silu_g[:, cols]).astype(BF16)

        for c in range(SG_WIDTH // LANES):
            cols = slice(c * LANES, (c + 1) * LANES)
            v_c = sv_ref[rows, cols]
            mean = jnp.dot(v_c, avg, preferred_element_type=F32)
            dev = v_c.astype(F32) - mean
            sq = dev * dev
            sq_hi = sq.astype(BF16)
            sq_lo = (sq - sq_hi.astype(F32)).astype(BF16)
            var = (jnp.dot(sq_hi, avg, preferred_element_type=F32)
                   + jnp.dot(sq_lo, avg, preferred_element_type=F32))
            vn = dev * lax.rsqrt(var + LN_EPS) * sgg_ref[:, cols] + sgb_ref[:, cols]
            stacked = jnp.concatenate([jnp.where(left, vn, 0.0), jnp.where(left, 0.0, vn)],
                                      axis=0).astype(BF16)
            mixed = jnp.dot(sgw_ref[c], stacked, preferred_element_type=F32) + sgbias_ref[:, cols]
            y_sg = su_ref[rows, cols].astype(F32) * mixed
            gc = slice(ATTN_WIDTH + c * LANES, ATTN_WIDTH + (c + 1) * LANES)
            y_scr[rows, gc] = (y_sg * silu_g[:, gc]).astype(BF16)
        return carry

    lax.fori_loop(0, sub_blocks, sub_block, 0)

    out = jnp.dot(y_scr[...], wout_ref[...], preferred_element_type=F32)
    z = DEEPNORM_ALPHA * x_ref[...] + gate_ref[...] * out
    o_ref[...] = _layer_norm_rows(z, lng_ref[...], lnb_ref[...])


def _even_mixer(q, k, v, su, sv, g, x, gate, sink, sg_ln_g, sg_ln_b, sg_w, sg_b, w_out, ln_g, ln_b):
    batch, seq, _ = x.shape
    lane = np.arange(LANES)
    avg = jnp.asarray((lane[:, None] // SG_GROUP_DIM == lane[None, :] // SG_GROUP_DIM) / SG_GROUP_DIM, dtype=BF16)
    w_pairs = sg_w.reshape(N_SG_GROUPS // 2, 2, SG_CHUNK, SG_CHUNK).transpose(0, 2, 1, 3)
    w_pairs = w_pairs.reshape(N_SG_GROUPS // 2, SG_CHUNK, 2 * SG_CHUNK).astype(BF16)
    bias_full = jnp.repeat(sg_b.T, SG_GROUP_DIM, axis=1)
    row = lambda width: pl.BlockSpec((None, ROW_TILE, width), lambda b, i: (b, i, 0))
    full_seq = pl.BlockSpec((None, seq, KV_WIDTH), lambda b, i: (b, 0, 0))
    const = lambda shape: pl.BlockSpec(shape, lambda b, i: (0,) * len(shape))
    return pl.pallas_call(
        _even_mix_kernel,
        grid=(batch, seq // ROW_TILE),
        in_specs=[
            pl.BlockSpec(memory_space=pltpu.SMEM),
            row(ATTN_WIDTH), full_seq, full_seq, row(SG_WIDTH), row(SG_WIDTH), row(EVEN_GATE_WIDTH),
            row(D_MODEL), pl.BlockSpec((None, 1, D_MODEL), lambda b, i: (b, 0, 0)),
            const((LANES, LANES)), const((1, SG_WIDTH)), const((1, SG_WIDTH)),
            const((N_SG_GROUPS // 2, SG_CHUNK, 2 * SG_CHUNK)), const((SG_CHUNK, SG_WIDTH)),
            const((EVEN_GATE_WIDTH, D_MODEL)), const((1, D_MODEL)), const((1, D_MODEL)),
        ],
        out_specs=row(D_MODEL),
        out_shape=jax.ShapeDtypeStruct((batch, seq, D_MODEL), F32),
        scratch_shapes=[pltpu.VMEM((seq, 2 * KV_WIDTH), BF16), pltpu.VMEM((seq, 4 * KV_WIDTH), BF16),
                        pltpu.VMEM((ROW_TILE, EVEN_GATE_WIDTH), BF16)],
        compiler_params=_params("arbitrary", "arbitrary"),
        name="even_mixer",
    )(sink, q, k, v, su, sv, g, x, gate, avg, sg_ln_g.reshape(1, -1), sg_ln_b.reshape(1, -1),
      w_pairs, bias_full, w_out.astype(BF16), ln_g.reshape(1, -1), ln_b.reshape(1, -1))


def _odd_in_kernel(x_ref, shift_ref, scale_ref, w_ref, xr_ref, g_ref):
    seg_len = xr_ref.shape[1]
    h = (x_ref[...] * (1.0 + scale_ref[...]) + shift_ref[...]).astype(BF16)
    p = jnp.dot(h, w_ref[...], preferred_element_type=F32)
    head_rows = seg_len * SEGS_PER_TILE
    for half, ref in enumerate((xr_ref, g_ref)):
        o_rows = ref.reshape(RNN_HEADS * head_rows, LANES)
        for s in range(SEGS_PER_TILE):
            for hd in range(RNN_HEADS):
                col = half * RNN_WIDTH + hd * LANES
                o_rows[pl.ds(hd * head_rows + s, seg_len, stride=SEGS_PER_TILE), :] = (
                    p[s * seg_len:(s + 1) * seg_len, col:col + LANES])


def _odd_in_proj(x, shift, scale, w_in):
    batch, seq, _ = x.shape
    seg_len = seq // N_SEG
    tile_rows = SEGS_PER_TILE * seg_len
    vec = pl.BlockSpec((None, 1, D_MODEL), lambda b, i: (b, 0, 0))
    out_spec = pl.BlockSpec((None, RNN_HEADS, seg_len, SEGS_PER_TILE, RNN_HEAD_DIM), lambda b, i: (b, 0, 0, i, 0))
    out_shape = jax.ShapeDtypeStruct((batch, RNN_HEADS, seg_len, N_SEG, RNN_HEAD_DIM), F32)
    return pl.pallas_call(
        _odd_in_kernel,
        grid=(batch, seq // tile_rows),
        in_specs=[pl.BlockSpec((None, tile_rows, D_MODEL), lambda b, i: (b, i, 0)), vec, vec,
                  pl.BlockSpec((D_MODEL, 2 * RNN_WIDTH), lambda b, i: (0, 0))],
        out_specs=[out_spec, out_spec],
        out_shape=[out_shape, out_shape],
        compiler_params=_params("parallel", "parallel"),
        name="odd_in_proj",
    )(x, shift, scale, w_in.astype(BF16))


def _rglru_kernel(x_ref, g_ref, convw_ref, convb_ref, wcat_ref, bcat_ref, lam_ref, o_ref,
                  xpad, a_f, b_f, a_b, b_b, edge):
    n_rows = x_ref.shape[0]
    seg_len = n_rows // N_SEG
    pad_rows = CONV_LEFT * N_SEG
    seg = lax.broadcasted_iota(jnp.int32, (N_SEG, LANES), 0)

    def x_rows(start, size):
        return x_ref[pl.ds(start, size), :]

    def from_prev_segment(blk):
        return jnp.where(seg == 0, 0.0, pltpu.roll(blk, 1, 0))

    def from_next_segment(blk):
        return jnp.where(seg == N_SEG - 1, 0.0, pltpu.roll(blk, N_SEG - 1, 0))

    for c in range(n_rows // SCAN_CHUNK):
        xpad[pl.ds(pad_rows + c * SCAN_CHUNK, SCAN_CHUNK), :] = x_rows(c * SCAN_CHUNK, SCAN_CHUNK)
    for t in range(CONV_LEFT):
        xpad[pl.ds(t * N_SEG, N_SEG), :] = from_prev_segment(x_rows((seg_len - CONV_LEFT + t) * N_SEG, N_SEG))
    for t in range(CONV_WIDTH - 1 - CONV_LEFT):
        xpad[pl.ds(pad_rows + n_rows + t * N_SEG, N_SEG), :] = from_next_segment(x_rows(t * N_SEG, N_SEG))

    lam = lam_ref[...]
    softplus_neg = jnp.maximum(-lam, 0.0) + jnp.log1p(jnp.exp(-jnp.abs(lam)))
    half_decay = (-0.5 * RG_LRU_C) * softplus_neg

    def gates_chunk(c, carry):
        base = pl.multiple_of(c * SCAN_CHUNK, SCAN_CHUNK)
        xr = convb_ref[...] + sum(convw_ref[k:k + 1, :] * xpad[pl.ds(base + k * N_SEG, SCAN_CHUNK), :]
                                  for k in range(CONV_WIDTH))
        th = jnp.tanh(jnp.dot(xr.astype(BF16), wcat_ref[...], preferred_element_type=F32) + bcat_ref[...])
        half_x = 0.5 * xr
        for d, (a_ref, b_ref) in enumerate(((a_f, b_f), (a_b, b_b))):
            o = 2 * d * LANES
            log_a = (th[:, o:o + LANES] + 1.0) * half_decay[d:d + 1, :]
            a = jnp.exp(log_a)
            one_minus_a2 = -jnp.tanh(log_a) * (a * a + 1.0)
            a_ref[pl.ds(base, SCAN_CHUNK), :] = a
            b_ref[pl.ds(base, SCAN_CHUNK), :] = (
                jnp.sqrt(one_minus_a2) * ((th[:, o + LANES:o + 2 * LANES] + 1.0) * half_x))
        return carry

    lax.fori_loop(0, n_rows // SCAN_CHUNK, gates_chunk, 0)

    def scan_step(t, carry):
        h_f, p_f, h_b, p_b = carry
        rf = pl.ds(pl.multiple_of(t * N_SEG, N_SEG), N_SEG)
        rb = pl.ds(pl.multiple_of((seg_len - 1 - t) * N_SEG, N_SEG), N_SEG)
        a = a_f[rf, :]
        h_f = a * h_f + b_f[rf, :]
        p_f = a * p_f
        b_f[rf, :] = h_f
        a_f[rf, :] = p_f
        a = a_b[rb, :]
        h_b = a * h_b + b_b[rb, :]
        p_b = a * p_b
        b_b[rb, :] = h_b
        a_b[rb, :] = p_b
        return h_f, p_f, h_b, p_b

    zeros = jnp.zeros((N_SEG, LANES), F32)
    ones = jnp.ones((N_SEG, LANES), F32)
    h_f, p_f, h_b, p_b = lax.fori_loop(0, seg_len, scan_step, (zeros, ones, zeros, ones), unroll=4)

    edge[0] = h_f
    edge[1] = p_f
    edge[2] = h_b
    edge[3] = p_b
    edge[4, 0:1, :] = jnp.zeros((1, LANES), F32)
    edge[5, N_SEG - 1:N_SEG, :] = jnp.zeros((1, LANES), F32)
    for s in range(1, N_SEG):
        edge[4, s:s + 1, :] = edge[0, s - 1:s, :] + edge[1, s - 1:s, :] * edge[4, s - 1:s, :]
        r = N_SEG - 1 - s
        edge[5, r:r + 1, :] = edge[2, r + 1:r + 2, :] + edge[3, r + 1:r + 2, :] * edge[5, r + 1:r + 2, :]
    blocks = SCAN_CHUNK // N_SEG
    in_f = jnp.concatenate([edge[4]] * blocks, axis=0)
    in_b = jnp.concatenate([edge[5]] * blocks, axis=0)

    def out_chunk(c, carry):
        rows = pl.ds(pl.multiple_of(c * SCAN_CHUNK, SCAN_CHUNK), SCAN_CHUNK)
        y = (b_f[rows, :] + a_f[rows, :] * in_f) + (b_b[rows, :] + a_b[rows, :] * in_b)
        g = g_ref[rows, :]
        o_ref[rows, :] = y * (g * _sigmoid(g))
        return carry

    lax.fori_loop(0, n_rows // SCAN_CHUNK, out_chunk, 0)


def _rglru(xr, g, conv_w, conv_b, w_a, b_a, w_x, b_x, lam):
    batch, _, n_rows, _ = xr.shape
    w_cat = (0.5 * jnp.concatenate([w_a[0], w_x[0], w_a[1], w_x[1]], axis=-1)).astype(BF16)
    per_head = lambda v: v.reshape(RNN_HEADS, 1, RNN_HEAD_DIM)
    b_cat = 0.5 * jnp.concatenate([per_head(b_a[0]), per_head(b_x[0]), per_head(b_a[1]), per_head(b_x[1])],
                                  axis=-1)
    col = lambda rows: pl.BlockSpec((rows, RNN_HEAD_DIM), lambda b, h: (0, h))
    seq_col = pl.BlockSpec((None, None, n_rows, RNN_HEAD_DIM), lambda b, h: (b, h, 0, 0))
    scratch_rows = n_rows + (CONV_WIDTH - 1) * N_SEG
    return pl.pallas_call(
        _rglru_kernel,
        grid=(batch, RNN_HEADS),
        in_specs=[seq_col, seq_col, col(CONV_WIDTH), col(1),
                  pl.BlockSpec((None, RNN_HEAD_DIM, 4 * RNN_HEAD_DIM), lambda b, h: (h, 0, 0)),
                  pl.BlockSpec((None, 1, 4 * RNN_HEAD_DIM), lambda b, h: (h, 0, 0)),
                  col(2)],
        out_specs=seq_col,
        out_shape=jax.ShapeDtypeStruct((batch, RNN_HEADS, n_rows, RNN_HEAD_DIM), F32),
        scratch_shapes=[pltpu.VMEM((scratch_rows, LANES), F32)]
                       + [pltpu.VMEM((n_rows, LANES), F32)] * 4
                       + [pltpu.VMEM((6, N_SEG, LANES), F32)],
        compiler_params=_params("parallel", "parallel"),
        name="rglru",
    )(xr, g, conv_w, conv_b.reshape(1, -1), w_cat, b_cat, lam)


def _odd_out_kernel(y_ref, x_ref, gate_ref, w_ref, lng_ref, lnb_ref, o_ref):
    seg_len = y_ref.shape[1]
    head_rows = seg_len * SEGS_PER_TILE
    y_rows = y_ref.reshape(RNN_HEADS * head_rows, LANES)
    y = jnp.concatenate(
        [jnp.concatenate([y_rows[pl.ds(hd * head_rows + s, seg_len, stride=SEGS_PER_TILE), :].astype(BF16)
                          for hd in range(RNN_HEADS)], axis=1)
         for s in range(SEGS_PER_TILE)], axis=0)
    out = jnp.dot(y, w_ref[...], preferred_element_type=F32)
    z = DEEPNORM_ALPHA * x_ref[...] + gate_ref[...] * out
    o_ref[...] = _layer_norm_rows(z, lng_ref[...], lnb_ref[...])


def _odd_out_proj(y, x, gate, w_out, ln_g, ln_b):
    batch, seq, _ = x.shape
    seg_len = seq // N_SEG
    tile_rows = SEGS_PER_TILE * seg_len
    const = lambda shape: pl.BlockSpec(shape, lambda b, i: (0,) * len(shape))
    return pl.pallas_call(
        _odd_out_kernel,
        grid=(batch, seq // tile_rows),
        in_specs=[pl.BlockSpec((None, RNN_HEADS, seg_len, SEGS_PER_TILE, RNN_HEAD_DIM),
                               lambda b, i: (b, 0, 0, i, 0)),
                  pl.BlockSpec((None, tile_rows, D_MODEL), lambda b, i: (b, i, 0)),
                  pl.BlockSpec((None, 1, D_MODEL), lambda b, i: (b, 0, 0)),
                  const((RNN_WIDTH, D_MODEL)), const((1, D_MODEL)), const((1, D_MODEL))],
        out_specs=pl.BlockSpec((None, tile_rows, D_MODEL), lambda b, i: (b, i, 0)),
        out_shape=jax.ShapeDtypeStruct((batch, seq, D_MODEL), F32),
        compiler_params=_params("parallel", "parallel"),
        name="odd_out_proj",
    )(y, x, gate, w_out.astype(BF16), ln_g.reshape(1, -1), ln_b.reshape(1, -1))


def kernel(x, c, positions, ada_w, ada_b, ln_g, ln_b, ev_w_in, ev_w_out, ev_sink, ev_sg_ln_g, ev_sg_ln_b,
           ev_sg_w, ev_sg_b, od_w_in, od_conv_w, od_conv_b, od_w_a, od_b_a, od_w_x, od_b_x, od_lam, od_w_out):
    batch, seq, d_model = x.shape
    seg_len = seq // N_SEG
    assert d_model == D_MODEL and seq % ROW_TILE == 0 and seq == N_SEG * seg_len
    assert N_SEG % SEGS_PER_TILE == 0 and seq % SCAN_CHUNK == 0 and SCAN_CHUNK % N_SEG == 0
    assert ada_w.shape[0] == DEPTH == 2

    mod = _modulation(c, ada_w, ada_b)
    shift, scale, gate = (mod[:, :, i * D_MODEL:(i + 1) * D_MODEL].reshape(DEPTH, batch, 1, D_MODEL)
                          for i in range(3))

    q, k, v, su, sv, g = _even_in_proj(x, shift[0], scale[0], positions, ev_w_in[0])
    x1 = _even_mixer(q, k, v, su, sv, g, x, gate[0], ev_sink[0], ev_sg_ln_g[0], ev_sg_ln_b[0],
                     ev_sg_w[0], ev_sg_b[0], ev_w_out[0], ln_g[0], ln_b[0])

    xr, g1 = _odd_in_proj(x1, shift[1], scale[1], od_w_in[0])
    scan_rows = (batch, RNN_HEADS, seq, RNN_HEAD_DIM)
    y = _rglru(xr.reshape(scan_rows), g1.reshape(scan_rows), od_conv_w[0], od_conv_b[0], od_w_a[0],
               od_b_a[0], od_w_x[0], od_b_x[0], od_lam[0])
    return _odd_out_proj(y.reshape(batch, RNN_HEADS, seg_len, N_SEG, RNN_HEAD_DIM), x1, gate[1], od_w_out[0],
                         ln_g[1], ln_b[1])
```

```python
import math

import numpy as np
import jax
import jax.numpy as jnp
from jax import lax
from jax.experimental import pallas as pl
from jax.experimental.pallas import tpu as pltpu

D_MODEL = 1024
DEPTH = 2
HEAD_DIM = 64
N_Q_HEADS = 8
N_KV_HEADS = 2
Q_PER_KV = N_Q_HEADS // N_KV_HEADS
ATTN_WIDTH = N_Q_HEADS * HEAD_DIM
KV_WIDTH = N_KV_HEADS * HEAD_DIM
ATTN_BLOCK = 128
ROPE_DIM = HEAD_DIM // 4
ROPE_HALF = ROPE_DIM // 2
ROPE_THETA = 500000.0
N_SG_GROUPS = 8
SG_GROUP_DIM = 64
SG_WIDTH = N_SG_GROUPS * SG_GROUP_DIM
SG_CHUNK = 128
EVEN_GATE_WIDTH = ATTN_WIDTH + SG_WIDTH
EVEN_IN_WIDTH = ATTN_WIDTH + 2 * KV_WIDTH + 2 * SG_WIDTH + EVEN_GATE_WIDTH
RNN_WIDTH = D_MODEL
RNN_HEADS = 8
RNN_HEAD_DIM = RNN_WIDTH // RNN_HEADS
CONV_WIDTH = 4
CONV_LEFT = 2
RG_LRU_C = 8.0
DEEPNORM_ALPHA = (2 * DEPTH) ** 0.25
LN_EPS = 1e-5
NEG_INF = -1e30
LOG2_E = math.log2(math.e)
F32_TINY = float(np.finfo(np.float32).tiny)

LANES = 128
SUBLANES = 8
VMEM_LIMIT = 56 * 1024 * 1024

ROW_TILE = 512
N_SEG = 32
SEGS_PER_TILE = SUBLANES
SCAN_CHUNK = 256

BF16 = jnp.bfloat16
F32 = jnp.float32


def _sigmoid(x):
    return 0.5 * jnp.tanh(0.5 * x) + 0.5


def _layer_norm_rows(z, g, b):
    mu = jnp.mean(z, axis=-1, keepdims=True)
    d = z - mu
    var = jnp.mean(d * d, axis=-1, keepdims=True)
    return d * lax.rsqrt(var + LN_EPS) * g + b


def _params(*semantics):
    return pltpu.CompilerParams(dimension_semantics=semantics, vmem_limit_bytes=VMEM_LIMIT)


def _mod_kernel(c_ref, w_ref, b_ref, o_ref):
    c = c_ref[...]
    cond = (c * _sigmoid(c)).astype(BF16)
    o_ref[...] = jnp.dot(cond, w_ref[...].astype(BF16), preferred_element_type=F32) + b_ref[...]


def _modulation(c, ada_w, ada_b):
    batch = c.shape[0]
    rows = -(-batch // SUBLANES) * SUBLANES
    c_pad = jnp.pad(c, ((0, rows - batch), (0, 0)))
    n_col = 3 * D_MODEL // D_MODEL
    out = pl.pallas_call(
        _mod_kernel,
        grid=(DEPTH, n_col),
        in_specs=[
            pl.BlockSpec((rows, D_MODEL), lambda l, j: (0, 0)),
            pl.BlockSpec((None, D_MODEL, D_MODEL), lambda l, j: (l, 0, j)),
            pl.BlockSpec((None, 1, D_MODEL), lambda l, j: (l, 0, j)),
        ],
        out_specs=pl.BlockSpec((None, rows, D_MODEL), lambda l, j: (l, 0, j)),
        out_shape=jax.ShapeDtypeStruct((DEPTH, rows, 3 * D_MODEL), F32),
        compiler_params=_params("parallel", "parallel"),
        name="adaln_mod",
    )(c_pad, ada_w, ada_b.reshape(DEPTH, 1, 3 * D_MODEL))
    return out[:, :batch]


def _even_in_kernel(x_ref, shift_ref, scale_ref, pos_ref, freq_ref, sign_ref, w_ref,
                    q_ref, k_ref, v_ref, su_ref, sv_ref, g_ref):
    h = (x_ref[...] * (1.0 + scale_ref[...]) + shift_ref[...]).astype(BF16)
    p = jnp.dot(h, w_ref[...], preferred_element_type=F32)

    ang = pos_ref[...].astype(F32) * freq_ref[...]
    cs = jnp.cos(ang)
    sn = jnp.sin(ang) * sign_ref[...]
    lane = lax.broadcasted_iota(jnp.int32, (1, LANES), 1)
    first_half = (lane % HEAD_DIM) < ROPE_HALF

    def rotary(t):
        partner = jnp.where(first_half, pltpu.roll(t, LANES - ROPE_HALF, 1), pltpu.roll(t, ROPE_HALF, 1))
        return t * cs + partner * sn

    q_scale = HEAD_DIM ** -0.5 * LOG2_E
    for j in range(ATTN_WIDTH // LANES):
        q_ref[:, j * LANES:(j + 1) * LANES] = (rotary(p[:, j * LANES:(j + 1) * LANES]) * q_scale).astype(BF16)
    o = ATTN_WIDTH
    k_ref[...] = rotary(p[:, o:o + KV_WIDTH]).astype(BF16)
    o += KV_WIDTH
    v_ref[...] = p[:, o:o + KV_WIDTH].astype(BF16)
    o += KV_WIDTH
    su_ref[...] = p[:, o:o + SG_WIDTH].astype(BF16)
    o += SG_WIDTH
    sv_ref[...] = p[:, o:o + SG_WIDTH].astype(BF16)
    o += SG_WIDTH
    g_ref[...] = p[:, o:o + EVEN_GATE_WIDTH].astype(BF16)


def _even_in_proj(x, shift, scale, positions, w_in):
    batch, seq, _ = x.shape
    lane = np.arange(LANES)
    d = lane % HEAD_DIM
    inv_freq = jnp.power(jnp.float32(ROPE_THETA), -jnp.arange(ROPE_HALF, dtype=F32) / ROPE_HALF)
    freq_lane = jnp.where(jnp.asarray(d < ROPE_DIM), inv_freq[d % ROPE_HALF], 0.0).reshape(1, LANES)
    sign_lane = jnp.asarray(np.where(d < ROPE_HALF, -1.0, np.where(d < ROPE_DIM, 1.0, 0.0)),
                            dtype=F32).reshape(1, LANES)
    row = lambda width: pl.BlockSpec((None, ROW_TILE, width), lambda b, i: (b, i, 0))
    vec = pl.BlockSpec((None, 1, D_MODEL), lambda b, i: (b, 0, 0))
    const = lambda shape: pl.BlockSpec(shape, lambda b, i: (0,) * len(shape))
    widths = (ATTN_WIDTH, KV_WIDTH, KV_WIDTH, SG_WIDTH, SG_WIDTH, EVEN_GATE_WIDTH)
    return pl.pallas_call(
        _even_in_kernel,
        grid=(batch, seq // ROW_TILE),
        in_specs=[row(D_MODEL), vec, vec, row(1), const((1, LANES)), const((1, LANES)),
                  const((D_MODEL, EVEN_IN_WIDTH))],
        out_specs=[row(w) for w in widths],
        out_shape=[jax.ShapeDtypeStruct((batch, seq, w), BF16) for w in widths],
        compiler_params=_params("parallel", "parallel"),
        name="even_in_proj",
    )(x, shift, scale, positions.reshape(batch, seq, 1), freq_lane, sign_lane, w_in.astype(BF16))


def _even_mix_kernel(sink_ref, q_ref, k_ref, v_ref, su_ref, sv_ref, g_ref, x_ref, gate_ref,
                     avg_ref, avg2_ref, sgg_ref, sgb_ref, sgw_ref, sgbias_ref, wout_ref, lng_ref, lnb_ref,
                     o_ref, kdup, vsplit, y_scr):
    seq = k_ref.shape[0]
    n_blocks = seq // ATTN_BLOCK
    sub_blocks = ROW_TILE // ATTN_BLOCK
    tile = pl.program_id(1)
    lane = lax.broadcasted_iota(jnp.int32, (1, LANES), 1)
    left = lane < HEAD_DIM

    @pl.when(tile == 0)
    def _():
        def build(c, carry):
            rows = pl.ds(pl.multiple_of(c * ROW_TILE, ROW_TILE), ROW_TILE)
            k = k_ref[rows, :].astype(F32)
            k_sw = pltpu.roll(k, HEAD_DIM, 1)
            kdup[rows, 0:LANES] = jnp.where(left, k, k_sw).astype(BF16)
            kdup[rows, LANES:2 * LANES] = jnp.where(left, k_sw, k).astype(BF16)
            v = v_ref[rows, :].astype(F32)
            v_sw = pltpu.roll(v, HEAD_DIM, 1)
            vsplit[rows, 0 * LANES:1 * LANES] = jnp.where(left, v, 0.0).astype(BF16)
            vsplit[rows, 1 * LANES:2 * LANES] = jnp.where(left, 0.0, v_sw).astype(BF16)
            vsplit[rows, 2 * LANES:3 * LANES] = jnp.where(left, v_sw, 0.0).astype(BF16)
            vsplit[rows, 3 * LANES:4 * LANES] = jnp.where(left, 0.0, v).astype(BF16)
            return carry
        lax.fori_loop(0, seq // ROW_TILE, build, 0)

    qi = lax.broadcasted_iota(jnp.int32, (ATTN_BLOCK, ATTN_BLOCK), 0)
    kj = lax.broadcasted_iota(jnp.int32, (ATTN_BLOCK, ATTN_BLOCK), 1)
    ones_cols = jnp.ones((3 * ATTN_BLOCK, LANES), BF16)
    avg = avg_ref[...]
    sink_cols = [jnp.concatenate([jnp.full((ATTN_BLOCK, 1), sink_ref[h] * LOG2_E, F32)
                                  for h in (4 * hk, 4 * hk + 2, 4 * hk + 1, 4 * hk + 3)], axis=0)
                 for hk in range(N_KV_HEADS)]

    def sub_block(j, carry):
        blk = tile * sub_blocks + j
        prev = jnp.maximum(blk - 1, 0)
        nxt = jnp.minimum(blk + 1, n_blocks - 1)
        rows = pl.ds(pl.multiple_of(j * ATTN_BLOCK, ATTN_BLOCK), ATTN_BLOCK)
        bias_prev = jnp.where((kj >= qi) & (blk > 0), 0.0, NEG_INF)
        bias_next = jnp.where((kj <= qi) & (blk < n_blocks - 1), 0.0, NEG_INF)
        bias_prev = jnp.concatenate([bias_prev] * Q_PER_KV, axis=0)
        bias_next = jnp.concatenate([bias_next] * Q_PER_KV, axis=0)

        def band(ref):
            return jnp.concatenate(
                [ref[pl.ds(pl.multiple_of(b * ATTN_BLOCK, ATTN_BLOCK), ATTN_BLOCK), :] for b in (prev, blk, nxt)],
                axis=0)

        k_band = band(kdup)
        v_band = band(vsplit)
        n_sg = SG_WIDTH // LANES
        sg_cols = [slice(c * LANES, (c + 1) * LANES) for c in range(n_sg)]

        def silu_gate(cols):
            g_c = g_ref[rows, cols].astype(F32)
            return g_c * _sigmoid(g_c)

        scores = []
        for hk in range(N_KV_HEADS):
            q_cols = [q_ref[rows, (2 * hk + c) * LANES:(2 * hk + c + 1) * LANES] for c in range(2)]
            zero = jnp.zeros_like(q_cols[0])
            lhs = jnp.concatenate([jnp.where(left, qc, zero) for qc in q_cols]
                                  + [jnp.where(left, zero, qc) for qc in q_cols], axis=0)
            scores.append(lax.dot_general(lhs, k_band[:, hk * LANES:(hk + 1) * LANES], (((1,), (1,)), ((), ())),
                                          preferred_element_type=F32))
        sv_cols = [sv_ref[rows, cols] for cols in sg_cols]
        means = [jnp.dot(v_c, avg, preferred_element_type=F32) for v_c in sv_cols]

        probs, sink_terms = [], []
        for hk in range(N_KV_HEADS):
            s = scores[hk]
            s0 = s[:, 0:ATTN_BLOCK] + bias_prev
            s1 = s[:, ATTN_BLOCK:2 * ATTN_BLOCK]
            s2 = s[:, 2 * ATTN_BLOCK:3 * ATTN_BLOCK] + bias_next
            m = jnp.max(jnp.maximum(jnp.maximum(s0, s1), s2), axis=-1, keepdims=True)
            m = jnp.maximum(m, sink_cols[hk])
            probs.append(jnp.concatenate([jnp.exp2(t - m).astype(BF16) for t in (s0, s1, s2)], axis=1))
            sink_terms.append(jnp.exp2(sink_cols[hk] - m))
        devs, sq_parts = [], []
        for v_c, mean in zip(sv_cols, means):
            dev = v_c.astype(F32) - mean
            sq = dev * dev
            sq_hi = sq.astype(BF16)
            sq_lo = (sq - sq_hi.astype(F32)).astype(BF16)
            devs.append(dev)
            sq_parts.append(jnp.concatenate([sq_hi, sq_lo], axis=1))

        half = 2 * ATTN_BLOCK
        pv = []
        for hk in range(N_KV_HEADS):
            rhs_l = jnp.concatenate([v_band[:, (2 * hk) * LANES:(2 * hk + 1) * LANES], ones_cols], axis=1)
            rhs_r = jnp.concatenate([v_band[:, (2 * hk + 1) * LANES:(2 * hk + 2) * LANES], ones_cols], axis=1)
            pv.append((jnp.dot(probs[hk][0:half], rhs_l, preferred_element_type=F32),
                       jnp.dot(probs[hk][half:2 * half], rhs_r, preferred_element_type=F32)))
        variances = [jnp.dot(sq, avg2_ref[...], preferred_element_type=F32) for sq in sq_parts]

        for hk in range(N_KV_HEADS):
            o_l, o_r = pv[hk]
            inv_l = 1.0 / (o_l[:, LANES:2 * LANES] + sink_terms[hk][0:half])
            inv_r = 1.0 / (o_r[:, LANES:2 * LANES] + sink_terms[hk][half:2 * half])
            pair = o_l[:, 0:LANES] * inv_l + o_r[:, 0:LANES] * inv_r
            for c in range(2):
                cols = slice((2 * hk + c) * LANES, (2 * hk + c + 1) * LANES)
                y_scr[rows, cols] = (pair[c * ATTN_BLOCK:(c + 1) * ATTN_BLOCK] * silu_gate(cols)).astype(BF16)

        for c, cols in enumerate(sg_cols):
            vn = devs[c] * lax.rsqrt(variances[c] + LN_EPS) * sgg_ref[:, cols] + sgb_ref[:, cols]
            stacked = jnp.concatenate([jnp.where(left, vn, 0.0), jnp.where(left, 0.0, vn)],
                                      axis=0).astype(BF16)
            mixed = jnp.dot(sgw_ref[c], stacked, preferred_element_type=F32) + sgbias_ref[:, cols]
            y_sg = su_ref[rows, cols].astype(F32) * mixed
            gc = slice(ATTN_WIDTH + c * LANES, ATTN_WIDTH + (c + 1) * LANES)
            y_scr[rows, gc] = (y_sg * silu_gate(gc)).astype(BF16)
        return carry

    lax.fori_loop(0, sub_blocks, sub_block, 0)

    out = jnp.dot(y_scr[...], wout_ref[...], preferred_element_type=F32)
    z = DEEPNORM_ALPHA * x_ref[...] + gate_ref[...] * out
    o_ref[...] = _layer_norm_rows(z, lng_ref[...], lnb_ref[...])


def _even_mixer(q, k, v, su, sv, g, x, gate, sink, sg_ln_g, sg_ln_b, sg_w, sg_b, w_out, ln_g, ln_b):
    batch, seq, _ = x.shape
    lane = np.arange(LANES)
    avg = jnp.asarray((lane[:, None] // SG_GROUP_DIM == lane[None, :] // SG_GROUP_DIM) / SG_GROUP_DIM, dtype=BF16)
    w_pairs = sg_w.reshape(N_SG_GROUPS // 2, 2, SG_CHUNK, SG_CHUNK).transpose(0, 2, 1, 3)
    w_pairs = w_pairs.reshape(N_SG_GROUPS // 2, SG_CHUNK, 2 * SG_CHUNK).astype(BF16)
    bias_full = jnp.repeat(sg_b.T, SG_GROUP_DIM, axis=1)
    row = lambda width: pl.BlockSpec((None, ROW_TILE, width), lambda b, i: (b, i, 0))
    full_seq = pl.BlockSpec((None, seq, KV_WIDTH), lambda b, i: (b, 0, 0))
    const = lambda shape: pl.BlockSpec(shape, lambda b, i: (0,) * len(shape))
    return pl.pallas_call(
        _even_mix_kernel,
        grid=(batch, seq // ROW_TILE),
        in_specs=[
            pl.BlockSpec(memory_space=pltpu.SMEM),
            row(ATTN_WIDTH), full_seq, full_seq, row(SG_WIDTH), row(SG_WIDTH), row(EVEN_GATE_WIDTH),
            row(D_MODEL), pl.BlockSpec((None, 1, D_MODEL), lambda b, i: (b, 0, 0)),
            const((LANES, LANES)), const((2 * LANES, LANES)), const((1, SG_WIDTH)), const((1, SG_WIDTH)),
            const((N_SG_GROUPS // 2, SG_CHUNK, 2 * SG_CHUNK)), const((SG_CHUNK, SG_WIDTH)),
            const((EVEN_GATE_WIDTH, D_MODEL)), const((1, D_MODEL)), const((1, D_MODEL)),
        ],
        out_specs=row(D_MODEL),
        out_shape=jax.ShapeDtypeStruct((batch, seq, D_MODEL), F32),
        scratch_shapes=[pltpu.VMEM((seq, 2 * KV_WIDTH), BF16), pltpu.VMEM((seq, 4 * KV_WIDTH), BF16),
                        pltpu.VMEM((ROW_TILE, EVEN_GATE_WIDTH), BF16)],
        compiler_params=_params("arbitrary", "arbitrary"),
        name="even_mixer",
    )(sink, q, k, v, su, sv, g, x, gate, avg, jnp.concatenate([avg, avg], axis=0), sg_ln_g.reshape(1, -1), sg_ln_b.reshape(1, -1),
      w_pairs, bias_full, w_out.astype(BF16), ln_g.reshape(1, -1), ln_b.reshape(1, -1))


def _odd_in_kernel(x_ref, shift_ref, scale_ref, w_ref, xr_ref, g_ref):
    seg_len = xr_ref.shape[1]
    h = (x_ref[...] * (1.0 + scale_ref[...]) + shift_ref[...]).astype(BF16)
    p = jnp.dot(h, w_ref[...], preferred_element_type=F32)
    head_rows = seg_len * SEGS_PER_TILE
    for half, ref in enumerate((xr_ref, g_ref)):
        o_rows = ref.reshape(RNN_HEADS * head_rows, LANES)
        for s in range(SEGS_PER_TILE):
            for hd in range(RNN_HEADS):
                col = half * RNN_WIDTH + hd * LANES
                o_rows[pl.ds(hd * head_rows + s, seg_len, stride=SEGS_PER_TILE), :] = (
                    p[s * seg_len:(s + 1) * seg_len, col:col + LANES])


def _odd_in_proj(x, shift, scale, w_in):
    batch, seq, _ = x.shape
    seg_len = seq // N_SEG
    tile_rows = SEGS_PER_TILE * seg_len
    vec = pl.BlockSpec((None, 1, D_MODEL), lambda b, i: (b, 0, 0))
    out_spec = pl.BlockSpec((None, RNN_HEADS, seg_len, SEGS_PER_TILE, RNN_HEAD_DIM), lambda b, i: (b, 0, 0, i, 0))
    out_shape = jax.ShapeDtypeStruct((batch, RNN_HEADS, seg_len, N_SEG, RNN_HEAD_DIM), F32)
    return pl.pallas_call(
        _odd_in_kernel,
        grid=(batch, seq // tile_rows),
        in_specs=[pl.BlockSpec((None, tile_rows, D_MODEL), lambda b, i: (b, i, 0)), vec, vec,
                  pl.BlockSpec((D_MODEL, 2 * RNN_WIDTH), lambda b, i: (0, 0))],
        out_specs=[out_spec, out_spec],
        out_shape=[out_shape, out_shape],
        compiler_params=_params("parallel", "parallel"),
        name="odd_in_proj",
    )(x, shift, scale, w_in.astype(BF16))


def _rglru_kernel(x_ref, g_ref, convw_ref, convb_ref, wcat_ref, bcat_ref, lam_ref, o_ref,
                  xpad, a_f, b_f, a_b, b_b, edge):
    n_rows = x_ref.shape[0]
    seg_len = n_rows // N_SEG
    pad_rows = CONV_LEFT * N_SEG
    seg = lax.broadcasted_iota(jnp.int32, (N_SEG, LANES), 0)

    def x_rows(start, size):
        return x_ref[pl.ds(start, size), :]

    def from_prev_segment(blk):
        return jnp.where(seg == 0, 0.0, pltpu.roll(blk, 1, 0))

    def from_next_segment(blk):
        return jnp.where(seg == N_SEG - 1, 0.0, pltpu.roll(blk, N_SEG - 1, 0))

    for c in range(n_rows // SCAN_CHUNK):
        xpad[pl.ds(pad_rows + c * SCAN_CHUNK, SCAN_CHUNK), :] = x_rows(c * SCAN_CHUNK, SCAN_CHUNK)
    for t in range(CONV_LEFT):
        xpad[pl.ds(t * N_SEG, N_SEG), :] = from_prev_segment(x_rows((seg_len - CONV_LEFT + t) * N_SEG, N_SEG))
    for t in range(CONV_WIDTH - 1 - CONV_LEFT):
        xpad[pl.ds(pad_rows + n_rows + t * N_SEG, N_SEG), :] = from_next_segment(x_rows(t * N_SEG, N_SEG))

    lam = lam_ref[...]
    softplus_neg = jnp.maximum(-lam, 0.0) + jnp.log1p(jnp.exp(-jnp.abs(lam)))
    half_rate = (0.5 * RG_LRU_C) * softplus_neg

    def gates_chunk(c, carry):
        base = pl.multiple_of(c * SCAN_CHUNK, SCAN_CHUNK)
        xr = convb_ref[...] + sum(convw_ref[k:k + 1, :] * xpad[pl.ds(base + k * N_SEG, SCAN_CHUNK), :]
                                  for k in range(CONV_WIDTH))
        th = jnp.tanh(jnp.dot(xr.astype(BF16), wcat_ref[...], preferred_element_type=F32) + bcat_ref[...])
        half_x = 0.5 * xr
        for d, (a_ref, b_ref) in enumerate(((a_f, b_f), (a_b, b_b))):
            o = 2 * d * LANES
            neg_log_a = (th[:, o:o + LANES] + 1.0) * half_rate[d:d + 1, :]
            a = jnp.exp2(neg_log_a * (-LOG2_E))
            one_minus_a2 = jnp.tanh(neg_log_a) * (a * a + 1.0)
            root = one_minus_a2 * lax.rsqrt(jnp.maximum(one_minus_a2, F32_TINY))
            a_ref[pl.ds(base, SCAN_CHUNK), :] = a
            b_ref[pl.ds(base, SCAN_CHUNK), :] = root * ((th[:, o + LANES:o + 2 * LANES] + 1.0) * half_x)
        return carry

    lax.fori_loop(0, n_rows // SCAN_CHUNK, gates_chunk, 0, unroll=2)

    def scan_step(t, carry):
        h_f, p_f, h_b, p_b = carry
        rf = pl.ds(pl.multiple_of(t * N_SEG, N_SEG), N_SEG)
        rb = pl.ds(pl.multiple_of((seg_len - 1 - t) * N_SEG, N_SEG), N_SEG)
        a = a_f[rf, :]
        h_f = a * h_f + b_f[rf, :]
        p_f = a * p_f
        b_f[rf, :] = h_f
        a_f[rf, :] = p_f
        a = a_b[rb, :]
        h_b = a * h_b + b_b[rb, :]
        p_b = a * p_b
        b_b[rb, :] = h_b
        a_b[rb, :] = p_b
        return h_f, p_f, h_b, p_b

    zeros = jnp.zeros((N_SEG, LANES), F32)
    ones = jnp.ones((N_SEG, LANES), F32)
    h_f, p_f, h_b, p_b = lax.fori_loop(0, seg_len, scan_step, (zeros, ones, zeros, ones), unroll=4)

    edge[0] = h_f
    edge[1] = p_f
    edge[2] = h_b
    edge[3] = p_b
    edge[4, 0:1, :] = jnp.zeros((1, LANES), F32)
    edge[5, N_SEG - 1:N_SEG, :] = jnp.zeros((1, LANES), F32)
    for s in range(1, N_SEG):
        edge[4, s:s + 1, :] = edge[0, s - 1:s, :] + edge[1, s - 1:s, :] * edge[4, s - 1:s, :]
        r = N_SEG - 1 - s
        edge[5, r:r + 1, :] = edge[2, r + 1:r + 2, :] + edge[3, r + 1:r + 2, :] * edge[5, r + 1:r + 2, :]
    blocks = SCAN_CHUNK // N_SEG
    in_f = jnp.concatenate([edge[4]] * blocks, axis=0)
    in_b = jnp.concatenate([edge[5]] * blocks, axis=0)

    def out_chunk(c, carry):
        rows = pl.ds(pl.multiple_of(c * SCAN_CHUNK, SCAN_CHUNK), SCAN_CHUNK)
        y = (b_f[rows, :] + a_f[rows, :] * in_f) + (b_b[rows, :] + a_b[rows, :] * in_b)
        g = g_ref[rows, :]
        o_ref[rows, :] = y * (g * _sigmoid(g))
        return carry

    lax.fori_loop(0, n_rows // SCAN_CHUNK, out_chunk, 0)


def _rglru(xr, g, conv_w, conv_b, w_a, b_a, w_x, b_x, lam):
    batch, _, n_rows, _ = xr.shape
    w_cat = (0.5 * jnp.concatenate([w_a[0], w_x[0], w_a[1], w_x[1]], axis=-1)).astype(BF16)
    per_head = lambda v: v.reshape(RNN_HEADS, 1, RNN_HEAD_DIM)
    b_cat = 0.5 * jnp.concatenate([per_head(b_a[0]), per_head(b_x[0]), per_head(b_a[1]), per_head(b_x[1])],
                                  axis=-1)
    col = lambda rows: pl.BlockSpec((rows, RNN_HEAD_DIM), lambda b, h: (0, h))
    seq_col = pl.BlockSpec((None, None, n_rows, RNN_HEAD_DIM), lambda b, h: (b, h, 0, 0))
    scratch_rows = n_rows + (CONV_WIDTH - 1) * N_SEG
    return pl.pallas_call(
        _rglru_kernel,
        grid=(batch, RNN_HEADS),
        in_specs=[seq_col, seq_col, col(CONV_WIDTH), col(1),
                  pl.BlockSpec((None, RNN_HEAD_DIM, 4 * RNN_HEAD_DIM), lambda b, h: (h, 0, 0)),
                  pl.BlockSpec((None, 1, 4 * RNN_HEAD_DIM), lambda b, h: (h, 0, 0)),
                  col(2)],
        out_specs=seq_col,
        out_shape=jax.ShapeDtypeStruct((batch, RNN_HEADS, n_rows, RNN_HEAD_DIM), F32),
        scratch_shapes=[pltpu.VMEM((scratch_rows, LANES), F32)]
                       + [pltpu.VMEM((n_rows, LANES), F32)] * 4
                       + [pltpu.VMEM((6, N_SEG, LANES), F32)],
        compiler_params=_params("parallel", "parallel"),
        name="rglru",
    )(xr, g, conv_w, conv_b.reshape(1, -1), w_cat, b_cat, lam)


def _odd_out_kernel(y_ref, x_ref, gate_ref, w_ref, lng_ref, lnb_ref, o_ref):
    seg_len = y_ref.shape[1]
    head_rows = seg_len * SEGS_PER_TILE
    y_rows = y_ref.reshape(RNN_HEADS * head_rows, LANES)
    y = jnp.concatenate(
        [jnp.concatenate([y_rows[pl.ds(hd * head_rows + s, seg_len, stride=SEGS_PER_TILE), :].astype(BF16)
                          for hd in range(RNN_HEADS)], axis=1)
         for s in range(SEGS_PER_TILE)], axis=0)
    out = jnp.dot(y, w_ref[...], preferred_element_type=F32)
    z = DEEPNORM_ALPHA * x_ref[...] + gate_ref[...] * out
    o_ref[...] = _layer_norm_rows(z, lng_ref[...], lnb_ref[...])


def _odd_out_proj(y, x, gate, w_out, ln_g, ln_b):
    batch, seq, _ = x.shape
    seg_len = seq // N_SEG
    tile_rows = SEGS_PER_TILE * seg_len
    const = lambda shape: pl.BlockSpec(shape, lambda b, i: (0,) * len(shape))
    return pl.pallas_call(
        _odd_out_kernel,
        grid=(batch, seq // tile_rows),
        in_specs=[pl.BlockSpec((None, RNN_HEADS, seg_len, SEGS_PER_TILE, RNN_HEAD_DIM),
                               lambda b, i: (b, 0, 0, i, 0)),
                  pl.BlockSpec((None, tile_rows, D_MODEL), lambda b, i: (b, i, 0)),
                  pl.BlockSpec((None, 1, D_MODEL), lambda b, i: (b, 0, 0)),
                  const((RNN_WIDTH, D_MODEL)), const((1, D_MODEL)), const((1, D_MODEL))],
        out_specs=pl.BlockSpec((None, tile_rows, D_MODEL), lambda b, i: (b, i, 0)),
        out_shape=jax.ShapeDtypeStruct((batch, seq, D_MODEL), F32),
        compiler_params=_params("parallel", "parallel"),
        name="odd_out_proj",
    )(y, x, gate, w_out.astype(BF16), ln_g.reshape(1, -1), ln_b.reshape(1, -1))


def kernel(x, c, positions, ada_w, ada_b, ln_g, ln_b, ev_w_in, ev_w_out, ev_sink, ev_sg_ln_g, ev_sg_ln_b,
           ev_sg_w, ev_sg_b, od_w_in, od_conv_w, od_conv_b, od_w_a, od_b_a, od_w_x, od_b_x, od_lam, od_w_out):
    batch, seq, d_model = x.shape
    seg_len = seq // N_SEG
    assert d_model == D_MODEL and seq % ROW_TILE == 0 and seq == N_SEG * seg_len
    assert N_SEG % SEGS_PER_TILE == 0 and seq % SCAN_CHUNK == 0 and SCAN_CHUNK % N_SEG == 0
    assert ada_w.shape[0] == DEPTH == 2

    mod = _modulation(c, ada_w, ada_b)
    shift, scale, gate = (mod[:, :, i * D_MODEL:(i + 1) * D_MODEL].reshape(DEPTH, batch, 1, D_MODEL)
                          for i in range(3))

    q, k, v, su, sv, g = _even_in_proj(x, shift[0], scale[0], positions, ev_w_in[0])
    x1 = _even_mixer(q, k, v, su, sv, g, x, gate[0], ev_sink[0], ev_sg_ln_g[0], ev_sg_ln_b[0],
                     ev_sg_w[0], ev_sg_b[0], ev_w_out[0], ln_g[0], ln_b[0])

    xr, g1 = _odd_in_proj(x1, shift[1], scale[1], od_w_in[0])
    scan_rows = (batch, RNN_HEADS, seq, RNN_HEAD_DIM)
    y = _rglru(xr.reshape(scan_rows), g1.reshape(scan_rows), od_conv_w[0], od_conv_b[0], od_w_a[0],
               od_b_a[0], od_w_x[0], od_b_x[0], od_lam[0])
    return _odd_out_proj(y.reshape(batch, RNN_HEADS, seg_len, N_SEG, RNN_HEAD_DIM), x1, gate[1], od_w_out[0],
                         ln_g[1], ln_b[1])
```

```python
import math

import numpy as np
import jax
import jax.numpy as jnp
from jax import lax
from jax.experimental import pallas as pl
from jax.experimental.pallas import tpu as pltpu

D_MODEL = 1024
DEPTH = 2
HEAD_DIM = 64
N_Q_HEADS = 8
N_KV_HEADS = 2
Q_PER_KV = N_Q_HEADS // N_KV_HEADS
ATTN_WIDTH = N_Q_HEADS * HEAD_DIM
KV_WIDTH = N_KV_HEADS * HEAD_DIM
ATTN_BLOCK = 128
ROPE_DIM = HEAD_DIM // 4
ROPE_HALF = ROPE_DIM // 2
ROPE_THETA = 500000.0
N_SG_GROUPS = 8
SG_GROUP_DIM = 64
SG_WIDTH = N_SG_GROUPS * SG_GROUP_DIM
SG_CHUNK = 128
EVEN_GATE_WIDTH = ATTN_WIDTH + SG_WIDTH
EVEN_IN_WIDTH = ATTN_WIDTH + 2 * KV_WIDTH + 2 * SG_WIDTH + EVEN_GATE_WIDTH
RNN_WIDTH = D_MODEL
RNN_HEADS = 8
RNN_HEAD_DIM = RNN_WIDTH // RNN_HEADS
CONV_WIDTH = 4
CONV_LEFT = 2
RG_LRU_C = 8.0
DEEPNORM_ALPHA = (2 * DEPTH) ** 0.25
LN_EPS = 1e-5
NEG_INF = -1e30
LOG2_E = math.log2(math.e)
F32_TINY = float(np.finfo(np.float32).tiny)

LANES = 128
SUBLANES = 8
VMEM_LIMIT = 56 * 1024 * 1024

ROW_TILE = 512
N_SEG = 32
SEGS_PER_TILE = SUBLANES
SCAN_CHUNK = 256

BF16 = jnp.bfloat16
F32 = jnp.float32


def _sigmoid(x):
    return 0.5 * jnp.tanh(0.5 * x) + 0.5


def _layer_norm_rows(z, g, b):
    mu = jnp.mean(z, axis=-1, keepdims=True)
    d = z - mu
    var = jnp.mean(d * d, axis=-1, keepdims=True)
    return d * lax.rsqrt(var + LN_EPS) * g + b


def _params(*semantics):
    return pltpu.CompilerParams(dimension_semantics=semantics, vmem_limit_bytes=VMEM_LIMIT)


def _mod_kernel(c_ref, w_ref, b_ref, o_ref):
    c = c_ref[...]
    cond = (c * _sigmoid(c)).astype(BF16)
    o_ref[...] = jnp.dot(cond, w_ref[...].astype(BF16), preferred_element_type=F32) + b_ref[...]


def _modulation(c, ada_w, ada_b):
    batch = c.shape[0]
    rows = -(-batch // SUBLANES) * SUBLANES
    c_pad = jnp.pad(c, ((0, rows - batch), (0, 0)))
    n_col = 3 * D_MODEL // D_MODEL
    out = pl.pallas_call(
        _mod_kernel,
        grid=(DEPTH, n_col),
        in_specs=[
            pl.BlockSpec((rows, D_MODEL), lambda l, j: (0, 0)),
            pl.BlockSpec((None, D_MODEL, D_MODEL), lambda l, j: (l, 0, j)),
            pl.BlockSpec((None, 1, D_MODEL), lambda l, j: (l, 0, j)),
        ],
        out_specs=pl.BlockSpec((None, rows, D_MODEL), lambda l, j: (l, 0, j)),
        out_shape=jax.ShapeDtypeStruct((DEPTH, rows, 3 * D_MODEL), F32),
        compiler_params=_params("parallel", "parallel"),
        name="adaln_mod",
    )(c_pad, ada_w, ada_b.reshape(DEPTH, 1, 3 * D_MODEL))
    return out[:, :batch]


def _even_in_kernel(x_ref, shift_ref, scale_ref, pos_ref, freq_ref, sign_ref, w_ref,
                    q_ref, k_ref, v_ref, su_ref, sv_ref, g_ref):
    lane = lax.broadcasted_iota(jnp.int32, (1, LANES), 1)
    first_half = (lane % HEAD_DIM) < ROPE_HALF
    q_scale = HEAD_DIM ** -0.5 * LOG2_E

    half_rows = x_ref.shape[0] // 2
    for r in range(2):
        rows = slice(r * half_rows, (r + 1) * half_rows)
        h = (x_ref[rows, :] * (1.0 + scale_ref[...]) + shift_ref[...]).astype(BF16)
        p = jnp.dot(h, w_ref[...], preferred_element_type=F32)

        ang = pos_ref[rows, :].astype(F32) * freq_ref[...]
        cs = jnp.cos(ang)
        sn = jnp.sin(ang) * sign_ref[...]

        def rotary(t):
            partner = jnp.where(first_half, pltpu.roll(t, LANES - ROPE_HALF, 1), pltpu.roll(t, ROPE_HALF, 1))
            return t * cs + partner * sn

        for j in range(ATTN_WIDTH // LANES):
            q_ref[rows, j * LANES:(j + 1) * LANES] = (
                rotary(p[:, j * LANES:(j + 1) * LANES]) * q_scale).astype(BF16)
        o = ATTN_WIDTH
        k_ref[rows, :] = rotary(p[:, o:o + KV_WIDTH]).astype(BF16)
        o += KV_WIDTH
        v_ref[rows, :] = p[:, o:o + KV_WIDTH].astype(BF16)
        o += KV_WIDTH
        su_ref[rows, :] = p[:, o:o + SG_WIDTH].astype(BF16)
        o += SG_WIDTH
        sv_ref[rows, :] = p[:, o:o + SG_WIDTH].astype(BF16)
        o += SG_WIDTH
        g_ref[rows, :] = p[:, o:o + EVEN_GATE_WIDTH].astype(BF16)


def _even_in_proj(x, shift, scale, positions, w_in):
    batch, seq, _ = x.shape
    lane = np.arange(LANES)
    d = lane % HEAD_DIM
    inv_freq = jnp.power(jnp.float32(ROPE_THETA), -jnp.arange(ROPE_HALF, dtype=F32) / ROPE_HALF)
    freq_lane = jnp.where(jnp.asarray(d < ROPE_DIM), inv_freq[d % ROPE_HALF], 0.0).reshape(1, LANES)
    sign_lane = jnp.asarray(np.where(d < ROPE_HALF, -1.0, np.where(d < ROPE_DIM, 1.0, 0.0)),
                            dtype=F32).reshape(1, LANES)
    row = lambda width: pl.BlockSpec((None, ROW_TILE, width), lambda b, i: (b, i, 0))
    vec = pl.BlockSpec((None, 1, D_MODEL), lambda b, i: (b, 0, 0))
    const = lambda shape: pl.BlockSpec(shape, lambda b, i: (0,) * len(shape))
    widths = (ATTN_WIDTH, KV_WIDTH, KV_WIDTH, SG_WIDTH, SG_WIDTH, EVEN_GATE_WIDTH)
    return pl.pallas_call(
        _even_in_kernel,
        grid=(batch, seq // ROW_TILE),
        in_specs=[row(D_MODEL), vec, vec, row(1), const((1, LANES)), const((1, LANES)),
                  const((D_MODEL, EVEN_IN_WIDTH))],
        out_specs=[row(w) for w in widths],
        out_shape=[jax.ShapeDtypeStruct((batch, seq, w), BF16) for w in widths],
        compiler_params=_params("parallel", "parallel"),
        name="even_in_proj",
    )(x, shift, scale, positions.reshape(batch, seq, 1), freq_lane, sign_lane, w_in.astype(BF16))


def _even_mix_kernel(sink_ref, q_ref, k_ref, v_ref, su_ref, sv_ref, g_ref, x_ref, gate_ref,
                     avg_ref, avg2_ref, sgg_ref, sgb_ref, sgw_ref, sgbias_ref, wout_ref, lng_ref, lnb_ref,
                     o_ref, kdup, vsplit, y_scr):
    seq = k_ref.shape[0]
    n_blocks = seq // ATTN_BLOCK
    sub_blocks = ROW_TILE // ATTN_BLOCK
    tile = pl.program_id(1)
    lane = lax.broadcasted_iota(jnp.int32, (1, LANES), 1)
    left = lane < HEAD_DIM

    @pl.when(tile == 0)
    def _():
        def build(c, carry):
            rows = pl.ds(pl.multiple_of(c * ROW_TILE, ROW_TILE), ROW_TILE)
            k = k_ref[rows, :].astype(F32)
            k_sw = pltpu.roll(k, HEAD_DIM, 1)
            kdup[rows, 0:LANES] = jnp.where(left, k, k_sw).astype(BF16)
            kdup[rows, LANES:2 * LANES] = jnp.where(left, k_sw, k).astype(BF16)
            v = v_ref[rows, :].astype(F32)
            v_sw = pltpu.roll(v, HEAD_DIM, 1)
            vsplit[rows, 0 * LANES:1 * LANES] = jnp.where(left, v, 0.0).astype(BF16)
            vsplit[rows, 1 * LANES:2 * LANES] = jnp.where(left, 0.0, v_sw).astype(BF16)
            vsplit[rows, 2 * LANES:3 * LANES] = jnp.where(left, v_sw, 0.0).astype(BF16)
            vsplit[rows, 3 * LANES:4 * LANES] = jnp.where(left, 0.0, v).astype(BF16)
            return carry
        lax.fori_loop(0, seq // ROW_TILE, build, 0)

    qi = lax.broadcasted_iota(jnp.int32, (ATTN_BLOCK, ATTN_BLOCK), 0)
    kj = lax.broadcasted_iota(jnp.int32, (ATTN_BLOCK, ATTN_BLOCK), 1)
    ones_cols = jnp.ones((3 * ATTN_BLOCK, LANES), BF16)
    avg = avg_ref[...]
    sink_cols = [jnp.concatenate([jnp.full((ATTN_BLOCK, LANES), sink_ref[h] * LOG2_E, F32)
                                  for h in (4 * hk, 4 * hk + 2, 4 * hk + 1, 4 * hk + 3)], axis=0)
                 for hk in range(N_KV_HEADS)]

    def sub_block(j, carry):
        blk = tile * sub_blocks + j
        prev = jnp.maximum(blk - 1, 0)
        nxt = jnp.minimum(blk + 1, n_blocks - 1)
        rows = pl.ds(pl.multiple_of(j * ATTN_BLOCK, ATTN_BLOCK), ATTN_BLOCK)
        bias_prev = jnp.where((kj >= qi) & (blk > 0), 0.0, NEG_INF)
        bias_next = jnp.where((kj <= qi) & (blk < n_blocks - 1), 0.0, NEG_INF)
        bias_prev = jnp.concatenate([bias_prev] * Q_PER_KV, axis=0)
        bias_next = jnp.concatenate([bias_next] * Q_PER_KV, axis=0)

        def band(ref):
            return jnp.concatenate(
                [ref[pl.ds(pl.multiple_of(b * ATTN_BLOCK, ATTN_BLOCK), ATTN_BLOCK), :] for b in (prev, blk, nxt)],
                axis=0)

        k_band = band(kdup)
        v_band = band(vsplit)
        n_sg = SG_WIDTH // LANES
        sg_cols = [slice(c * LANES, (c + 1) * LANES) for c in range(n_sg)]

        def silu_gate(cols):
            g_c = g_ref[rows, cols].astype(F32)
            return g_c * _sigmoid(g_c)

        scores = []
        for hk in range(N_KV_HEADS):
            q_cols = [q_ref[rows, (2 * hk + c) * LANES:(2 * hk + c + 1) * LANES] for c in range(2)]
            zero = jnp.zeros_like(q_cols[0])
            lhs = jnp.concatenate([jnp.where(left, qc, zero) for qc in q_cols]
                                  + [jnp.where(left, zero, qc) for qc in q_cols], axis=0)
            scores.append(lax.dot_general(lhs, k_band[:, hk * LANES:(hk + 1) * LANES], (((1,), (1,)), ((), ())),
                                          preferred_element_type=F32))
        sv_cols = [sv_ref[rows, cols] for cols in sg_cols]
        means = [jnp.dot(v_c, avg, preferred_element_type=F32) for v_c in sv_cols]

        probs, sink_terms = [], []
        for hk in range(N_KV_HEADS):
            s = scores[hk]
            s0 = s[:, 0:ATTN_BLOCK] + bias_prev
            s1 = s[:, ATTN_BLOCK:2 * ATTN_BLOCK]
            s2 = s[:, 2 * ATTN_BLOCK:3 * ATTN_BLOCK] + bias_next
            m = jnp.max(jnp.maximum(jnp.maximum(s0, s1), s2), axis=-1, keepdims=True)
            m = jnp.maximum(jnp.broadcast_to(m, sink_cols[hk].shape), sink_cols[hk])
            probs.append(jnp.concatenate([jnp.exp2(t - m).astype(BF16) for t in (s0, s1, s2)], axis=1))
            sink_terms.append(jnp.exp2(sink_cols[hk] - m))
        devs, sq_parts = [], []
        for v_c, mean in zip(sv_cols, means):
            dev = v_c.astype(F32) - mean
            sq = dev * dev
            sq_hi = sq.astype(BF16)
            sq_lo = (sq - sq_hi.astype(F32)).astype(BF16)
            devs.append(dev)
            sq_parts.append(jnp.concatenate([sq_hi, sq_lo], axis=1))

        half = 2 * ATTN_BLOCK
        pv = []
        for hk in range(N_KV_HEADS):
            rhs_l = jnp.concatenate([v_band[:, (2 * hk) * LANES:(2 * hk + 1) * LANES], ones_cols], axis=1)
            rhs_r = jnp.concatenate([v_band[:, (2 * hk + 1) * LANES:(2 * hk + 2) * LANES], ones_cols], axis=1)
            pv.append((jnp.dot(probs[hk][0:half], rhs_l, preferred_element_type=F32),
                       jnp.dot(probs[hk][half:2 * half], rhs_r, preferred_element_type=F32)))
        variances = [jnp.dot(sq, avg2_ref[...], preferred_element_type=F32) for sq in sq_parts]

        for hk in range(N_KV_HEADS):
            o_l, o_r = pv[hk]
            inv_l = 1.0 / (o_l[:, LANES:2 * LANES] + sink_terms[hk][0:half])
            inv_r = 1.0 / (o_r[:, LANES:2 * LANES] + sink_terms[hk][half:2 * half])
            pair = o_l[:, 0:LANES] * inv_l + o_r[:, 0:LANES] * inv_r
            for c in range(2):
                cols = slice((2 * hk + c) * LANES, (2 * hk + c + 1) * LANES)
                y_scr[rows, cols] = (pair[c * ATTN_BLOCK:(c + 1) * ATTN_BLOCK] * silu_gate(cols)).astype(BF16)

        for c, cols in enumerate(sg_cols):
            vn = devs[c] * lax.rsqrt(variances[c] + LN_EPS) * sgg_ref[:, cols] + sgb_ref[:, cols]
            stacked = jnp.concatenate([jnp.where(left, vn, 0.0), jnp.where(left, 0.0, vn)],
                                      axis=0).astype(BF16)
            mixed = jnp.dot(sgw_ref[c], stacked, preferred_element_type=F32) + sgbias_ref[:, cols]
            y_sg = su_ref[rows, cols].astype(F32) * mixed
            gc = slice(ATTN_WIDTH + c * LANES, ATTN_WIDTH + (c + 1) * LANES)
            y_scr[rows, gc] = (y_sg * silu_gate(gc)).astype(BF16)
        return carry

    lax.fori_loop(0, sub_blocks, sub_block, 0)

    out = jnp.dot(y_scr[...], wout_ref[...], preferred_element_type=F32)
    z = DEEPNORM_ALPHA * x_ref[...] + gate_ref[...] * out
    o_ref[...] = _layer_norm_rows(z, lng_ref[...], lnb_ref[...])


def _even_mixer(q, k, v, su, sv, g, x, gate, sink, sg_ln_g, sg_ln_b, sg_w, sg_b, w_out, ln_g, ln_b):
    batch, seq, _ = x.shape
    lane = np.arange(LANES)
    avg = jnp.asarray((lane[:, None] // SG_GROUP_DIM == lane[None, :] // SG_GROUP_DIM) / SG_GROUP_DIM, dtype=BF16)
    w_pairs = sg_w.reshape(N_SG_GROUPS // 2, 2, SG_CHUNK, SG_CHUNK).transpose(0, 2, 1, 3)
    w_pairs = w_pairs.reshape(N_SG_GROUPS // 2, SG_CHUNK, 2 * SG_CHUNK).astype(BF16)
    bias_full = jnp.repeat(sg_b.T, SG_GROUP_DIM, axis=1)
    row = lambda width: pl.BlockSpec((None, ROW_TILE, width), lambda b, i: (b, i, 0))
    full_seq = pl.BlockSpec((None, seq, KV_WIDTH), lambda b, i: (b, 0, 0))
    const = lambda shape: pl.BlockSpec(shape, lambda b, i: (0,) * len(shape))
    return pl.pallas_call(
        _even_mix_kernel,
        grid=(batch, seq // ROW_TILE),
        in_specs=[
            pl.BlockSpec(memory_space=pltpu.SMEM),
            row(ATTN_WIDTH), full_seq, full_seq, row(SG_WIDTH), row(SG_WIDTH), row(EVEN_GATE_WIDTH),
            row(D_MODEL), pl.BlockSpec((None, 1, D_MODEL), lambda b, i: (b, 0, 0)),
            const((LANES, LANES)), const((2 * LANES, LANES)), const((1, SG_WIDTH)), const((1, SG_WIDTH)),
            const((N_SG_GROUPS // 2, SG_CHUNK, 2 * SG_CHUNK)), const((SG_CHUNK, SG_WIDTH)),
            const((EVEN_GATE_WIDTH, D_MODEL)), const((1, D_MODEL)), const((1, D_MODEL)),
        ],
        out_specs=row(D_MODEL),
        out_shape=jax.ShapeDtypeStruct((batch, seq, D_MODEL), F32),
        scratch_shapes=[pltpu.VMEM((seq, 2 * KV_WIDTH), BF16), pltpu.VMEM((seq, 4 * KV_WIDTH), BF16),
                        pltpu.VMEM((ROW_TILE, EVEN_GATE_WIDTH), BF16)],
        compiler_params=_params("arbitrary", "arbitrary"),
        name="even_mixer",
    )(sink, q, k, v, su, sv, g, x, gate, avg, jnp.concatenate([avg, avg], axis=0), sg_ln_g.reshape(1, -1), sg_ln_b.reshape(1, -1),
      w_pairs, bias_full, w_out.astype(BF16), ln_g.reshape(1, -1), ln_b.reshape(1, -1))


def _odd_in_kernel(x_ref, shift_ref, scale_ref, w_ref, xr_ref, g_ref):
    seg_len = xr_ref.shape[1]
    h = (x_ref[...] * (1.0 + scale_ref[...]) + shift_ref[...]).astype(BF16)
    p = jnp.dot(h, w_ref[...], preferred_element_type=F32)
    head_rows = seg_len * SEGS_PER_TILE
    for half, ref in enumerate((xr_ref, g_ref)):
        o_rows = ref.reshape(RNN_HEADS * head_rows, LANES)
        for s in range(SEGS_PER_TILE):
            for hd in range(RNN_HEADS):
                col = half * RNN_WIDTH + hd * LANES
                o_rows[pl.ds(hd * head_rows + s, seg_len, stride=SEGS_PER_TILE), :] = (
                    p[s * seg_len:(s + 1) * seg_len, col:col + LANES])


def _odd_in_proj(x, shift, scale, w_in):
    batch, seq, _ = x.shape
    seg_len = seq // N_SEG
    tile_rows = SEGS_PER_TILE * seg_len
    vec = pl.BlockSpec((None, 1, D_MODEL), lambda b, i: (b, 0, 0))
    out_spec = pl.BlockSpec((None, RNN_HEADS, seg_len, SEGS_PER_TILE, RNN_HEAD_DIM), lambda b, i: (b, 0, 0, i, 0))
    out_shape = jax.ShapeDtypeStruct((batch, RNN_HEADS, seg_len, N_SEG, RNN_HEAD_DIM), F32)
    return pl.pallas_call(
        _odd_in_kernel,
        grid=(batch, seq // tile_rows),
        in_specs=[pl.BlockSpec((None, tile_rows, D_MODEL), lambda b, i: (b, i, 0)), vec, vec,
                  pl.BlockSpec((D_MODEL, 2 * RNN_WIDTH), lambda b, i: (0, 0))],
        out_specs=[out_spec, out_spec],
        out_shape=[out_shape, out_shape],
        compiler_params=_params("parallel", "parallel"),
        name="odd_in_proj",
    )(x, shift, scale, w_in.astype(BF16))


def _rglru_kernel(x_ref, g_ref, convw_ref, convb_ref, wcat_ref, bcat_ref, lam_ref, o_ref,
                  xpad, a_f, b_f, a_b, b_b, edge):
    n_rows = x_ref.shape[0]
    seg_len = n_rows // N_SEG
    pad_rows = CONV_LEFT * N_SEG
    seg = lax.broadcasted_iota(jnp.int32, (N_SEG, LANES), 0)

    def x_rows(start, size):
        return x_ref[pl.ds(start, size), :]

    def from_prev_segment(blk):
        return jnp.where(seg == 0, 0.0, pltpu.roll(blk, 1, 0))

    def from_next_segment(blk):
        return jnp.where(seg == N_SEG - 1, 0.0, pltpu.roll(blk, N_SEG - 1, 0))

    for c in range(n_rows // SCAN_CHUNK):
        xpad[pl.ds(pad_rows + c * SCAN_CHUNK, SCAN_CHUNK), :] = x_rows(c * SCAN_CHUNK, SCAN_CHUNK)
    for t in range(CONV_LEFT):
        xpad[pl.ds(t * N_SEG, N_SEG), :] = from_prev_segment(x_rows((seg_len - CONV_LEFT + t) * N_SEG, N_SEG))
    for t in range(CONV_WIDTH - 1 - CONV_LEFT):
        xpad[pl.ds(pad_rows + n_rows + t * N_SEG, N_SEG), :] = from_next_segment(x_rows(t * N_SEG, N_SEG))

    lam = lam_ref[...]
    softplus_neg = jnp.maximum(-lam, 0.0) + jnp.log1p(jnp.exp(-jnp.abs(lam)))
    half_rate = (0.5 * RG_LRU_C) * softplus_neg

    def gates_chunk(c, carry):
        base = pl.multiple_of(c * SCAN_CHUNK, SCAN_CHUNK)
        xr = convb_ref[...] + sum(convw_ref[k:k + 1, :] * xpad[pl.ds(base + k * N_SEG, SCAN_CHUNK), :]
                                  for k in range(CONV_WIDTH))
        th = jnp.tanh(jnp.dot(xr.astype(BF16), wcat_ref[...], preferred_element_type=F32) + bcat_ref[...])
        half_x = 0.5 * xr
        for d, (a_ref, b_ref) in enumerate(((a_f, b_f), (a_b, b_b))):
            o = 2 * d * LANES
            neg_log_a = (th[:, o:o + LANES] + 1.0) * half_rate[d:d + 1, :]
            a = jnp.exp2(neg_log_a * (-LOG2_E))
            one_minus_a2 = jnp.tanh(neg_log_a) * (a * a + 1.0)
            root = one_minus_a2 * lax.rsqrt(jnp.maximum(one_minus_a2, F32_TINY))
            a_ref[pl.ds(base, SCAN_CHUNK), :] = a
            b_ref[pl.ds(base, SCAN_CHUNK), :] = root * ((th[:, o + LANES:o + 2 * LANES] + 1.0) * half_x)
        return carry

    lax.fori_loop(0, n_rows // SCAN_CHUNK, gates_chunk, 0, unroll=2)

    def scan_step(t, carry):
        h_f, p_f, h_b, p_b = carry
        rf = pl.ds(pl.multiple_of(t * N_SEG, N_SEG), N_SEG)
        rb = pl.ds(pl.multiple_of((seg_len - 1 - t) * N_SEG, N_SEG), N_SEG)
        a = a_f[rf, :]
        h_f = a * h_f + b_f[rf, :]
        p_f = a * p_f
        b_f[rf, :] = h_f
        a_f[rf, :] = p_f
        a = a_b[rb, :]
        h_b = a * h_b + b_b[rb, :]
        p_b = a * p_b
        b_b[rb, :] = h_b
        a_b[rb, :] = p_b
        return h_f, p_f, h_b, p_b

    zeros = jnp.zeros((N_SEG, LANES), F32)
    ones = jnp.ones((N_SEG, LANES), F32)
    h_f, p_f, h_b, p_b = lax.fori_loop(0, seg_len, scan_step, (zeros, ones, zeros, ones), unroll=4)

    edge[0] = h_f
    edge[1] = p_f
    edge[2] = h_b
    edge[3] = p_b
    edge[4, 0:1, :] = jnp.zeros((1, LANES), F32)
    edge[5, N_SEG - 1:N_SEG, :] = jnp.zeros((1, LANES), F32)
    for s in range(1, N_SEG):
        edge[4, s:s + 1, :] = edge[0, s - 1:s, :] + edge[1, s - 1:s, :] * edge[4, s - 1:s, :]
        r = N_SEG - 1 - s
        edge[5, r:r + 1, :] = edge[2, r + 1:r + 2, :] + edge[3, r + 1:r + 2, :] * edge[5, r + 1:r + 2, :]
    blocks = SCAN_CHUNK // N_SEG
    in_f = jnp.concatenate([edge[4]] * blocks, axis=0)
    in_b = jnp.concatenate([edge[5]] * blocks, axis=0)

    def out_chunk(c, carry):
        rows = pl.ds(pl.multiple_of(c * SCAN_CHUNK, SCAN_CHUNK), SCAN_CHUNK)
        y = (b_f[rows, :] + a_f[rows, :] * in_f) + (b_b[rows, :] + a_b[rows, :] * in_b)
        g = g_ref[rows, :]
        o_ref[rows, :] = y * (g * _sigmoid(g))
        return carry

    lax.fori_loop(0, n_rows // SCAN_CHUNK, out_chunk, 0)


def _rglru(xr, g, conv_w, conv_b, w_a, b_a, w_x, b_x, lam):
    batch, _, n_rows, _ = xr.shape
    w_cat = (0.5 * jnp.concatenate([w_a[0], w_x[0], w_a[1], w_x[1]], axis=-1)).astype(BF16)
    per_head = lambda v: v.reshape(RNN_HEADS, 1, RNN_HEAD_DIM)
    b_cat = 0.5 * jnp.concatenate([per_head(b_a[0]), per_head(b_x[0]), per_head(b_a[1]), per_head(b_x[1])],
                                  axis=-1)
    col = lambda rows: pl.BlockSpec((rows, RNN_HEAD_DIM), lambda b, h: (0, h))
    seq_col = pl.BlockSpec((None, None, n_rows, RNN_HEAD_DIM), lambda b, h: (b, h, 0, 0))
    scratch_rows = n_rows + (CONV_WIDTH - 1) * N_SEG
    return pl.pallas_call(
        _rglru_kernel,
        grid=(batch, RNN_HEADS),
        in_specs=[seq_col, seq_col, col(CONV_WIDTH), col(1),
                  pl.BlockSpec((None, RNN_HEAD_DIM, 4 * RNN_HEAD_DIM), lambda b, h: (h, 0, 0)),
                  pl.BlockSpec((None, 1, 4 * RNN_HEAD_DIM), lambda b, h: (h, 0, 0)),
                  col(2)],
        out_specs=seq_col,
        out_shape=jax.ShapeDtypeStruct((batch, RNN_HEADS, n_rows, RNN_HEAD_DIM), F32),
        scratch_shapes=[pltpu.VMEM((scratch_rows, LANES), F32)]
                       + [pltpu.VMEM((n_rows, LANES), F32)] * 4
                       + [pltpu.VMEM((6, N_SEG, LANES), F32)],
        compiler_params=_params("parallel", "parallel"),
        name="rglru",
    )(xr, g, conv_w, conv_b.reshape(1, -1), w_cat, b_cat, lam)


def _odd_out_kernel(y_ref, x_ref, gate_ref, w_ref, lng_ref, lnb_ref, o_ref):
    seg_len = y_ref.shape[1]
    head_rows = seg_len * SEGS_PER_TILE
    y_rows = y_ref.reshape(RNN_HEADS * head_rows, LANES)
    y = jnp.concatenate(
        [jnp.concatenate([y_rows[pl.ds(hd * head_rows + s, seg_len, stride=SEGS_PER_TILE), :].astype(BF16)
                          for hd in range(RNN_HEADS)], axis=1)
         for s in range(SEGS_PER_TILE)], axis=0)
    out = jnp.dot(y, w_ref[...], preferred_element_type=F32)
    z = DEEPNORM_ALPHA * x_ref[...] + gate_ref[...] * out
    o_ref[...] = _layer_norm_rows(z, lng_ref[...], lnb_ref[...])


def _odd_out_proj(y, x, gate, w_out, ln_g, ln_b):
    batch, seq, _ = x.shape
    seg_len = seq // N_SEG
    tile_rows = SEGS_PER_TILE * seg_len
    const = lambda shape: pl.BlockSpec(shape, lambda b, i: (0,) * len(shape))
    return pl.pallas_call(
        _odd_out_kernel,
        grid=(batch, seq // tile_rows),
        in_specs=[pl.BlockSpec((None, RNN_HEADS, seg_len, SEGS_PER_TILE, RNN_HEAD_DIM),
                               lambda b, i: (b, 0, 0, i, 0)),
                  pl.BlockSpec((None, tile_rows, D_MODEL), lambda b, i: (b, i, 0)),
                  pl.BlockSpec((None, 1, D_MODEL), lambda b, i: (b, 0, 0)),
                  const((RNN_WIDTH, D_MODEL)), const((1, D_MODEL)), const((1, D_MODEL))],
        out_specs=pl.BlockSpec((None, tile_rows, D_MODEL), lambda b, i: (b, i, 0)),
        out_shape=jax.ShapeDtypeStruct((batch, seq, D_MODEL), F32),
        compiler_params=_params("parallel", "parallel"),
        name="odd_out_proj",
    )(y, x, gate, w_out.astype(BF16), ln_g.reshape(1, -1), ln_b.reshape(1, -1))


def kernel(x, c, positions, ada_w, ada_b, ln_g, ln_b, ev_w_in, ev_w_out, ev_sink, ev_sg_ln_g, ev_sg_ln_b,
           ev_sg_w, ev_sg_b, od_w_in, od_conv_w, od_conv_b, od_w_a, od_b_a, od_w_x, od_b_x, od_lam, od_w_out):
    batch, seq, d_model = x.shape
    seg_len = seq // N_SEG
    assert d_model == D_MODEL and seq % ROW_TILE == 0 and seq == N_SEG * seg_len
    assert N_SEG % SEGS_PER_TILE == 0 and seq % SCAN_CHUNK == 0 and SCAN_CHUNK % N_SEG == 0
    assert ada_w.shape[0] == DEPTH == 2

    mod = _modulation(c, ada_w, ada_b)
    shift, scale, gate = (mod[:, :, i * D_MODEL:(i + 1) * D_MODEL].reshape(DEPTH, batch, 1, D_MODEL)
                          for i in range(3))

    q, k, v, su, sv, g = _even_in_proj(x, shift[0], scale[0], positions, ev_w_in[0])
    x1 = _even_mixer(q, k, v, su, sv, g, x, gate[0], ev_sink[0], ev_sg_ln_g[0], ev_sg_ln_b[0],
                     ev_sg_w[0], ev_sg_b[0], ev_w_out[0], ln_g[0], ln_b[0])

    xr, g1 = _odd_in_proj(x1, shift[1], scale[1], od_w_in[0])
    scan_rows = (batch, RNN_HEADS, seq, RNN_HEAD_DIM)
    y = _rglru(xr.reshape(scan_rows), g1.reshape(scan_rows), od_conv_w[0], od_conv_b[0], od_w_a[0],
               od_b_a[0], od_w_x[0], od_b_x[0], od_lam[0])
    return _odd_out_proj(y.reshape(batch, RNN_HEADS, seg_len, N_SEG, RNN_HEAD_DIM), x1, gate[1], od_w_out[0],
                         ln_g[1], ln_b[1])
```

```python
import math

import numpy as np
import jax
import jax.numpy as jnp
from jax import lax
from jax.experimental import pallas as pl
from jax.experimental.pallas import tpu as pltpu

D_MODEL = 1024
DEPTH = 2
HEAD_DIM = 64
N_Q_HEADS = 8
N_KV_HEADS = 2
Q_PER_KV = N_Q_HEADS // N_KV_HEADS
ATTN_WIDTH = N_Q_HEADS * HEAD_DIM
KV_WIDTH = N_KV_HEADS * HEAD_DIM
ATTN_BLOCK = 128
ROPE_DIM = HEAD_DIM // 4
ROPE_HALF = ROPE_DIM // 2
ROPE_THETA = 500000.0
N_SG_GROUPS = 8
SG_GROUP_DIM = 64
SG_WIDTH = N_SG_GROUPS * SG_GROUP_DIM
SG_CHUNK = 128
EVEN_GATE_WIDTH = ATTN_WIDTH + SG_WIDTH
EVEN_IN_WIDTH = ATTN_WIDTH + 2 * KV_WIDTH + 2 * SG_WIDTH + EVEN_GATE_WIDTH
RNN_WIDTH = D_MODEL
RNN_HEADS = 8
RNN_HEAD_DIM = RNN_WIDTH // RNN_HEADS
CONV_WIDTH = 4
CONV_LEFT = 2
RG_LRU_C = 8.0
DEEPNORM_ALPHA = (2 * DEPTH) ** 0.25
LN_EPS = 1e-5
NEG_INF = -1e30
LOG2_E = math.log2(math.e)
F32_TINY = float(np.finfo(np.float32).tiny)

LANES = 128
SUBLANES = 8
VMEM_LIMIT = 56 * 1024 * 1024

ROW_TILE = 512
N_SEG = 32
SEGS_PER_TILE = SUBLANES
SCAN_CHUNK = 256

BF16 = jnp.bfloat16
F32 = jnp.float32


def _sigmoid(x):
    return 0.5 * jnp.tanh(0.5 * x) + 0.5


def _layer_norm_rows(z, g, b):
    mu = jnp.mean(z, axis=-1, keepdims=True)
    d = z - mu
    var = jnp.mean(d * d, axis=-1, keepdims=True)
    return d * lax.rsqrt(var + LN_EPS) * g + b


def _params(*semantics):
    return pltpu.CompilerParams(dimension_semantics=semantics, vmem_limit_bytes=VMEM_LIMIT)


def _mod_kernel(c_ref, w_ref, b_ref, o_ref):
    c = c_ref[...]
    cond = (c * _sigmoid(c)).astype(BF16)
    o_ref[...] = jnp.dot(cond, w_ref[...].astype(BF16), preferred_element_type=F32) + b_ref[...]


def _modulation(c, ada_w, ada_b):
    batch = c.shape[0]
    rows = -(-batch // SUBLANES) * SUBLANES
    c_pad = jnp.pad(c, ((0, rows - batch), (0, 0)))
    n_col = 3 * D_MODEL // D_MODEL
    out = pl.pallas_call(
        _mod_kernel,
        grid=(DEPTH, n_col),
        in_specs=[
            pl.BlockSpec((rows, D_MODEL), lambda l, j: (0, 0)),
            pl.BlockSpec((None, D_MODEL, D_MODEL), lambda l, j: (l, 0, j)),
            pl.BlockSpec((None, 1, D_MODEL), lambda l, j: (l, 0, j)),
        ],
        out_specs=pl.BlockSpec((None, rows, D_MODEL), lambda l, j: (l, 0, j)),
        out_shape=jax.ShapeDtypeStruct((DEPTH, rows, 3 * D_MODEL), F32),
        compiler_params=_params("parallel", "parallel"),
        name="adaln_mod",
    )(c_pad, ada_w, ada_b.reshape(DEPTH, 1, 3 * D_MODEL))
    return out[:, :batch]


def _even_in_kernel(x_ref, shift_ref, scale_ref, pos_ref, freq_ref, sign_ref, w_ref,
                    q_ref, k_ref, v_ref, su_ref, sv_ref, g_ref):
    lane = lax.broadcasted_iota(jnp.int32, (1, LANES), 1)
    first_half = (lane % HEAD_DIM) < ROPE_HALF
    q_scale = HEAD_DIM ** -0.5 * LOG2_E

    half_rows = x_ref.shape[0] // 2
    for r in range(2):
        rows = slice(r * half_rows, (r + 1) * half_rows)
        h = (x_ref[rows, :] * (1.0 + scale_ref[...]) + shift_ref[...]).astype(BF16)
        p = jnp.dot(h, w_ref[...], preferred_element_type=F32)

        ang = pos_ref[rows, :].astype(F32) * freq_ref[...]
        cs = jnp.cos(ang)
        sn = jnp.sin(ang) * sign_ref[...]

        def rotary(t):
            partner = jnp.where(first_half, pltpu.roll(t, LANES - ROPE_HALF, 1), pltpu.roll(t, ROPE_HALF, 1))
            return t * cs + partner * sn

        for j in range(ATTN_WIDTH // LANES):
            q_ref[rows, j * LANES:(j + 1) * LANES] = (
                rotary(p[:, j * LANES:(j + 1) * LANES]) * q_scale).astype(BF16)
        o = ATTN_WIDTH
        k_ref[rows, :] = rotary(p[:, o:o + KV_WIDTH]).astype(BF16)
        o += KV_WIDTH
        v_ref[rows, :] = p[:, o:o + KV_WIDTH].astype(BF16)
        o += KV_WIDTH
        su_ref[rows, :] = p[:, o:o + SG_WIDTH].astype(BF16)
        o += SG_WIDTH
        sv_ref[rows, :] = p[:, o:o + SG_WIDTH].astype(BF16)
        o += SG_WIDTH
        g_ref[rows, :] = p[:, o:o + EVEN_GATE_WIDTH].astype(BF16)


def _even_in_proj(x, shift, scale, positions, w_in):
    batch, seq, _ = x.shape
    lane = np.arange(LANES)
    d = lane % HEAD_DIM
    inv_freq = jnp.power(jnp.float32(ROPE_THETA), -jnp.arange(ROPE_HALF, dtype=F32) / ROPE_HALF)
    freq_lane = jnp.where(jnp.asarray(d < ROPE_DIM), inv_freq[d % ROPE_HALF], 0.0).reshape(1, LANES)
    sign_lane = jnp.asarray(np.where(d < ROPE_HALF, -1.0, np.where(d < ROPE_DIM, 1.0, 0.0)),
                            dtype=F32).reshape(1, LANES)
    row = lambda width: pl.BlockSpec((None, ROW_TILE, width), lambda b, i: (b, i, 0))
    vec = pl.BlockSpec((None, 1, D_MODEL), lambda b, i: (b, 0, 0))
    const = lambda shape: pl.BlockSpec(shape, lambda b, i: (0,) * len(shape))
    widths = (ATTN_WIDTH, KV_WIDTH, KV_WIDTH, SG_WIDTH, SG_WIDTH, EVEN_GATE_WIDTH)
    return pl.pallas_call(
        _even_in_kernel,
        grid=(batch, seq // ROW_TILE),
        in_specs=[row(D_MODEL), vec, vec, row(1), const((1, LANES)), const((1, LANES)),
                  const((D_MODEL, EVEN_IN_WIDTH))],
        out_specs=[row(w) for w in widths],
        out_shape=[jax.ShapeDtypeStruct((batch, seq, w), BF16) for w in widths],
        compiler_params=_params("parallel", "parallel"),
        name="even_in_proj",
    )(x, shift, scale, positions.reshape(batch, seq, 1), freq_lane, sign_lane, w_in.astype(BF16))


def _even_mix_kernel(sink_ref, q_ref, k_ref, v_ref, su_ref, sv_ref, g_ref, x_ref, gate_ref,
                     avg_ref, avg2_ref, sgg_ref, sgb_ref, sgw_ref, sgbias_ref, wout_ref, lng_ref, lnb_ref,
                     o_ref, kdup, vsplit, y_scr):
    seq = k_ref.shape[0]
    n_blocks = seq // ATTN_BLOCK
    sub_blocks = ROW_TILE // ATTN_BLOCK
    tile = pl.program_id(1)
    lane = lax.broadcasted_iota(jnp.int32, (1, LANES), 1)
    left = lane < HEAD_DIM

    @pl.when(tile == 0)
    def _():
        def build(c, carry):
            rows = pl.ds(pl.multiple_of(c * ROW_TILE, ROW_TILE), ROW_TILE)
            k = k_ref[rows, :].astype(F32)
            k_sw = pltpu.roll(k, HEAD_DIM, 1)
            kdup[rows, 0:LANES] = jnp.where(left, k, k_sw).astype(BF16)
            kdup[rows, LANES:2 * LANES] = jnp.where(left, k_sw, k).astype(BF16)
            v = v_ref[rows, :].astype(F32)
            v_sw = pltpu.roll(v, HEAD_DIM, 1)
            vsplit[rows, 0 * LANES:1 * LANES] = jnp.where(left, v, 0.0).astype(BF16)
            vsplit[rows, 1 * LANES:2 * LANES] = jnp.where(left, 0.0, v_sw).astype(BF16)
            vsplit[rows, 2 * LANES:3 * LANES] = jnp.where(left, v_sw, 0.0).astype(BF16)
            vsplit[rows, 3 * LANES:4 * LANES] = jnp.where(left, 0.0, v).astype(BF16)
            return carry
        lax.fori_loop(0, seq // ROW_TILE, build, 0)

    qi = lax.broadcasted_iota(jnp.int32, (ATTN_BLOCK, ATTN_BLOCK), 0)
    kj = lax.broadcasted_iota(jnp.int32, (ATTN_BLOCK, ATTN_BLOCK), 1)
    ones_left = jnp.broadcast_to(jnp.where(left, 1.0, 0.0).astype(BF16), (3 * ATTN_BLOCK, LANES))
    ones_right = jnp.broadcast_to(jnp.where(left, 0.0, 1.0).astype(BF16), (3 * ATTN_BLOCK, LANES))
    avg = avg_ref[...]
    sink_cols = [jnp.concatenate([jnp.full((ATTN_BLOCK, LANES), sink_ref[h] * LOG2_E, F32)
                                  for h in (4 * hk, 4 * hk + 2, 4 * hk + 1, 4 * hk + 3)], axis=0)
                 for hk in range(N_KV_HEADS)]

    def sub_block(j, carry):
        blk = tile * sub_blocks + j
        prev = jnp.maximum(blk - 1, 0)
        nxt = jnp.minimum(blk + 1, n_blocks - 1)
        rows = pl.ds(pl.multiple_of(j * ATTN_BLOCK, ATTN_BLOCK), ATTN_BLOCK)
        bias_prev = jnp.where((kj >= qi) & (blk > 0), 0.0, NEG_INF)
        bias_next = jnp.where((kj <= qi) & (blk < n_blocks - 1), 0.0, NEG_INF)
        bias_prev = jnp.concatenate([bias_prev] * Q_PER_KV, axis=0)
        bias_next = jnp.concatenate([bias_next] * Q_PER_KV, axis=0)

        def band(ref):
            return jnp.concatenate(
                [ref[pl.ds(pl.multiple_of(b * ATTN_BLOCK, ATTN_BLOCK), ATTN_BLOCK), :] for b in (prev, blk, nxt)],
                axis=0)

        k_band = band(kdup)
        v_band = band(vsplit)
        n_sg = SG_WIDTH // LANES
        sg_cols = [slice(c * LANES, (c + 1) * LANES) for c in range(n_sg)]

        def silu_gate(cols):
            g_c = g_ref[rows, cols].astype(F32)
            return g_c * _sigmoid(g_c)

        scores = []
        for hk in range(N_KV_HEADS):
            q_cols = [q_ref[rows, (2 * hk + c) * LANES:(2 * hk + c + 1) * LANES] for c in range(2)]
            zero = jnp.zeros_like(q_cols[0])
            lhs = jnp.concatenate([jnp.where(left, qc, zero) for qc in q_cols]
                                  + [jnp.where(left, zero, qc) for qc in q_cols], axis=0)
            scores.append(lax.dot_general(lhs, k_band[:, hk * LANES:(hk + 1) * LANES], (((1,), (1,)), ((), ())),
                                          preferred_element_type=F32))
        sv_cols = [sv_ref[rows, cols] for cols in sg_cols]
        means = [jnp.dot(v_c, avg, preferred_element_type=F32) for v_c in sv_cols]

        probs, row_max = [], []
        for hk in range(N_KV_HEADS):
            s = scores[hk]
            s0 = s[:, 0:ATTN_BLOCK] + bias_prev
            s1 = s[:, ATTN_BLOCK:2 * ATTN_BLOCK]
            s2 = s[:, 2 * ATTN_BLOCK:3 * ATTN_BLOCK] + bias_next
            m = jnp.max(jnp.maximum(jnp.maximum(s0, s1), s2), axis=-1, keepdims=True)
            m = jnp.maximum(jnp.broadcast_to(m, sink_cols[hk].shape), sink_cols[hk])
            probs.append(jnp.concatenate([jnp.exp2(t - m).astype(BF16) for t in (s0, s1, s2)], axis=1))
            row_max.append(m)
        devs, sq_parts = [], []
        for v_c, mean in zip(sv_cols, means):
            dev = v_c.astype(F32) - mean
            sq = dev * dev
            sq_hi = sq.astype(BF16)
            sq_lo = (sq - sq_hi.astype(F32)).astype(BF16)
            devs.append(dev)
            sq_parts.append(jnp.concatenate([sq_hi, sq_lo], axis=1))

        pv = []
        for hk in range(N_KV_HEADS):
            p = probs[hk]
            lhs = jnp.concatenate(
                [jnp.concatenate([p[c * ATTN_BLOCK:(c + 1) * ATTN_BLOCK], p[(2 + c) * ATTN_BLOCK:(3 + c) * ATTN_BLOCK]],
                                 axis=1) for c in range(2)], axis=0)
            rhs = jnp.concatenate(
                [jnp.concatenate([v_band[:, (2 * hk) * LANES:(2 * hk + 1) * LANES], ones_left], axis=1),
                 jnp.concatenate([v_band[:, (2 * hk + 1) * LANES:(2 * hk + 2) * LANES], ones_right], axis=1)],
                axis=0)
            pv.append(jnp.dot(lhs, rhs, preferred_element_type=F32))
        variances = [jnp.dot(sq, avg2_ref[...], preferred_element_type=F32) for sq in sq_parts]

        for hk in range(N_KV_HEADS):
            for c in range(2):
                even, odd = slice(c * ATTN_BLOCK, (c + 1) * ATTN_BLOCK), slice((2 + c) * ATTN_BLOCK, (3 + c) * ATTN_BLOCK)
                sink_term = jnp.exp2(jnp.where(left, sink_cols[hk][even] - row_max[hk][even],
                                               sink_cols[hk][odd] - row_max[hk][odd]))
                o = pv[hk][c * ATTN_BLOCK:(c + 1) * ATTN_BLOCK]
                pair = o[:, 0:LANES] * (1.0 / (o[:, LANES:2 * LANES] + sink_term))
                cols = slice((2 * hk + c) * LANES, (2 * hk + c + 1) * LANES)
                y_scr[rows, cols] = (pair * silu_gate(cols)).astype(BF16)

        for c, cols in enumerate(sg_cols):
            vn = devs[c] * lax.rsqrt(variances[c] + LN_EPS) * sgg_ref[:, cols] + sgb_ref[:, cols]
            stacked = jnp.concatenate([jnp.where(left, vn, 0.0), jnp.where(left, 0.0, vn)],
                                      axis=0).astype(BF16)
            mixed = jnp.dot(sgw_ref[c], stacked, preferred_element_type=F32) + sgbias_ref[:, cols]
            y_sg = su_ref[rows, cols].astype(F32) * mixed
            gc = slice(ATTN_WIDTH + c * LANES, ATTN_WIDTH + (c + 1) * LANES)
            y_scr[rows, gc] = (y_sg * silu_gate(gc)).astype(BF16)
        return carry

    lax.fori_loop(0, sub_blocks, sub_block, 0)

    for r in range(2):
        half_rows = slice(r * (ROW_TILE // 2), (r + 1) * (ROW_TILE // 2))
        out = jnp.dot(y_scr[half_rows, :], wout_ref[...], preferred_element_type=F32)
        z = DEEPNORM_ALPHA * x_ref[half_rows, :] + gate_ref[...] * out
        o_ref[half_rows, :] = _layer_norm_rows(z, lng_ref[...], lnb_ref[...])


def _even_mixer(q, k, v, su, sv, g, x, gate, sink, sg_ln_g, sg_ln_b, sg_w, sg_b, w_out, ln_g, ln_b):
    batch, seq, _ = x.shape
    lane = np.arange(LANES)
    avg = jnp.asarray((lane[:, None] // SG_GROUP_DIM == lane[None, :] // SG_GROUP_DIM) / SG_GROUP_DIM, dtype=BF16)
    w_pairs = sg_w.reshape(N_SG_GROUPS // 2, 2, SG_CHUNK, SG_CHUNK).transpose(0, 2, 1, 3)
    w_pairs = w_pairs.reshape(N_SG_GROUPS // 2, SG_CHUNK, 2 * SG_CHUNK).astype(BF16)
    bias_full = jnp.repeat(sg_b.T, SG_GROUP_DIM, axis=1)
    row = lambda width: pl.BlockSpec((None, ROW_TILE, width), lambda b, i: (b, i, 0))
    full_seq = pl.BlockSpec((None, seq, KV_WIDTH), lambda b, i: (b, 0, 0))
    const = lambda shape: pl.BlockSpec(shape, lambda b, i: (0,) * len(shape))
    return pl.pallas_call(
        _even_mix_kernel,
        grid=(batch, seq // ROW_TILE),
        in_specs=[
            pl.BlockSpec(memory_space=pltpu.SMEM),
            row(ATTN_WIDTH), full_seq, full_seq, row(SG_WIDTH), row(SG_WIDTH), row(EVEN_GATE_WIDTH),
            row(D_MODEL), pl.BlockSpec((None, 1, D_MODEL), lambda b, i: (b, 0, 0)),
            const((LANES, LANES)), const((2 * LANES, LANES)), const((1, SG_WIDTH)), const((1, SG_WIDTH)),
            const((N_SG_GROUPS // 2, SG_CHUNK, 2 * SG_CHUNK)), const((SG_CHUNK, SG_WIDTH)),
            const((EVEN_GATE_WIDTH, D_MODEL)), const((1, D_MODEL)), const((1, D_MODEL)),
        ],
        out_specs=row(D_MODEL),
        out_shape=jax.ShapeDtypeStruct((batch, seq, D_MODEL), F32),
        scratch_shapes=[pltpu.VMEM((seq, 2 * KV_WIDTH), BF16), pltpu.VMEM((seq, 4 * KV_WIDTH), BF16),
                        pltpu.VMEM((ROW_TILE, EVEN_GATE_WIDTH), BF16)],
        compiler_params=_params("arbitrary", "arbitrary"),
        name="even_mixer",
    )(sink, q, k, v, su, sv, g, x, gate, avg, jnp.concatenate([avg, avg], axis=0), sg_ln_g.reshape(1, -1), sg_ln_b.reshape(1, -1),
      w_pairs, bias_full, w_out.astype(BF16), ln_g.reshape(1, -1), ln_b.reshape(1, -1))


def _odd_in_kernel(x_ref, shift_ref, scale_ref, w_ref, xr_ref, g_ref):
    seg_len = xr_ref.shape[1]
    h = (x_ref[...] * (1.0 + scale_ref[...]) + shift_ref[...]).astype(BF16)
    p = jnp.dot(h, w_ref[...], preferred_element_type=F32)
    head_rows = seg_len * SEGS_PER_TILE
    for half, ref in enumerate((xr_ref, g_ref)):
        o_rows = ref.reshape(RNN_HEADS * head_rows, LANES)
        for s in range(SEGS_PER_TILE):
            for hd in range(RNN_HEADS):
                col = half * RNN_WIDTH + hd * LANES
                o_rows[pl.ds(hd * head_rows + s, seg_len, stride=SEGS_PER_TILE), :] = (
                    p[s * seg_len:(s + 1) * seg_len, col:col + LANES])


def _odd_in_proj(x, shift, scale, w_in):
    batch, seq, _ = x.shape
    seg_len = seq // N_SEG
    tile_rows = SEGS_PER_TILE * seg_len
    vec = pl.BlockSpec((None, 1, D_MODEL), lambda b, i: (b, 0, 0))
    out_spec = pl.BlockSpec((None, RNN_HEADS, seg_len, SEGS_PER_TILE, RNN_HEAD_DIM), lambda b, i: (b, 0, 0, i, 0))
    out_shape = jax.ShapeDtypeStruct((batch, RNN_HEADS, seg_len, N_SEG, RNN_HEAD_DIM), F32)
    return pl.pallas_call(
        _odd_in_kernel,
        grid=(batch, seq // tile_rows),
        in_specs=[pl.BlockSpec((None, tile_rows, D_MODEL), lambda b, i: (b, i, 0)), vec, vec,
                  pl.BlockSpec((D_MODEL, 2 * RNN_WIDTH), lambda b, i: (0, 0))],
        out_specs=[out_spec, out_spec],
        out_shape=[out_shape, out_shape],
        compiler_params=_params("parallel", "parallel"),
        name="odd_in_proj",
    )(x, shift, scale, w_in.astype(BF16))


def _rglru_kernel(x_ref, g_ref, convw_ref, convb_ref, wcat_ref, bcat_ref, lam_ref, o_ref,
                  xpad, a_f, b_f, a_b, b_b, edge):
    n_rows = x_ref.shape[0]
    seg_len = n_rows // N_SEG
    pad_rows = CONV_LEFT * N_SEG
    seg = lax.broadcasted_iota(jnp.int32, (N_SEG, LANES), 0)

    def x_rows(start, size):
        return x_ref[pl.ds(start, size), :]

    def from_prev_segment(blk):
        return jnp.where(seg == 0, 0.0, pltpu.roll(blk, 1, 0))

    def from_next_segment(blk):
        return jnp.where(seg == N_SEG - 1, 0.0, pltpu.roll(blk, N_SEG - 1, 0))

    for c in range(n_rows // SCAN_CHUNK):
        xpad[pl.ds(pad_rows + c * SCAN_CHUNK, SCAN_CHUNK), :] = x_rows(c * SCAN_CHUNK, SCAN_CHUNK)
    for t in range(CONV_LEFT):
        xpad[pl.ds(t * N_SEG, N_SEG), :] = from_prev_segment(x_rows((seg_len - CONV_LEFT + t) * N_SEG, N_SEG))
    for t in range(CONV_WIDTH - 1 - CONV_LEFT):
        xpad[pl.ds(pad_rows + n_rows + t * N_SEG, N_SEG), :] = from_next_segment(x_rows(t * N_SEG, N_SEG))

    lam = lam_ref[...]
    softplus_neg = jnp.maximum(-lam, 0.0) + jnp.log1p(jnp.exp(-jnp.abs(lam)))
    half_rate = (0.5 * RG_LRU_C) * softplus_neg

    def gates_chunk(c, carry):
        base = pl.multiple_of(c * SCAN_CHUNK, SCAN_CHUNK)
        xr = convb_ref[...] + sum(convw_ref[k:k + 1, :] * xpad[pl.ds(base + k * N_SEG, SCAN_CHUNK), :]
                                  for k in range(CONV_WIDTH))
        th = jnp.tanh(jnp.dot(xr.astype(BF16), wcat_ref[...], preferred_element_type=F32) + bcat_ref[...])
        half_x = 0.5 * xr
        for d, (a_ref, b_ref) in enumerate(((a_f, b_f), (a_b, b_b))):
            o = 2 * d * LANES
            neg_log_a = (th[:, o:o + LANES] + 1.0) * half_rate[d:d + 1, :]
            a = jnp.exp2(neg_log_a * (-LOG2_E))
            one_minus_a2 = jnp.tanh(neg_log_a) * (a * a + 1.0)
            root = one_minus_a2 * lax.rsqrt(jnp.maximum(one_minus_a2, F32_TINY))
            a_ref[pl.ds(base, SCAN_CHUNK), :] = a
            b_ref[pl.ds(base, SCAN_CHUNK), :] = root * ((th[:, o + LANES:o + 2 * LANES] + 1.0) * half_x)
        return carry

    lax.fori_loop(0, n_rows // SCAN_CHUNK, gates_chunk, 0, unroll=4)

    def scan_step(t, carry):
        h_f, p_f, h_b, p_b = carry
        rf = pl.ds(pl.multiple_of(t * N_SEG, N_SEG), N_SEG)
        rb = pl.ds(pl.multiple_of((seg_len - 1 - t) * N_SEG, N_SEG), N_SEG)
        a = a_f[rf, :]
        h_f = a * h_f + b_f[rf, :]
        p_f = a * p_f
        b_f[rf, :] = h_f
        a_f[rf, :] = p_f
        a = a_b[rb, :]
        h_b = a * h_b + b_b[rb, :]
        p_b = a * p_b
        b_b[rb, :] = h_b
        a_b[rb, :] = p_b
        return h_f, p_f, h_b, p_b

    zeros = jnp.zeros((N_SEG, LANES), F32)
    ones = jnp.ones((N_SEG, LANES), F32)
    h_f, p_f, h_b, p_b = lax.fori_loop(0, seg_len, scan_step, (zeros, ones, zeros, ones), unroll=4)

    edge[0] = h_f
    edge[1] = p_f
    edge[2] = h_b
    edge[3] = p_b
    edge[4, 0:1, :] = jnp.zeros((1, LANES), F32)
    edge[5, N_SEG - 1:N_SEG, :] = jnp.zeros((1, LANES), F32)
    for s in range(1, N_SEG):
        edge[4, s:s + 1, :] = edge[0, s - 1:s, :] + edge[1, s - 1:s, :] * edge[4, s - 1:s, :]
        r = N_SEG - 1 - s
        edge[5, r:r + 1, :] = edge[2, r + 1:r + 2, :] + edge[3, r + 1:r + 2, :] * edge[5, r + 1:r + 2, :]
    blocks = SCAN_CHUNK // N_SEG
    in_f = jnp.concatenate([edge[4]] * blocks, axis=0)
    in_b = jnp.concatenate([edge[5]] * blocks, axis=0)

    def out_chunk(c, carry):
        rows = pl.ds(pl.multiple_of(c * SCAN_CHUNK, SCAN_CHUNK), SCAN_CHUNK)
        y = (b_f[rows, :] + a_f[rows, :] * in_f) + (b_b[rows, :] + a_b[rows, :] * in_b)
        g = g_ref[rows, :]
        o_ref[rows, :] = y * (g * _sigmoid(g))
        return carry

    lax.fori_loop(0, n_rows // SCAN_CHUNK, out_chunk, 0, unroll=2)


def _rglru(xr, g, conv_w, conv_b, w_a, b_a, w_x, b_x, lam):
    batch, _, n_rows, _ = xr.shape
    w_cat = (0.5 * jnp.concatenate([w_a[0], w_x[0], w_a[1], w_x[1]], axis=-1)).astype(BF16)
    per_head = lambda v: v.reshape(RNN_HEADS, 1, RNN_HEAD_DIM)
    b_cat = 0.5 * jnp.concatenate([per_head(b_a[0]), per_head(b_x[0]), per_head(b_a[1]), per_head(b_x[1])],
                                  axis=-1)
    col = lambda rows: pl.BlockSpec((rows, RNN_HEAD_DIM), lambda b, h: (0, h))
    seq_col = pl.BlockSpec((None, None, n_rows, RNN_HEAD_DIM), lambda b, h: (b, h, 0, 0))
    scratch_rows = n_rows + (CONV_WIDTH - 1) * N_SEG
    return pl.pallas_call(
        _rglru_kernel,
        grid=(batch, RNN_HEADS),
        in_specs=[seq_col, seq_col, col(CONV_WIDTH), col(1),
                  pl.BlockSpec((None, RNN_HEAD_DIM, 4 * RNN_HEAD_DIM), lambda b, h: (h, 0, 0)),
                  pl.BlockSpec((None, 1, 4 * RNN_HEAD_DIM), lambda b, h: (h, 0, 0)),
                  col(2)],
        out_specs=seq_col,
        out_shape=jax.ShapeDtypeStruct((batch, RNN_HEADS, n_rows, RNN_HEAD_DIM), F32),
        scratch_shapes=[pltpu.VMEM((scratch_rows, LANES), F32)]
                       + [pltpu.VMEM((n_rows, LANES), F32)] * 4
                       + [pltpu.VMEM((6, N_SEG, LANES), F32)],
        compiler_params=_params("parallel", "parallel"),
        name="rglru",
    )(xr, g, conv_w, conv_b.reshape(1, -1), w_cat, b_cat, lam)


def _odd_out_kernel(y_ref, x_ref, gate_ref, w_ref, lng_ref, lnb_ref, o_ref):
    seg_len = y_ref.shape[1]
    head_rows = seg_len * SEGS_PER_TILE
    y_rows = y_ref.reshape(RNN_HEADS * head_rows, LANES)
    y = jnp.concatenate(
        [jnp.concatenate([y_rows[pl.ds(hd * head_rows + s, seg_len, stride=SEGS_PER_TILE), :].astype(BF16)
                          for hd in range(RNN_HEADS)], axis=1)
         for s in range(SEGS_PER_TILE)], axis=0)
    out = jnp.dot(y, w_ref[...], preferred_element_type=F32)
    z = DEEPNORM_ALPHA * x_ref[...] + gate_ref[...] * out
    o_ref[...] = _layer_norm_rows(z, lng_ref[...], lnb_ref[...])


def _odd_out_proj(y, x, gate, w_out, ln_g, ln_b):
    batch, seq, _ = x.shape
    seg_len = seq // N_SEG
    tile_rows = SEGS_PER_TILE * seg_len
    const = lambda shape: pl.BlockSpec(shape, lambda b, i: (0,) * len(shape))
    return pl.pallas_call(
        _odd_out_kernel,
        grid=(batch, seq // tile_rows),
        in_specs=[pl.BlockSpec((None, RNN_HEADS, seg_len, SEGS_PER_TILE, RNN_HEAD_DIM),
                               lambda b, i: (b, 0, 0, i, 0)),
                  pl.BlockSpec((None, tile_rows, D_MODEL), lambda b, i: (b, i, 0)),
                  pl.BlockSpec((None, 1, D_MODEL), lambda b, i: (b, 0, 0)),
                  const((RNN_WIDTH, D_MODEL)), const((1, D_MODEL)), const((1, D_MODEL))],
        out_specs=pl.BlockSpec((None, tile_rows, D_MODEL), lambda b, i: (b, i, 0)),
        out_shape=jax.ShapeDtypeStruct((batch, seq, D_MODEL), F32),
        compiler_params=_params("parallel", "parallel"),
        name="odd_out_proj",
    )(y, x, gate, w_out.astype(BF16), ln_g.reshape(1, -1), ln_b.reshape(1, -1))


def kernel(x, c, positions, ada_w, ada_b, ln_g, ln_b, ev_w_in, ev_w_out, ev_sink, ev_sg_ln_g, ev_sg_ln_b,
           ev_sg_w, ev_sg_b, od_w_in, od_conv_w, od_conv_b, od_w_a, od_b_a, od_w_x, od_b_x, od_lam, od_w_out):
    batch, seq, d_model = x.shape
    seg_len = seq // N_SEG
    assert d_model == D_MODEL and seq % ROW_TILE == 0 and seq == N_SEG * seg_len
    assert N_SEG % SEGS_PER_TILE == 0 and seq % SCAN_CHUNK == 0 and SCAN_CHUNK % N_SEG == 0
    assert ada_w.shape[0] == DEPTH == 2

    mod = _modulation(c, ada_w, ada_b)
    shift, scale, gate = (mod[:, :, i * D_MODEL:(i + 1) * D_MODEL].reshape(DEPTH, batch, 1, D_MODEL)
                          for i in range(3))

    q, k, v, su, sv, g = _even_in_proj(x, shift[0], scale[0], positions, ev_w_in[0])
    x1 = _even_mixer(q, k, v, su, sv, g, x, gate[0], ev_sink[0], ev_sg_ln_g[0], ev_sg_ln_b[0],
                     ev_sg_w[0], ev_sg_b[0], ev_w_out[0], ln_g[0], ln_b[0])

    xr, g1 = _odd_in_proj(x1, shift[1], scale[1], od_w_in[0])
    scan_rows = (batch, RNN_HEADS, seq, RNN_HEAD_DIM)
    y = _rglru(xr.reshape(scan_rows), g1.reshape(scan_rows), od_conv_w[0], od_conv_b[0], od_w_a[0],
               od_b_a[0], od_w_x[0], od_b_x[0], od_lam[0])
    return _odd_out_proj(y.reshape(batch, RNN_HEADS, seg_len, N_SEG, RNN_HEAD_DIM), x1, gate[1], od_w_out[0],
                         ln_g[1], ln_b[1])
```

```python
import math

import numpy as np
import jax
import jax.numpy as jnp
from jax import lax
from jax.experimental import pallas as pl
from jax.experimental.pallas import tpu as pltpu

D_MODEL = 1024
DEPTH = 2
HEAD_DIM = 64
N_Q_HEADS = 8
N_KV_HEADS = 2
Q_PER_KV = N_Q_HEADS // N_KV_HEADS
ATTN_WIDTH = N_Q_HEADS * HEAD_DIM
KV_WIDTH = N_KV_HEADS * HEAD_DIM
ATTN_BLOCK = 128
ROPE_DIM = HEAD_DIM // 4
ROPE_HALF = ROPE_DIM // 2
ROPE_THETA = 500000.0
N_SG_GROUPS = 8
SG_GROUP_DIM = 64
SG_WIDTH = N_SG_GROUPS * SG_GROUP_DIM
SG_CHUNK = 128
EVEN_GATE_WIDTH = ATTN_WIDTH + SG_WIDTH
EVEN_IN_WIDTH = ATTN_WIDTH + 2 * KV_WIDTH + 2 * SG_WIDTH + EVEN_GATE_WIDTH
RNN_WIDTH = D_MODEL
RNN_HEADS = 8
RNN_HEAD_DIM = RNN_WIDTH // RNN_HEADS
CONV_WIDTH = 4
CONV_LEFT = 2
RG_LRU_C = 8.0
DEEPNORM_ALPHA = (2 * DEPTH) ** 0.25
LN_EPS = 1e-5
NEG_INF = -1e30
LOG2_E = math.log2(math.e)
F32_TINY = float(np.finfo(np.float32).tiny)

LANES = 128
SUBLANES = 8
VMEM_LIMIT = 56 * 1024 * 1024

ROW_TILE = 512
N_SEG = 32
SEGS_PER_TILE = SUBLANES
SCAN_CHUNK = 256

BF16 = jnp.bfloat16
F32 = jnp.float32


def _sigmoid(x):
    return 0.5 * jnp.tanh(0.5 * x) + 0.5


def _layer_norm_rows(z, g, b):
    mu = jnp.mean(z, axis=-1, keepdims=True)
    d = z - mu
    var = jnp.mean(d * d, axis=-1, keepdims=True)
    return d * lax.rsqrt(var + LN_EPS) * g + b


def _params(*semantics):
    return pltpu.CompilerParams(dimension_semantics=semantics, vmem_limit_bytes=VMEM_LIMIT)


def _mod_kernel(c_ref, w_ref, b_ref, o_ref):
    c = c_ref[...]
    cond = (c * _sigmoid(c)).astype(BF16)
    o_ref[...] = jnp.dot(cond, w_ref[...].astype(BF16), preferred_element_type=F32) + b_ref[...]


def _modulation(c, ada_w, ada_b):
    batch = c.shape[0]
    rows = -(-batch // SUBLANES) * SUBLANES
    c_pad = jnp.pad(c, ((0, rows - batch), (0, 0)))
    n_col = 3 * D_MODEL // D_MODEL
    out = pl.pallas_call(
        _mod_kernel,
        grid=(DEPTH, n_col),
        in_specs=[
            pl.BlockSpec((rows, D_MODEL), lambda l, j: (0, 0)),
            pl.BlockSpec((None, D_MODEL, D_MODEL), lambda l, j: (l, 0, j)),
            pl.BlockSpec((None, 1, D_MODEL), lambda l, j: (l, 0, j)),
        ],
        out_specs=pl.BlockSpec((None, rows, D_MODEL), lambda l, j: (l, 0, j)),
        out_shape=jax.ShapeDtypeStruct((DEPTH, rows, 3 * D_MODEL), F32),
        compiler_params=_params("parallel", "parallel"),
        name="adaln_mod",
    )(c_pad, ada_w, ada_b.reshape(DEPTH, 1, 3 * D_MODEL))
    return out[:, :batch]


def _even_in_kernel(x_ref, shift_ref, scale_ref, pos_ref, freq_ref, sign_ref, w_ref,
                    q_ref, k_ref, v_ref, su_ref, sv_ref, g_ref):
    lane = lax.broadcasted_iota(jnp.int32, (1, LANES), 1)
    first_half = (lane % HEAD_DIM) < ROPE_HALF
    diagonal = lax.broadcasted_iota(jnp.int32, (LANES, LANES), 0) == lax.broadcasted_iota(jnp.int32, (LANES, LANES), 1)
    q_scale = HEAD_DIM ** -0.5 * LOG2_E

    half_rows = x_ref.shape[0] // 2
    for r in range(2):
        rows = slice(r * half_rows, (r + 1) * half_rows)
        h = (x_ref[rows, :] * (1.0 + scale_ref[...]) + shift_ref[...]).astype(BF16)
        p = jnp.dot(h, w_ref[...], preferred_element_type=F32)

        pos_cols = []
        for j in range(r * half_rows // LANES, (r + 1) * half_rows // LANES):
            pos_row = pos_ref[:, j * LANES:(j + 1) * LANES].astype(F32)
            pos_cols.append(jnp.sum(jnp.where(diagonal, pos_row, 0.0), axis=1, keepdims=True))
        ang = jnp.concatenate(pos_cols, axis=0) * freq_ref[...]
        cs = jnp.cos(ang)
        sn = jnp.sin(ang) * sign_ref[...]

        def rotary(t):
            partner = jnp.where(first_half, pltpu.roll(t, LANES - ROPE_HALF, 1), pltpu.roll(t, ROPE_HALF, 1))
            return t * cs + partner * sn

        for j in range(ATTN_WIDTH // LANES):
            q_ref[rows, j * LANES:(j + 1) * LANES] = (
                rotary(p[:, j * LANES:(j + 1) * LANES]) * q_scale).astype(BF16)
        o = ATTN_WIDTH
        k_ref[rows, :] = rotary(p[:, o:o + KV_WIDTH]).astype(BF16)
        o += KV_WIDTH
        v_ref[rows, :] = p[:, o:o + KV_WIDTH].astype(BF16)
        o += KV_WIDTH
        su_ref[rows, :] = p[:, o:o + SG_WIDTH].astype(BF16)
        o += SG_WIDTH
        sv_ref[rows, :] = p[:, o:o + SG_WIDTH].astype(BF16)
        o += SG_WIDTH
        g_ref[rows, :] = p[:, o:o + EVEN_GATE_WIDTH].astype(BF16)


def _even_in_proj(x, shift, scale, positions, w_in):
    batch, seq, _ = x.shape
    d = np.arange(LANES) % HEAD_DIM
    inv_freq = np.power(np.float64(ROPE_THETA), -np.arange(ROPE_HALF) / ROPE_HALF)
    freq_lane = jnp.asarray(np.where(d < ROPE_DIM, inv_freq[d % ROPE_HALF], 0.0), dtype=F32).reshape(1, LANES)
    sign_lane = jnp.asarray(np.where(d < ROPE_HALF, -1.0, np.where(d < ROPE_DIM, 1.0, 0.0)),
                            dtype=F32).reshape(1, LANES)
    row = lambda width: pl.BlockSpec((None, ROW_TILE, width), lambda b, i: (b, i, 0))
    vec = pl.BlockSpec((None, 1, D_MODEL), lambda b, i: (b, 0, 0))
    const = lambda shape: pl.BlockSpec(shape, lambda b, i: (0,) * len(shape))
    widths = (ATTN_WIDTH, KV_WIDTH, KV_WIDTH, SG_WIDTH, SG_WIDTH, EVEN_GATE_WIDTH)
    return pl.pallas_call(
        _even_in_kernel,
        grid=(batch, seq // ROW_TILE),
        in_specs=[row(D_MODEL), vec, vec, pl.BlockSpec((None, 1, ROW_TILE), lambda b, i: (b, 0, i)),
                  const((1, LANES)), const((1, LANES)), const((D_MODEL, EVEN_IN_WIDTH))],
        out_specs=[row(w) for w in widths],
        out_shape=[jax.ShapeDtypeStruct((batch, seq, w), BF16) for w in widths],
        compiler_params=_params("parallel", "parallel"),
        name="even_in_proj",
    )(x, shift, scale, positions.reshape(batch, 1, seq), freq_lane, sign_lane, w_in.astype(BF16))


def _even_mix_kernel(sink_ref, q_ref, k_ref, v_ref, su_ref, sv_ref, g_ref, x_ref, gate_ref,
                     avg_ref, avg2_ref, sgg_ref, sgb_ref, sgw_ref, sgbias_ref, wout_ref, lng_ref, lnb_ref,
                     o_ref, kdup, vsplit, y_scr):
    seq = k_ref.shape[0]
    n_blocks = seq // ATTN_BLOCK
    sub_blocks = ROW_TILE // ATTN_BLOCK
    tile = pl.program_id(1)
    lane = lax.broadcasted_iota(jnp.int32, (1, LANES), 1)
    left = lane < HEAD_DIM

    @pl.when(tile == 0)
    def _():
        def build(c, carry):
            rows = pl.ds(pl.multiple_of(c * ROW_TILE, ROW_TILE), ROW_TILE)
            k = k_ref[rows, :].astype(F32)
            k_sw = pltpu.roll(k, HEAD_DIM, 1)
            kdup[rows, 0:LANES] = jnp.where(left, k, k_sw).astype(BF16)
            kdup[rows, LANES:2 * LANES] = jnp.where(left, k_sw, k).astype(BF16)
            v = v_ref[rows, :].astype(F32)
            v_sw = pltpu.roll(v, HEAD_DIM, 1)
            vsplit[rows, 0 * LANES:1 * LANES] = jnp.where(left, v, 0.0).astype(BF16)
            vsplit[rows, 1 * LANES:2 * LANES] = jnp.where(left, 0.0, v_sw).astype(BF16)
            vsplit[rows, 2 * LANES:3 * LANES] = jnp.where(left, v_sw, 0.0).astype(BF16)
            vsplit[rows, 3 * LANES:4 * LANES] = jnp.where(left, 0.0, v).astype(BF16)
            return carry
        lax.fori_loop(0, seq // ROW_TILE, build, 0)

    qi = lax.broadcasted_iota(jnp.int32, (ATTN_BLOCK, ATTN_BLOCK), 0)
    kj = lax.broadcasted_iota(jnp.int32, (ATTN_BLOCK, ATTN_BLOCK), 1)
    ones_left = jnp.broadcast_to(jnp.where(left, 1.0, 0.0).astype(BF16), (3 * ATTN_BLOCK, LANES))
    ones_right = jnp.broadcast_to(jnp.where(left, 0.0, 1.0).astype(BF16), (3 * ATTN_BLOCK, LANES))
    avg = avg_ref[...]
    sink_cols = [jnp.concatenate([jnp.full((ATTN_BLOCK, LANES), sink_ref[h] * LOG2_E, F32)
                                  for h in (4 * hk, 4 * hk + 2, 4 * hk + 1, 4 * hk + 3)], axis=0)
                 for hk in range(N_KV_HEADS)]

    def sub_block(j, carry):
        blk = tile * sub_blocks + j
        prev = jnp.maximum(blk - 1, 0)
        nxt = jnp.minimum(blk + 1, n_blocks - 1)
        rows = pl.ds(pl.multiple_of(j * ATTN_BLOCK, ATTN_BLOCK), ATTN_BLOCK)
        bias_prev = jnp.where((kj >= qi) & (blk > 0), 0.0, NEG_INF)
        bias_next = jnp.where((kj <= qi) & (blk < n_blocks - 1), 0.0, NEG_INF)
        bias_prev = jnp.concatenate([bias_prev] * Q_PER_KV, axis=0)
        bias_next = jnp.concatenate([bias_next] * Q_PER_KV, axis=0)

        def band(ref):
            return jnp.concatenate(
                [ref[pl.ds(pl.multiple_of(b * ATTN_BLOCK, ATTN_BLOCK), ATTN_BLOCK), :] for b in (prev, blk, nxt)],
                axis=0)

        k_band = band(kdup)
        v_band = band(vsplit)
        n_sg = SG_WIDTH // LANES
        sg_cols = [slice(c * LANES, (c + 1) * LANES) for c in range(n_sg)]

        def silu_gate(cols):
            g_c = g_ref[rows, cols].astype(F32)
            return g_c * _sigmoid(g_c)

        scores = []
        for hk in range(N_KV_HEADS):
            q_cols = [q_ref[rows, (2 * hk + c) * LANES:(2 * hk + c + 1) * LANES] for c in range(2)]
            zero = jnp.zeros_like(q_cols[0])
            lhs = jnp.concatenate([jnp.where(left, qc, zero) for qc in q_cols]
                                  + [jnp.where(left, zero, qc) for qc in q_cols], axis=0)
            scores.append(lax.dot_general(lhs, k_band[:, hk * LANES:(hk + 1) * LANES], (((1,), (1,)), ((), ())),
                                          preferred_element_type=F32))
        sv_cols = [sv_ref[rows, cols] for cols in sg_cols]
        means = [jnp.dot(v_c, avg, preferred_element_type=F32) for v_c in sv_cols]

        probs, row_max = [], []
        for hk in range(N_KV_HEADS):
            s = scores[hk]
            s0 = s[:, 0:ATTN_BLOCK] + bias_prev
            s1 = s[:, ATTN_BLOCK:2 * ATTN_BLOCK]
            s2 = s[:, 2 * ATTN_BLOCK:3 * ATTN_BLOCK] + bias_next
            m = jnp.max(jnp.maximum(jnp.maximum(s0, s1), s2), axis=-1, keepdims=True)
            m = jnp.maximum(jnp.broadcast_to(m, sink_cols[hk].shape), sink_cols[hk])
            probs.append(jnp.concatenate([jnp.exp2(t - m).astype(BF16) for t in (s0, s1, s2)], axis=1))
            row_max.append(m)
        devs, sq_parts = [], []
        for v_c, mean in zip(sv_cols, means):
            dev = v_c.astype(F32) - mean
            sq = dev * dev
            sq_hi = sq.astype(BF16)
            sq_lo = (sq - sq_hi.astype(F32)).astype(BF16)
            devs.append(dev)
            sq_parts.append(jnp.concatenate([sq_hi, sq_lo], axis=1))

        pv = []
        for hk in range(N_KV_HEADS):
            p = probs[hk]
            lhs = jnp.concatenate(
                [jnp.concatenate([p[c * ATTN_BLOCK:(c + 1) * ATTN_BLOCK], p[(2 + c) * ATTN_BLOCK:(3 + c) * ATTN_BLOCK]],
                                 axis=1) for c in range(2)], axis=0)
            rhs = jnp.concatenate(
                [jnp.concatenate([v_band[:, (2 * hk) * LANES:(2 * hk + 1) * LANES], ones_left], axis=1),
                 jnp.concatenate([v_band[:, (2 * hk + 1) * LANES:(2 * hk + 2) * LANES], ones_right], axis=1)],
                axis=0)
            pv.append(jnp.dot(lhs, rhs, preferred_element_type=F32))
        variances = [jnp.dot(sq, avg2_ref[...], preferred_element_type=F32) for sq in sq_parts]

        for hk in range(N_KV_HEADS):
            for c in range(2):
                even, odd = slice(c * ATTN_BLOCK, (c + 1) * ATTN_BLOCK), slice((2 + c) * ATTN_BLOCK, (3 + c) * ATTN_BLOCK)
                sink_term = jnp.exp2(jnp.where(left, sink_cols[hk][even] - row_max[hk][even],
                                               sink_cols[hk][odd] - row_max[hk][odd]))
                o = pv[hk][c * ATTN_BLOCK:(c + 1) * ATTN_BLOCK]
                pair = o[:, 0:LANES] * (1.0 / (o[:, LANES:2 * LANES] + sink_term))
                cols = slice((2 * hk + c) * LANES, (2 * hk + c + 1) * LANES)
                y_scr[rows, cols] = (pair * silu_gate(cols)).astype(BF16)

        for c, cols in enumerate(sg_cols):
            vn = devs[c] * lax.rsqrt(variances[c] + LN_EPS) * sgg_ref[:, cols] + sgb_ref[:, cols]
            stacked = jnp.concatenate([jnp.where(left, vn, 0.0), jnp.where(left, 0.0, vn)],
                                      axis=0).astype(BF16)
            mixed = jnp.dot(sgw_ref[c], stacked, preferred_element_type=F32) + sgbias_ref[:, cols]
            y_sg = su_ref[rows, cols].astype(F32) * mixed
            gc = slice(ATTN_WIDTH + c * LANES, ATTN_WIDTH + (c + 1) * LANES)
            y_scr[rows, gc] = (y_sg * silu_gate(gc)).astype(BF16)
        return carry

    lax.fori_loop(0, sub_blocks, sub_block, 0, unroll=True)

    for r in range(2):
        half_rows = slice(r * (ROW_TILE // 2), (r + 1) * (ROW_TILE // 2))
        out = jnp.dot(y_scr[half_rows, :], wout_ref[...], preferred_element_type=F32)
        z = DEEPNORM_ALPHA * x_ref[half_rows, :] + gate_ref[...] * out
        o_ref[half_rows, :] = _layer_norm_rows(z, lng_ref[...], lnb_ref[...])


def _even_mixer(q, k, v, su, sv, g, x, gate, sink, sg_ln_g, sg_ln_b, sg_w, sg_b, w_out, ln_g, ln_b):
    batch, seq, _ = x.shape
    lane = np.arange(LANES)
    avg = jnp.asarray((lane[:, None] // SG_GROUP_DIM == lane[None, :] // SG_GROUP_DIM) / SG_GROUP_DIM, dtype=BF16)
    w_pairs = sg_w.reshape(N_SG_GROUPS // 2, 2, SG_CHUNK, SG_CHUNK).transpose(0, 2, 1, 3)
    w_pairs = w_pairs.reshape(N_SG_GROUPS // 2, SG_CHUNK, 2 * SG_CHUNK).astype(BF16)
    bias_full = jnp.repeat(sg_b.T, SG_GROUP_DIM, axis=1)
    row = lambda width: pl.BlockSpec((None, ROW_TILE, width), lambda b, i: (b, i, 0))
    full_seq = pl.BlockSpec((None, seq, KV_WIDTH), lambda b, i: (b, 0, 0))
    const = lambda shape: pl.BlockSpec(shape, lambda b, i: (0,) * len(shape))
    return pl.pallas_call(
        _even_mix_kernel,
        grid=(batch, seq // ROW_TILE),
        in_specs=[
            pl.BlockSpec(memory_space=pltpu.SMEM),
            row(ATTN_WIDTH), full_seq, full_seq, row(SG_WIDTH), row(SG_WIDTH), row(EVEN_GATE_WIDTH),
            row(D_MODEL), pl.BlockSpec((None, 1, D_MODEL), lambda b, i: (b, 0, 0)),
            const((LANES, LANES)), const((2 * LANES, LANES)), const((1, SG_WIDTH)), const((1, SG_WIDTH)),
            const((N_SG_GROUPS // 2, SG_CHUNK, 2 * SG_CHUNK)), const((SG_CHUNK, SG_WIDTH)),
            const((EVEN_GATE_WIDTH, D_MODEL)), const((1, D_MODEL)), const((1, D_MODEL)),
        ],
        out_specs=row(D_MODEL),
        out_shape=jax.ShapeDtypeStruct((batch, seq, D_MODEL), F32),
        scratch_shapes=[pltpu.VMEM((seq, 2 * KV_WIDTH), BF16), pltpu.VMEM((seq, 4 * KV_WIDTH), BF16),
                        pltpu.VMEM((ROW_TILE, EVEN_GATE_WIDTH), BF16)],
        compiler_params=_params("arbitrary", "arbitrary"),
        name="even_mixer",
    )(sink, q, k, v, su, sv, g, x, gate, avg, jnp.concatenate([avg, avg], axis=0), sg_ln_g.reshape(1, -1), sg_ln_b.reshape(1, -1),
      w_pairs, bias_full, w_out.astype(BF16), ln_g.reshape(1, -1), ln_b.reshape(1, -1))


def _odd_in_kernel(x_ref, shift_ref, scale_ref, w_ref, xr_ref, g_ref):
    seg_len = xr_ref.shape[1]
    head_rows = seg_len * SEGS_PER_TILE
    out_rows = [ref.reshape(RNN_HEADS * head_rows, LANES) for ref in (xr_ref, g_ref)]
    halves = 2
    for r in range(halves):
        first = r * (SEGS_PER_TILE // halves)
        rows = slice(first * seg_len, (first + SEGS_PER_TILE // halves) * seg_len)
        h = (x_ref[rows, :] * (1.0 + scale_ref[...]) + shift_ref[...]).astype(BF16)
        p = jnp.dot(h, w_ref[...], preferred_element_type=F32)
        for half, o_rows in enumerate(out_rows):
            for s in range(SEGS_PER_TILE // halves):
                for hd in range(RNN_HEADS):
                    col = half * RNN_WIDTH + hd * LANES
                    o_rows[pl.ds(hd * head_rows + first + s, seg_len, stride=SEGS_PER_TILE), :] = (
                        p[s * seg_len:(s + 1) * seg_len, col:col + LANES])


def _odd_in_proj(x, shift, scale, w_in):
    batch, seq, _ = x.shape
    seg_len = seq // N_SEG
    tile_rows = SEGS_PER_TILE * seg_len
    vec = pl.BlockSpec((None, 1, D_MODEL), lambda b, i: (b, 0, 0))
    out_spec = pl.BlockSpec((None, RNN_HEADS, seg_len, SEGS_PER_TILE, RNN_HEAD_DIM), lambda b, i: (b, 0, 0, i, 0))
    out_shape = jax.ShapeDtypeStruct((batch, RNN_HEADS, seg_len, N_SEG, RNN_HEAD_DIM), F32)
    return pl.pallas_call(
        _odd_in_kernel,
        grid=(batch, seq // tile_rows),
        in_specs=[pl.BlockSpec((None, tile_rows, D_MODEL), lambda b, i: (b, i, 0)), vec, vec,
                  pl.BlockSpec((D_MODEL, 2 * RNN_WIDTH), lambda b, i: (0, 0))],
        out_specs=[out_spec, out_spec],
        out_shape=[out_shape, out_shape],
        compiler_params=_params("parallel", "parallel"),
        name="odd_in_proj",
    )(x, shift, scale, w_in.astype(BF16))


def _rglru_kernel(x_ref, g_ref, convw_ref, convb_ref, wcat_ref, bcat_ref, lam_ref, o_ref,
                  xpad, a_f, b_f, a_b, b_b, edge):
    n_rows = x_ref.shape[0]
    seg_len = n_rows // N_SEG
    pad_rows = CONV_LEFT * N_SEG
    seg = lax.broadcasted_iota(jnp.int32, (N_SEG, LANES), 0)

    def x_rows(start, size):
        return x_ref[pl.ds(start, size), :]

    def from_prev_segment(blk):
        return jnp.where(seg == 0, 0.0, pltpu.roll(blk, 1, 0))

    def from_next_segment(blk):
        return jnp.where(seg == N_SEG - 1, 0.0, pltpu.roll(blk, N_SEG - 1, 0))

    for c in range(n_rows // SCAN_CHUNK):
        xpad[pl.ds(pad_rows + c * SCAN_CHUNK, SCAN_CHUNK), :] = x_rows(c * SCAN_CHUNK, SCAN_CHUNK)
    for t in range(CONV_LEFT):
        xpad[pl.ds(t * N_SEG, N_SEG), :] = from_prev_segment(x_rows((seg_len - CONV_LEFT + t) * N_SEG, N_SEG))
    for t in range(CONV_WIDTH - 1 - CONV_LEFT):
        xpad[pl.ds(pad_rows + n_rows + t * N_SEG, N_SEG), :] = from_next_segment(x_rows(t * N_SEG, N_SEG))

    lam = lam_ref[...]
    softplus_neg = jnp.maximum(-lam, 0.0) + jnp.log1p(jnp.exp(-jnp.abs(lam)))
    half_rate = (0.5 * RG_LRU_C) * softplus_neg

    def gates_chunk(c, carry):
        base = pl.multiple_of(c * SCAN_CHUNK, SCAN_CHUNK)
        xr = convb_ref[...] + sum(convw_ref[k:k + 1, :] * xpad[pl.ds(base + k * N_SEG, SCAN_CHUNK), :]
                                  for k in range(CONV_WIDTH))
        th = jnp.tanh(jnp.dot(xr.astype(BF16), wcat_ref[...], preferred_element_type=F32) + bcat_ref[...])
        half_x = 0.5 * xr
        for d, (a_ref, b_ref) in enumerate(((a_f, b_f), (a_b, b_b))):
            o = 2 * d * LANES
            neg_log_a = (th[:, o:o + LANES] + 1.0) * half_rate[d:d + 1, :]
            a = jnp.exp2(neg_log_a * (-LOG2_E))
            one_minus_a2 = jnp.tanh(neg_log_a) * (a * a + 1.0)
            root = one_minus_a2 * lax.rsqrt(jnp.maximum(one_minus_a2, F32_TINY))
            a_ref[pl.ds(base, SCAN_CHUNK), :] = a
            b_ref[pl.ds(base, SCAN_CHUNK), :] = root * ((th[:, o + LANES:o + 2 * LANES] + 1.0) * half_x)
        return carry

    lax.fori_loop(0, n_rows // SCAN_CHUNK, gates_chunk, 0, unroll=4)

    def scan_step(t, carry):
        h_f, p_f, h_b, p_b = carry
        rf = pl.ds(pl.multiple_of(t * N_SEG, N_SEG), N_SEG)
        rb = pl.ds(pl.multiple_of((seg_len - 1 - t) * N_SEG, N_SEG), N_SEG)
        a = a_f[rf, :]
        h_f = a * h_f + b_f[rf, :]
        p_f = a * p_f
        b_f[rf, :] = h_f
        a_f[rf, :] = p_f
        a = a_b[rb, :]
        h_b = a * h_b + b_b[rb, :]
        p_b = a * p_b
        b_b[rb, :] = h_b
        a_b[rb, :] = p_b
        return h_f, p_f, h_b, p_b

    zeros = jnp.zeros((N_SEG, LANES), F32)
    ones = jnp.ones((N_SEG, LANES), F32)
    h_f, p_f, h_b, p_b = lax.fori_loop(0, seg_len, scan_step, (zeros, ones, zeros, ones), unroll=4)

    edge[0] = h_f
    edge[1] = p_f
    edge[2] = h_b
    edge[3] = p_b
    edge[4, 0:1, :] = jnp.zeros((1, LANES), F32)
    edge[5, N_SEG - 1:N_SEG, :] = jnp.zeros((1, LANES), F32)
    for s in range(1, N_SEG):
        edge[4, s:s + 1, :] = edge[0, s - 1:s, :] + edge[1, s - 1:s, :] * edge[4, s - 1:s, :]
        r = N_SEG - 1 - s
        edge[5, r:r + 1, :] = edge[2, r + 1:r + 2, :] + edge[3, r + 1:r + 2, :] * edge[5, r + 1:r + 2, :]
    blocks = SCAN_CHUNK // N_SEG
    in_f = jnp.concatenate([edge[4]] * blocks, axis=0)
    in_b = jnp.concatenate([edge[5]] * blocks, axis=0)

    def out_chunk(c, carry):
        rows = pl.ds(pl.multiple_of(c * SCAN_CHUNK, SCAN_CHUNK), SCAN_CHUNK)
        y = (b_f[rows, :] + a_f[rows, :] * in_f) + (b_b[rows, :] + a_b[rows, :] * in_b)
        g = g_ref[rows, :]
        o_ref[rows, :] = y * (g * _sigmoid(g))
        return carry

    lax.fori_loop(0, n_rows // SCAN_CHUNK, out_chunk, 0, unroll=2)


def _rglru(xr, g, conv_w, conv_b, w_a, b_a, w_x, b_x, lam):
    batch, _, n_rows, _ = xr.shape
    w_cat = (0.5 * jnp.concatenate([w_a[0], w_x[0], w_a[1], w_x[1]], axis=-1)).astype(BF16)
    per_head = lambda v: v.reshape(RNN_HEADS, 1, RNN_HEAD_DIM)
    b_cat = 0.5 * jnp.concatenate([per_head(b_a[0]), per_head(b_x[0]), per_head(b_a[1]), per_head(b_x[1])],
                                  axis=-1)
    col = lambda rows: pl.BlockSpec((rows, RNN_HEAD_DIM), lambda b, h: (0, h))
    seq_col = pl.BlockSpec((None, None, n_rows, RNN_HEAD_DIM), lambda b, h: (b, h, 0, 0))
    scratch_rows = n_rows + (CONV_WIDTH - 1) * N_SEG
    return pl.pallas_call(
        _rglru_kernel,
        grid=(batch, RNN_HEADS),
        in_specs=[seq_col, seq_col, col(CONV_WIDTH), col(1),
                  pl.BlockSpec((None, RNN_HEAD_DIM, 4 * RNN_HEAD_DIM), lambda b, h: (h, 0, 0)),
                  pl.BlockSpec((None, 1, 4 * RNN_HEAD_DIM), lambda b, h: (h, 0, 0)),
                  col(2)],
        out_specs=seq_col,
        out_shape=jax.ShapeDtypeStruct((batch, RNN_HEADS, n_rows, RNN_HEAD_DIM), F32),
        scratch_shapes=[pltpu.VMEM((scratch_rows, LANES), F32)]
                       + [pltpu.VMEM((n_rows, LANES), F32)] * 4
                       + [pltpu.VMEM((6, N_SEG, LANES), F32)],
        compiler_params=_params("parallel", "parallel"),
        name="rglru",
    )(xr, g, conv_w, conv_b.reshape(1, -1), w_cat, b_cat, lam)


def _odd_out_kernel(y_ref, x_ref, gate_ref, w_ref, lng_ref, lnb_ref, o_ref):
    seg_len = y_ref.shape[1]
    head_rows = seg_len * SEGS_PER_TILE
    y_rows = y_ref.reshape(RNN_HEADS * head_rows, LANES)
    y = jnp.concatenate(
        [jnp.concatenate([y_rows[pl.ds(hd * head_rows + s, seg_len, stride=SEGS_PER_TILE), :].astype(BF16)
                          for hd in range(RNN_HEADS)], axis=1)
         for s in range(SEGS_PER_TILE)], axis=0)
    out = jnp.dot(y, w_ref[...], preferred_element_type=F32)
    z = DEEPNORM_ALPHA * x_ref[...] + gate_ref[...] * out
    o_ref[...] = _layer_norm_rows(z, lng_ref[...], lnb_ref[...])


def _odd_out_proj(y, x, gate, w_out, ln_g, ln_b):
    batch, seq, _ = x.shape
    seg_len = seq // N_SEG
    tile_rows = SEGS_PER_TILE * seg_len
    const = lambda shape: pl.BlockSpec(shape, lambda b, i: (0,) * len(shape))
    return pl.pallas_call(
        _odd_out_kernel,
        grid=(batch, seq // tile_rows),
        in_specs=[pl.BlockSpec((None, RNN_HEADS, seg_len, SEGS_PER_TILE, RNN_HEAD_DIM),
                               lambda b, i: (b, 0, 0, i, 0)),
                  pl.BlockSpec((None, tile_rows, D_MODEL), lambda b, i: (b, i, 0)),
                  pl.BlockSpec((None, 1, D_MODEL), lambda b, i: (b, 0, 0)),
                  const((RNN_WIDTH, D_MODEL)), const((1, D_MODEL)), const((1, D_MODEL))],
        out_specs=pl.BlockSpec((None, tile_rows, D_MODEL), lambda b, i: (b, i, 0)),
        out_shape=jax.ShapeDtypeStruct((batch, seq, D_MODEL), F32),
        compiler_params=_params("parallel", "parallel"),
        name="odd_out_proj",
    )(y, x, gate, w_out.astype(BF16), ln_g.reshape(1, -1), ln_b.reshape(1, -1))


def kernel(x, c, positions, ada_w, ada_b, ln_g, ln_b, ev_w_in, ev_w_out, ev_sink, ev_sg_ln_g, ev_sg_ln_b,
           ev_sg_w, ev_sg_b, od_w_in, od_conv_w, od_conv_b, od_w_a, od_b_a, od_w_x, od_b_x, od_lam, od_w_out):
    batch, seq, d_model = x.shape
    seg_len = seq // N_SEG
    assert d_model == D_MODEL and seq % ROW_TILE == 0 and seq == N_SEG * seg_len
    assert N_SEG % SEGS_PER_TILE == 0 and seq % SCAN_CHUNK == 0 and SCAN_CHUNK % N_SEG == 0
    assert ada_w.shape[0] == DEPTH == 2

    mod = _modulation(c, ada_w, ada_b)
    shift, scale, gate = (mod[:, :, i * D_MODEL:(i + 1) * D_MODEL].reshape(DEPTH, batch, 1, D_MODEL)
                          for i in range(3))

    q, k, v, su, sv, g = _even_in_proj(x, shift[0], scale[0], positions, ev_w_in[0])
    x1 = _even_mixer(q, k, v, su, sv, g, x, gate[0], ev_sink[0], ev_sg_ln_g[0], ev_sg_ln_b[0],
                     ev_sg_w[0], ev_sg_b[0], ev_w_out[0], ln_g[0], ln_b[0])

    xr, g1 = _odd_in_proj(x1, shift[1], scale[1], od_w_in[0])
    scan_rows = (batch, RNN_HEADS, seq, RNN_HEAD_DIM)
    y = _rglru(xr.reshape(scan_rows), g1.reshape(scan_rows), od_conv_w[0], od_conv_b[0], od_w_a[0],
               od_b_a[0], od_w_x[0], od_b_x[0], od_lam[0])
    return _odd_out_proj(y.reshape(batch, RNN_HEADS, seg_len, N_SEG, RNN_HEAD_DIM), x1, gate[1], od_w_out[0],
                         ln_g[1], ln_b[1])
```

```python
import math

import numpy as np
import jax
import jax.numpy as jnp
from jax import lax
from jax.experimental import pallas as pl
from jax.experimental.pallas import tpu as pltpu

D_MODEL = 1024
DEPTH = 2
HEAD_DIM = 64
N_Q_HEADS = 8
N_KV_HEADS = 2
Q_PER_KV = N_Q_HEADS // N_KV_HEADS
ATTN_WIDTH = N_Q_HEADS * HEAD_DIM
KV_WIDTH = N_KV_HEADS * HEAD_DIM
ATTN_BLOCK = 128
ROPE_DIM = HEAD_DIM // 4
ROPE_HALF = ROPE_DIM // 2
ROPE_THETA = 500000.0
N_SG_GROUPS = 8
SG_GROUP_DIM = 64
SG_WIDTH = N_SG_GROUPS * SG_GROUP_DIM
SG_CHUNK = 128
EVEN_GATE_WIDTH = ATTN_WIDTH + SG_WIDTH
EVEN_IN_WIDTH = ATTN_WIDTH + 2 * KV_WIDTH + 2 * SG_WIDTH + EVEN_GATE_WIDTH
RNN_WIDTH = D_MODEL
RNN_HEADS = 8
RNN_HEAD_DIM = RNN_WIDTH // RNN_HEADS
CONV_WIDTH = 4
CONV_LEFT = 2
RG_LRU_C = 8.0
DEEPNORM_ALPHA = (2 * DEPTH) ** 0.25
LN_EPS = 1e-5
NEG_INF = -1e30
LOG2_E = math.log2(math.e)
F32_TINY = float(np.finfo(np.float32).tiny)

LANES = 128
SUBLANES = 8
VMEM_LIMIT = 56 * 1024 * 1024

ROW_TILE = 512
N_SEG = 32
SEGS_PER_TILE = SUBLANES
SCAN_CHUNK = 256

BF16 = jnp.bfloat16
F32 = jnp.float32


def _sigmoid(x):
    return 0.5 * jnp.tanh(0.5 * x) + 0.5


def _layer_norm_rows(z, g, b):
    mu = jnp.mean(z, axis=-1, keepdims=True)
    d = z - mu
    var = jnp.mean(d * d, axis=-1, keepdims=True)
    return d * lax.rsqrt(var + LN_EPS) * g + b


def _params(*semantics):
    return pltpu.CompilerParams(dimension_semantics=semantics, vmem_limit_bytes=VMEM_LIMIT)


def _mod_kernel(c_ref, w_ref, b_ref, o_ref):
    c = c_ref[...]
    cond = (c * _sigmoid(c)).astype(BF16)
    o_ref[...] = jnp.dot(cond, w_ref[...].astype(BF16), preferred_element_type=F32) + b_ref[...]


def _modulation(c, ada_w, ada_b):
    batch = c.shape[0]
    rows = -(-batch // SUBLANES) * SUBLANES
    c_pad = jnp.pad(c, ((0, rows - batch), (0, 0)))
    n_col = 3 * D_MODEL // D_MODEL
    out = pl.pallas_call(
        _mod_kernel,
        grid=(DEPTH, n_col),
        in_specs=[
            pl.BlockSpec((rows, D_MODEL), lambda l, j: (0, 0)),
            pl.BlockSpec((None, D_MODEL, D_MODEL), lambda l, j: (l, 0, j)),
            pl.BlockSpec((None, 1, D_MODEL), lambda l, j: (l, 0, j)),
        ],
        out_specs=pl.BlockSpec((None, rows, D_MODEL), lambda l, j: (l, 0, j)),
        out_shape=jax.ShapeDtypeStruct((DEPTH, rows, 3 * D_MODEL), F32),
        compiler_params=_params("parallel", "parallel"),
        name="adaln_mod",
    )(c_pad, ada_w, ada_b.reshape(DEPTH, 1, 3 * D_MODEL))
    return out[:, :batch]


def _even_in_kernel(x_ref, shift_ref, scale_ref, pos_ref, freq_ref, sign_ref, w_ref,
                    q_ref, k_ref, v_ref, su_ref, sv_ref, g_ref):
    lane = lax.broadcasted_iota(jnp.int32, (1, LANES), 1)
    first_half = (lane % HEAD_DIM) < ROPE_HALF
    diagonal = lax.broadcasted_iota(jnp.int32, (LANES, LANES), 0) == lax.broadcasted_iota(jnp.int32, (LANES, LANES), 1)
    q_scale = HEAD_DIM ** -0.5 * LOG2_E

    half_rows = x_ref.shape[0] // 2
    for r in range(2):
        rows = slice(r * half_rows, (r + 1) * half_rows)
        h = (x_ref[rows, :] * (1.0 + scale_ref[...]) + shift_ref[...]).astype(BF16)
        p = jnp.dot(h, w_ref[...], preferred_element_type=F32)

        pos_cols = []
        for j in range(r * half_rows // LANES, (r + 1) * half_rows // LANES):
            pos_row = pos_ref[:, j * LANES:(j + 1) * LANES].astype(F32)
            pos_cols.append(jnp.sum(jnp.where(diagonal, pos_row, 0.0), axis=1, keepdims=True))
        ang = jnp.concatenate(pos_cols, axis=0) * freq_ref[...]
        cs = jnp.cos(ang)
        sn = jnp.sin(ang) * sign_ref[...]

        def rotary(t):
            partner = jnp.where(first_half, pltpu.roll(t, LANES - ROPE_HALF, 1), pltpu.roll(t, ROPE_HALF, 1))
            return t * cs + partner * sn

        for j in range(ATTN_WIDTH // LANES):
            q_ref[rows, j * LANES:(j + 1) * LANES] = (
                rotary(p[:, j * LANES:(j + 1) * LANES]) * q_scale).astype(BF16)
        o = ATTN_WIDTH
        k_ref[rows, :] = rotary(p[:, o:o + KV_WIDTH]).astype(BF16)
        o += KV_WIDTH
        v_ref[rows, :] = p[:, o:o + KV_WIDTH].astype(BF16)
        o += KV_WIDTH
        su_ref[rows, :] = p[:, o:o + SG_WIDTH].astype(BF16)
        o += SG_WIDTH
        sv_ref[rows, :] = p[:, o:o + SG_WIDTH].astype(BF16)
        o += SG_WIDTH
        g_ref[rows, :] = p[:, o:o + EVEN_GATE_WIDTH].astype(BF16)


def _even_in_proj(x, shift, scale, positions, w_in):
    batch, seq, _ = x.shape
    d = np.arange(LANES) % HEAD_DIM
    inv_freq = np.power(np.float64(ROPE_THETA), -np.arange(ROPE_HALF) / ROPE_HALF)
    freq_lane = jnp.asarray(np.where(d < ROPE_DIM, inv_freq[d % ROPE_HALF], 0.0), dtype=F32).reshape(1, LANES)
    sign_lane = jnp.asarray(np.where(d < ROPE_HALF, -1.0, np.where(d < ROPE_DIM, 1.0, 0.0)),
                            dtype=F32).reshape(1, LANES)
    row = lambda width: pl.BlockSpec((None, ROW_TILE, width), lambda b, i: (b, i, 0))
    vec = pl.BlockSpec((None, 1, D_MODEL), lambda b, i: (b, 0, 0))
    const = lambda shape: pl.BlockSpec(shape, lambda b, i: (0,) * len(shape))
    widths = (ATTN_WIDTH, KV_WIDTH, KV_WIDTH, SG_WIDTH, SG_WIDTH, EVEN_GATE_WIDTH)
    return pl.pallas_call(
        _even_in_kernel,
        grid=(batch, seq // ROW_TILE),
        in_specs=[row(D_MODEL), vec, vec, pl.BlockSpec((None, 1, ROW_TILE), lambda b, i: (b, 0, i)),
                  const((1, LANES)), const((1, LANES)), const((D_MODEL, EVEN_IN_WIDTH))],
        out_specs=[row(w) for w in widths],
        out_shape=[jax.ShapeDtypeStruct((batch, seq, w), BF16) for w in widths],
        compiler_params=_params("parallel", "parallel"),
        name="even_in_proj",
    )(x, shift, scale, positions.reshape(batch, 1, seq), freq_lane, sign_lane, w_in.astype(BF16))


def _even_mix_kernel(sink_ref, q_ref, k_ref, v_ref, su_ref, sv_ref, g_ref, x_ref, gate_ref,
                     avg_ref, avg2_ref, sgg_ref, sgb_ref, sgw_ref, sgbias_ref, wout_ref, lng_ref, lnb_ref,
                     o_ref, kdup, vsplit, y_scr):
    seq = k_ref.shape[0]
    n_blocks = seq // ATTN_BLOCK
    sub_blocks = ROW_TILE // ATTN_BLOCK
    tile = pl.program_id(1)
    lane = lax.broadcasted_iota(jnp.int32, (1, LANES), 1)
    left = lane < HEAD_DIM

    @pl.when(tile == 0)
    def _():
        def build(c, carry):
            rows = pl.ds(pl.multiple_of(c * ROW_TILE, ROW_TILE), ROW_TILE)
            k = k_ref[rows, :].astype(F32)
            k_sw = pltpu.roll(k, HEAD_DIM, 1)
            kdup[rows, 0:LANES] = jnp.where(left, k, k_sw).astype(BF16)
            kdup[rows, LANES:2 * LANES] = jnp.where(left, k_sw, k).astype(BF16)
            v = v_ref[rows, :].astype(F32)
            v_sw = pltpu.roll(v, HEAD_DIM, 1)
            vsplit[rows, 0 * LANES:1 * LANES] = jnp.where(left, v, 0.0).astype(BF16)
            vsplit[rows, 1 * LANES:2 * LANES] = jnp.where(left, 0.0, v_sw).astype(BF16)
            vsplit[rows, 2 * LANES:3 * LANES] = jnp.where(left, v_sw, 0.0).astype(BF16)
            vsplit[rows, 3 * LANES:4 * LANES] = jnp.where(left, 0.0, v).astype(BF16)
            return carry
        lax.fori_loop(0, seq // ROW_TILE, build, 0)

    qi = lax.broadcasted_iota(jnp.int32, (ATTN_BLOCK, ATTN_BLOCK), 0)
    kj = lax.broadcasted_iota(jnp.int32, (ATTN_BLOCK, ATTN_BLOCK), 1)
    ones_left = jnp.broadcast_to(jnp.where(left, 1.0, 0.0).astype(BF16), (3 * ATTN_BLOCK, LANES))
    ones_right = jnp.broadcast_to(jnp.where(left, 0.0, 1.0).astype(BF16), (3 * ATTN_BLOCK, LANES))
    avg = avg_ref[...]
    sink_cols = [jnp.concatenate([jnp.full((ATTN_BLOCK, LANES), sink_ref[h] * LOG2_E, F32)
                                  for h in (4 * hk, 4 * hk + 2, 4 * hk + 1, 4 * hk + 3)], axis=0)
                 for hk in range(N_KV_HEADS)]

    def sub_block(j, carry):
        blk = tile * sub_blocks + j
        prev = jnp.maximum(blk - 1, 0)
        nxt = jnp.minimum(blk + 1, n_blocks - 1)
        rows = pl.ds(pl.multiple_of(j * ATTN_BLOCK, ATTN_BLOCK), ATTN_BLOCK)
        bias_prev = jnp.where((kj >= qi) & (blk > 0), 0.0, NEG_INF)
        bias_next = jnp.where((kj <= qi) & (blk < n_blocks - 1), 0.0, NEG_INF)
        bias_prev = jnp.concatenate([bias_prev] * Q_PER_KV, axis=0)
        bias_next = jnp.concatenate([bias_next] * Q_PER_KV, axis=0)

        def band(ref):
            return jnp.concatenate(
                [ref[pl.ds(pl.multiple_of(b * ATTN_BLOCK, ATTN_BLOCK), ATTN_BLOCK), :] for b in (prev, blk, nxt)],
                axis=0)

        k_band = band(kdup)
        v_band = band(vsplit)
        n_sg = SG_WIDTH // LANES
        sg_cols = [slice(c * LANES, (c + 1) * LANES) for c in range(n_sg)]

        def silu_gate(cols):
            g_c = g_ref[rows, cols].astype(F32)
            return g_c * _sigmoid(g_c)

        scores = []
        for hk in range(N_KV_HEADS):
            q_cols = [q_ref[rows, (2 * hk + c) * LANES:(2 * hk + c + 1) * LANES] for c in range(2)]
            zero = jnp.zeros_like(q_cols[0])
            lhs = jnp.concatenate([jnp.where(left, qc, zero) for qc in q_cols]
                                  + [jnp.where(left, zero, qc) for qc in q_cols], axis=0)
            scores.append(lax.dot_general(lhs, k_band[:, hk * LANES:(hk + 1) * LANES], (((1,), (1,)), ((), ())),
                                          preferred_element_type=F32))
        sv_cols = [sv_ref[rows, cols] for cols in sg_cols]
        means = [jnp.dot(v_c, avg, preferred_element_type=F32) for v_c in sv_cols]

        probs, row_max = [], []
        for hk in range(N_KV_HEADS):
            s = scores[hk]
            s0 = s[:, 0:ATTN_BLOCK] + bias_prev
            s1 = s[:, ATTN_BLOCK:2 * ATTN_BLOCK]
            s2 = s[:, 2 * ATTN_BLOCK:3 * ATTN_BLOCK] + bias_next
            m = jnp.max(jnp.maximum(jnp.maximum(s0, s1), s2), axis=-1, keepdims=True)
            m = jnp.maximum(jnp.broadcast_to(m, sink_cols[hk].shape), sink_cols[hk])
            probs.append(jnp.concatenate([jnp.exp2(t - m).astype(BF16) for t in (s0, s1, s2)], axis=1))
            row_max.append(m)
        devs, sq_parts = [], []
        for v_c, mean in zip(sv_cols, means):
            dev = v_c.astype(F32) - mean
            sq = dev * dev
            sq_hi = sq.astype(BF16)
            sq_lo = (sq - sq_hi.astype(F32)).astype(BF16)
            devs.append(dev)
            sq_parts.append(jnp.concatenate([sq_hi, sq_lo], axis=1))

        pv = []
        for hk in range(N_KV_HEADS):
            p = probs[hk]
            lhs = jnp.concatenate(
                [jnp.concatenate([p[c * ATTN_BLOCK:(c + 1) * ATTN_BLOCK], p[(2 + c) * ATTN_BLOCK:(3 + c) * ATTN_BLOCK]],
                                 axis=1) for c in range(2)], axis=0)
            rhs = jnp.concatenate(
                [jnp.concatenate([v_band[:, (2 * hk) * LANES:(2 * hk + 1) * LANES], ones_left], axis=1),
                 jnp.concatenate([v_band[:, (2 * hk + 1) * LANES:(2 * hk + 2) * LANES], ones_right], axis=1)],
                axis=0)
            pv.append(jnp.dot(lhs, rhs, preferred_element_type=F32))
        variances = [jnp.dot(sq, avg2_ref[...], preferred_element_type=F32) for sq in sq_parts]

        for hk in range(N_KV_HEADS):
            for c in range(2):
                even, odd = slice(c * ATTN_BLOCK, (c + 1) * ATTN_BLOCK), slice((2 + c) * ATTN_BLOCK, (3 + c) * ATTN_BLOCK)
                sink_term = jnp.exp2(jnp.where(left, sink_cols[hk][even] - row_max[hk][even],
                                               sink_cols[hk][odd] - row_max[hk][odd]))
                o = pv[hk][c * ATTN_BLOCK:(c + 1) * ATTN_BLOCK]
                pair = o[:, 0:LANES] * (1.0 / (o[:, LANES:2 * LANES] + sink_term))
                cols = slice((2 * hk + c) * LANES, (2 * hk + c + 1) * LANES)
                y_scr[rows, cols] = (pair * silu_gate(cols)).astype(BF16)

        for c, cols in enumerate(sg_cols):
            vn = devs[c] * lax.rsqrt(variances[c] + LN_EPS) * sgg_ref[:, cols] + sgb_ref[:, cols]
            stacked = jnp.concatenate([jnp.where(left, vn, 0.0), jnp.where(left, 0.0, vn)],
                                      axis=0).astype(BF16)
            mixed = jnp.dot(sgw_ref[c], stacked, preferred_element_type=F32) + sgbias_ref[:, cols]
            y_sg = su_ref[rows, cols].astype(F32) * mixed
            gc = slice(ATTN_WIDTH + c * LANES, ATTN_WIDTH + (c + 1) * LANES)
            y_scr[rows, gc] = (y_sg * silu_gate(gc)).astype(BF16)
        return carry

    lax.fori_loop(0, sub_blocks, sub_block, 0, unroll=True)

    for r in range(2):
        half_rows = slice(r * (ROW_TILE // 2), (r + 1) * (ROW_TILE // 2))
        out = jnp.dot(y_scr[half_rows, :], wout_ref[...], preferred_element_type=F32)
        z = DEEPNORM_ALPHA * x_ref[half_rows, :] + gate_ref[...] * out
        o_ref[half_rows, :] = _layer_norm_rows(z, lng_ref[...], lnb_ref[...])


def _even_mixer(q, k, v, su, sv, g, x, gate, sink, sg_ln_g, sg_ln_b, sg_w, sg_b, w_out, ln_g, ln_b):
    batch, seq, _ = x.shape
    lane = np.arange(LANES)
    avg = jnp.asarray((lane[:, None] // SG_GROUP_DIM == lane[None, :] // SG_GROUP_DIM) / SG_GROUP_DIM, dtype=BF16)
    w_pairs = sg_w.reshape(N_SG_GROUPS // 2, 2, SG_CHUNK, SG_CHUNK).transpose(0, 2, 1, 3)
    w_pairs = w_pairs.reshape(N_SG_GROUPS // 2, SG_CHUNK, 2 * SG_CHUNK).astype(BF16)
    bias_full = jnp.repeat(sg_b.T, SG_GROUP_DIM, axis=1)
    row = lambda width: pl.BlockSpec((None, ROW_TILE, width), lambda b, i: (b, i, 0))
    full_seq = pl.BlockSpec((None, seq, KV_WIDTH), lambda b, i: (b, 0, 0))
    const = lambda shape: pl.BlockSpec(shape, lambda b, i: (0,) * len(shape))
    return pl.pallas_call(
        _even_mix_kernel,
        grid=(batch, seq // ROW_TILE),
        in_specs=[
            pl.BlockSpec(memory_space=pltpu.SMEM),
            row(ATTN_WIDTH), full_seq, full_seq, row(SG_WIDTH), row(SG_WIDTH), row(EVEN_GATE_WIDTH),
            row(D_MODEL), pl.BlockSpec((None, 1, D_MODEL), lambda b, i: (b, 0, 0)),
            const((LANES, LANES)), const((2 * LANES, LANES)), const((1, SG_WIDTH)), const((1, SG_WIDTH)),
            const((N_SG_GROUPS // 2, SG_CHUNK, 2 * SG_CHUNK)), const((SG_CHUNK, SG_WIDTH)),
            const((EVEN_GATE_WIDTH, D_MODEL)), const((1, D_MODEL)), const((1, D_MODEL)),
        ],
        out_specs=row(D_MODEL),
        out_shape=jax.ShapeDtypeStruct((batch, seq, D_MODEL), F32),
        scratch_shapes=[pltpu.VMEM((seq, 2 * KV_WIDTH), BF16), pltpu.VMEM((seq, 4 * KV_WIDTH), BF16),
                        pltpu.VMEM((ROW_TILE, EVEN_GATE_WIDTH), BF16)],
        compiler_params=_params("arbitrary", "arbitrary"),
        name="even_mixer",
    )(sink, q, k, v, su, sv, g, x, gate, avg, jnp.concatenate([avg, avg], axis=0), sg_ln_g.reshape(1, -1), sg_ln_b.reshape(1, -1),
      w_pairs, bias_full, w_out.astype(BF16), ln_g.reshape(1, -1), ln_b.reshape(1, -1))


def _odd_in_kernel(x_ref, shift_ref, scale_ref, w_ref, xr_ref, g_ref):
    seg_len = xr_ref.shape[1]
    head_rows = seg_len * SEGS_PER_TILE
    xr_rows = xr_ref.reshape(RNN_HEADS * head_rows, LANES)
    halves = 2
    for r in range(halves):
        first = r * (SEGS_PER_TILE // halves)
        rows = slice(first * seg_len, (first + SEGS_PER_TILE // halves) * seg_len)
        h = (x_ref[rows, :] * (1.0 + scale_ref[...]) + shift_ref[...]).astype(BF16)
        p = jnp.dot(h, w_ref[...], preferred_element_type=F32)
        for s in range(SEGS_PER_TILE // halves):
            for hd in range(RNN_HEADS):
                xr_rows[pl.ds(hd * head_rows + first + s, seg_len, stride=SEGS_PER_TILE), :] = (
                    p[s * seg_len:(s + 1) * seg_len, hd * LANES:(hd + 1) * LANES])
        g_ref[rows, :] = p[:, RNN_WIDTH:2 * RNN_WIDTH].astype(BF16)


def _odd_in_proj(x, shift, scale, w_in):
    batch, seq, _ = x.shape
    seg_len = seq // N_SEG
    tile_rows = SEGS_PER_TILE * seg_len
    vec = pl.BlockSpec((None, 1, D_MODEL), lambda b, i: (b, 0, 0))
    out_spec = pl.BlockSpec((None, RNN_HEADS, seg_len, SEGS_PER_TILE, RNN_HEAD_DIM), lambda b, i: (b, 0, 0, i, 0))
    out_shape = jax.ShapeDtypeStruct((batch, RNN_HEADS, seg_len, N_SEG, RNN_HEAD_DIM), F32)
    return pl.pallas_call(
        _odd_in_kernel,
        grid=(batch, seq // tile_rows),
        in_specs=[pl.BlockSpec((None, tile_rows, D_MODEL), lambda b, i: (b, i, 0)), vec, vec,
                  pl.BlockSpec((D_MODEL, 2 * RNN_WIDTH), lambda b, i: (0, 0))],
        out_specs=[out_spec, pl.BlockSpec((None, tile_rows, RNN_WIDTH), lambda b, i: (b, i, 0))],
        out_shape=[out_shape, jax.ShapeDtypeStruct((batch, seq, RNN_WIDTH), BF16)],
        compiler_params=_params("parallel", "parallel"),
        name="odd_in_proj",
    )(x, shift, scale, w_in.astype(BF16))


def _rglru_kernel(x_ref, convw_ref, convb_ref, wcat_ref, bcat_ref, lam_ref, o_ref,
                  halo, a_f, b_f, a_b, b_b, edge):
    n_rows = x_ref.shape[0]
    seg_len = n_rows // N_SEG
    n_chunks = n_rows // SCAN_CHUNK
    left_rows = CONV_LEFT * N_SEG
    right_rows = (CONV_WIDTH - 1 - CONV_LEFT) * N_SEG
    seg = lax.broadcasted_iota(jnp.int32, (N_SEG, LANES), 0)

    def from_prev_segment(blk):
        return jnp.where(seg == 0, 0.0, pltpu.roll(blk, 1, 0))

    def from_next_segment(blk):
        return jnp.where(seg == N_SEG - 1, 0.0, pltpu.roll(blk, N_SEG - 1, 0))

    for t in range(CONV_LEFT):
        halo[0, pl.ds(t * N_SEG, N_SEG), :] = from_prev_segment(
            x_ref[pl.ds((seg_len - CONV_LEFT + t) * N_SEG, N_SEG), :])
    halo[0, pl.ds(left_rows, SCAN_CHUNK + right_rows), :] = x_ref[pl.ds(0, SCAN_CHUNK + right_rows), :]
    halo[1, pl.ds(0, left_rows + SCAN_CHUNK), :] = x_ref[pl.ds(n_rows - SCAN_CHUNK - left_rows, left_rows + SCAN_CHUNK), :]
    for t in range(CONV_WIDTH - 1 - CONV_LEFT):
        halo[1, pl.ds(left_rows + SCAN_CHUNK + t * N_SEG, N_SEG), :] = from_next_segment(
            x_ref[pl.ds(t * N_SEG, N_SEG), :])

    lam = lam_ref[...]
    softplus_neg = jnp.maximum(-lam, 0.0) + jnp.log1p(jnp.exp(-jnp.abs(lam)))
    half_rate = (0.5 * RG_LRU_C) * softplus_neg

    def gates(base, tap):
        xr = convb_ref[...] + sum(convw_ref[k:k + 1, :] * tap(k) for k in range(CONV_WIDTH))
        th = jnp.tanh(jnp.dot(xr.astype(BF16), wcat_ref[...], preferred_element_type=F32) + bcat_ref[...])
        half_x = 0.5 * xr
        for d, (a_ref, b_ref) in enumerate(((a_f, b_f), (a_b, b_b))):
            o = 2 * d * LANES
            neg_log_a = (th[:, o:o + LANES] + 1.0) * half_rate[d:d + 1, :]
            a = jnp.exp2(neg_log_a * (-LOG2_E))
            one_minus_a2 = jnp.tanh(neg_log_a) * (a * a + 1.0)
            root = one_minus_a2 * lax.rsqrt(jnp.maximum(one_minus_a2, F32_TINY))
            a_ref[pl.ds(base, SCAN_CHUNK), :] = a
            b_ref[pl.ds(base, SCAN_CHUNK), :] = root * ((th[:, o + LANES:o + 2 * LANES] + 1.0) * half_x)

    def inner_chunk(c, carry):
        base = pl.multiple_of(c * SCAN_CHUNK, SCAN_CHUNK)
        gates(base, lambda k: x_ref[pl.ds(base + (k - CONV_LEFT) * N_SEG, SCAN_CHUNK), :])
        return carry

    gates(0, lambda k: halo[0, pl.ds(k * N_SEG, SCAN_CHUNK), :])
    gates(n_rows - SCAN_CHUNK, lambda k: halo[1, pl.ds(k * N_SEG, SCAN_CHUNK), :])
    lax.fori_loop(1, n_chunks - 1, inner_chunk, 0, unroll=7)

    def scan_step(t, carry):
        h_f, p_f, h_b, p_b = carry
        rf = pl.ds(pl.multiple_of(t * N_SEG, N_SEG), N_SEG)
        rb = pl.ds(pl.multiple_of((seg_len - 1 - t) * N_SEG, N_SEG), N_SEG)
        a = a_f[rf, :]
        h_f = a * h_f + b_f[rf, :]
        p_f = a * p_f
        b_f[rf, :] = h_f
        a_f[rf, :] = p_f
        a = a_b[rb, :]
        h_b = a * h_b + b_b[rb, :]
        p_b = a * p_b
        b_b[rb, :] = h_b
        a_b[rb, :] = p_b
        return h_f, p_f, h_b, p_b

    zeros = jnp.zeros((N_SEG, LANES), F32)
    ones = jnp.ones((N_SEG, LANES), F32)
    h_f, p_f, h_b, p_b = lax.fori_loop(0, seg_len, scan_step, (zeros, ones, zeros, ones), unroll=4)

    edge[0] = h_f
    edge[1] = p_f
    edge[2] = h_b
    edge[3] = p_b
    edge[4, 0:1, :] = jnp.zeros((1, LANES), F32)
    edge[5, N_SEG - 1:N_SEG, :] = jnp.zeros((1, LANES), F32)
    for s in range(1, N_SEG):
        edge[4, s:s + 1, :] = edge[0, s - 1:s, :] + edge[1, s - 1:s, :] * edge[4, s - 1:s, :]
        r = N_SEG - 1 - s
        edge[5, r:r + 1, :] = edge[2, r + 1:r + 2, :] + edge[3, r + 1:r + 2, :] * edge[5, r + 1:r + 2, :]
    blocks = SCAN_CHUNK // N_SEG
    in_f = jnp.concatenate([edge[4]] * blocks, axis=0)
    in_b = jnp.concatenate([edge[5]] * blocks, axis=0)

    def out_chunk(c, carry):
        rows = pl.ds(pl.multiple_of(c * SCAN_CHUNK, SCAN_CHUNK), SCAN_CHUNK)
        o_ref[rows, :] = (b_f[rows, :] + a_f[rows, :] * in_f) + (b_b[rows, :] + a_b[rows, :] * in_b)
        return carry

    lax.fori_loop(0, n_rows // SCAN_CHUNK, out_chunk, 0, unroll=2)


def _rglru(xr, conv_w, conv_b, w_a, b_a, w_x, b_x, lam):
    batch, _, n_rows, _ = xr.shape
    w_cat = (0.5 * jnp.concatenate([w_a[0], w_x[0], w_a[1], w_x[1]], axis=-1)).astype(BF16)
    per_head = lambda v: v.reshape(RNN_HEADS, 1, RNN_HEAD_DIM)
    b_cat = 0.5 * jnp.concatenate([per_head(b_a[0]), per_head(b_x[0]), per_head(b_a[1]), per_head(b_x[1])],
                                  axis=-1)
    col = lambda rows: pl.BlockSpec((rows, RNN_HEAD_DIM), lambda b, h: (0, h))
    seq_col = pl.BlockSpec((None, None, n_rows, RNN_HEAD_DIM), lambda b, h: (b, h, 0, 0))
    halo_rows = SCAN_CHUNK + (CONV_WIDTH - 1) * N_SEG
    return pl.pallas_call(
        _rglru_kernel,
        grid=(batch, RNN_HEADS),
        in_specs=[seq_col, col(CONV_WIDTH), col(1),
                  pl.BlockSpec((None, RNN_HEAD_DIM, 4 * RNN_HEAD_DIM), lambda b, h: (h, 0, 0)),
                  pl.BlockSpec((None, 1, 4 * RNN_HEAD_DIM), lambda b, h: (h, 0, 0)),
                  col(2)],
        out_specs=seq_col,
        out_shape=jax.ShapeDtypeStruct((batch, RNN_HEADS, n_rows, RNN_HEAD_DIM), F32),
        scratch_shapes=[pltpu.VMEM((2, halo_rows, LANES), F32)]
                       + [pltpu.VMEM((n_rows, LANES), F32)] * 4
                       + [pltpu.VMEM((6, N_SEG, LANES), F32)],
        compiler_params=_params("parallel", "parallel"),
        name="rglru",
    )(xr, conv_w, conv_b.reshape(1, -1), w_cat, b_cat, lam)


def _odd_out_kernel(y_ref, g_ref, x_ref, gate_ref, w_ref, lng_ref, lnb_ref, o_ref):
    seg_len = y_ref.shape[1]
    head_rows = seg_len * SEGS_PER_TILE
    y_rows = y_ref.reshape(RNN_HEADS * head_rows, LANES)
    gated = []
    for s in range(SEGS_PER_TILE):
        y = jnp.concatenate([y_rows[pl.ds(hd * head_rows + s, seg_len, stride=SEGS_PER_TILE), :]
                             for hd in range(RNN_HEADS)], axis=1)
        g = g_ref[s * seg_len:(s + 1) * seg_len, :].astype(F32)
        gated.append((y * (g * _sigmoid(g))).astype(BF16))
    out = jnp.dot(jnp.concatenate(gated, axis=0), w_ref[...], preferred_element_type=F32)
    z = DEEPNORM_ALPHA * x_ref[...] + gate_ref[...] * out
    o_ref[...] = _layer_norm_rows(z, lng_ref[...], lnb_ref[...])


def _odd_out_proj(y, g, x, gate, w_out, ln_g, ln_b):
    batch, seq, _ = x.shape
    seg_len = seq // N_SEG
    tile_rows = SEGS_PER_TILE * seg_len
    const = lambda shape: pl.BlockSpec(shape, lambda b, i: (0,) * len(shape))
    return pl.pallas_call(
        _odd_out_kernel,
        grid=(batch, seq // tile_rows),
        in_specs=[pl.BlockSpec((None, RNN_HEADS, seg_len, SEGS_PER_TILE, RNN_HEAD_DIM),
                               lambda b, i: (b, 0, 0, i, 0)),
                  pl.BlockSpec((None, tile_rows, RNN_WIDTH), lambda b, i: (b, i, 0)),
                  pl.BlockSpec((None, tile_rows, D_MODEL), lambda b, i: (b, i, 0)),
                  pl.BlockSpec((None, 1, D_MODEL), lambda b, i: (b, 0, 0)),
                  const((RNN_WIDTH, D_MODEL)), const((1, D_MODEL)), const((1, D_MODEL))],
        out_specs=pl.BlockSpec((None, tile_rows, D_MODEL), lambda b, i: (b, i, 0)),
        out_shape=jax.ShapeDtypeStruct((batch, seq, D_MODEL), F32),
        compiler_params=_params("parallel", "parallel"),
        name="odd_out_proj",
    )(y, g, x, gate, w_out.astype(BF16), ln_g.reshape(1, -1), ln_b.reshape(1, -1))


def kernel(x, c, positions, ada_w, ada_b, ln_g, ln_b, ev_w_in, ev_w_out, ev_sink, ev_sg_ln_g, ev_sg_ln_b,
           ev_sg_w, ev_sg_b, od_w_in, od_conv_w, od_conv_b, od_w_a, od_b_a, od_w_x, od_b_x, od_lam, od_w_out):
    batch, seq, d_model = x.shape
    seg_len = seq // N_SEG
    assert d_model == D_MODEL and seq % ROW_TILE == 0 and seq == N_SEG * seg_len
    assert N_SEG % SEGS_PER_TILE == 0 and seq % SCAN_CHUNK == 0 and SCAN_CHUNK % N_SEG == 0
    assert ada_w.shape[0] == DEPTH == 2

    mod = _modulation(c, ada_w, ada_b)
    shift, scale, gate = (mod[:, :, i * D_MODEL:(i + 1) * D_MODEL].reshape(DEPTH, batch, 1, D_MODEL)
                          for i in range(3))

    q, k, v, su, sv, g = _even_in_proj(x, shift[0], scale[0], positions, ev_w_in[0])
    x1 = _even_mixer(q, k, v, su, sv, g, x, gate[0], ev_sink[0], ev_sg_ln_g[0], ev_sg_ln_b[0],
                     ev_sg_w[0], ev_sg_b[0], ev_w_out[0], ln_g[0], ln_b[0])

    xr, g1 = _odd_in_proj(x1, shift[1], scale[1], od_w_in[0])
    y = _rglru(xr.reshape(batch, RNN_HEADS, seq, RNN_HEAD_DIM), od_conv_w[0], od_conv_b[0], od_w_a[0],
               od_b_a[0], od_w_x[0], od_b_x[0], od_lam[0])
    return _odd_out_proj(y.reshape(batch, RNN_HEADS, seg_len, N_SEG, RNN_HEAD_DIM), g1, x1, gate[1],
                         od_w_out[0], ln_g[1], ln_b[1])
```

```python
import math

import numpy as np
import jax
import jax.numpy as jnp
from jax import lax
from jax.experimental import pallas as pl
from jax.experimental.pallas import tpu as pltpu

D_MODEL = 1024
DEPTH = 2
HEAD_DIM = 64
N_Q_HEADS = 8
N_KV_HEADS = 2
Q_PER_KV = N_Q_HEADS // N_KV_HEADS
ATTN_WIDTH = N_Q_HEADS * HEAD_DIM
KV_WIDTH = N_KV_HEADS * HEAD_DIM
ATTN_BLOCK = 128
ROPE_DIM = HEAD_DIM // 4
ROPE_HALF = ROPE_DIM // 2
ROPE_THETA = 500000.0
N_SG_GROUPS = 8
SG_GROUP_DIM = 64
SG_WIDTH = N_SG_GROUPS * SG_GROUP_DIM
SG_CHUNK = 128
EVEN_GATE_WIDTH = ATTN_WIDTH + SG_WIDTH
EVEN_IN_WIDTH = ATTN_WIDTH + 2 * KV_WIDTH + 2 * SG_WIDTH + EVEN_GATE_WIDTH
RNN_WIDTH = D_MODEL
RNN_HEADS = 8
RNN_HEAD_DIM = RNN_WIDTH // RNN_HEADS
CONV_WIDTH = 4
CONV_LEFT = 2
RG_LRU_C = 8.0
DEEPNORM_ALPHA = (2 * DEPTH) ** 0.25
LN_EPS = 1e-5
NEG_INF = -1e30
LOG2_E = math.log2(math.e)
F32_TINY = float(np.finfo(np.float32).tiny)

LANES = 128
SUBLANES = 8
VMEM_LIMIT = 56 * 1024 * 1024

ROW_TILE = 512
N_SEG = 32
SEGS_PER_TILE = SUBLANES
SCAN_CHUNK = 256
WEIGHT_CAST_ROWS = 128

BF16 = jnp.bfloat16
F32 = jnp.float32


def _sigmoid(x):
    return 0.5 * jnp.tanh(0.5 * x) + 0.5


def _layer_norm_rows(z, g, b):
    mu = jnp.mean(z, axis=-1, keepdims=True)
    d = z - mu
    var = jnp.mean(d * d, axis=-1, keepdims=True)
    return d * lax.rsqrt(var + LN_EPS) * g + b


def _params(*semantics):
    return pltpu.CompilerParams(dimension_semantics=semantics, vmem_limit_bytes=VMEM_LIMIT)


def _cast_weight_once(w_ref, w_bf16):
    @pl.when((pl.program_id(0) == 0) & (pl.program_id(1) == 0))
    def _():
        for r in range(0, w_ref.shape[0], WEIGHT_CAST_ROWS):
            w_bf16[r:r + WEIGHT_CAST_ROWS, :] = w_ref[r:r + WEIGHT_CAST_ROWS, :].astype(BF16)


def _resident(shape):
    return pl.BlockSpec(shape, lambda b, i: (0,) * len(shape), pipeline_mode=pl.Buffered(1))


def _mod_kernel(c_ref, w_ref, b_ref, o_ref):
    c = c_ref[...]
    cond = (c * _sigmoid(c)).astype(BF16)
    o_ref[...] = jnp.dot(cond, w_ref[...].astype(BF16), preferred_element_type=F32) + b_ref[...]


def _modulation(c, ada_w, ada_b):
    batch = c.shape[0]
    rows = -(-batch // SUBLANES) * SUBLANES
    c_pad = jnp.pad(c, ((0, rows - batch), (0, 0)))
    n_col = 3 * D_MODEL // D_MODEL
    out = pl.pallas_call(
        _mod_kernel,
        grid=(DEPTH, n_col),
        in_specs=[
            pl.BlockSpec((rows, D_MODEL), lambda l, j: (0, 0)),
            pl.BlockSpec((None, D_MODEL, D_MODEL), lambda l, j: (l, 0, j)),
            pl.BlockSpec((None, 1, D_MODEL), lambda l, j: (l, 0, j)),
        ],
        out_specs=pl.BlockSpec((None, rows, D_MODEL), lambda l, j: (l, 0, j)),
        out_shape=jax.ShapeDtypeStruct((DEPTH, rows, 3 * D_MODEL), F32),
        compiler_params=_params("parallel", "parallel"),
        name="adaln_mod",
    )(c_pad, ada_w, ada_b.reshape(DEPTH, 1, 3 * D_MODEL))
    return out[:, :batch]


def _even_in_kernel(x_ref, shift_ref, scale_ref, pos_ref, freq_ref, sign_ref, w_f32_ref,
                    q_ref, k_ref, v_ref, su_ref, sv_ref, g_ref, w_ref):
    _cast_weight_once(w_f32_ref, w_ref)
    lane = lax.broadcasted_iota(jnp.int32, (1, LANES), 1)
    first_half = (lane % HEAD_DIM) < ROPE_HALF
    diagonal = lax.broadcasted_iota(jnp.int32, (LANES, LANES), 0) == lax.broadcasted_iota(jnp.int32, (LANES, LANES), 1)
    q_scale = HEAD_DIM ** -0.5 * LOG2_E

    half_rows = x_ref.shape[0] // 2
    for r in range(2):
        rows = slice(r * half_rows, (r + 1) * half_rows)
        h = (x_ref[rows, :] * (1.0 + scale_ref[...]) + shift_ref[...]).astype(BF16)
        p = jnp.dot(h, w_ref[...], preferred_element_type=F32)

        pos_cols = []
        for j in range(r * half_rows // LANES, (r + 1) * half_rows // LANES):
            pos_row = pos_ref[:, j * LANES:(j + 1) * LANES].astype(F32)
            pos_cols.append(jnp.sum(jnp.where(diagonal, pos_row, 0.0), axis=1, keepdims=True))
        ang = jnp.concatenate(pos_cols, axis=0) * freq_ref[...]
        cs = jnp.cos(ang)
        sn = jnp.sin(ang) * sign_ref[...]

        def rotary(t):
            partner = jnp.where(first_half, pltpu.roll(t, LANES - ROPE_HALF, 1), pltpu.roll(t, ROPE_HALF, 1))
            return t * cs + partner * sn

        for j in range(ATTN_WIDTH // LANES):
            q_ref[rows, j * LANES:(j + 1) * LANES] = (
                rotary(p[:, j * LANES:(j + 1) * LANES]) * q_scale).astype(BF16)
        o = ATTN_WIDTH
        k_ref[rows, :] = rotary(p[:, o:o + KV_WIDTH]).astype(BF16)
        o += KV_WIDTH
        v_ref[rows, :] = p[:, o:o + KV_WIDTH].astype(BF16)
        o += KV_WIDTH
        su_ref[rows, :] = p[:, o:o + SG_WIDTH].astype(BF16)
        o += SG_WIDTH
        sv_ref[rows, :] = p[:, o:o + SG_WIDTH].astype(BF16)
        o += SG_WIDTH
        g_ref[rows, :] = p[:, o:o + EVEN_GATE_WIDTH].astype(BF16)


def _even_in_proj(x, shift, scale, positions, w_in):
    batch, seq, _ = x.shape
    d = np.arange(LANES) % HEAD_DIM
    inv_freq = np.power(np.float64(ROPE_THETA), -np.arange(ROPE_HALF) / ROPE_HALF)
    freq_lane = jnp.asarray(np.where(d < ROPE_DIM, inv_freq[d % ROPE_HALF], 0.0), dtype=F32).reshape(1, LANES)
    sign_lane = jnp.asarray(np.where(d < ROPE_HALF, -1.0, np.where(d < ROPE_DIM, 1.0, 0.0)),
                            dtype=F32).reshape(1, LANES)
    row = lambda width: pl.BlockSpec((None, ROW_TILE, width), lambda b, i: (b, i, 0))
    vec = pl.BlockSpec((None, 1, D_MODEL), lambda b, i: (b, 0, 0))
    const = lambda shape: pl.BlockSpec(shape, lambda b, i: (0,) * len(shape))
    widths = (ATTN_WIDTH, KV_WIDTH, KV_WIDTH, SG_WIDTH, SG_WIDTH, EVEN_GATE_WIDTH)
    return pl.pallas_call(
        _even_in_kernel,
        grid=(batch, seq // ROW_TILE),
        in_specs=[row(D_MODEL), vec, vec, pl.BlockSpec((None, 1, ROW_TILE), lambda b, i: (b, 0, i)),
                  const((1, LANES)), const((1, LANES)), _resident((D_MODEL, EVEN_IN_WIDTH))],
        out_specs=[row(w) for w in widths],
        out_shape=[jax.ShapeDtypeStruct((batch, seq, w), BF16) for w in widths],
        scratch_shapes=[pltpu.VMEM((D_MODEL, EVEN_IN_WIDTH), BF16)],
        compiler_params=_params("arbitrary", "arbitrary"),
        name="even_in_proj",
    )(x, shift, scale, positions.reshape(batch, 1, seq), freq_lane, sign_lane, w_in)


def _even_mix_kernel(sink_ref, q_ref, k_ref, v_ref, su_ref, sv_ref, g_ref, x_ref, gate_ref,
                     avg_ref, avg2_ref, sgg_ref, sgb_ref, sgw_ref, sgbias_ref, wout_f32_ref, lng_ref, lnb_ref,
                     o_ref, kdup, vsplit, y_scr, wout_ref):
    _cast_weight_once(wout_f32_ref, wout_ref)
    seq = k_ref.shape[0]
    n_blocks = seq // ATTN_BLOCK
    sub_blocks = ROW_TILE // ATTN_BLOCK
    tile = pl.program_id(1)
    lane = lax.broadcasted_iota(jnp.int32, (1, LANES), 1)
    left = lane < HEAD_DIM

    @pl.when(tile == 0)
    def _():
        def build(c, carry):
            rows = pl.ds(pl.multiple_of(c * ROW_TILE, ROW_TILE), ROW_TILE)
            k = k_ref[rows, :].astype(F32)
            k_sw = pltpu.roll(k, HEAD_DIM, 1)
            kdup[rows, 0:LANES] = jnp.where(left, k, k_sw).astype(BF16)
            kdup[rows, LANES:2 * LANES] = jnp.where(left, k_sw, k).astype(BF16)
            v = v_ref[rows, :].astype(F32)
            v_sw = pltpu.roll(v, HEAD_DIM, 1)
            vsplit[rows, 0 * LANES:1 * LANES] = jnp.where(left, v, 0.0).astype(BF16)
            vsplit[rows, 1 * LANES:2 * LANES] = jnp.where(left, 0.0, v_sw).astype(BF16)
            vsplit[rows, 2 * LANES:3 * LANES] = jnp.where(left, v_sw, 0.0).astype(BF16)
            vsplit[rows, 3 * LANES:4 * LANES] = jnp.where(left, 0.0, v).astype(BF16)
            return carry
        lax.fori_loop(0, seq // ROW_TILE, build, 0)

    qi = lax.broadcasted_iota(jnp.int32, (ATTN_BLOCK, ATTN_BLOCK), 0)
    kj = lax.broadcasted_iota(jnp.int32, (ATTN_BLOCK, ATTN_BLOCK), 1)
    ones_left = jnp.broadcast_to(jnp.where(left, 1.0, 0.0).astype(BF16), (3 * ATTN_BLOCK, LANES))
    ones_right = jnp.broadcast_to(jnp.where(left, 0.0, 1.0).astype(BF16), (3 * ATTN_BLOCK, LANES))
    avg = avg_ref[...]
    sink_cols = [jnp.concatenate([jnp.full((ATTN_BLOCK, LANES), sink_ref[h] * LOG2_E, F32)
                                  for h in (4 * hk, 4 * hk + 2, 4 * hk + 1, 4 * hk + 3)], axis=0)
                 for hk in range(N_KV_HEADS)]

    def sub_block(j, carry):
        blk = tile * sub_blocks + j
        prev = jnp.maximum(blk - 1, 0)
        nxt = jnp.minimum(blk + 1, n_blocks - 1)
        rows = pl.ds(pl.multiple_of(j * ATTN_BLOCK, ATTN_BLOCK), ATTN_BLOCK)
        bias_prev = jnp.where((kj >= qi) & (blk > 0), 0.0, NEG_INF)
        bias_next = jnp.where((kj <= qi) & (blk < n_blocks - 1), 0.0, NEG_INF)
        bias_prev = jnp.concatenate([bias_prev] * Q_PER_KV, axis=0)
        bias_next = jnp.concatenate([bias_next] * Q_PER_KV, axis=0)

        def band(ref):
            return jnp.concatenate(
                [ref[pl.ds(pl.multiple_of(b * ATTN_BLOCK, ATTN_BLOCK), ATTN_BLOCK), :] for b in (prev, blk, nxt)],
                axis=0)

        k_band = band(kdup)
        v_band = band(vsplit)
        n_sg = SG_WIDTH // LANES
        sg_cols = [slice(c * LANES, (c + 1) * LANES) for c in range(n_sg)]

        def silu_gate(cols):
            g_c = g_ref[rows, cols].astype(F32)
            return g_c * _sigmoid(g_c)

        scores = []
        for hk in range(N_KV_HEADS):
            q_cols = [q_ref[rows, (2 * hk + c) * LANES:(2 * hk + c + 1) * LANES] for c in range(2)]
            zero = jnp.zeros_like(q_cols[0])
            lhs = jnp.concatenate([jnp.where(left, qc, zero) for qc in q_cols]
                                  + [jnp.where(left, zero, qc) for qc in q_cols], axis=0)
            scores.append(lax.dot_general(lhs, k_band[:, hk * LANES:(hk + 1) * LANES], (((1,), (1,)), ((), ())),
                                          preferred_element_type=F32))
        sv_cols = [sv_ref[rows, cols] for cols in sg_cols]
        means = [jnp.dot(v_c, avg, preferred_element_type=F32) for v_c in sv_cols]

        probs, row_max = [], []
        for hk in range(N_KV_HEADS):
            s = scores[hk]
            s0 = s[:, 0:ATTN_BLOCK] + bias_prev
            s1 = s[:, ATTN_BLOCK:2 * ATTN_BLOCK]
            s2 = s[:, 2 * ATTN_BLOCK:3 * ATTN_BLOCK] + bias_next
            m = jnp.max(jnp.maximum(jnp.maximum(s0, s1), s2), axis=-1, keepdims=True)
            m = jnp.maximum(jnp.broadcast_to(m, sink_cols[hk].shape), sink_cols[hk])
            probs.append(jnp.concatenate([jnp.exp2(t - m).astype(BF16) for t in (s0, s1, s2)], axis=1))
            row_max.append(m)
        devs, sq_parts = [], []
        for v_c, mean in zip(sv_cols, means):
            dev = v_c.astype(F32) - mean
            sq = dev * dev
            sq_hi = sq.astype(BF16)
            sq_lo = (sq - sq_hi.astype(F32)).astype(BF16)
            devs.append(dev)
            sq_parts.append(jnp.concatenate([sq_hi, sq_lo], axis=1))

        pv = []
        for hk in range(N_KV_HEADS):
            p = probs[hk]
            lhs = jnp.concatenate(
                [jnp.concatenate([p[c * ATTN_BLOCK:(c + 1) * ATTN_BLOCK], p[(2 + c) * ATTN_BLOCK:(3 + c) * ATTN_BLOCK]],
                                 axis=1) for c in range(2)], axis=0)
            rhs = jnp.concatenate(
                [jnp.concatenate([v_band[:, (2 * hk) * LANES:(2 * hk + 1) * LANES], ones_left], axis=1),
                 jnp.concatenate([v_band[:, (2 * hk + 1) * LANES:(2 * hk + 2) * LANES], ones_right], axis=1)],
                axis=0)
            pv.append(jnp.dot(lhs, rhs, preferred_element_type=F32))
        variances = [jnp.dot(sq, avg2_ref[...], preferred_element_type=F32) for sq in sq_parts]

        for hk in range(N_KV_HEADS):
            for c in range(2):
                even, odd = slice(c * ATTN_BLOCK, (c + 1) * ATTN_BLOCK), slice((2 + c) * ATTN_BLOCK, (3 + c) * ATTN_BLOCK)
                sink_term = jnp.exp2(jnp.where(left, sink_cols[hk][even] - row_max[hk][even],
                                               sink_cols[hk][odd] - row_max[hk][odd]))
                o = pv[hk][c * ATTN_BLOCK:(c + 1) * ATTN_BLOCK]
                pair = o[:, 0:LANES] * (1.0 / (o[:, LANES:2 * LANES] + sink_term))
                cols = slice((2 * hk + c) * LANES, (2 * hk + c + 1) * LANES)
                y_scr[rows, cols] = (pair * silu_gate(cols)).astype(BF16)

        for c, cols in enumerate(sg_cols):
            vn = devs[c] * lax.rsqrt(variances[c] + LN_EPS) * sgg_ref[:, cols] + sgb_ref[:, cols]
            stacked = jnp.concatenate([jnp.where(left, vn, 0.0), jnp.where(left, 0.0, vn)],
                                      axis=0).astype(BF16)
            mixed = jnp.dot(sgw_ref[c], stacked, preferred_element_type=F32) + sgbias_ref[:, cols]
            y_sg = su_ref[rows, cols].astype(F32) * mixed
            gc = slice(ATTN_WIDTH + c * LANES, ATTN_WIDTH + (c + 1) * LANES)
            y_scr[rows, gc] = (y_sg * silu_gate(gc)).astype(BF16)
        return carry

    lax.fori_loop(0, sub_blocks, sub_block, 0, unroll=True)

    for r in range(2):
        half_rows = slice(r * (ROW_TILE // 2), (r + 1) * (ROW_TILE // 2))
        out = jnp.dot(y_scr[half_rows, :], wout_ref[...], preferred_element_type=F32)
        z = DEEPNORM_ALPHA * x_ref[half_rows, :] + gate_ref[...] * out
        o_ref[half_rows, :] = _layer_norm_rows(z, lng_ref[...], lnb_ref[...])


def _even_mixer(q, k, v, su, sv, g, x, gate, sink, sg_ln_g, sg_ln_b, sg_w, sg_b, w_out, ln_g, ln_b):
    batch, seq, _ = x.shape
    lane = np.arange(LANES)
    avg = jnp.asarray((lane[:, None] // SG_GROUP_DIM == lane[None, :] // SG_GROUP_DIM) / SG_GROUP_DIM, dtype=BF16)
    w_pairs = sg_w.reshape(N_SG_GROUPS // 2, 2, SG_CHUNK, SG_CHUNK).transpose(0, 2, 1, 3)
    w_pairs = w_pairs.reshape(N_SG_GROUPS // 2, SG_CHUNK, 2 * SG_CHUNK).astype(BF16)
    bias_full = jnp.repeat(sg_b.T, SG_GROUP_DIM, axis=1)
    row = lambda width: pl.BlockSpec((None, ROW_TILE, width), lambda b, i: (b, i, 0))
    full_seq = pl.BlockSpec((None, seq, KV_WIDTH), lambda b, i: (b, 0, 0))
    const = lambda shape: pl.BlockSpec(shape, lambda b, i: (0,) * len(shape))
    return pl.pallas_call(
        _even_mix_kernel,
        grid=(batch, seq // ROW_TILE),
        in_specs=[
            pl.BlockSpec(memory_space=pltpu.SMEM),
            row(ATTN_WIDTH), full_seq, full_seq, row(SG_WIDTH), row(SG_WIDTH), row(EVEN_GATE_WIDTH),
            row(D_MODEL), pl.BlockSpec((None, 1, D_MODEL), lambda b, i: (b, 0, 0)),
            const((LANES, LANES)), const((2 * LANES, LANES)), const((1, SG_WIDTH)), const((1, SG_WIDTH)),
            const((N_SG_GROUPS // 2, SG_CHUNK, 2 * SG_CHUNK)), const((SG_CHUNK, SG_WIDTH)),
            _resident((EVEN_GATE_WIDTH, D_MODEL)), const((1, D_MODEL)), const((1, D_MODEL)),
        ],
        out_specs=row(D_MODEL),
        out_shape=jax.ShapeDtypeStruct((batch, seq, D_MODEL), F32),
        scratch_shapes=[pltpu.VMEM((seq, 2 * KV_WIDTH), BF16), pltpu.VMEM((seq, 4 * KV_WIDTH), BF16),
                        pltpu.VMEM((ROW_TILE, EVEN_GATE_WIDTH), BF16),
                        pltpu.VMEM((EVEN_GATE_WIDTH, D_MODEL), BF16)],
        compiler_params=_params("arbitrary", "arbitrary"),
        name="even_mixer",
    )(sink, q, k, v, su, sv, g, x, gate, avg, jnp.concatenate([avg, avg], axis=0), sg_ln_g.reshape(1, -1), sg_ln_b.reshape(1, -1),
      w_pairs, bias_full, w_out, ln_g.reshape(1, -1), ln_b.reshape(1, -1))


def _odd_in_kernel(x_ref, shift_ref, scale_ref, w_f32_ref, xr_ref, g_ref, w_ref):
    _cast_weight_once(w_f32_ref, w_ref)
    seg_len = xr_ref.shape[1]
    head_rows = seg_len * SEGS_PER_TILE
    xr_rows = xr_ref.reshape(RNN_HEADS * head_rows, LANES)
    halves = 2
    for r in range(halves):
        first = r * (SEGS_PER_TILE // halves)
        rows = slice(first * seg_len, (first + SEGS_PER_TILE // halves) * seg_len)
        h = (x_ref[rows, :] * (1.0 + scale_ref[...]) + shift_ref[...]).astype(BF16)
        p = jnp.dot(h, w_ref[...], preferred_element_type=F32)
        for s in range(SEGS_PER_TILE // halves):
            for hd in range(RNN_HEADS):
                xr_rows[pl.ds(hd * head_rows + first + s, seg_len, stride=SEGS_PER_TILE), :] = (
                    p[s * seg_len:(s + 1) * seg_len, hd * LANES:(hd + 1) * LANES])
        g_ref[rows, :] = p[:, RNN_WIDTH:2 * RNN_WIDTH].astype(BF16)


def _odd_in_proj(x, shift, scale, w_in):
    batch, seq, _ = x.shape
    seg_len = seq // N_SEG
    tile_rows = SEGS_PER_TILE * seg_len
    vec = pl.BlockSpec((None, 1, D_MODEL), lambda b, i: (b, 0, 0))
    out_spec = pl.BlockSpec((None, RNN_HEADS, seg_len, SEGS_PER_TILE, RNN_HEAD_DIM), lambda b, i: (b, 0, 0, i, 0))
    out_shape = jax.ShapeDtypeStruct((batch, RNN_HEADS, seg_len, N_SEG, RNN_HEAD_DIM), F32)
    return pl.pallas_call(
        _odd_in_kernel,
        grid=(batch, seq // tile_rows),
        in_specs=[pl.BlockSpec((None, tile_rows, D_MODEL), lambda b, i: (b, i, 0)), vec, vec,
                  _resident((D_MODEL, 2 * RNN_WIDTH))],
        out_specs=[out_spec, pl.BlockSpec((None, tile_rows, RNN_WIDTH), lambda b, i: (b, i, 0))],
        out_shape=[out_shape, jax.ShapeDtypeStruct((batch, seq, RNN_WIDTH), BF16)],
        scratch_shapes=[pltpu.VMEM((D_MODEL, 2 * RNN_WIDTH), BF16)],
        compiler_params=_params("arbitrary", "arbitrary"),
        name="odd_in_proj",
    )(x, shift, scale, w_in)


def _rglru_kernel(x_ref, convw_ref, convb_ref, wcat_ref, bcat_ref, lam_ref, o_ref,
                  halo, a_f, b_f, a_b, b_b, edge):
    n_rows = x_ref.shape[0]
    seg_len = n_rows // N_SEG
    n_chunks = n_rows // SCAN_CHUNK
    left_rows = CONV_LEFT * N_SEG
    right_rows = (CONV_WIDTH - 1 - CONV_LEFT) * N_SEG
    seg = lax.broadcasted_iota(jnp.int32, (N_SEG, LANES), 0)

    def from_prev_segment(blk):
        return jnp.where(seg == 0, 0.0, pltpu.roll(blk, 1, 0))

    def from_next_segment(blk):
        return jnp.where(seg == N_SEG - 1, 0.0, pltpu.roll(blk, N_SEG - 1, 0))

    for t in range(CONV_LEFT):
        halo[0, pl.ds(t * N_SEG, N_SEG), :] = from_prev_segment(
            x_ref[pl.ds((seg_len - CONV_LEFT + t) * N_SEG, N_SEG), :])
    halo[0, pl.ds(left_rows, SCAN_CHUNK + right_rows), :] = x_ref[pl.ds(0, SCAN_CHUNK + right_rows), :]
    halo[1, pl.ds(0, left_rows + SCAN_CHUNK), :] = x_ref[pl.ds(n_rows - SCAN_CHUNK - left_rows, left_rows + SCAN_CHUNK), :]
    for t in range(CONV_WIDTH - 1 - CONV_LEFT):
        halo[1, pl.ds(left_rows + SCAN_CHUNK + t * N_SEG, N_SEG), :] = from_next_segment(
            x_ref[pl.ds(t * N_SEG, N_SEG), :])

    lam = lam_ref[...]
    softplus_neg = jnp.maximum(-lam, 0.0) + jnp.log1p(jnp.exp(-jnp.abs(lam)))
    half_rate = (0.5 * RG_LRU_C) * softplus_neg

    def gates(base, tap):
        xr = convb_ref[...] + sum(convw_ref[k:k + 1, :] * tap(k) for k in range(CONV_WIDTH))
        th = jnp.tanh(jnp.dot(xr.astype(BF16), wcat_ref[...], preferred_element_type=F32) + bcat_ref[...])
        half_x = 0.5 * xr
        for d, (a_ref, b_ref) in enumerate(((a_f, b_f), (a_b, b_b))):
            o = 2 * d * LANES
            neg_log_a = (th[:, o:o + LANES] + 1.0) * half_rate[d:d + 1, :]
            a = jnp.exp2(neg_log_a * (-LOG2_E))
            one_minus_a2 = jnp.tanh(neg_log_a) * (a * a + 1.0)
            root = one_minus_a2 * lax.rsqrt(jnp.maximum(one_minus_a2, F32_TINY))
            a_ref[pl.ds(base, SCAN_CHUNK), :] = a
            b_ref[pl.ds(base, SCAN_CHUNK), :] = root * ((th[:, o + LANES:o + 2 * LANES] + 1.0) * half_x)

    def inner_chunk(c, carry):
        base = pl.multiple_of(c * SCAN_CHUNK, SCAN_CHUNK)
        gates(base, lambda k: x_ref[pl.ds(base + (k - CONV_LEFT) * N_SEG, SCAN_CHUNK), :])
        return carry

    gates(0, lambda k: halo[0, pl.ds(k * N_SEG, SCAN_CHUNK), :])
    gates(n_rows - SCAN_CHUNK, lambda k: halo[1, pl.ds(k * N_SEG, SCAN_CHUNK), :])
    lax.fori_loop(1, n_chunks - 1, inner_chunk, 0, unroll=7)

    def scan_step(t, carry):
        h_f, p_f, h_b, p_b = carry
        rf = pl.ds(pl.multiple_of(t * N_SEG, N_SEG), N_SEG)
        rb = pl.ds(pl.multiple_of((seg_len - 1 - t) * N_SEG, N_SEG), N_SEG)
        a = a_f[rf, :]
        h_f = a * h_f + b_f[rf, :]
        p_f = a * p_f
        b_f[rf, :] = h_f
        a_f[rf, :] = p_f
        a = a_b[rb, :]
        h_b = a * h_b + b_b[rb, :]
        p_b = a * p_b
        b_b[rb, :] = h_b
        a_b[rb, :] = p_b
        return h_f, p_f, h_b, p_b

    zeros = jnp.zeros((N_SEG, LANES), F32)
    ones = jnp.ones((N_SEG, LANES), F32)
    h_f, p_f, h_b, p_b = lax.fori_loop(0, seg_len, scan_step, (zeros, ones, zeros, ones), unroll=4)

    edge[0] = h_f
    edge[1] = p_f
    edge[2] = h_b
    edge[3] = p_b
    edge[4, 0:1, :] = jnp.zeros((1, LANES), F32)
    edge[5, N_SEG - 1:N_SEG, :] = jnp.zeros((1, LANES), F32)
    for s in range(1, N_SEG):
        edge[4, s:s + 1, :] = edge[0, s - 1:s, :] + edge[1, s - 1:s, :] * edge[4, s - 1:s, :]
        r = N_SEG - 1 - s
        edge[5, r:r + 1, :] = edge[2, r + 1:r + 2, :] + edge[3, r + 1:r + 2, :] * edge[5, r + 1:r + 2, :]
    blocks = SCAN_CHUNK // N_SEG
    in_f = jnp.concatenate([edge[4]] * blocks, axis=0)
    in_b = jnp.concatenate([edge[5]] * blocks, axis=0)

    def out_chunk(c, carry):
        rows = pl.ds(pl.multiple_of(c * SCAN_CHUNK, SCAN_CHUNK), SCAN_CHUNK)
        o_ref[rows, :] = (b_f[rows, :] + a_f[rows, :] * in_f) + (b_b[rows, :] + a_b[rows, :] * in_b)
        return carry

    lax.fori_loop(0, n_rows // SCAN_CHUNK, out_chunk, 0, unroll=2)


def _rglru(xr, conv_w, conv_b, w_a, b_a, w_x, b_x, lam):
    batch, _, n_rows, _ = xr.shape
    w_cat = (0.5 * jnp.concatenate([w_a[0], w_x[0], w_a[1], w_x[1]], axis=-1)).astype(BF16)
    per_head = lambda v: v.reshape(RNN_HEADS, 1, RNN_HEAD_DIM)
    b_cat = 0.5 * jnp.concatenate([per_head(b_a[0]), per_head(b_x[0]), per_head(b_a[1]), per_head(b_x[1])],
                                  axis=-1)
    col = lambda rows: pl.BlockSpec((rows, RNN_HEAD_DIM), lambda b, h: (0, h))
    seq_col = pl.BlockSpec((None, None, n_rows, RNN_HEAD_DIM), lambda b, h: (b, h, 0, 0))
    halo_rows = SCAN_CHUNK + (CONV_WIDTH - 1) * N_SEG
    return pl.pallas_call(
        _rglru_kernel,
        grid=(batch, RNN_HEADS),
        in_specs=[seq_col, col(CONV_WIDTH), col(1),
                  pl.BlockSpec((None, RNN_HEAD_DIM, 4 * RNN_HEAD_DIM), lambda b, h: (h, 0, 0)),
                  pl.BlockSpec((None, 1, 4 * RNN_HEAD_DIM), lambda b, h: (h, 0, 0)),
                  col(2)],
        out_specs=seq_col,
        out_shape=jax.ShapeDtypeStruct((batch, RNN_HEADS, n_rows, RNN_HEAD_DIM), F32),
        scratch_shapes=[pltpu.VMEM((2, halo_rows, LANES), F32)]
                       + [pltpu.VMEM((n_rows, LANES), F32)] * 4
                       + [pltpu.VMEM((6, N_SEG, LANES), F32)],
        compiler_params=_params("parallel", "parallel"),
        name="rglru",
    )(xr, conv_w, conv_b.reshape(1, -1), w_cat, b_cat, lam)


def _odd_out_kernel(y_ref, g_ref, x_ref, gate_ref, w_f32_ref, lng_ref, lnb_ref, o_ref, w_ref):
    _cast_weight_once(w_f32_ref, w_ref)
    seg_len = y_ref.shape[1]
    head_rows = seg_len * SEGS_PER_TILE
    y_rows = y_ref.reshape(RNN_HEADS * head_rows, LANES)
    gated = []
    for s in range(SEGS_PER_TILE):
        y = jnp.concatenate([y_rows[pl.ds(hd * head_rows + s, seg_len, stride=SEGS_PER_TILE), :]
                             for hd in range(RNN_HEADS)], axis=1)
        g = g_ref[s * seg_len:(s + 1) * seg_len, :].astype(F32)
        gated.append((y * (g * _sigmoid(g))).astype(BF16))
    out = jnp.dot(jnp.concatenate(gated, axis=0), w_ref[...], preferred_element_type=F32)
    z = DEEPNORM_ALPHA * x_ref[...] + gate_ref[...] * out
    o_ref[...] = _layer_norm_rows(z, lng_ref[...], lnb_ref[...])


def _odd_out_proj(y, g, x, gate, w_out, ln_g, ln_b):
    batch, seq, _ = x.shape
    seg_len = seq // N_SEG
    tile_rows = SEGS_PER_TILE * seg_len
    const = lambda shape: pl.BlockSpec(shape, lambda b, i: (0,) * len(shape))
    return pl.pallas_call(
        _odd_out_kernel,
        grid=(batch, seq // tile_rows),
        in_specs=[pl.BlockSpec((None, RNN_HEADS, seg_len, SEGS_PER_TILE, RNN_HEAD_DIM),
                               lambda b, i: (b, 0, 0, i, 0)),
                  pl.BlockSpec((None, tile_rows, RNN_WIDTH), lambda b, i: (b, i, 0)),
                  pl.BlockSpec((None, tile_rows, D_MODEL), lambda b, i: (b, i, 0)),
                  pl.BlockSpec((None, 1, D_MODEL), lambda b, i: (b, 0, 0)),
                  _resident((RNN_WIDTH, D_MODEL)), const((1, D_MODEL)), const((1, D_MODEL))],
        out_specs=pl.BlockSpec((None, tile_rows, D_MODEL), lambda b, i: (b, i, 0)),
        out_shape=jax.ShapeDtypeStruct((batch, seq, D_MODEL), F32),
        scratch_shapes=[pltpu.VMEM((RNN_WIDTH, D_MODEL), BF16)],
        compiler_params=_params("arbitrary", "arbitrary"),
        name="odd_out_proj",
    )(y, g, x, gate, w_out, ln_g.reshape(1, -1), ln_b.reshape(1, -1))


def kernel(x, c, positions, ada_w, ada_b, ln_g, ln_b, ev_w_in, ev_w_out, ev_sink, ev_sg_ln_g, ev_sg_ln_b,
           ev_sg_w, ev_sg_b, od_w_in, od_conv_w, od_conv_b, od_w_a, od_b_a, od_w_x, od_b_x, od_lam, od_w_out):
    batch, seq, d_model = x.shape
    seg_len = seq // N_SEG
    assert d_model == D_MODEL and seq % ROW_TILE == 0 and seq == N_SEG * seg_len
    assert N_SEG % SEGS_PER_TILE == 0 and seq % SCAN_CHUNK == 0 and SCAN_CHUNK % N_SEG == 0
    assert ada_w.shape[0] == DEPTH == 2

    mod = _modulation(c, ada_w, ada_b)
    shift, scale, gate = (mod[:, :, i * D_MODEL:(i + 1) * D_MODEL].reshape(DEPTH, batch, 1, D_MODEL)
                          for i in range(3))

    q, k, v, su, sv, g = _even_in_proj(x, shift[0], scale[0], positions, ev_w_in[0])
    x1 = _even_mixer(q, k, v, su, sv, g, x, gate[0], ev_sink[0], ev_sg_ln_g[0], ev_sg_ln_b[0],
                     ev_sg_w[0], ev_sg_b[0], ev_w_out[0], ln_g[0], ln_b[0])

    xr, g1 = _odd_in_proj(x1, shift[1], scale[1], od_w_in[0])
    y = _rglru(xr.reshape(batch, RNN_HEADS, seq, RNN_HEAD_DIM), od_conv_w[0], od_conv_b[0], od_w_a[0],
               od_b_a[0], od_w_x[0], od_b_x[0], od_lam[0])
    return _odd_out_proj(y.reshape(batch, RNN_HEADS, seg_len, N_SEG, RNN_HEAD_DIM), g1, x1, gate[1],
                         od_w_out[0], ln_g[1], ln_b[1])
```

```python
import math

import numpy as np
import jax
import jax.numpy as jnp
from jax import lax
from jax.experimental import pallas as pl
from jax.experimental.pallas import tpu as pltpu

D_MODEL = 1024
DEPTH = 2
HEAD_DIM = 64
N_Q_HEADS = 8
N_KV_HEADS = 2
Q_PER_KV = N_Q_HEADS // N_KV_HEADS
ATTN_WIDTH = N_Q_HEADS * HEAD_DIM
KV_WIDTH = N_KV_HEADS * HEAD_DIM
ATTN_BLOCK = 128
ROPE_DIM = HEAD_DIM // 4
ROPE_HALF = ROPE_DIM // 2
ROPE_THETA = 500000.0
N_SG_GROUPS = 8
SG_GROUP_DIM = 64
SG_WIDTH = N_SG_GROUPS * SG_GROUP_DIM
SG_CHUNK = 128
EVEN_GATE_WIDTH = ATTN_WIDTH + SG_WIDTH
EVEN_IN_WIDTH = ATTN_WIDTH + 2 * KV_WIDTH + 2 * SG_WIDTH + EVEN_GATE_WIDTH
RNN_WIDTH = D_MODEL
RNN_HEADS = 8
RNN_HEAD_DIM = RNN_WIDTH // RNN_HEADS
CONV_WIDTH = 4
CONV_LEFT = 2
RG_LRU_C = 8.0
DEEPNORM_ALPHA = (2 * DEPTH) ** 0.25
LN_EPS = 1e-5
NEG_INF = -1e30
LOG2_E = math.log2(math.e)
F32_TINY = float(np.finfo(np.float32).tiny)

LANES = 128
SUBLANES = 8
VMEM_LIMIT = 56 * 1024 * 1024

ROW_TILE = 512
N_SEG = 32
SEGS_PER_TILE = SUBLANES
SCAN_CHUNK = 256
WEIGHT_CAST_ROWS = 128

BF16 = jnp.bfloat16
F32 = jnp.float32


def _sigmoid(x):
    return 0.5 * jnp.tanh(0.5 * x) + 0.5


def _layer_norm_rows(z, g, b):
    mu = jnp.mean(z, axis=-1, keepdims=True)
    d = z - mu
    var = jnp.mean(d * d, axis=-1, keepdims=True)
    return d * lax.rsqrt(var + LN_EPS) * g + b


def _params(*semantics):
    return pltpu.CompilerParams(dimension_semantics=semantics, vmem_limit_bytes=VMEM_LIMIT)


def _cast_weight_once(w_ref, w_bf16):
    @pl.when((pl.program_id(0) == 0) & (pl.program_id(1) == 0))
    def _():
        for r in range(0, w_ref.shape[0], WEIGHT_CAST_ROWS):
            w_bf16[r:r + WEIGHT_CAST_ROWS, :] = w_ref[r:r + WEIGHT_CAST_ROWS, :].astype(BF16)


def _resident(shape):
    return pl.BlockSpec(shape, lambda b, i: (0,) * len(shape), pipeline_mode=pl.Buffered(1))


def _mod_kernel(c_ref, w_ref, b_ref, o_ref):
    c = c_ref[...]
    cond = c * _sigmoid(c)
    pad_rows = o_ref.shape[0] - cond.shape[0]
    if pad_rows:
        cond = jnp.concatenate([cond, jnp.zeros((pad_rows, cond.shape[1]), F32)], axis=0)
    bias = b_ref[pl.ds(pl.program_id(0), 1), :]
    o_ref[...] = jnp.dot(cond.astype(BF16), w_ref[...].astype(BF16), preferred_element_type=F32) + bias


def _modulation(c, ada_w, ada_b):
    batch = c.shape[0]
    rows = -(-batch // SUBLANES) * SUBLANES
    n_col = 3 * D_MODEL // D_MODEL
    return pl.pallas_call(
        _mod_kernel,
        grid=(DEPTH, n_col),
        in_specs=[
            pl.BlockSpec((batch, D_MODEL), lambda l, j: (0, 0)),
            pl.BlockSpec((None, D_MODEL, D_MODEL), lambda l, j: (l, 0, j)),
            pl.BlockSpec((DEPTH, D_MODEL), lambda l, j: (0, j)),
        ],
        out_specs=pl.BlockSpec((None, rows, D_MODEL), lambda l, j: (l, 0, j)),
        out_shape=jax.ShapeDtypeStruct((DEPTH, rows, 3 * D_MODEL), F32),
        compiler_params=_params("parallel", "parallel"),
        name="adaln_mod",
    )(c, ada_w, ada_b)


MOD_SHIFT, MOD_SCALE, MOD_GATE = range(3)
EVEN_LAYER, ODD_LAYER = 0, 1


def _mod_spec(mod, layer, part):
    return pl.BlockSpec((None, mod.shape[1], D_MODEL), lambda b, i: (layer, 0, part))


def _batch_row(ref):
    return ref[pl.ds(pl.program_id(0), 1), :]


def _even_in_kernel(x_ref, shift_ref, scale_ref, pos_ref, freq_ref, sign_ref, w_f32_ref,
                    q_ref, k_ref, v_ref, su_ref, sv_ref, g_ref, w_ref):
    _cast_weight_once(w_f32_ref, w_ref)
    lane = lax.broadcasted_iota(jnp.int32, (1, LANES), 1)
    first_half = (lane % HEAD_DIM) < ROPE_HALF
    diagonal = lax.broadcasted_iota(jnp.int32, (LANES, LANES), 0) == lax.broadcasted_iota(jnp.int32, (LANES, LANES), 1)
    batch_ids = lax.broadcasted_iota(jnp.int32, (pos_ref.shape[0], LANES), 0)
    q_scale = HEAD_DIM ** -0.5 * LOG2_E

    half_rows = x_ref.shape[0] // 2
    for r in range(2):
        rows = slice(r * half_rows, (r + 1) * half_rows)
        h = (x_ref[rows, :] * (1.0 + _batch_row(scale_ref)) + _batch_row(shift_ref)).astype(BF16)
        p = jnp.dot(h, w_ref[...], preferred_element_type=F32)

        pos_cols = []
        for j in range(r * half_rows // LANES, (r + 1) * half_rows // LANES):
            pos_all = pos_ref[:, j * LANES:(j + 1) * LANES].astype(F32)
            pos_row = jnp.sum(jnp.where(batch_ids == pl.program_id(0), pos_all, 0.0), axis=0, keepdims=True)
            pos_cols.append(jnp.sum(jnp.where(diagonal, pos_row, 0.0), axis=1, keepdims=True))
        ang = jnp.concatenate(pos_cols, axis=0) * freq_ref[...]
        cs = jnp.cos(ang)
        sn = jnp.sin(ang) * sign_ref[...]

        def rotary(t):
            partner = jnp.where(first_half, pltpu.roll(t, LANES - ROPE_HALF, 1), pltpu.roll(t, ROPE_HALF, 1))
            return t * cs + partner * sn

        for j in range(ATTN_WIDTH // LANES):
            q_ref[rows, j * LANES:(j + 1) * LANES] = (
                rotary(p[:, j * LANES:(j + 1) * LANES]) * q_scale).astype(BF16)
        o = ATTN_WIDTH
        k_ref[rows, :] = rotary(p[:, o:o + KV_WIDTH]).astype(BF16)
        o += KV_WIDTH
        v_ref[rows, :] = p[:, o:o + KV_WIDTH].astype(BF16)
        o += KV_WIDTH
        su_ref[rows, :] = p[:, o:o + SG_WIDTH].astype(BF16)
        o += SG_WIDTH
        sv_ref[rows, :] = p[:, o:o + SG_WIDTH].astype(BF16)
        o += SG_WIDTH
        g_ref[rows, :] = p[:, o:o + EVEN_GATE_WIDTH].astype(BF16)


def _even_in_proj(x, mod, positions, w_in):
    batch, seq, _ = x.shape
    d = np.arange(LANES) % HEAD_DIM
    inv_freq = np.power(np.float64(ROPE_THETA), -np.arange(ROPE_HALF) / ROPE_HALF)
    freq_lane = jnp.asarray(np.where(d < ROPE_DIM, inv_freq[d % ROPE_HALF], 0.0), dtype=F32).reshape(1, LANES)
    sign_lane = jnp.asarray(np.where(d < ROPE_HALF, -1.0, np.where(d < ROPE_DIM, 1.0, 0.0)),
                            dtype=F32).reshape(1, LANES)
    row = lambda width: pl.BlockSpec((None, ROW_TILE, width), lambda b, i: (b, i, 0))
    const = lambda shape: pl.BlockSpec(shape, lambda b, i: (0,) * len(shape))
    widths = (ATTN_WIDTH, KV_WIDTH, KV_WIDTH, SG_WIDTH, SG_WIDTH, EVEN_GATE_WIDTH)
    return pl.pallas_call(
        _even_in_kernel,
        grid=(batch, seq // ROW_TILE),
        in_specs=[row(D_MODEL), _mod_spec(mod, EVEN_LAYER, MOD_SHIFT), _mod_spec(mod, EVEN_LAYER, MOD_SCALE),
                  pl.BlockSpec((batch, ROW_TILE), lambda b, i: (0, i)),
                  const((1, LANES)), const((1, LANES)), _resident((D_MODEL, EVEN_IN_WIDTH))],
        out_specs=[row(w) for w in widths],
        out_shape=[jax.ShapeDtypeStruct((batch, seq, w), BF16) for w in widths],
        scratch_shapes=[pltpu.VMEM((D_MODEL, EVEN_IN_WIDTH), BF16)],
        compiler_params=_params("arbitrary", "arbitrary"),
        name="even_in_proj",
    )(x, mod, mod, positions, freq_lane, sign_lane, w_in)


def _even_mix_kernel(sink_ref, q_ref, k_ref, v_ref, su_ref, sv_ref, g_ref, x_ref, gate_ref,
                     avg_ref, avg2_ref, sgg_ref, sgb_ref, sgw_ref, sgbias_ref, wout_f32_ref, lng_ref, lnb_ref,
                     o_ref, kdup, vsplit, y_scr, wout_ref):
    _cast_weight_once(wout_f32_ref, wout_ref)
    seq = k_ref.shape[0]
    n_blocks = seq // ATTN_BLOCK
    sub_blocks = ROW_TILE // ATTN_BLOCK
    tile = pl.program_id(1)
    lane = lax.broadcasted_iota(jnp.int32, (1, LANES), 1)
    left = lane < HEAD_DIM

    @pl.when(tile == 0)
    def _():
        def build(c, carry):
            rows = pl.ds(pl.multiple_of(c * ROW_TILE, ROW_TILE), ROW_TILE)
            k = k_ref[rows, :].astype(F32)
            k_sw = pltpu.roll(k, HEAD_DIM, 1)
            kdup[rows, 0:LANES] = jnp.where(left, k, k_sw).astype(BF16)
            kdup[rows, LANES:2 * LANES] = jnp.where(left, k_sw, k).astype(BF16)
            v = v_ref[rows, :].astype(F32)
            v_sw = pltpu.roll(v, HEAD_DIM, 1)
            vsplit[rows, 0 * LANES:1 * LANES] = jnp.where(left, v, 0.0).astype(BF16)
            vsplit[rows, 1 * LANES:2 * LANES] = jnp.where(left, 0.0, v_sw).astype(BF16)
            vsplit[rows, 2 * LANES:3 * LANES] = jnp.where(left, v_sw, 0.0).astype(BF16)
            vsplit[rows, 3 * LANES:4 * LANES] = jnp.where(left, 0.0, v).astype(BF16)
            return carry
        lax.fori_loop(0, seq // ROW_TILE, build, 0)

    qi = lax.broadcasted_iota(jnp.int32, (ATTN_BLOCK, ATTN_BLOCK), 0)
    kj = lax.broadcasted_iota(jnp.int32, (ATTN_BLOCK, ATTN_BLOCK), 1)
    ones_left = jnp.broadcast_to(jnp.where(left, 1.0, 0.0).astype(BF16), (3 * ATTN_BLOCK, LANES))
    ones_right = jnp.broadcast_to(jnp.where(left, 0.0, 1.0).astype(BF16), (3 * ATTN_BLOCK, LANES))
    avg = avg_ref[...]
    sink_cols = [jnp.concatenate([jnp.full((ATTN_BLOCK, LANES), sink_ref[h] * LOG2_E, F32)
                                  for h in (4 * hk, 4 * hk + 2, 4 * hk + 1, 4 * hk + 3)], axis=0)
                 for hk in range(N_KV_HEADS)]

    def sub_block(j, carry):
        blk = tile * sub_blocks + j
        prev = jnp.maximum(blk - 1, 0)
        nxt = jnp.minimum(blk + 1, n_blocks - 1)
        rows = pl.ds(pl.multiple_of(j * ATTN_BLOCK, ATTN_BLOCK), ATTN_BLOCK)
        bias_prev = jnp.where((kj >= qi) & (blk > 0), 0.0, NEG_INF)
        bias_next = jnp.where((kj <= qi) & (blk < n_blocks - 1), 0.0, NEG_INF)
        bias_prev = jnp.concatenate([bias_prev] * Q_PER_KV, axis=0)
        bias_next = jnp.concatenate([bias_next] * Q_PER_KV, axis=0)

        def band(ref):
            return jnp.concatenate(
                [ref[pl.ds(pl.multiple_of(b * ATTN_BLOCK, ATTN_BLOCK), ATTN_BLOCK), :] for b in (prev, blk, nxt)],
                axis=0)

        k_band = band(kdup)
        v_band = band(vsplit)
        n_sg = SG_WIDTH // LANES
        sg_cols = [slice(c * LANES, (c + 1) * LANES) for c in range(n_sg)]

        def silu_gate(cols):
            g_c = g_ref[rows, cols].astype(F32)
            return g_c * _sigmoid(g_c)

        scores = []
        for hk in range(N_KV_HEADS):
            q_cols = [q_ref[rows, (2 * hk + c) * LANES:(2 * hk + c + 1) * LANES] for c in range(2)]
            zero = jnp.zeros_like(q_cols[0])
            lhs = jnp.concatenate([jnp.where(left, qc, zero) for qc in q_cols]
                                  + [jnp.where(left, zero, qc) for qc in q_cols], axis=0)
            scores.append(lax.dot_general(lhs, k_band[:, hk * LANES:(hk + 1) * LANES], (((1,), (1,)), ((), ())),
                                          preferred_element_type=F32))
        sv_cols = [sv_ref[rows, cols] for cols in sg_cols]
        means = [jnp.dot(v_c, avg, preferred_element_type=F32) for v_c in sv_cols]

        probs, row_max = [], []
        for hk in range(N_KV_HEADS):
            s = scores[hk]
            s0 = s[:, 0:ATTN_BLOCK] + bias_prev
            s1 = s[:, ATTN_BLOCK:2 * ATTN_BLOCK]
            s2 = s[:, 2 * ATTN_BLOCK:3 * ATTN_BLOCK] + bias_next
            m = jnp.max(jnp.maximum(jnp.maximum(s0, s1), s2), axis=-1, keepdims=True)
            m = jnp.maximum(jnp.broadcast_to(m, sink_cols[hk].shape), sink_cols[hk])
            probs.append(jnp.concatenate([jnp.exp2(t - m).astype(BF16) for t in (s0, s1, s2)], axis=1))
            row_max.append(m)
        devs, sq_parts = [], []
        for v_c, mean in zip(sv_cols, means):
            dev = v_c.astype(F32) - mean
            sq = dev * dev
            sq_hi = sq.astype(BF16)
            sq_lo = (sq - sq_hi.astype(F32)).astype(BF16)
            devs.append(dev)
            sq_parts.append(jnp.concatenate([sq_hi, sq_lo], axis=1))

        pv = []
        for hk in range(N_KV_HEADS):
            p = probs[hk]
            lhs = jnp.concatenate(
                [jnp.concatenate([p[c * ATTN_BLOCK:(c + 1) * ATTN_BLOCK], p[(2 + c) * ATTN_BLOCK:(3 + c) * ATTN_BLOCK]],
                                 axis=1) for c in range(2)], axis=0)
            rhs = jnp.concatenate(
                [jnp.concatenate([v_band[:, (2 * hk) * LANES:(2 * hk + 1) * LANES], ones_left], axis=1),
                 jnp.concatenate([v_band[:, (2 * hk + 1) * LANES:(2 * hk + 2) * LANES], ones_right], axis=1)],
                axis=0)
            pv.append(jnp.dot(lhs, rhs, preferred_element_type=F32))
        variances = [jnp.dot(sq, avg2_ref[...], preferred_element_type=F32) for sq in sq_parts]

        for hk in range(N_KV_HEADS):
            for c in range(2):
                even, odd = slice(c * ATTN_BLOCK, (c + 1) * ATTN_BLOCK), slice((2 + c) * ATTN_BLOCK, (3 + c) * ATTN_BLOCK)
                sink_term = jnp.exp2(jnp.where(left, sink_cols[hk][even] - row_max[hk][even],
                                               sink_cols[hk][odd] - row_max[hk][odd]))
                o = pv[hk][c * ATTN_BLOCK:(c + 1) * ATTN_BLOCK]
                pair = o[:, 0:LANES] * (1.0 / (o[:, LANES:2 * LANES] + sink_term))
                cols = slice((2 * hk + c) * LANES, (2 * hk + c + 1) * LANES)
                y_scr[rows, cols] = (pair * silu_gate(cols)).astype(BF16)

        for c, cols in enumerate(sg_cols):
            vn = devs[c] * lax.rsqrt(variances[c] + LN_EPS) * sgg_ref[:, cols] + sgb_ref[:, cols]
            stacked = jnp.concatenate([jnp.where(left, vn, 0.0), jnp.where(left, 0.0, vn)],
                                      axis=0).astype(BF16)
            mixed = jnp.dot(sgw_ref[c], stacked, preferred_element_type=F32) + sgbias_ref[:, cols]
            y_sg = su_ref[rows, cols].astype(F32) * mixed
            gc = slice(ATTN_WIDTH + c * LANES, ATTN_WIDTH + (c + 1) * LANES)
            y_scr[rows, gc] = (y_sg * silu_gate(gc)).astype(BF16)
        return carry

    lax.fori_loop(0, sub_blocks, sub_block, 0, unroll=True)

    for r in range(2):
        half_rows = slice(r * (ROW_TILE // 2), (r + 1) * (ROW_TILE // 2))
        out = jnp.dot(y_scr[half_rows, :], wout_ref[...], preferred_element_type=F32)
        z = DEEPNORM_ALPHA * x_ref[half_rows, :] + _batch_row(gate_ref) * out
        o_ref[half_rows, :] = _layer_norm_rows(z, lng_ref[EVEN_LAYER:EVEN_LAYER + 1, :], lnb_ref[EVEN_LAYER:EVEN_LAYER + 1, :])


def _even_mixer(q, k, v, su, sv, g, x, mod, sink, sg_ln_g, sg_ln_b, sg_w, sg_b, w_out, ln_g, ln_b):
    batch, seq, _ = x.shape
    lane = np.arange(LANES)
    avg = jnp.asarray((lane[:, None] // SG_GROUP_DIM == lane[None, :] // SG_GROUP_DIM) / SG_GROUP_DIM, dtype=BF16)
    w_pairs = sg_w.reshape(N_SG_GROUPS // 2, 2, SG_CHUNK, SG_CHUNK).transpose(0, 2, 1, 3)
    w_pairs = w_pairs.reshape(N_SG_GROUPS // 2, SG_CHUNK, 2 * SG_CHUNK).astype(BF16)
    bias_full = jnp.repeat(sg_b.T, SG_GROUP_DIM, axis=1)
    row = lambda width: pl.BlockSpec((None, ROW_TILE, width), lambda b, i: (b, i, 0))
    full_seq = pl.BlockSpec((None, seq, KV_WIDTH), lambda b, i: (b, 0, 0))
    const = lambda shape: pl.BlockSpec(shape, lambda b, i: (0,) * len(shape))
    return pl.pallas_call(
        _even_mix_kernel,
        grid=(batch, seq // ROW_TILE),
        in_specs=[
            pl.BlockSpec(memory_space=pltpu.SMEM),
            row(ATTN_WIDTH), full_seq, full_seq, row(SG_WIDTH), row(SG_WIDTH), row(EVEN_GATE_WIDTH),
            row(D_MODEL), _mod_spec(mod, EVEN_LAYER, MOD_GATE),
            const((LANES, LANES)), const((2 * LANES, LANES)), const((1, SG_WIDTH)), const((1, SG_WIDTH)),
            const((N_SG_GROUPS // 2, SG_CHUNK, 2 * SG_CHUNK)), const((SG_CHUNK, SG_WIDTH)),
            _resident((EVEN_GATE_WIDTH, D_MODEL)), const(ln_g.shape), const(ln_b.shape),
        ],
        out_specs=row(D_MODEL),
        out_shape=jax.ShapeDtypeStruct((batch, seq, D_MODEL), F32),
        scratch_shapes=[pltpu.VMEM((seq, 2 * KV_WIDTH), BF16), pltpu.VMEM((seq, 4 * KV_WIDTH), BF16),
                        pltpu.VMEM((ROW_TILE, EVEN_GATE_WIDTH), BF16),
                        pltpu.VMEM((EVEN_GATE_WIDTH, D_MODEL), BF16)],
        compiler_params=_params("arbitrary", "arbitrary"),
        name="even_mixer",
    )(sink, q, k, v, su, sv, g, x, mod, avg, jnp.concatenate([avg, avg], axis=0), sg_ln_g.reshape(1, -1), sg_ln_b.reshape(1, -1),
      w_pairs, bias_full, w_out, ln_g, ln_b)


def _odd_in_kernel(x_ref, shift_ref, scale_ref, w_f32_ref, xr_ref, g_ref, w_ref):
    _cast_weight_once(w_f32_ref, w_ref)
    seg_len = xr_ref.shape[1]
    head_rows = seg_len * SEGS_PER_TILE
    xr_rows = xr_ref.reshape(RNN_HEADS * head_rows, LANES)
    halves = 2
    for r in range(halves):
        first = r * (SEGS_PER_TILE // halves)
        rows = slice(first * seg_len, (first + SEGS_PER_TILE // halves) * seg_len)
        h = (x_ref[rows, :] * (1.0 + _batch_row(scale_ref)) + _batch_row(shift_ref)).astype(BF16)
        p = jnp.dot(h, w_ref[...], preferred_element_type=F32)
        for s in range(SEGS_PER_TILE // halves):
            for hd in range(RNN_HEADS):
                xr_rows[pl.ds(hd * head_rows + first + s, seg_len, stride=SEGS_PER_TILE), :] = (
                    p[s * seg_len:(s + 1) * seg_len, hd * LANES:(hd + 1) * LANES])
        g_ref[rows, :] = p[:, RNN_WIDTH:2 * RNN_WIDTH].astype(BF16)


def _odd_in_proj(x, mod, w_in):
    batch, seq, _ = x.shape
    seg_len = seq // N_SEG
    tile_rows = SEGS_PER_TILE * seg_len
    out_spec = pl.BlockSpec((None, RNN_HEADS, seg_len, SEGS_PER_TILE, RNN_HEAD_DIM), lambda b, i: (b, 0, 0, i, 0))
    out_shape = jax.ShapeDtypeStruct((batch, RNN_HEADS, seg_len, N_SEG, RNN_HEAD_DIM), F32)
    return pl.pallas_call(
        _odd_in_kernel,
        grid=(batch, seq // tile_rows),
        in_specs=[pl.BlockSpec((None, tile_rows, D_MODEL), lambda b, i: (b, i, 0)),
                  _mod_spec(mod, ODD_LAYER, MOD_SHIFT), _mod_spec(mod, ODD_LAYER, MOD_SCALE),
                  _resident((D_MODEL, 2 * RNN_WIDTH))],
        out_specs=[out_spec, pl.BlockSpec((None, tile_rows, RNN_WIDTH), lambda b, i: (b, i, 0))],
        out_shape=[out_shape, jax.ShapeDtypeStruct((batch, seq, RNN_WIDTH), BF16)],
        scratch_shapes=[pltpu.VMEM((D_MODEL, 2 * RNN_WIDTH), BF16)],
        compiler_params=_params("arbitrary", "arbitrary"),
        name="odd_in_proj",
    )(x, mod, mod, w_in)


def _rglru_kernel(x_ref, convw_ref, convb_ref, wcat_ref, bcat_ref, lam_ref, o_ref,
                  halo, a_f, b_f, a_b, b_b, edge):
    n_rows = x_ref.shape[0]
    seg_len = n_rows // N_SEG
    n_chunks = n_rows // SCAN_CHUNK
    left_rows = CONV_LEFT * N_SEG
    right_rows = (CONV_WIDTH - 1 - CONV_LEFT) * N_SEG
    seg = lax.broadcasted_iota(jnp.int32, (N_SEG, LANES), 0)

    def from_prev_segment(blk):
        return jnp.where(seg == 0, 0.0, pltpu.roll(blk, 1, 0))

    def from_next_segment(blk):
        return jnp.where(seg == N_SEG - 1, 0.0, pltpu.roll(blk, N_SEG - 1, 0))

    for t in range(CONV_LEFT):
        halo[0, pl.ds(t * N_SEG, N_SEG), :] = from_prev_segment(
            x_ref[pl.ds((seg_len - CONV_LEFT + t) * N_SEG, N_SEG), :])
    halo[0, pl.ds(left_rows, SCAN_CHUNK + right_rows), :] = x_ref[pl.ds(0, SCAN_CHUNK + right_rows), :]
    halo[1, pl.ds(0, left_rows + SCAN_CHUNK), :] = x_ref[pl.ds(n_rows - SCAN_CHUNK - left_rows, left_rows + SCAN_CHUNK), :]
    for t in range(CONV_WIDTH - 1 - CONV_LEFT):
        halo[1, pl.ds(left_rows + SCAN_CHUNK + t * N_SEG, N_SEG), :] = from_next_segment(
            x_ref[pl.ds(t * N_SEG, N_SEG), :])

    lam = lam_ref[...]
    softplus_neg = jnp.maximum(-lam, 0.0) + jnp.log1p(jnp.exp(-jnp.abs(lam)))
    half_rate = (0.5 * RG_LRU_C) * softplus_neg

    def gates(base, tap):
        xr = convb_ref[...] + sum(convw_ref[k:k + 1, :] * tap(k) for k in range(CONV_WIDTH))
        th = jnp.tanh(jnp.dot(xr.astype(BF16), wcat_ref[...], preferred_element_type=F32) + bcat_ref[...])
        half_x = 0.5 * xr
        for d, (a_ref, b_ref) in enumerate(((a_f, b_f), (a_b, b_b))):
            o = 2 * d * LANES
            neg_log_a = (th[:, o:o + LANES] + 1.0) * half_rate[d:d + 1, :]
            a = jnp.exp2(neg_log_a * (-LOG2_E))
            one_minus_a2 = jnp.tanh(neg_log_a) * (a * a + 1.0)
            root = one_minus_a2 * lax.rsqrt(jnp.maximum(one_minus_a2, F32_TINY))
            a_ref[pl.ds(base, SCAN_CHUNK), :] = a
            b_ref[pl.ds(base, SCAN_CHUNK), :] = root * ((th[:, o + LANES:o + 2 * LANES] + 1.0) * half_x)

    def inner_chunk(c, carry):
        base = pl.multiple_of(c * SCAN_CHUNK, SCAN_CHUNK)
        gates(base, lambda k: x_ref[pl.ds(base + (k - CONV_LEFT) * N_SEG, SCAN_CHUNK), :])
        return carry

    gates(0, lambda k: halo[0, pl.ds(k * N_SEG, SCAN_CHUNK), :])
    gates(n_rows - SCAN_CHUNK, lambda k: halo[1, pl.ds(k * N_SEG, SCAN_CHUNK), :])
    lax.fori_loop(1, n_chunks - 1, inner_chunk, 0, unroll=7)

    def scan_step(t, carry):
        h_f, p_f, h_b, p_b = carry
        rf = pl.ds(pl.multiple_of(t * N_SEG, N_SEG), N_SEG)
        rb = pl.ds(pl.multiple_of((seg_len - 1 - t) * N_SEG, N_SEG), N_SEG)
        a = a_f[rf, :]
        h_f = a * h_f + b_f[rf, :]
        p_f = a * p_f
        b_f[rf, :] = h_f
        a_f[rf, :] = p_f
        a = a_b[rb, :]
        h_b = a * h_b + b_b[rb, :]
        p_b = a * p_b
        b_b[rb, :] = h_b
        a_b[rb, :] = p_b
        return h_f, p_f, h_b, p_b

    zeros = jnp.zeros((N_SEG, LANES), F32)
    ones = jnp.ones((N_SEG, LANES), F32)
    h_f, p_f, h_b, p_b = lax.fori_loop(0, seg_len, scan_step, (zeros, ones, zeros, ones), unroll=4)

    edge[0] = h_f
    edge[1] = p_f
    edge[2] = h_b
    edge[3] = p_b
    edge[4, 0:1, :] = jnp.zeros((1, LANES), F32)
    edge[5, N_SEG - 1:N_SEG, :] = jnp.zeros((1, LANES), F32)
    for s in range(1, N_SEG):
        edge[4, s:s + 1, :] = edge[0, s - 1:s, :] + edge[1, s - 1:s, :] * edge[4, s - 1:s, :]
        r = N_SEG - 1 - s
        edge[5, r:r + 1, :] = edge[2, r + 1:r + 2, :] + edge[3, r + 1:r + 2, :] * edge[5, r + 1:r + 2, :]
    blocks = SCAN_CHUNK // N_SEG
    in_f = jnp.concatenate([edge[4]] * blocks, axis=0)
    in_b = jnp.concatenate([edge[5]] * blocks, axis=0)

    def out_chunk(c, carry):
        rows = pl.ds(pl.multiple_of(c * SCAN_CHUNK, SCAN_CHUNK), SCAN_CHUNK)
        o_ref[rows, :] = (b_f[rows, :] + a_f[rows, :] * in_f) + (b_b[rows, :] + a_b[rows, :] * in_b)
        return carry

    lax.fori_loop(0, n_rows // SCAN_CHUNK, out_chunk, 0, unroll=2)


def _rglru(xr, conv_w, conv_b, w_a, b_a, w_x, b_x, lam):
    batch, _, n_rows, _ = xr.shape
    w_cat = (0.5 * jnp.concatenate([w_a[0], w_x[0], w_a[1], w_x[1]], axis=-1)).astype(BF16)
    per_head = lambda v: v.reshape(RNN_HEADS, 1, RNN_HEAD_DIM)
    b_cat = 0.5 * jnp.concatenate([per_head(b_a[0]), per_head(b_x[0]), per_head(b_a[1]), per_head(b_x[1])],
                                  axis=-1)
    col = lambda rows: pl.BlockSpec((rows, RNN_HEAD_DIM), lambda b, h: (0, h))
    seq_col = pl.BlockSpec((None, None, n_rows, RNN_HEAD_DIM), lambda b, h: (b, h, 0, 0))
    halo_rows = SCAN_CHUNK + (CONV_WIDTH - 1) * N_SEG
    return pl.pallas_call(
        _rglru_kernel,
        grid=(batch, RNN_HEADS),
        in_specs=[seq_col, col(CONV_WIDTH), col(1),
                  pl.BlockSpec((None, RNN_HEAD_DIM, 4 * RNN_HEAD_DIM), lambda b, h: (h, 0, 0)),
                  pl.BlockSpec((None, 1, 4 * RNN_HEAD_DIM), lambda b, h: (h, 0, 0)),
                  col(2)],
        out_specs=seq_col,
        out_shape=jax.ShapeDtypeStruct((batch, RNN_HEADS, n_rows, RNN_HEAD_DIM), F32),
        scratch_shapes=[pltpu.VMEM((2, halo_rows, LANES), F32)]
                       + [pltpu.VMEM((n_rows, LANES), F32)] * 4
                       + [pltpu.VMEM((6, N_SEG, LANES), F32)],
        compiler_params=_params("parallel", "parallel"),
        name="rglru",
    )(xr, conv_w, conv_b.reshape(1, -1), w_cat, b_cat, lam)


def _odd_out_kernel(y_ref, g_ref, x_ref, gate_ref, w_f32_ref, lng_ref, lnb_ref, o_ref, w_ref):
    _cast_weight_once(w_f32_ref, w_ref)
    seg_len = y_ref.shape[1]
    head_rows = seg_len * SEGS_PER_TILE
    y_rows = y_ref.reshape(RNN_HEADS * head_rows, LANES)
    gated = []
    for s in range(SEGS_PER_TILE):
        y = jnp.concatenate([y_rows[pl.ds(hd * head_rows + s, seg_len, stride=SEGS_PER_TILE), :]
                             for hd in range(RNN_HEADS)], axis=1)
        g = g_ref[s * seg_len:(s + 1) * seg_len, :].astype(F32)
        gated.append((y * (g * _sigmoid(g))).astype(BF16))
    out = jnp.dot(jnp.concatenate(gated, axis=0), w_ref[...], preferred_element_type=F32)
    z = DEEPNORM_ALPHA * x_ref[...] + _batch_row(gate_ref) * out
    o_ref[...] = _layer_norm_rows(z, lng_ref[ODD_LAYER:ODD_LAYER + 1, :], lnb_ref[ODD_LAYER:ODD_LAYER + 1, :])


def _odd_out_proj(y, g, x, mod, w_out, ln_g, ln_b):
    batch, seq, _ = x.shape
    seg_len = seq // N_SEG
    tile_rows = SEGS_PER_TILE * seg_len
    const = lambda shape: pl.BlockSpec(shape, lambda b, i: (0,) * len(shape))
    return pl.pallas_call(
        _odd_out_kernel,
        grid=(batch, seq // tile_rows),
        in_specs=[pl.BlockSpec((None, RNN_HEADS, seg_len, SEGS_PER_TILE, RNN_HEAD_DIM),
                               lambda b, i: (b, 0, 0, i, 0)),
                  pl.BlockSpec((None, tile_rows, RNN_WIDTH), lambda b, i: (b, i, 0)),
                  pl.BlockSpec((None, tile_rows, D_MODEL), lambda b, i: (b, i, 0)),
                  _mod_spec(mod, ODD_LAYER, MOD_GATE),
                  _resident((RNN_WIDTH, D_MODEL)), const(ln_g.shape), const(ln_b.shape)],
        out_specs=pl.BlockSpec((None, tile_rows, D_MODEL), lambda b, i: (b, i, 0)),
        out_shape=jax.ShapeDtypeStruct((batch, seq, D_MODEL), F32),
        scratch_shapes=[pltpu.VMEM((RNN_WIDTH, D_MODEL), BF16)],
        compiler_params=_params("arbitrary", "arbitrary"),
        name="odd_out_proj",
    )(y, g, x, mod, w_out, ln_g, ln_b)


def kernel(x, c, positions, ada_w, ada_b, ln_g, ln_b, ev_w_in, ev_w_out, ev_sink, ev_sg_ln_g, ev_sg_ln_b,
           ev_sg_w, ev_sg_b, od_w_in, od_conv_w, od_conv_b, od_w_a, od_b_a, od_w_x, od_b_x, od_lam, od_w_out):
    batch, seq, d_model = x.shape
    seg_len = seq // N_SEG
    assert d_model == D_MODEL and seq % ROW_TILE == 0 and seq == N_SEG * seg_len
    assert N_SEG % SEGS_PER_TILE == 0 and seq % SCAN_CHUNK == 0 and SCAN_CHUNK % N_SEG == 0
    assert ada_w.shape[0] == DEPTH == 2

    mod = _modulation(c, ada_w, ada_b)

    q, k, v, su, sv, g = _even_in_proj(x, mod, positions, ev_w_in[0])
    x1 = _even_mixer(q, k, v, su, sv, g, x, mod, ev_sink[0], ev_sg_ln_g[0], ev_sg_ln_b[0],
                     ev_sg_w[0], ev_sg_b[0], ev_w_out[0], ln_g, ln_b)

    xr, g1 = _odd_in_proj(x1, mod, od_w_in[0])
    y = _rglru(xr.reshape(batch, RNN_HEADS, seq, RNN_HEAD_DIM), od_conv_w[0], od_conv_b[0], od_w_a[0],
               od_b_a[0], od_w_x[0], od_b_x[0], od_lam[0])
    return _odd_out_proj(y.reshape(batch, RNN_HEADS, seg_len, N_SEG, RNN_HEAD_DIM), g1, x1, mod,
                         od_w_out[0], ln_g, ln_b)
```

```python
import math

import numpy as np
import jax
import jax.numpy as jnp
from jax import lax
from jax.experimental import pallas as pl
from jax.experimental.pallas import tpu as pltpu

D_MODEL = 1024
DEPTH = 2
HEAD_DIM = 64
N_Q_HEADS = 8
N_KV_HEADS = 2
Q_PER_KV = N_Q_HEADS // N_KV_HEADS
ATTN_WIDTH = N_Q_HEADS * HEAD_DIM
KV_WIDTH = N_KV_HEADS * HEAD_DIM
ATTN_BLOCK = 128
ROPE_DIM = HEAD_DIM // 4
ROPE_HALF = ROPE_DIM // 2
ROPE_THETA = 500000.0
N_SG_GROUPS = 8
SG_GROUP_DIM = 64
SG_WIDTH = N_SG_GROUPS * SG_GROUP_DIM
SG_CHUNK = 128
EVEN_GATE_WIDTH = ATTN_WIDTH + SG_WIDTH
EVEN_IN_WIDTH = ATTN_WIDTH + 2 * KV_WIDTH + 2 * SG_WIDTH + EVEN_GATE_WIDTH
RNN_WIDTH = D_MODEL
RNN_HEADS = 8
RNN_HEAD_DIM = RNN_WIDTH // RNN_HEADS
CONV_WIDTH = 4
CONV_LEFT = 2
RG_LRU_C = 8.0
DEEPNORM_ALPHA = (2 * DEPTH) ** 0.25
LN_EPS = 1e-5
NEG_INF = -1e30
LOG2_E = math.log2(math.e)
F32_TINY = float(np.finfo(np.float32).tiny)

LANES = 128
SUBLANES = 8
VMEM_LIMIT = 56 * 1024 * 1024

ROW_TILE = 512
MIX_TILE = 512
N_SEG = 32
SEGS_PER_TILE = SUBLANES
SCAN_CHUNK = 256
WEIGHT_CAST_ROWS = 128
SUM_ROWS = 16

BF16 = jnp.bfloat16
F32 = jnp.float32


def _sigmoid(x):
    return 0.5 * jnp.tanh(0.5 * x) + 0.5


def _layer_norm_rows(z, g, b):
    mu = jnp.mean(z, axis=-1, keepdims=True)
    d = z - mu
    var = jnp.mean(d * d, axis=-1, keepdims=True)
    return d * lax.rsqrt(var + LN_EPS) * g + b


def _params(*semantics):
    return pltpu.CompilerParams(dimension_semantics=semantics, vmem_limit_bytes=VMEM_LIMIT)


def _cast_weight_once(w_ref, w_bf16):
    @pl.when((pl.program_id(0) == 0) & (pl.program_id(1) == 0))
    def _():
        for r in range(0, w_ref.shape[0], WEIGHT_CAST_ROWS):
            w_bf16[r:r + WEIGHT_CAST_ROWS, :] = w_ref[r:r + WEIGHT_CAST_ROWS, :].astype(BF16)


def _resident(shape):
    return pl.BlockSpec(shape, lambda b, i: (0,) * len(shape), pipeline_mode=pl.Buffered(1))


def _mod_kernel(c_ref, w_ref, b_ref, o_ref):
    c = c_ref[...]
    cond = c * _sigmoid(c)
    pad_rows = o_ref.shape[0] - cond.shape[0]
    if pad_rows:
        cond = jnp.concatenate([cond, jnp.zeros((pad_rows, cond.shape[1]), F32)], axis=0)
    bias = b_ref[pl.ds(pl.program_id(0), 1), :]
    o_ref[...] = jnp.dot(cond.astype(BF16), w_ref[...].astype(BF16), preferred_element_type=F32) + bias


def _modulation(c, ada_w, ada_b):
    batch = c.shape[0]
    rows = -(-batch // SUBLANES) * SUBLANES
    n_col = 3 * D_MODEL // D_MODEL
    return pl.pallas_call(
        _mod_kernel,
        grid=(DEPTH, n_col),
        in_specs=[
            pl.BlockSpec((batch, D_MODEL), lambda l, j: (0, 0)),
            pl.BlockSpec((None, D_MODEL, D_MODEL), lambda l, j: (l, 0, j)),
            pl.BlockSpec((DEPTH, D_MODEL), lambda l, j: (0, j)),
        ],
        out_specs=pl.BlockSpec((None, rows, D_MODEL), lambda l, j: (l, 0, j)),
        out_shape=jax.ShapeDtypeStruct((DEPTH, rows, 3 * D_MODEL), F32),
        compiler_params=_params("parallel", "parallel"),
        name="adaln_mod",
    )(c, ada_w, ada_b)


MOD_SHIFT, MOD_SCALE, MOD_GATE = range(3)
EVEN_LAYER, ODD_LAYER = 0, 1


def _mod_spec(mod, layer, part):
    return pl.BlockSpec((None, mod.shape[1], D_MODEL), lambda b, i: (layer, 0, part))


def _batch_row(ref):
    return ref[pl.ds(pl.program_id(0), 1), :]


def _even_in_kernel(x_ref, shift_ref, scale_ref, pos_ref, freq_ref, sign_ref, w_f32_ref,
                    q_ref, k_ref, vt_ref, su_ref, sv_ref, g_ref, w_ref):
    _cast_weight_once(w_f32_ref, w_ref)
    lane = lax.broadcasted_iota(jnp.int32, (1, LANES), 1)
    first_half = (lane % HEAD_DIM) < ROPE_HALF
    diagonal = lax.broadcasted_iota(jnp.int32, (LANES, LANES), 0) == lax.broadcasted_iota(jnp.int32, (LANES, LANES), 1)
    batch_ids = lax.broadcasted_iota(jnp.int32, (pos_ref.shape[0], LANES), 0)
    q_scale = HEAD_DIM ** -0.5 * LOG2_E

    half_rows = x_ref.shape[0] // 2
    for r in range(2):
        rows = slice(r * half_rows, (r + 1) * half_rows)
        h = (x_ref[rows, :] * (1.0 + _batch_row(scale_ref)) + _batch_row(shift_ref)).astype(BF16)
        p = jnp.dot(h, w_ref[...], preferred_element_type=F32)

        pos_cols = []
        for j in range(r * half_rows // LANES, (r + 1) * half_rows // LANES):
            pos_all = pos_ref[:, j * LANES:(j + 1) * LANES].astype(F32)
            pos_row = jnp.sum(jnp.where(batch_ids == pl.program_id(0), pos_all, 0.0), axis=0, keepdims=True)
            pos_cols.append(jnp.sum(jnp.where(diagonal, pos_row, 0.0), axis=1, keepdims=True))
        ang = jnp.concatenate(pos_cols, axis=0) * freq_ref[...]
        cs = jnp.cos(ang)
        sn = jnp.sin(ang) * sign_ref[...]

        def rotary(t):
            partner = jnp.where(first_half, pltpu.roll(t, LANES - ROPE_HALF, 1), pltpu.roll(t, ROPE_HALF, 1))
            return t * cs + partner * sn

        for j in range(ATTN_WIDTH // LANES):
            q_ref[rows, j * LANES:(j + 1) * LANES] = (
                rotary(p[:, j * LANES:(j + 1) * LANES]) * q_scale).astype(BF16)
        o = ATTN_WIDTH
        k_ref[rows, :] = rotary(p[:, o:o + KV_WIDTH]).astype(BF16)
        o += KV_WIDTH
        for j in range(half_rows // ATTN_BLOCK):
            v_blk = p[j * ATTN_BLOCK:(j + 1) * ATTN_BLOCK, o:o + KV_WIDTH]
            vt_ref[r * (half_rows // ATTN_BLOCK) + j] = v_blk.T.astype(BF16)
        o += KV_WIDTH
        su_ref[rows, :] = p[:, o:o + SG_WIDTH].astype(BF16)
        o += SG_WIDTH
        sv_ref[rows, :] = p[:, o:o + SG_WIDTH].astype(BF16)
        o += SG_WIDTH
        g_ref[rows, :] = p[:, o:o + EVEN_GATE_WIDTH].astype(BF16)


def _even_in_proj(x, mod, positions, w_in):
    batch, seq, _ = x.shape
    d = np.arange(LANES) % HEAD_DIM
    inv_freq = np.power(np.float64(ROPE_THETA), -np.arange(ROPE_HALF) / ROPE_HALF)
    freq_lane = jnp.asarray(np.where(d < ROPE_DIM, inv_freq[d % ROPE_HALF], 0.0), dtype=F32).reshape(1, LANES)
    sign_lane = jnp.asarray(np.where(d < ROPE_HALF, -1.0, np.where(d < ROPE_DIM, 1.0, 0.0)),
                            dtype=F32).reshape(1, LANES)
    row = lambda width: pl.BlockSpec((None, ROW_TILE, width), lambda b, i: (b, i, 0))
    const = lambda shape: pl.BlockSpec(shape, lambda b, i: (0,) * len(shape))
    widths = (ATTN_WIDTH, KV_WIDTH, None, SG_WIDTH, SG_WIDTH, EVEN_GATE_WIDTH)
    blocks_per_tile = ROW_TILE // ATTN_BLOCK
    vt_spec = pl.BlockSpec((None, blocks_per_tile, KV_WIDTH, ATTN_BLOCK), lambda b, i: (b, i, 0, 0))
    vt_shape = jax.ShapeDtypeStruct((batch, seq // ATTN_BLOCK, KV_WIDTH, ATTN_BLOCK), BF16)
    return pl.pallas_call(
        _even_in_kernel,
        grid=(batch, seq // ROW_TILE),
        in_specs=[row(D_MODEL), _mod_spec(mod, EVEN_LAYER, MOD_SHIFT), _mod_spec(mod, EVEN_LAYER, MOD_SCALE),
                  pl.BlockSpec((batch, ROW_TILE), lambda b, i: (0, i)),
                  const((1, LANES)), const((1, LANES)), _resident((D_MODEL, EVEN_IN_WIDTH))],
        out_specs=[vt_spec if w is None else row(w) for w in widths],
        out_shape=[vt_shape if w is None else jax.ShapeDtypeStruct((batch, seq, w), BF16) for w in widths],
        scratch_shapes=[pltpu.VMEM((D_MODEL, EVEN_IN_WIDTH), BF16)],
        compiler_params=_params("arbitrary", "arbitrary"),
        name="even_in_proj",
    )(x, mod, mod, positions, freq_lane, sign_lane, w_in)


def _even_mix_kernel(sink_ref, q_ref, k_ref, vt_ref, su_ref, sv_ref, g_ref, x_ref, gate_ref,
                     avg_ref, avg2_ref, sgg_ref, sgb_ref, sgw_ref, sgbias_ref, wout_f32_ref, lng_ref, lnb_ref,
                     o_ref, kdup, y_scr, wout_ref):
    _cast_weight_once(wout_f32_ref, wout_ref)
    seq = k_ref.shape[0]
    n_blocks = seq // ATTN_BLOCK
    sub_blocks = MIX_TILE // ATTN_BLOCK
    tile = pl.program_id(1)
    lane = lax.broadcasted_iota(jnp.int32, (1, LANES), 1)
    left = lane < HEAD_DIM

    @pl.when(tile == 0)
    def _():
        def build(c, carry):
            rows = pl.ds(pl.multiple_of(c * MIX_TILE, MIX_TILE), MIX_TILE)
            k = k_ref[rows, :].astype(F32)
            k_sw = pltpu.roll(k, HEAD_DIM, 1)
            kdup[rows, 0:LANES] = jnp.where(left, k, k_sw).astype(BF16)
            kdup[rows, LANES:2 * LANES] = jnp.where(left, k_sw, k).astype(BF16)
            return carry
        lax.fori_loop(0, seq // MIX_TILE, build, 0)

    kj = lax.broadcasted_iota(jnp.int32, (ATTN_BLOCK, ATTN_BLOCK), 0)
    qi = lax.broadcasted_iota(jnp.int32, (ATTN_BLOCK, ATTN_BLOCK), 1)
    ones_rows = jnp.ones((SUM_ROWS, 3 * ATTN_BLOCK), BF16)
    avg = avg_ref[...]
    sink_rows = [jnp.concatenate([jnp.full((1, ATTN_BLOCK), sink_ref[h] * LOG2_E, F32)
                                  for h in (4 * hk, 4 * hk + 2, 4 * hk + 1, 4 * hk + 3)], axis=1)
                 for hk in range(N_KV_HEADS)]

    def sub_block(j, carry):
        blk = tile * sub_blocks + j
        prev = jnp.maximum(blk - 1, 0)
        nxt = jnp.minimum(blk + 1, n_blocks - 1)
        rows = pl.ds(pl.multiple_of(j * ATTN_BLOCK, ATTN_BLOCK), ATTN_BLOCK)
        bias_prev = jnp.where((kj >= qi) & (blk > 0), 0.0, NEG_INF)
        bias_next = jnp.where((kj <= qi) & (blk < n_blocks - 1), 0.0, NEG_INF)
        bias_prev = jnp.concatenate([bias_prev] * Q_PER_KV, axis=1)
        bias_next = jnp.concatenate([bias_next] * Q_PER_KV, axis=1)

        def band(ref):
            return jnp.concatenate(
                [ref[pl.ds(pl.multiple_of(b * ATTN_BLOCK, ATTN_BLOCK), ATTN_BLOCK), :] for b in (prev, blk, nxt)],
                axis=0)

        k_band = band(kdup)
        n_sg = SG_WIDTH // LANES
        sg_cols = [slice(c * LANES, (c + 1) * LANES) for c in range(n_sg)]

        def silu_gate(cols):
            g_c = g_ref[rows, cols].astype(F32)
            return g_c * _sigmoid(g_c)

        scores = []
        for hk in range(N_KV_HEADS):
            q_cols = [q_ref[rows, (2 * hk + c) * LANES:(2 * hk + c + 1) * LANES] for c in range(2)]
            zero = jnp.zeros_like(q_cols[0])
            lhs = jnp.concatenate([jnp.where(left, qc, zero) for qc in q_cols]
                                  + [jnp.where(left, zero, qc) for qc in q_cols], axis=0)
            scores.append(lax.dot_general(k_band[:, hk * LANES:(hk + 1) * LANES], lhs, (((1,), (1,)), ((), ())),
                                          preferred_element_type=F32))
        sv_cols = [sv_ref[rows, cols] for cols in sg_cols]
        means = [jnp.dot(v_c, avg, preferred_element_type=F32) for v_c in sv_cols]

        probs, row_max = [], []
        for hk in range(N_KV_HEADS):
            s = scores[hk]
            s0 = s[0:ATTN_BLOCK] + bias_prev
            s1 = s[ATTN_BLOCK:2 * ATTN_BLOCK]
            s2 = s[2 * ATTN_BLOCK:3 * ATTN_BLOCK] + bias_next
            m = jnp.max(jnp.maximum(jnp.maximum(s0, s1), s2), axis=0, keepdims=True)
            m = jnp.maximum(m, sink_rows[hk])
            probs.append(jnp.concatenate([jnp.exp2(t - m).astype(BF16) for t in (s0, s1, s2)], axis=0))
            row_max.append(m)
        devs, sq_parts = [], []
        for v_c, mean in zip(sv_cols, means):
            dev = v_c.astype(F32) - mean
            sq = dev * dev
            sq_hi = sq.astype(BF16)
            sq_lo = (sq - sq_hi.astype(F32)).astype(BF16)
            devs.append(dev)
            sq_parts.append(jnp.concatenate([sq_hi, sq_lo], axis=1))

        pv = []
        for hk in range(N_KV_HEADS):
            v_t = jnp.concatenate([vt_ref[b, hk * HEAD_DIM:(hk + 1) * HEAD_DIM, :] for b in (prev, blk, nxt)], axis=1)
            pv.append(jnp.dot(jnp.concatenate([v_t, ones_rows], axis=0), probs[hk],
                              preferred_element_type=F32))
        variances = [jnp.dot(sq, avg2_ref[...], preferred_element_type=F32) for sq in sq_parts]

        for hk in range(N_KV_HEADS):
            denom = pv[hk][HEAD_DIM:HEAD_DIM + 1] + jnp.exp2(sink_rows[hk] - row_max[hk])
            o_t = pv[hk][0:HEAD_DIM] * (1.0 / denom)
            for c in range(2):
                pair_t = jnp.concatenate([o_t[:, c * ATTN_BLOCK:(c + 1) * ATTN_BLOCK],
                                          o_t[:, (2 + c) * ATTN_BLOCK:(3 + c) * ATTN_BLOCK]], axis=0)
                cols = slice((2 * hk + c) * LANES, (2 * hk + c + 1) * LANES)
                y_scr[rows, cols] = (pair_t.T * silu_gate(cols)).astype(BF16)

        for c, cols in enumerate(sg_cols):
            vn = devs[c] * lax.rsqrt(variances[c] + LN_EPS) * sgg_ref[:, cols] + sgb_ref[:, cols]
            stacked = jnp.concatenate([jnp.where(left, vn, 0.0), jnp.where(left, 0.0, vn)],
                                      axis=0).astype(BF16)
            mixed = jnp.dot(sgw_ref[c], stacked, preferred_element_type=F32) + sgbias_ref[:, cols]
            y_sg = su_ref[rows, cols].astype(F32) * mixed
            gc = slice(ATTN_WIDTH + c * LANES, ATTN_WIDTH + (c + 1) * LANES)
            y_scr[rows, gc] = (y_sg * silu_gate(gc)).astype(BF16)
        return carry

    lax.fori_loop(0, sub_blocks, sub_block, 0, unroll=True)

    for r in range(2):
        half_rows = slice(r * (MIX_TILE // 2), (r + 1) * (MIX_TILE // 2))
        out = jnp.dot(y_scr[half_rows, :], wout_ref[...], preferred_element_type=F32)
        z = DEEPNORM_ALPHA * x_ref[half_rows, :] + _batch_row(gate_ref) * out
        o_ref[half_rows, :] = _layer_norm_rows(z, lng_ref[EVEN_LAYER:EVEN_LAYER + 1, :], lnb_ref[EVEN_LAYER:EVEN_LAYER + 1, :])


def _even_mixer(q, k, vt, su, sv, g, x, mod, sink, sg_ln_g, sg_ln_b, sg_w, sg_b, w_out, ln_g, ln_b):
    batch, seq, _ = x.shape
    lane = np.arange(LANES)
    avg = jnp.asarray((lane[:, None] // SG_GROUP_DIM == lane[None, :] // SG_GROUP_DIM) / SG_GROUP_DIM, dtype=BF16)
    w_pairs = sg_w.reshape(N_SG_GROUPS // 2, 2, SG_CHUNK, SG_CHUNK).transpose(0, 2, 1, 3)
    w_pairs = w_pairs.reshape(N_SG_GROUPS // 2, SG_CHUNK, 2 * SG_CHUNK).astype(BF16)
    bias_full = jnp.repeat(sg_b.T, SG_GROUP_DIM, axis=1)
    row = lambda width: pl.BlockSpec((None, MIX_TILE, width), lambda b, i: (b, i, 0))
    full_seq = pl.BlockSpec((None, seq, KV_WIDTH), lambda b, i: (b, 0, 0))
    const = lambda shape: pl.BlockSpec(shape, lambda b, i: (0,) * len(shape))
    return pl.pallas_call(
        _even_mix_kernel,
        grid=(batch, seq // MIX_TILE),
        in_specs=[
            pl.BlockSpec(memory_space=pltpu.SMEM),
            row(ATTN_WIDTH), full_seq, pl.BlockSpec((None,) + vt.shape[1:], lambda b, i: (b, 0, 0, 0)), row(SG_WIDTH), row(SG_WIDTH), row(EVEN_GATE_WIDTH),
            row(D_MODEL), _mod_spec(mod, EVEN_LAYER, MOD_GATE),
            const((LANES, LANES)), const((2 * LANES, LANES)), const((1, SG_WIDTH)), const((1, SG_WIDTH)),
            const((N_SG_GROUPS // 2, SG_CHUNK, 2 * SG_CHUNK)), const((SG_CHUNK, SG_WIDTH)),
            _resident((EVEN_GATE_WIDTH, D_MODEL)), const(ln_g.shape), const(ln_b.shape),
        ],
        out_specs=row(D_MODEL),
        out_shape=jax.ShapeDtypeStruct((batch, seq, D_MODEL), F32),
        scratch_shapes=[pltpu.VMEM((seq, 2 * KV_WIDTH), BF16),
                        pltpu.VMEM((MIX_TILE, EVEN_GATE_WIDTH), BF16),
                        pltpu.VMEM((EVEN_GATE_WIDTH, D_MODEL), BF16)],
        compiler_params=_params("arbitrary", "arbitrary"),
        name="even_mixer",
    )(sink, q, k, vt, su, sv, g, x, mod, avg, jnp.concatenate([avg, avg], axis=0), sg_ln_g.reshape(1, -1), sg_ln_b.reshape(1, -1),
      w_pairs, bias_full, w_out, ln_g, ln_b)


def _odd_in_kernel(x_ref, shift_ref, scale_ref, w_f32_ref, xr_ref, g_ref, w_ref):
    _cast_weight_once(w_f32_ref, w_ref)
    seg_len = xr_ref.shape[1]
    head_rows = seg_len * SEGS_PER_TILE
    xr_rows = xr_ref.reshape(RNN_HEADS * head_rows, LANES)
    halves = 2
    for r in range(halves):
        first = r * (SEGS_PER_TILE // halves)
        rows = slice(first * seg_len, (first + SEGS_PER_TILE // halves) * seg_len)
        h = (x_ref[rows, :] * (1.0 + _batch_row(scale_ref)) + _batch_row(shift_ref)).astype(BF16)
        p = jnp.dot(h, w_ref[...], preferred_element_type=F32)
        for s in range(SEGS_PER_TILE // halves):
            for hd in range(RNN_HEADS):
                xr_rows[pl.ds(hd * head_rows + first + s, seg_len, stride=SEGS_PER_TILE), :] = (
                    p[s * seg_len:(s + 1) * seg_len, hd * LANES:(hd + 1) * LANES])
        g_ref[rows, :] = p[:, RNN_WIDTH:2 * RNN_WIDTH].astype(BF16)


def _odd_in_proj(x, mod, w_in):
    batch, seq, _ = x.shape
    seg_len = seq // N_SEG
    tile_rows = SEGS_PER_TILE * seg_len
    out_spec = pl.BlockSpec((None, RNN_HEADS, seg_len, SEGS_PER_TILE, RNN_HEAD_DIM), lambda b, i: (b, 0, 0, i, 0))
    out_shape = jax.ShapeDtypeStruct((batch, RNN_HEADS, seg_len, N_SEG, RNN_HEAD_DIM), F32)
    return pl.pallas_call(
        _odd_in_kernel,
        grid=(batch, seq // tile_rows),
        in_specs=[pl.BlockSpec((None, tile_rows, D_MODEL), lambda b, i: (b, i, 0)),
                  _mod_spec(mod, ODD_LAYER, MOD_SHIFT), _mod_spec(mod, ODD_LAYER, MOD_SCALE),
                  _resident((D_MODEL, 2 * RNN_WIDTH))],
        out_specs=[out_spec, pl.BlockSpec((None, tile_rows, RNN_WIDTH), lambda b, i: (b, i, 0))],
        out_shape=[out_shape, jax.ShapeDtypeStruct((batch, seq, RNN_WIDTH), BF16)],
        scratch_shapes=[pltpu.VMEM((D_MODEL, 2 * RNN_WIDTH), BF16)],
        compiler_params=_params("arbitrary", "arbitrary"),
        name="odd_in_proj",
    )(x, mod, mod, w_in)


def _rglru_kernel(x_ref, convw_ref, convb_ref, wcat_ref, bcat_ref, lam_ref, o_ref,
                  halo, a_f, b_f, a_b, b_b, edge):
    n_rows = x_ref.shape[0]
    seg_len = n_rows // N_SEG
    n_chunks = n_rows // SCAN_CHUNK
    left_rows = CONV_LEFT * N_SEG
    right_rows = (CONV_WIDTH - 1 - CONV_LEFT) * N_SEG
    seg = lax.broadcasted_iota(jnp.int32, (N_SEG, LANES), 0)

    def from_prev_segment(blk):
        return jnp.where(seg == 0, 0.0, pltpu.roll(blk, 1, 0))

    def from_next_segment(blk):
        return jnp.where(seg == N_SEG - 1, 0.0, pltpu.roll(blk, N_SEG - 1, 0))

    for t in range(CONV_LEFT):
        halo[0, pl.ds(t * N_SEG, N_SEG), :] = from_prev_segment(
            x_ref[pl.ds((seg_len - CONV_LEFT + t) * N_SEG, N_SEG), :])
    halo[0, pl.ds(left_rows, SCAN_CHUNK + right_rows), :] = x_ref[pl.ds(0, SCAN_CHUNK + right_rows), :]
    halo[1, pl.ds(0, left_rows + SCAN_CHUNK), :] = x_ref[pl.ds(n_rows - SCAN_CHUNK - left_rows, left_rows + SCAN_CHUNK), :]
    for t in range(CONV_WIDTH - 1 - CONV_LEFT):
        halo[1, pl.ds(left_rows + SCAN_CHUNK + t * N_SEG, N_SEG), :] = from_next_segment(
            x_ref[pl.ds(t * N_SEG, N_SEG), :])

    lam = lam_ref[...]
    softplus_neg = jnp.maximum(-lam, 0.0) + jnp.log1p(jnp.exp(-jnp.abs(lam)))
    half_rate = (0.5 * RG_LRU_C) * softplus_neg

    def gates(base, tap):
        xr = convb_ref[...] + sum(convw_ref[k:k + 1, :] * tap(k) for k in range(CONV_WIDTH))
        th = jnp.tanh(jnp.dot(xr.astype(BF16), wcat_ref[...], preferred_element_type=F32) + bcat_ref[...])
        half_x = 0.5 * xr
        for d, (a_ref, b_ref) in enumerate(((a_f, b_f), (a_b, b_b))):
            o = 2 * d * LANES
            neg_log_a = (th[:, o:o + LANES] + 1.0) * half_rate[d:d + 1, :]
            a = jnp.exp2(neg_log_a * (-LOG2_E))
            one_minus_a2 = jnp.tanh(neg_log_a) * (a * a + 1.0)
            root = one_minus_a2 * lax.rsqrt(jnp.maximum(one_minus_a2, F32_TINY))
            a_ref[pl.ds(base, SCAN_CHUNK), :] = a
            b_ref[pl.ds(base, SCAN_CHUNK), :] = root * ((th[:, o + LANES:o + 2 * LANES] + 1.0) * half_x)

    def inner_chunk(c, carry):
        base = pl.multiple_of(c * SCAN_CHUNK, SCAN_CHUNK)
        gates(base, lambda k: x_ref[pl.ds(base + (k - CONV_LEFT) * N_SEG, SCAN_CHUNK), :])
        return carry

    gates(0, lambda k: halo[0, pl.ds(k * N_SEG, SCAN_CHUNK), :])
    gates(n_rows - SCAN_CHUNK, lambda k: halo[1, pl.ds(k * N_SEG, SCAN_CHUNK), :])
    lax.fori_loop(1, n_chunks - 1, inner_chunk, 0, unroll=7)

    def scan_step(t, carry):
        h_f, p_f, h_b, p_b = carry
        rf = pl.ds(pl.multiple_of(t * N_SEG, N_SEG), N_SEG)
        rb = pl.ds(pl.multiple_of((seg_len - 1 - t) * N_SEG, N_SEG), N_SEG)
        a = a_f[rf, :]
        h_f = a * h_f + b_f[rf, :]
        p_f = a * p_f
        b_f[rf, :] = h_f
        a_f[rf, :] = p_f
        a = a_b[rb, :]
        h_b = a * h_b + b_b[rb, :]
        p_b = a * p_b
        b_b[rb, :] = h_b
        a_b[rb, :] = p_b
        return h_f, p_f, h_b, p_b

    zeros = jnp.zeros((N_SEG, LANES), F32)
    ones = jnp.ones((N_SEG, LANES), F32)
    h_f, p_f, h_b, p_b = lax.fori_loop(0, seg_len, scan_step, (zeros, ones, zeros, ones), unroll=4)

    edge[0] = h_f
    edge[1] = p_f
    edge[2] = h_b
    edge[3] = p_b
    edge[4, 0:1, :] = jnp.zeros((1, LANES), F32)
    edge[5, N_SEG - 1:N_SEG, :] = jnp.zeros((1, LANES), F32)
    for s in range(1, N_SEG):
        edge[4, s:s + 1, :] = edge[0, s - 1:s, :] + edge[1, s - 1:s, :] * edge[4, s - 1:s, :]
        r = N_SEG - 1 - s
        edge[5, r:r + 1, :] = edge[2, r + 1:r + 2, :] + edge[3, r + 1:r + 2, :] * edge[5, r + 1:r + 2, :]
    blocks = SCAN_CHUNK // N_SEG
    in_f = jnp.concatenate([edge[4]] * blocks, axis=0)
    in_b = jnp.concatenate([edge[5]] * blocks, axis=0)

    def out_chunk(c, carry):
        rows = pl.ds(pl.multiple_of(c * SCAN_CHUNK, SCAN_CHUNK), SCAN_CHUNK)
        o_ref[rows, :] = (b_f[rows, :] + a_f[rows, :] * in_f) + (b_b[rows, :] + a_b[rows, :] * in_b)
        return carry

    lax.fori_loop(0, n_rows // SCAN_CHUNK, out_chunk, 0, unroll=2)


def _rglru(xr, conv_w, conv_b, w_a, b_a, w_x, b_x, lam):
    batch, _, n_rows, _ = xr.shape
    w_cat = (0.5 * jnp.concatenate([w_a[0], w_x[0], w_a[1], w_x[1]], axis=-1)).astype(BF16)
    per_head = lambda v: v.reshape(RNN_HEADS, 1, RNN_HEAD_DIM)
    b_cat = 0.5 * jnp.concatenate([per_head(b_a[0]), per_head(b_x[0]), per_head(b_a[1]), per_head(b_x[1])],
                                  axis=-1)
    col = lambda rows: pl.BlockSpec((rows, RNN_HEAD_DIM), lambda b, h: (0, h))
    seq_col = pl.BlockSpec((None, None, n_rows, RNN_HEAD_DIM), lambda b, h: (b, h, 0, 0))
    halo_rows = SCAN_CHUNK + (CONV_WIDTH - 1) * N_SEG
    return pl.pallas_call(
        _rglru_kernel,
        grid=(batch, RNN_HEADS),
        in_specs=[seq_col, col(CONV_WIDTH), col(1),
                  pl.BlockSpec((None, RNN_HEAD_DIM, 4 * RNN_HEAD_DIM), lambda b, h: (h, 0, 0)),
                  pl.BlockSpec((None, 1, 4 * RNN_HEAD_DIM), lambda b, h: (h, 0, 0)),
                  col(2)],
        out_specs=seq_col,
        out_shape=jax.ShapeDtypeStruct((batch, RNN_HEADS, n_rows, RNN_HEAD_DIM), F32),
        scratch_shapes=[pltpu.VMEM((2, halo_rows, LANES), F32)]
                       + [pltpu.VMEM((n_rows, LANES), F32)] * 4
                       + [pltpu.VMEM((6, N_SEG, LANES), F32)],
        compiler_params=_params("parallel", "parallel"),
        name="rglru",
    )(xr, conv_w, conv_b.reshape(1, -1), w_cat, b_cat, lam)


def _odd_out_kernel(y_ref, g_ref, x_ref, gate_ref, w_f32_ref, lng_ref, lnb_ref, o_ref, w_ref):
    _cast_weight_once(w_f32_ref, w_ref)
    seg_len = y_ref.shape[1]
    head_rows = seg_len * SEGS_PER_TILE
    y_rows = y_ref.reshape(RNN_HEADS * head_rows, LANES)
    gated = []
    for s in range(SEGS_PER_TILE):
        y = jnp.concatenate([y_rows[pl.ds(hd * head_rows + s, seg_len, stride=SEGS_PER_TILE), :]
                             for hd in range(RNN_HEADS)], axis=1)
        g = g_ref[s * seg_len:(s + 1) * seg_len, :].astype(F32)
        gated.append((y * (g * _sigmoid(g))).astype(BF16))
    out = jnp.dot(jnp.concatenate(gated, axis=0), w_ref[...], preferred_element_type=F32)
    z = DEEPNORM_ALPHA * x_ref[...] + _batch_row(gate_ref) * out
    o_ref[...] = _layer_norm_rows(z, lng_ref[ODD_LAYER:ODD_LAYER + 1, :], lnb_ref[ODD_LAYER:ODD_LAYER + 1, :])


def _odd_out_proj(y, g, x, mod, w_out, ln_g, ln_b):
    batch, seq, _ = x.shape
    seg_len = seq // N_SEG
    tile_rows = SEGS_PER_TILE * seg_len
    const = lambda shape: pl.BlockSpec(shape, lambda b, i: (0,) * len(shape))
    return pl.pallas_call(
        _odd_out_kernel,
        grid=(batch, seq // tile_rows),
        in_specs=[pl.BlockSpec((None, RNN_HEADS, seg_len, SEGS_PER_TILE, RNN_HEAD_DIM),
                               lambda b, i: (b, 0, 0, i, 0)),
                  pl.BlockSpec((None, tile_rows, RNN_WIDTH), lambda b, i: (b, i, 0)),
                  pl.BlockSpec((None, tile_rows, D_MODEL), lambda b, i: (b, i, 0)),
                  _mod_spec(mod, ODD_LAYER, MOD_GATE),
                  _resident((RNN_WIDTH, D_MODEL)), const(ln_g.shape), const(ln_b.shape)],
        out_specs=pl.BlockSpec((None, tile_rows, D_MODEL), lambda b, i: (b, i, 0)),
        out_shape=jax.ShapeDtypeStruct((batch, seq, D_MODEL), F32),
        scratch_shapes=[pltpu.VMEM((RNN_WIDTH, D_MODEL), BF16)],
        compiler_params=_params("arbitrary", "arbitrary"),
        name="odd_out_proj",
    )(y, g, x, mod, w_out, ln_g, ln_b)


def kernel(x, c, positions, ada_w, ada_b, ln_g, ln_b, ev_w_in, ev_w_out, ev_sink, ev_sg_ln_g, ev_sg_ln_b,
           ev_sg_w, ev_sg_b, od_w_in, od_conv_w, od_conv_b, od_w_a, od_b_a, od_w_x, od_b_x, od_lam, od_w_out):
    batch, seq, d_model = x.shape
    seg_len = seq // N_SEG
    assert d_model == D_MODEL and seq % ROW_TILE == 0 and seq % MIX_TILE == 0 and seq == N_SEG * seg_len
    assert N_SEG % SEGS_PER_TILE == 0 and seq % SCAN_CHUNK == 0 and SCAN_CHUNK % N_SEG == 0
    assert ada_w.shape[0] == DEPTH == 2

    mod = _modulation(c, ada_w, ada_b)

    q, k, vt, su, sv, g = _even_in_proj(x, mod, positions, ev_w_in[0])
    x1 = _even_mixer(q, k, vt, su, sv, g, x, mod, ev_sink[0], ev_sg_ln_g[0], ev_sg_ln_b[0],
                     ev_sg_w[0], ev_sg_b[0], ev_w_out[0], ln_g, ln_b)

    xr, g1 = _odd_in_proj(x1, mod, od_w_in[0])
    y = _rglru(xr.reshape(batch, RNN_HEADS, seq, RNN_HEAD_DIM), od_conv_w[0], od_conv_b[0], od_w_a[0],
               od_b_a[0], od_w_x[0], od_b_x[0], od_lam[0])
    return _odd_out_proj(y.reshape(batch, RNN_HEADS, seg_len, N_SEG, RNN_HEAD_DIM), g1, x1, mod,
                         od_w_out[0], ln_g, ln_b)
```

```python
import math

import numpy as np
import jax
import jax.numpy as jnp
from jax import lax
from jax.experimental import pallas as pl
from jax.experimental.pallas import tpu as pltpu

D_MODEL = 1024
DEPTH = 2
HEAD_DIM = 64
N_Q_HEADS = 8
N_KV_HEADS = 2
Q_PER_KV = N_Q_HEADS // N_KV_HEADS
ATTN_WIDTH = N_Q_HEADS * HEAD_DIM
KV_WIDTH = N_KV_HEADS * HEAD_DIM
ATTN_BLOCK = 128
ROPE_DIM = HEAD_DIM // 4
ROPE_HALF = ROPE_DIM // 2
ROPE_THETA = 500000.0
N_SG_GROUPS = 8
SG_GROUP_DIM = 64
SG_WIDTH = N_SG_GROUPS * SG_GROUP_DIM
SG_CHUNK = 128
EVEN_GATE_WIDTH = ATTN_WIDTH + SG_WIDTH
EVEN_IN_WIDTH = ATTN_WIDTH + 2 * KV_WIDTH + 2 * SG_WIDTH + EVEN_GATE_WIDTH
RNN_WIDTH = D_MODEL
RNN_HEADS = 8
RNN_HEAD_DIM = RNN_WIDTH // RNN_HEADS
CONV_WIDTH = 4
CONV_LEFT = 2
RG_LRU_C = 8.0
DEEPNORM_ALPHA = (2 * DEPTH) ** 0.25
LN_EPS = 1e-5
NEG_INF = -1e30
LOG2_E = math.log2(math.e)
F32_TINY = float(np.finfo(np.float32).tiny)

LANES = 128
SUBLANES = 8
VMEM_LIMIT = 56 * 1024 * 1024

ROW_TILE = 512
MIX_TILE = 512
N_SEG = 32
SEGS_PER_TILE = SUBLANES
SCAN_CHUNK = 256
WEIGHT_CAST_ROWS = 128
BIAS_ROWS = 2

BF16 = jnp.bfloat16
F32 = jnp.float32


def _sigmoid(x):
    return 0.5 * jnp.tanh(0.5 * x) + 0.5


def _layer_norm_rows(z, g, b):
    mu = jnp.mean(z, axis=-1, keepdims=True)
    d = z - mu
    var = jnp.mean(d * d, axis=-1, keepdims=True)
    return d * lax.rsqrt(var + LN_EPS) * g + b


def _params(*semantics):
    return pltpu.CompilerParams(dimension_semantics=semantics, vmem_limit_bytes=VMEM_LIMIT)


def _cast_weight_once(w_ref, w_bf16):
    @pl.when((pl.program_id(0) == 0) & (pl.program_id(1) == 0))
    def _():
        for r in range(0, w_ref.shape[0], WEIGHT_CAST_ROWS):
            w_bf16[r:r + WEIGHT_CAST_ROWS, :] = w_ref[r:r + WEIGHT_CAST_ROWS, :].astype(BF16)


def _resident(shape):
    return pl.BlockSpec(shape, lambda b, i: (0,) * len(shape), pipeline_mode=pl.Buffered(1))


def _mod_kernel(c_ref, w_ref, b_ref, o_ref):
    c = c_ref[...]
    cond = c * _sigmoid(c)
    pad_rows = o_ref.shape[0] - cond.shape[0]
    if pad_rows:
        cond = jnp.concatenate([cond, jnp.zeros((pad_rows, cond.shape[1]), F32)], axis=0)
    bias = b_ref[pl.ds(pl.program_id(0), 1), :]
    o_ref[...] = jnp.dot(cond.astype(BF16), w_ref[...].astype(BF16), preferred_element_type=F32) + bias


def _modulation(c, ada_w, ada_b):
    batch = c.shape[0]
    rows = -(-batch // SUBLANES) * SUBLANES
    n_col = 3 * D_MODEL // D_MODEL
    return pl.pallas_call(
        _mod_kernel,
        grid=(DEPTH, n_col),
        in_specs=[
            pl.BlockSpec((batch, D_MODEL), lambda l, j: (0, 0)),
            pl.BlockSpec((None, D_MODEL, D_MODEL), lambda l, j: (l, 0, j)),
            pl.BlockSpec((DEPTH, D_MODEL), lambda l, j: (0, j)),
        ],
        out_specs=pl.BlockSpec((None, rows, D_MODEL), lambda l, j: (l, 0, j)),
        out_shape=jax.ShapeDtypeStruct((DEPTH, rows, 3 * D_MODEL), F32),
        compiler_params=_params("parallel", "parallel"),
        name="adaln_mod",
    )(c, ada_w, ada_b)


MOD_SHIFT, MOD_SCALE, MOD_GATE = range(3)
EVEN_LAYER, ODD_LAYER = 0, 1


def _mod_spec(mod, layer, part):
    return pl.BlockSpec((None, mod.shape[1], D_MODEL), lambda b, i: (layer, 0, part))


def _batch_row(ref):
    return ref[pl.ds(pl.program_id(0), 1), :]


def _even_in_kernel(x_ref, shift_ref, scale_ref, pos_ref, freq_ref, sign_ref, w_f32_ref,
                    q_ref, k_ref, v_ref, su_ref, sv_ref, g_ref, w_ref):
    _cast_weight_once(w_f32_ref, w_ref)
    lane = lax.broadcasted_iota(jnp.int32, (1, LANES), 1)
    first_half = (lane % HEAD_DIM) < ROPE_HALF
    diagonal = lax.broadcasted_iota(jnp.int32, (LANES, LANES), 0) == lax.broadcasted_iota(jnp.int32, (LANES, LANES), 1)
    batch_ids = lax.broadcasted_iota(jnp.int32, (pos_ref.shape[0], LANES), 0)
    q_scale = HEAD_DIM ** -0.5 * LOG2_E

    half_rows = x_ref.shape[0] // 2
    for r in range(2):
        rows = slice(r * half_rows, (r + 1) * half_rows)
        h = (x_ref[rows, :] * (1.0 + _batch_row(scale_ref)) + _batch_row(shift_ref)).astype(BF16)
        p = jnp.dot(h, w_ref[...], preferred_element_type=F32)

        pos_cols = []
        for j in range(r * half_rows // LANES, (r + 1) * half_rows // LANES):
            pos_all = pos_ref[:, j * LANES:(j + 1) * LANES].astype(F32)
            pos_row = jnp.sum(jnp.where(batch_ids == pl.program_id(0), pos_all, 0.0), axis=0, keepdims=True)
            pos_cols.append(jnp.sum(jnp.where(diagonal, pos_row, 0.0), axis=1, keepdims=True))
        ang = jnp.concatenate(pos_cols, axis=0) * freq_ref[...]
        cs = jnp.cos(ang)
        sn = jnp.sin(ang) * sign_ref[...]

        def rotary(t):
            partner = jnp.where(first_half, pltpu.roll(t, LANES - ROPE_HALF, 1), pltpu.roll(t, ROPE_HALF, 1))
            return t * cs + partner * sn

        for j in range(ATTN_WIDTH // LANES):
            q_ref[rows, j * LANES:(j + 1) * LANES] = (
                rotary(p[:, j * LANES:(j + 1) * LANES]) * q_scale).astype(BF16)
        o = ATTN_WIDTH
        k_ref[rows, :] = rotary(p[:, o:o + KV_WIDTH]).astype(BF16)
        o += KV_WIDTH
        v_ref[rows, :] = p[:, o:o + KV_WIDTH].astype(BF16)
        o += KV_WIDTH
        su_ref[rows, :] = p[:, o:o + SG_WIDTH].astype(BF16)
        o += SG_WIDTH
        sv_ref[rows, :] = p[:, o:o + SG_WIDTH].astype(BF16)
        o += SG_WIDTH
        g_ref[rows, :] = p[:, o:o + EVEN_GATE_WIDTH].astype(BF16)


def _even_in_proj(x, mod, positions, w_in):
    batch, seq, _ = x.shape
    d = np.arange(LANES) % HEAD_DIM
    inv_freq = np.power(np.float64(ROPE_THETA), -np.arange(ROPE_HALF) / ROPE_HALF)
    freq_lane = jnp.asarray(np.where(d < ROPE_DIM, inv_freq[d % ROPE_HALF], 0.0), dtype=F32).reshape(1, LANES)
    sign_lane = jnp.asarray(np.where(d < ROPE_HALF, -1.0, np.where(d < ROPE_DIM, 1.0, 0.0)),
                            dtype=F32).reshape(1, LANES)
    row = lambda width: pl.BlockSpec((None, ROW_TILE, width), lambda b, i: (b, i, 0))
    const = lambda shape: pl.BlockSpec(shape, lambda b, i: (0,) * len(shape))
    widths = (ATTN_WIDTH, KV_WIDTH, KV_WIDTH, SG_WIDTH, SG_WIDTH, EVEN_GATE_WIDTH)
    return pl.pallas_call(
        _even_in_kernel,
        grid=(batch, seq // ROW_TILE),
        in_specs=[row(D_MODEL), _mod_spec(mod, EVEN_LAYER, MOD_SHIFT), _mod_spec(mod, EVEN_LAYER, MOD_SCALE),
                  pl.BlockSpec((batch, ROW_TILE), lambda b, i: (0, i)),
                  const((1, LANES)), const((1, LANES)), _resident((D_MODEL, EVEN_IN_WIDTH))],
        out_specs=[row(w) for w in widths],
        out_shape=[jax.ShapeDtypeStruct((batch, seq, w), BF16) for w in widths],
        scratch_shapes=[pltpu.VMEM((D_MODEL, EVEN_IN_WIDTH), BF16)],
        compiler_params=_params("arbitrary", "arbitrary"),
        name="even_in_proj",
    )(x, mod, mod, positions, freq_lane, sign_lane, w_in)


def _even_mix_kernel(sink_ref, q_ref, k_ref, v_ref, su_ref, sv_ref, g_ref, x_ref, gate_ref,
                     avg_ref, avg2_ref, sgg_ref, sgb_ref, sgw_ref, sgbias_ref, wout_f32_ref, lng_ref, lnb_ref,
                     o_ref, kdup, vsplit, y_scr, wout_ref):
    _cast_weight_once(wout_f32_ref, wout_ref)
    seq = k_ref.shape[0]
    n_blocks = seq // ATTN_BLOCK
    sub_blocks = MIX_TILE // ATTN_BLOCK
    tile = pl.program_id(1)
    lane = lax.broadcasted_iota(jnp.int32, (1, LANES), 1)
    left = lane < HEAD_DIM

    @pl.when(tile == 0)
    def _():
        def build(c, carry):
            rows = pl.ds(pl.multiple_of(c * MIX_TILE, MIX_TILE), MIX_TILE)
            k = k_ref[rows, :].astype(F32)
            k_sw = pltpu.roll(k, HEAD_DIM, 1)
            kdup[rows, 0:LANES] = jnp.where(left, k, k_sw).astype(BF16)
            kdup[rows, LANES:2 * LANES] = jnp.where(left, k_sw, k).astype(BF16)
            v = v_ref[rows, :].astype(F32)
            v_sw = pltpu.roll(v, HEAD_DIM, 1)
            vsplit[rows, 0 * LANES:1 * LANES] = jnp.where(left, v, 0.0).astype(BF16)
            vsplit[rows, 1 * LANES:2 * LANES] = jnp.where(left, 0.0, v_sw).astype(BF16)
            vsplit[rows, 2 * LANES:3 * LANES] = jnp.where(left, v_sw, 0.0).astype(BF16)
            vsplit[rows, 3 * LANES:4 * LANES] = jnp.where(left, 0.0, v).astype(BF16)
            return carry
        lax.fori_loop(0, seq // MIX_TILE, build, 0)

    qi = lax.broadcasted_iota(jnp.int32, (ATTN_BLOCK, ATTN_BLOCK), 0)
    kj = lax.broadcasted_iota(jnp.int32, (ATTN_BLOCK, ATTN_BLOCK), 1)
    ones_left = jnp.broadcast_to(jnp.where(left, 1.0, 0.0).astype(BF16), (3 * ATTN_BLOCK, LANES))
    ones_right = jnp.broadcast_to(jnp.where(left, 0.0, 1.0).astype(BF16), (3 * ATTN_BLOCK, LANES))
    avg = avg_ref[...]
    sink_cols = [jnp.concatenate([jnp.full((ATTN_BLOCK, LANES), sink_ref[h] * LOG2_E, F32)
                                  for h in (4 * hk, 4 * hk + 2, 4 * hk + 1, 4 * hk + 3)], axis=0)
                 for hk in range(N_KV_HEADS)]

    def sub_block(j, carry):
        blk = tile * sub_blocks + j
        prev = jnp.maximum(blk - 1, 0)
        nxt = jnp.minimum(blk + 1, n_blocks - 1)
        rows = pl.ds(pl.multiple_of(j * ATTN_BLOCK, ATTN_BLOCK), ATTN_BLOCK)
        bias_prev = jnp.where((kj >= qi) & (blk > 0), 0.0, NEG_INF)
        bias_next = jnp.where((kj <= qi) & (blk < n_blocks - 1), 0.0, NEG_INF)
        bias_prev = jnp.concatenate([bias_prev] * Q_PER_KV, axis=0)
        bias_next = jnp.concatenate([bias_next] * Q_PER_KV, axis=0)

        def band(ref):
            return jnp.concatenate(
                [ref[pl.ds(pl.multiple_of(b * ATTN_BLOCK, ATTN_BLOCK), ATTN_BLOCK), :] for b in (prev, blk, nxt)],
                axis=0)

        k_band = band(kdup)
        v_band = band(vsplit)
        n_sg = SG_WIDTH // LANES
        sg_cols = [slice(c * LANES, (c + 1) * LANES) for c in range(n_sg)]

        def silu_gate(cols):
            g_c = g_ref[rows, cols].astype(F32)
            return g_c * _sigmoid(g_c)

        scores = []
        for hk in range(N_KV_HEADS):
            q_cols = [q_ref[rows, (2 * hk + c) * LANES:(2 * hk + c + 1) * LANES] for c in range(2)]
            zero = jnp.zeros_like(q_cols[0])
            lhs = jnp.concatenate([jnp.where(left, qc, zero) for qc in q_cols]
                                  + [jnp.where(left, zero, qc) for qc in q_cols], axis=0)
            scores.append(lax.dot_general(lhs, k_band[:, hk * LANES:(hk + 1) * LANES], (((1,), (1,)), ((), ())),
                                          preferred_element_type=F32))
        sv_cols = [sv_ref[rows, cols] for cols in sg_cols]
        means = [jnp.dot(v_c, avg, preferred_element_type=F32) for v_c in sv_cols]

        probs, row_max = [], []
        for hk in range(N_KV_HEADS):
            s = scores[hk]
            s0 = s[:, 0:ATTN_BLOCK] + bias_prev
            s1 = s[:, ATTN_BLOCK:2 * ATTN_BLOCK]
            s2 = s[:, 2 * ATTN_BLOCK:3 * ATTN_BLOCK] + bias_next
            m = jnp.max(jnp.maximum(jnp.maximum(s0, s1), s2), axis=-1, keepdims=True)
            m = jnp.maximum(jnp.broadcast_to(m, sink_cols[hk].shape), sink_cols[hk])
            probs.append(jnp.concatenate([jnp.exp2(t - m).astype(BF16) for t in (s0, s1, s2)], axis=1))
            row_max.append(m)
        devs, sq_parts = [], []
        for v_c, mean in zip(sv_cols, means):
            dev = v_c.astype(F32) - mean
            sq = dev * dev
            sq_hi = sq.astype(BF16)
            sq_lo = (sq - sq_hi.astype(F32)).astype(BF16)
            devs.append(dev)
            sq_parts.append(jnp.concatenate([sq_hi, sq_lo], axis=1))

        pv = []
        for hk in range(N_KV_HEADS):
            p = probs[hk]
            lhs = jnp.concatenate(
                [jnp.concatenate([p[c * ATTN_BLOCK:(c + 1) * ATTN_BLOCK], p[(2 + c) * ATTN_BLOCK:(3 + c) * ATTN_BLOCK]],
                                 axis=1) for c in range(2)], axis=0)
            rhs = jnp.concatenate(
                [jnp.concatenate([v_band[:, (2 * hk) * LANES:(2 * hk + 1) * LANES], ones_left], axis=1),
                 jnp.concatenate([v_band[:, (2 * hk + 1) * LANES:(2 * hk + 2) * LANES], ones_right], axis=1)],
                axis=0)
            pv.append(jnp.dot(lhs, rhs, preferred_element_type=F32))
        variances = [jnp.dot(sq, avg2_ref[...], preferred_element_type=F32) for sq in sq_parts]

        for hk in range(N_KV_HEADS):
            for c in range(2):
                even, odd = slice(c * ATTN_BLOCK, (c + 1) * ATTN_BLOCK), slice((2 + c) * ATTN_BLOCK, (3 + c) * ATTN_BLOCK)
                sink_term = jnp.exp2(jnp.where(left, sink_cols[hk][even] - row_max[hk][even],
                                               sink_cols[hk][odd] - row_max[hk][odd]))
                o = pv[hk][c * ATTN_BLOCK:(c + 1) * ATTN_BLOCK]
                pair = o[:, 0:LANES] * (1.0 / (o[:, LANES:2 * LANES] + sink_term))
                cols = slice((2 * hk + c) * LANES, (2 * hk + c + 1) * LANES)
                y_scr[rows, cols] = (pair * silu_gate(cols)).astype(BF16)

        for c, cols in enumerate(sg_cols):
            vn = devs[c] * lax.rsqrt(variances[c] + LN_EPS) * sgg_ref[:, cols] + sgb_ref[:, cols]
            stacked = jnp.concatenate([jnp.where(left, vn, 0.0), jnp.where(left, 0.0, vn)],
                                      axis=0).astype(BF16)
            mixed = jnp.dot(sgw_ref[c], stacked, preferred_element_type=F32) + sgbias_ref[:, cols]
            y_sg = su_ref[rows, cols].astype(F32) * mixed
            gc = slice(ATTN_WIDTH + c * LANES, ATTN_WIDTH + (c + 1) * LANES)
            y_scr[rows, gc] = (y_sg * silu_gate(gc)).astype(BF16)
        return carry

    lax.fori_loop(0, sub_blocks, sub_block, 0, unroll=True)

    for r in range(2):
        half_rows = slice(r * (MIX_TILE // 2), (r + 1) * (MIX_TILE // 2))
        out = jnp.dot(y_scr[half_rows, :], wout_ref[...], preferred_element_type=F32)
        z = DEEPNORM_ALPHA * x_ref[half_rows, :] + _batch_row(gate_ref) * out
        o_ref[half_rows, :] = _layer_norm_rows(z, lng_ref[EVEN_LAYER:EVEN_LAYER + 1, :], lnb_ref[EVEN_LAYER:EVEN_LAYER + 1, :])


def _even_mixer(q, k, v, su, sv, g, x, mod, sink, sg_ln_g, sg_ln_b, sg_w, sg_b, w_out, ln_g, ln_b):
    batch, seq, _ = x.shape
    lane = np.arange(LANES)
    avg = jnp.asarray((lane[:, None] // SG_GROUP_DIM == lane[None, :] // SG_GROUP_DIM) / SG_GROUP_DIM, dtype=BF16)
    w_pairs = sg_w.reshape(N_SG_GROUPS // 2, 2, SG_CHUNK, SG_CHUNK).transpose(0, 2, 1, 3)
    w_pairs = w_pairs.reshape(N_SG_GROUPS // 2, SG_CHUNK, 2 * SG_CHUNK).astype(BF16)
    bias_full = jnp.repeat(sg_b.T, SG_GROUP_DIM, axis=1)
    row = lambda width: pl.BlockSpec((None, MIX_TILE, width), lambda b, i: (b, i, 0))
    full_seq = pl.BlockSpec((None, seq, KV_WIDTH), lambda b, i: (b, 0, 0))
    const = lambda shape: pl.BlockSpec(shape, lambda b, i: (0,) * len(shape))
    return pl.pallas_call(
        _even_mix_kernel,
        grid=(batch, seq // MIX_TILE),
        in_specs=[
            pl.BlockSpec(memory_space=pltpu.SMEM),
            row(ATTN_WIDTH), full_seq, full_seq, row(SG_WIDTH), row(SG_WIDTH), row(EVEN_GATE_WIDTH),
            row(D_MODEL), _mod_spec(mod, EVEN_LAYER, MOD_GATE),
            const((LANES, LANES)), const((2 * LANES, LANES)), const((1, SG_WIDTH)), const((1, SG_WIDTH)),
            const((N_SG_GROUPS // 2, SG_CHUNK, 2 * SG_CHUNK)), const((SG_CHUNK, SG_WIDTH)),
            _resident((EVEN_GATE_WIDTH, D_MODEL)), const(ln_g.shape), const(ln_b.shape),
        ],
        out_specs=row(D_MODEL),
        out_shape=jax.ShapeDtypeStruct((batch, seq, D_MODEL), F32),
        scratch_shapes=[pltpu.VMEM((seq, 2 * KV_WIDTH), BF16), pltpu.VMEM((seq, 4 * KV_WIDTH), BF16),
                        pltpu.VMEM((MIX_TILE, EVEN_GATE_WIDTH), BF16),
                        pltpu.VMEM((EVEN_GATE_WIDTH, D_MODEL), BF16)],
        compiler_params=_params("arbitrary", "arbitrary"),
        name="even_mixer",
    )(sink, q, k, v, su, sv, g, x, mod, avg, jnp.concatenate([avg, avg], axis=0), sg_ln_g.reshape(1, -1), sg_ln_b.reshape(1, -1),
      w_pairs, bias_full, w_out, ln_g, ln_b)


def _odd_in_kernel(x_ref, shift_ref, scale_ref, w_f32_ref, xr_ref, g_ref, w_ref):
    _cast_weight_once(w_f32_ref, w_ref)
    seg_len = xr_ref.shape[1]
    head_rows = seg_len * SEGS_PER_TILE
    xr_rows = xr_ref.reshape(RNN_HEADS * head_rows, LANES)
    halves = 2
    for r in range(halves):
        first = r * (SEGS_PER_TILE // halves)
        rows = slice(first * seg_len, (first + SEGS_PER_TILE // halves) * seg_len)
        h = (x_ref[rows, :] * (1.0 + _batch_row(scale_ref)) + _batch_row(shift_ref)).astype(BF16)
        p = jnp.dot(h, w_ref[...], preferred_element_type=F32)
        for s in range(SEGS_PER_TILE // halves):
            for hd in range(RNN_HEADS):
                xr_rows[pl.ds(hd * head_rows + first + s, seg_len, stride=SEGS_PER_TILE), :] = (
                    p[s * seg_len:(s + 1) * seg_len, hd * LANES:(hd + 1) * LANES])
        g_ref[rows, :] = p[:, RNN_WIDTH:2 * RNN_WIDTH].astype(BF16)


def _odd_in_proj(x, mod, w_in):
    batch, seq, _ = x.shape
    seg_len = seq // N_SEG
    tile_rows = SEGS_PER_TILE * seg_len
    out_spec = pl.BlockSpec((None, RNN_HEADS, seg_len, SEGS_PER_TILE, RNN_HEAD_DIM), lambda b, i: (b, 0, 0, i, 0))
    out_shape = jax.ShapeDtypeStruct((batch, RNN_HEADS, seg_len, N_SEG, RNN_HEAD_DIM), F32)
    return pl.pallas_call(
        _odd_in_kernel,
        grid=(batch, seq // tile_rows),
        in_specs=[pl.BlockSpec((None, tile_rows, D_MODEL), lambda b, i: (b, i, 0)),
                  _mod_spec(mod, ODD_LAYER, MOD_SHIFT), _mod_spec(mod, ODD_LAYER, MOD_SCALE),
                  _resident((D_MODEL, 2 * RNN_WIDTH))],
        out_specs=[out_spec, pl.BlockSpec((None, tile_rows, RNN_WIDTH), lambda b, i: (b, i, 0))],
        out_shape=[out_shape, jax.ShapeDtypeStruct((batch, seq, RNN_WIDTH), BF16)],
        scratch_shapes=[pltpu.VMEM((D_MODEL, 2 * RNN_WIDTH), BF16)],
        compiler_params=_params("arbitrary", "arbitrary"),
        name="odd_in_proj",
    )(x, mod, mod, w_in)


def _rglru_kernel(x_ref, convw_ref, convb_ref, waug_ref, lam_ref, o_ref,
                  halo, a_f, b_f, a_b, b_b, edge):
    n_rows = x_ref.shape[0]
    seg_len = n_rows // N_SEG
    n_chunks = n_rows // SCAN_CHUNK
    left_rows = CONV_LEFT * N_SEG
    right_rows = (CONV_WIDTH - 1 - CONV_LEFT) * N_SEG
    seg = lax.broadcasted_iota(jnp.int32, (N_SEG, LANES), 0)

    def from_prev_segment(blk):
        return jnp.where(seg == 0, 0.0, pltpu.roll(blk, 1, 0))

    def from_next_segment(blk):
        return jnp.where(seg == N_SEG - 1, 0.0, pltpu.roll(blk, N_SEG - 1, 0))

    for t in range(CONV_LEFT):
        halo[0, pl.ds(t * N_SEG, N_SEG), :] = from_prev_segment(
            x_ref[pl.ds((seg_len - CONV_LEFT + t) * N_SEG, N_SEG), :])
    halo[0, pl.ds(left_rows, SCAN_CHUNK + right_rows), :] = x_ref[pl.ds(0, SCAN_CHUNK + right_rows), :]
    halo[1, pl.ds(0, left_rows + SCAN_CHUNK), :] = x_ref[pl.ds(n_rows - SCAN_CHUNK - left_rows, left_rows + SCAN_CHUNK), :]
    for t in range(CONV_WIDTH - 1 - CONV_LEFT):
        halo[1, pl.ds(left_rows + SCAN_CHUNK + t * N_SEG, N_SEG), :] = from_next_segment(
            x_ref[pl.ds(t * N_SEG, N_SEG), :])

    lam = lam_ref[...]
    softplus_neg = jnp.maximum(-lam, 0.0) + jnp.log1p(jnp.exp(-jnp.abs(lam)))
    half_rate = (0.5 * RG_LRU_C) * softplus_neg
    bias_lanes = jnp.broadcast_to(
        jnp.where(lax.broadcasted_iota(jnp.int32, (1, LANES), 1) < BIAS_ROWS, 1.0, 0.0).astype(BF16),
        (SCAN_CHUNK, LANES))

    def gates(base, tap):
        half_x = convb_ref[...] + sum(convw_ref[k:k + 1, :] * tap(k) for k in range(CONV_WIDTH))
        th = jnp.tanh(jnp.dot(jnp.concatenate([half_x.astype(BF16), bias_lanes], axis=1), waug_ref[...],
                              preferred_element_type=F32))
        for d, (a_ref, b_ref) in enumerate(((a_f, b_f), (a_b, b_b))):
            o = 2 * d * LANES
            neg_log_a = (th[:, o:o + LANES] + 1.0) * half_rate[d:d + 1, :]
            a = jnp.exp2(neg_log_a * (-LOG2_E))
            one_minus_a2 = jnp.tanh(neg_log_a) * (a * a + 1.0)
            root = one_minus_a2 * lax.rsqrt(jnp.maximum(one_minus_a2, F32_TINY))
            a_ref[pl.ds(base, SCAN_CHUNK), :] = a
            b_ref[pl.ds(base, SCAN_CHUNK), :] = root * ((th[:, o + LANES:o + 2 * LANES] + 1.0) * half_x)

    def inner_chunk(c, carry):
        base = pl.multiple_of(c * SCAN_CHUNK, SCAN_CHUNK)
        gates(base, lambda k: x_ref[pl.ds(base + (k - CONV_LEFT) * N_SEG, SCAN_CHUNK), :])
        return carry

    gates(0, lambda k: halo[0, pl.ds(k * N_SEG, SCAN_CHUNK), :])
    gates(n_rows - SCAN_CHUNK, lambda k: halo[1, pl.ds(k * N_SEG, SCAN_CHUNK), :])
    lax.fori_loop(1, n_chunks - 1, inner_chunk, 0, unroll=7)

    def scan_step(t, carry):
        h_f, p_f, h_b, p_b = carry
        rf = pl.ds(pl.multiple_of(t * N_SEG, N_SEG), N_SEG)
        rb = pl.ds(pl.multiple_of((seg_len - 1 - t) * N_SEG, N_SEG), N_SEG)
        a = a_f[rf, :]
        h_f = a * h_f + b_f[rf, :]
        p_f = a * p_f
        b_f[rf, :] = h_f
        a_f[rf, :] = p_f
        a = a_b[rb, :]
        h_b = a * h_b + b_b[rb, :]
        p_b = a * p_b
        b_b[rb, :] = h_b
        a_b[rb, :] = p_b
        return h_f, p_f, h_b, p_b

    zeros = jnp.zeros((N_SEG, LANES), F32)
    ones = jnp.ones((N_SEG, LANES), F32)
    h_f, p_f, h_b, p_b = lax.fori_loop(0, seg_len, scan_step, (zeros, ones, zeros, ones), unroll=4)

    edge[0] = h_f
    edge[1] = p_f
    edge[2] = h_b
    edge[3] = p_b
    edge[4, 0:1, :] = jnp.zeros((1, LANES), F32)
    edge[5, N_SEG - 1:N_SEG, :] = jnp.zeros((1, LANES), F32)
    for s in range(1, N_SEG):
        edge[4, s:s + 1, :] = edge[0, s - 1:s, :] + edge[1, s - 1:s, :] * edge[4, s - 1:s, :]
        r = N_SEG - 1 - s
        edge[5, r:r + 1, :] = edge[2, r + 1:r + 2, :] + edge[3, r + 1:r + 2, :] * edge[5, r + 1:r + 2, :]
    blocks = SCAN_CHUNK // N_SEG
    in_f = jnp.concatenate([edge[4]] * blocks, axis=0)
    in_b = jnp.concatenate([edge[5]] * blocks, axis=0)

    def out_chunk(c, carry):
        rows = pl.ds(pl.multiple_of(c * SCAN_CHUNK, SCAN_CHUNK), SCAN_CHUNK)
        o_ref[rows, :] = (b_f[rows, :] + a_f[rows, :] * in_f) + (b_b[rows, :] + a_b[rows, :] * in_b)
        return carry

    lax.fori_loop(0, n_rows // SCAN_CHUNK, out_chunk, 0, unroll=2)


def _rglru(xr, conv_w, conv_b, w_a, b_a, w_x, b_x, lam):
    batch, _, n_rows, _ = xr.shape
    w_cat = jnp.concatenate([w_a[0], w_x[0], w_a[1], w_x[1]], axis=-1).astype(BF16)
    per_head = lambda v: v.reshape(RNN_HEADS, 1, RNN_HEAD_DIM)
    half_b = 0.5 * jnp.concatenate([per_head(b_a[0]), per_head(b_x[0]), per_head(b_a[1]), per_head(b_x[1])],
                                   axis=-1)
    b_hi = half_b.astype(BF16)
    b_lo = (half_b - b_hi.astype(F32)).astype(BF16)
    w_aug = jnp.concatenate([w_cat, b_hi, b_lo,
                             jnp.zeros((RNN_HEADS, RNN_HEAD_DIM - BIAS_ROWS, 4 * RNN_HEAD_DIM), BF16)], axis=1)
    col = lambda rows: pl.BlockSpec((rows, RNN_HEAD_DIM), lambda b, h: (0, h))
    seq_col = pl.BlockSpec((None, None, n_rows, RNN_HEAD_DIM), lambda b, h: (b, h, 0, 0))
    halo_rows = SCAN_CHUNK + (CONV_WIDTH - 1) * N_SEG
    return pl.pallas_call(
        _rglru_kernel,
        grid=(batch, RNN_HEADS),
        in_specs=[seq_col, col(CONV_WIDTH), col(1),
                  pl.BlockSpec((None, 2 * RNN_HEAD_DIM, 4 * RNN_HEAD_DIM), lambda b, h: (h, 0, 0)),
                  col(2)],
        out_specs=seq_col,
        out_shape=jax.ShapeDtypeStruct((batch, RNN_HEADS, n_rows, RNN_HEAD_DIM), F32),
        scratch_shapes=[pltpu.VMEM((2, halo_rows, LANES), F32)]
                       + [pltpu.VMEM((n_rows, LANES), F32)] * 4
                       + [pltpu.VMEM((6, N_SEG, LANES), F32)],
        compiler_params=_params("parallel", "parallel"),
        name="rglru",
    )(xr, 0.5 * conv_w, 0.5 * conv_b.reshape(1, -1), w_aug, lam)


def _odd_out_kernel(y_ref, g_ref, x_ref, gate_ref, w_f32_ref, lng_ref, lnb_ref, o_ref, w_ref):
    _cast_weight_once(w_f32_ref, w_ref)
    seg_len = y_ref.shape[1]
    head_rows = seg_len * SEGS_PER_TILE
    y_rows = y_ref.reshape(RNN_HEADS * head_rows, LANES)
    gated = []
    for s in range(SEGS_PER_TILE):
        y = jnp.concatenate([y_rows[pl.ds(hd * head_rows + s, seg_len, stride=SEGS_PER_TILE), :]
                             for hd in range(RNN_HEADS)], axis=1)
        g = g_ref[s * seg_len:(s + 1) * seg_len, :].astype(F32)
        gated.append((y * (g * _sigmoid(g))).astype(BF16))
    out = jnp.dot(jnp.concatenate(gated, axis=0), w_ref[...], preferred_element_type=F32)
    z = DEEPNORM_ALPHA * x_ref[...] + _batch_row(gate_ref) * out
    o_ref[...] = _layer_norm_rows(z, lng_ref[ODD_LAYER:ODD_LAYER + 1, :], lnb_ref[ODD_LAYER:ODD_LAYER + 1, :])


def _odd_out_proj(y, g, x, mod, w_out, ln_g, ln_b):
    batch, seq, _ = x.shape
    seg_len = seq // N_SEG
    tile_rows = SEGS_PER_TILE * seg_len
    const = lambda shape: pl.BlockSpec(shape, lambda b, i: (0,) * len(shape))
    return pl.pallas_call(
        _odd_out_kernel,
        grid=(batch, seq // tile_rows),
        in_specs=[pl.BlockSpec((None, RNN_HEADS, seg_len, SEGS_PER_TILE, RNN_HEAD_DIM),
                               lambda b, i: (b, 0, 0, i, 0)),
                  pl.BlockSpec((None, tile_rows, RNN_WIDTH), lambda b, i: (b, i, 0)),
                  pl.BlockSpec((None, tile_rows, D_MODEL), lambda b, i: (b, i, 0)),
                  _mod_spec(mod, ODD_LAYER, MOD_GATE),
                  _resident((RNN_WIDTH, D_MODEL)), const(ln_g.shape), const(ln_b.shape)],
        out_specs=pl.BlockSpec((None, tile_rows, D_MODEL), lambda b, i: (b, i, 0)),
        out_shape=jax.ShapeDtypeStruct((batch, seq, D_MODEL), F32),
        scratch_shapes=[pltpu.VMEM((RNN_WIDTH, D_MODEL), BF16)],
        compiler_params=_params("arbitrary", "arbitrary"),
        name="odd_out_proj",
    )(y, g, x, mod, w_out, ln_g, ln_b)


def kernel(x, c, positions, ada_w, ada_b, ln_g, ln_b, ev_w_in, ev_w_out, ev_sink, ev_sg_ln_g, ev_sg_ln_b,
           ev_sg_w, ev_sg_b, od_w_in, od_conv_w, od_conv_b, od_w_a, od_b_a, od_w_x, od_b_x, od_lam, od_w_out):
    batch, seq, d_model = x.shape
    seg_len = seq // N_SEG
    assert d_model == D_MODEL and seq % ROW_TILE == 0 and seq % MIX_TILE == 0 and seq == N_SEG * seg_len
    assert N_SEG % SEGS_PER_TILE == 0 and seq % SCAN_CHUNK == 0 and SCAN_CHUNK % N_SEG == 0
    assert ada_w.shape[0] == DEPTH == 2

    mod = _modulation(c, ada_w, ada_b)

    q, k, v, su, sv, g = _even_in_proj(x, mod, positions, ev_w_in[0])
    x1 = _even_mixer(q, k, v, su, sv, g, x, mod, ev_sink[0], ev_sg_ln_g[0], ev_sg_ln_b[0],
                     ev_sg_w[0], ev_sg_b[0], ev_w_out[0], ln_g, ln_b)

    xr, g1 = _odd_in_proj(x1, mod, od_w_in[0])
    y = _rglru(xr.reshape(batch, RNN_HEADS, seq, RNN_HEAD_DIM), od_conv_w[0], od_conv_b[0], od_w_a[0],
               od_b_a[0], od_w_x[0], od_b_x[0], od_lam[0])
    return _odd_out_proj(y.reshape(batch, RNN_HEADS, seg_len, N_SEG, RNN_HEAD_DIM), g1, x1, mod,
                         od_w_out[0], ln_g, ln_b)
```

```python
import math

import numpy as np
import jax
import jax.numpy as jnp
from jax import lax
from jax.experimental import pallas as pl
from jax.experimental.pallas import tpu as pltpu

D_MODEL = 1024
DEPTH = 2
HEAD_DIM = 64
N_Q_HEADS = 8
N_KV_HEADS = 2
Q_PER_KV = N_Q_HEADS // N_KV_HEADS
ATTN_WIDTH = N_Q_HEADS * HEAD_DIM
KV_WIDTH = N_KV_HEADS * HEAD_DIM
ATTN_BLOCK = 128
ROPE_DIM = HEAD_DIM // 4
ROPE_HALF = ROPE_DIM // 2
ROPE_THETA = 500000.0
N_SG_GROUPS = 8
SG_GROUP_DIM = 64
SG_WIDTH = N_SG_GROUPS * SG_GROUP_DIM
SG_CHUNK = 128
EVEN_GATE_WIDTH = ATTN_WIDTH + SG_WIDTH
EVEN_IN_WIDTH = ATTN_WIDTH + 2 * KV_WIDTH + 2 * SG_WIDTH + EVEN_GATE_WIDTH
RNN_WIDTH = D_MODEL
RNN_HEADS = 8
RNN_HEAD_DIM = RNN_WIDTH // RNN_HEADS
CONV_WIDTH = 4
CONV_LEFT = 2
RG_LRU_C = 8.0
DEEPNORM_ALPHA = (2 * DEPTH) ** 0.25
LN_EPS = 1e-5
NEG_INF = -1e30
LOG2_E = math.log2(math.e)
F32_TINY = float(np.finfo(np.float32).tiny)

LANES = 128
SUBLANES = 8
VMEM_LIMIT = 56 * 1024 * 1024

ROW_TILE = 512
MIX_TILE = 512
N_SEG = 32
SEGS_PER_TILE = SUBLANES
SCAN_CHUNK = 256
WEIGHT_CAST_ROWS = 128
BIAS_ROWS = 2

BF16 = jnp.bfloat16
F32 = jnp.float32


def _sigmoid(x):
    return 0.5 * jnp.tanh(0.5 * x) + 0.5


def _layer_norm_rows(z, g, b, eps=LN_EPS):
    mu = jnp.mean(z, axis=-1, keepdims=True)
    d = z - mu
    var = jnp.mean(d * d, axis=-1, keepdims=True)
    return d * lax.rsqrt(var + eps) * g + b


def _deepnorm_rows(x, gate, y, g, b):
    return _layer_norm_rows(x + (gate * (1.0 / DEEPNORM_ALPHA)) * y, g, b, LN_EPS / DEEPNORM_ALPHA ** 2)


def _params(*semantics):
    return pltpu.CompilerParams(dimension_semantics=semantics, vmem_limit_bytes=VMEM_LIMIT)


def _cast_weight_once(w_ref, w_bf16):
    @pl.when((pl.program_id(0) == 0) & (pl.program_id(1) == 0))
    def _():
        for r in range(0, w_ref.shape[0], WEIGHT_CAST_ROWS):
            w_bf16[r:r + WEIGHT_CAST_ROWS, :] = w_ref[r:r + WEIGHT_CAST_ROWS, :].astype(BF16)


def _resident(shape):
    return pl.BlockSpec(shape, lambda b, i: (0,) * len(shape), pipeline_mode=pl.Buffered(1))


def _mod_kernel(c_ref, w_ref, b_ref, o_ref):
    c = c_ref[...]
    cond = c * _sigmoid(c)
    pad_rows = o_ref.shape[0] - cond.shape[0]
    if pad_rows:
        cond = jnp.concatenate([cond, jnp.zeros((pad_rows, cond.shape[1]), F32)], axis=0)
    bias = b_ref[pl.ds(pl.program_id(0), 1), :]
    o_ref[...] = jnp.dot(cond.astype(BF16), w_ref[...].astype(BF16), preferred_element_type=F32) + bias


def _modulation(c, ada_w, ada_b):
    batch = c.shape[0]
    rows = -(-batch // SUBLANES) * SUBLANES
    n_col = 3 * D_MODEL // D_MODEL
    return pl.pallas_call(
        _mod_kernel,
        grid=(DEPTH, n_col),
        in_specs=[
            pl.BlockSpec((batch, D_MODEL), lambda l, j: (0, 0)),
            pl.BlockSpec((None, D_MODEL, D_MODEL), lambda l, j: (l, 0, j)),
            pl.BlockSpec((DEPTH, D_MODEL), lambda l, j: (0, j)),
        ],
        out_specs=pl.BlockSpec((None, rows, D_MODEL), lambda l, j: (l, 0, j)),
        out_shape=jax.ShapeDtypeStruct((DEPTH, rows, 3 * D_MODEL), F32),
        compiler_params=_params("parallel", "parallel"),
        name="adaln_mod",
    )(c, ada_w, ada_b)


MOD_SHIFT, MOD_SCALE, MOD_GATE = range(3)
EVEN_LAYER, ODD_LAYER = 0, 1


def _mod_spec(mod, layer, part):
    return pl.BlockSpec((None, mod.shape[1], D_MODEL), lambda b, i: (layer, 0, part))


def _batch_row(ref):
    return ref[pl.ds(pl.program_id(0), 1), :]


def _even_in_kernel(x_ref, shift_ref, scale_ref, pos_ref, freq_ref, sign_ref, w_f32_ref,
                    q_ref, k_ref, v_ref, su_ref, sv_ref, g_ref, w_ref):
    _cast_weight_once(w_f32_ref, w_ref)
    lane = lax.broadcasted_iota(jnp.int32, (1, LANES), 1)
    first_half = (lane % HEAD_DIM) < ROPE_HALF
    left_head = lane < HEAD_DIM
    diagonal = lax.broadcasted_iota(jnp.int32, (LANES, LANES), 0) == lax.broadcasted_iota(jnp.int32, (LANES, LANES), 1)
    batch_ids = lax.broadcasted_iota(jnp.int32, (pos_ref.shape[0], LANES), 0)
    q_scale = HEAD_DIM ** -0.5 * LOG2_E

    half_rows = x_ref.shape[0] // 2
    for r in range(2):
        rows = slice(r * half_rows, (r + 1) * half_rows)
        h = (x_ref[rows, :] * (1.0 + _batch_row(scale_ref)) + _batch_row(shift_ref)).astype(BF16)
        p = jnp.dot(h, w_ref[...], preferred_element_type=F32)

        pos_cols = []
        for j in range(r * half_rows // LANES, (r + 1) * half_rows // LANES):
            pos_all = pos_ref[:, j * LANES:(j + 1) * LANES].astype(F32)
            pos_row = jnp.sum(jnp.where(batch_ids == pl.program_id(0), pos_all, 0.0), axis=0, keepdims=True)
            pos_cols.append(jnp.sum(jnp.where(diagonal, pos_row, 0.0), axis=1, keepdims=True))
        ang = jnp.concatenate(pos_cols, axis=0) * freq_ref[...]
        cs = jnp.cos(ang)
        sn = jnp.sin(ang) * sign_ref[...]

        def rotary(t):
            partner = jnp.where(first_half, pltpu.roll(t, LANES - ROPE_HALF, 1), pltpu.roll(t, ROPE_HALF, 1))
            return t * cs + partner * sn

        for j in range(ATTN_WIDTH // LANES):
            q_ref[rows, j * LANES:(j + 1) * LANES] = (
                rotary(p[:, j * LANES:(j + 1) * LANES]) * q_scale).astype(BF16)
        o = ATTN_WIDTH
        k = rotary(p[:, o:o + KV_WIDTH])
        k_sw = pltpu.roll(k, HEAD_DIM, 1)
        k_ref[rows, 0:LANES] = jnp.where(left_head, k, k_sw).astype(BF16)
        k_ref[rows, LANES:2 * LANES] = jnp.where(left_head, k_sw, k).astype(BF16)
        o += KV_WIDTH
        v = p[:, o:o + KV_WIDTH]
        v_sw = pltpu.roll(v, HEAD_DIM, 1)
        v_ref[rows, 0 * LANES:1 * LANES] = jnp.where(left_head, v, 0.0).astype(BF16)
        v_ref[rows, 1 * LANES:2 * LANES] = jnp.where(left_head, 0.0, v_sw).astype(BF16)
        v_ref[rows, 2 * LANES:3 * LANES] = jnp.where(left_head, v_sw, 0.0).astype(BF16)
        v_ref[rows, 3 * LANES:4 * LANES] = jnp.where(left_head, 0.0, v).astype(BF16)
        o += KV_WIDTH
        su_ref[rows, :] = p[:, o:o + SG_WIDTH].astype(BF16)
        o += SG_WIDTH
        sv_ref[rows, :] = p[:, o:o + SG_WIDTH].astype(BF16)
        o += SG_WIDTH
        g_ref[rows, :] = p[:, o:o + EVEN_GATE_WIDTH].astype(BF16)


def _even_in_proj(x, mod, positions, w_in):
    batch, seq, _ = x.shape
    d = np.arange(LANES) % HEAD_DIM
    inv_freq = np.power(np.float64(ROPE_THETA), -np.arange(ROPE_HALF) / ROPE_HALF)
    freq_lane = jnp.asarray(np.where(d < ROPE_DIM, inv_freq[d % ROPE_HALF], 0.0), dtype=F32).reshape(1, LANES)
    sign_lane = jnp.asarray(np.where(d < ROPE_HALF, -1.0, np.where(d < ROPE_DIM, 1.0, 0.0)),
                            dtype=F32).reshape(1, LANES)
    row = lambda width: pl.BlockSpec((None, ROW_TILE, width), lambda b, i: (b, i, 0))
    const = lambda shape: pl.BlockSpec(shape, lambda b, i: (0,) * len(shape))
    widths = (ATTN_WIDTH, 2 * KV_WIDTH, 4 * KV_WIDTH, SG_WIDTH, SG_WIDTH, EVEN_GATE_WIDTH)
    return pl.pallas_call(
        _even_in_kernel,
        grid=(batch, seq // ROW_TILE),
        in_specs=[row(D_MODEL), _mod_spec(mod, EVEN_LAYER, MOD_SHIFT), _mod_spec(mod, EVEN_LAYER, MOD_SCALE),
                  pl.BlockSpec((batch, ROW_TILE), lambda b, i: (0, i)),
                  const((1, LANES)), const((1, LANES)), _resident((D_MODEL, EVEN_IN_WIDTH))],
        out_specs=[row(w) for w in widths],
        out_shape=[jax.ShapeDtypeStruct((batch, seq, w), BF16) for w in widths],
        scratch_shapes=[pltpu.VMEM((D_MODEL, EVEN_IN_WIDTH), BF16)],
        compiler_params=_params("arbitrary", "arbitrary"),
        name="even_in_proj",
    )(x, mod, mod, positions, freq_lane, sign_lane, w_in)


def _even_mix_kernel(sink_ref, q_ref, k_ref, v_ref, su_ref, sv_ref, g_ref, x_ref, gate_ref,
                     avg_ref, avg2_ref, sgg_ref, sgb_ref, sgw_ref, sgbias_ref, wout_f32_ref, lng_ref, lnb_ref,
                     o_ref, y_scr, wout_ref):
    _cast_weight_once(wout_f32_ref, wout_ref)
    seq = k_ref.shape[0]
    n_blocks = seq // ATTN_BLOCK
    sub_blocks = MIX_TILE // ATTN_BLOCK
    tile = pl.program_id(1)
    lane = lax.broadcasted_iota(jnp.int32, (1, LANES), 1)
    left = lane < HEAD_DIM

    qi = lax.broadcasted_iota(jnp.int32, (ATTN_BLOCK, ATTN_BLOCK), 0)
    kj = lax.broadcasted_iota(jnp.int32, (ATTN_BLOCK, ATTN_BLOCK), 1)
    ones_left = jnp.broadcast_to(jnp.where(left, 1.0, 0.0).astype(BF16), (3 * ATTN_BLOCK, LANES))
    ones_right = jnp.broadcast_to(jnp.where(left, 0.0, 1.0).astype(BF16), (3 * ATTN_BLOCK, LANES))
    avg = avg_ref[...]
    sink_cols = [jnp.concatenate([jnp.full((ATTN_BLOCK, LANES), sink_ref[h] * LOG2_E, F32)
                                  for h in (4 * hk, 4 * hk + 2, 4 * hk + 1, 4 * hk + 3)], axis=0)
                 for hk in range(N_KV_HEADS)]

    def sub_block(j, carry):
        blk = tile * sub_blocks + j
        prev = jnp.maximum(blk - 1, 0)
        nxt = jnp.minimum(blk + 1, n_blocks - 1)
        rows = pl.ds(pl.multiple_of(j * ATTN_BLOCK, ATTN_BLOCK), ATTN_BLOCK)
        bias_prev = jnp.where((kj >= qi) & (blk > 0), 0.0, NEG_INF)
        bias_next = jnp.where((kj <= qi) & (blk < n_blocks - 1), 0.0, NEG_INF)
        bias_prev = jnp.concatenate([bias_prev] * Q_PER_KV, axis=0)
        bias_next = jnp.concatenate([bias_next] * Q_PER_KV, axis=0)

        def band(ref):
            return jnp.concatenate(
                [ref[pl.ds(pl.multiple_of(b * ATTN_BLOCK, ATTN_BLOCK), ATTN_BLOCK), :] for b in (prev, blk, nxt)],
                axis=0)

        k_band = band(k_ref)
        v_band = band(v_ref)
        n_sg = SG_WIDTH // LANES
        sg_cols = [slice(c * LANES, (c + 1) * LANES) for c in range(n_sg)]

        def silu_gate(cols):
            g_c = g_ref[rows, cols].astype(F32)
            return g_c * _sigmoid(g_c)

        scores = []
        for hk in range(N_KV_HEADS):
            q_cols = [q_ref[rows, (2 * hk + c) * LANES:(2 * hk + c + 1) * LANES] for c in range(2)]
            zero = jnp.zeros_like(q_cols[0])
            lhs = jnp.concatenate([jnp.where(left, qc, zero) for qc in q_cols]
                                  + [jnp.where(left, zero, qc) for qc in q_cols], axis=0)
            scores.append(lax.dot_general(lhs, k_band[:, hk * LANES:(hk + 1) * LANES], (((1,), (1,)), ((), ())),
                                          preferred_element_type=F32))
        sv_cols = [sv_ref[rows, cols] for cols in sg_cols]
        means = [jnp.dot(v_c, avg, preferred_element_type=F32) for v_c in sv_cols]

        probs, row_max = [], []
        for hk in range(N_KV_HEADS):
            s = scores[hk]
            s0 = s[:, 0:ATTN_BLOCK] + bias_prev
            s1 = s[:, ATTN_BLOCK:2 * ATTN_BLOCK]
            s2 = s[:, 2 * ATTN_BLOCK:3 * ATTN_BLOCK] + bias_next
            m = jnp.max(jnp.maximum(jnp.maximum(s0, s1), s2), axis=-1, keepdims=True)
            m = jnp.maximum(jnp.broadcast_to(m, sink_cols[hk].shape), sink_cols[hk])
            probs.append(jnp.concatenate([jnp.exp2(t - m).astype(BF16) for t in (s0, s1, s2)], axis=1))
            row_max.append(m)
        devs, sq_parts = [], []
        for v_c, mean in zip(sv_cols, means):
            dev = v_c.astype(F32) - mean
            sq = dev * dev
            sq_hi = sq.astype(BF16)
            sq_lo = (sq - sq_hi.astype(F32)).astype(BF16)
            devs.append(dev)
            sq_parts.append(jnp.concatenate([sq_hi, sq_lo], axis=1))

        pv = []
        for hk in range(N_KV_HEADS):
            p = probs[hk]
            lhs = jnp.concatenate(
                [jnp.concatenate([p[c * ATTN_BLOCK:(c + 1) * ATTN_BLOCK], p[(2 + c) * ATTN_BLOCK:(3 + c) * ATTN_BLOCK]],
                                 axis=1) for c in range(2)], axis=0)
            rhs = jnp.concatenate(
                [jnp.concatenate([v_band[:, (2 * hk) * LANES:(2 * hk + 1) * LANES], ones_left], axis=1),
                 jnp.concatenate([v_band[:, (2 * hk + 1) * LANES:(2 * hk + 2) * LANES], ones_right], axis=1)],
                axis=0)
            pv.append(jnp.dot(lhs, rhs, preferred_element_type=F32))
        variances = [jnp.dot(sq, avg2_ref[...], preferred_element_type=F32) for sq in sq_parts]

        for hk in range(N_KV_HEADS):
            for c in range(2):
                even, odd = slice(c * ATTN_BLOCK, (c + 1) * ATTN_BLOCK), slice((2 + c) * ATTN_BLOCK, (3 + c) * ATTN_BLOCK)
                sink_term = jnp.exp2(jnp.where(left, sink_cols[hk][even] - row_max[hk][even],
                                               sink_cols[hk][odd] - row_max[hk][odd]))
                o = pv[hk][c * ATTN_BLOCK:(c + 1) * ATTN_BLOCK]
                pair = o[:, 0:LANES] * (1.0 / (o[:, LANES:2 * LANES] + sink_term))
                cols = slice((2 * hk + c) * LANES, (2 * hk + c + 1) * LANES)
                y_scr[rows, cols] = (pair * silu_gate(cols)).astype(BF16)

        for c, cols in enumerate(sg_cols):
            vn = devs[c] * lax.rsqrt(variances[c] + LN_EPS) * sgg_ref[:, cols] + sgb_ref[:, cols]
            stacked = jnp.concatenate([jnp.where(left, vn, 0.0), jnp.where(left, 0.0, vn)],
                                      axis=0).astype(BF16)
            mixed = jnp.dot(sgw_ref[c], stacked, preferred_element_type=F32) + sgbias_ref[:, cols]
            y_sg = su_ref[rows, cols].astype(F32) * mixed
            gc = slice(ATTN_WIDTH + c * LANES, ATTN_WIDTH + (c + 1) * LANES)
            y_scr[rows, gc] = (y_sg * silu_gate(gc)).astype(BF16)
        return carry

    lax.fori_loop(0, sub_blocks, sub_block, 0, unroll=True)

    for r in range(2):
        half_rows = slice(r * (MIX_TILE // 2), (r + 1) * (MIX_TILE // 2))
        out = jnp.dot(y_scr[half_rows, :], wout_ref[...], preferred_element_type=F32)
        o_ref[half_rows, :] = _deepnorm_rows(x_ref[half_rows, :], _batch_row(gate_ref), out,
                                             lng_ref[EVEN_LAYER:EVEN_LAYER + 1, :], lnb_ref[EVEN_LAYER:EVEN_LAYER + 1, :])


def _even_mixer(q, k, v, su, sv, g, x, mod, sink, sg_ln_g, sg_ln_b, sg_w, sg_b, w_out, ln_g, ln_b):
    batch, seq, _ = x.shape
    lane = np.arange(LANES)
    avg = jnp.asarray((lane[:, None] // SG_GROUP_DIM == lane[None, :] // SG_GROUP_DIM) / SG_GROUP_DIM, dtype=BF16)
    w_pairs = sg_w.reshape(N_SG_GROUPS // 2, 2, SG_CHUNK, SG_CHUNK).transpose(0, 2, 1, 3)
    w_pairs = w_pairs.reshape(N_SG_GROUPS // 2, SG_CHUNK, 2 * SG_CHUNK).astype(BF16)
    bias_full = jnp.repeat(sg_b.T, SG_GROUP_DIM, axis=1)
    row = lambda width: pl.BlockSpec((None, MIX_TILE, width), lambda b, i: (b, i, 0))
    full_seq = lambda arr: pl.BlockSpec((None, seq, arr.shape[-1]), lambda b, i: (b, 0, 0))
    const = lambda shape: pl.BlockSpec(shape, lambda b, i: (0,) * len(shape))
    return pl.pallas_call(
        _even_mix_kernel,
        grid=(batch, seq // MIX_TILE),
        in_specs=[
            pl.BlockSpec(memory_space=pltpu.SMEM),
            row(ATTN_WIDTH), full_seq(k), full_seq(v), row(SG_WIDTH), row(SG_WIDTH), row(EVEN_GATE_WIDTH),
            row(D_MODEL), _mod_spec(mod, EVEN_LAYER, MOD_GATE),
            const((LANES, LANES)), const((2 * LANES, LANES)), const((1, SG_WIDTH)), const((1, SG_WIDTH)),
            const((N_SG_GROUPS // 2, SG_CHUNK, 2 * SG_CHUNK)), const((SG_CHUNK, SG_WIDTH)),
            _resident((EVEN_GATE_WIDTH, D_MODEL)), const(ln_g.shape), const(ln_b.shape),
        ],
        out_specs=row(D_MODEL),
        out_shape=jax.ShapeDtypeStruct((batch, seq, D_MODEL), F32),
        scratch_shapes=[pltpu.VMEM((MIX_TILE, EVEN_GATE_WIDTH), BF16),
                        pltpu.VMEM((EVEN_GATE_WIDTH, D_MODEL), BF16)],
        compiler_params=_params("arbitrary", "arbitrary"),
        name="even_mixer",
    )(sink, q, k, v, su, sv, g, x, mod, avg, jnp.concatenate([avg, avg], axis=0), sg_ln_g.reshape(1, -1), sg_ln_b.reshape(1, -1),
      w_pairs, bias_full, w_out, ln_g, ln_b)


def _odd_in_kernel(x_ref, shift_ref, scale_ref, w_f32_ref, xr_ref, g_ref, w_ref):
    _cast_weight_once(w_f32_ref, w_ref)
    seg_len = xr_ref.shape[1]
    head_rows = seg_len * SEGS_PER_TILE
    xr_rows = xr_ref.reshape(RNN_HEADS * head_rows, LANES)
    halves = 2
    for r in range(halves):
        first = r * (SEGS_PER_TILE // halves)
        rows = slice(first * seg_len, (first + SEGS_PER_TILE // halves) * seg_len)
        h = (x_ref[rows, :] * (1.0 + _batch_row(scale_ref)) + _batch_row(shift_ref)).astype(BF16)
        p = jnp.dot(h, w_ref[...], preferred_element_type=F32)
        for s in range(SEGS_PER_TILE // halves):
            for hd in range(RNN_HEADS):
                xr_rows[pl.ds(hd * head_rows + first + s, seg_len, stride=SEGS_PER_TILE), :] = (
                    p[s * seg_len:(s + 1) * seg_len, hd * LANES:(hd + 1) * LANES])
        g_ref[rows, :] = p[:, RNN_WIDTH:2 * RNN_WIDTH].astype(BF16)


def _odd_in_proj(x, mod, w_in):
    batch, seq, _ = x.shape
    seg_len = seq // N_SEG
    tile_rows = SEGS_PER_TILE * seg_len
    out_spec = pl.BlockSpec((None, RNN_HEADS, seg_len, SEGS_PER_TILE, RNN_HEAD_DIM), lambda b, i: (b, 0, 0, i, 0))
    out_shape = jax.ShapeDtypeStruct((batch, RNN_HEADS, seg_len, N_SEG, RNN_HEAD_DIM), F32)
    return pl.pallas_call(
        _odd_in_kernel,
        grid=(batch, seq // tile_rows),
        in_specs=[pl.BlockSpec((None, tile_rows, D_MODEL), lambda b, i: (b, i, 0)),
                  _mod_spec(mod, ODD_LAYER, MOD_SHIFT), _mod_spec(mod, ODD_LAYER, MOD_SCALE),
                  _resident((D_MODEL, 2 * RNN_WIDTH))],
        out_specs=[out_spec, pl.BlockSpec((None, tile_rows, RNN_WIDTH), lambda b, i: (b, i, 0))],
        out_shape=[out_shape, jax.ShapeDtypeStruct((batch, seq, RNN_WIDTH), BF16)],
        scratch_shapes=[pltpu.VMEM((D_MODEL, 2 * RNN_WIDTH), BF16)],
        compiler_params=_params("arbitrary", "arbitrary"),
        name="odd_in_proj",
    )(x, mod, mod, w_in)


def _rglru_kernel(x_ref, convw_ref, convb_ref, waug_ref, lam_ref, o_ref,
                  halo, a_f, b_f, a_b, b_b, edge):
    n_rows = x_ref.shape[0]
    seg_len = n_rows // N_SEG
    n_chunks = n_rows // SCAN_CHUNK
    left_rows = CONV_LEFT * N_SEG
    right_rows = (CONV_WIDTH - 1 - CONV_LEFT) * N_SEG
    seg = lax.broadcasted_iota(jnp.int32, (N_SEG, LANES), 0)

    def from_prev_segment(blk):
        return jnp.where(seg == 0, 0.0, pltpu.roll(blk, 1, 0))

    def from_next_segment(blk):
        return jnp.where(seg == N_SEG - 1, 0.0, pltpu.roll(blk, N_SEG - 1, 0))

    for t in range(CONV_LEFT):
        halo[0, pl.ds(t * N_SEG, N_SEG), :] = from_prev_segment(
            x_ref[pl.ds((seg_len - CONV_LEFT + t) * N_SEG, N_SEG), :])
    halo[0, pl.ds(left_rows, SCAN_CHUNK + right_rows), :] = x_ref[pl.ds(0, SCAN_CHUNK + right_rows), :]
    halo[1, pl.ds(0, left_rows + SCAN_CHUNK), :] = x_ref[pl.ds(n_rows - SCAN_CHUNK - left_rows, left_rows + SCAN_CHUNK), :]
    for t in range(CONV_WIDTH - 1 - CONV_LEFT):
        halo[1, pl.ds(left_rows + SCAN_CHUNK + t * N_SEG, N_SEG), :] = from_next_segment(
            x_ref[pl.ds(t * N_SEG, N_SEG), :])

    lam = lam_ref[...]
    softplus_neg = jnp.maximum(-lam, 0.0) + jnp.log1p(jnp.exp(-jnp.abs(lam)))
    half_rate = (0.5 * RG_LRU_C) * softplus_neg
    bias_lanes = jnp.broadcast_to(
        jnp.where(lax.broadcasted_iota(jnp.int32, (1, LANES), 1) < BIAS_ROWS, 1.0, 0.0).astype(BF16),
        (SCAN_CHUNK, LANES))

    def gates(base, tap):
        half_x = convb_ref[...] + sum(convw_ref[k:k + 1, :] * tap(k) for k in range(CONV_WIDTH))
        th = jnp.tanh(jnp.dot(jnp.concatenate([half_x.astype(BF16), bias_lanes], axis=1), waug_ref[...],
                              preferred_element_type=F32))
        for d, (a_ref, b_ref) in enumerate(((a_f, b_f), (a_b, b_b))):
            o = 2 * d * LANES
            neg_log_a = (th[:, o:o + LANES] + 1.0) * half_rate[d:d + 1, :]
            a = jnp.exp2(neg_log_a * (-LOG2_E))
            one_minus_a2 = jnp.tanh(neg_log_a) * (a * a + 1.0)
            root = one_minus_a2 * lax.rsqrt(jnp.maximum(one_minus_a2, F32_TINY))
            a_ref[pl.ds(base, SCAN_CHUNK), :] = a
            b_ref[pl.ds(base, SCAN_CHUNK), :] = root * ((th[:, o + LANES:o + 2 * LANES] + 1.0) * half_x)

    def inner_chunk(c, carry):
        base = pl.multiple_of(c * SCAN_CHUNK, SCAN_CHUNK)
        gates(base, lambda k: x_ref[pl.ds(base + (k - CONV_LEFT) * N_SEG, SCAN_CHUNK), :])
        return carry

    gates(0, lambda k: halo[0, pl.ds(k * N_SEG, SCAN_CHUNK), :])
    gates(n_rows - SCAN_CHUNK, lambda k: halo[1, pl.ds(k * N_SEG, SCAN_CHUNK), :])
    lax.fori_loop(1, n_chunks - 1, inner_chunk, 0, unroll=7)

    def scan_step(t, carry):
        h_f, p_f, h_b, p_b = carry
        rf = pl.ds(pl.multiple_of(t * N_SEG, N_SEG), N_SEG)
        rb = pl.ds(pl.multiple_of((seg_len - 1 - t) * N_SEG, N_SEG), N_SEG)
        a = a_f[rf, :]
        h_f = a * h_f + b_f[rf, :]
        p_f = a * p_f
        b_f[rf, :] = h_f
        a_f[rf, :] = p_f
        a = a_b[rb, :]
        h_b = a * h_b + b_b[rb, :]
        p_b = a * p_b
        b_b[rb, :] = h_b
        a_b[rb, :] = p_b
        return h_f, p_f, h_b, p_b

    zeros = jnp.zeros((N_SEG, LANES), F32)
    ones = jnp.ones((N_SEG, LANES), F32)
    h_f, p_f, h_b, p_b = lax.fori_loop(0, seg_len, scan_step, (zeros, ones, zeros, ones), unroll=4)

    edge[0] = h_f
    edge[1] = p_f
    edge[2] = h_b
    edge[3] = p_b
    enter_f = enter_b = jnp.zeros((1, LANES), F32)
    edge[4, 0:1, :] = enter_f
    edge[5, N_SEG - 1:N_SEG, :] = enter_b
    for s in range(1, N_SEG):
        enter_f = edge[0, s - 1:s, :] + edge[1, s - 1:s, :] * enter_f
        edge[4, s:s + 1, :] = enter_f
        r = N_SEG - 1 - s
        enter_b = edge[2, r + 1:r + 2, :] + edge[3, r + 1:r + 2, :] * enter_b
        edge[5, r:r + 1, :] = enter_b
    blocks = SCAN_CHUNK // N_SEG
    in_f = jnp.concatenate([edge[4]] * blocks, axis=0)
    in_b = jnp.concatenate([edge[5]] * blocks, axis=0)

    def out_chunk(c, carry):
        rows = pl.ds(pl.multiple_of(c * SCAN_CHUNK, SCAN_CHUNK), SCAN_CHUNK)
        o_ref[rows, :] = (b_f[rows, :] + a_f[rows, :] * in_f) + (b_b[rows, :] + a_b[rows, :] * in_b)
        return carry

    lax.fori_loop(0, n_rows // SCAN_CHUNK, out_chunk, 0, unroll=2)


def _rglru(xr, conv_w, conv_b, w_a, b_a, w_x, b_x, lam):
    batch, _, n_rows, _ = xr.shape
    w_cat = jnp.concatenate([w_a[0], w_x[0], w_a[1], w_x[1]], axis=-1).astype(BF16)
    per_head = lambda v: v.reshape(RNN_HEADS, 1, RNN_HEAD_DIM)
    half_b = 0.5 * jnp.concatenate([per_head(b_a[0]), per_head(b_x[0]), per_head(b_a[1]), per_head(b_x[1])],
                                   axis=-1)
    b_hi = half_b.astype(BF16)
    b_lo = (half_b - b_hi.astype(F32)).astype(BF16)
    w_aug = jnp.concatenate([w_cat, b_hi, b_lo,
                             jnp.zeros((RNN_HEADS, RNN_HEAD_DIM - BIAS_ROWS, 4 * RNN_HEAD_DIM), BF16)], axis=1)
    col = lambda rows: pl.BlockSpec((rows, RNN_HEAD_DIM), lambda b, h: (0, h))
    seq_col = pl.BlockSpec((None, None, n_rows, RNN_HEAD_DIM), lambda b, h: (b, h, 0, 0))
    halo_rows = SCAN_CHUNK + (CONV_WIDTH - 1) * N_SEG
    return pl.pallas_call(
        _rglru_kernel,
        grid=(batch, RNN_HEADS),
        in_specs=[seq_col, col(CONV_WIDTH), col(1),
                  pl.BlockSpec((None, 2 * RNN_HEAD_DIM, 4 * RNN_HEAD_DIM), lambda b, h: (h, 0, 0)),
                  col(2)],
        out_specs=seq_col,
        out_shape=jax.ShapeDtypeStruct((batch, RNN_HEADS, n_rows, RNN_HEAD_DIM), F32),
        scratch_shapes=[pltpu.VMEM((2, halo_rows, LANES), F32)]
                       + [pltpu.VMEM((n_rows, LANES), F32)] * 4
                       + [pltpu.VMEM((6, N_SEG, LANES), F32)],
        compiler_params=_params("parallel", "parallel"),
        name="rglru",
    )(xr, 0.5 * conv_w, 0.5 * conv_b.reshape(1, -1), w_aug, lam)


def _odd_out_kernel(y_ref, g_ref, x_ref, gate_ref, w_f32_ref, lng_ref, lnb_ref, o_ref, w_ref):
    _cast_weight_once(w_f32_ref, w_ref)
    seg_len = y_ref.shape[1]
    head_rows = seg_len * SEGS_PER_TILE
    y_rows = y_ref.reshape(RNN_HEADS * head_rows, LANES)
    gated = []
    for s in range(SEGS_PER_TILE):
        y = jnp.concatenate([y_rows[pl.ds(hd * head_rows + s, seg_len, stride=SEGS_PER_TILE), :]
                             for hd in range(RNN_HEADS)], axis=1)
        g = g_ref[s * seg_len:(s + 1) * seg_len, :].astype(F32)
        gated.append((y * (g * _sigmoid(g))).astype(BF16))
    out = jnp.dot(jnp.concatenate(gated, axis=0), w_ref[...], preferred_element_type=F32)
    o_ref[...] = _deepnorm_rows(x_ref[...], _batch_row(gate_ref), out,
                                lng_ref[ODD_LAYER:ODD_LAYER + 1, :], lnb_ref[ODD_LAYER:ODD_LAYER + 1, :])


def _odd_out_proj(y, g, x, mod, w_out, ln_g, ln_b):
    batch, seq, _ = x.shape
    seg_len = seq // N_SEG
    tile_rows = SEGS_PER_TILE * seg_len
    const = lambda shape: pl.BlockSpec(shape, lambda b, i: (0,) * len(shape))
    return pl.pallas_call(
        _odd_out_kernel,
        grid=(batch, seq // tile_rows),
        in_specs=[pl.BlockSpec((None, RNN_HEADS, seg_len, SEGS_PER_TILE, RNN_HEAD_DIM),
                               lambda b, i: (b, 0, 0, i, 0)),
                  pl.BlockSpec((None, tile_rows, RNN_WIDTH), lambda b, i: (b, i, 0)),
                  pl.BlockSpec((None, tile_rows, D_MODEL), lambda b, i: (b, i, 0)),
                  _mod_spec(mod, ODD_LAYER, MOD_GATE),
                  _resident((RNN_WIDTH, D_MODEL)), const(ln_g.shape), const(ln_b.shape)],
        out_specs=pl.BlockSpec((None, tile_rows, D_MODEL), lambda b, i: (b, i, 0)),
        out_shape=jax.ShapeDtypeStruct((batch, seq, D_MODEL), F32),
        scratch_shapes=[pltpu.VMEM((RNN_WIDTH, D_MODEL), BF16)],
        compiler_params=_params("arbitrary", "arbitrary"),
        name="odd_out_proj",
    )(y, g, x, mod, w_out, ln_g, ln_b)


def kernel(x, c, positions, ada_w, ada_b, ln_g, ln_b, ev_w_in, ev_w_out, ev_sink, ev_sg_ln_g, ev_sg_ln_b,
           ev_sg_w, ev_sg_b, od_w_in, od_conv_w, od_conv_b, od_w_a, od_b_a, od_w_x, od_b_x, od_lam, od_w_out):
    batch, seq, d_model = x.shape
    seg_len = seq // N_SEG
    assert d_model == D_MODEL and seq % ROW_TILE == 0 and seq % MIX_TILE == 0 and seq == N_SEG * seg_len
    assert N_SEG % SEGS_PER_TILE == 0 and seq % SCAN_CHUNK == 0 and SCAN_CHUNK % N_SEG == 0
    assert ada_w.shape[0] == DEPTH == 2

    mod = _modulation(c, ada_w, ada_b)

    q, k, v, su, sv, g = _even_in_proj(x, mod, positions, ev_w_in[0])
    x1 = _even_mixer(q, k, v, su, sv, g, x, mod, ev_sink[0], ev_sg_ln_g[0], ev_sg_ln_b[0],
                     ev_sg_w[0], ev_sg_b[0], ev_w_out[0], ln_g, ln_b)

    xr, g1 = _odd_in_proj(x1, mod, od_w_in[0])
    y = _rglru(xr.reshape(batch, RNN_HEADS, seq, RNN_HEAD_DIM), od_conv_w[0], od_conv_b[0], od_w_a[0],
               od_b_a[0], od_w_x[0], od_b_x[0], od_lam[0])
    return _odd_out_proj(y.reshape(batch, RNN_HEADS, seg_len, N_SEG, RNN_HEAD_DIM), g1, x1, mod,
                         od_w_out[0], ln_g, ln_b)
```

```python
import math

import numpy as np
import jax
import jax.numpy as jnp
from jax import lax
from jax.experimental import pallas as pl
from jax.experimental.pallas import tpu as pltpu

D_MODEL = 1024
DEPTH = 2
HEAD_DIM = 64
N_Q_HEADS = 8
N_KV_HEADS = 2
Q_PER_KV = N_Q_HEADS // N_KV_HEADS
ATTN_WIDTH = N_Q_HEADS * HEAD_DIM
KV_WIDTH = N_KV_HEADS * HEAD_DIM
ATTN_BLOCK = 128
ROPE_DIM = HEAD_DIM // 4
ROPE_HALF = ROPE_DIM // 2
ROPE_THETA = 500000.0
N_SG_GROUPS = 8
SG_GROUP_DIM = 64
SG_WIDTH = N_SG_GROUPS * SG_GROUP_DIM
SG_CHUNK = 128
EVEN_GATE_WIDTH = ATTN_WIDTH + SG_WIDTH
EVEN_IN_WIDTH = ATTN_WIDTH + 2 * KV_WIDTH + 2 * SG_WIDTH + EVEN_GATE_WIDTH
RNN_WIDTH = D_MODEL
RNN_HEADS = 8
RNN_HEAD_DIM = RNN_WIDTH // RNN_HEADS
CONV_WIDTH = 4
CONV_LEFT = 2
RG_LRU_C = 8.0
DEEPNORM_ALPHA = (2 * DEPTH) ** 0.25
LN_EPS = 1e-5
NEG_INF = -1e30
LOG2_E = math.log2(math.e)
F32_TINY = float(np.finfo(np.float32).tiny)

LANES = 128
SUBLANES = 8
VMEM_LIMIT = 56 * 1024 * 1024

ROW_TILE = 512
MIX_TILE = 512
N_SEG = 32
SEGS_PER_TILE = SUBLANES
SCAN_CHUNK = 256
WEIGHT_CAST_ROWS = 128
BIAS_ROWS = 2

BF16 = jnp.bfloat16
F32 = jnp.float32


def _sigmoid(x):
    return 0.5 * jnp.tanh(0.5 * x) + 0.5


def _layer_norm_rows(z, g, b, eps=LN_EPS):
    mu = jnp.mean(z, axis=-1, keepdims=True)
    d = z - mu
    var = jnp.mean(d * d, axis=-1, keepdims=True)
    return d * lax.rsqrt(var + eps) * g + b


def _deepnorm_sum(x, gate, y):
    return x + (gate * (1.0 / DEEPNORM_ALPHA)) * y


def _deepnorm_finish(z, g, b):
    return _layer_norm_rows(z, g, b, LN_EPS / DEEPNORM_ALPHA ** 2)


def _params(*semantics):
    return pltpu.CompilerParams(dimension_semantics=semantics, vmem_limit_bytes=VMEM_LIMIT)


def _cast_weight_once(w_ref, w_bf16):
    @pl.when((pl.program_id(0) == 0) & (pl.program_id(1) == 0))
    def _():
        for r in range(0, w_ref.shape[0], WEIGHT_CAST_ROWS):
            w_bf16[r:r + WEIGHT_CAST_ROWS, :] = w_ref[r:r + WEIGHT_CAST_ROWS, :].astype(BF16)


def _resident(shape):
    return pl.BlockSpec(shape, lambda b, i: (0,) * len(shape), pipeline_mode=pl.Buffered(1))


def _mod_kernel(c_ref, w_ref, b_ref, o_ref):
    c = c_ref[...]
    cond = c * _sigmoid(c)
    pad_rows = o_ref.shape[0] - cond.shape[0]
    if pad_rows:
        cond = jnp.concatenate([cond, jnp.zeros((pad_rows, cond.shape[1]), F32)], axis=0)
    bias = b_ref[pl.ds(pl.program_id(0), 1), :]
    o_ref[...] = jnp.dot(cond.astype(BF16), w_ref[...].astype(BF16), preferred_element_type=F32) + bias


def _modulation(c, ada_w, ada_b):
    batch = c.shape[0]
    rows = -(-batch // SUBLANES) * SUBLANES
    n_col = 3 * D_MODEL // D_MODEL
    return pl.pallas_call(
        _mod_kernel,
        grid=(DEPTH, n_col),
        in_specs=[
            pl.BlockSpec((batch, D_MODEL), lambda l, j: (0, 0)),
            pl.BlockSpec((None, D_MODEL, D_MODEL), lambda l, j: (l, 0, j)),
            pl.BlockSpec((DEPTH, D_MODEL), lambda l, j: (0, j)),
        ],
        out_specs=pl.BlockSpec((None, rows, D_MODEL), lambda l, j: (l, 0, j)),
        out_shape=jax.ShapeDtypeStruct((DEPTH, rows, 3 * D_MODEL), F32),
        compiler_params=_params("parallel", "parallel"),
        name="adaln_mod",
    )(c, ada_w, ada_b)


MOD_SHIFT, MOD_SCALE, MOD_GATE = range(3)
EVEN_LAYER, ODD_LAYER = 0, 1


def _mod_spec(mod, layer, part):
    return pl.BlockSpec((None, mod.shape[1], D_MODEL), lambda b, i: (layer, 0, part))


def _batch_row(ref):
    return ref[pl.ds(pl.program_id(0), 1), :]


def _even_in_kernel(x_ref, shift_ref, scale_ref, pos_ref, freq_ref, sign_ref, w_f32_ref,
                    q_ref, k_ref, v_ref, su_ref, sv_ref, g_ref, w_ref):
    _cast_weight_once(w_f32_ref, w_ref)
    lane = lax.broadcasted_iota(jnp.int32, (1, LANES), 1)
    first_half = (lane % HEAD_DIM) < ROPE_HALF
    left_head = lane < HEAD_DIM
    diagonal = lax.broadcasted_iota(jnp.int32, (LANES, LANES), 0) == lax.broadcasted_iota(jnp.int32, (LANES, LANES), 1)
    batch_ids = lax.broadcasted_iota(jnp.int32, (pos_ref.shape[0], LANES), 0)
    q_scale = HEAD_DIM ** -0.5 * LOG2_E

    half_rows = x_ref.shape[0] // 2
    for r in range(2):
        rows = slice(r * half_rows, (r + 1) * half_rows)
        h = (x_ref[rows, :] * (1.0 + _batch_row(scale_ref)) + _batch_row(shift_ref)).astype(BF16)
        p = jnp.dot(h, w_ref[...], preferred_element_type=F32)

        pos_cols = []
        for j in range(r * half_rows // LANES, (r + 1) * half_rows // LANES):
            pos_all = pos_ref[:, j * LANES:(j + 1) * LANES].astype(F32)
            pos_row = jnp.sum(jnp.where(batch_ids == pl.program_id(0), pos_all, 0.0), axis=0, keepdims=True)
            pos_cols.append(jnp.sum(jnp.where(diagonal, pos_row, 0.0), axis=1, keepdims=True))
        ang = jnp.concatenate(pos_cols, axis=0) * freq_ref[...]
        cs = jnp.cos(ang)
        sn = jnp.sin(ang) * sign_ref[...]

        def rotary(t):
            partner = jnp.where(first_half, pltpu.roll(t, LANES - ROPE_HALF, 1), pltpu.roll(t, ROPE_HALF, 1))
            return t * cs + partner * sn

        for j in range(ATTN_WIDTH // LANES):
            q_ref[rows, j * LANES:(j + 1) * LANES] = (
                rotary(p[:, j * LANES:(j + 1) * LANES]) * q_scale).astype(BF16)
        o = ATTN_WIDTH
        k = rotary(p[:, o:o + KV_WIDTH])
        k_sw = pltpu.roll(k, HEAD_DIM, 1)
        k_ref[rows, 0:LANES] = jnp.where(left_head, k, k_sw).astype(BF16)
        k_ref[rows, LANES:2 * LANES] = jnp.where(left_head, k_sw, k).astype(BF16)
        o += KV_WIDTH
        v = p[:, o:o + KV_WIDTH]
        v_sw = pltpu.roll(v, HEAD_DIM, 1)
        v_ref[rows, 0 * LANES:1 * LANES] = jnp.where(left_head, v, 0.0).astype(BF16)
        v_ref[rows, 1 * LANES:2 * LANES] = jnp.where(left_head, 0.0, v_sw).astype(BF16)
        v_ref[rows, 2 * LANES:3 * LANES] = jnp.where(left_head, v_sw, 0.0).astype(BF16)
        v_ref[rows, 3 * LANES:4 * LANES] = jnp.where(left_head, 0.0, v).astype(BF16)
        o += KV_WIDTH
        su_ref[rows, :] = p[:, o:o + SG_WIDTH].astype(BF16)
        o += SG_WIDTH
        sv_ref[rows, :] = p[:, o:o + SG_WIDTH].astype(BF16)
        o += SG_WIDTH
        g_ref[rows, :] = p[:, o:o + EVEN_GATE_WIDTH].astype(BF16)


def _even_in_proj(x, mod, positions, w_in):
    batch, seq, _ = x.shape
    d = np.arange(LANES) % HEAD_DIM
    inv_freq = np.power(np.float64(ROPE_THETA), -np.arange(ROPE_HALF) / ROPE_HALF)
    freq_lane = jnp.asarray(np.where(d < ROPE_DIM, inv_freq[d % ROPE_HALF], 0.0), dtype=F32).reshape(1, LANES)
    sign_lane = jnp.asarray(np.where(d < ROPE_HALF, -1.0, np.where(d < ROPE_DIM, 1.0, 0.0)),
                            dtype=F32).reshape(1, LANES)
    row = lambda width: pl.BlockSpec((None, ROW_TILE, width), lambda b, i: (b, i, 0))
    const = lambda shape: pl.BlockSpec(shape, lambda b, i: (0,) * len(shape))
    widths = (ATTN_WIDTH, 2 * KV_WIDTH, 4 * KV_WIDTH, SG_WIDTH, SG_WIDTH, EVEN_GATE_WIDTH)
    return pl.pallas_call(
        _even_in_kernel,
        grid=(batch, seq // ROW_TILE),
        in_specs=[row(D_MODEL), _mod_spec(mod, EVEN_LAYER, MOD_SHIFT), _mod_spec(mod, EVEN_LAYER, MOD_SCALE),
                  pl.BlockSpec((batch, ROW_TILE), lambda b, i: (0, i)),
                  const((1, LANES)), const((1, LANES)), _resident((D_MODEL, EVEN_IN_WIDTH))],
        out_specs=[row(w) for w in widths],
        out_shape=[jax.ShapeDtypeStruct((batch, seq, w), BF16) for w in widths],
        scratch_shapes=[pltpu.VMEM((D_MODEL, EVEN_IN_WIDTH), BF16)],
        compiler_params=_params("arbitrary", "arbitrary"),
        name="even_in_proj",
    )(x, mod, mod, positions, freq_lane, sign_lane, w_in)


def _even_mix_kernel(sink_ref, q_ref, k_ref, v_ref, su_ref, sv_ref, g_ref, x_ref, gate_ref,
                     avg_ref, avg2_ref, sgg_ref, sgb_ref, sgw_ref, sgbias_ref, wout_f32_ref,
                     o_ref, y_scr, wout_ref):
    _cast_weight_once(wout_f32_ref, wout_ref)
    seq = k_ref.shape[0]
    n_blocks = seq // ATTN_BLOCK
    sub_blocks = MIX_TILE // ATTN_BLOCK
    tile = pl.program_id(1)
    lane = lax.broadcasted_iota(jnp.int32, (1, LANES), 1)
    left = lane < HEAD_DIM

    qi = lax.broadcasted_iota(jnp.int32, (ATTN_BLOCK, ATTN_BLOCK), 0)
    kj = lax.broadcasted_iota(jnp.int32, (ATTN_BLOCK, ATTN_BLOCK), 1)
    ones_left = jnp.broadcast_to(jnp.where(left, 1.0, 0.0).astype(BF16), (3 * ATTN_BLOCK, LANES))
    ones_right = jnp.broadcast_to(jnp.where(left, 0.0, 1.0).astype(BF16), (3 * ATTN_BLOCK, LANES))
    avg = avg_ref[...]
    sink_cols = [jnp.concatenate([jnp.full((ATTN_BLOCK, LANES), sink_ref[h] * LOG2_E, F32)
                                  for h in (4 * hk, 4 * hk + 2, 4 * hk + 1, 4 * hk + 3)], axis=0)
                 for hk in range(N_KV_HEADS)]

    def sub_block(j, carry):
        blk = tile * sub_blocks + j
        prev = jnp.maximum(blk - 1, 0)
        nxt = jnp.minimum(blk + 1, n_blocks - 1)
        rows = pl.ds(pl.multiple_of(j * ATTN_BLOCK, ATTN_BLOCK), ATTN_BLOCK)
        bias_prev = jnp.where((kj >= qi) & (blk > 0), 0.0, NEG_INF)
        bias_next = jnp.where((kj <= qi) & (blk < n_blocks - 1), 0.0, NEG_INF)
        bias_prev = jnp.concatenate([bias_prev] * Q_PER_KV, axis=0)
        bias_next = jnp.concatenate([bias_next] * Q_PER_KV, axis=0)

        def band(ref):
            return jnp.concatenate(
                [ref[pl.ds(pl.multiple_of(b * ATTN_BLOCK, ATTN_BLOCK), ATTN_BLOCK), :] for b in (prev, blk, nxt)],
                axis=0)

        k_band = band(k_ref)
        v_band = band(v_ref)
        n_sg = SG_WIDTH // LANES
        sg_cols = [slice(c * LANES, (c + 1) * LANES) for c in range(n_sg)]

        def silu_gate(cols):
            g_c = g_ref[rows, cols].astype(F32)
            return g_c * _sigmoid(g_c)

        scores = []
        for hk in range(N_KV_HEADS):
            q_cols = [q_ref[rows, (2 * hk + c) * LANES:(2 * hk + c + 1) * LANES] for c in range(2)]
            zero = jnp.zeros_like(q_cols[0])
            lhs = jnp.concatenate([jnp.where(left, qc, zero) for qc in q_cols]
                                  + [jnp.where(left, zero, qc) for qc in q_cols], axis=0)
            scores.append(lax.dot_general(lhs, k_band[:, hk * LANES:(hk + 1) * LANES], (((1,), (1,)), ((), ())),
                                          preferred_element_type=F32))
        sv_cols = [sv_ref[rows, cols] for cols in sg_cols]
        means = [jnp.dot(v_c, avg, preferred_element_type=F32) for v_c in sv_cols]

        probs, row_max = [], []
        for hk in range(N_KV_HEADS):
            s = scores[hk]
            s0 = s[:, 0:ATTN_BLOCK] + bias_prev
            s1 = s[:, ATTN_BLOCK:2 * ATTN_BLOCK]
            s2 = s[:, 2 * ATTN_BLOCK:3 * ATTN_BLOCK] + bias_next
            m = jnp.max(jnp.maximum(jnp.maximum(s0, s1), s2), axis=-1, keepdims=True)
            m = jnp.maximum(jnp.broadcast_to(m, sink_cols[hk].shape), sink_cols[hk])
            probs.append(jnp.concatenate([jnp.exp2(t - m).astype(BF16) for t in (s0, s1, s2)], axis=1))
            row_max.append(m)
        devs, sq_parts = [], []
        for v_c, mean in zip(sv_cols, means):
            dev = v_c.astype(F32) - mean
            sq = dev * dev
            sq_hi = sq.astype(BF16)
            sq_lo = (sq - sq_hi.astype(F32)).astype(BF16)
            devs.append(dev)
            sq_parts.append(jnp.concatenate([sq_hi, sq_lo], axis=1))

        pv = []
        for hk in range(N_KV_HEADS):
            p = probs[hk]
            lhs = jnp.concatenate(
                [jnp.concatenate([p[c * ATTN_BLOCK:(c + 1) * ATTN_BLOCK], p[(2 + c) * ATTN_BLOCK:(3 + c) * ATTN_BLOCK]],
                                 axis=1) for c in range(2)], axis=0)
            rhs = jnp.concatenate(
                [jnp.concatenate([v_band[:, (2 * hk) * LANES:(2 * hk + 1) * LANES], ones_left], axis=1),
                 jnp.concatenate([v_band[:, (2 * hk + 1) * LANES:(2 * hk + 2) * LANES], ones_right], axis=1)],
                axis=0)
            pv.append(jnp.dot(lhs, rhs, preferred_element_type=F32))
        variances = [jnp.dot(sq, avg2_ref[...], preferred_element_type=F32) for sq in sq_parts]

        for hk in range(N_KV_HEADS):
            for c in range(2):
                even, odd = slice(c * ATTN_BLOCK, (c + 1) * ATTN_BLOCK), slice((2 + c) * ATTN_BLOCK, (3 + c) * ATTN_BLOCK)
                sink_term = jnp.exp2(jnp.where(left, sink_cols[hk][even] - row_max[hk][even],
                                               sink_cols[hk][odd] - row_max[hk][odd]))
                o = pv[hk][c * ATTN_BLOCK:(c + 1) * ATTN_BLOCK]
                pair = o[:, 0:LANES] * (1.0 / (o[:, LANES:2 * LANES] + sink_term))
                cols = slice((2 * hk + c) * LANES, (2 * hk + c + 1) * LANES)
                y_scr[rows, cols] = (pair * silu_gate(cols)).astype(BF16)

        for c, cols in enumerate(sg_cols):
            vn = devs[c] * lax.rsqrt(variances[c] + LN_EPS) * sgg_ref[:, cols] + sgb_ref[:, cols]
            stacked = jnp.concatenate([jnp.where(left, vn, 0.0), jnp.where(left, 0.0, vn)],
                                      axis=0).astype(BF16)
            mixed = jnp.dot(sgw_ref[c], stacked, preferred_element_type=F32) + sgbias_ref[:, cols]
            y_sg = su_ref[rows, cols].astype(F32) * mixed
            gc = slice(ATTN_WIDTH + c * LANES, ATTN_WIDTH + (c + 1) * LANES)
            y_scr[rows, gc] = (y_sg * silu_gate(gc)).astype(BF16)
        return carry

    lax.fori_loop(0, sub_blocks, sub_block, 0, unroll=True)

    for r in range(2):
        half_rows = slice(r * (MIX_TILE // 2), (r + 1) * (MIX_TILE // 2))
        out = jnp.dot(y_scr[half_rows, :], wout_ref[...], preferred_element_type=F32)
        o_ref[half_rows, :] = _deepnorm_sum(x_ref[half_rows, :], _batch_row(gate_ref), out)


def _even_mixer(q, k, v, su, sv, g, x, mod, sink, sg_ln_g, sg_ln_b, sg_w, sg_b, w_out):
    batch, seq, _ = x.shape
    lane = np.arange(LANES)
    avg = jnp.asarray((lane[:, None] // SG_GROUP_DIM == lane[None, :] // SG_GROUP_DIM) / SG_GROUP_DIM, dtype=BF16)
    w_pairs = sg_w.reshape(N_SG_GROUPS // 2, 2, SG_CHUNK, SG_CHUNK).transpose(0, 2, 1, 3)
    w_pairs = w_pairs.reshape(N_SG_GROUPS // 2, SG_CHUNK, 2 * SG_CHUNK).astype(BF16)
    bias_full = jnp.repeat(sg_b.T, SG_GROUP_DIM, axis=1)
    row = lambda width: pl.BlockSpec((None, MIX_TILE, width), lambda b, i: (b, i, 0))
    full_seq = lambda arr: pl.BlockSpec((None, seq, arr.shape[-1]), lambda b, i: (b, 0, 0))
    const = lambda shape: pl.BlockSpec(shape, lambda b, i: (0,) * len(shape))
    return pl.pallas_call(
        _even_mix_kernel,
        grid=(batch, seq // MIX_TILE),
        in_specs=[
            pl.BlockSpec(memory_space=pltpu.SMEM),
            row(ATTN_WIDTH), full_seq(k), full_seq(v), row(SG_WIDTH), row(SG_WIDTH), row(EVEN_GATE_WIDTH),
            row(D_MODEL), _mod_spec(mod, EVEN_LAYER, MOD_GATE),
            const((LANES, LANES)), const((2 * LANES, LANES)), const((1, SG_WIDTH)), const((1, SG_WIDTH)),
            const((N_SG_GROUPS // 2, SG_CHUNK, 2 * SG_CHUNK)), const((SG_CHUNK, SG_WIDTH)),
            _resident((EVEN_GATE_WIDTH, D_MODEL)),
        ],
        out_specs=row(D_MODEL),
        out_shape=jax.ShapeDtypeStruct((batch, seq, D_MODEL), F32),
        scratch_shapes=[pltpu.VMEM((MIX_TILE, EVEN_GATE_WIDTH), BF16),
                        pltpu.VMEM((EVEN_GATE_WIDTH, D_MODEL), BF16)],
        compiler_params=_params("arbitrary", "arbitrary"),
        name="even_mixer",
    )(sink, q, k, v, su, sv, g, x, mod, avg, jnp.concatenate([avg, avg], axis=0), sg_ln_g.reshape(1, -1), sg_ln_b.reshape(1, -1),
      w_pairs, bias_full, w_out)


def _odd_in_kernel(z_ref, shift_ref, scale_ref, lng_ref, lnb_ref, w_f32_ref, xr_ref, g_ref, w_ref):
    _cast_weight_once(w_f32_ref, w_ref)
    seg_len = xr_ref.shape[1]
    head_rows = seg_len * SEGS_PER_TILE
    xr_rows = xr_ref.reshape(RNN_HEADS * head_rows, LANES)
    halves = 2
    for r in range(halves):
        first = r * (SEGS_PER_TILE // halves)
        rows = slice(first * seg_len, (first + SEGS_PER_TILE // halves) * seg_len)
        x1 = _deepnorm_finish(z_ref[rows, :], lng_ref[EVEN_LAYER:EVEN_LAYER + 1, :], lnb_ref[EVEN_LAYER:EVEN_LAYER + 1, :])
        h = (x1 * (1.0 + _batch_row(scale_ref)) + _batch_row(shift_ref)).astype(BF16)
        p = jnp.dot(h, w_ref[...], preferred_element_type=F32)
        for s in range(SEGS_PER_TILE // halves):
            for hd in range(RNN_HEADS):
                xr_rows[pl.ds(hd * head_rows + first + s, seg_len, stride=SEGS_PER_TILE), :] = (
                    p[s * seg_len:(s + 1) * seg_len, hd * LANES:(hd + 1) * LANES])
        g_ref[rows, :] = p[:, RNN_WIDTH:2 * RNN_WIDTH].astype(BF16)


def _odd_in_proj(z, mod, ln_g, ln_b, w_in):
    batch, seq, _ = z.shape
    const = lambda shape: pl.BlockSpec(shape, lambda b, i: (0,) * len(shape))
    seg_len = seq // N_SEG
    tile_rows = SEGS_PER_TILE * seg_len
    out_spec = pl.BlockSpec((None, RNN_HEADS, seg_len, SEGS_PER_TILE, RNN_HEAD_DIM), lambda b, i: (b, 0, 0, i, 0))
    out_shape = jax.ShapeDtypeStruct((batch, RNN_HEADS, seg_len, N_SEG, RNN_HEAD_DIM), F32)
    return pl.pallas_call(
        _odd_in_kernel,
        grid=(batch, seq // tile_rows),
        in_specs=[pl.BlockSpec((None, tile_rows, D_MODEL), lambda b, i: (b, i, 0)),
                  _mod_spec(mod, ODD_LAYER, MOD_SHIFT), _mod_spec(mod, ODD_LAYER, MOD_SCALE),
                  const(ln_g.shape), const(ln_b.shape), _resident((D_MODEL, 2 * RNN_WIDTH))],
        out_specs=[out_spec, pl.BlockSpec((None, tile_rows, RNN_WIDTH), lambda b, i: (b, i, 0))],
        out_shape=[out_shape, jax.ShapeDtypeStruct((batch, seq, RNN_WIDTH), BF16)],
        scratch_shapes=[pltpu.VMEM((D_MODEL, 2 * RNN_WIDTH), BF16)],
        compiler_params=_params("arbitrary", "arbitrary"),
        name="odd_in_proj",
    )(z, mod, mod, ln_g, ln_b, w_in)


def _rglru_kernel(x_ref, convw_ref, convb_ref, waug_ref, lam_ref, o_ref,
                  halo, a_f, b_f, a_b, b_b, edge):
    n_rows = x_ref.shape[0]
    seg_len = n_rows // N_SEG
    n_chunks = n_rows // SCAN_CHUNK
    left_rows = CONV_LEFT * N_SEG
    right_rows = (CONV_WIDTH - 1 - CONV_LEFT) * N_SEG
    seg = lax.broadcasted_iota(jnp.int32, (N_SEG, LANES), 0)

    def from_prev_segment(blk):
        return jnp.where(seg == 0, 0.0, pltpu.roll(blk, 1, 0))

    def from_next_segment(blk):
        return jnp.where(seg == N_SEG - 1, 0.0, pltpu.roll(blk, N_SEG - 1, 0))

    for t in range(CONV_LEFT):
        halo[0, pl.ds(t * N_SEG, N_SEG), :] = from_prev_segment(
            x_ref[pl.ds((seg_len - CONV_LEFT + t) * N_SEG, N_SEG), :])
    halo[0, pl.ds(left_rows, SCAN_CHUNK + right_rows), :] = x_ref[pl.ds(0, SCAN_CHUNK + right_rows), :]
    halo[1, pl.ds(0, left_rows + SCAN_CHUNK), :] = x_ref[pl.ds(n_rows - SCAN_CHUNK - left_rows, left_rows + SCAN_CHUNK), :]
    for t in range(CONV_WIDTH - 1 - CONV_LEFT):
        halo[1, pl.ds(left_rows + SCAN_CHUNK + t * N_SEG, N_SEG), :] = from_next_segment(
            x_ref[pl.ds(t * N_SEG, N_SEG), :])

    lam = lam_ref[...]
    softplus_neg = jnp.maximum(-lam, 0.0) + jnp.log1p(jnp.exp(-jnp.abs(lam)))
    half_rate = (0.5 * RG_LRU_C) * softplus_neg
    bias_lanes = jnp.broadcast_to(
        jnp.where(lax.broadcasted_iota(jnp.int32, (1, LANES), 1) < BIAS_ROWS, 1.0, 0.0).astype(BF16),
        (SCAN_CHUNK, LANES))

    def gates(base, tap):
        half_x = convb_ref[...] + sum(convw_ref[k:k + 1, :] * tap(k) for k in range(CONV_WIDTH))
        th = jnp.tanh(jnp.dot(jnp.concatenate([half_x.astype(BF16), bias_lanes], axis=1), waug_ref[...],
                              preferred_element_type=F32))
        for d, (a_ref, b_ref) in enumerate(((a_f, b_f), (a_b, b_b))):
            o = 2 * d * LANES
            neg_log_a = (th[:, o:o + LANES] + 1.0) * half_rate[d:d + 1, :]
            a = jnp.exp2(neg_log_a * (-LOG2_E))
            one_minus_a2 = jnp.tanh(neg_log_a) * (a * a + 1.0)
            root = one_minus_a2 * lax.rsqrt(jnp.maximum(one_minus_a2, F32_TINY))
            a_ref[pl.ds(base, SCAN_CHUNK), :] = a
            b_ref[pl.ds(base, SCAN_CHUNK), :] = root * ((th[:, o + LANES:o + 2 * LANES] + 1.0) * half_x)

    def inner_chunk(c, carry):
        base = pl.multiple_of(c * SCAN_CHUNK, SCAN_CHUNK)
        gates(base, lambda k: x_ref[pl.ds(base + (k - CONV_LEFT) * N_SEG, SCAN_CHUNK), :])
        return carry

    gates(0, lambda k: halo[0, pl.ds(k * N_SEG, SCAN_CHUNK), :])
    gates(n_rows - SCAN_CHUNK, lambda k: halo[1, pl.ds(k * N_SEG, SCAN_CHUNK), :])
    lax.fori_loop(1, n_chunks - 1, inner_chunk, 0, unroll=7)

    def scan_step(t, carry):
        h_f, p_f, h_b, p_b = carry
        rf = pl.ds(pl.multiple_of(t * N_SEG, N_SEG), N_SEG)
        rb = pl.ds(pl.multiple_of((seg_len - 1 - t) * N_SEG, N_SEG), N_SEG)
        a = a_f[rf, :]
        h_f = a * h_f + b_f[rf, :]
        p_f = a * p_f
        b_f[rf, :] = h_f
        a_f[rf, :] = p_f
        a = a_b[rb, :]
        h_b = a * h_b + b_b[rb, :]
        p_b = a * p_b
        b_b[rb, :] = h_b
        a_b[rb, :] = p_b
        return h_f, p_f, h_b, p_b

    zeros = jnp.zeros((N_SEG, LANES), F32)
    ones = jnp.ones((N_SEG, LANES), F32)
    h_f, p_f, h_b, p_b = lax.fori_loop(0, seg_len, scan_step, (zeros, ones, zeros, ones), unroll=4)

    edge[0] = h_f
    edge[1] = p_f
    edge[2] = h_b
    edge[3] = p_b
    enter_f = enter_b = jnp.zeros((1, LANES), F32)
    edge[4, 0:1, :] = enter_f
    edge[5, N_SEG - 1:N_SEG, :] = enter_b
    for s in range(1, N_SEG):
        enter_f = edge[0, s - 1:s, :] + edge[1, s - 1:s, :] * enter_f
        edge[4, s:s + 1, :] = enter_f
        r = N_SEG - 1 - s
        enter_b = edge[2, r + 1:r + 2, :] + edge[3, r + 1:r + 2, :] * enter_b
        edge[5, r:r + 1, :] = enter_b
    blocks = SCAN_CHUNK // N_SEG
    in_f = jnp.concatenate([edge[4]] * blocks, axis=0)
    in_b = jnp.concatenate([edge[5]] * blocks, axis=0)

    def out_chunk(c, carry):
        rows = pl.ds(pl.multiple_of(c * SCAN_CHUNK, SCAN_CHUNK), SCAN_CHUNK)
        o_ref[rows, :] = (b_f[rows, :] + a_f[rows, :] * in_f) + (b_b[rows, :] + a_b[rows, :] * in_b)
        return carry

    lax.fori_loop(0, n_rows // SCAN_CHUNK, out_chunk, 0, unroll=2)


def _rglru(xr, conv_w, conv_b, w_a, b_a, w_x, b_x, lam):
    batch, _, n_rows, _ = xr.shape
    w_cat = jnp.concatenate([w_a[0], w_x[0], w_a[1], w_x[1]], axis=-1).astype(BF16)
    per_head = lambda v: v.reshape(RNN_HEADS, 1, RNN_HEAD_DIM)
    half_b = 0.5 * jnp.concatenate([per_head(b_a[0]), per_head(b_x[0]), per_head(b_a[1]), per_head(b_x[1])],
                                   axis=-1)
    b_hi = half_b.astype(BF16)
    b_lo = (half_b - b_hi.astype(F32)).astype(BF16)
    w_aug = jnp.concatenate([w_cat, b_hi, b_lo,
                             jnp.zeros((RNN_HEADS, RNN_HEAD_DIM - BIAS_ROWS, 4 * RNN_HEAD_DIM), BF16)], axis=1)
    col = lambda rows: pl.BlockSpec((rows, RNN_HEAD_DIM), lambda b, h: (0, h))
    seq_col = pl.BlockSpec((None, None, n_rows, RNN_HEAD_DIM), lambda b, h: (b, h, 0, 0))
    halo_rows = SCAN_CHUNK + (CONV_WIDTH - 1) * N_SEG
    return pl.pallas_call(
        _rglru_kernel,
        grid=(batch, RNN_HEADS),
        in_specs=[seq_col, col(CONV_WIDTH), col(1),
                  pl.BlockSpec((None, 2 * RNN_HEAD_DIM, 4 * RNN_HEAD_DIM), lambda b, h: (h, 0, 0)),
                  col(2)],
        out_specs=seq_col,
        out_shape=jax.ShapeDtypeStruct((batch, RNN_HEADS, n_rows, RNN_HEAD_DIM), F32),
        scratch_shapes=[pltpu.VMEM((2, halo_rows, LANES), F32)]
                       + [pltpu.VMEM((n_rows, LANES), F32)] * 4
                       + [pltpu.VMEM((6, N_SEG, LANES), F32)],
        compiler_params=_params("parallel", "parallel"),
        name="rglru",
    )(xr, 0.5 * conv_w, 0.5 * conv_b.reshape(1, -1), w_aug, lam)


def _odd_out_kernel(y_ref, g_ref, z_ref, gate_ref, w_f32_ref, lng_ref, lnb_ref, o_ref, w_ref):
    _cast_weight_once(w_f32_ref, w_ref)
    seg_len = y_ref.shape[1]
    head_rows = seg_len * SEGS_PER_TILE
    y_rows = y_ref.reshape(RNN_HEADS * head_rows, LANES)
    gated = []
    for s in range(SEGS_PER_TILE):
        y = jnp.concatenate([y_rows[pl.ds(hd * head_rows + s, seg_len, stride=SEGS_PER_TILE), :]
                             for hd in range(RNN_HEADS)], axis=1)
        g = g_ref[s * seg_len:(s + 1) * seg_len, :].astype(F32)
        gated.append((y * (g * _sigmoid(g))).astype(BF16))
    out = jnp.dot(jnp.concatenate(gated, axis=0), w_ref[...], preferred_element_type=F32)
    x1 = _deepnorm_finish(z_ref[...], lng_ref[EVEN_LAYER:EVEN_LAYER + 1, :], lnb_ref[EVEN_LAYER:EVEN_LAYER + 1, :])
    o_ref[...] = _deepnorm_finish(_deepnorm_sum(x1, _batch_row(gate_ref), out),
                                  lng_ref[ODD_LAYER:ODD_LAYER + 1, :], lnb_ref[ODD_LAYER:ODD_LAYER + 1, :])


def _odd_out_proj(y, g, z, mod, w_out, ln_g, ln_b):
    batch, seq, _ = z.shape
    seg_len = seq // N_SEG
    tile_rows = SEGS_PER_TILE * seg_len
    const = lambda shape: pl.BlockSpec(shape, lambda b, i: (0,) * len(shape))
    return pl.pallas_call(
        _odd_out_kernel,
        grid=(batch, seq // tile_rows),
        in_specs=[pl.BlockSpec((None, RNN_HEADS, seg_len, SEGS_PER_TILE, RNN_HEAD_DIM),
                               lambda b, i: (b, 0, 0, i, 0)),
                  pl.BlockSpec((None, tile_rows, RNN_WIDTH), lambda b, i: (b, i, 0)),
                  pl.BlockSpec((None, tile_rows, D_MODEL), lambda b, i: (b, i, 0)),
                  _mod_spec(mod, ODD_LAYER, MOD_GATE),
                  _resident((RNN_WIDTH, D_MODEL)), const(ln_g.shape), const(ln_b.shape)],
        out_specs=pl.BlockSpec((None, tile_rows, D_MODEL), lambda b, i: (b, i, 0)),
        out_shape=jax.ShapeDtypeStruct((batch, seq, D_MODEL), F32),
        scratch_shapes=[pltpu.VMEM((RNN_WIDTH, D_MODEL), BF16)],
        compiler_params=_params("arbitrary", "arbitrary"),
        name="odd_out_proj",
    )(y, g, z, mod, w_out, ln_g, ln_b)


def kernel(x, c, positions, ada_w, ada_b, ln_g, ln_b, ev_w_in, ev_w_out, ev_sink, ev_sg_ln_g, ev_sg_ln_b,
           ev_sg_w, ev_sg_b, od_w_in, od_conv_w, od_conv_b, od_w_a, od_b_a, od_w_x, od_b_x, od_lam, od_w_out):
    batch, seq, d_model = x.shape
    seg_len = seq // N_SEG
    assert d_model == D_MODEL and seq % ROW_TILE == 0 and seq % MIX_TILE == 0 and seq == N_SEG * seg_len
    assert N_SEG % SEGS_PER_TILE == 0 and seq % SCAN_CHUNK == 0 and SCAN_CHUNK % N_SEG == 0
    assert ada_w.shape[0] == DEPTH == 2

    mod = _modulation(c, ada_w, ada_b)

    q, k, v, su, sv, g = _even_in_proj(x, mod, positions, ev_w_in[0])
    z1 = _even_mixer(q, k, v, su, sv, g, x, mod, ev_sink[0], ev_sg_ln_g[0], ev_sg_ln_b[0],
                     ev_sg_w[0], ev_sg_b[0], ev_w_out[0])

    xr, g1 = _odd_in_proj(z1, mod, ln_g, ln_b, od_w_in[0])
    y = _rglru(xr.reshape(batch, RNN_HEADS, seq, RNN_HEAD_DIM), od_conv_w[0], od_conv_b[0], od_w_a[0],
               od_b_a[0], od_w_x[0], od_b_x[0], od_lam[0])
    return _odd_out_proj(y.reshape(batch, RNN_HEADS, seg_len, N_SEG, RNN_HEAD_DIM), g1, z1, mod,
                         od_w_out[0], ln_g, ln_b)
```

```python
import math

import numpy as np
import jax
import jax.numpy as jnp
from jax import lax
from jax.experimental import pallas as pl
from jax.experimental.pallas import tpu as pltpu

D_MODEL = 1024
DEPTH = 2
HEAD_DIM = 64
N_Q_HEADS = 8
N_KV_HEADS = 2
Q_PER_KV = N_Q_HEADS // N_KV_HEADS
ATTN_WIDTH = N_Q_HEADS * HEAD_DIM
KV_WIDTH = N_KV_HEADS * HEAD_DIM
ATTN_BLOCK = 128
ROPE_DIM = HEAD_DIM // 4
ROPE_HALF = ROPE_DIM // 2
ROPE_THETA = 500000.0
N_SG_GROUPS = 8
SG_GROUP_DIM = 64
SG_WIDTH = N_SG_GROUPS * SG_GROUP_DIM
SG_CHUNK = 128
EVEN_GATE_WIDTH = ATTN_WIDTH + SG_WIDTH
EVEN_IN_WIDTH = ATTN_WIDTH + 2 * KV_WIDTH + 2 * SG_WIDTH + EVEN_GATE_WIDTH
RNN_WIDTH = D_MODEL
RNN_HEADS = 8
RNN_HEAD_DIM = RNN_WIDTH // RNN_HEADS
CONV_WIDTH = 4
CONV_LEFT = 2
RG_LRU_C = 8.0
DEEPNORM_ALPHA = (2 * DEPTH) ** 0.25
LN_EPS = 1e-5
NEG_INF = -1e30
LOG2_E = math.log2(math.e)
F32_TINY = float(np.finfo(np.float32).tiny)

LANES = 128
SUBLANES = 8
VMEM_LIMIT = 56 * 1024 * 1024

ROW_TILE = 512
MIX_TILE = 512
N_SEG = 32
SEGS_PER_TILE = SUBLANES
SCAN_CHUNK = 256
WEIGHT_CAST_ROWS = 128
BIAS_ROWS = 2

BF16 = jnp.bfloat16
F32 = jnp.float32


def _sigmoid(x):
    return 0.5 * jnp.tanh(0.5 * x) + 0.5


def _layer_norm_rows(z, g, b, eps=LN_EPS):
    mu = jnp.mean(z, axis=-1, keepdims=True)
    d = z - mu
    var = jnp.mean(d * d, axis=-1, keepdims=True)
    return d * lax.rsqrt(var + eps) * g + b


def _deepnorm_rows(x, gate, y, g, b):
    return _layer_norm_rows(x + (gate * (1.0 / DEEPNORM_ALPHA)) * y, g, b, LN_EPS / DEEPNORM_ALPHA ** 2)


def _params(*semantics):
    return pltpu.CompilerParams(dimension_semantics=semantics, vmem_limit_bytes=VMEM_LIMIT)


def _cast_weight_once(w_ref, w_bf16):
    @pl.when((pl.program_id(0) == 0) & (pl.program_id(1) == 0))
    def _():
        for r in range(0, w_ref.shape[0], WEIGHT_CAST_ROWS):
            w_bf16[r:r + WEIGHT_CAST_ROWS, :] = w_ref[r:r + WEIGHT_CAST_ROWS, :].astype(BF16)


def _resident(shape, col_block=0):
    return pl.BlockSpec(shape, lambda b, i: (0,) * (len(shape) - 1) + (col_block,), pipeline_mode=pl.Buffered(1))


def _mod_kernel(c_ref, w_ref, b_ref, o_ref):
    c = c_ref[...]
    cond = c * _sigmoid(c)
    pad_rows = o_ref.shape[0] - cond.shape[0]
    if pad_rows:
        cond = jnp.concatenate([cond, jnp.zeros((pad_rows, cond.shape[1]), F32)], axis=0)
    bias = b_ref[pl.ds(pl.program_id(0), 1), :]
    o_ref[...] = jnp.dot(cond.astype(BF16), w_ref[...].astype(BF16), preferred_element_type=F32) + bias


def _modulation(c, ada_w, ada_b):
    batch = c.shape[0]
    rows = -(-batch // SUBLANES) * SUBLANES
    n_col = 3 * D_MODEL // D_MODEL
    return pl.pallas_call(
        _mod_kernel,
        grid=(DEPTH, n_col),
        in_specs=[
            pl.BlockSpec((batch, D_MODEL), lambda l, j: (0, 0)),
            pl.BlockSpec((None, D_MODEL, D_MODEL), lambda l, j: (l, 0, j)),
            pl.BlockSpec((DEPTH, D_MODEL), lambda l, j: (0, j)),
        ],
        out_specs=pl.BlockSpec((None, rows, D_MODEL), lambda l, j: (l, 0, j)),
        out_shape=jax.ShapeDtypeStruct((DEPTH, rows, 3 * D_MODEL), F32),
        compiler_params=_params("parallel", "parallel"),
        name="adaln_mod",
    )(c, ada_w, ada_b)


MOD_SHIFT, MOD_SCALE, MOD_GATE = range(3)
EVEN_LAYER, ODD_LAYER = 0, 1


def _mod_spec(mod, layer, part):
    return pl.BlockSpec((None, mod.shape[1], D_MODEL), lambda b, i: (layer, 0, part))


def _batch_row(ref):
    return ref[pl.ds(pl.program_id(0), 1), :]


def _even_in_kernel(x_ref, shift_ref, scale_ref, pos_ref, freq_ref, sign_ref, w_f32_ref,
                    q_ref, k_ref, v_ref, su_ref, sv_ref, g_ref, w_ref):
    _cast_weight_once(w_f32_ref, w_ref)
    lane = lax.broadcasted_iota(jnp.int32, (1, LANES), 1)
    first_half = (lane % HEAD_DIM) < ROPE_HALF
    left_head = lane < HEAD_DIM
    diagonal = lax.broadcasted_iota(jnp.int32, (LANES, LANES), 0) == lax.broadcasted_iota(jnp.int32, (LANES, LANES), 1)
    batch_ids = lax.broadcasted_iota(jnp.int32, (pos_ref.shape[0], LANES), 0)
    q_scale = HEAD_DIM ** -0.5 * LOG2_E

    half_rows = x_ref.shape[0] // 2
    for r in range(2):
        rows = slice(r * half_rows, (r + 1) * half_rows)
        h = (x_ref[rows, :] * (1.0 + _batch_row(scale_ref)) + _batch_row(shift_ref)).astype(BF16)
        p = jnp.dot(h, w_ref[...], preferred_element_type=F32)

        pos_cols = []
        for j in range(r * half_rows // LANES, (r + 1) * half_rows // LANES):
            pos_all = pos_ref[:, j * LANES:(j + 1) * LANES].astype(F32)
            pos_row = jnp.sum(jnp.where(batch_ids == pl.program_id(0), pos_all, 0.0), axis=0, keepdims=True)
            pos_cols.append(jnp.sum(jnp.where(diagonal, pos_row, 0.0), axis=1, keepdims=True))
        ang = jnp.concatenate(pos_cols, axis=0) * freq_ref[...]
        cs = jnp.cos(ang)
        sn = jnp.sin(ang) * sign_ref[...]

        def rotary(t):
            partner = jnp.where(first_half, pltpu.roll(t, LANES - ROPE_HALF, 1), pltpu.roll(t, ROPE_HALF, 1))
            return t * cs + partner * sn

        for j in range(ATTN_WIDTH // LANES):
            q_ref[rows, j * LANES:(j + 1) * LANES] = (
                rotary(p[:, j * LANES:(j + 1) * LANES]) * q_scale).astype(BF16)
        o = ATTN_WIDTH
        k = rotary(p[:, o:o + KV_WIDTH])
        k_sw = pltpu.roll(k, HEAD_DIM, 1)
        k_ref[rows, 0:LANES] = jnp.where(left_head, k, k_sw).astype(BF16)
        k_ref[rows, LANES:2 * LANES] = jnp.where(left_head, k_sw, k).astype(BF16)
        o += KV_WIDTH
        v = p[:, o:o + KV_WIDTH]
        v_sw = pltpu.roll(v, HEAD_DIM, 1)
        v_ref[rows, 0 * LANES:1 * LANES] = jnp.where(left_head, v, 0.0).astype(BF16)
        v_ref[rows, 1 * LANES:2 * LANES] = jnp.where(left_head, 0.0, v_sw).astype(BF16)
        v_ref[rows, 2 * LANES:3 * LANES] = jnp.where(left_head, v_sw, 0.0).astype(BF16)
        v_ref[rows, 3 * LANES:4 * LANES] = jnp.where(left_head, 0.0, v).astype(BF16)
        o += KV_WIDTH
        su_ref[rows, :] = p[:, o:o + SG_WIDTH].astype(BF16)
        o += SG_WIDTH
        sv_ref[rows, :] = p[:, o:o + SG_WIDTH].astype(BF16)
        o += SG_WIDTH
        g_ref[rows, :] = p[:, o:o + EVEN_GATE_WIDTH].astype(BF16)


def _even_in_proj(x, mod, positions, w_in):
    batch, seq, _ = x.shape
    d = np.arange(LANES) % HEAD_DIM
    inv_freq = np.power(np.float64(ROPE_THETA), -np.arange(ROPE_HALF) / ROPE_HALF)
    freq_lane = jnp.asarray(np.where(d < ROPE_DIM, inv_freq[d % ROPE_HALF], 0.0), dtype=F32).reshape(1, LANES)
    sign_lane = jnp.asarray(np.where(d < ROPE_HALF, -1.0, np.where(d < ROPE_DIM, 1.0, 0.0)),
                            dtype=F32).reshape(1, LANES)
    row = lambda width: pl.BlockSpec((None, ROW_TILE, width), lambda b, i: (b, i, 0))
    const = lambda shape: pl.BlockSpec(shape, lambda b, i: (0,) * len(shape))
    widths = (ATTN_WIDTH, 2 * KV_WIDTH, 4 * KV_WIDTH, SG_WIDTH, SG_WIDTH, EVEN_GATE_WIDTH)
    return pl.pallas_call(
        _even_in_kernel,
        grid=(batch, seq // ROW_TILE),
        in_specs=[row(D_MODEL), _mod_spec(mod, EVEN_LAYER, MOD_SHIFT), _mod_spec(mod, EVEN_LAYER, MOD_SCALE),
                  pl.BlockSpec((batch, ROW_TILE), lambda b, i: (0, i)),
                  const((1, LANES)), const((1, LANES)), _resident((D_MODEL, EVEN_IN_WIDTH))],
        out_specs=[row(w) for w in widths],
        out_shape=[jax.ShapeDtypeStruct((batch, seq, w), BF16) for w in widths],
        scratch_shapes=[pltpu.VMEM((D_MODEL, EVEN_IN_WIDTH), BF16)],
        compiler_params=_params("arbitrary", "arbitrary"),
        name="even_in_proj",
    )(x, mod, mod, positions, freq_lane, sign_lane, w_in)


def _even_mix_kernel(sink_ref, q_ref, k_ref, v_ref, su_ref, sv_ref, g_ref, x_ref, gate_ref,
                     avg_ref, avg2_ref, sgg_ref, sgb_ref, sgw_ref, sgbias_ref, wout_f32_ref, lng_ref, lnb_ref,
                     o_ref, y_scr, wout_ref):
    _cast_weight_once(wout_f32_ref, wout_ref)
    seq = k_ref.shape[0]
    n_blocks = seq // ATTN_BLOCK
    sub_blocks = MIX_TILE // ATTN_BLOCK
    tile = pl.program_id(1)
    lane = lax.broadcasted_iota(jnp.int32, (1, LANES), 1)
    left = lane < HEAD_DIM

    qi = lax.broadcasted_iota(jnp.int32, (ATTN_BLOCK, ATTN_BLOCK), 0)
    kj = lax.broadcasted_iota(jnp.int32, (ATTN_BLOCK, ATTN_BLOCK), 1)
    ones_left = jnp.broadcast_to(jnp.where(left, 1.0, 0.0).astype(BF16), (3 * ATTN_BLOCK, LANES))
    ones_right = jnp.broadcast_to(jnp.where(left, 0.0, 1.0).astype(BF16), (3 * ATTN_BLOCK, LANES))
    avg = avg_ref[...]
    sink_cols = [jnp.concatenate([jnp.full((ATTN_BLOCK, LANES), sink_ref[h] * LOG2_E, F32)
                                  for h in (4 * hk, 4 * hk + 2, 4 * hk + 1, 4 * hk + 3)], axis=0)
                 for hk in range(N_KV_HEADS)]

    def sub_block(j, carry):
        blk = tile * sub_blocks + j
        prev = jnp.maximum(blk - 1, 0)
        nxt = jnp.minimum(blk + 1, n_blocks - 1)
        rows = pl.ds(pl.multiple_of(j * ATTN_BLOCK, ATTN_BLOCK), ATTN_BLOCK)
        bias_prev = jnp.where((kj >= qi) & (blk > 0), 0.0, NEG_INF)
        bias_next = jnp.where((kj <= qi) & (blk < n_blocks - 1), 0.0, NEG_INF)
        bias_prev = jnp.concatenate([bias_prev] * Q_PER_KV, axis=0)
        bias_next = jnp.concatenate([bias_next] * Q_PER_KV, axis=0)

        def band(ref):
            return jnp.concatenate(
                [ref[pl.ds(pl.multiple_of(b * ATTN_BLOCK, ATTN_BLOCK), ATTN_BLOCK), :] for b in (prev, blk, nxt)],
                axis=0)

        k_band = band(k_ref)
        v_band = band(v_ref)
        n_sg = SG_WIDTH // LANES
        sg_cols = [slice(c * LANES, (c + 1) * LANES) for c in range(n_sg)]

        def silu_gate(cols):
            g_c = g_ref[rows, cols].astype(F32)
            return g_c * _sigmoid(g_c)

        scores = []
        for hk in range(N_KV_HEADS):
            q_cols = [q_ref[rows, (2 * hk + c) * LANES:(2 * hk + c + 1) * LANES] for c in range(2)]
            zero = jnp.zeros_like(q_cols[0])
            lhs = jnp.concatenate([jnp.where(left, qc, zero) for qc in q_cols]
                                  + [jnp.where(left, zero, qc) for qc in q_cols], axis=0)
            scores.append(lax.dot_general(lhs, k_band[:, hk * LANES:(hk + 1) * LANES], (((1,), (1,)), ((), ())),
                                          preferred_element_type=F32))
        sv_cols = [sv_ref[rows, cols] for cols in sg_cols]
        means = [jnp.dot(v_c, avg, preferred_element_type=F32) for v_c in sv_cols]

        probs, row_max = [], []
        for hk in range(N_KV_HEADS):
            s = scores[hk]
            s0 = s[:, 0:ATTN_BLOCK] + bias_prev
            s1 = s[:, ATTN_BLOCK:2 * ATTN_BLOCK]
            s2 = s[:, 2 * ATTN_BLOCK:3 * ATTN_BLOCK] + bias_next
            m = jnp.max(jnp.maximum(jnp.maximum(s0, s1), s2), axis=-1, keepdims=True)
            m = jnp.maximum(jnp.broadcast_to(m, sink_cols[hk].shape), sink_cols[hk])
            probs.append(jnp.concatenate([jnp.exp2(t - m).astype(BF16) for t in (s0, s1, s2)], axis=1))
            row_max.append(m)
        devs, sq_parts = [], []
        for v_c, mean in zip(sv_cols, means):
            dev = v_c.astype(F32) - mean
            sq = dev * dev
            sq_hi = sq.astype(BF16)
            sq_lo = (sq - sq_hi.astype(F32)).astype(BF16)
            devs.append(dev)
            sq_parts.append(jnp.concatenate([sq_hi, sq_lo], axis=1))

        pv = []
        for hk in range(N_KV_HEADS):
            p = probs[hk]
            lhs = jnp.concatenate(
                [jnp.concatenate([p[c * ATTN_BLOCK:(c + 1) * ATTN_BLOCK], p[(2 + c) * ATTN_BLOCK:(3 + c) * ATTN_BLOCK]],
                                 axis=1) for c in range(2)], axis=0)
            rhs = jnp.concatenate(
                [jnp.concatenate([v_band[:, (2 * hk) * LANES:(2 * hk + 1) * LANES], ones_left], axis=1),
                 jnp.concatenate([v_band[:, (2 * hk + 1) * LANES:(2 * hk + 2) * LANES], ones_right], axis=1)],
                axis=0)
            pv.append(jnp.dot(lhs, rhs, preferred_element_type=F32))
        variances = [jnp.dot(sq, avg2_ref[...], preferred_element_type=F32) for sq in sq_parts]

        for hk in range(N_KV_HEADS):
            for c in range(2):
                even, odd = slice(c * ATTN_BLOCK, (c + 1) * ATTN_BLOCK), slice((2 + c) * ATTN_BLOCK, (3 + c) * ATTN_BLOCK)
                sink_term = jnp.exp2(jnp.where(left, sink_cols[hk][even] - row_max[hk][even],
                                               sink_cols[hk][odd] - row_max[hk][odd]))
                o = pv[hk][c * ATTN_BLOCK:(c + 1) * ATTN_BLOCK]
                pair = o[:, 0:LANES] * (1.0 / (o[:, LANES:2 * LANES] + sink_term))
                cols = slice((2 * hk + c) * LANES, (2 * hk + c + 1) * LANES)
                y_scr[rows, cols] = (pair * silu_gate(cols)).astype(BF16)

        for c, cols in enumerate(sg_cols):
            vn = devs[c] * lax.rsqrt(variances[c] + LN_EPS) * sgg_ref[:, cols] + sgb_ref[:, cols]
            stacked = jnp.concatenate([jnp.where(left, vn, 0.0), jnp.where(left, 0.0, vn)],
                                      axis=0).astype(BF16)
            mixed = jnp.dot(sgw_ref[c], stacked, preferred_element_type=F32) + sgbias_ref[:, cols]
            y_sg = su_ref[rows, cols].astype(F32) * mixed
            gc = slice(ATTN_WIDTH + c * LANES, ATTN_WIDTH + (c + 1) * LANES)
            y_scr[rows, gc] = (y_sg * silu_gate(gc)).astype(BF16)
        return carry

    lax.fori_loop(0, sub_blocks, sub_block, 0, unroll=True)

    for r in range(2):
        half_rows = slice(r * (MIX_TILE // 2), (r + 1) * (MIX_TILE // 2))
        out = jnp.dot(y_scr[half_rows, :], wout_ref[...], preferred_element_type=F32)
        o_ref[half_rows, :] = _deepnorm_rows(x_ref[half_rows, :], _batch_row(gate_ref), out,
                                             lng_ref[EVEN_LAYER:EVEN_LAYER + 1, :], lnb_ref[EVEN_LAYER:EVEN_LAYER + 1, :])


def _even_mixer(q, k, v, su, sv, g, x, mod, sink, sg_ln_g, sg_ln_b, sg_w, sg_b, w_out, ln_g, ln_b):
    batch, seq, _ = x.shape
    lane = np.arange(LANES)
    avg = jnp.asarray((lane[:, None] // SG_GROUP_DIM == lane[None, :] // SG_GROUP_DIM) / SG_GROUP_DIM, dtype=BF16)
    w_pairs = sg_w.reshape(N_SG_GROUPS // 2, 2, SG_CHUNK, SG_CHUNK).transpose(0, 2, 1, 3)
    w_pairs = w_pairs.reshape(N_SG_GROUPS // 2, SG_CHUNK, 2 * SG_CHUNK).astype(BF16)
    bias_full = jnp.repeat(sg_b.T, SG_GROUP_DIM, axis=1)
    row = lambda width: pl.BlockSpec((None, MIX_TILE, width), lambda b, i: (b, i, 0))
    full_seq = lambda arr: pl.BlockSpec((None, seq, arr.shape[-1]), lambda b, i: (b, 0, 0))
    const = lambda shape: pl.BlockSpec(shape, lambda b, i: (0,) * len(shape))
    return pl.pallas_call(
        _even_mix_kernel,
        grid=(batch, seq // MIX_TILE),
        in_specs=[
            pl.BlockSpec(memory_space=pltpu.SMEM),
            row(ATTN_WIDTH), full_seq(k), full_seq(v), row(SG_WIDTH), row(SG_WIDTH), row(EVEN_GATE_WIDTH),
            row(D_MODEL), _mod_spec(mod, EVEN_LAYER, MOD_GATE),
            const((LANES, LANES)), const((2 * LANES, LANES)), const((1, SG_WIDTH)), const((1, SG_WIDTH)),
            const((N_SG_GROUPS // 2, SG_CHUNK, 2 * SG_CHUNK)), const((SG_CHUNK, SG_WIDTH)),
            _resident((EVEN_GATE_WIDTH, D_MODEL)), const(ln_g.shape), const(ln_b.shape),
        ],
        out_specs=row(D_MODEL),
        out_shape=jax.ShapeDtypeStruct((batch, seq, D_MODEL), F32),
        scratch_shapes=[pltpu.VMEM((MIX_TILE, EVEN_GATE_WIDTH), BF16),
                        pltpu.VMEM((EVEN_GATE_WIDTH, D_MODEL), BF16)],
        compiler_params=_params("arbitrary", "arbitrary"),
        name="even_mixer",
    )(sink, q, k, v, su, sv, g, x, mod, avg, jnp.concatenate([avg, avg], axis=0), sg_ln_g.reshape(1, -1), sg_ln_b.reshape(1, -1),
      w_pairs, bias_full, w_out, ln_g, ln_b)


def _odd_in_kernel(x_ref, shift_ref, scale_ref, w_f32_ref, xr_ref, w_ref):
    _cast_weight_once(w_f32_ref, w_ref)
    seg_len = xr_ref.shape[1]
    head_rows = seg_len * SEGS_PER_TILE
    xr_rows = xr_ref.reshape(RNN_HEADS * head_rows, LANES)
    halves = 2
    for r in range(halves):
        first = r * (SEGS_PER_TILE // halves)
        rows = slice(first * seg_len, (first + SEGS_PER_TILE // halves) * seg_len)
        h = (x_ref[rows, :] * (1.0 + _batch_row(scale_ref)) + _batch_row(shift_ref)).astype(BF16)
        p = jnp.dot(h, w_ref[...], preferred_element_type=F32)
        for s in range(SEGS_PER_TILE // halves):
            for hd in range(RNN_HEADS):
                xr_rows[pl.ds(hd * head_rows + first + s, seg_len, stride=SEGS_PER_TILE), :] = (
                    p[s * seg_len:(s + 1) * seg_len, hd * LANES:(hd + 1) * LANES])


def _odd_in_proj(x, mod, w_in):
    batch, seq, _ = x.shape
    seg_len = seq // N_SEG
    tile_rows = SEGS_PER_TILE * seg_len
    out_spec = pl.BlockSpec((None, RNN_HEADS, seg_len, SEGS_PER_TILE, RNN_HEAD_DIM), lambda b, i: (b, 0, 0, i, 0))
    out_shape = jax.ShapeDtypeStruct((batch, RNN_HEADS, seg_len, N_SEG, RNN_HEAD_DIM), F32)
    return pl.pallas_call(
        _odd_in_kernel,
        grid=(batch, seq // tile_rows),
        in_specs=[pl.BlockSpec((None, tile_rows, D_MODEL), lambda b, i: (b, i, 0)),
                  _mod_spec(mod, ODD_LAYER, MOD_SHIFT), _mod_spec(mod, ODD_LAYER, MOD_SCALE),
                  _resident((D_MODEL, RNN_WIDTH), col_block=0)],
        out_specs=out_spec,
        out_shape=out_shape,
        scratch_shapes=[pltpu.VMEM((D_MODEL, RNN_WIDTH), BF16)],
        compiler_params=_params("arbitrary", "arbitrary"),
        name="odd_in_proj",
    )(x, mod, mod, w_in)


def _rglru_kernel(x_ref, convw_ref, convb_ref, waug_ref, lam_ref, o_ref,
                  halo, a_f, b_f, a_b, b_b, edge):
    n_rows = x_ref.shape[0]
    seg_len = n_rows // N_SEG
    n_chunks = n_rows // SCAN_CHUNK
    left_rows = CONV_LEFT * N_SEG
    right_rows = (CONV_WIDTH - 1 - CONV_LEFT) * N_SEG
    seg = lax.broadcasted_iota(jnp.int32, (N_SEG, LANES), 0)

    def from_prev_segment(blk):
        return jnp.where(seg == 0, 0.0, pltpu.roll(blk, 1, 0))

    def from_next_segment(blk):
        return jnp.where(seg == N_SEG - 1, 0.0, pltpu.roll(blk, N_SEG - 1, 0))

    for t in range(CONV_LEFT):
        halo[0, pl.ds(t * N_SEG, N_SEG), :] = from_prev_segment(
            x_ref[pl.ds((seg_len - CONV_LEFT + t) * N_SEG, N_SEG), :])
    halo[0, pl.ds(left_rows, SCAN_CHUNK + right_rows), :] = x_ref[pl.ds(0, SCAN_CHUNK + right_rows), :]
    halo[1, pl.ds(0, left_rows + SCAN_CHUNK), :] = x_ref[pl.ds(n_rows - SCAN_CHUNK - left_rows, left_rows + SCAN_CHUNK), :]
    for t in range(CONV_WIDTH - 1 - CONV_LEFT):
        halo[1, pl.ds(left_rows + SCAN_CHUNK + t * N_SEG, N_SEG), :] = from_next_segment(
            x_ref[pl.ds(t * N_SEG, N_SEG), :])

    lam = lam_ref[...]
    softplus_neg = jnp.maximum(-lam, 0.0) + jnp.log1p(jnp.exp(-jnp.abs(lam)))
    half_rate = (0.5 * RG_LRU_C) * softplus_neg
    bias_lanes = jnp.broadcast_to(
        jnp.where(lax.broadcasted_iota(jnp.int32, (1, LANES), 1) < BIAS_ROWS, 1.0, 0.0).astype(BF16),
        (SCAN_CHUNK, LANES))

    def gates(base, tap):
        half_x = convb_ref[...] + sum(convw_ref[k:k + 1, :] * tap(k) for k in range(CONV_WIDTH))
        th = jnp.tanh(jnp.dot(jnp.concatenate([half_x.astype(BF16), bias_lanes], axis=1), waug_ref[...],
                              preferred_element_type=F32))
        for d, (a_ref, b_ref) in enumerate(((a_f, b_f), (a_b, b_b))):
            o = 2 * d * LANES
            neg_log_a = (th[:, o:o + LANES] + 1.0) * half_rate[d:d + 1, :]
            a = jnp.exp2(neg_log_a * (-LOG2_E))
            one_minus_a2 = jnp.tanh(neg_log_a) * (a * a + 1.0)
            root = one_minus_a2 * lax.rsqrt(jnp.maximum(one_minus_a2, F32_TINY))
            a_ref[pl.ds(base, SCAN_CHUNK), :] = a
            b_ref[pl.ds(base, SCAN_CHUNK), :] = root * ((th[:, o + LANES:o + 2 * LANES] + 1.0) * half_x)

    def inner_chunk(c, carry):
        base = pl.multiple_of(c * SCAN_CHUNK, SCAN_CHUNK)
        gates(base, lambda k: x_ref[pl.ds(base + (k - CONV_LEFT) * N_SEG, SCAN_CHUNK), :])
        return carry

    gates(0, lambda k: halo[0, pl.ds(k * N_SEG, SCAN_CHUNK), :])
    gates(n_rows - SCAN_CHUNK, lambda k: halo[1, pl.ds(k * N_SEG, SCAN_CHUNK), :])
    lax.fori_loop(1, n_chunks - 1, inner_chunk, 0, unroll=7)

    def scan_step(t, carry):
        h_f, p_f, h_b, p_b = carry
        rf = pl.ds(pl.multiple_of(t * N_SEG, N_SEG), N_SEG)
        rb = pl.ds(pl.multiple_of((seg_len - 1 - t) * N_SEG, N_SEG), N_SEG)
        a = a_f[rf, :]
        h_f = a * h_f + b_f[rf, :]
        p_f = a * p_f
        b_f[rf, :] = h_f
        a_f[rf, :] = p_f
        a = a_b[rb, :]
        h_b = a * h_b + b_b[rb, :]
        p_b = a * p_b
        b_b[rb, :] = h_b
        a_b[rb, :] = p_b
        return h_f, p_f, h_b, p_b

    zeros = jnp.zeros((N_SEG, LANES), F32)
    ones = jnp.ones((N_SEG, LANES), F32)
    h_f, p_f, h_b, p_b = lax.fori_loop(0, seg_len, scan_step, (zeros, ones, zeros, ones), unroll=4)

    edge[0] = h_f
    edge[1] = p_f
    edge[2] = h_b
    edge[3] = p_b
    enter_f = enter_b = jnp.zeros((1, LANES), F32)
    edge[4, 0:1, :] = enter_f
    edge[5, N_SEG - 1:N_SEG, :] = enter_b
    for s in range(1, N_SEG):
        enter_f = edge[0, s - 1:s, :] + edge[1, s - 1:s, :] * enter_f
        edge[4, s:s + 1, :] = enter_f
        r = N_SEG - 1 - s
        enter_b = edge[2, r + 1:r + 2, :] + edge[3, r + 1:r + 2, :] * enter_b
        edge[5, r:r + 1, :] = enter_b
    blocks = SCAN_CHUNK // N_SEG
    in_f = jnp.concatenate([edge[4]] * blocks, axis=0)
    in_b = jnp.concatenate([edge[5]] * blocks, axis=0)

    def out_chunk(c, carry):
        rows = pl.ds(pl.multiple_of(c * SCAN_CHUNK, SCAN_CHUNK), SCAN_CHUNK)
        o_ref[rows, :] = (b_f[rows, :] + a_f[rows, :] * in_f) + (b_b[rows, :] + a_b[rows, :] * in_b)
        return carry

    lax.fori_loop(0, n_rows // SCAN_CHUNK, out_chunk, 0, unroll=2)


def _rglru(xr, conv_w, conv_b, w_a, b_a, w_x, b_x, lam):
    batch, _, n_rows, _ = xr.shape
    w_cat = jnp.concatenate([w_a[0], w_x[0], w_a[1], w_x[1]], axis=-1).astype(BF16)
    per_head = lambda v: v.reshape(RNN_HEADS, 1, RNN_HEAD_DIM)
    half_b = 0.5 * jnp.concatenate([per_head(b_a[0]), per_head(b_x[0]), per_head(b_a[1]), per_head(b_x[1])],
                                   axis=-1)
    b_hi = half_b.astype(BF16)
    b_lo = (half_b - b_hi.astype(F32)).astype(BF16)
    w_aug = jnp.concatenate([w_cat, b_hi, b_lo,
                             jnp.zeros((RNN_HEADS, RNN_HEAD_DIM - BIAS_ROWS, 4 * RNN_HEAD_DIM), BF16)], axis=1)
    col = lambda rows: pl.BlockSpec((rows, RNN_HEAD_DIM), lambda b, h: (0, h))
    seq_col = pl.BlockSpec((None, None, n_rows, RNN_HEAD_DIM), lambda b, h: (b, h, 0, 0))
    halo_rows = SCAN_CHUNK + (CONV_WIDTH - 1) * N_SEG
    return pl.pallas_call(
        _rglru_kernel,
        grid=(batch, RNN_HEADS),
        in_specs=[seq_col, col(CONV_WIDTH), col(1),
                  pl.BlockSpec((None, 2 * RNN_HEAD_DIM, 4 * RNN_HEAD_DIM), lambda b, h: (h, 0, 0)),
                  col(2)],
        out_specs=seq_col,
        out_shape=jax.ShapeDtypeStruct((batch, RNN_HEADS, n_rows, RNN_HEAD_DIM), F32),
        scratch_shapes=[pltpu.VMEM((2, halo_rows, LANES), F32)]
                       + [pltpu.VMEM((n_rows, LANES), F32)] * 4
                       + [pltpu.VMEM((6, N_SEG, LANES), F32)],
        compiler_params=_params("parallel", "parallel"),
        name="rglru",
    )(xr, 0.5 * conv_w, 0.5 * conv_b.reshape(1, -1), w_aug, lam)


def _odd_out_kernel(y_ref, x_ref, shift_ref, scale_ref, gate_ref, wg_f32_ref, w_f32_ref, lng_ref, lnb_ref,
                    o_ref, wg_ref, w_ref):
    _cast_weight_once(wg_f32_ref, wg_ref)
    _cast_weight_once(w_f32_ref, w_ref)
    seg_len = y_ref.shape[1]
    head_rows = seg_len * SEGS_PER_TILE
    y_rows = y_ref.reshape(RNN_HEADS * head_rows, LANES)
    halves = 2
    for r in range(halves):
        first = r * (SEGS_PER_TILE // halves)
        rows = slice(first * seg_len, (first + SEGS_PER_TILE // halves) * seg_len)
        x = x_ref[rows, :]
        h = (x * (1.0 + _batch_row(scale_ref)) + _batch_row(shift_ref)).astype(BF16)
        g_all = jnp.dot(h, wg_ref[...], preferred_element_type=F32)
        gated = []
        for s in range(SEGS_PER_TILE // halves):
            y = jnp.concatenate([y_rows[pl.ds(hd * head_rows + first + s, seg_len, stride=SEGS_PER_TILE), :]
                                 for hd in range(RNN_HEADS)], axis=1)
            g = g_all[s * seg_len:(s + 1) * seg_len, :]
            gated.append((y * (g * _sigmoid(g))).astype(BF16))
        out = jnp.dot(jnp.concatenate(gated, axis=0), w_ref[...], preferred_element_type=F32)
        o_ref[rows, :] = _deepnorm_rows(x, _batch_row(gate_ref), out,
                                        lng_ref[ODD_LAYER:ODD_LAYER + 1, :], lnb_ref[ODD_LAYER:ODD_LAYER + 1, :])


def _odd_out_proj(y, x, mod, w_in, w_out, ln_g, ln_b):
    batch, seq, _ = x.shape
    seg_len = seq // N_SEG
    tile_rows = SEGS_PER_TILE * seg_len
    const = lambda shape: pl.BlockSpec(shape, lambda b, i: (0,) * len(shape))
    return pl.pallas_call(
        _odd_out_kernel,
        grid=(batch, seq // tile_rows),
        in_specs=[pl.BlockSpec((None, RNN_HEADS, seg_len, SEGS_PER_TILE, RNN_HEAD_DIM),
                               lambda b, i: (b, 0, 0, i, 0)),
                  pl.BlockSpec((None, tile_rows, D_MODEL), lambda b, i: (b, i, 0)),
                  _mod_spec(mod, ODD_LAYER, MOD_SHIFT), _mod_spec(mod, ODD_LAYER, MOD_SCALE),
                  _mod_spec(mod, ODD_LAYER, MOD_GATE),
                  _resident((D_MODEL, RNN_WIDTH), col_block=1), _resident((RNN_WIDTH, D_MODEL)),
                  const(ln_g.shape), const(ln_b.shape)],
        out_specs=pl.BlockSpec((None, tile_rows, D_MODEL), lambda b, i: (b, i, 0)),
        out_shape=jax.ShapeDtypeStruct((batch, seq, D_MODEL), F32),
        scratch_shapes=[pltpu.VMEM((D_MODEL, RNN_WIDTH), BF16), pltpu.VMEM((RNN_WIDTH, D_MODEL), BF16)],
        compiler_params=_params("arbitrary", "arbitrary"),
        name="odd_out_proj",
    )(y, x, mod, mod, mod, w_in, w_out, ln_g, ln_b)


def kernel(x, c, positions, ada_w, ada_b, ln_g, ln_b, ev_w_in, ev_w_out, ev_sink, ev_sg_ln_g, ev_sg_ln_b,
           ev_sg_w, ev_sg_b, od_w_in, od_conv_w, od_conv_b, od_w_a, od_b_a, od_w_x, od_b_x, od_lam, od_w_out):
    batch, seq, d_model = x.shape
    seg_len = seq // N_SEG
    assert d_model == D_MODEL and seq % ROW_TILE == 0 and seq % MIX_TILE == 0 and seq == N_SEG * seg_len
    assert N_SEG % SEGS_PER_TILE == 0 and seq % SCAN_CHUNK == 0 and SCAN_CHUNK % N_SEG == 0
    assert ada_w.shape[0] == DEPTH == 2

    mod = _modulation(c, ada_w, ada_b)

    q, k, v, su, sv, g = _even_in_proj(x, mod, positions, ev_w_in[0])
    x1 = _even_mixer(q, k, v, su, sv, g, x, mod, ev_sink[0], ev_sg_ln_g[0], ev_sg_ln_b[0],
                     ev_sg_w[0], ev_sg_b[0], ev_w_out[0], ln_g, ln_b)

    xr = _odd_in_proj(x1, mod, od_w_in[0])
    y = _rglru(xr.reshape(batch, RNN_HEADS, seq, RNN_HEAD_DIM), od_conv_w[0], od_conv_b[0], od_w_a[0],
               od_b_a[0], od_w_x[0], od_b_x[0], od_lam[0])
    return _odd_out_proj(y.reshape(batch, RNN_HEADS, seg_len, N_SEG, RNN_HEAD_DIM), x1, mod, od_w_in[0],
                         od_w_out[0], ln_g, ln_b)
```

```python
import math

import numpy as np
import jax
import jax.numpy as jnp
from jax import lax
from jax.experimental import pallas as pl
from jax.experimental.pallas import tpu as pltpu

D_MODEL = 1024
DEPTH = 2
HEAD_DIM = 64
N_Q_HEADS = 8
N_KV_HEADS = 2
Q_PER_KV = N_Q_HEADS // N_KV_HEADS
ATTN_WIDTH = N_Q_HEADS * HEAD_DIM
KV_WIDTH = N_KV_HEADS * HEAD_DIM
ATTN_BLOCK = 128
ROPE_DIM = HEAD_DIM // 4
ROPE_HALF = ROPE_DIM // 2
ROPE_THETA = 500000.0
N_SG_GROUPS = 8
SG_GROUP_DIM = 64
SG_WIDTH = N_SG_GROUPS * SG_GROUP_DIM
SG_CHUNK = 128
EVEN_GATE_WIDTH = ATTN_WIDTH + SG_WIDTH
EVEN_IN_WIDTH = ATTN_WIDTH + 2 * KV_WIDTH + 2 * SG_WIDTH + EVEN_GATE_WIDTH
RNN_WIDTH = D_MODEL
RNN_HEADS = 8
RNN_HEAD_DIM = RNN_WIDTH // RNN_HEADS
CONV_WIDTH = 4
CONV_LEFT = 2
RG_LRU_C = 8.0
DEEPNORM_ALPHA = (2 * DEPTH) ** 0.25
LN_EPS = 1e-5
NEG_INF = -1e30
LOG2_E = math.log2(math.e)
F32_TINY = float(np.finfo(np.float32).tiny)

LANES = 128
SUBLANES = 8
VMEM_LIMIT = 56 * 1024 * 1024

ROW_TILE = 512
MIX_TILE = 512
N_SEG = 32
SEGS_PER_TILE = SUBLANES
IN_SEGS_PER_TILE = 2 * SUBLANES
SCAN_CHUNK = 256
WEIGHT_CAST_ROWS = 128
BIAS_ROWS = 2

BF16 = jnp.bfloat16
F32 = jnp.float32


def _sigmoid(x):
    return 0.5 * jnp.tanh(0.5 * x) + 0.5


def _layer_norm_rows(z, g, b, eps=LN_EPS):
    mu = jnp.mean(z, axis=-1, keepdims=True)
    d = z - mu
    var = jnp.mean(d * d, axis=-1, keepdims=True)
    return d * lax.rsqrt(var + eps) * g + b


def _deepnorm_rows(x, gate, y, g, b):
    return _layer_norm_rows(x + (gate * (1.0 / DEEPNORM_ALPHA)) * y, g, b, LN_EPS / DEEPNORM_ALPHA ** 2)


def _params(*semantics):
    return pltpu.CompilerParams(dimension_semantics=semantics, vmem_limit_bytes=VMEM_LIMIT)


def _cast_weight_once(w_ref, w_bf16):
    @pl.when((pl.program_id(0) == 0) & (pl.program_id(1) == 0))
    def _():
        for r in range(0, w_ref.shape[0], WEIGHT_CAST_ROWS):
            w_bf16[r:r + WEIGHT_CAST_ROWS, :] = w_ref[r:r + WEIGHT_CAST_ROWS, :].astype(BF16)


def _resident(shape, col_block=0):
    return pl.BlockSpec(shape, lambda b, i: (0,) * (len(shape) - 1) + (col_block,), pipeline_mode=pl.Buffered(1))


def _mod_kernel(c_ref, w_ref, b_ref, o_ref):
    c = c_ref[...]
    cond = c * _sigmoid(c)
    pad_rows = o_ref.shape[0] - cond.shape[0]
    if pad_rows:
        cond = jnp.concatenate([cond, jnp.zeros((pad_rows, cond.shape[1]), F32)], axis=0)
    bias = b_ref[pl.ds(pl.program_id(0), 1), :]
    o_ref[...] = jnp.dot(cond.astype(BF16), w_ref[...].astype(BF16), preferred_element_type=F32) + bias


def _modulation(c, ada_w, ada_b):
    batch = c.shape[0]
    rows = -(-batch // SUBLANES) * SUBLANES
    n_col = 3 * D_MODEL // D_MODEL
    return pl.pallas_call(
        _mod_kernel,
        grid=(DEPTH, n_col),
        in_specs=[
            pl.BlockSpec((batch, D_MODEL), lambda l, j: (0, 0)),
            pl.BlockSpec((None, D_MODEL, D_MODEL), lambda l, j: (l, 0, j)),
            pl.BlockSpec((DEPTH, D_MODEL), lambda l, j: (0, j)),
        ],
        out_specs=pl.BlockSpec((None, rows, D_MODEL), lambda l, j: (l, 0, j)),
        out_shape=jax.ShapeDtypeStruct((DEPTH, rows, 3 * D_MODEL), F32),
        compiler_params=_params("parallel", "parallel"),
        name="adaln_mod",
    )(c, ada_w, ada_b)


MOD_SHIFT, MOD_SCALE, MOD_GATE = range(3)
EVEN_LAYER, ODD_LAYER = 0, 1


def _mod_spec(mod, layer, part):
    return pl.BlockSpec((None, mod.shape[1], D_MODEL), lambda b, i: (layer, 0, part))


def _batch_row(ref):
    return ref[pl.ds(pl.program_id(0), 1), :]


def _even_in_kernel(x_ref, shift_ref, scale_ref, pos_ref, freq_ref, sign_ref, w_f32_ref,
                    q_ref, k_ref, v_ref, su_ref, sv_ref, g_ref, w_ref):
    _cast_weight_once(w_f32_ref, w_ref)
    lane = lax.broadcasted_iota(jnp.int32, (1, LANES), 1)
    first_half = (lane % HEAD_DIM) < ROPE_HALF
    left_head = lane < HEAD_DIM
    diagonal = lax.broadcasted_iota(jnp.int32, (LANES, LANES), 0) == lax.broadcasted_iota(jnp.int32, (LANES, LANES), 1)
    batch_ids = lax.broadcasted_iota(jnp.int32, (pos_ref.shape[0], LANES), 0)
    q_scale = HEAD_DIM ** -0.5 * LOG2_E

    half_rows = x_ref.shape[0] // 2
    for r in range(2):
        rows = slice(r * half_rows, (r + 1) * half_rows)
        h = (x_ref[rows, :] * (1.0 + _batch_row(scale_ref)) + _batch_row(shift_ref)).astype(BF16)
        p = jnp.dot(h, w_ref[...], preferred_element_type=F32)

        pos_cols = []
        for j in range(r * half_rows // LANES, (r + 1) * half_rows // LANES):
            pos_all = pos_ref[:, j * LANES:(j + 1) * LANES].astype(F32)
            pos_row = jnp.sum(jnp.where(batch_ids == pl.program_id(0), pos_all, 0.0), axis=0, keepdims=True)
            pos_cols.append(jnp.sum(jnp.where(diagonal, pos_row, 0.0), axis=1, keepdims=True))
        ang = jnp.concatenate(pos_cols, axis=0) * freq_ref[...]
        cs = jnp.cos(ang)
        sn = jnp.sin(ang) * sign_ref[...]

        def rotary(t):
            partner = jnp.where(first_half, pltpu.roll(t, LANES - ROPE_HALF, 1), pltpu.roll(t, ROPE_HALF, 1))
            return t * cs + partner * sn

        for j in range(ATTN_WIDTH // LANES):
            q_ref[rows, j * LANES:(j + 1) * LANES] = (
                rotary(p[:, j * LANES:(j + 1) * LANES]) * q_scale).astype(BF16)
        o = ATTN_WIDTH
        k = rotary(p[:, o:o + KV_WIDTH])
        k_sw = pltpu.roll(k, HEAD_DIM, 1)
        k_ref[rows, 0:LANES] = jnp.where(left_head, k, k_sw).astype(BF16)
        k_ref[rows, LANES:2 * LANES] = jnp.where(left_head, k_sw, k).astype(BF16)
        o += KV_WIDTH
        v = p[:, o:o + KV_WIDTH]
        v_sw = pltpu.roll(v, HEAD_DIM, 1)
        v_ref[rows, 0 * LANES:1 * LANES] = jnp.where(left_head, v, 0.0).astype(BF16)
        v_ref[rows, 1 * LANES:2 * LANES] = jnp.where(left_head, 0.0, v_sw).astype(BF16)
        v_ref[rows, 2 * LANES:3 * LANES] = jnp.where(left_head, v_sw, 0.0).astype(BF16)
        v_ref[rows, 3 * LANES:4 * LANES] = jnp.where(left_head, 0.0, v).astype(BF16)
        o += KV_WIDTH
        su_ref[rows, :] = p[:, o:o + SG_WIDTH].astype(BF16)
        o += SG_WIDTH
        sv_ref[rows, :] = p[:, o:o + SG_WIDTH].astype(BF16)
        o += SG_WIDTH
        g_ref[rows, :] = p[:, o:o + EVEN_GATE_WIDTH].astype(BF16)


def _even_in_proj(x, mod, positions, w_in):
    batch, seq, _ = x.shape
    d = np.arange(LANES) % HEAD_DIM
    inv_freq = np.power(np.float64(ROPE_THETA), -np.arange(ROPE_HALF) / ROPE_HALF)
    freq_lane = jnp.asarray(np.where(d < ROPE_DIM, inv_freq[d % ROPE_HALF], 0.0), dtype=F32).reshape(1, LANES)
    sign_lane = jnp.asarray(np.where(d < ROPE_HALF, -1.0, np.where(d < ROPE_DIM, 1.0, 0.0)),
                            dtype=F32).reshape(1, LANES)
    row = lambda width: pl.BlockSpec((None, ROW_TILE, width), lambda b, i: (b, i, 0))
    const = lambda shape: pl.BlockSpec(shape, lambda b, i: (0,) * len(shape))
    widths = (ATTN_WIDTH, 2 * KV_WIDTH, 4 * KV_WIDTH, SG_WIDTH, SG_WIDTH, EVEN_GATE_WIDTH)
    return pl.pallas_call(
        _even_in_kernel,
        grid=(batch, seq // ROW_TILE),
        in_specs=[row(D_MODEL), _mod_spec(mod, EVEN_LAYER, MOD_SHIFT), _mod_spec(mod, EVEN_LAYER, MOD_SCALE),
                  pl.BlockSpec((batch, ROW_TILE), lambda b, i: (0, i)),
                  const((1, LANES)), const((1, LANES)), _resident((D_MODEL, EVEN_IN_WIDTH))],
        out_specs=[row(w) for w in widths],
        out_shape=[jax.ShapeDtypeStruct((batch, seq, w), BF16) for w in widths],
        scratch_shapes=[pltpu.VMEM((D_MODEL, EVEN_IN_WIDTH), BF16)],
        compiler_params=_params("arbitrary", "arbitrary"),
        name="even_in_proj",
    )(x, mod, mod, positions, freq_lane, sign_lane, w_in)


def _even_mix_kernel(sink_ref, q_ref, k_ref, v_ref, su_ref, sv_ref, g_ref, x_ref, gate_ref,
                     avg_ref, avg2_ref, sgg_ref, sgb_ref, sgw_ref, sgbias_ref, wout_f32_ref, lng_ref, lnb_ref,
                     o_ref, y_scr, wout_ref):
    _cast_weight_once(wout_f32_ref, wout_ref)
    seq = k_ref.shape[0]
    n_blocks = seq // ATTN_BLOCK
    sub_blocks = MIX_TILE // ATTN_BLOCK
    tile = pl.program_id(1)
    lane = lax.broadcasted_iota(jnp.int32, (1, LANES), 1)
    left = lane < HEAD_DIM

    qi = lax.broadcasted_iota(jnp.int32, (ATTN_BLOCK, ATTN_BLOCK), 0)
    kj = lax.broadcasted_iota(jnp.int32, (ATTN_BLOCK, ATTN_BLOCK), 1)
    ones_left = jnp.broadcast_to(jnp.where(left, 1.0, 0.0).astype(BF16), (3 * ATTN_BLOCK, LANES))
    ones_right = jnp.broadcast_to(jnp.where(left, 0.0, 1.0).astype(BF16), (3 * ATTN_BLOCK, LANES))
    avg = avg_ref[...]
    sink_cols = [jnp.concatenate([jnp.full((ATTN_BLOCK, LANES), sink_ref[h] * LOG2_E, F32)
                                  for h in (4 * hk, 4 * hk + 2, 4 * hk + 1, 4 * hk + 3)], axis=0)
                 for hk in range(N_KV_HEADS)]

    def sub_block(j, carry):
        blk = tile * sub_blocks + j
        prev = jnp.maximum(blk - 1, 0)
        nxt = jnp.minimum(blk + 1, n_blocks - 1)
        rows = pl.ds(pl.multiple_of(j * ATTN_BLOCK, ATTN_BLOCK), ATTN_BLOCK)
        bias_prev = jnp.where((kj >= qi) & (blk > 0), 0.0, NEG_INF)
        bias_next = jnp.where((kj <= qi) & (blk < n_blocks - 1), 0.0, NEG_INF)
        bias_prev = jnp.concatenate([bias_prev] * Q_PER_KV, axis=0)
        bias_next = jnp.concatenate([bias_next] * Q_PER_KV, axis=0)

        def band(ref):
            return jnp.concatenate(
                [ref[pl.ds(pl.multiple_of(b * ATTN_BLOCK, ATTN_BLOCK), ATTN_BLOCK), :] for b in (prev, blk, nxt)],
                axis=0)

        k_band = band(k_ref)
        v_band = band(v_ref)
        n_sg = SG_WIDTH // LANES
        sg_cols = [slice(c * LANES, (c + 1) * LANES) for c in range(n_sg)]

        def silu_gate(cols):
            g_c = g_ref[rows, cols].astype(F32)
            return g_c * _sigmoid(g_c)

        scores = []
        for hk in range(N_KV_HEADS):
            q_cols = [q_ref[rows, (2 * hk + c) * LANES:(2 * hk + c + 1) * LANES] for c in range(2)]
            zero = jnp.zeros_like(q_cols[0])
            lhs = jnp.concatenate([jnp.where(left, qc, zero) for qc in q_cols]
                                  + [jnp.where(left, zero, qc) for qc in q_cols], axis=0)
            scores.append(lax.dot_general(lhs, k_band[:, hk * LANES:(hk + 1) * LANES], (((1,), (1,)), ((), ())),
                                          preferred_element_type=F32))
        sv_cols = [sv_ref[rows, cols] for cols in sg_cols]
        means = [jnp.dot(v_c, avg, preferred_element_type=F32) for v_c in sv_cols]

        probs, row_max = [], []
        for hk in range(N_KV_HEADS):
            s = scores[hk]
            s0 = s[:, 0:ATTN_BLOCK] + bias_prev
            s1 = s[:, ATTN_BLOCK:2 * ATTN_BLOCK]
            s2 = s[:, 2 * ATTN_BLOCK:3 * ATTN_BLOCK] + bias_next
            m = jnp.max(jnp.maximum(jnp.maximum(s0, s1), s2), axis=-1, keepdims=True)
            m = jnp.maximum(jnp.broadcast_to(m, sink_cols[hk].shape), sink_cols[hk])
            probs.append(jnp.concatenate([jnp.exp2(t - m).astype(BF16) for t in (s0, s1, s2)], axis=1))
            row_max.append(m)
        devs, sq_parts = [], []
        for v_c, mean in zip(sv_cols, means):
            dev = v_c.astype(F32) - mean
            sq = dev * dev
            sq_hi = sq.astype(BF16)
            sq_lo = (sq - sq_hi.astype(F32)).astype(BF16)
            devs.append(dev)
            sq_parts.append(jnp.concatenate([sq_hi, sq_lo], axis=1))

        pv = []
        for hk in range(N_KV_HEADS):
            p = probs[hk]
            lhs = jnp.concatenate(
                [jnp.concatenate([p[c * ATTN_BLOCK:(c + 1) * ATTN_BLOCK], p[(2 + c) * ATTN_BLOCK:(3 + c) * ATTN_BLOCK]],
                                 axis=1) for c in range(2)], axis=0)
            rhs = jnp.concatenate(
                [jnp.concatenate([v_band[:, (2 * hk) * LANES:(2 * hk + 1) * LANES], ones_left], axis=1),
                 jnp.concatenate([v_band[:, (2 * hk + 1) * LANES:(2 * hk + 2) * LANES], ones_right], axis=1)],
                axis=0)
            pv.append(jnp.dot(lhs, rhs, preferred_element_type=F32))
        variances = [jnp.dot(sq, avg2_ref[...], preferred_element_type=F32) for sq in sq_parts]

        for hk in range(N_KV_HEADS):
            for c in range(2):
                even, odd = slice(c * ATTN_BLOCK, (c + 1) * ATTN_BLOCK), slice((2 + c) * ATTN_BLOCK, (3 + c) * ATTN_BLOCK)
                sink_term = jnp.exp2(jnp.where(left, sink_cols[hk][even] - row_max[hk][even],
                                               sink_cols[hk][odd] - row_max[hk][odd]))
                o = pv[hk][c * ATTN_BLOCK:(c + 1) * ATTN_BLOCK]
                pair = o[:, 0:LANES] * (1.0 / (o[:, LANES:2 * LANES] + sink_term))
                cols = slice((2 * hk + c) * LANES, (2 * hk + c + 1) * LANES)
                y_scr[rows, cols] = (pair * silu_gate(cols)).astype(BF16)

        for c, cols in enumerate(sg_cols):
            vn = devs[c] * lax.rsqrt(variances[c] + LN_EPS) * sgg_ref[:, cols] + sgb_ref[:, cols]
            stacked = jnp.concatenate([jnp.where(left, vn, 0.0), jnp.where(left, 0.0, vn)],
                                      axis=0).astype(BF16)
            mixed = jnp.dot(sgw_ref[c], stacked, preferred_element_type=F32) + sgbias_ref[:, cols]
            y_sg = su_ref[rows, cols].astype(F32) * mixed
            gc = slice(ATTN_WIDTH + c * LANES, ATTN_WIDTH + (c + 1) * LANES)
            y_scr[rows, gc] = (y_sg * silu_gate(gc)).astype(BF16)
        return carry

    lax.fori_loop(0, sub_blocks, sub_block, 0, unroll=True)

    for r in range(2):
        half_rows = slice(r * (MIX_TILE // 2), (r + 1) * (MIX_TILE // 2))
        out = jnp.dot(y_scr[half_rows, :], wout_ref[...], preferred_element_type=F32)
        o_ref[half_rows, :] = _deepnorm_rows(x_ref[half_rows, :], _batch_row(gate_ref), out,
                                             lng_ref[EVEN_LAYER:EVEN_LAYER + 1, :], lnb_ref[EVEN_LAYER:EVEN_LAYER + 1, :])


def _even_mixer(q, k, v, su, sv, g, x, mod, sink, sg_ln_g, sg_ln_b, sg_w, sg_b, w_out, ln_g, ln_b):
    batch, seq, _ = x.shape
    lane = np.arange(LANES)
    avg = jnp.asarray((lane[:, None] // SG_GROUP_DIM == lane[None, :] // SG_GROUP_DIM) / SG_GROUP_DIM, dtype=BF16)
    w_pairs = sg_w.reshape(N_SG_GROUPS // 2, 2, SG_CHUNK, SG_CHUNK).transpose(0, 2, 1, 3)
    w_pairs = w_pairs.reshape(N_SG_GROUPS // 2, SG_CHUNK, 2 * SG_CHUNK).astype(BF16)
    bias_full = jnp.repeat(sg_b.T, SG_GROUP_DIM, axis=1)
    row = lambda width: pl.BlockSpec((None, MIX_TILE, width), lambda b, i: (b, i, 0))
    full_seq = lambda arr: pl.BlockSpec((None, seq, arr.shape[-1]), lambda b, i: (b, 0, 0))
    const = lambda shape: pl.BlockSpec(shape, lambda b, i: (0,) * len(shape))
    return pl.pallas_call(
        _even_mix_kernel,
        grid=(batch, seq // MIX_TILE),
        in_specs=[
            pl.BlockSpec(memory_space=pltpu.SMEM),
            row(ATTN_WIDTH), full_seq(k), full_seq(v), row(SG_WIDTH), row(SG_WIDTH), row(EVEN_GATE_WIDTH),
            row(D_MODEL), _mod_spec(mod, EVEN_LAYER, MOD_GATE),
            const((LANES, LANES)), const((2 * LANES, LANES)), const((1, SG_WIDTH)), const((1, SG_WIDTH)),
            const((N_SG_GROUPS // 2, SG_CHUNK, 2 * SG_CHUNK)), const((SG_CHUNK, SG_WIDTH)),
            _resident((EVEN_GATE_WIDTH, D_MODEL)), const(ln_g.shape), const(ln_b.shape),
        ],
        out_specs=row(D_MODEL),
        out_shape=jax.ShapeDtypeStruct((batch, seq, D_MODEL), F32),
        scratch_shapes=[pltpu.VMEM((MIX_TILE, EVEN_GATE_WIDTH), BF16),
                        pltpu.VMEM((EVEN_GATE_WIDTH, D_MODEL), BF16)],
        compiler_params=_params("arbitrary", "arbitrary"),
        name="even_mixer",
    )(sink, q, k, v, su, sv, g, x, mod, avg, jnp.concatenate([avg, avg], axis=0), sg_ln_g.reshape(1, -1), sg_ln_b.reshape(1, -1),
      w_pairs, bias_full, w_out, ln_g, ln_b)


def _odd_in_kernel(x_ref, shift_ref, scale_ref, w_f32_ref, xr_ref, w_ref):
    _cast_weight_once(w_f32_ref, w_ref)
    seg_len = xr_ref.shape[1]
    head_rows = seg_len * IN_SEGS_PER_TILE
    xr_rows = xr_ref.reshape(RNN_HEADS * head_rows, LANES)
    halves = 4
    for r in range(halves):
        first = r * (IN_SEGS_PER_TILE // halves)
        rows = slice(first * seg_len, (first + IN_SEGS_PER_TILE // halves) * seg_len)
        h = (x_ref[rows, :] * (1.0 + _batch_row(scale_ref)) + _batch_row(shift_ref)).astype(BF16)
        p = jnp.dot(h, w_ref[...], preferred_element_type=F32)
        for s in range(IN_SEGS_PER_TILE // halves):
            for hd in range(RNN_HEADS):
                xr_rows[pl.ds(hd * head_rows + first + s, seg_len, stride=IN_SEGS_PER_TILE), :] = (
                    p[s * seg_len:(s + 1) * seg_len, hd * LANES:(hd + 1) * LANES])


def _odd_in_proj(x, mod, w_in):
    batch, seq, _ = x.shape
    seg_len = seq // N_SEG
    tile_rows = IN_SEGS_PER_TILE * seg_len
    out_spec = pl.BlockSpec((None, RNN_HEADS, seg_len, IN_SEGS_PER_TILE, RNN_HEAD_DIM), lambda b, i: (b, 0, 0, i, 0))
    out_shape = jax.ShapeDtypeStruct((batch, RNN_HEADS, seg_len, N_SEG, RNN_HEAD_DIM), F32)
    return pl.pallas_call(
        _odd_in_kernel,
        grid=(batch, seq // tile_rows),
        in_specs=[pl.BlockSpec((None, tile_rows, D_MODEL), lambda b, i: (b, i, 0)),
                  _mod_spec(mod, ODD_LAYER, MOD_SHIFT), _mod_spec(mod, ODD_LAYER, MOD_SCALE),
                  _resident((D_MODEL, RNN_WIDTH), col_block=0)],
        out_specs=out_spec,
        out_shape=out_shape,
        scratch_shapes=[pltpu.VMEM((D_MODEL, RNN_WIDTH), BF16)],
        compiler_params=_params("arbitrary", "arbitrary"),
        name="odd_in_proj",
    )(x, mod, mod, w_in)


def _rglru_kernel(x_ref, convw_ref, convb_ref, waug_ref, lam_ref, o_ref,
                  halo, a_f, b_f, a_b, b_b, edge):
    n_rows = x_ref.shape[0]
    seg_len = n_rows // N_SEG
    n_chunks = n_rows // SCAN_CHUNK
    left_rows = CONV_LEFT * N_SEG
    right_rows = (CONV_WIDTH - 1 - CONV_LEFT) * N_SEG
    seg = lax.broadcasted_iota(jnp.int32, (N_SEG, LANES), 0)

    def from_prev_segment(blk):
        return jnp.where(seg == 0, 0.0, pltpu.roll(blk, 1, 0))

    def from_next_segment(blk):
        return jnp.where(seg == N_SEG - 1, 0.0, pltpu.roll(blk, N_SEG - 1, 0))

    for t in range(CONV_LEFT):
        halo[0, pl.ds(t * N_SEG, N_SEG), :] = from_prev_segment(
            x_ref[pl.ds((seg_len - CONV_LEFT + t) * N_SEG, N_SEG), :])
    halo[0, pl.ds(left_rows, SCAN_CHUNK + right_rows), :] = x_ref[pl.ds(0, SCAN_CHUNK + right_rows), :]
    halo[1, pl.ds(0, left_rows + SCAN_CHUNK), :] = x_ref[pl.ds(n_rows - SCAN_CHUNK - left_rows, left_rows + SCAN_CHUNK), :]
    for t in range(CONV_WIDTH - 1 - CONV_LEFT):
        halo[1, pl.ds(left_rows + SCAN_CHUNK + t * N_SEG, N_SEG), :] = from_next_segment(
            x_ref[pl.ds(t * N_SEG, N_SEG), :])

    lam = lam_ref[...]
    softplus_neg = jnp.maximum(-lam, 0.0) + jnp.log1p(jnp.exp(-jnp.abs(lam)))
    half_rate = (0.5 * RG_LRU_C) * softplus_neg
    bias_lanes = jnp.broadcast_to(
        jnp.where(lax.broadcasted_iota(jnp.int32, (1, LANES), 1) < BIAS_ROWS, 1.0, 0.0).astype(BF16),
        (SCAN_CHUNK, LANES))

    def gates(base, tap):
        half_x = convb_ref[...] + sum(convw_ref[k:k + 1, :] * tap(k) for k in range(CONV_WIDTH))
        th = jnp.tanh(jnp.dot(jnp.concatenate([half_x.astype(BF16), bias_lanes], axis=1), waug_ref[...],
                              preferred_element_type=F32))
        for d, (a_ref, b_ref) in enumerate(((a_f, b_f), (a_b, b_b))):
            o = 2 * d * LANES
            neg_log_a = (th[:, o:o + LANES] + 1.0) * half_rate[d:d + 1, :]
            a = jnp.exp2(neg_log_a * (-LOG2_E))
            one_minus_a2 = jnp.tanh(neg_log_a) * (a * a + 1.0)
            root = one_minus_a2 * lax.rsqrt(jnp.maximum(one_minus_a2, F32_TINY))
            a_ref[pl.ds(base, SCAN_CHUNK), :] = a
            b_ref[pl.ds(base, SCAN_CHUNK), :] = root * ((th[:, o + LANES:o + 2 * LANES] + 1.0) * half_x)

    def inner_chunk(c, carry):
        base = pl.multiple_of(c * SCAN_CHUNK, SCAN_CHUNK)
        gates(base, lambda k: x_ref[pl.ds(base + (k - CONV_LEFT) * N_SEG, SCAN_CHUNK), :])
        return carry

    gates(0, lambda k: halo[0, pl.ds(k * N_SEG, SCAN_CHUNK), :])
    gates(n_rows - SCAN_CHUNK, lambda k: halo[1, pl.ds(k * N_SEG, SCAN_CHUNK), :])
    lax.fori_loop(1, n_chunks - 1, inner_chunk, 0, unroll=7)

    def scan_step(t, carry):
        h_f, p_f, h_b, p_b = carry
        rf = pl.ds(pl.multiple_of(t * N_SEG, N_SEG), N_SEG)
        rb = pl.ds(pl.multiple_of((seg_len - 1 - t) * N_SEG, N_SEG), N_SEG)
        a = a_f[rf, :]
        h_f = a * h_f + b_f[rf, :]
        p_f = a * p_f
        b_f[rf, :] = h_f
        a_f[rf, :] = p_f
        a = a_b[rb, :]
        h_b = a * h_b + b_b[rb, :]
        p_b = a * p_b
        b_b[rb, :] = h_b
        a_b[rb, :] = p_b
        return h_f, p_f, h_b, p_b

    zeros = jnp.zeros((N_SEG, LANES), F32)
    ones = jnp.ones((N_SEG, LANES), F32)
    h_f, p_f, h_b, p_b = lax.fori_loop(0, seg_len, scan_step, (zeros, ones, zeros, ones), unroll=4)

    edge[0] = h_f
    edge[1] = p_f
    edge[2] = h_b
    edge[3] = p_b
    enter_f = enter_b = jnp.zeros((1, LANES), F32)
    edge[4, 0:1, :] = enter_f
    edge[5, N_SEG - 1:N_SEG, :] = enter_b
    for s in range(1, N_SEG):
        enter_f = edge[0, s - 1:s, :] + edge[1, s - 1:s, :] * enter_f
        edge[4, s:s + 1, :] = enter_f
        r = N_SEG - 1 - s
        enter_b = edge[2, r + 1:r + 2, :] + edge[3, r + 1:r + 2, :] * enter_b
        edge[5, r:r + 1, :] = enter_b
    blocks = SCAN_CHUNK // N_SEG
    in_f = jnp.concatenate([edge[4]] * blocks, axis=0)
    in_b = jnp.concatenate([edge[5]] * blocks, axis=0)

    def out_chunk(c, carry):
        rows = pl.ds(pl.multiple_of(c * SCAN_CHUNK, SCAN_CHUNK), SCAN_CHUNK)
        o_ref[rows, :] = (b_f[rows, :] + a_f[rows, :] * in_f) + (b_b[rows, :] + a_b[rows, :] * in_b)
        return carry

    lax.fori_loop(0, n_rows // SCAN_CHUNK, out_chunk, 0, unroll=2)


def _rglru(xr, conv_w, conv_b, w_a, b_a, w_x, b_x, lam):
    batch, _, n_rows, _ = xr.shape
    w_cat = jnp.concatenate([w_a[0], w_x[0], w_a[1], w_x[1]], axis=-1).astype(BF16)
    per_head = lambda v: v.reshape(RNN_HEADS, 1, RNN_HEAD_DIM)
    half_b = 0.5 * jnp.concatenate([per_head(b_a[0]), per_head(b_x[0]), per_head(b_a[1]), per_head(b_x[1])],
                                   axis=-1)
    b_hi = half_b.astype(BF16)
    b_lo = (half_b - b_hi.astype(F32)).astype(BF16)
    w_aug = jnp.concatenate([w_cat, b_hi, b_lo,
                             jnp.zeros((RNN_HEADS, RNN_HEAD_DIM - BIAS_ROWS, 4 * RNN_HEAD_DIM), BF16)], axis=1)
    col = lambda rows: pl.BlockSpec((rows, RNN_HEAD_DIM), lambda b, h: (0, h))
    seq_col = pl.BlockSpec((None, None, n_rows, RNN_HEAD_DIM), lambda b, h: (b, h, 0, 0))
    halo_rows = SCAN_CHUNK + (CONV_WIDTH - 1) * N_SEG
    return pl.pallas_call(
        _rglru_kernel,
        grid=(batch, RNN_HEADS),
        in_specs=[seq_col, col(CONV_WIDTH), col(1),
                  pl.BlockSpec((None, 2 * RNN_HEAD_DIM, 4 * RNN_HEAD_DIM), lambda b, h: (h, 0, 0)),
                  col(2)],
        out_specs=seq_col,
        out_shape=jax.ShapeDtypeStruct((batch, RNN_HEADS, n_rows, RNN_HEAD_DIM), F32),
        scratch_shapes=[pltpu.VMEM((2, halo_rows, LANES), F32)]
                       + [pltpu.VMEM((n_rows, LANES), F32)] * 4
                       + [pltpu.VMEM((6, N_SEG, LANES), F32)],
        compiler_params=_params("parallel", "parallel"),
        name="rglru",
    )(xr, 0.5 * conv_w, 0.5 * conv_b.reshape(1, -1), w_aug, lam)


def _odd_out_kernel(y_ref, x_ref, shift_ref, scale_ref, gate_ref, wg_f32_ref, w_f32_ref, lng_ref, lnb_ref,
                    o_ref, wg_ref, w_ref):
    _cast_weight_once(wg_f32_ref, wg_ref)
    _cast_weight_once(w_f32_ref, w_ref)
    seg_len = y_ref.shape[1]
    head_rows = seg_len * SEGS_PER_TILE
    y_rows = y_ref.reshape(RNN_HEADS * head_rows, LANES)
    halves = 4
    for r in range(halves):
        first = r * (SEGS_PER_TILE // halves)
        rows = slice(first * seg_len, (first + SEGS_PER_TILE // halves) * seg_len)
        x = x_ref[rows, :]
        h = (x * (1.0 + _batch_row(scale_ref)) + _batch_row(shift_ref)).astype(BF16)
        g_all = jnp.dot(h, wg_ref[...], preferred_element_type=F32)
        gated = []
        for s in range(SEGS_PER_TILE // halves):
            y = jnp.concatenate([y_rows[pl.ds(hd * head_rows + first + s, seg_len, stride=SEGS_PER_TILE), :]
                                 for hd in range(RNN_HEADS)], axis=1)
            g = g_all[s * seg_len:(s + 1) * seg_len, :]
            gated.append((y * (g * _sigmoid(g))).astype(BF16))
        out = jnp.dot(jnp.concatenate(gated, axis=0), w_ref[...], preferred_element_type=F32)
        o_ref[rows, :] = _deepnorm_rows(x, _batch_row(gate_ref), out,
                                        lng_ref[ODD_LAYER:ODD_LAYER + 1, :], lnb_ref[ODD_LAYER:ODD_LAYER + 1, :])


def _odd_out_proj(y, x, mod, w_in, w_out, ln_g, ln_b):
    batch, seq, _ = x.shape
    seg_len = seq // N_SEG
    tile_rows = SEGS_PER_TILE * seg_len
    const = lambda shape: pl.BlockSpec(shape, lambda b, i: (0,) * len(shape))
    return pl.pallas_call(
        _odd_out_kernel,
        grid=(batch, seq // tile_rows),
        in_specs=[pl.BlockSpec((None, RNN_HEADS, seg_len, SEGS_PER_TILE, RNN_HEAD_DIM),
                               lambda b, i: (b, 0, 0, i, 0)),
                  pl.BlockSpec((None, tile_rows, D_MODEL), lambda b, i: (b, i, 0)),
                  _mod_spec(mod, ODD_LAYER, MOD_SHIFT), _mod_spec(mod, ODD_LAYER, MOD_SCALE),
                  _mod_spec(mod, ODD_LAYER, MOD_GATE),
                  _resident((D_MODEL, RNN_WIDTH), col_block=1), _resident((RNN_WIDTH, D_MODEL)),
                  const(ln_g.shape), const(ln_b.shape)],
        out_specs=pl.BlockSpec((None, tile_rows, D_MODEL), lambda b, i: (b, i, 0)),
        out_shape=jax.ShapeDtypeStruct((batch, seq, D_MODEL), F32),
        scratch_shapes=[pltpu.VMEM((D_MODEL, RNN_WIDTH), BF16), pltpu.VMEM((RNN_WIDTH, D_MODEL), BF16)],
        compiler_params=_params("arbitrary", "arbitrary"),
        name="odd_out_proj",
    )(y, x, mod, mod, mod, w_in, w_out, ln_g, ln_b)


def kernel(x, c, positions, ada_w, ada_b, ln_g, ln_b, ev_w_in, ev_w_out, ev_sink, ev_sg_ln_g, ev_sg_ln_b,
           ev_sg_w, ev_sg_b, od_w_in, od_conv_w, od_conv_b, od_w_a, od_b_a, od_w_x, od_b_x, od_lam, od_w_out):
    batch, seq, d_model = x.shape
    seg_len = seq // N_SEG
    assert d_model == D_MODEL and seq % ROW_TILE == 0 and seq % MIX_TILE == 0 and seq == N_SEG * seg_len
    assert N_SEG % SEGS_PER_TILE == 0 and N_SEG % IN_SEGS_PER_TILE == 0 and seq % SCAN_CHUNK == 0 and SCAN_CHUNK % N_SEG == 0
    assert ada_w.shape[0] == DEPTH == 2

    mod = _modulation(c, ada_w, ada_b)

    q, k, v, su, sv, g = _even_in_proj(x, mod, positions, ev_w_in[0])
    x1 = _even_mixer(q, k, v, su, sv, g, x, mod, ev_sink[0], ev_sg_ln_g[0], ev_sg_ln_b[0],
                     ev_sg_w[0], ev_sg_b[0], ev_w_out[0], ln_g, ln_b)

    xr = _odd_in_proj(x1, mod, od_w_in[0])
    y = _rglru(xr.reshape(batch, RNN_HEADS, seq, RNN_HEAD_DIM), od_conv_w[0], od_conv_b[0], od_w_a[0],
               od_b_a[0], od_w_x[0], od_b_x[0], od_lam[0])
    return _odd_out_proj(y.reshape(batch, RNN_HEADS, seg_len, N_SEG, RNN_HEAD_DIM), x1, mod, od_w_in[0],
                         od_w_out[0], ln_g, ln_b)
```

```python
import math

import numpy as np
import jax
import jax.numpy as jnp
from jax import lax
from jax.experimental import pallas as pl
from jax.experimental.pallas import tpu as pltpu

D_MODEL = 1024
DEPTH = 2
HEAD_DIM = 64
N_Q_HEADS = 8
N_KV_HEADS = 2
Q_PER_KV = N_Q_HEADS // N_KV_HEADS
ATTN_WIDTH = N_Q_HEADS * HEAD_DIM
KV_WIDTH = N_KV_HEADS * HEAD_DIM
ATTN_BLOCK = 128
ROPE_DIM = HEAD_DIM // 4
ROPE_HALF = ROPE_DIM // 2
ROPE_THETA = 500000.0
N_SG_GROUPS = 8
SG_GROUP_DIM = 64
SG_WIDTH = N_SG_GROUPS * SG_GROUP_DIM
SG_CHUNK = 128
EVEN_GATE_WIDTH = ATTN_WIDTH + SG_WIDTH
EVEN_IN_WIDTH = ATTN_WIDTH + 2 * KV_WIDTH + 2 * SG_WIDTH + EVEN_GATE_WIDTH
RNN_WIDTH = D_MODEL
RNN_HEADS = 8
RNN_HEAD_DIM = RNN_WIDTH // RNN_HEADS
CONV_WIDTH = 4
CONV_LEFT = 2
RG_LRU_C = 8.0
DEEPNORM_ALPHA = (2 * DEPTH) ** 0.25
LN_EPS = 1e-5
NEG_INF = -1e30
LOG2_E = math.log2(math.e)
F32_TINY = float(np.finfo(np.float32).tiny)

LANES = 128
SUBLANES = 8
VMEM_LIMIT = 56 * 1024 * 1024

ROW_TILE = 512
MIX_TILE = 512
N_SEG = 32
SEGS_PER_TILE = SUBLANES
SCAN_CHUNK = 256
WEIGHT_CAST_ROWS = 128
BIAS_ROWS = 2

BF16 = jnp.bfloat16
F32 = jnp.float32


def _sigmoid(x):
    return 0.5 * jnp.tanh(0.5 * x) + 0.5


def _layer_norm_rows(z, g, b, eps=LN_EPS):
    mu = jnp.mean(z, axis=-1, keepdims=True)
    d = z - mu
    var = jnp.mean(d * d, axis=-1, keepdims=True)
    return d * lax.rsqrt(var + eps) * g + b


def _deepnorm_rows(x, gate, y, g, b):
    return _layer_norm_rows(x + (gate * (1.0 / DEEPNORM_ALPHA)) * y, g, b, LN_EPS / DEEPNORM_ALPHA ** 2)


def _params(*semantics):
    return pltpu.CompilerParams(dimension_semantics=semantics, vmem_limit_bytes=VMEM_LIMIT)


def _cast_weight_once(w_ref, w_bf16):
    @pl.when((pl.program_id(0) == 0) & (pl.program_id(1) == 0))
    def _():
        for r in range(0, w_ref.shape[0], WEIGHT_CAST_ROWS):
            w_bf16[r:r + WEIGHT_CAST_ROWS, :] = w_ref[r:r + WEIGHT_CAST_ROWS, :].astype(BF16)


def _resident(shape, col_block=0):
    return pl.BlockSpec(shape, lambda b, i: (0,) * (len(shape) - 1) + (col_block,), pipeline_mode=pl.Buffered(1))


def _mod_kernel(c_ref, w_ref, b_ref, o_ref):
    c = c_ref[...]
    cond = c * _sigmoid(c)
    pad_rows = o_ref.shape[0] - cond.shape[0]
    if pad_rows:
        cond = jnp.concatenate([cond, jnp.zeros((pad_rows, cond.shape[1]), F32)], axis=0)
    bias = b_ref[pl.ds(pl.program_id(0), 1), :]
    o_ref[...] = jnp.dot(cond.astype(BF16), w_ref[...].astype(BF16), preferred_element_type=F32) + bias


def _modulation(c, ada_w, ada_b):
    batch = c.shape[0]
    rows = -(-batch // SUBLANES) * SUBLANES
    n_col = 3 * D_MODEL // D_MODEL
    return pl.pallas_call(
        _mod_kernel,
        grid=(DEPTH, n_col),
        in_specs=[
            pl.BlockSpec((batch, D_MODEL), lambda l, j: (0, 0)),
            pl.BlockSpec((None, D_MODEL, D_MODEL), lambda l, j: (l, 0, j)),
            pl.BlockSpec((DEPTH, D_MODEL), lambda l, j: (0, j)),
        ],
        out_specs=pl.BlockSpec((None, rows, D_MODEL), lambda l, j: (l, 0, j)),
        out_shape=jax.ShapeDtypeStruct((DEPTH, rows, 3 * D_MODEL), F32),
        compiler_params=_params("parallel", "parallel"),
        name="adaln_mod",
    )(c, ada_w, ada_b)


MOD_SHIFT, MOD_SCALE, MOD_GATE = range(3)
EVEN_LAYER, ODD_LAYER = 0, 1


def _mod_spec(mod, layer, part):
    return pl.BlockSpec((None, mod.shape[1], D_MODEL), lambda b, i: (layer, 0, part))


def _batch_row(ref):
    return ref[pl.ds(pl.program_id(0), 1), :]


def _even_in_kernel(x_ref, shift_ref, scale_ref, pos_ref, freq_ref, sign_ref, w_f32_ref,
                    q_ref, k_ref, v_ref, su_ref, sv_ref, g_ref, w_ref):
    _cast_weight_once(w_f32_ref, w_ref)
    lane = lax.broadcasted_iota(jnp.int32, (1, LANES), 1)
    first_half = (lane % HEAD_DIM) < ROPE_HALF
    left_head = lane < HEAD_DIM
    diagonal = lax.broadcasted_iota(jnp.int32, (LANES, LANES), 0) == lax.broadcasted_iota(jnp.int32, (LANES, LANES), 1)
    batch_ids = lax.broadcasted_iota(jnp.int32, (pos_ref.shape[0], LANES), 0)
    q_scale = HEAD_DIM ** -0.5 * LOG2_E

    half_rows = x_ref.shape[0] // 2
    for r in range(2):
        rows = slice(r * half_rows, (r + 1) * half_rows)
        h = (x_ref[rows, :] * (1.0 + _batch_row(scale_ref)) + _batch_row(shift_ref)).astype(BF16)
        p = jnp.dot(h, w_ref[...], preferred_element_type=F32)

        pos_cols = []
        for j in range(r * half_rows // LANES, (r + 1) * half_rows // LANES):
            pos_all = pos_ref[:, j * LANES:(j + 1) * LANES].astype(F32)
            pos_row = jnp.sum(jnp.where(batch_ids == pl.program_id(0), pos_all, 0.0), axis=0, keepdims=True)
            pos_cols.append(jnp.sum(jnp.where(diagonal, pos_row, 0.0), axis=1, keepdims=True))
        ang = jnp.concatenate(pos_cols, axis=0) * freq_ref[...]
        cs = jnp.cos(ang)
        sn = jnp.sin(ang) * sign_ref[...]

        def rotary(t):
            partner = jnp.where(first_half, pltpu.roll(t, LANES - ROPE_HALF, 1), pltpu.roll(t, ROPE_HALF, 1))
            return t * cs + partner * sn

        for j in range(ATTN_WIDTH // LANES):
            q_ref[rows, j * LANES:(j + 1) * LANES] = (
                rotary(p[:, j * LANES:(j + 1) * LANES]) * q_scale).astype(BF16)
        o = ATTN_WIDTH
        k = rotary(p[:, o:o + KV_WIDTH])
        k_sw = pltpu.roll(k, HEAD_DIM, 1)
        k_ref[rows, 0:LANES] = jnp.where(left_head, k, k_sw).astype(BF16)
        k_ref[rows, LANES:2 * LANES] = jnp.where(left_head, k_sw, k).astype(BF16)
        o += KV_WIDTH
        v = p[:, o:o + KV_WIDTH]
        v_sw = pltpu.roll(v, HEAD_DIM, 1)
        v_ref[rows, 0 * LANES:1 * LANES] = jnp.where(left_head, v, 0.0).astype(BF16)
        v_ref[rows, 1 * LANES:2 * LANES] = jnp.where(left_head, 0.0, v_sw).astype(BF16)
        v_ref[rows, 2 * LANES:3 * LANES] = jnp.where(left_head, v_sw, 0.0).astype(BF16)
        v_ref[rows, 3 * LANES:4 * LANES] = jnp.where(left_head, 0.0, v).astype(BF16)
        o += KV_WIDTH
        su_ref[rows, :] = p[:, o:o + SG_WIDTH].astype(BF16)
        o += SG_WIDTH
        sv_ref[rows, :] = p[:, o:o + SG_WIDTH].astype(BF16)
        o += SG_WIDTH
        g_ref[rows, :] = p[:, o:o + EVEN_GATE_WIDTH].astype(BF16)


def _even_in_proj(x, mod, positions, w_in):
    batch, seq, _ = x.shape
    d = np.arange(LANES) % HEAD_DIM
    inv_freq = np.power(np.float64(ROPE_THETA), -np.arange(ROPE_HALF) / ROPE_HALF)
    freq_lane = jnp.asarray(np.where(d < ROPE_DIM, inv_freq[d % ROPE_HALF], 0.0), dtype=F32).reshape(1, LANES)
    sign_lane = jnp.asarray(np.where(d < ROPE_HALF, -1.0, np.where(d < ROPE_DIM, 1.0, 0.0)),
                            dtype=F32).reshape(1, LANES)
    row = lambda width: pl.BlockSpec((None, ROW_TILE, width), lambda b, i: (b, i, 0))
    const = lambda shape: pl.BlockSpec(shape, lambda b, i: (0,) * len(shape))
    widths = (ATTN_WIDTH, 2 * KV_WIDTH, 4 * KV_WIDTH, SG_WIDTH, SG_WIDTH, EVEN_GATE_WIDTH)
    return pl.pallas_call(
        _even_in_kernel,
        grid=(batch, seq // ROW_TILE),
        in_specs=[row(D_MODEL), _mod_spec(mod, EVEN_LAYER, MOD_SHIFT), _mod_spec(mod, EVEN_LAYER, MOD_SCALE),
                  pl.BlockSpec((batch, ROW_TILE), lambda b, i: (0, i)),
                  const((1, LANES)), const((1, LANES)), _resident((D_MODEL, EVEN_IN_WIDTH))],
        out_specs=[row(w) for w in widths],
        out_shape=[jax.ShapeDtypeStruct((batch, seq, w), BF16) for w in widths],
        scratch_shapes=[pltpu.VMEM((D_MODEL, EVEN_IN_WIDTH), BF16)],
        compiler_params=_params("arbitrary", "arbitrary"),
        name="even_in_proj",
    )(x, mod, mod, positions, freq_lane, sign_lane, w_in)


def _even_mix_kernel(sink_ref, q_ref, k_ref, v_ref, su_ref, sv_ref, g_ref, x_ref, gate_ref,
                     avg_ref, avg2_ref, sgg_ref, sgb_ref, sgw_ref, sgbias_ref, wout_f32_ref, lng_ref, lnb_ref,
                     o_ref, y_scr, wout_ref):
    _cast_weight_once(wout_f32_ref, wout_ref)
    seq = k_ref.shape[0]
    n_blocks = seq // ATTN_BLOCK
    sub_blocks = MIX_TILE // ATTN_BLOCK
    tile = pl.program_id(1)
    lane = lax.broadcasted_iota(jnp.int32, (1, LANES), 1)
    left = lane < HEAD_DIM

    qi = lax.broadcasted_iota(jnp.int32, (ATTN_BLOCK, ATTN_BLOCK), 0)
    kj = lax.broadcasted_iota(jnp.int32, (ATTN_BLOCK, ATTN_BLOCK), 1)
    ones_left = jnp.broadcast_to(jnp.where(left, 1.0, 0.0).astype(BF16), (3 * ATTN_BLOCK, LANES))
    ones_right = jnp.broadcast_to(jnp.where(left, 0.0, 1.0).astype(BF16), (3 * ATTN_BLOCK, LANES))
    avg = avg_ref[...]
    sink_cols = [jnp.concatenate([jnp.full((ATTN_BLOCK, LANES), sink_ref[h] * LOG2_E, F32)
                                  for h in (4 * hk, 4 * hk + 2, 4 * hk + 1, 4 * hk + 3)], axis=0)
                 for hk in range(N_KV_HEADS)]

    def sub_block(j, carry):
        blk = tile * sub_blocks + j
        prev = jnp.maximum(blk - 1, 0)
        nxt = jnp.minimum(blk + 1, n_blocks - 1)
        rows = pl.ds(pl.multiple_of(j * ATTN_BLOCK, ATTN_BLOCK), ATTN_BLOCK)
        bias_prev = jnp.where((kj >= qi) & (blk > 0), 0.0, NEG_INF)
        bias_next = jnp.where((kj <= qi) & (blk < n_blocks - 1), 0.0, NEG_INF)
        bias_prev = jnp.concatenate([bias_prev] * Q_PER_KV, axis=0)
        bias_next = jnp.concatenate([bias_next] * Q_PER_KV, axis=0)

        def band(ref):
            return jnp.concatenate(
                [ref[pl.ds(pl.multiple_of(b * ATTN_BLOCK, ATTN_BLOCK), ATTN_BLOCK), :] for b in (prev, blk, nxt)],
                axis=0)

        k_band = band(k_ref)
        v_band = band(v_ref)
        n_sg = SG_WIDTH // LANES
        sg_cols = [slice(c * LANES, (c + 1) * LANES) for c in range(n_sg)]

        def silu_gate(cols):
            g_c = g_ref[rows, cols].astype(F32)
            return g_c * _sigmoid(g_c)

        scores = []
        for hk in range(N_KV_HEADS):
            q_cols = [q_ref[rows, (2 * hk + c) * LANES:(2 * hk + c + 1) * LANES] for c in range(2)]
            zero = jnp.zeros_like(q_cols[0])
            lhs = jnp.concatenate([jnp.where(left, qc, zero) for qc in q_cols]
                                  + [jnp.where(left, zero, qc) for qc in q_cols], axis=0)
            scores.append(lax.dot_general(lhs, k_band[:, hk * LANES:(hk + 1) * LANES], (((1,), (1,)), ((), ())),
                                          preferred_element_type=F32))
        sv_cols = [sv_ref[rows, cols] for cols in sg_cols]
        means = [jnp.dot(v_c, avg, preferred_element_type=F32) for v_c in sv_cols]

        probs, row_max = [], []
        for hk in range(N_KV_HEADS):
            s = scores[hk]
            s0 = s[:, 0:ATTN_BLOCK] + bias_prev
            s1 = s[:, ATTN_BLOCK:2 * ATTN_BLOCK]
            s2 = s[:, 2 * ATTN_BLOCK:3 * ATTN_BLOCK] + bias_next
            m = jnp.max(jnp.maximum(jnp.maximum(s0, s1), s2), axis=-1, keepdims=True)
            m = jnp.maximum(jnp.broadcast_to(m, sink_cols[hk].shape), sink_cols[hk])
            probs.append(jnp.concatenate([jnp.exp2(t - m).astype(BF16) for t in (s0, s1, s2)], axis=1))
            row_max.append(m)
        devs, sq_parts = [], []
        for v_c, mean in zip(sv_cols, means):
            dev = v_c.astype(F32) - mean
            sq = dev * dev
            sq_hi = sq.astype(BF16)
            sq_lo = (sq - sq_hi.astype(F32)).astype(BF16)
            devs.append(dev)
            sq_parts.append(jnp.concatenate([sq_hi, sq_lo], axis=1))

        pv = []
        for hk in range(N_KV_HEADS):
            p = probs[hk]
            lhs = jnp.concatenate(
                [jnp.concatenate([p[c * ATTN_BLOCK:(c + 1) * ATTN_BLOCK], p[(2 + c) * ATTN_BLOCK:(3 + c) * ATTN_BLOCK]],
                                 axis=1) for c in range(2)], axis=0)
            rhs = jnp.concatenate(
                [jnp.concatenate([v_band[:, (2 * hk) * LANES:(2 * hk + 1) * LANES], ones_left], axis=1),
                 jnp.concatenate([v_band[:, (2 * hk + 1) * LANES:(2 * hk + 2) * LANES], ones_right], axis=1)],
                axis=0)
            pv.append(jnp.dot(lhs, rhs, preferred_element_type=F32))
        variances = [jnp.dot(sq, avg2_ref[...], preferred_element_type=F32) for sq in sq_parts]

        for hk in range(N_KV_HEADS):
            for c in range(2):
                even, odd = slice(c * ATTN_BLOCK, (c + 1) * ATTN_BLOCK), slice((2 + c) * ATTN_BLOCK, (3 + c) * ATTN_BLOCK)
                sink_term = jnp.exp2(jnp.where(left, sink_cols[hk][even] - row_max[hk][even],
                                               sink_cols[hk][odd] - row_max[hk][odd]))
                o = pv[hk][c * ATTN_BLOCK:(c + 1) * ATTN_BLOCK]
                pair = o[:, 0:LANES] * (1.0 / (o[:, LANES:2 * LANES] + sink_term))
                cols = slice((2 * hk + c) * LANES, (2 * hk + c + 1) * LANES)
                y_scr[rows, cols] = (pair * silu_gate(cols)).astype(BF16)

        for c, cols in enumerate(sg_cols):
            vn = devs[c] * lax.rsqrt(variances[c] + LN_EPS) * sgg_ref[:, cols] + sgb_ref[:, cols]
            stacked = jnp.concatenate([jnp.where(left, vn, 0.0), jnp.where(left, 0.0, vn)],
                                      axis=0).astype(BF16)
            mixed = jnp.dot(sgw_ref[c], stacked, preferred_element_type=F32) + sgbias_ref[:, cols]
            y_sg = su_ref[rows, cols].astype(F32) * mixed
            gc = slice(ATTN_WIDTH + c * LANES, ATTN_WIDTH + (c + 1) * LANES)
            y_scr[rows, gc] = (y_sg * silu_gate(gc)).astype(BF16)
        return carry

    lax.fori_loop(0, sub_blocks, sub_block, 0, unroll=True)

    for r in range(2):
        half_rows = slice(r * (MIX_TILE // 2), (r + 1) * (MIX_TILE // 2))
        out = jnp.dot(y_scr[half_rows, :], wout_ref[...], preferred_element_type=F32)
        o_ref[half_rows, :] = _deepnorm_rows(x_ref[half_rows, :], _batch_row(gate_ref), out,
                                             lng_ref[EVEN_LAYER:EVEN_LAYER + 1, :], lnb_ref[EVEN_LAYER:EVEN_LAYER + 1, :])


def _even_mixer(q, k, v, su, sv, g, x, mod, sink, sg_ln_g, sg_ln_b, sg_w, sg_b, w_out, ln_g, ln_b):
    batch, seq, _ = x.shape
    lane = np.arange(LANES)
    avg = jnp.asarray((lane[:, None] // SG_GROUP_DIM == lane[None, :] // SG_GROUP_DIM) / SG_GROUP_DIM, dtype=BF16)
    w_pairs = sg_w.reshape(N_SG_GROUPS // 2, 2, SG_CHUNK, SG_CHUNK).transpose(0, 2, 1, 3)
    w_pairs = w_pairs.reshape(N_SG_GROUPS // 2, SG_CHUNK, 2 * SG_CHUNK).astype(BF16)
    bias_full = jnp.repeat(sg_b.T, SG_GROUP_DIM, axis=1)
    row = lambda width: pl.BlockSpec((None, MIX_TILE, width), lambda b, i: (b, i, 0))
    full_seq = lambda arr: pl.BlockSpec((None, seq, arr.shape[-1]), lambda b, i: (b, 0, 0))
    const = lambda shape: pl.BlockSpec(shape, lambda b, i: (0,) * len(shape))
    return pl.pallas_call(
        _even_mix_kernel,
        grid=(batch, seq // MIX_TILE),
        in_specs=[
            pl.BlockSpec(memory_space=pltpu.SMEM),
            row(ATTN_WIDTH), full_seq(k), full_seq(v), row(SG_WIDTH), row(SG_WIDTH), row(EVEN_GATE_WIDTH),
            row(D_MODEL), _mod_spec(mod, EVEN_LAYER, MOD_GATE),
            const((LANES, LANES)), const((2 * LANES, LANES)), const((1, SG_WIDTH)), const((1, SG_WIDTH)),
            const((N_SG_GROUPS // 2, SG_CHUNK, 2 * SG_CHUNK)), const((SG_CHUNK, SG_WIDTH)),
            _resident((EVEN_GATE_WIDTH, D_MODEL)), const(ln_g.shape), const(ln_b.shape),
        ],
        out_specs=row(D_MODEL),
        out_shape=jax.ShapeDtypeStruct((batch, seq, D_MODEL), F32),
        scratch_shapes=[pltpu.VMEM((MIX_TILE, EVEN_GATE_WIDTH), BF16),
                        pltpu.VMEM((EVEN_GATE_WIDTH, D_MODEL), BF16)],
        compiler_params=_params("arbitrary", "arbitrary"),
        name="even_mixer",
    )(sink, q, k, v, su, sv, g, x, mod, avg, jnp.concatenate([avg, avg], axis=0), sg_ln_g.reshape(1, -1), sg_ln_b.reshape(1, -1),
      w_pairs, bias_full, w_out, ln_g, ln_b)


def _odd_in_kernel(x_ref, shift_ref, scale_ref, w_f32_ref, xr_ref, w_ref):
    _cast_weight_once(w_f32_ref, w_ref)
    seg_len = xr_ref.shape[1]
    head_rows = seg_len * SEGS_PER_TILE
    xr_rows = xr_ref.reshape(RNN_HEADS * head_rows, LANES)
    halves = 2
    for r in range(halves):
        first = r * (SEGS_PER_TILE // halves)
        rows = slice(first * seg_len, (first + SEGS_PER_TILE // halves) * seg_len)
        h = (x_ref[rows, :] * (1.0 + _batch_row(scale_ref)) + _batch_row(shift_ref)).astype(BF16)
        p = jnp.dot(h, w_ref[...], preferred_element_type=F32)
        for s in range(SEGS_PER_TILE // halves):
            for hd in range(RNN_HEADS):
                xr_rows[pl.ds(hd * head_rows + first + s, seg_len, stride=SEGS_PER_TILE), :] = (
                    p[s * seg_len:(s + 1) * seg_len, hd * LANES:(hd + 1) * LANES])


def _odd_in_proj(x, mod, w_in):
    batch, seq, _ = x.shape
    seg_len = seq // N_SEG
    tile_rows = SEGS_PER_TILE * seg_len
    out_spec = pl.BlockSpec((None, RNN_HEADS, seg_len, SEGS_PER_TILE, RNN_HEAD_DIM), lambda b, i: (b, 0, 0, i, 0))
    out_shape = jax.ShapeDtypeStruct((batch, RNN_HEADS, seg_len, N_SEG, RNN_HEAD_DIM), F32)
    return pl.pallas_call(
        _odd_in_kernel,
        grid=(batch, seq // tile_rows),
        in_specs=[pl.BlockSpec((None, tile_rows, D_MODEL), lambda b, i: (b, i, 0)),
                  _mod_spec(mod, ODD_LAYER, MOD_SHIFT), _mod_spec(mod, ODD_LAYER, MOD_SCALE),
                  _resident((D_MODEL, RNN_WIDTH), col_block=0)],
        out_specs=out_spec,
        out_shape=out_shape,
        scratch_shapes=[pltpu.VMEM((D_MODEL, RNN_WIDTH), BF16)],
        compiler_params=_params("arbitrary", "arbitrary"),
        name="odd_in_proj",
    )(x, mod, mod, w_in)


def _rglru_kernel(x_ref, convw_ref, convb_ref, waug_ref, lam_ref, o_ref,
                  halo, a_f, b_f, a_b, b_b, edge):
    n_rows = x_ref.shape[0]
    seg_len = n_rows // N_SEG
    n_chunks = n_rows // SCAN_CHUNK
    left_rows = CONV_LEFT * N_SEG
    right_rows = (CONV_WIDTH - 1 - CONV_LEFT) * N_SEG
    seg = lax.broadcasted_iota(jnp.int32, (N_SEG, LANES), 0)

    def from_prev_segment(blk):
        return jnp.where(seg == 0, 0.0, pltpu.roll(blk, 1, 0))

    def from_next_segment(blk):
        return jnp.where(seg == N_SEG - 1, 0.0, pltpu.roll(blk, N_SEG - 1, 0))

    for t in range(CONV_LEFT):
        halo[0, pl.ds(t * N_SEG, N_SEG), :] = from_prev_segment(
            x_ref[pl.ds((seg_len - CONV_LEFT + t) * N_SEG, N_SEG), :])
    halo[0, pl.ds(left_rows, SCAN_CHUNK + right_rows), :] = x_ref[pl.ds(0, SCAN_CHUNK + right_rows), :]
    halo[1, pl.ds(0, left_rows + SCAN_CHUNK), :] = x_ref[pl.ds(n_rows - SCAN_CHUNK - left_rows, left_rows + SCAN_CHUNK), :]
    for t in range(CONV_WIDTH - 1 - CONV_LEFT):
        halo[1, pl.ds(left_rows + SCAN_CHUNK + t * N_SEG, N_SEG), :] = from_next_segment(
            x_ref[pl.ds(t * N_SEG, N_SEG), :])

    lam = lam_ref[...]
    softplus_neg = jnp.maximum(-lam, 0.0) + jnp.log1p(jnp.exp(-jnp.abs(lam)))
    half_rate = (0.5 * RG_LRU_C) * softplus_neg
    bias_lanes = jnp.broadcast_to(
        jnp.where(lax.broadcasted_iota(jnp.int32, (1, LANES), 1) < BIAS_ROWS, 1.0, 0.0).astype(BF16),
        (SCAN_CHUNK, LANES))

    def gates(base, tap):
        half_x = convb_ref[...] + sum(convw_ref[k:k + 1, :] * tap(k) for k in range(CONV_WIDTH))
        th = jnp.tanh(jnp.dot(jnp.concatenate([half_x.astype(BF16), bias_lanes], axis=1), waug_ref[...],
                              preferred_element_type=F32))
        for d, (a_ref, b_ref) in enumerate(((a_f, b_f), (a_b, b_b))):
            o = 2 * d * LANES
            neg_log_a = (th[:, o:o + LANES] + 1.0) * half_rate[d:d + 1, :]
            a = jnp.exp2(neg_log_a * (-LOG2_E))
            one_minus_a2 = jnp.tanh(neg_log_a) * (a * a + 1.0)
            root = one_minus_a2 * lax.rsqrt(jnp.maximum(one_minus_a2, F32_TINY))
            a_ref[pl.ds(base, SCAN_CHUNK), :] = a
            b_ref[pl.ds(base, SCAN_CHUNK), :] = root * ((th[:, o + LANES:o + 2 * LANES] + 1.0) * half_x)

    def inner_chunk(c, carry):
        base = pl.multiple_of(c * SCAN_CHUNK, SCAN_CHUNK)
        gates(base, lambda k: x_ref[pl.ds(base + (k - CONV_LEFT) * N_SEG, SCAN_CHUNK), :])
        return carry

    gates(0, lambda k: halo[0, pl.ds(k * N_SEG, SCAN_CHUNK), :])
    gates(n_rows - SCAN_CHUNK, lambda k: halo[1, pl.ds(k * N_SEG, SCAN_CHUNK), :])
    lax.fori_loop(1, n_chunks - 1, inner_chunk, 0, unroll=7)

    def scan_step(t, carry):
        h_f, p_f, h_b, p_b = carry
        rf = pl.ds(pl.multiple_of(t * N_SEG, N_SEG), N_SEG)
        rb = pl.ds(pl.multiple_of((seg_len - 1 - t) * N_SEG, N_SEG), N_SEG)
        a = a_f[rf, :]
        h_f = a * h_f + b_f[rf, :]
        p_f = a * p_f
        b_f[rf, :] = h_f
        a_f[rf, :] = p_f
        a = a_b[rb, :]
        h_b = a * h_b + b_b[rb, :]
        p_b = a * p_b
        b_b[rb, :] = h_b
        a_b[rb, :] = p_b
        return h_f, p_f, h_b, p_b

    zeros = jnp.zeros((N_SEG, LANES), F32)
    ones = jnp.ones((N_SEG, LANES), F32)
    h_f, p_f, h_b, p_b = lax.fori_loop(0, seg_len, scan_step, (zeros, ones, zeros, ones), unroll=4)

    edge[0] = h_f
    edge[1] = p_f
    edge[2] = h_b
    edge[3] = p_b
    enter_f = enter_b = jnp.zeros((1, LANES), F32)
    edge[4, 0:1, :] = enter_f
    edge[5, N_SEG - 1:N_SEG, :] = enter_b
    for s in range(1, N_SEG):
        enter_f = edge[0, s - 1:s, :] + edge[1, s - 1:s, :] * enter_f
        edge[4, s:s + 1, :] = enter_f
        r = N_SEG - 1 - s
        enter_b = edge[2, r + 1:r + 2, :] + edge[3, r + 1:r + 2, :] * enter_b
        edge[5, r:r + 1, :] = enter_b
    blocks = SCAN_CHUNK // N_SEG
    in_f = jnp.concatenate([edge[4]] * blocks, axis=0)
    in_b = jnp.concatenate([edge[5]] * blocks, axis=0)

    def out_chunk(c, carry):
        rows = pl.ds(pl.multiple_of(c * SCAN_CHUNK, SCAN_CHUNK), SCAN_CHUNK)
        o_ref[rows, :] = (b_f[rows, :] + a_f[rows, :] * in_f) + (b_b[rows, :] + a_b[rows, :] * in_b)
        return carry

    lax.fori_loop(0, n_rows // SCAN_CHUNK, out_chunk, 0, unroll=2)


def _rglru(xr, conv_w, conv_b, w_a, b_a, w_x, b_x, lam):
    batch, _, n_rows, _ = xr.shape
    w_cat = jnp.concatenate([w_a[0], w_x[0], w_a[1], w_x[1]], axis=-1).astype(BF16)
    per_head = lambda v: v.reshape(RNN_HEADS, 1, RNN_HEAD_DIM)
    half_b = 0.5 * jnp.concatenate([per_head(b_a[0]), per_head(b_x[0]), per_head(b_a[1]), per_head(b_x[1])],
                                   axis=-1)
    b_hi = half_b.astype(BF16)
    b_lo = (half_b - b_hi.astype(F32)).astype(BF16)
    w_aug = jnp.concatenate([w_cat, b_hi, b_lo,
                             jnp.zeros((RNN_HEADS, RNN_HEAD_DIM - BIAS_ROWS, 4 * RNN_HEAD_DIM), BF16)], axis=1)
    col = lambda rows: pl.BlockSpec((rows, RNN_HEAD_DIM), lambda b, h: (0, h))
    seq_col = pl.BlockSpec((None, None, n_rows, RNN_HEAD_DIM), lambda b, h: (b, h, 0, 0))
    halo_rows = SCAN_CHUNK + (CONV_WIDTH - 1) * N_SEG
    return pl.pallas_call(
        _rglru_kernel,
        grid=(batch, RNN_HEADS),
        in_specs=[seq_col, col(CONV_WIDTH), col(1),
                  pl.BlockSpec((None, 2 * RNN_HEAD_DIM, 4 * RNN_HEAD_DIM), lambda b, h: (h, 0, 0)),
                  col(2)],
        out_specs=seq_col,
        out_shape=jax.ShapeDtypeStruct((batch, RNN_HEADS, n_rows, RNN_HEAD_DIM), F32),
        scratch_shapes=[pltpu.VMEM((2, halo_rows, LANES), F32)]
                       + [pltpu.VMEM((n_rows, LANES), F32)] * 4
                       + [pltpu.VMEM((6, N_SEG, LANES), F32)],
        compiler_params=_params("parallel", "parallel"),
        name="rglru",
    )(xr, 0.5 * conv_w, 0.5 * conv_b.reshape(1, -1), w_aug, lam)


def _odd_out_kernel(y_ref, x_ref, shift_ref, scale_ref, gate_ref, wg_f32_ref, w_f32_ref, lng_ref, lnb_ref,
                    o_ref, wg_ref, w_ref):
    _cast_weight_once(wg_f32_ref, wg_ref)
    _cast_weight_once(w_f32_ref, w_ref)
    seg_len = y_ref.shape[1]
    head_rows = seg_len * SEGS_PER_TILE
    y_rows = y_ref.reshape(RNN_HEADS * head_rows, LANES)
    halves = 1
    for r in range(halves):
        first = r * (SEGS_PER_TILE // halves)
        rows = slice(first * seg_len, (first + SEGS_PER_TILE // halves) * seg_len)
        x = x_ref[rows, :]
        h = (x * (1.0 + _batch_row(scale_ref)) + _batch_row(shift_ref)).astype(BF16)
        g_all = jnp.dot(h, wg_ref[...], preferred_element_type=F32)
        gated = []
        for s in range(SEGS_PER_TILE // halves):
            y = jnp.concatenate([y_rows[pl.ds(hd * head_rows + first + s, seg_len, stride=SEGS_PER_TILE), :]
                                 for hd in range(RNN_HEADS)], axis=1)
            g = g_all[s * seg_len:(s + 1) * seg_len, :]
            gated.append((y * (g * _sigmoid(g))).astype(BF16))
        out = jnp.dot(jnp.concatenate(gated, axis=0), w_ref[...], preferred_element_type=F32)
        o_ref[rows, :] = _deepnorm_rows(x, _batch_row(gate_ref), out,
                                        lng_ref[ODD_LAYER:ODD_LAYER + 1, :], lnb_ref[ODD_LAYER:ODD_LAYER + 1, :])


def _odd_out_proj(y, x, mod, w_in, w_out, ln_g, ln_b):
    batch, seq, _ = x.shape
    seg_len = seq // N_SEG
    tile_rows = SEGS_PER_TILE * seg_len
    const = lambda shape: pl.BlockSpec(shape, lambda b, i: (0,) * len(shape))
    return pl.pallas_call(
        _odd_out_kernel,
        grid=(batch, seq // tile_rows),
        in_specs=[pl.BlockSpec((None, RNN_HEADS, seg_len, SEGS_PER_TILE, RNN_HEAD_DIM),
                               lambda b, i: (b, 0, 0, i, 0)),
                  pl.BlockSpec((None, tile_rows, D_MODEL), lambda b, i: (b, i, 0)),
                  _mod_spec(mod, ODD_LAYER, MOD_SHIFT), _mod_spec(mod, ODD_LAYER, MOD_SCALE),
                  _mod_spec(mod, ODD_LAYER, MOD_GATE),
                  _resident((D_MODEL, RNN_WIDTH), col_block=1), _resident((RNN_WIDTH, D_MODEL)),
                  const(ln_g.shape), const(ln_b.shape)],
        out_specs=pl.BlockSpec((None, tile_rows, D_MODEL), lambda b, i: (b, i, 0)),
        out_shape=jax.ShapeDtypeStruct((batch, seq, D_MODEL), F32),
        scratch_shapes=[pltpu.VMEM((D_MODEL, RNN_WIDTH), BF16), pltpu.VMEM((RNN_WIDTH, D_MODEL), BF16)],
        compiler_params=_params("arbitrary", "arbitrary"),
        name="odd_out_proj",
    )(y, x, mod, mod, mod, w_in, w_out, ln_g, ln_b)


def kernel(x, c, positions, ada_w, ada_b, ln_g, ln_b, ev_w_in, ev_w_out, ev_sink, ev_sg_ln_g, ev_sg_ln_b,
           ev_sg_w, ev_sg_b, od_w_in, od_conv_w, od_conv_b, od_w_a, od_b_a, od_w_x, od_b_x, od_lam, od_w_out):
    batch, seq, d_model = x.shape
    seg_len = seq // N_SEG
    assert d_model == D_MODEL and seq % ROW_TILE == 0 and seq % MIX_TILE == 0 and seq == N_SEG * seg_len
    assert N_SEG % SEGS_PER_TILE == 0 and seq % SCAN_CHUNK == 0 and SCAN_CHUNK % N_SEG == 0
    assert ada_w.shape[0] == DEPTH == 2

    mod = _modulation(c, ada_w, ada_b)

    q, k, v, su, sv, g = _even_in_proj(x, mod, positions, ev_w_in[0])
    x1 = _even_mixer(q, k, v, su, sv, g, x, mod, ev_sink[0], ev_sg_ln_g[0], ev_sg_ln_b[0],
                     ev_sg_w[0], ev_sg_b[0], ev_w_out[0], ln_g, ln_b)

    xr = _odd_in_proj(x1, mod, od_w_in[0])
    y = _rglru(xr.reshape(batch, RNN_HEADS, seq, RNN_HEAD_DIM), od_conv_w[0], od_conv_b[0], od_w_a[0],
               od_b_a[0], od_w_x[0], od_b_x[0], od_lam[0])
    return _odd_out_proj(y.reshape(batch, RNN_HEADS, seg_len, N_SEG, RNN_HEAD_DIM), x1, mod, od_w_in[0],
                         od_w_out[0], ln_g, ln_b)
```

```python
import math

import numpy as np
import jax
import jax.numpy as jnp
from jax import lax
from jax.experimental import pallas as pl
from jax.experimental.pallas import tpu as pltpu

D_MODEL = 1024
DEPTH = 2
HEAD_DIM = 64
N_Q_HEADS = 8
N_KV_HEADS = 2
Q_PER_KV = N_Q_HEADS // N_KV_HEADS
ATTN_WIDTH = N_Q_HEADS * HEAD_DIM
KV_WIDTH = N_KV_HEADS * HEAD_DIM
ATTN_BLOCK = 128
ROPE_DIM = HEAD_DIM // 4
ROPE_HALF = ROPE_DIM // 2
ROPE_THETA = 500000.0
N_SG_GROUPS = 8
SG_GROUP_DIM = 64
SG_WIDTH = N_SG_GROUPS * SG_GROUP_DIM
SG_CHUNK = 128
EVEN_GATE_WIDTH = ATTN_WIDTH + SG_WIDTH
EVEN_IN_WIDTH = ATTN_WIDTH + 2 * KV_WIDTH + 2 * SG_WIDTH + EVEN_GATE_WIDTH
RNN_WIDTH = D_MODEL
RNN_HEADS = 8
RNN_HEAD_DIM = RNN_WIDTH // RNN_HEADS
CONV_WIDTH = 4
CONV_LEFT = 2
RG_LRU_C = 8.0
DEEPNORM_ALPHA = (2 * DEPTH) ** 0.25
LN_EPS = 1e-5
NEG_INF = -1e30
LOG2_E = math.log2(math.e)
F32_TINY = float(np.finfo(np.float32).tiny)

LANES = 128
SUBLANES = 8
VMEM_LIMIT = 56 * 1024 * 1024

ROW_TILE = 512
MIX_TILE = 512
N_SEG = 32
SEGS_PER_TILE = SUBLANES
SCAN_CHUNK = 256
WEIGHT_CAST_ROWS = 128
BIAS_ROWS = 2

BF16 = jnp.bfloat16
F32 = jnp.float32


def _sigmoid(x):
    return 0.5 * jnp.tanh(0.5 * x) + 0.5


def _layer_norm_rows(z, g, b, eps=LN_EPS):
    mu = jnp.mean(z, axis=-1, keepdims=True)
    d = z - mu
    var = jnp.mean(d * d, axis=-1, keepdims=True)
    return d * lax.rsqrt(var + eps) * g + b


def _deepnorm_rows(x, gate, y, g, b):
    return _layer_norm_rows(x + (gate * (1.0 / DEEPNORM_ALPHA)) * y, g, b, LN_EPS / DEEPNORM_ALPHA ** 2)


def _params(*semantics):
    return pltpu.CompilerParams(dimension_semantics=semantics, vmem_limit_bytes=VMEM_LIMIT)


def _cast_weight_once(w_ref, w_bf16):
    @pl.when((pl.program_id(0) == 0) & (pl.program_id(1) == 0))
    def _():
        for r in range(0, w_ref.shape[0], WEIGHT_CAST_ROWS):
            w_bf16[r:r + WEIGHT_CAST_ROWS, :] = w_ref[r:r + WEIGHT_CAST_ROWS, :].astype(BF16)


def _resident(shape, col_block=0):
    return pl.BlockSpec(shape, lambda b, i: (0,) * (len(shape) - 1) + (col_block,), pipeline_mode=pl.Buffered(1))


def _mod_kernel(c_ref, w_ref, b_ref, o_ref):
    c = c_ref[...]
    cond = c * _sigmoid(c)
    pad_rows = o_ref.shape[0] - cond.shape[0]
    if pad_rows:
        cond = jnp.concatenate([cond, jnp.zeros((pad_rows, cond.shape[1]), F32)], axis=0)
    bias = b_ref[pl.ds(pl.program_id(0), 1), :]
    o_ref[...] = jnp.dot(cond.astype(BF16), w_ref[...].astype(BF16), preferred_element_type=F32) + bias


def _modulation(c, ada_w, ada_b):
    batch = c.shape[0]
    rows = -(-batch // SUBLANES) * SUBLANES
    n_col = 3 * D_MODEL // D_MODEL
    return pl.pallas_call(
        _mod_kernel,
        grid=(DEPTH, n_col),
        in_specs=[
            pl.BlockSpec((batch, D_MODEL), lambda l, j: (0, 0)),
            pl.BlockSpec((None, D_MODEL, D_MODEL), lambda l, j: (l, 0, j)),
            pl.BlockSpec((DEPTH, D_MODEL), lambda l, j: (0, j)),
        ],
        out_specs=pl.BlockSpec((None, rows, D_MODEL), lambda l, j: (l, 0, j)),
        out_shape=jax.ShapeDtypeStruct((DEPTH, rows, 3 * D_MODEL), F32),
        compiler_params=_params("parallel", "parallel"),
        name="adaln_mod",
    )(c, ada_w, ada_b)


MOD_SHIFT, MOD_SCALE, MOD_GATE = range(3)
EVEN_LAYER, ODD_LAYER = 0, 1


def _mod_spec(mod, layer, part):
    return pl.BlockSpec((None, mod.shape[1], D_MODEL), lambda b, i: (layer, 0, part))


def _batch_row(ref):
    return ref[pl.ds(pl.program_id(0), 1), :]


def _even_in_kernel(x_ref, shift_ref, scale_ref, pos_ref, freq_ref, sign_ref, w_f32_ref,
                    q_ref, k_ref, v_ref, su_ref, sv_ref, g_ref, w_ref):
    _cast_weight_once(w_f32_ref, w_ref)
    lane = lax.broadcasted_iota(jnp.int32, (1, LANES), 1)
    first_half = (lane % HEAD_DIM) < ROPE_HALF
    left_head = lane < HEAD_DIM
    diagonal = lax.broadcasted_iota(jnp.int32, (LANES, LANES), 0) == lax.broadcasted_iota(jnp.int32, (LANES, LANES), 1)
    batch_ids = lax.broadcasted_iota(jnp.int32, (pos_ref.shape[0], LANES), 0)
    q_scale = HEAD_DIM ** -0.5 * LOG2_E

    half_rows = x_ref.shape[0] // 2
    for r in range(2):
        rows = slice(r * half_rows, (r + 1) * half_rows)
        h = (x_ref[rows, :] * (1.0 + _batch_row(scale_ref)) + _batch_row(shift_ref)).astype(BF16)
        p = jnp.dot(h, w_ref[...], preferred_element_type=F32)

        pos_cols = []
        for j in range(r * half_rows // LANES, (r + 1) * half_rows // LANES):
            pos_all = pos_ref[:, j * LANES:(j + 1) * LANES].astype(F32)
            pos_row = jnp.sum(jnp.where(batch_ids == pl.program_id(0), pos_all, 0.0), axis=0, keepdims=True)
            pos_cols.append(jnp.sum(jnp.where(diagonal, pos_row, 0.0), axis=1, keepdims=True))
        ang = jnp.concatenate(pos_cols, axis=0) * freq_ref[...]
        cs = jnp.cos(ang)
        sn = jnp.sin(ang) * sign_ref[...]

        def rotary(t):
            partner = jnp.where(first_half, pltpu.roll(t, LANES - ROPE_HALF, 1), pltpu.roll(t, ROPE_HALF, 1))
            return t * cs + partner * sn

        for j in range(ATTN_WIDTH // LANES):
            q_ref[rows, j * LANES:(j + 1) * LANES] = (
                rotary(p[:, j * LANES:(j + 1) * LANES]) * q_scale).astype(BF16)
        o = ATTN_WIDTH
        k = rotary(p[:, o:o + KV_WIDTH])
        k_sw = pltpu.roll(k, HEAD_DIM, 1)
        k_ref[rows, 0:LANES] = jnp.where(left_head, k, k_sw).astype(BF16)
        k_ref[rows, LANES:2 * LANES] = jnp.where(left_head, k_sw, k).astype(BF16)
        o += KV_WIDTH
        v = p[:, o:o + KV_WIDTH]
        v_sw = pltpu.roll(v, HEAD_DIM, 1)
        v_ref[rows, 0 * LANES:1 * LANES] = jnp.where(left_head, v, 0.0).astype(BF16)
        v_ref[rows, 1 * LANES:2 * LANES] = jnp.where(left_head, 0.0, v_sw).astype(BF16)
        v_ref[rows, 2 * LANES:3 * LANES] = jnp.where(left_head, v_sw, 0.0).astype(BF16)
        v_ref[rows, 3 * LANES:4 * LANES] = jnp.where(left_head, 0.0, v).astype(BF16)
        o += KV_WIDTH
        su_ref[rows, :] = p[:, o:o + SG_WIDTH].astype(BF16)
        o += SG_WIDTH
        sv_ref[rows, :] = p[:, o:o + SG_WIDTH].astype(BF16)
        o += SG_WIDTH
        g_ref[rows, :] = p[:, o:o + EVEN_GATE_WIDTH].astype(BF16)


def _even_in_proj(x, mod, positions, w_in):
    batch, seq, _ = x.shape
    d = np.arange(LANES) % HEAD_DIM
    inv_freq = np.power(np.float64(ROPE_THETA), -np.arange(ROPE_HALF) / ROPE_HALF)
    freq_lane = jnp.asarray(np.where(d < ROPE_DIM, inv_freq[d % ROPE_HALF], 0.0), dtype=F32).reshape(1, LANES)
    sign_lane = jnp.asarray(np.where(d < ROPE_HALF, -1.0, np.where(d < ROPE_DIM, 1.0, 0.0)),
                            dtype=F32).reshape(1, LANES)
    row = lambda width: pl.BlockSpec((None, ROW_TILE, width), lambda b, i: (b, i, 0))
    const = lambda shape: pl.BlockSpec(shape, lambda b, i: (0,) * len(shape))
    widths = (ATTN_WIDTH, 2 * KV_WIDTH, 4 * KV_WIDTH, SG_WIDTH, SG_WIDTH, EVEN_GATE_WIDTH)
    return pl.pallas_call(
        _even_in_kernel,
        grid=(batch, seq // ROW_TILE),
        in_specs=[row(D_MODEL), _mod_spec(mod, EVEN_LAYER, MOD_SHIFT), _mod_spec(mod, EVEN_LAYER, MOD_SCALE),
                  pl.BlockSpec((batch, ROW_TILE), lambda b, i: (0, i)),
                  const((1, LANES)), const((1, LANES)), _resident((D_MODEL, EVEN_IN_WIDTH))],
        out_specs=[row(w) for w in widths],
        out_shape=[jax.ShapeDtypeStruct((batch, seq, w), BF16) for w in widths],
        scratch_shapes=[pltpu.VMEM((D_MODEL, EVEN_IN_WIDTH), BF16)],
        compiler_params=_params("arbitrary", "arbitrary"),
        name="even_in_proj",
    )(x, mod, mod, positions, freq_lane, sign_lane, w_in)


def _even_mix_kernel(sink_ref, q_ref, k_ref, v_ref, su_ref, sv_ref, g_ref, x_ref, gate_ref,
                     avg_ref, avg2_ref, sgg_ref, sgb_ref, sgw_ref, sgbias_ref, wout_f32_ref, lng_ref, lnb_ref,
                     o_ref, y_scr, wout_ref):
    _cast_weight_once(wout_f32_ref, wout_ref)
    seq = k_ref.shape[0]
    n_blocks = seq // ATTN_BLOCK
    sub_blocks = MIX_TILE // ATTN_BLOCK
    tile = pl.program_id(1)
    lane = lax.broadcasted_iota(jnp.int32, (1, LANES), 1)
    left = lane < HEAD_DIM

    qi = lax.broadcasted_iota(jnp.int32, (ATTN_BLOCK, ATTN_BLOCK), 0)
    kj = lax.broadcasted_iota(jnp.int32, (ATTN_BLOCK, ATTN_BLOCK), 1)
    ones_left = jnp.broadcast_to(jnp.where(left, 1.0, 0.0).astype(BF16), (3 * ATTN_BLOCK, LANES))
    ones_right = jnp.broadcast_to(jnp.where(left, 0.0, 1.0).astype(BF16), (3 * ATTN_BLOCK, LANES))
    avg = avg_ref[...]
    sink_cols = [jnp.concatenate([jnp.full((ATTN_BLOCK, LANES), sink_ref[h] * LOG2_E, F32)
                                  for h in (4 * hk, 4 * hk + 2, 4 * hk + 1, 4 * hk + 3)], axis=0)
                 for hk in range(N_KV_HEADS)]

    def sub_block(j, carry):
        blk = tile * sub_blocks + j
        prev = jnp.maximum(blk - 1, 0)
        nxt = jnp.minimum(blk + 1, n_blocks - 1)
        rows = pl.ds(pl.multiple_of(j * ATTN_BLOCK, ATTN_BLOCK), ATTN_BLOCK)
        bias_prev = jnp.where((kj >= qi) & (blk > 0), 0.0, NEG_INF)
        bias_next = jnp.where((kj <= qi) & (blk < n_blocks - 1), 0.0, NEG_INF)
        bias_prev = jnp.concatenate([bias_prev] * Q_PER_KV, axis=0)
        bias_next = jnp.concatenate([bias_next] * Q_PER_KV, axis=0)

        def band(ref):
            return jnp.concatenate(
                [ref[pl.ds(pl.multiple_of(b * ATTN_BLOCK, ATTN_BLOCK), ATTN_BLOCK), :] for b in (prev, blk, nxt)],
                axis=0)

        k_band = band(k_ref)
        v_band = band(v_ref)
        n_sg = SG_WIDTH // LANES
        sg_cols = [slice(c * LANES, (c + 1) * LANES) for c in range(n_sg)]

        def silu_gate(cols):
            g_c = g_ref[rows, cols].astype(F32)
            return g_c * _sigmoid(g_c)

        scores = []
        for hk in range(N_KV_HEADS):
            q_cols = [q_ref[rows, (2 * hk + c) * LANES:(2 * hk + c + 1) * LANES] for c in range(2)]
            zero = jnp.zeros_like(q_cols[0])
            lhs = jnp.concatenate([jnp.where(left, qc, zero) for qc in q_cols]
                                  + [jnp.where(left, zero, qc) for qc in q_cols], axis=0)
            scores.append(lax.dot_general(lhs, k_band[:, hk * LANES:(hk + 1) * LANES], (((1,), (1,)), ((), ())),
                                          preferred_element_type=F32))
        sv_cols = [sv_ref[rows, cols] for cols in sg_cols]
        means = [jnp.dot(v_c, avg, preferred_element_type=F32) for v_c in sv_cols]

        probs, row_max = [], []
        for hk in range(N_KV_HEADS):
            s = scores[hk]
            s0 = s[:, 0:ATTN_BLOCK] + bias_prev
            s1 = s[:, ATTN_BLOCK:2 * ATTN_BLOCK]
            s2 = s[:, 2 * ATTN_BLOCK:3 * ATTN_BLOCK] + bias_next
            m = jnp.max(jnp.maximum(jnp.maximum(s0, s1), s2), axis=-1, keepdims=True)
            m = jnp.maximum(jnp.broadcast_to(m, sink_cols[hk].shape), sink_cols[hk])
            probs.append(jnp.concatenate([jnp.exp2(t - m).astype(BF16) for t in (s0, s1, s2)], axis=1))
            row_max.append(m)
        devs, sq_parts = [], []
        for v_c, mean in zip(sv_cols, means):
            dev = v_c.astype(F32) - mean
            sq = dev * dev
            sq_hi = sq.astype(BF16)
            sq_lo = (sq - sq_hi.astype(F32)).astype(BF16)
            devs.append(dev)
            sq_parts.append(jnp.concatenate([sq_hi, sq_lo], axis=1))

        pv = []
        for hk in range(N_KV_HEADS):
            p = probs[hk]
            lhs = jnp.concatenate(
                [jnp.concatenate([p[c * ATTN_BLOCK:(c + 1) * ATTN_BLOCK], p[(2 + c) * ATTN_BLOCK:(3 + c) * ATTN_BLOCK]],
                                 axis=1) for c in range(2)], axis=0)
            rhs = jnp.concatenate(
                [jnp.concatenate([v_band[:, (2 * hk) * LANES:(2 * hk + 1) * LANES], ones_left], axis=1),
                 jnp.concatenate([v_band[:, (2 * hk + 1) * LANES:(2 * hk + 2) * LANES], ones_right], axis=1)],
                axis=0)
            pv.append(jnp.dot(lhs, rhs, preferred_element_type=F32))
        variances = [jnp.dot(sq, avg2_ref[...], preferred_element_type=F32) for sq in sq_parts]

        for hk in range(N_KV_HEADS):
            for c in range(2):
                even, odd = slice(c * ATTN_BLOCK, (c + 1) * ATTN_BLOCK), slice((2 + c) * ATTN_BLOCK, (3 + c) * ATTN_BLOCK)
                sink_term = jnp.exp2(jnp.where(left, sink_cols[hk][even] - row_max[hk][even],
                                               sink_cols[hk][odd] - row_max[hk][odd]))
                o = pv[hk][c * ATTN_BLOCK:(c + 1) * ATTN_BLOCK]
                pair = o[:, 0:LANES] * (1.0 / (o[:, LANES:2 * LANES] + sink_term))
                cols = slice((2 * hk + c) * LANES, (2 * hk + c + 1) * LANES)
                y_scr[rows, cols] = (pair * silu_gate(cols)).astype(BF16)

        for c, cols in enumerate(sg_cols):
            vn = devs[c] * lax.rsqrt(variances[c] + LN_EPS) * sgg_ref[:, cols] + sgb_ref[:, cols]
            stacked = jnp.concatenate([jnp.where(left, vn, 0.0), jnp.where(left, 0.0, vn)],
                                      axis=0).astype(BF16)
            mixed = jnp.dot(sgw_ref[c], stacked, preferred_element_type=F32) + sgbias_ref[:, cols]
            y_sg = su_ref[rows, cols].astype(F32) * mixed
            gc = slice(ATTN_WIDTH + c * LANES, ATTN_WIDTH + (c + 1) * LANES)
            y_scr[rows, gc] = (y_sg * silu_gate(gc)).astype(BF16)
        return carry

    lax.fori_loop(0, sub_blocks, sub_block, 0, unroll=True)

    out = jnp.dot(y_scr[...], wout_ref[...], preferred_element_type=F32)
    o_ref[...] = _deepnorm_rows(x_ref[...], _batch_row(gate_ref), out,
                                lng_ref[EVEN_LAYER:EVEN_LAYER + 1, :], lnb_ref[EVEN_LAYER:EVEN_LAYER + 1, :])


def _even_mixer(q, k, v, su, sv, g, x, mod, sink, sg_ln_g, sg_ln_b, sg_w, sg_b, w_out, ln_g, ln_b):
    batch, seq, _ = x.shape
    lane = np.arange(LANES)
    avg = jnp.asarray((lane[:, None] // SG_GROUP_DIM == lane[None, :] // SG_GROUP_DIM) / SG_GROUP_DIM, dtype=BF16)
    w_pairs = sg_w.reshape(N_SG_GROUPS // 2, 2, SG_CHUNK, SG_CHUNK).transpose(0, 2, 1, 3)
    w_pairs = w_pairs.reshape(N_SG_GROUPS // 2, SG_CHUNK, 2 * SG_CHUNK).astype(BF16)
    bias_full = jnp.repeat(sg_b.T, SG_GROUP_DIM, axis=1)
    row = lambda width: pl.BlockSpec((None, MIX_TILE, width), lambda b, i: (b, i, 0))
    full_seq = lambda arr: pl.BlockSpec((None, seq, arr.shape[-1]), lambda b, i: (b, 0, 0))
    const = lambda shape: pl.BlockSpec(shape, lambda b, i: (0,) * len(shape))
    return pl.pallas_call(
        _even_mix_kernel,
        grid=(batch, seq // MIX_TILE),
        in_specs=[
            pl.BlockSpec(memory_space=pltpu.SMEM),
            row(ATTN_WIDTH), full_seq(k), full_seq(v), row(SG_WIDTH), row(SG_WIDTH), row(EVEN_GATE_WIDTH),
            row(D_MODEL), _mod_spec(mod, EVEN_LAYER, MOD_GATE),
            const((LANES, LANES)), const((2 * LANES, LANES)), const((1, SG_WIDTH)), const((1, SG_WIDTH)),
            const((N_SG_GROUPS // 2, SG_CHUNK, 2 * SG_CHUNK)), const((SG_CHUNK, SG_WIDTH)),
            _resident((EVEN_GATE_WIDTH, D_MODEL)), const(ln_g.shape), const(ln_b.shape),
        ],
        out_specs=row(D_MODEL),
        out_shape=jax.ShapeDtypeStruct((batch, seq, D_MODEL), F32),
        scratch_shapes=[pltpu.VMEM((MIX_TILE, EVEN_GATE_WIDTH), BF16),
                        pltpu.VMEM((EVEN_GATE_WIDTH, D_MODEL), BF16)],
        compiler_params=_params("arbitrary", "arbitrary"),
        name="even_mixer",
    )(sink, q, k, v, su, sv, g, x, mod, avg, jnp.concatenate([avg, avg], axis=0), sg_ln_g.reshape(1, -1), sg_ln_b.reshape(1, -1),
      w_pairs, bias_full, w_out, ln_g, ln_b)


def _odd_in_kernel(x_ref, shift_ref, scale_ref, w_f32_ref, xr_ref, w_ref):
    _cast_weight_once(w_f32_ref, w_ref)
    seg_len = xr_ref.shape[1]
    head_rows = seg_len * SEGS_PER_TILE
    xr_rows = xr_ref.reshape(RNN_HEADS * head_rows, LANES)
    halves = 2
    for r in range(halves):
        first = r * (SEGS_PER_TILE // halves)
        rows = slice(first * seg_len, (first + SEGS_PER_TILE // halves) * seg_len)
        h = (x_ref[rows, :] * (1.0 + _batch_row(scale_ref)) + _batch_row(shift_ref)).astype(BF16)
        p = jnp.dot(h, w_ref[...], preferred_element_type=F32)
        for s in range(SEGS_PER_TILE // halves):
            for hd in range(RNN_HEADS):
                xr_rows[pl.ds(hd * head_rows + first + s, seg_len, stride=SEGS_PER_TILE), :] = (
                    p[s * seg_len:(s + 1) * seg_len, hd * LANES:(hd + 1) * LANES])


def _odd_in_proj(x, mod, w_in):
    batch, seq, _ = x.shape
    seg_len = seq // N_SEG
    tile_rows = SEGS_PER_TILE * seg_len
    out_spec = pl.BlockSpec((None, RNN_HEADS, seg_len, SEGS_PER_TILE, RNN_HEAD_DIM), lambda b, i: (b, 0, 0, i, 0))
    out_shape = jax.ShapeDtypeStruct((batch, RNN_HEADS, seg_len, N_SEG, RNN_HEAD_DIM), F32)
    return pl.pallas_call(
        _odd_in_kernel,
        grid=(batch, seq // tile_rows),
        in_specs=[pl.BlockSpec((None, tile_rows, D_MODEL), lambda b, i: (b, i, 0)),
                  _mod_spec(mod, ODD_LAYER, MOD_SHIFT), _mod_spec(mod, ODD_LAYER, MOD_SCALE),
                  _resident((D_MODEL, RNN_WIDTH), col_block=0)],
        out_specs=out_spec,
        out_shape=out_shape,
        scratch_shapes=[pltpu.VMEM((D_MODEL, RNN_WIDTH), BF16)],
        compiler_params=_params("arbitrary", "arbitrary"),
        name="odd_in_proj",
    )(x, mod, mod, w_in)


def _rglru_kernel(x_ref, convw_ref, convb_ref, waug_ref, lam_ref, o_ref,
                  halo, a_f, b_f, a_b, b_b, edge):
    n_rows = x_ref.shape[0]
    seg_len = n_rows // N_SEG
    n_chunks = n_rows // SCAN_CHUNK
    left_rows = CONV_LEFT * N_SEG
    right_rows = (CONV_WIDTH - 1 - CONV_LEFT) * N_SEG
    seg = lax.broadcasted_iota(jnp.int32, (N_SEG, LANES), 0)

    def from_prev_segment(blk):
        return jnp.where(seg == 0, 0.0, pltpu.roll(blk, 1, 0))

    def from_next_segment(blk):
        return jnp.where(seg == N_SEG - 1, 0.0, pltpu.roll(blk, N_SEG - 1, 0))

    for t in range(CONV_LEFT):
        halo[0, pl.ds(t * N_SEG, N_SEG), :] = from_prev_segment(
            x_ref[pl.ds((seg_len - CONV_LEFT + t) * N_SEG, N_SEG), :])
    halo[0, pl.ds(left_rows, SCAN_CHUNK + right_rows), :] = x_ref[pl.ds(0, SCAN_CHUNK + right_rows), :]
    halo[1, pl.ds(0, left_rows + SCAN_CHUNK), :] = x_ref[pl.ds(n_rows - SCAN_CHUNK - left_rows, left_rows + SCAN_CHUNK), :]
    for t in range(CONV_WIDTH - 1 - CONV_LEFT):
        halo[1, pl.ds(left_rows + SCAN_CHUNK + t * N_SEG, N_SEG), :] = from_next_segment(
            x_ref[pl.ds(t * N_SEG, N_SEG), :])

    lam = lam_ref[...]
    softplus_neg = jnp.maximum(-lam, 0.0) + jnp.log1p(jnp.exp(-jnp.abs(lam)))
    half_rate = (0.5 * RG_LRU_C) * softplus_neg
    bias_lanes = jnp.broadcast_to(
        jnp.where(lax.broadcasted_iota(jnp.int32, (1, LANES), 1) < BIAS_ROWS, 1.0, 0.0).astype(BF16),
        (SCAN_CHUNK, LANES))

    def gates(base, tap):
        half_x = convb_ref[...] + sum(convw_ref[k:k + 1, :] * tap(k) for k in range(CONV_WIDTH))
        th = jnp.tanh(jnp.dot(jnp.concatenate([half_x.astype(BF16), bias_lanes], axis=1), waug_ref[...],
                              preferred_element_type=F32))
        for d, (a_ref, b_ref) in enumerate(((a_f, b_f), (a_b, b_b))):
            o = 2 * d * LANES
            neg_log_a = (th[:, o:o + LANES] + 1.0) * half_rate[d:d + 1, :]
            a = jnp.exp2(neg_log_a * (-LOG2_E))
            one_minus_a2 = jnp.tanh(neg_log_a) * (a * a + 1.0)
            root = one_minus_a2 * lax.rsqrt(jnp.maximum(one_minus_a2, F32_TINY))
            a_ref[pl.ds(base, SCAN_CHUNK), :] = a
            b_ref[pl.ds(base, SCAN_CHUNK), :] = root * ((th[:, o + LANES:o + 2 * LANES] + 1.0) * half_x)

    def inner_chunk(c, carry):
        base = pl.multiple_of(c * SCAN_CHUNK, SCAN_CHUNK)
        gates(base, lambda k: x_ref[pl.ds(base + (k - CONV_LEFT) * N_SEG, SCAN_CHUNK), :])
        return carry

    gates(0, lambda k: halo[0, pl.ds(k * N_SEG, SCAN_CHUNK), :])
    gates(n_rows - SCAN_CHUNK, lambda k: halo[1, pl.ds(k * N_SEG, SCAN_CHUNK), :])
    lax.fori_loop(1, n_chunks - 1, inner_chunk, 0, unroll=7)

    def scan_step(t, carry):
        h_f, p_f, h_b, p_b = carry
        rf = pl.ds(pl.multiple_of(t * N_SEG, N_SEG), N_SEG)
        rb = pl.ds(pl.multiple_of((seg_len - 1 - t) * N_SEG, N_SEG), N_SEG)
        a = a_f[rf, :]
        h_f = a * h_f + b_f[rf, :]
        p_f = a * p_f
        b_f[rf, :] = h_f
        a_f[rf, :] = p_f
        a = a_b[rb, :]
        h_b = a * h_b + b_b[rb, :]
        p_b = a * p_b
        b_b[rb, :] = h_b
        a_b[rb, :] = p_b
        return h_f, p_f, h_b, p_b

    zeros = jnp.zeros((N_SEG, LANES), F32)
    ones = jnp.ones((N_SEG, LANES), F32)
    h_f, p_f, h_b, p_b = lax.fori_loop(0, seg_len, scan_step, (zeros, ones, zeros, ones), unroll=4)

    edge[0] = h_f
    edge[1] = p_f
    edge[2] = h_b
    edge[3] = p_b
    enter_f = enter_b = jnp.zeros((1, LANES), F32)
    edge[4, 0:1, :] = enter_f
    edge[5, N_SEG - 1:N_SEG, :] = enter_b
    for s in range(1, N_SEG):
        enter_f = edge[0, s - 1:s, :] + edge[1, s - 1:s, :] * enter_f
        edge[4, s:s + 1, :] = enter_f
        r = N_SEG - 1 - s
        enter_b = edge[2, r + 1:r + 2, :] + edge[3, r + 1:r + 2, :] * enter_b
        edge[5, r:r + 1, :] = enter_b
    blocks = SCAN_CHUNK // N_SEG
    in_f = jnp.concatenate([edge[4]] * blocks, axis=0)
    in_b = jnp.concatenate([edge[5]] * blocks, axis=0)

    def out_chunk(c, carry):
        rows = pl.ds(pl.multiple_of(c * SCAN_CHUNK, SCAN_CHUNK), SCAN_CHUNK)
        o_ref[rows, :] = (b_f[rows, :] + a_f[rows, :] * in_f) + (b_b[rows, :] + a_b[rows, :] * in_b)
        return carry

    lax.fori_loop(0, n_rows // SCAN_CHUNK, out_chunk, 0, unroll=2)


def _rglru(xr, conv_w, conv_b, w_a, b_a, w_x, b_x, lam):
    batch, _, n_rows, _ = xr.shape
    w_cat = jnp.concatenate([w_a[0], w_x[0], w_a[1], w_x[1]], axis=-1).astype(BF16)
    per_head = lambda v: v.reshape(RNN_HEADS, 1, RNN_HEAD_DIM)
    half_b = 0.5 * jnp.concatenate([per_head(b_a[0]), per_head(b_x[0]), per_head(b_a[1]), per_head(b_x[1])],
                                   axis=-1)
    b_hi = half_b.astype(BF16)
    b_lo = (half_b - b_hi.astype(F32)).astype(BF16)
    w_aug = jnp.concatenate([w_cat, b_hi, b_lo,
                             jnp.zeros((RNN_HEADS, RNN_HEAD_DIM - BIAS_ROWS, 4 * RNN_HEAD_DIM), BF16)], axis=1)
    col = lambda rows: pl.BlockSpec((rows, RNN_HEAD_DIM), lambda b, h: (0, h))
    seq_col = pl.BlockSpec((None, None, n_rows, RNN_HEAD_DIM), lambda b, h: (b, h, 0, 0))
    halo_rows = SCAN_CHUNK + (CONV_WIDTH - 1) * N_SEG
    return pl.pallas_call(
        _rglru_kernel,
        grid=(batch, RNN_HEADS),
        in_specs=[seq_col, col(CONV_WIDTH), col(1),
                  pl.BlockSpec((None, 2 * RNN_HEAD_DIM, 4 * RNN_HEAD_DIM), lambda b, h: (h, 0, 0)),
                  col(2)],
        out_specs=seq_col,
        out_shape=jax.ShapeDtypeStruct((batch, RNN_HEADS, n_rows, RNN_HEAD_DIM), F32),
        scratch_shapes=[pltpu.VMEM((2, halo_rows, LANES), F32)]
                       + [pltpu.VMEM((n_rows, LANES), F32)] * 4
                       + [pltpu.VMEM((6, N_SEG, LANES), F32)],
        compiler_params=_params("parallel", "parallel"),
        name="rglru",
    )(xr, 0.5 * conv_w, 0.5 * conv_b.reshape(1, -1), w_aug, lam)


def _odd_out_kernel(y_ref, x_ref, shift_ref, scale_ref, gate_ref, wg_f32_ref, w_f32_ref, lng_ref, lnb_ref,
                    o_ref, wg_ref, w_ref):
    _cast_weight_once(wg_f32_ref, wg_ref)
    _cast_weight_once(w_f32_ref, w_ref)
    seg_len = y_ref.shape[1]
    head_rows = seg_len * SEGS_PER_TILE
    y_rows = y_ref.reshape(RNN_HEADS * head_rows, LANES)
    halves = 1
    for r in range(halves):
        first = r * (SEGS_PER_TILE // halves)
        rows = slice(first * seg_len, (first + SEGS_PER_TILE // halves) * seg_len)
        x = x_ref[rows, :]
        h = (x * (1.0 + _batch_row(scale_ref)) + _batch_row(shift_ref)).astype(BF16)
        g_all = jnp.dot(h, wg_ref[...], preferred_element_type=F32)
        gated = []
        for s in range(SEGS_PER_TILE // halves):
            y = jnp.concatenate([y_rows[pl.ds(hd * head_rows + first + s, seg_len, stride=SEGS_PER_TILE), :]
                                 for hd in range(RNN_HEADS)], axis=1)
            g = g_all[s * seg_len:(s + 1) * seg_len, :]
            gated.append((y * (g * _sigmoid(g))).astype(BF16))
        out = jnp.dot(jnp.concatenate(gated, axis=0), w_ref[...], preferred_element_type=F32)
        o_ref[rows, :] = _deepnorm_rows(x, _batch_row(gate_ref), out,
                                        lng_ref[ODD_LAYER:ODD_LAYER + 1, :], lnb_ref[ODD_LAYER:ODD_LAYER + 1, :])


def _odd_out_proj(y, x, mod, w_in, w_out, ln_g, ln_b):
    batch, seq, _ = x.shape
    seg_len = seq // N_SEG
    tile_rows = SEGS_PER_TILE * seg_len
    const = lambda shape: pl.BlockSpec(shape, lambda b, i: (0,) * len(shape))
    return pl.pallas_call(
        _odd_out_kernel,
        grid=(batch, seq // tile_rows),
        in_specs=[pl.BlockSpec((None, RNN_HEADS, seg_len, SEGS_PER_TILE, RNN_HEAD_DIM),
                               lambda b, i: (b, 0, 0, i, 0)),
                  pl.BlockSpec((None, tile_rows, D_MODEL), lambda b, i: (b, i, 0)),
                  _mod_spec(mod, ODD_LAYER, MOD_SHIFT), _mod_spec(mod, ODD_LAYER, MOD_SCALE),
                  _mod_spec(mod, ODD_LAYER, MOD_GATE),
                  _resident((D_MODEL, RNN_WIDTH), col_block=1), _resident((RNN_WIDTH, D_MODEL)),
                  const(ln_g.shape), const(ln_b.shape)],
        out_specs=pl.BlockSpec((None, tile_rows, D_MODEL), lambda b, i: (b, i, 0)),
        out_shape=jax.ShapeDtypeStruct((batch, seq, D_MODEL), F32),
        scratch_shapes=[pltpu.VMEM((D_MODEL, RNN_WIDTH), BF16), pltpu.VMEM((RNN_WIDTH, D_MODEL), BF16)],
        compiler_params=_params("arbitrary", "arbitrary"),
        name="odd_out_proj",
    )(y, x, mod, mod, mod, w_in, w_out, ln_g, ln_b)


def kernel(x, c, positions, ada_w, ada_b, ln_g, ln_b, ev_w_in, ev_w_out, ev_sink, ev_sg_ln_g, ev_sg_ln_b,
           ev_sg_w, ev_sg_b, od_w_in, od_conv_w, od_conv_b, od_w_a, od_b_a, od_w_x, od_b_x, od_lam, od_w_out):
    batch, seq, d_model = x.shape
    seg_len = seq // N_SEG
    assert d_model == D_MODEL and seq % ROW_TILE == 0 and seq % MIX_TILE == 0 and seq == N_SEG * seg_len
    assert N_SEG % SEGS_PER_TILE == 0 and seq % SCAN_CHUNK == 0 and SCAN_CHUNK % N_SEG == 0
    assert ada_w.shape[0] == DEPTH == 2

    mod = _modulation(c, ada_w, ada_b)

    q, k, v, su, sv, g = _even_in_proj(x, mod, positions, ev_w_in[0])
    x1 = _even_mixer(q, k, v, su, sv, g, x, mod, ev_sink[0], ev_sg_ln_g[0], ev_sg_ln_b[0],
                     ev_sg_w[0], ev_sg_b[0], ev_w_out[0], ln_g, ln_b)

    xr = _odd_in_proj(x1, mod, od_w_in[0])
    y = _rglru(xr.reshape(batch, RNN_HEADS, seq, RNN_HEAD_DIM), od_conv_w[0], od_conv_b[0], od_w_a[0],
               od_b_a[0], od_w_x[0], od_b_x[0], od_lam[0])
    return _odd_out_proj(y.reshape(batch, RNN_HEADS, seg_len, N_SEG, RNN_HEAD_DIM), x1, mod, od_w_in[0],
                         od_w_out[0], ln_g, ln_b)
```

```python
import math

import numpy as np
import jax
import jax.numpy as jnp
from jax import lax
from jax.experimental import pallas as pl
from jax.experimental.pallas import tpu as pltpu

D_MODEL = 1024
DEPTH = 2
HEAD_DIM = 64
N_Q_HEADS = 8
N_KV_HEADS = 2
Q_PER_KV = N_Q_HEADS // N_KV_HEADS
ATTN_WIDTH = N_Q_HEADS * HEAD_DIM
KV_WIDTH = N_KV_HEADS * HEAD_DIM
ATTN_BLOCK = 128
ROPE_DIM = HEAD_DIM // 4
ROPE_HALF = ROPE_DIM // 2
ROPE_THETA = 500000.0
N_SG_GROUPS = 8
SG_GROUP_DIM = 64
SG_WIDTH = N_SG_GROUPS * SG_GROUP_DIM
SG_CHUNK = 128
EVEN_GATE_WIDTH = ATTN_WIDTH + SG_WIDTH
EVEN_IN_WIDTH = ATTN_WIDTH + 2 * KV_WIDTH + 2 * SG_WIDTH + EVEN_GATE_WIDTH
RNN_WIDTH = D_MODEL
RNN_HEADS = 8
RNN_HEAD_DIM = RNN_WIDTH // RNN_HEADS
CONV_WIDTH = 4
CONV_LEFT = 2
RG_LRU_C = 8.0
DEEPNORM_ALPHA = (2 * DEPTH) ** 0.25
LN_EPS = 1e-5
NEG_INF = -1e30
LOG2_E = math.log2(math.e)
F32_TINY = float(np.finfo(np.float32).tiny)

LANES = 128
SUBLANES = 8
VMEM_LIMIT = 56 * 1024 * 1024

ROW_TILE = 512
MIX_TILE = 1024
N_SEG = 32
SEGS_PER_TILE = SUBLANES
SCAN_CHUNK = 256
WEIGHT_CAST_ROWS = 128
BIAS_ROWS = 2

BF16 = jnp.bfloat16
F32 = jnp.float32


def _sigmoid(x):
    return 0.5 * jnp.tanh(0.5 * x) + 0.5


def _layer_norm_rows(z, g, b, eps=LN_EPS):
    mu = jnp.mean(z, axis=-1, keepdims=True)
    d = z - mu
    var = jnp.mean(d * d, axis=-1, keepdims=True)
    return d * lax.rsqrt(var + eps) * g + b


def _deepnorm_rows(x, gate, y, g, b):
    return _layer_norm_rows(x + (gate * (1.0 / DEEPNORM_ALPHA)) * y, g, b, LN_EPS / DEEPNORM_ALPHA ** 2)


def _params(*semantics):
    return pltpu.CompilerParams(dimension_semantics=semantics, vmem_limit_bytes=VMEM_LIMIT)


def _cast_weight_once(w_ref, w_bf16):
    @pl.when((pl.program_id(0) == 0) & (pl.program_id(1) == 0))
    def _():
        for r in range(0, w_ref.shape[0], WEIGHT_CAST_ROWS):
            w_bf16[r:r + WEIGHT_CAST_ROWS, :] = w_ref[r:r + WEIGHT_CAST_ROWS, :].astype(BF16)


def _resident(shape, col_block=0):
    return pl.BlockSpec(shape, lambda b, i: (0,) * (len(shape) - 1) + (col_block,), pipeline_mode=pl.Buffered(1))


def _mod_kernel(c_ref, w_ref, b_ref, o_ref):
    c = c_ref[...]
    cond = c * _sigmoid(c)
    pad_rows = o_ref.shape[0] - cond.shape[0]
    if pad_rows:
        cond = jnp.concatenate([cond, jnp.zeros((pad_rows, cond.shape[1]), F32)], axis=0)
    bias = b_ref[pl.ds(pl.program_id(0), 1), :]
    o_ref[...] = jnp.dot(cond.astype(BF16), w_ref[...].astype(BF16), preferred_element_type=F32) + bias


def _modulation(c, ada_w, ada_b):
    batch = c.shape[0]
    rows = -(-batch // SUBLANES) * SUBLANES
    n_col = 3 * D_MODEL // D_MODEL
    return pl.pallas_call(
        _mod_kernel,
        grid=(DEPTH, n_col),
        in_specs=[
            pl.BlockSpec((batch, D_MODEL), lambda l, j: (0, 0)),
            pl.BlockSpec((None, D_MODEL, D_MODEL), lambda l, j: (l, 0, j)),
            pl.BlockSpec((DEPTH, D_MODEL), lambda l, j: (0, j)),
        ],
        out_specs=pl.BlockSpec((None, rows, D_MODEL), lambda l, j: (l, 0, j)),
        out_shape=jax.ShapeDtypeStruct((DEPTH, rows, 3 * D_MODEL), F32),
        compiler_params=_params("parallel", "parallel"),
        name="adaln_mod",
    )(c, ada_w, ada_b)


MOD_SHIFT, MOD_SCALE, MOD_GATE = range(3)
EVEN_LAYER, ODD_LAYER = 0, 1


def _mod_spec(mod, layer, part):
    return pl.BlockSpec((None, mod.shape[1], D_MODEL), lambda b, i: (layer, 0, part))


def _batch_row(ref):
    return ref[pl.ds(pl.program_id(0), 1), :]


def _even_in_kernel(x_ref, shift_ref, scale_ref, pos_ref, freq_ref, sign_ref, w_f32_ref,
                    q_ref, k_ref, v_ref, su_ref, sv_ref, g_ref, w_ref):
    _cast_weight_once(w_f32_ref, w_ref)
    lane = lax.broadcasted_iota(jnp.int32, (1, LANES), 1)
    first_half = (lane % HEAD_DIM) < ROPE_HALF
    left_head = lane < HEAD_DIM
    diagonal = lax.broadcasted_iota(jnp.int32, (LANES, LANES), 0) == lax.broadcasted_iota(jnp.int32, (LANES, LANES), 1)
    batch_ids = lax.broadcasted_iota(jnp.int32, (pos_ref.shape[0], LANES), 0)
    q_scale = HEAD_DIM ** -0.5 * LOG2_E

    half_rows = x_ref.shape[0] // 2
    for r in range(2):
        rows = slice(r * half_rows, (r + 1) * half_rows)
        h = (x_ref[rows, :] * (1.0 + _batch_row(scale_ref)) + _batch_row(shift_ref)).astype(BF16)
        p = jnp.dot(h, w_ref[...], preferred_element_type=F32)

        pos_cols = []
        for j in range(r * half_rows // LANES, (r + 1) * half_rows // LANES):
            pos_all = pos_ref[:, j * LANES:(j + 1) * LANES].astype(F32)
            pos_row = jnp.sum(jnp.where(batch_ids == pl.program_id(0), pos_all, 0.0), axis=0, keepdims=True)
            pos_cols.append(jnp.sum(jnp.where(diagonal, pos_row, 0.0), axis=1, keepdims=True))
        ang = jnp.concatenate(pos_cols, axis=0) * freq_ref[...]
        cs = jnp.cos(ang)
        sn = jnp.sin(ang) * sign_ref[...]

        def rotary(t):
            partner = jnp.where(first_half, pltpu.roll(t, LANES - ROPE_HALF, 1), pltpu.roll(t, ROPE_HALF, 1))
            return t * cs + partner * sn

        for j in range(ATTN_WIDTH // LANES):
            q_ref[rows, j * LANES:(j + 1) * LANES] = (
                rotary(p[:, j * LANES:(j + 1) * LANES]) * q_scale).astype(BF16)
        o = ATTN_WIDTH
        k = rotary(p[:, o:o + KV_WIDTH])
        k_sw = pltpu.roll(k, HEAD_DIM, 1)
        k_ref[rows, 0:LANES] = jnp.where(left_head, k, k_sw).astype(BF16)
        k_ref[rows, LANES:2 * LANES] = jnp.where(left_head, k_sw, k).astype(BF16)
        o += KV_WIDTH
        v = p[:, o:o + KV_WIDTH]
        v_sw = pltpu.roll(v, HEAD_DIM, 1)
        v_ref[rows, 0 * LANES:1 * LANES] = jnp.where(left_head, v, 0.0).astype(BF16)
        v_ref[rows, 1 * LANES:2 * LANES] = jnp.where(left_head, 0.0, v_sw).astype(BF16)
        v_ref[rows, 2 * LANES:3 * LANES] = jnp.where(left_head, v_sw, 0.0).astype(BF16)
        v_ref[rows, 3 * LANES:4 * LANES] = jnp.where(left_head, 0.0, v).astype(BF16)
        o += KV_WIDTH
        su_ref[rows, :] = p[:, o:o + SG_WIDTH].astype(BF16)
        o += SG_WIDTH
        sv_ref[rows, :] = p[:, o:o + SG_WIDTH].astype(BF16)
        o += SG_WIDTH
        g_ref[rows, :] = p[:, o:o + EVEN_GATE_WIDTH].astype(BF16)


def _even_in_proj(x, mod, positions, w_in):
    batch, seq, _ = x.shape
    d = np.arange(LANES) % HEAD_DIM
    inv_freq = np.power(np.float64(ROPE_THETA), -np.arange(ROPE_HALF) / ROPE_HALF)
    freq_lane = jnp.asarray(np.where(d < ROPE_DIM, inv_freq[d % ROPE_HALF], 0.0), dtype=F32).reshape(1, LANES)
    sign_lane = jnp.asarray(np.where(d < ROPE_HALF, -1.0, np.where(d < ROPE_DIM, 1.0, 0.0)),
                            dtype=F32).reshape(1, LANES)
    row = lambda width: pl.BlockSpec((None, ROW_TILE, width), lambda b, i: (b, i, 0))
    const = lambda shape: pl.BlockSpec(shape, lambda b, i: (0,) * len(shape))
    widths = (ATTN_WIDTH, 2 * KV_WIDTH, 4 * KV_WIDTH, SG_WIDTH, SG_WIDTH, EVEN_GATE_WIDTH)
    return pl.pallas_call(
        _even_in_kernel,
        grid=(batch, seq // ROW_TILE),
        in_specs=[row(D_MODEL), _mod_spec(mod, EVEN_LAYER, MOD_SHIFT), _mod_spec(mod, EVEN_LAYER, MOD_SCALE),
                  pl.BlockSpec((batch, ROW_TILE), lambda b, i: (0, i)),
                  const((1, LANES)), const((1, LANES)), _resident((D_MODEL, EVEN_IN_WIDTH))],
        out_specs=[row(w) for w in widths],
        out_shape=[jax.ShapeDtypeStruct((batch, seq, w), BF16) for w in widths],
        scratch_shapes=[pltpu.VMEM((D_MODEL, EVEN_IN_WIDTH), BF16)],
        compiler_params=_params("arbitrary", "arbitrary"),
        name="even_in_proj",
    )(x, mod, mod, positions, freq_lane, sign_lane, w_in)


def _even_mix_kernel(sink_ref, q_ref, k_ref, v_ref, su_ref, sv_ref, g_ref, x_ref, gate_ref,
                     avg_ref, avg2_ref, sgg_ref, sgb_ref, sgw_ref, sgbias_ref, wout_f32_ref, lng_ref, lnb_ref,
                     o_ref, y_scr, wout_ref):
    _cast_weight_once(wout_f32_ref, wout_ref)
    seq = k_ref.shape[0]
    n_blocks = seq // ATTN_BLOCK
    sub_blocks = MIX_TILE // ATTN_BLOCK
    tile = pl.program_id(1)
    lane = lax.broadcasted_iota(jnp.int32, (1, LANES), 1)
    left = lane < HEAD_DIM

    qi = lax.broadcasted_iota(jnp.int32, (ATTN_BLOCK, ATTN_BLOCK), 0)
    kj = lax.broadcasted_iota(jnp.int32, (ATTN_BLOCK, ATTN_BLOCK), 1)
    ones_left = jnp.broadcast_to(jnp.where(left, 1.0, 0.0).astype(BF16), (3 * ATTN_BLOCK, LANES))
    ones_right = jnp.broadcast_to(jnp.where(left, 0.0, 1.0).astype(BF16), (3 * ATTN_BLOCK, LANES))
    avg = avg_ref[...]
    sink_cols = [jnp.concatenate([jnp.full((ATTN_BLOCK, LANES), sink_ref[h] * LOG2_E, F32)
                                  for h in (4 * hk, 4 * hk + 2, 4 * hk + 1, 4 * hk + 3)], axis=0)
                 for hk in range(N_KV_HEADS)]

    def sub_block(j, carry):
        blk = tile * sub_blocks + j
        prev = jnp.maximum(blk - 1, 0)
        nxt = jnp.minimum(blk + 1, n_blocks - 1)
        rows = pl.ds(pl.multiple_of(j * ATTN_BLOCK, ATTN_BLOCK), ATTN_BLOCK)
        bias_prev = jnp.where((kj >= qi) & (blk > 0), 0.0, NEG_INF)
        bias_next = jnp.where((kj <= qi) & (blk < n_blocks - 1), 0.0, NEG_INF)
        bias_prev = jnp.concatenate([bias_prev] * Q_PER_KV, axis=0)
        bias_next = jnp.concatenate([bias_next] * Q_PER_KV, axis=0)

        def band(ref):
            return jnp.concatenate(
                [ref[pl.ds(pl.multiple_of(b * ATTN_BLOCK, ATTN_BLOCK), ATTN_BLOCK), :] for b in (prev, blk, nxt)],
                axis=0)

        k_band = band(k_ref)
        v_band = band(v_ref)
        n_sg = SG_WIDTH // LANES
        sg_cols = [slice(c * LANES, (c + 1) * LANES) for c in range(n_sg)]

        def silu_gate(cols):
            g_c = g_ref[rows, cols].astype(F32)
            return g_c * _sigmoid(g_c)

        scores = []
        for hk in range(N_KV_HEADS):
            q_cols = [q_ref[rows, (2 * hk + c) * LANES:(2 * hk + c + 1) * LANES] for c in range(2)]
            zero = jnp.zeros_like(q_cols[0])
            lhs = jnp.concatenate([jnp.where(left, qc, zero) for qc in q_cols]
                                  + [jnp.where(left, zero, qc) for qc in q_cols], axis=0)
            scores.append(lax.dot_general(lhs, k_band[:, hk * LANES:(hk + 1) * LANES], (((1,), (1,)), ((), ())),
                                          preferred_element_type=F32))
        sv_cols = [sv_ref[rows, cols] for cols in sg_cols]
        means = [jnp.dot(v_c, avg, preferred_element_type=F32) for v_c in sv_cols]

        probs, row_max = [], []
        for hk in range(N_KV_HEADS):
            s = scores[hk]
            s0 = s[:, 0:ATTN_BLOCK] + bias_prev
            s1 = s[:, ATTN_BLOCK:2 * ATTN_BLOCK]
            s2 = s[:, 2 * ATTN_BLOCK:3 * ATTN_BLOCK] + bias_next
            m = jnp.max(jnp.maximum(jnp.maximum(s0, s1), s2), axis=-1, keepdims=True)
            m = jnp.maximum(jnp.broadcast_to(m, sink_cols[hk].shape), sink_cols[hk])
            probs.append(jnp.concatenate([jnp.exp2(t - m).astype(BF16) for t in (s0, s1, s2)], axis=1))
            row_max.append(m)
        devs, sq_parts = [], []
        for v_c, mean in zip(sv_cols, means):
            dev = v_c.astype(F32) - mean
            sq = dev * dev
            sq_hi = sq.astype(BF16)
            sq_lo = (sq - sq_hi.astype(F32)).astype(BF16)
            devs.append(dev)
            sq_parts.append(jnp.concatenate([sq_hi, sq_lo], axis=1))

        pv = []
        for hk in range(N_KV_HEADS):
            p = probs[hk]
            lhs = jnp.concatenate(
                [jnp.concatenate([p[c * ATTN_BLOCK:(c + 1) * ATTN_BLOCK], p[(2 + c) * ATTN_BLOCK:(3 + c) * ATTN_BLOCK]],
                                 axis=1) for c in range(2)], axis=0)
            rhs = jnp.concatenate(
                [jnp.concatenate([v_band[:, (2 * hk) * LANES:(2 * hk + 1) * LANES], ones_left], axis=1),
                 jnp.concatenate([v_band[:, (2 * hk + 1) * LANES:(2 * hk + 2) * LANES], ones_right], axis=1)],
                axis=0)
            pv.append(jnp.dot(lhs, rhs, preferred_element_type=F32))
        variances = [jnp.dot(sq, avg2_ref[...], preferred_element_type=F32) for sq in sq_parts]

        for hk in range(N_KV_HEADS):
            for c in range(2):
                even, odd = slice(c * ATTN_BLOCK, (c + 1) * ATTN_BLOCK), slice((2 + c) * ATTN_BLOCK, (3 + c) * ATTN_BLOCK)
                sink_term = jnp.exp2(jnp.where(left, sink_cols[hk][even] - row_max[hk][even],
                                               sink_cols[hk][odd] - row_max[hk][odd]))
                o = pv[hk][c * ATTN_BLOCK:(c + 1) * ATTN_BLOCK]
                pair = o[:, 0:LANES] * (1.0 / (o[:, LANES:2 * LANES] + sink_term))
                cols = slice((2 * hk + c) * LANES, (2 * hk + c + 1) * LANES)
                y_scr[rows, cols] = (pair * silu_gate(cols)).astype(BF16)

        for c, cols in enumerate(sg_cols):
            vn = devs[c] * lax.rsqrt(variances[c] + LN_EPS) * sgg_ref[:, cols] + sgb_ref[:, cols]
            stacked = jnp.concatenate([jnp.where(left, vn, 0.0), jnp.where(left, 0.0, vn)],
                                      axis=0).astype(BF16)
            mixed = jnp.dot(sgw_ref[c], stacked, preferred_element_type=F32) + sgbias_ref[:, cols]
            y_sg = su_ref[rows, cols].astype(F32) * mixed
            gc = slice(ATTN_WIDTH + c * LANES, ATTN_WIDTH + (c + 1) * LANES)
            y_scr[rows, gc] = (y_sg * silu_gate(gc)).astype(BF16)
        return carry

    lax.fori_loop(0, sub_blocks, sub_block, 0, unroll=True)

    for r in range(2):
        half_rows = slice(r * (MIX_TILE // 2), (r + 1) * (MIX_TILE // 2))
        out = jnp.dot(y_scr[half_rows, :], wout_ref[...], preferred_element_type=F32)
        o_ref[half_rows, :] = _deepnorm_rows(x_ref[half_rows, :], _batch_row(gate_ref), out,
                                             lng_ref[EVEN_LAYER:EVEN_LAYER + 1, :], lnb_ref[EVEN_LAYER:EVEN_LAYER + 1, :])


def _even_mixer(q, k, v, su, sv, g, x, mod, sink, sg_ln_g, sg_ln_b, sg_w, sg_b, w_out, ln_g, ln_b):
    batch, seq, _ = x.shape
    lane = np.arange(LANES)
    avg = jnp.asarray((lane[:, None] // SG_GROUP_DIM == lane[None, :] // SG_GROUP_DIM) / SG_GROUP_DIM, dtype=BF16)
    w_pairs = sg_w.reshape(N_SG_GROUPS // 2, 2, SG_CHUNK, SG_CHUNK).transpose(0, 2, 1, 3)
    w_pairs = w_pairs.reshape(N_SG_GROUPS // 2, SG_CHUNK, 2 * SG_CHUNK).astype(BF16)
    bias_full = jnp.repeat(sg_b.T, SG_GROUP_DIM, axis=1)
    row = lambda width: pl.BlockSpec((None, MIX_TILE, width), lambda b, i: (b, i, 0))
    full_seq = lambda arr: pl.BlockSpec((None, seq, arr.shape[-1]), lambda b, i: (b, 0, 0))
    const = lambda shape: pl.BlockSpec(shape, lambda b, i: (0,) * len(shape))
    return pl.pallas_call(
        _even_mix_kernel,
        grid=(batch, seq // MIX_TILE),
        in_specs=[
            pl.BlockSpec(memory_space=pltpu.SMEM),
            row(ATTN_WIDTH), full_seq(k), full_seq(v), row(SG_WIDTH), row(SG_WIDTH), row(EVEN_GATE_WIDTH),
            row(D_MODEL), _mod_spec(mod, EVEN_LAYER, MOD_GATE),
            const((LANES, LANES)), const((2 * LANES, LANES)), const((1, SG_WIDTH)), const((1, SG_WIDTH)),
            const((N_SG_GROUPS // 2, SG_CHUNK, 2 * SG_CHUNK)), const((SG_CHUNK, SG_WIDTH)),
            _resident((EVEN_GATE_WIDTH, D_MODEL)), const(ln_g.shape), const(ln_b.shape),
        ],
        out_specs=row(D_MODEL),
        out_shape=jax.ShapeDtypeStruct((batch, seq, D_MODEL), F32),
        scratch_shapes=[pltpu.VMEM((MIX_TILE, EVEN_GATE_WIDTH), BF16),
                        pltpu.VMEM((EVEN_GATE_WIDTH, D_MODEL), BF16)],
        compiler_params=_params("arbitrary", "arbitrary"),
        name="even_mixer",
    )(sink, q, k, v, su, sv, g, x, mod, avg, jnp.concatenate([avg, avg], axis=0), sg_ln_g.reshape(1, -1), sg_ln_b.reshape(1, -1),
      w_pairs, bias_full, w_out, ln_g, ln_b)


def _odd_in_kernel(x_ref, shift_ref, scale_ref, w_f32_ref, xr_ref, w_ref):
    _cast_weight_once(w_f32_ref, w_ref)
    seg_len = xr_ref.shape[1]
    head_rows = seg_len * SEGS_PER_TILE
    xr_rows = xr_ref.reshape(RNN_HEADS * head_rows, LANES)
    halves = 2
    for r in range(halves):
        first = r * (SEGS_PER_TILE // halves)
        rows = slice(first * seg_len, (first + SEGS_PER_TILE // halves) * seg_len)
        h = (x_ref[rows, :] * (1.0 + _batch_row(scale_ref)) + _batch_row(shift_ref)).astype(BF16)
        p = jnp.dot(h, w_ref[...], preferred_element_type=F32)
        for s in range(SEGS_PER_TILE // halves):
            for hd in range(RNN_HEADS):
                xr_rows[pl.ds(hd * head_rows + first + s, seg_len, stride=SEGS_PER_TILE), :] = (
                    p[s * seg_len:(s + 1) * seg_len, hd * LANES:(hd + 1) * LANES])


def _odd_in_proj(x, mod, w_in):
    batch, seq, _ = x.shape
    seg_len = seq // N_SEG
    tile_rows = SEGS_PER_TILE * seg_len
    out_spec = pl.BlockSpec((None, RNN_HEADS, seg_len, SEGS_PER_TILE, RNN_HEAD_DIM), lambda b, i: (b, 0, 0, i, 0))
    out_shape = jax.ShapeDtypeStruct((batch, RNN_HEADS, seg_len, N_SEG, RNN_HEAD_DIM), F32)
    return pl.pallas_call(
        _odd_in_kernel,
        grid=(batch, seq // tile_rows),
        in_specs=[pl.BlockSpec((None, tile_rows, D_MODEL), lambda b, i: (b, i, 0)),
                  _mod_spec(mod, ODD_LAYER, MOD_SHIFT), _mod_spec(mod, ODD_LAYER, MOD_SCALE),
                  _resident((D_MODEL, RNN_WIDTH), col_block=0)],
        out_specs=out_spec,
        out_shape=out_shape,
        scratch_shapes=[pltpu.VMEM((D_MODEL, RNN_WIDTH), BF16)],
        compiler_params=_params("arbitrary", "arbitrary"),
        name="odd_in_proj",
    )(x, mod, mod, w_in)


def _rglru_kernel(x_ref, convw_ref, convb_ref, waug_ref, lam_ref, o_ref,
                  halo, a_f, b_f, a_b, b_b, edge):
    n_rows = x_ref.shape[0]
    seg_len = n_rows // N_SEG
    n_chunks = n_rows // SCAN_CHUNK
    left_rows = CONV_LEFT * N_SEG
    right_rows = (CONV_WIDTH - 1 - CONV_LEFT) * N_SEG
    seg = lax.broadcasted_iota(jnp.int32, (N_SEG, LANES), 0)

    def from_prev_segment(blk):
        return jnp.where(seg == 0, 0.0, pltpu.roll(blk, 1, 0))

    def from_next_segment(blk):
        return jnp.where(seg == N_SEG - 1, 0.0, pltpu.roll(blk, N_SEG - 1, 0))

    for t in range(CONV_LEFT):
        halo[0, pl.ds(t * N_SEG, N_SEG), :] = from_prev_segment(
            x_ref[pl.ds((seg_len - CONV_LEFT + t) * N_SEG, N_SEG), :])
    halo[0, pl.ds(left_rows, SCAN_CHUNK + right_rows), :] = x_ref[pl.ds(0, SCAN_CHUNK + right_rows), :]
    halo[1, pl.ds(0, left_rows + SCAN_CHUNK), :] = x_ref[pl.ds(n_rows - SCAN_CHUNK - left_rows, left_rows + SCAN_CHUNK), :]
    for t in range(CONV_WIDTH - 1 - CONV_LEFT):
        halo[1, pl.ds(left_rows + SCAN_CHUNK + t * N_SEG, N_SEG), :] = from_next_segment(
            x_ref[pl.ds(t * N_SEG, N_SEG), :])

    lam = lam_ref[...]
    softplus_neg = jnp.maximum(-lam, 0.0) + jnp.log1p(jnp.exp(-jnp.abs(lam)))
    half_rate = (0.5 * RG_LRU_C) * softplus_neg
    bias_lanes = jnp.broadcast_to(
        jnp.where(lax.broadcasted_iota(jnp.int32, (1, LANES), 1) < BIAS_ROWS, 1.0, 0.0).astype(BF16),
        (SCAN_CHUNK, LANES))

    def gates(base, tap):
        half_x = convb_ref[...] + sum(convw_ref[k:k + 1, :] * tap(k) for k in range(CONV_WIDTH))
        th = jnp.tanh(jnp.dot(jnp.concatenate([half_x.astype(BF16), bias_lanes], axis=1), waug_ref[...],
                              preferred_element_type=F32))
        for d, (a_ref, b_ref) in enumerate(((a_f, b_f), (a_b, b_b))):
            o = 2 * d * LANES
            neg_log_a = (th[:, o:o + LANES] + 1.0) * half_rate[d:d + 1, :]
            a = jnp.exp2(neg_log_a * (-LOG2_E))
            one_minus_a2 = jnp.tanh(neg_log_a) * (a * a + 1.0)
            root = one_minus_a2 * lax.rsqrt(jnp.maximum(one_minus_a2, F32_TINY))
            a_ref[pl.ds(base, SCAN_CHUNK), :] = a
            b_ref[pl.ds(base, SCAN_CHUNK), :] = root * ((th[:, o + LANES:o + 2 * LANES] + 1.0) * half_x)

    def inner_chunk(c, carry):
        base = pl.multiple_of(c * SCAN_CHUNK, SCAN_CHUNK)
        gates(base, lambda k: x_ref[pl.ds(base + (k - CONV_LEFT) * N_SEG, SCAN_CHUNK), :])
        return carry

    gates(0, lambda k: halo[0, pl.ds(k * N_SEG, SCAN_CHUNK), :])
    gates(n_rows - SCAN_CHUNK, lambda k: halo[1, pl.ds(k * N_SEG, SCAN_CHUNK), :])
    lax.fori_loop(1, n_chunks - 1, inner_chunk, 0, unroll=7)

    def scan_step(t, carry):
        h_f, p_f, h_b, p_b = carry
        rf = pl.ds(pl.multiple_of(t * N_SEG, N_SEG), N_SEG)
        rb = pl.ds(pl.multiple_of((seg_len - 1 - t) * N_SEG, N_SEG), N_SEG)
        a = a_f[rf, :]
        h_f = a * h_f + b_f[rf, :]
        p_f = a * p_f
        b_f[rf, :] = h_f
        a_f[rf, :] = p_f
        a = a_b[rb, :]
        h_b = a * h_b + b_b[rb, :]
        p_b = a * p_b
        b_b[rb, :] = h_b
        a_b[rb, :] = p_b
        return h_f, p_f, h_b, p_b

    zeros = jnp.zeros((N_SEG, LANES), F32)
    ones = jnp.ones((N_SEG, LANES), F32)
    h_f, p_f, h_b, p_b = lax.fori_loop(0, seg_len, scan_step, (zeros, ones, zeros, ones), unroll=4)

    edge[0] = h_f
    edge[1] = p_f
    edge[2] = h_b
    edge[3] = p_b
    enter_f = enter_b = jnp.zeros((1, LANES), F32)
    edge[4, 0:1, :] = enter_f
    edge[5, N_SEG - 1:N_SEG, :] = enter_b
    for s in range(1, N_SEG):
        enter_f = edge[0, s - 1:s, :] + edge[1, s - 1:s, :] * enter_f
        edge[4, s:s + 1, :] = enter_f
        r = N_SEG - 1 - s
        enter_b = edge[2, r + 1:r + 2, :] + edge[3, r + 1:r + 2, :] * enter_b
        edge[5, r:r + 1, :] = enter_b
    blocks = SCAN_CHUNK // N_SEG
    in_f = jnp.concatenate([edge[4]] * blocks, axis=0)
    in_b = jnp.concatenate([edge[5]] * blocks, axis=0)

    def out_chunk(c, carry):
        rows = pl.ds(pl.multiple_of(c * SCAN_CHUNK, SCAN_CHUNK), SCAN_CHUNK)
        o_ref[rows, :] = (b_f[rows, :] + a_f[rows, :] * in_f) + (b_b[rows, :] + a_b[rows, :] * in_b)
        return carry

    lax.fori_loop(0, n_rows // SCAN_CHUNK, out_chunk, 0, unroll=2)


def _rglru(xr, conv_w, conv_b, w_a, b_a, w_x, b_x, lam):
    batch, _, n_rows, _ = xr.shape
    w_cat = jnp.concatenate([w_a[0], w_x[0], w_a[1], w_x[1]], axis=-1).astype(BF16)
    per_head = lambda v: v.reshape(RNN_HEADS, 1, RNN_HEAD_DIM)
    half_b = 0.5 * jnp.concatenate([per_head(b_a[0]), per_head(b_x[0]), per_head(b_a[1]), per_head(b_x[1])],
                                   axis=-1)
    b_hi = half_b.astype(BF16)
    b_lo = (half_b - b_hi.astype(F32)).astype(BF16)
    w_aug = jnp.concatenate([w_cat, b_hi, b_lo,
                             jnp.zeros((RNN_HEADS, RNN_HEAD_DIM - BIAS_ROWS, 4 * RNN_HEAD_DIM), BF16)], axis=1)
    col = lambda rows: pl.BlockSpec((rows, RNN_HEAD_DIM), lambda b, h: (0, h))
    seq_col = pl.BlockSpec((None, None, n_rows, RNN_HEAD_DIM), lambda b, h: (b, h, 0, 0))
    halo_rows = SCAN_CHUNK + (CONV_WIDTH - 1) * N_SEG
    return pl.pallas_call(
        _rglru_kernel,
        grid=(batch, RNN_HEADS),
        in_specs=[seq_col, col(CONV_WIDTH), col(1),
                  pl.BlockSpec((None, 2 * RNN_HEAD_DIM, 4 * RNN_HEAD_DIM), lambda b, h: (h, 0, 0)),
                  col(2)],
        out_specs=seq_col,
        out_shape=jax.ShapeDtypeStruct((batch, RNN_HEADS, n_rows, RNN_HEAD_DIM), F32),
        scratch_shapes=[pltpu.VMEM((2, halo_rows, LANES), F32)]
                       + [pltpu.VMEM((n_rows, LANES), F32)] * 4
                       + [pltpu.VMEM((6, N_SEG, LANES), F32)],
        compiler_params=_params("parallel", "parallel"),
        name="rglru",
    )(xr, 0.5 * conv_w, 0.5 * conv_b.reshape(1, -1), w_aug, lam)


def _odd_out_kernel(y_ref, x_ref, shift_ref, scale_ref, gate_ref, wg_f32_ref, w_f32_ref, lng_ref, lnb_ref,
                    o_ref, wg_ref, w_ref):
    _cast_weight_once(wg_f32_ref, wg_ref)
    _cast_weight_once(w_f32_ref, w_ref)
    seg_len = y_ref.shape[1]
    head_rows = seg_len * SEGS_PER_TILE
    y_rows = y_ref.reshape(RNN_HEADS * head_rows, LANES)
    halves = 1
    for r in range(halves):
        first = r * (SEGS_PER_TILE // halves)
        rows = slice(first * seg_len, (first + SEGS_PER_TILE // halves) * seg_len)
        x = x_ref[rows, :]
        h = (x * (1.0 + _batch_row(scale_ref)) + _batch_row(shift_ref)).astype(BF16)
        g_all = jnp.dot(h, wg_ref[...], preferred_element_type=F32)
        gated = []
        for s in range(SEGS_PER_TILE // halves):
            y = jnp.concatenate([y_rows[pl.ds(hd * head_rows + first + s, seg_len, stride=SEGS_PER_TILE), :]
                                 for hd in range(RNN_HEADS)], axis=1)
            g = g_all[s * seg_len:(s + 1) * seg_len, :]
            gated.append((y * (g * _sigmoid(g))).astype(BF16))
        out = jnp.dot(jnp.concatenate(gated, axis=0), w_ref[...], preferred_element_type=F32)
        o_ref[rows, :] = _deepnorm_rows(x, _batch_row(gate_ref), out,
                                        lng_ref[ODD_LAYER:ODD_LAYER + 1, :], lnb_ref[ODD_LAYER:ODD_LAYER + 1, :])


def _odd_out_proj(y, x, mod, w_in, w_out, ln_g, ln_b):
    batch, seq, _ = x.shape
    seg_len = seq // N_SEG
    tile_rows = SEGS_PER_TILE * seg_len
    const = lambda shape: pl.BlockSpec(shape, lambda b, i: (0,) * len(shape))
    return pl.pallas_call(
        _odd_out_kernel,
        grid=(batch, seq // tile_rows),
        in_specs=[pl.BlockSpec((None, RNN_HEADS, seg_len, SEGS_PER_TILE, RNN_HEAD_DIM),
                               lambda b, i: (b, 0, 0, i, 0)),
                  pl.BlockSpec((None, tile_rows, D_MODEL), lambda b, i: (b, i, 0)),
                  _mod_spec(mod, ODD_LAYER, MOD_SHIFT), _mod_spec(mod, ODD_LAYER, MOD_SCALE),
                  _mod_spec(mod, ODD_LAYER, MOD_GATE),
                  _resident((D_MODEL, RNN_WIDTH), col_block=1), _resident((RNN_WIDTH, D_MODEL)),
                  const(ln_g.shape), const(ln_b.shape)],
        out_specs=pl.BlockSpec((None, tile_rows, D_MODEL), lambda b, i: (b, i, 0)),
        out_shape=jax.ShapeDtypeStruct((batch, seq, D_MODEL), F32),
        scratch_shapes=[pltpu.VMEM((D_MODEL, RNN_WIDTH), BF16), pltpu.VMEM((RNN_WIDTH, D_MODEL), BF16)],
        compiler_params=_params("arbitrary", "arbitrary"),
        name="odd_out_proj",
    )(y, x, mod, mod, mod, w_in, w_out, ln_g, ln_b)


def kernel(x, c, positions, ada_w, ada_b, ln_g, ln_b, ev_w_in, ev_w_out, ev_sink, ev_sg_ln_g, ev_sg_ln_b,
           ev_sg_w, ev_sg_b, od_w_in, od_conv_w, od_conv_b, od_w_a, od_b_a, od_w_x, od_b_x, od_lam, od_w_out):
    batch, seq, d_model = x.shape
    seg_len = seq // N_SEG
    assert d_model == D_MODEL and seq % ROW_TILE == 0 and seq % MIX_TILE == 0 and seq == N_SEG * seg_len
    assert N_SEG % SEGS_PER_TILE == 0 and seq % SCAN_CHUNK == 0 and SCAN_CHUNK % N_SEG == 0
    assert ada_w.shape[0] == DEPTH == 2

    mod = _modulation(c, ada_w, ada_b)

    q, k, v, su, sv, g = _even_in_proj(x, mod, positions, ev_w_in[0])
    x1 = _even_mixer(q, k, v, su, sv, g, x, mod, ev_sink[0], ev_sg_ln_g[0], ev_sg_ln_b[0],
                     ev_sg_w[0], ev_sg_b[0], ev_w_out[0], ln_g, ln_b)

    xr = _odd_in_proj(x1, mod, od_w_in[0])
    y = _rglru(xr.reshape(batch, RNN_HEADS, seq, RNN_HEAD_DIM), od_conv_w[0], od_conv_b[0], od_w_a[0],
               od_b_a[0], od_w_x[0], od_b_x[0], od_lam[0])
    return _odd_out_proj(y.reshape(batch, RNN_HEADS, seg_len, N_SEG, RNN_HEAD_DIM), x1, mod, od_w_in[0],
                         od_w_out[0], ln_g, ln_b)
```

```python
import math

import numpy as np
import jax
import jax.numpy as jnp
from jax import lax
from jax.experimental import pallas as pl
from jax.experimental.pallas import tpu as pltpu

D_MODEL = 1024
DEPTH = 2
HEAD_DIM = 64
N_Q_HEADS = 8
N_KV_HEADS = 2
Q_PER_KV = N_Q_HEADS // N_KV_HEADS
ATTN_WIDTH = N_Q_HEADS * HEAD_DIM
KV_WIDTH = N_KV_HEADS * HEAD_DIM
ATTN_BLOCK = 128
ROPE_DIM = HEAD_DIM // 4
ROPE_HALF = ROPE_DIM // 2
ROPE_THETA = 500000.0
N_SG_GROUPS = 8
SG_GROUP_DIM = 64
SG_WIDTH = N_SG_GROUPS * SG_GROUP_DIM
SG_CHUNK = 128
EVEN_GATE_WIDTH = ATTN_WIDTH + SG_WIDTH
EVEN_IN_WIDTH = ATTN_WIDTH + 2 * KV_WIDTH + 2 * SG_WIDTH + EVEN_GATE_WIDTH
RNN_WIDTH = D_MODEL
RNN_HEADS = 8
RNN_HEAD_DIM = RNN_WIDTH // RNN_HEADS
CONV_WIDTH = 4
CONV_LEFT = 2
RG_LRU_C = 8.0
DEEPNORM_ALPHA = (2 * DEPTH) ** 0.25
LN_EPS = 1e-5
NEG_INF = -1e30
LOG2_E = math.log2(math.e)
F32_TINY = float(np.finfo(np.float32).tiny)

LANES = 128
SUBLANES = 8
VMEM_LIMIT = 56 * 1024 * 1024

ROW_TILE = 1024
MIX_TILE = 512
N_SEG = 32
SEGS_PER_TILE = SUBLANES
SCAN_CHUNK = 256
WEIGHT_CAST_ROWS = 128
BIAS_ROWS = 2

BF16 = jnp.bfloat16
F32 = jnp.float32


def _sigmoid(x):
    return 0.5 * jnp.tanh(0.5 * x) + 0.5


def _layer_norm_rows(z, g, b, eps=LN_EPS):
    mu = jnp.mean(z, axis=-1, keepdims=True)
    d = z - mu
    var = jnp.mean(d * d, axis=-1, keepdims=True)
    return d * lax.rsqrt(var + eps) * g + b


def _deepnorm_rows(x, gate, y, g, b):
    return _layer_norm_rows(x + (gate * (1.0 / DEEPNORM_ALPHA)) * y, g, b, LN_EPS / DEEPNORM_ALPHA ** 2)


def _params(*semantics):
    return pltpu.CompilerParams(dimension_semantics=semantics, vmem_limit_bytes=VMEM_LIMIT)


def _cast_weight_once(w_ref, w_bf16):
    @pl.when((pl.program_id(0) == 0) & (pl.program_id(1) == 0))
    def _():
        for r in range(0, w_ref.shape[0], WEIGHT_CAST_ROWS):
            w_bf16[r:r + WEIGHT_CAST_ROWS, :] = w_ref[r:r + WEIGHT_CAST_ROWS, :].astype(BF16)


def _resident(shape, col_block=0):
    return pl.BlockSpec(shape, lambda b, i: (0,) * (len(shape) - 1) + (col_block,), pipeline_mode=pl.Buffered(1))


def _mod_kernel(c_ref, w_ref, b_ref, o_ref):
    c = c_ref[...]
    cond = c * _sigmoid(c)
    pad_rows = o_ref.shape[0] - cond.shape[0]
    if pad_rows:
        cond = jnp.concatenate([cond, jnp.zeros((pad_rows, cond.shape[1]), F32)], axis=0)
    bias = b_ref[pl.ds(pl.program_id(0), 1), :]
    o_ref[...] = jnp.dot(cond.astype(BF16), w_ref[...].astype(BF16), preferred_element_type=F32) + bias


def _modulation(c, ada_w, ada_b):
    batch = c.shape[0]
    rows = -(-batch // SUBLANES) * SUBLANES
    n_col = 3 * D_MODEL // D_MODEL
    return pl.pallas_call(
        _mod_kernel,
        grid=(DEPTH, n_col),
        in_specs=[
            pl.BlockSpec((batch, D_MODEL), lambda l, j: (0, 0)),
            pl.BlockSpec((None, D_MODEL, D_MODEL), lambda l, j: (l, 0, j)),
            pl.BlockSpec((DEPTH, D_MODEL), lambda l, j: (0, j)),
        ],
        out_specs=pl.BlockSpec((None, rows, D_MODEL), lambda l, j: (l, 0, j)),
        out_shape=jax.ShapeDtypeStruct((DEPTH, rows, 3 * D_MODEL), F32),
        compiler_params=_params("parallel", "parallel"),
        name="adaln_mod",
    )(c, ada_w, ada_b)


MOD_SHIFT, MOD_SCALE, MOD_GATE = range(3)
EVEN_LAYER, ODD_LAYER = 0, 1


def _mod_spec(mod, layer, part):
    return pl.BlockSpec((None, mod.shape[1], D_MODEL), lambda b, i: (layer, 0, part))


def _batch_row(ref):
    return ref[pl.ds(pl.program_id(0), 1), :]


def _even_in_kernel(x_ref, shift_ref, scale_ref, pos_ref, freq_ref, sign_ref, w_f32_ref,
                    q_ref, k_ref, v_ref, su_ref, sv_ref, g_ref, w_ref):
    _cast_weight_once(w_f32_ref, w_ref)
    lane = lax.broadcasted_iota(jnp.int32, (1, LANES), 1)
    first_half = (lane % HEAD_DIM) < ROPE_HALF
    left_head = lane < HEAD_DIM
    diagonal = lax.broadcasted_iota(jnp.int32, (LANES, LANES), 0) == lax.broadcasted_iota(jnp.int32, (LANES, LANES), 1)
    batch_ids = lax.broadcasted_iota(jnp.int32, (pos_ref.shape[0], LANES), 0)
    q_scale = HEAD_DIM ** -0.5 * LOG2_E

    half_rows = x_ref.shape[0] // 2
    for r in range(2):
        rows = slice(r * half_rows, (r + 1) * half_rows)
        h = (x_ref[rows, :] * (1.0 + _batch_row(scale_ref)) + _batch_row(shift_ref)).astype(BF16)
        p = jnp.dot(h, w_ref[...], preferred_element_type=F32)

        pos_cols = []
        for j in range(r * half_rows // LANES, (r + 1) * half_rows // LANES):
            pos_all = pos_ref[:, j * LANES:(j + 1) * LANES].astype(F32)
            pos_row = jnp.sum(jnp.where(batch_ids == pl.program_id(0), pos_all, 0.0), axis=0, keepdims=True)
            pos_cols.append(jnp.sum(jnp.where(diagonal, pos_row, 0.0), axis=1, keepdims=True))
        ang = jnp.concatenate(pos_cols, axis=0) * freq_ref[...]
        cs = jnp.cos(ang)
        sn = jnp.sin(ang) * sign_ref[...]

        def rotary(t):
            partner = jnp.where(first_half, pltpu.roll(t, LANES - ROPE_HALF, 1), pltpu.roll(t, ROPE_HALF, 1))
            return t * cs + partner * sn

        for j in range(ATTN_WIDTH // LANES):
            q_ref[rows, j * LANES:(j + 1) * LANES] = (
                rotary(p[:, j * LANES:(j + 1) * LANES]) * q_scale).astype(BF16)
        o = ATTN_WIDTH
        k = rotary(p[:, o:o + KV_WIDTH])
        k_sw = pltpu.roll(k, HEAD_DIM, 1)
        k_ref[rows, 0:LANES] = jnp.where(left_head, k, k_sw).astype(BF16)
        k_ref[rows, LANES:2 * LANES] = jnp.where(left_head, k_sw, k).astype(BF16)
        o += KV_WIDTH
        v = p[:, o:o + KV_WIDTH]
        v_sw = pltpu.roll(v, HEAD_DIM, 1)
        v_ref[rows, 0 * LANES:1 * LANES] = jnp.where(left_head, v, 0.0).astype(BF16)
        v_ref[rows, 1 * LANES:2 * LANES] = jnp.where(left_head, 0.0, v_sw).astype(BF16)
        v_ref[rows, 2 * LANES:3 * LANES] = jnp.where(left_head, v_sw, 0.0).astype(BF16)
        v_ref[rows, 3 * LANES:4 * LANES] = jnp.where(left_head, 0.0, v).astype(BF16)
        o += KV_WIDTH
        su_ref[rows, :] = p[:, o:o + SG_WIDTH].astype(BF16)
        o += SG_WIDTH
        sv_ref[rows, :] = p[:, o:o + SG_WIDTH].astype(BF16)
        o += SG_WIDTH
        g_ref[rows, :] = p[:, o:o + EVEN_GATE_WIDTH].astype(BF16)


def _even_in_proj(x, mod, positions, w_in):
    batch, seq, _ = x.shape
    d = np.arange(LANES) % HEAD_DIM
    inv_freq = np.power(np.float64(ROPE_THETA), -np.arange(ROPE_HALF) / ROPE_HALF)
    freq_lane = jnp.asarray(np.where(d < ROPE_DIM, inv_freq[d % ROPE_HALF], 0.0), dtype=F32).reshape(1, LANES)
    sign_lane = jnp.asarray(np.where(d < ROPE_HALF, -1.0, np.where(d < ROPE_DIM, 1.0, 0.0)),
                            dtype=F32).reshape(1, LANES)
    row = lambda width: pl.BlockSpec((None, ROW_TILE, width), lambda b, i: (b, i, 0))
    const = lambda shape: pl.BlockSpec(shape, lambda b, i: (0,) * len(shape))
    widths = (ATTN_WIDTH, 2 * KV_WIDTH, 4 * KV_WIDTH, SG_WIDTH, SG_WIDTH, EVEN_GATE_WIDTH)
    return pl.pallas_call(
        _even_in_kernel,
        grid=(batch, seq // ROW_TILE),
        in_specs=[row(D_MODEL), _mod_spec(mod, EVEN_LAYER, MOD_SHIFT), _mod_spec(mod, EVEN_LAYER, MOD_SCALE),
                  pl.BlockSpec((batch, ROW_TILE), lambda b, i: (0, i)),
                  const((1, LANES)), const((1, LANES)), _resident((D_MODEL, EVEN_IN_WIDTH))],
        out_specs=[row(w) for w in widths],
        out_shape=[jax.ShapeDtypeStruct((batch, seq, w), BF16) for w in widths],
        scratch_shapes=[pltpu.VMEM((D_MODEL, EVEN_IN_WIDTH), BF16)],
        compiler_params=_params("arbitrary", "arbitrary"),
        name="even_in_proj",
    )(x, mod, mod, positions, freq_lane, sign_lane, w_in)


def _even_mix_kernel(sink_ref, q_ref, k_ref, v_ref, su_ref, sv_ref, g_ref, x_ref, gate_ref,
                     avg_ref, avg2_ref, sgg_ref, sgb_ref, sgw_ref, sgbias_ref, wout_f32_ref, lng_ref, lnb_ref,
                     o_ref, y_scr, wout_ref):
    _cast_weight_once(wout_f32_ref, wout_ref)
    seq = k_ref.shape[0]
    n_blocks = seq // ATTN_BLOCK
    sub_blocks = MIX_TILE // ATTN_BLOCK
    tile = pl.program_id(1)
    lane = lax.broadcasted_iota(jnp.int32, (1, LANES), 1)
    left = lane < HEAD_DIM

    qi = lax.broadcasted_iota(jnp.int32, (ATTN_BLOCK, ATTN_BLOCK), 0)
    kj = lax.broadcasted_iota(jnp.int32, (ATTN_BLOCK, ATTN_BLOCK), 1)
    ones_left = jnp.broadcast_to(jnp.where(left, 1.0, 0.0).astype(BF16), (3 * ATTN_BLOCK, LANES))
    ones_right = jnp.broadcast_to(jnp.where(left, 0.0, 1.0).astype(BF16), (3 * ATTN_BLOCK, LANES))
    avg = avg_ref[...]
    sink_cols = [jnp.concatenate([jnp.full((ATTN_BLOCK, LANES), sink_ref[h] * LOG2_E, F32)
                                  for h in (4 * hk, 4 * hk + 2, 4 * hk + 1, 4 * hk + 3)], axis=0)
                 for hk in range(N_KV_HEADS)]

    def sub_block(j, carry):
        blk = tile * sub_blocks + j
        prev = jnp.maximum(blk - 1, 0)
        nxt = jnp.minimum(blk + 1, n_blocks - 1)
        rows = pl.ds(pl.multiple_of(j * ATTN_BLOCK, ATTN_BLOCK), ATTN_BLOCK)
        bias_prev = jnp.where((kj >= qi) & (blk > 0), 0.0, NEG_INF)
        bias_next = jnp.where((kj <= qi) & (blk < n_blocks - 1), 0.0, NEG_INF)
        bias_prev = jnp.concatenate([bias_prev] * Q_PER_KV, axis=0)
        bias_next = jnp.concatenate([bias_next] * Q_PER_KV, axis=0)

        def band(ref):
            return jnp.concatenate(
                [ref[pl.ds(pl.multiple_of(b * ATTN_BLOCK, ATTN_BLOCK), ATTN_BLOCK), :] for b in (prev, blk, nxt)],
                axis=0)

        k_band = band(k_ref)
        v_band = band(v_ref)
        n_sg = SG_WIDTH // LANES
        sg_cols = [slice(c * LANES, (c + 1) * LANES) for c in range(n_sg)]

        def silu_gate(cols):
            g_c = g_ref[rows, cols].astype(F32)
            return g_c * _sigmoid(g_c)

        scores = []
        for hk in range(N_KV_HEADS):
            q_cols = [q_ref[rows, (2 * hk + c) * LANES:(2 * hk + c + 1) * LANES] for c in range(2)]
            zero = jnp.zeros_like(q_cols[0])
            lhs = jnp.concatenate([jnp.where(left, qc, zero) for qc in q_cols]
                                  + [jnp.where(left, zero, qc) for qc in q_cols], axis=0)
            scores.append(lax.dot_general(lhs, k_band[:, hk * LANES:(hk + 1) * LANES], (((1,), (1,)), ((), ())),
                                          preferred_element_type=F32))
        sv_cols = [sv_ref[rows, cols] for cols in sg_cols]
        means = [jnp.dot(v_c, avg, preferred_element_type=F32) for v_c in sv_cols]

        probs, row_max = [], []
        for hk in range(N_KV_HEADS):
            s = scores[hk]
            s0 = s[:, 0:ATTN_BLOCK] + bias_prev
            s1 = s[:, ATTN_BLOCK:2 * ATTN_BLOCK]
            s2 = s[:, 2 * ATTN_BLOCK:3 * ATTN_BLOCK] + bias_next
            m = jnp.max(jnp.maximum(jnp.maximum(s0, s1), s2), axis=-1, keepdims=True)
            m = jnp.maximum(jnp.broadcast_to(m, sink_cols[hk].shape), sink_cols[hk])
            probs.append(jnp.concatenate([jnp.exp2(t - m).astype(BF16) for t in (s0, s1, s2)], axis=1))
            row_max.append(m)
        devs, sq_parts = [], []
        for v_c, mean in zip(sv_cols, means):
            dev = v_c.astype(F32) - mean
            sq = dev * dev
            sq_hi = sq.astype(BF16)
            sq_lo = (sq - sq_hi.astype(F32)).astype(BF16)
            devs.append(dev)
            sq_parts.append(jnp.concatenate([sq_hi, sq_lo], axis=1))

        pv = []
        for hk in range(N_KV_HEADS):
            p = probs[hk]
            lhs = jnp.concatenate(
                [jnp.concatenate([p[c * ATTN_BLOCK:(c + 1) * ATTN_BLOCK], p[(2 + c) * ATTN_BLOCK:(3 + c) * ATTN_BLOCK]],
                                 axis=1) for c in range(2)], axis=0)
            rhs = jnp.concatenate(
                [jnp.concatenate([v_band[:, (2 * hk) * LANES:(2 * hk + 1) * LANES], ones_left], axis=1),
                 jnp.concatenate([v_band[:, (2 * hk + 1) * LANES:(2 * hk + 2) * LANES], ones_right], axis=1)],
                axis=0)
            pv.append(jnp.dot(lhs, rhs, preferred_element_type=F32))
        variances = [jnp.dot(sq, avg2_ref[...], preferred_element_type=F32) for sq in sq_parts]

        for hk in range(N_KV_HEADS):
            for c in range(2):
                even, odd = slice(c * ATTN_BLOCK, (c + 1) * ATTN_BLOCK), slice((2 + c) * ATTN_BLOCK, (3 + c) * ATTN_BLOCK)
                sink_term = jnp.exp2(jnp.where(left, sink_cols[hk][even] - row_max[hk][even],
                                               sink_cols[hk][odd] - row_max[hk][odd]))
                o = pv[hk][c * ATTN_BLOCK:(c + 1) * ATTN_BLOCK]
                pair = o[:, 0:LANES] * (1.0 / (o[:, LANES:2 * LANES] + sink_term))
                cols = slice((2 * hk + c) * LANES, (2 * hk + c + 1) * LANES)
                y_scr[rows, cols] = (pair * silu_gate(cols)).astype(BF16)

        for c, cols in enumerate(sg_cols):
            vn = devs[c] * lax.rsqrt(variances[c] + LN_EPS) * sgg_ref[:, cols] + sgb_ref[:, cols]
            stacked = jnp.concatenate([jnp.where(left, vn, 0.0), jnp.where(left, 0.0, vn)],
                                      axis=0).astype(BF16)
            mixed = jnp.dot(sgw_ref[c], stacked, preferred_element_type=F32) + sgbias_ref[:, cols]
            y_sg = su_ref[rows, cols].astype(F32) * mixed
            gc = slice(ATTN_WIDTH + c * LANES, ATTN_WIDTH + (c + 1) * LANES)
            y_scr[rows, gc] = (y_sg * silu_gate(gc)).astype(BF16)
        return carry

    lax.fori_loop(0, sub_blocks, sub_block, 0, unroll=True)

    for r in range(2):
        half_rows = slice(r * (MIX_TILE // 2), (r + 1) * (MIX_TILE // 2))
        out = jnp.dot(y_scr[half_rows, :], wout_ref[...], preferred_element_type=F32)
        o_ref[half_rows, :] = _deepnorm_rows(x_ref[half_rows, :], _batch_row(gate_ref), out,
                                             lng_ref[EVEN_LAYER:EVEN_LAYER + 1, :], lnb_ref[EVEN_LAYER:EVEN_LAYER + 1, :])


def _even_mixer(q, k, v, su, sv, g, x, mod, sink, sg_ln_g, sg_ln_b, sg_w, sg_b, w_out, ln_g, ln_b):
    batch, seq, _ = x.shape
    lane = np.arange(LANES)
    avg = jnp.asarray((lane[:, None] // SG_GROUP_DIM == lane[None, :] // SG_GROUP_DIM) / SG_GROUP_DIM, dtype=BF16)
    w_pairs = sg_w.reshape(N_SG_GROUPS // 2, 2, SG_CHUNK, SG_CHUNK).transpose(0, 2, 1, 3)
    w_pairs = w_pairs.reshape(N_SG_GROUPS // 2, SG_CHUNK, 2 * SG_CHUNK).astype(BF16)
    bias_full = jnp.repeat(sg_b.T, SG_GROUP_DIM, axis=1)
    row = lambda width: pl.BlockSpec((None, MIX_TILE, width), lambda b, i: (b, i, 0))
    full_seq = lambda arr: pl.BlockSpec((None, seq, arr.shape[-1]), lambda b, i: (b, 0, 0))
    const = lambda shape: pl.BlockSpec(shape, lambda b, i: (0,) * len(shape))
    return pl.pallas_call(
        _even_mix_kernel,
        grid=(batch, seq // MIX_TILE),
        in_specs=[
            pl.BlockSpec(memory_space=pltpu.SMEM),
            row(ATTN_WIDTH), full_seq(k), full_seq(v), row(SG_WIDTH), row(SG_WIDTH), row(EVEN_GATE_WIDTH),
            row(D_MODEL), _mod_spec(mod, EVEN_LAYER, MOD_GATE),
            const((LANES, LANES)), const((2 * LANES, LANES)), const((1, SG_WIDTH)), const((1, SG_WIDTH)),
            const((N_SG_GROUPS // 2, SG_CHUNK, 2 * SG_CHUNK)), const((SG_CHUNK, SG_WIDTH)),
            _resident((EVEN_GATE_WIDTH, D_MODEL)), const(ln_g.shape), const(ln_b.shape),
        ],
        out_specs=row(D_MODEL),
        out_shape=jax.ShapeDtypeStruct((batch, seq, D_MODEL), F32),
        scratch_shapes=[pltpu.VMEM((MIX_TILE, EVEN_GATE_WIDTH), BF16),
                        pltpu.VMEM((EVEN_GATE_WIDTH, D_MODEL), BF16)],
        compiler_params=_params("arbitrary", "arbitrary"),
        name="even_mixer",
    )(sink, q, k, v, su, sv, g, x, mod, avg, jnp.concatenate([avg, avg], axis=0), sg_ln_g.reshape(1, -1), sg_ln_b.reshape(1, -1),
      w_pairs, bias_full, w_out, ln_g, ln_b)


def _odd_in_kernel(x_ref, shift_ref, scale_ref, w_f32_ref, xr_ref, w_ref):
    _cast_weight_once(w_f32_ref, w_ref)
    seg_len = xr_ref.shape[1]
    head_rows = seg_len * SEGS_PER_TILE
    xr_rows = xr_ref.reshape(RNN_HEADS * head_rows, LANES)
    halves = 2
    for r in range(halves):
        first = r * (SEGS_PER_TILE // halves)
        rows = slice(first * seg_len, (first + SEGS_PER_TILE // halves) * seg_len)
        h = (x_ref[rows, :] * (1.0 + _batch_row(scale_ref)) + _batch_row(shift_ref)).astype(BF16)
        p = jnp.dot(h, w_ref[...], preferred_element_type=F32)
        for s in range(SEGS_PER_TILE // halves):
            for hd in range(RNN_HEADS):
                xr_rows[pl.ds(hd * head_rows + first + s, seg_len, stride=SEGS_PER_TILE), :] = (
                    p[s * seg_len:(s + 1) * seg_len, hd * LANES:(hd + 1) * LANES])


def _odd_in_proj(x, mod, w_in):
    batch, seq, _ = x.shape
    seg_len = seq // N_SEG
    tile_rows = SEGS_PER_TILE * seg_len
    out_spec = pl.BlockSpec((None, RNN_HEADS, seg_len, SEGS_PER_TILE, RNN_HEAD_DIM), lambda b, i: (b, 0, 0, i, 0))
    out_shape = jax.ShapeDtypeStruct((batch, RNN_HEADS, seg_len, N_SEG, RNN_HEAD_DIM), F32)
    return pl.pallas_call(
        _odd_in_kernel,
        grid=(batch, seq // tile_rows),
        in_specs=[pl.BlockSpec((None, tile_rows, D_MODEL), lambda b, i: (b, i, 0)),
                  _mod_spec(mod, ODD_LAYER, MOD_SHIFT), _mod_spec(mod, ODD_LAYER, MOD_SCALE),
                  _resident((D_MODEL, RNN_WIDTH), col_block=0)],
        out_specs=out_spec,
        out_shape=out_shape,
        scratch_shapes=[pltpu.VMEM((D_MODEL, RNN_WIDTH), BF16)],
        compiler_params=_params("arbitrary", "arbitrary"),
        name="odd_in_proj",
    )(x, mod, mod, w_in)


def _rglru_kernel(x_ref, convw_ref, convb_ref, waug_ref, lam_ref, o_ref,
                  halo, a_f, b_f, a_b, b_b, edge):
    n_rows = x_ref.shape[0]
    seg_len = n_rows // N_SEG
    n_chunks = n_rows // SCAN_CHUNK
    left_rows = CONV_LEFT * N_SEG
    right_rows = (CONV_WIDTH - 1 - CONV_LEFT) * N_SEG
    seg = lax.broadcasted_iota(jnp.int32, (N_SEG, LANES), 0)

    def from_prev_segment(blk):
        return jnp.where(seg == 0, 0.0, pltpu.roll(blk, 1, 0))

    def from_next_segment(blk):
        return jnp.where(seg == N_SEG - 1, 0.0, pltpu.roll(blk, N_SEG - 1, 0))

    for t in range(CONV_LEFT):
        halo[0, pl.ds(t * N_SEG, N_SEG), :] = from_prev_segment(
            x_ref[pl.ds((seg_len - CONV_LEFT + t) * N_SEG, N_SEG), :])
    halo[0, pl.ds(left_rows, SCAN_CHUNK + right_rows), :] = x_ref[pl.ds(0, SCAN_CHUNK + right_rows), :]
    halo[1, pl.ds(0, left_rows + SCAN_CHUNK), :] = x_ref[pl.ds(n_rows - SCAN_CHUNK - left_rows, left_rows + SCAN_CHUNK), :]
    for t in range(CONV_WIDTH - 1 - CONV_LEFT):
        halo[1, pl.ds(left_rows + SCAN_CHUNK + t * N_SEG, N_SEG), :] = from_next_segment(
            x_ref[pl.ds(t * N_SEG, N_SEG), :])

    lam = lam_ref[...]
    softplus_neg = jnp.maximum(-lam, 0.0) + jnp.log1p(jnp.exp(-jnp.abs(lam)))
    half_rate = (0.5 * RG_LRU_C) * softplus_neg
    bias_lanes = jnp.broadcast_to(
        jnp.where(lax.broadcasted_iota(jnp.int32, (1, LANES), 1) < BIAS_ROWS, 1.0, 0.0).astype(BF16),
        (SCAN_CHUNK, LANES))

    def gates(base, tap):
        half_x = convb_ref[...] + sum(convw_ref[k:k + 1, :] * tap(k) for k in range(CONV_WIDTH))
        th = jnp.tanh(jnp.dot(jnp.concatenate([half_x.astype(BF16), bias_lanes], axis=1), waug_ref[...],
                              preferred_element_type=F32))
        for d, (a_ref, b_ref) in enumerate(((a_f, b_f), (a_b, b_b))):
            o = 2 * d * LANES
            neg_log_a = (th[:, o:o + LANES] + 1.0) * half_rate[d:d + 1, :]
            a = jnp.exp2(neg_log_a * (-LOG2_E))
            one_minus_a2 = jnp.tanh(neg_log_a) * (a * a + 1.0)
            root = one_minus_a2 * lax.rsqrt(jnp.maximum(one_minus_a2, F32_TINY))
            a_ref[pl.ds(base, SCAN_CHUNK), :] = a
            b_ref[pl.ds(base, SCAN_CHUNK), :] = root * ((th[:, o + LANES:o + 2 * LANES] + 1.0) * half_x)

    def inner_chunk(c, carry):
        base = pl.multiple_of(c * SCAN_CHUNK, SCAN_CHUNK)
        gates(base, lambda k: x_ref[pl.ds(base + (k - CONV_LEFT) * N_SEG, SCAN_CHUNK), :])
        return carry

    gates(0, lambda k: halo[0, pl.ds(k * N_SEG, SCAN_CHUNK), :])
    gates(n_rows - SCAN_CHUNK, lambda k: halo[1, pl.ds(k * N_SEG, SCAN_CHUNK), :])
    lax.fori_loop(1, n_chunks - 1, inner_chunk, 0, unroll=7)

    def scan_step(t, carry):
        h_f, p_f, h_b, p_b = carry
        rf = pl.ds(pl.multiple_of(t * N_SEG, N_SEG), N_SEG)
        rb = pl.ds(pl.multiple_of((seg_len - 1 - t) * N_SEG, N_SEG), N_SEG)
        a = a_f[rf, :]
        h_f = a * h_f + b_f[rf, :]
        p_f = a * p_f
        b_f[rf, :] = h_f
        a_f[rf, :] = p_f
        a = a_b[rb, :]
        h_b = a * h_b + b_b[rb, :]
        p_b = a * p_b
        b_b[rb, :] = h_b
        a_b[rb, :] = p_b
        return h_f, p_f, h_b, p_b

    zeros = jnp.zeros((N_SEG, LANES), F32)
    ones = jnp.ones((N_SEG, LANES), F32)
    h_f, p_f, h_b, p_b = lax.fori_loop(0, seg_len, scan_step, (zeros, ones, zeros, ones), unroll=4)

    edge[0] = h_f
    edge[1] = p_f
    edge[2] = h_b
    edge[3] = p_b
    enter_f = enter_b = jnp.zeros((1, LANES), F32)
    edge[4, 0:1, :] = enter_f
    edge[5, N_SEG - 1:N_SEG, :] = enter_b
    for s in range(1, N_SEG):
        enter_f = edge[0, s - 1:s, :] + edge[1, s - 1:s, :] * enter_f
        edge[4, s:s + 1, :] = enter_f
        r = N_SEG - 1 - s
        enter_b = edge[2, r + 1:r + 2, :] + edge[3, r + 1:r + 2, :] * enter_b
        edge[5, r:r + 1, :] = enter_b
    blocks = SCAN_CHUNK // N_SEG
    in_f = jnp.concatenate([edge[4]] * blocks, axis=0)
    in_b = jnp.concatenate([edge[5]] * blocks, axis=0)

    def out_chunk(c, carry):
        rows = pl.ds(pl.multiple_of(c * SCAN_CHUNK, SCAN_CHUNK), SCAN_CHUNK)
        o_ref[rows, :] = (b_f[rows, :] + a_f[rows, :] * in_f) + (b_b[rows, :] + a_b[rows, :] * in_b)
        return carry

    lax.fori_loop(0, n_rows // SCAN_CHUNK, out_chunk, 0, unroll=2)


def _rglru(xr, conv_w, conv_b, w_a, b_a, w_x, b_x, lam):
    batch, _, n_rows, _ = xr.shape
    w_cat = jnp.concatenate([w_a[0], w_x[0], w_a[1], w_x[1]], axis=-1).astype(BF16)
    per_head = lambda v: v.reshape(RNN_HEADS, 1, RNN_HEAD_DIM)
    half_b = 0.5 * jnp.concatenate([per_head(b_a[0]), per_head(b_x[0]), per_head(b_a[1]), per_head(b_x[1])],
                                   axis=-1)
    b_hi = half_b.astype(BF16)
    b_lo = (half_b - b_hi.astype(F32)).astype(BF16)
    w_aug = jnp.concatenate([w_cat, b_hi, b_lo,
                             jnp.zeros((RNN_HEADS, RNN_HEAD_DIM - BIAS_ROWS, 4 * RNN_HEAD_DIM), BF16)], axis=1)
    col = lambda rows: pl.BlockSpec((rows, RNN_HEAD_DIM), lambda b, h: (0, h))
    seq_col = pl.BlockSpec((None, None, n_rows, RNN_HEAD_DIM), lambda b, h: (b, h, 0, 0))
    halo_rows = SCAN_CHUNK + (CONV_WIDTH - 1) * N_SEG
    return pl.pallas_call(
        _rglru_kernel,
        grid=(batch, RNN_HEADS),
        in_specs=[seq_col, col(CONV_WIDTH), col(1),
                  pl.BlockSpec((None, 2 * RNN_HEAD_DIM, 4 * RNN_HEAD_DIM), lambda b, h: (h, 0, 0)),
                  col(2)],
        out_specs=seq_col,
        out_shape=jax.ShapeDtypeStruct((batch, RNN_HEADS, n_rows, RNN_HEAD_DIM), F32),
        scratch_shapes=[pltpu.VMEM((2, halo_rows, LANES), F32)]
                       + [pltpu.VMEM((n_rows, LANES), F32)] * 4
                       + [pltpu.VMEM((6, N_SEG, LANES), F32)],
        compiler_params=_params("parallel", "parallel"),
        name="rglru",
    )(xr, 0.5 * conv_w, 0.5 * conv_b.reshape(1, -1), w_aug, lam)


def _odd_out_kernel(y_ref, x_ref, shift_ref, scale_ref, gate_ref, wg_f32_ref, w_f32_ref, lng_ref, lnb_ref,
                    o_ref, wg_ref, w_ref):
    _cast_weight_once(wg_f32_ref, wg_ref)
    _cast_weight_once(w_f32_ref, w_ref)
    seg_len = y_ref.shape[1]
    head_rows = seg_len * SEGS_PER_TILE
    y_rows = y_ref.reshape(RNN_HEADS * head_rows, LANES)
    halves = 1
    for r in range(halves):
        first = r * (SEGS_PER_TILE // halves)
        rows = slice(first * seg_len, (first + SEGS_PER_TILE // halves) * seg_len)
        x = x_ref[rows, :]
        h = (x * (1.0 + _batch_row(scale_ref)) + _batch_row(shift_ref)).astype(BF16)
        g_all = jnp.dot(h, wg_ref[...], preferred_element_type=F32)
        gated = []
        for s in range(SEGS_PER_TILE // halves):
            y = jnp.concatenate([y_rows[pl.ds(hd * head_rows + first + s, seg_len, stride=SEGS_PER_TILE), :]
                                 for hd in range(RNN_HEADS)], axis=1)
            g = g_all[s * seg_len:(s + 1) * seg_len, :]
            gated.append((y * (g * _sigmoid(g))).astype(BF16))
        out = jnp.dot(jnp.concatenate(gated, axis=0), w_ref[...], preferred_element_type=F32)
        o_ref[rows, :] = _deepnorm_rows(x, _batch_row(gate_ref), out,
                                        lng_ref[ODD_LAYER:ODD_LAYER + 1, :], lnb_ref[ODD_LAYER:ODD_LAYER + 1, :])


def _odd_out_proj(y, x, mod, w_in, w_out, ln_g, ln_b):
    batch, seq, _ = x.shape
    seg_len = seq // N_SEG
    tile_rows = SEGS_PER_TILE * seg_len
    const = lambda shape: pl.BlockSpec(shape, lambda b, i: (0,) * len(shape))
    return pl.pallas_call(
        _odd_out_kernel,
        grid=(batch, seq // tile_rows),
        in_specs=[pl.BlockSpec((None, RNN_HEADS, seg_len, SEGS_PER_TILE, RNN_HEAD_DIM),
                               lambda b, i: (b, 0, 0, i, 0)),
                  pl.BlockSpec((None, tile_rows, D_MODEL), lambda b, i: (b, i, 0)),
                  _mod_spec(mod, ODD_LAYER, MOD_SHIFT), _mod_spec(mod, ODD_LAYER, MOD_SCALE),
                  _mod_spec(mod, ODD_LAYER, MOD_GATE),
                  _resident((D_MODEL, RNN_WIDTH), col_block=1), _resident((RNN_WIDTH, D_MODEL)),
                  const(ln_g.shape), const(ln_b.shape)],
        out_specs=pl.BlockSpec((None, tile_rows, D_MODEL), lambda b, i: (b, i, 0)),
        out_shape=jax.ShapeDtypeStruct((batch, seq, D_MODEL), F32),
        scratch_shapes=[pltpu.VMEM((D_MODEL, RNN_WIDTH), BF16), pltpu.VMEM((RNN_WIDTH, D_MODEL), BF16)],
        compiler_params=_params("arbitrary", "arbitrary"),
        name="odd_out_proj",
    )(y, x, mod, mod, mod, w_in, w_out, ln_g, ln_b)


def kernel(x, c, positions, ada_w, ada_b, ln_g, ln_b, ev_w_in, ev_w_out, ev_sink, ev_sg_ln_g, ev_sg_ln_b,
           ev_sg_w, ev_sg_b, od_w_in, od_conv_w, od_conv_b, od_w_a, od_b_a, od_w_x, od_b_x, od_lam, od_w_out):
    batch, seq, d_model = x.shape
    seg_len = seq // N_SEG
    assert d_model == D_MODEL and seq % ROW_TILE == 0 and seq % MIX_TILE == 0 and seq == N_SEG * seg_len
    assert N_SEG % SEGS_PER_TILE == 0 and seq % SCAN_CHUNK == 0 and SCAN_CHUNK % N_SEG == 0
    assert ada_w.shape[0] == DEPTH == 2

    mod = _modulation(c, ada_w, ada_b)

    q, k, v, su, sv, g = _even_in_proj(x, mod, positions, ev_w_in[0])
    x1 = _even_mixer(q, k, v, su, sv, g, x, mod, ev_sink[0], ev_sg_ln_g[0], ev_sg_ln_b[0],
                     ev_sg_w[0], ev_sg_b[0], ev_w_out[0], ln_g, ln_b)

    xr = _odd_in_proj(x1, mod, od_w_in[0])
    y = _rglru(xr.reshape(batch, RNN_HEADS, seq, RNN_HEAD_DIM), od_conv_w[0], od_conv_b[0], od_w_a[0],
               od_b_a[0], od_w_x[0], od_b_x[0], od_lam[0])
    return _odd_out_proj(y.reshape(batch, RNN_HEADS, seg_len, N_SEG, RNN_HEAD_DIM), x1, mod, od_w_in[0],
                         od_w_out[0], ln_g, ln_b)
```

```python
import math

import numpy as np
import jax
import jax.numpy as jnp
from jax import lax
from jax.experimental import pallas as pl
from jax.experimental.pallas import tpu as pltpu

D_MODEL = 1024
DEPTH = 2
HEAD_DIM = 64
N_Q_HEADS = 8
N_KV_HEADS = 2
Q_PER_KV = N_Q_HEADS // N_KV_HEADS
ATTN_WIDTH = N_Q_HEADS * HEAD_DIM
KV_WIDTH = N_KV_HEADS * HEAD_DIM
ATTN_BLOCK = 128
ROPE_DIM = HEAD_DIM // 4
ROPE_HALF = ROPE_DIM // 2
ROPE_THETA = 500000.0
N_SG_GROUPS = 8
SG_GROUP_DIM = 64
SG_WIDTH = N_SG_GROUPS * SG_GROUP_DIM
SG_CHUNK = 128
EVEN_GATE_WIDTH = ATTN_WIDTH + SG_WIDTH
EVEN_IN_WIDTH = ATTN_WIDTH + 2 * KV_WIDTH + 2 * SG_WIDTH + EVEN_GATE_WIDTH
RNN_WIDTH = D_MODEL
RNN_HEADS = 8
RNN_HEAD_DIM = RNN_WIDTH // RNN_HEADS
CONV_WIDTH = 4
CONV_LEFT = 2
RG_LRU_C = 8.0
DEEPNORM_ALPHA = (2 * DEPTH) ** 0.25
LN_EPS = 1e-5
NEG_INF = -1e30
LOG2_E = math.log2(math.e)
F32_TINY = float(np.finfo(np.float32).tiny)

LANES = 128
SUBLANES = 8
VMEM_LIMIT = 56 * 1024 * 1024

ROW_TILE = 1024
MIX_TILE = 512
N_SEG = 32
SEGS_PER_TILE = SUBLANES
SCAN_CHUNK = 256
WEIGHT_CAST_ROWS = 128
BIAS_ROWS = 2

BF16 = jnp.bfloat16
F32 = jnp.float32


def _sigmoid(x):
    return 0.5 * jnp.tanh(0.5 * x) + 0.5


def _layer_norm_rows(z, g, b, eps=LN_EPS):
    mu = jnp.mean(z, axis=-1, keepdims=True)
    d = z - mu
    var = jnp.mean(d * d, axis=-1, keepdims=True)
    return d * lax.rsqrt(var + eps) * g + b


def _deepnorm_rows(x, gate, y, g, b):
    return _layer_norm_rows(x + (gate * (1.0 / DEEPNORM_ALPHA)) * y, g, b, LN_EPS / DEEPNORM_ALPHA ** 2)


def _params(*semantics):
    return pltpu.CompilerParams(dimension_semantics=semantics, vmem_limit_bytes=VMEM_LIMIT)


def _cast_weight_once(w_ref, w_bf16):
    @pl.when((pl.program_id(0) == 0) & (pl.program_id(1) == 0))
    def _():
        for r in range(0, w_ref.shape[0], WEIGHT_CAST_ROWS):
            w_bf16[r:r + WEIGHT_CAST_ROWS, :] = w_ref[r:r + WEIGHT_CAST_ROWS, :].astype(BF16)


def _resident(shape, col_block=0):
    return pl.BlockSpec(shape, lambda b, i: (0,) * (len(shape) - 1) + (col_block,), pipeline_mode=pl.Buffered(1))


def _mod_kernel(c_ref, w_ref, b_ref, o_ref):
    c = c_ref[...]
    cond = c * _sigmoid(c)
    pad_rows = o_ref.shape[0] - cond.shape[0]
    if pad_rows:
        cond = jnp.concatenate([cond, jnp.zeros((pad_rows, cond.shape[1]), F32)], axis=0)
    bias = b_ref[pl.ds(pl.program_id(0), 1), :]
    o_ref[...] = jnp.dot(cond.astype(BF16), w_ref[...].astype(BF16), preferred_element_type=F32) + bias


def _modulation(c, ada_w, ada_b):
    batch = c.shape[0]
    rows = -(-batch // SUBLANES) * SUBLANES
    n_col = 3 * D_MODEL // D_MODEL
    return pl.pallas_call(
        _mod_kernel,
        grid=(DEPTH, n_col),
        in_specs=[
            pl.BlockSpec((batch, D_MODEL), lambda l, j: (0, 0)),
            pl.BlockSpec((None, D_MODEL, D_MODEL), lambda l, j: (l, 0, j)),
            pl.BlockSpec((DEPTH, D_MODEL), lambda l, j: (0, j)),
        ],
        out_specs=pl.BlockSpec((None, rows, D_MODEL), lambda l, j: (l, 0, j)),
        out_shape=jax.ShapeDtypeStruct((DEPTH, rows, 3 * D_MODEL), F32),
        compiler_params=_params("parallel", "parallel"),
        name="adaln_mod",
    )(c, ada_w, ada_b)


MOD_SHIFT, MOD_SCALE, MOD_GATE = range(3)
EVEN_LAYER, ODD_LAYER = 0, 1


def _mod_spec(mod, layer, part):
    return pl.BlockSpec((None, mod.shape[1], D_MODEL), lambda b, i: (layer, 0, part))


def _batch_row(ref):
    return ref[pl.ds(pl.program_id(0), 1), :]


def _even_in_kernel(x_ref, shift_ref, scale_ref, pos_ref, freq_ref, sign_ref, w_f32_ref,
                    q_ref, k_ref, v_ref, su_ref, sv_ref, g_ref, w_ref):
    _cast_weight_once(w_f32_ref, w_ref)
    lane = lax.broadcasted_iota(jnp.int32, (1, LANES), 1)
    first_half = (lane % HEAD_DIM) < ROPE_HALF
    left_head = lane < HEAD_DIM
    diagonal = lax.broadcasted_iota(jnp.int32, (LANES, LANES), 0) == lax.broadcasted_iota(jnp.int32, (LANES, LANES), 1)
    batch_ids = lax.broadcasted_iota(jnp.int32, (pos_ref.shape[0], LANES), 0)
    q_scale = HEAD_DIM ** -0.5 * LOG2_E

    half_rows = x_ref.shape[0] // 1
    for r in range(1):
        rows = slice(r * half_rows, (r + 1) * half_rows)
        h = (x_ref[rows, :] * (1.0 + _batch_row(scale_ref)) + _batch_row(shift_ref)).astype(BF16)
        p = jnp.dot(h, w_ref[...], preferred_element_type=F32)

        pos_cols = []
        for j in range(r * half_rows // LANES, (r + 1) * half_rows // LANES):
            pos_all = pos_ref[:, j * LANES:(j + 1) * LANES].astype(F32)
            pos_row = jnp.sum(jnp.where(batch_ids == pl.program_id(0), pos_all, 0.0), axis=0, keepdims=True)
            pos_cols.append(jnp.sum(jnp.where(diagonal, pos_row, 0.0), axis=1, keepdims=True))
        ang = jnp.concatenate(pos_cols, axis=0) * freq_ref[...]
        cs = jnp.cos(ang)
        sn = jnp.sin(ang) * sign_ref[...]

        def rotary(t):
            partner = jnp.where(first_half, pltpu.roll(t, LANES - ROPE_HALF, 1), pltpu.roll(t, ROPE_HALF, 1))
            return t * cs + partner * sn

        for j in range(ATTN_WIDTH // LANES):
            q_ref[rows, j * LANES:(j + 1) * LANES] = (
                rotary(p[:, j * LANES:(j + 1) * LANES]) * q_scale).astype(BF16)
        o = ATTN_WIDTH
        k = rotary(p[:, o:o + KV_WIDTH])
        k_sw = pltpu.roll(k, HEAD_DIM, 1)
        k_ref[rows, 0:LANES] = jnp.where(left_head, k, k_sw).astype(BF16)
        k_ref[rows, LANES:2 * LANES] = jnp.where(left_head, k_sw, k).astype(BF16)
        o += KV_WIDTH
        v = p[:, o:o + KV_WIDTH]
        v_sw = pltpu.roll(v, HEAD_DIM, 1)
        v_ref[rows, 0 * LANES:1 * LANES] = jnp.where(left_head, v, 0.0).astype(BF16)
        v_ref[rows, 1 * LANES:2 * LANES] = jnp.where(left_head, 0.0, v_sw).astype(BF16)
        v_ref[rows, 2 * LANES:3 * LANES] = jnp.where(left_head, v_sw, 0.0).astype(BF16)
        v_ref[rows, 3 * LANES:4 * LANES] = jnp.where(left_head, 0.0, v).astype(BF16)
        o += KV_WIDTH
        su_ref[rows, :] = p[:, o:o + SG_WIDTH].astype(BF16)
        o += SG_WIDTH
        sv_ref[rows, :] = p[:, o:o + SG_WIDTH].astype(BF16)
        o += SG_WIDTH
        g_ref[rows, :] = p[:, o:o + EVEN_GATE_WIDTH].astype(BF16)


def _even_in_proj(x, mod, positions, w_in):
    batch, seq, _ = x.shape
    d = np.arange(LANES) % HEAD_DIM
    inv_freq = np.power(np.float64(ROPE_THETA), -np.arange(ROPE_HALF) / ROPE_HALF)
    freq_lane = jnp.asarray(np.where(d < ROPE_DIM, inv_freq[d % ROPE_HALF], 0.0), dtype=F32).reshape(1, LANES)
    sign_lane = jnp.asarray(np.where(d < ROPE_HALF, -1.0, np.where(d < ROPE_DIM, 1.0, 0.0)),
                            dtype=F32).reshape(1, LANES)
    row = lambda width: pl.BlockSpec((None, ROW_TILE, width), lambda b, i: (b, i, 0))
    const = lambda shape: pl.BlockSpec(shape, lambda b, i: (0,) * len(shape))
    widths = (ATTN_WIDTH, 2 * KV_WIDTH, 4 * KV_WIDTH, SG_WIDTH, SG_WIDTH, EVEN_GATE_WIDTH)
    return pl.pallas_call(
        _even_in_kernel,
        grid=(batch, seq // ROW_TILE),
        in_specs=[row(D_MODEL), _mod_spec(mod, EVEN_LAYER, MOD_SHIFT), _mod_spec(mod, EVEN_LAYER, MOD_SCALE),
                  pl.BlockSpec((batch, ROW_TILE), lambda b, i: (0, i)),
                  const((1, LANES)), const((1, LANES)), _resident((D_MODEL, EVEN_IN_WIDTH))],
        out_specs=[row(w) for w in widths],
        out_shape=[jax.ShapeDtypeStruct((batch, seq, w), BF16) for w in widths],
        scratch_shapes=[pltpu.VMEM((D_MODEL, EVEN_IN_WIDTH), BF16)],
        compiler_params=_params("arbitrary", "arbitrary"),
        name="even_in_proj",
    )(x, mod, mod, positions, freq_lane, sign_lane, w_in)


def _even_mix_kernel(sink_ref, q_ref, k_ref, v_ref, su_ref, sv_ref, g_ref, x_ref, gate_ref,
                     avg_ref, avg2_ref, sgg_ref, sgb_ref, sgw_ref, sgbias_ref, wout_f32_ref, lng_ref, lnb_ref,
                     o_ref, y_scr, wout_ref):
    _cast_weight_once(wout_f32_ref, wout_ref)
    seq = k_ref.shape[0]
    n_blocks = seq // ATTN_BLOCK
    sub_blocks = MIX_TILE // ATTN_BLOCK
    tile = pl.program_id(1)
    lane = lax.broadcasted_iota(jnp.int32, (1, LANES), 1)
    left = lane < HEAD_DIM

    qi = lax.broadcasted_iota(jnp.int32, (ATTN_BLOCK, ATTN_BLOCK), 0)
    kj = lax.broadcasted_iota(jnp.int32, (ATTN_BLOCK, ATTN_BLOCK), 1)
    ones_left = jnp.broadcast_to(jnp.where(left, 1.0, 0.0).astype(BF16), (3 * ATTN_BLOCK, LANES))
    ones_right = jnp.broadcast_to(jnp.where(left, 0.0, 1.0).astype(BF16), (3 * ATTN_BLOCK, LANES))
    avg = avg_ref[...]
    sink_cols = [jnp.concatenate([jnp.full((ATTN_BLOCK, LANES), sink_ref[h] * LOG2_E, F32)
                                  for h in (4 * hk, 4 * hk + 2, 4 * hk + 1, 4 * hk + 3)], axis=0)
                 for hk in range(N_KV_HEADS)]

    def sub_block(j, carry):
        blk = tile * sub_blocks + j
        prev = jnp.maximum(blk - 1, 0)
        nxt = jnp.minimum(blk + 1, n_blocks - 1)
        rows = pl.ds(pl.multiple_of(j * ATTN_BLOCK, ATTN_BLOCK), ATTN_BLOCK)
        bias_prev = jnp.where((kj >= qi) & (blk > 0), 0.0, NEG_INF)
        bias_next = jnp.where((kj <= qi) & (blk < n_blocks - 1), 0.0, NEG_INF)
        bias_prev = jnp.concatenate([bias_prev] * Q_PER_KV, axis=0)
        bias_next = jnp.concatenate([bias_next] * Q_PER_KV, axis=0)

        def band(ref):
            return jnp.concatenate(
                [ref[pl.ds(pl.multiple_of(b * ATTN_BLOCK, ATTN_BLOCK), ATTN_BLOCK), :] for b in (prev, blk, nxt)],
                axis=0)

        k_band = band(k_ref)
        v_band = band(v_ref)
        n_sg = SG_WIDTH // LANES
        sg_cols = [slice(c * LANES, (c + 1) * LANES) for c in range(n_sg)]

        def silu_gate(cols):
            g_c = g_ref[rows, cols].astype(F32)
            return g_c * _sigmoid(g_c)

        scores = []
        for hk in range(N_KV_HEADS):
            q_cols = [q_ref[rows, (2 * hk + c) * LANES:(2 * hk + c + 1) * LANES] for c in range(2)]
            zero = jnp.zeros_like(q_cols[0])
            lhs = jnp.concatenate([jnp.where(left, qc, zero) for qc in q_cols]
                                  + [jnp.where(left, zero, qc) for qc in q_cols], axis=0)
            scores.append(lax.dot_general(lhs, k_band[:, hk * LANES:(hk + 1) * LANES], (((1,), (1,)), ((), ())),
                                          preferred_element_type=F32))
        sv_cols = [sv_ref[rows, cols] for cols in sg_cols]
        means = [jnp.dot(v_c, avg, preferred_element_type=F32) for v_c in sv_cols]

        probs, row_max = [], []
        for hk in range(N_KV_HEADS):
            s = scores[hk]
            s0 = s[:, 0:ATTN_BLOCK] + bias_prev
            s1 = s[:, ATTN_BLOCK:2 * ATTN_BLOCK]
            s2 = s[:, 2 * ATTN_BLOCK:3 * ATTN_BLOCK] + bias_next
            m = jnp.max(jnp.maximum(jnp.maximum(s0, s1), s2), axis=-1, keepdims=True)
            m = jnp.maximum(jnp.broadcast_to(m, sink_cols[hk].shape), sink_cols[hk])
            probs.append(jnp.concatenate([jnp.exp2(t - m).astype(BF16) for t in (s0, s1, s2)], axis=1))
            row_max.append(m)
        devs, sq_parts = [], []
        for v_c, mean in zip(sv_cols, means):
            dev = v_c.astype(F32) - mean
            sq = dev * dev
            sq_hi = sq.astype(BF16)
            sq_lo = (sq - sq_hi.astype(F32)).astype(BF16)
            devs.append(dev)
            sq_parts.append(jnp.concatenate([sq_hi, sq_lo], axis=1))

        pv = []
        for hk in range(N_KV_HEADS):
            p = probs[hk]
            lhs = jnp.concatenate(
                [jnp.concatenate([p[c * ATTN_BLOCK:(c + 1) * ATTN_BLOCK], p[(2 + c) * ATTN_BLOCK:(3 + c) * ATTN_BLOCK]],
                                 axis=1) for c in range(2)], axis=0)
            rhs = jnp.concatenate(
                [jnp.concatenate([v_band[:, (2 * hk) * LANES:(2 * hk + 1) * LANES], ones_left], axis=1),
                 jnp.concatenate([v_band[:, (2 * hk + 1) * LANES:(2 * hk + 2) * LANES], ones_right], axis=1)],
                axis=0)
            pv.append(jnp.dot(lhs, rhs, preferred_element_type=F32))
        variances = [jnp.dot(sq, avg2_ref[...], preferred_element_type=F32) for sq in sq_parts]

        for hk in range(N_KV_HEADS):
            for c in range(2):
                even, odd = slice(c * ATTN_BLOCK, (c + 1) * ATTN_BLOCK), slice((2 + c) * ATTN_BLOCK, (3 + c) * ATTN_BLOCK)
                sink_term = jnp.exp2(jnp.where(left, sink_cols[hk][even] - row_max[hk][even],
                                               sink_cols[hk][odd] - row_max[hk][odd]))
                o = pv[hk][c * ATTN_BLOCK:(c + 1) * ATTN_BLOCK]
                pair = o[:, 0:LANES] * (1.0 / (o[:, LANES:2 * LANES] + sink_term))
                cols = slice((2 * hk + c) * LANES, (2 * hk + c + 1) * LANES)
                y_scr[rows, cols] = (pair * silu_gate(cols)).astype(BF16)

        for c, cols in enumerate(sg_cols):
            vn = devs[c] * lax.rsqrt(variances[c] + LN_EPS) * sgg_ref[:, cols] + sgb_ref[:, cols]
            stacked = jnp.concatenate([jnp.where(left, vn, 0.0), jnp.where(left, 0.0, vn)],
                                      axis=0).astype(BF16)
            mixed = jnp.dot(sgw_ref[c], stacked, preferred_element_type=F32) + sgbias_ref[:, cols]
            y_sg = su_ref[rows, cols].astype(F32) * mixed
            gc = slice(ATTN_WIDTH + c * LANES, ATTN_WIDTH + (c + 1) * LANES)
            y_scr[rows, gc] = (y_sg * silu_gate(gc)).astype(BF16)
        return carry

    lax.fori_loop(0, sub_blocks, sub_block, 0, unroll=True)

    for r in range(2):
        half_rows = slice(r * (MIX_TILE // 2), (r + 1) * (MIX_TILE // 2))
        out = jnp.dot(y_scr[half_rows, :], wout_ref[...], preferred_element_type=F32)
        o_ref[half_rows, :] = _deepnorm_rows(x_ref[half_rows, :], _batch_row(gate_ref), out,
                                             lng_ref[EVEN_LAYER:EVEN_LAYER + 1, :], lnb_ref[EVEN_LAYER:EVEN_LAYER + 1, :])


def _even_mixer(q, k, v, su, sv, g, x, mod, sink, sg_ln_g, sg_ln_b, sg_w, sg_b, w_out, ln_g, ln_b):
    batch, seq, _ = x.shape
    lane = np.arange(LANES)
    avg = jnp.asarray((lane[:, None] // SG_GROUP_DIM == lane[None, :] // SG_GROUP_DIM) / SG_GROUP_DIM, dtype=BF16)
    w_pairs = sg_w.reshape(N_SG_GROUPS // 2, 2, SG_CHUNK, SG_CHUNK).transpose(0, 2, 1, 3)
    w_pairs = w_pairs.reshape(N_SG_GROUPS // 2, SG_CHUNK, 2 * SG_CHUNK).astype(BF16)
    bias_full = jnp.repeat(sg_b.T, SG_GROUP_DIM, axis=1)
    row = lambda width: pl.BlockSpec((None, MIX_TILE, width), lambda b, i: (b, i, 0))
    full_seq = lambda arr: pl.BlockSpec((None, seq, arr.shape[-1]), lambda b, i: (b, 0, 0))
    const = lambda shape: pl.BlockSpec(shape, lambda b, i: (0,) * len(shape))
    return pl.pallas_call(
        _even_mix_kernel,
        grid=(batch, seq // MIX_TILE),
        in_specs=[
            pl.BlockSpec(memory_space=pltpu.SMEM),
            row(ATTN_WIDTH), full_seq(k), full_seq(v), row(SG_WIDTH), row(SG_WIDTH), row(EVEN_GATE_WIDTH),
            row(D_MODEL), _mod_spec(mod, EVEN_LAYER, MOD_GATE),
            const((LANES, LANES)), const((2 * LANES, LANES)), const((1, SG_WIDTH)), const((1, SG_WIDTH)),
            const((N_SG_GROUPS // 2, SG_CHUNK, 2 * SG_CHUNK)), const((SG_CHUNK, SG_WIDTH)),
            _resident((EVEN_GATE_WIDTH, D_MODEL)), const(ln_g.shape), const(ln_b.shape),
        ],
        out_specs=row(D_MODEL),
        out_shape=jax.ShapeDtypeStruct((batch, seq, D_MODEL), F32),
        scratch_shapes=[pltpu.VMEM((MIX_TILE, EVEN_GATE_WIDTH), BF16),
                        pltpu.VMEM((EVEN_GATE_WIDTH, D_MODEL), BF16)],
        compiler_params=_params("arbitrary", "arbitrary"),
        name="even_mixer",
    )(sink, q, k, v, su, sv, g, x, mod, avg, jnp.concatenate([avg, avg], axis=0), sg_ln_g.reshape(1, -1), sg_ln_b.reshape(1, -1),
      w_pairs, bias_full, w_out, ln_g, ln_b)


def _odd_in_kernel(x_ref, shift_ref, scale_ref, w_f32_ref, xr_ref, w_ref):
    _cast_weight_once(w_f32_ref, w_ref)
    seg_len = xr_ref.shape[1]
    head_rows = seg_len * SEGS_PER_TILE
    xr_rows = xr_ref.reshape(RNN_HEADS * head_rows, LANES)
    halves = 2
    for r in range(halves):
        first = r * (SEGS_PER_TILE // halves)
        rows = slice(first * seg_len, (first + SEGS_PER_TILE // halves) * seg_len)
        h = (x_ref[rows, :] * (1.0 + _batch_row(scale_ref)) + _batch_row(shift_ref)).astype(BF16)
        p = jnp.dot(h, w_ref[...], preferred_element_type=F32)
        for s in range(SEGS_PER_TILE // halves):
            for hd in range(RNN_HEADS):
                xr_rows[pl.ds(hd * head_rows + first + s, seg_len, stride=SEGS_PER_TILE), :] = (
                    p[s * seg_len:(s + 1) * seg_len, hd * LANES:(hd + 1) * LANES])


def _odd_in_proj(x, mod, w_in):
    batch, seq, _ = x.shape
    seg_len = seq // N_SEG
    tile_rows = SEGS_PER_TILE * seg_len
    out_spec = pl.BlockSpec((None, RNN_HEADS, seg_len, SEGS_PER_TILE, RNN_HEAD_DIM), lambda b, i: (b, 0, 0, i, 0))
    out_shape = jax.ShapeDtypeStruct((batch, RNN_HEADS, seg_len, N_SEG, RNN_HEAD_DIM), F32)
    return pl.pallas_call(
        _odd_in_kernel,
        grid=(batch, seq // tile_rows),
        in_specs=[pl.BlockSpec((None, tile_rows, D_MODEL), lambda b, i: (b, i, 0)),
                  _mod_spec(mod, ODD_LAYER, MOD_SHIFT), _mod_spec(mod, ODD_LAYER, MOD_SCALE),
                  _resident((D_MODEL, RNN_WIDTH), col_block=0)],
        out_specs=out_spec,
        out_shape=out_shape,
        scratch_shapes=[pltpu.VMEM((D_MODEL, RNN_WIDTH), BF16)],
        compiler_params=_params("arbitrary", "arbitrary"),
        name="odd_in_proj",
    )(x, mod, mod, w_in)


def _rglru_kernel(x_ref, convw_ref, convb_ref, waug_ref, lam_ref, o_ref,
                  halo, a_f, b_f, a_b, b_b, edge):
    n_rows = x_ref.shape[0]
    seg_len = n_rows // N_SEG
    n_chunks = n_rows // SCAN_CHUNK
    left_rows = CONV_LEFT * N_SEG
    right_rows = (CONV_WIDTH - 1 - CONV_LEFT) * N_SEG
    seg = lax.broadcasted_iota(jnp.int32, (N_SEG, LANES), 0)

    def from_prev_segment(blk):
        return jnp.where(seg == 0, 0.0, pltpu.roll(blk, 1, 0))

    def from_next_segment(blk):
        return jnp.where(seg == N_SEG - 1, 0.0, pltpu.roll(blk, N_SEG - 1, 0))

    for t in range(CONV_LEFT):
        halo[0, pl.ds(t * N_SEG, N_SEG), :] = from_prev_segment(
            x_ref[pl.ds((seg_len - CONV_LEFT + t) * N_SEG, N_SEG), :])
    halo[0, pl.ds(left_rows, SCAN_CHUNK + right_rows), :] = x_ref[pl.ds(0, SCAN_CHUNK + right_rows), :]
    halo[1, pl.ds(0, left_rows + SCAN_CHUNK), :] = x_ref[pl.ds(n_rows - SCAN_CHUNK - left_rows, left_rows + SCAN_CHUNK), :]
    for t in range(CONV_WIDTH - 1 - CONV_LEFT):
        halo[1, pl.ds(left_rows + SCAN_CHUNK + t * N_SEG, N_SEG), :] = from_next_segment(
            x_ref[pl.ds(t * N_SEG, N_SEG), :])

    lam = lam_ref[...]
    softplus_neg = jnp.maximum(-lam, 0.0) + jnp.log1p(jnp.exp(-jnp.abs(lam)))
    half_rate = (0.5 * RG_LRU_C) * softplus_neg
    bias_lanes = jnp.broadcast_to(
        jnp.where(lax.broadcasted_iota(jnp.int32, (1, LANES), 1) < BIAS_ROWS, 1.0, 0.0).astype(BF16),
        (SCAN_CHUNK, LANES))

    def gates(base, tap):
        half_x = convb_ref[...] + sum(convw_ref[k:k + 1, :] * tap(k) for k in range(CONV_WIDTH))
        th = jnp.tanh(jnp.dot(jnp.concatenate([half_x.astype(BF16), bias_lanes], axis=1), waug_ref[...],
                              preferred_element_type=F32))
        for d, (a_ref, b_ref) in enumerate(((a_f, b_f), (a_b, b_b))):
            o = 2 * d * LANES
            neg_log_a = (th[:, o:o + LANES] + 1.0) * half_rate[d:d + 1, :]
            a = jnp.exp2(neg_log_a * (-LOG2_E))
            one_minus_a2 = jnp.tanh(neg_log_a) * (a * a + 1.0)
            root = one_minus_a2 * lax.rsqrt(jnp.maximum(one_minus_a2, F32_TINY))
            a_ref[pl.ds(base, SCAN_CHUNK), :] = a
            b_ref[pl.ds(base, SCAN_CHUNK), :] = root * ((th[:, o + LANES:o + 2 * LANES] + 1.0) * half_x)

    def inner_chunk(c, carry):
        base = pl.multiple_of(c * SCAN_CHUNK, SCAN_CHUNK)
        gates(base, lambda k: x_ref[pl.ds(base + (k - CONV_LEFT) * N_SEG, SCAN_CHUNK), :])
        return carry

    gates(0, lambda k: halo[0, pl.ds(k * N_SEG, SCAN_CHUNK), :])
    gates(n_rows - SCAN_CHUNK, lambda k: halo[1, pl.ds(k * N_SEG, SCAN_CHUNK), :])
    lax.fori_loop(1, n_chunks - 1, inner_chunk, 0, unroll=7)

    def scan_step(t, carry):
        h_f, p_f, h_b, p_b = carry
        rf = pl.ds(pl.multiple_of(t * N_SEG, N_SEG), N_SEG)
        rb = pl.ds(pl.multiple_of((seg_len - 1 - t) * N_SEG, N_SEG), N_SEG)
        a = a_f[rf, :]
        h_f = a * h_f + b_f[rf, :]
        p_f = a * p_f
        b_f[rf, :] = h_f
        a_f[rf, :] = p_f
        a = a_b[rb, :]
        h_b = a * h_b + b_b[rb, :]
        p_b = a * p_b
        b_b[rb, :] = h_b
        a_b[rb, :] = p_b
        return h_f, p_f, h_b, p_b

    zeros = jnp.zeros((N_SEG, LANES), F32)
    ones = jnp.ones((N_SEG, LANES), F32)
    h_f, p_f, h_b, p_b = lax.fori_loop(0, seg_len, scan_step, (zeros, ones, zeros, ones), unroll=4)

    edge[0] = h_f
    edge[1] = p_f
    edge[2] = h_b
    edge[3] = p_b
    enter_f = enter_b = jnp.zeros((1, LANES), F32)
    edge[4, 0:1, :] = enter_f
    edge[5, N_SEG - 1:N_SEG, :] = enter_b
    for s in range(1, N_SEG):
        enter_f = edge[0, s - 1:s, :] + edge[1, s - 1:s, :] * enter_f
        edge[4, s:s + 1, :] = enter_f
        r = N_SEG - 1 - s
        enter_b = edge[2, r + 1:r + 2, :] + edge[3, r + 1:r + 2, :] * enter_b
        edge[5, r:r + 1, :] = enter_b
    blocks = SCAN_CHUNK // N_SEG
    in_f = jnp.concatenate([edge[4]] * blocks, axis=0)
    in_b = jnp.concatenate([edge[5]] * blocks, axis=0)

    def out_chunk(c, carry):
        rows = pl.ds(pl.multiple_of(c * SCAN_CHUNK, SCAN_CHUNK), SCAN_CHUNK)
        o_ref[rows, :] = (b_f[rows, :] + a_f[rows, :] * in_f) + (b_b[rows, :] + a_b[rows, :] * in_b)
        return carry

    lax.fori_loop(0, n_rows // SCAN_CHUNK, out_chunk, 0, unroll=2)


def _rglru(xr, conv_w, conv_b, w_a, b_a, w_x, b_x, lam):
    batch, _, n_rows, _ = xr.shape
    w_cat = jnp.concatenate([w_a[0], w_x[0], w_a[1], w_x[1]], axis=-1).astype(BF16)
    per_head = lambda v: v.reshape(RNN_HEADS, 1, RNN_HEAD_DIM)
    half_b = 0.5 * jnp.concatenate([per_head(b_a[0]), per_head(b_x[0]), per_head(b_a[1]), per_head(b_x[1])],
                                   axis=-1)
    b_hi = half_b.astype(BF16)
    b_lo = (half_b - b_hi.astype(F32)).astype(BF16)
    w_aug = jnp.concatenate([w_cat, b_hi, b_lo,
                             jnp.zeros((RNN_HEADS, RNN_HEAD_DIM - BIAS_ROWS, 4 * RNN_HEAD_DIM), BF16)], axis=1)
    col = lambda rows: pl.BlockSpec((rows, RNN_HEAD_DIM), lambda b, h: (0, h))
    seq_col = pl.BlockSpec((None, None, n_rows, RNN_HEAD_DIM), lambda b, h: (b, h, 0, 0))
    halo_rows = SCAN_CHUNK + (CONV_WIDTH - 1) * N_SEG
    return pl.pallas_call(
        _rglru_kernel,
        grid=(batch, RNN_HEADS),
        in_specs=[seq_col, col(CONV_WIDTH), col(1),
                  pl.BlockSpec((None, 2 * RNN_HEAD_DIM, 4 * RNN_HEAD_DIM), lambda b, h: (h, 0, 0)),
                  col(2)],
        out_specs=seq_col,
        out_shape=jax.ShapeDtypeStruct((batch, RNN_HEADS, n_rows, RNN_HEAD_DIM), F32),
        scratch_shapes=[pltpu.VMEM((2, halo_rows, LANES), F32)]
                       + [pltpu.VMEM((n_rows, LANES), F32)] * 4
                       + [pltpu.VMEM((6, N_SEG, LANES), F32)],
        compiler_params=_params("parallel", "parallel"),
        name="rglru",
    )(xr, 0.5 * conv_w, 0.5 * conv_b.reshape(1, -1), w_aug, lam)


def _odd_out_kernel(y_ref, x_ref, shift_ref, scale_ref, gate_ref, wg_f32_ref, w_f32_ref, lng_ref, lnb_ref,
                    o_ref, wg_ref, w_ref):
    _cast_weight_once(wg_f32_ref, wg_ref)
    _cast_weight_once(w_f32_ref, w_ref)
    seg_len = y_ref.shape[1]
    head_rows = seg_len * SEGS_PER_TILE
    y_rows = y_ref.reshape(RNN_HEADS * head_rows, LANES)
    halves = 1
    for r in range(halves):
        first = r * (SEGS_PER_TILE // halves)
        rows = slice(first * seg_len, (first + SEGS_PER_TILE // halves) * seg_len)
        x = x_ref[rows, :]
        h = (x * (1.0 + _batch_row(scale_ref)) + _batch_row(shift_ref)).astype(BF16)
        g_all = jnp.dot(h, wg_ref[...], preferred_element_type=F32)
        gated = []
        for s in range(SEGS_PER_TILE // halves):
            y = jnp.concatenate([y_rows[pl.ds(hd * head_rows + first + s, seg_len, stride=SEGS_PER_TILE), :]
                                 for hd in range(RNN_HEADS)], axis=1)
            g = g_all[s * seg_len:(s + 1) * seg_len, :]
            gated.append((y * (g * _sigmoid(g))).astype(BF16))
        out = jnp.dot(jnp.concatenate(gated, axis=0), w_ref[...], preferred_element_type=F32)
        o_ref[rows, :] = _deepnorm_rows(x, _batch_row(gate_ref), out,
                                        lng_ref[ODD_LAYER:ODD_LAYER + 1, :], lnb_ref[ODD_LAYER:ODD_LAYER + 1, :])


def _odd_out_proj(y, x, mod, w_in, w_out, ln_g, ln_b):
    batch, seq, _ = x.shape
    seg_len = seq // N_SEG
    tile_rows = SEGS_PER_TILE * seg_len
    const = lambda shape: pl.BlockSpec(shape, lambda b, i: (0,) * len(shape))
    return pl.pallas_call(
        _odd_out_kernel,
        grid=(batch, seq // tile_rows),
        in_specs=[pl.BlockSpec((None, RNN_HEADS, seg_len, SEGS_PER_TILE, RNN_HEAD_DIM),
                               lambda b, i: (b, 0, 0, i, 0)),
                  pl.BlockSpec((None, tile_rows, D_MODEL), lambda b, i: (b, i, 0)),
                  _mod_spec(mod, ODD_LAYER, MOD_SHIFT), _mod_spec(mod, ODD_LAYER, MOD_SCALE),
                  _mod_spec(mod, ODD_LAYER, MOD_GATE),
                  _resident((D_MODEL, RNN_WIDTH), col_block=1), _resident((RNN_WIDTH, D_MODEL)),
                  const(ln_g.shape), const(ln_b.shape)],
        out_specs=pl.BlockSpec((None, tile_rows, D_MODEL), lambda b, i: (b, i, 0)),
        out_shape=jax.ShapeDtypeStruct((batch, seq, D_MODEL), F32),
        scratch_shapes=[pltpu.VMEM((D_MODEL, RNN_WIDTH), BF16), pltpu.VMEM((RNN_WIDTH, D_MODEL), BF16)],
        compiler_params=_params("arbitrary", "arbitrary"),
        name="odd_out_proj",
    )(y, x, mod, mod, mod, w_in, w_out, ln_g, ln_b)


def kernel(x, c, positions, ada_w, ada_b, ln_g, ln_b, ev_w_in, ev_w_out, ev_sink, ev_sg_ln_g, ev_sg_ln_b,
           ev_sg_w, ev_sg_b, od_w_in, od_conv_w, od_conv_b, od_w_a, od_b_a, od_w_x, od_b_x, od_lam, od_w_out):
    batch, seq, d_model = x.shape
    seg_len = seq // N_SEG
    assert d_model == D_MODEL and seq % ROW_TILE == 0 and seq % MIX_TILE == 0 and seq == N_SEG * seg_len
    assert N_SEG % SEGS_PER_TILE == 0 and seq % SCAN_CHUNK == 0 and SCAN_CHUNK % N_SEG == 0
    assert ada_w.shape[0] == DEPTH == 2

    mod = _modulation(c, ada_w, ada_b)

    q, k, v, su, sv, g = _even_in_proj(x, mod, positions, ev_w_in[0])
    x1 = _even_mixer(q, k, v, su, sv, g, x, mod, ev_sink[0], ev_sg_ln_g[0], ev_sg_ln_b[0],
                     ev_sg_w[0], ev_sg_b[0], ev_w_out[0], ln_g, ln_b)

    xr = _odd_in_proj(x1, mod, od_w_in[0])
    y = _rglru(xr.reshape(batch, RNN_HEADS, seq, RNN_HEAD_DIM), od_conv_w[0], od_conv_b[0], od_w_a[0],
               od_b_a[0], od_w_x[0], od_b_x[0], od_lam[0])
    return _odd_out_proj(y.reshape(batch, RNN_HEADS, seg_len, N_SEG, RNN_HEAD_DIM), x1, mod, od_w_in[0],
                         od_w_out[0], ln_g, ln_b)
```

```python
import math

import numpy as np
import jax
import jax.numpy as jnp
from jax import lax
from jax.experimental import pallas as pl
from jax.experimental.pallas import tpu as pltpu

D_MODEL = 1024
DEPTH = 2
HEAD_DIM = 64
N_Q_HEADS = 8
N_KV_HEADS = 2
Q_PER_KV = N_Q_HEADS // N_KV_HEADS
ATTN_WIDTH = N_Q_HEADS * HEAD_DIM
KV_WIDTH = N_KV_HEADS * HEAD_DIM
ATTN_BLOCK = 128
ROPE_DIM = HEAD_DIM // 4
ROPE_HALF = ROPE_DIM // 2
ROPE_THETA = 500000.0
N_SG_GROUPS = 8
SG_GROUP_DIM = 64
SG_WIDTH = N_SG_GROUPS * SG_GROUP_DIM
SG_CHUNK = 128
EVEN_GATE_WIDTH = ATTN_WIDTH + SG_WIDTH
EVEN_IN_WIDTH = ATTN_WIDTH + 2 * KV_WIDTH + 2 * SG_WIDTH + EVEN_GATE_WIDTH
RNN_WIDTH = D_MODEL
RNN_HEADS = 8
RNN_HEAD_DIM = RNN_WIDTH // RNN_HEADS
CONV_WIDTH = 4
CONV_LEFT = 2
RG_LRU_C = 8.0
DEEPNORM_ALPHA = (2 * DEPTH) ** 0.25
LN_EPS = 1e-5
NEG_INF = -1e30
LOG2_E = math.log2(math.e)
F32_TINY = float(np.finfo(np.float32).tiny)

LANES = 128
SUBLANES = 8
VMEM_LIMIT = 56 * 1024 * 1024

ROW_TILE = 1024
MIX_TILE = 512
N_SEG = 32
SEGS_PER_TILE = SUBLANES
SCAN_CHUNK = 256
WEIGHT_CAST_ROWS = 128
BIAS_ROWS = 2

BF16 = jnp.bfloat16
F32 = jnp.float32


def _sigmoid(x):
    return 0.5 * jnp.tanh(0.5 * x) + 0.5


def _layer_norm_rows(z, g, b, eps=LN_EPS):
    mu = jnp.mean(z, axis=-1, keepdims=True)
    d = z - mu
    var = jnp.mean(d * d, axis=-1, keepdims=True)
    return d * lax.rsqrt(var + eps) * g + b


def _deepnorm_rows(x, gate, y, g, b):
    return _layer_norm_rows(x + (gate * (1.0 / DEEPNORM_ALPHA)) * y, g, b, LN_EPS / DEEPNORM_ALPHA ** 2)


def _params(*semantics):
    return pltpu.CompilerParams(dimension_semantics=semantics, vmem_limit_bytes=VMEM_LIMIT)


def _cast_weight_once(w_ref, w_bf16):
    @pl.when((pl.program_id(0) == 0) & (pl.program_id(1) == 0))
    def _():
        for r in range(0, w_ref.shape[0], WEIGHT_CAST_ROWS):
            w_bf16[r:r + WEIGHT_CAST_ROWS, :] = w_ref[r:r + WEIGHT_CAST_ROWS, :].astype(BF16)


def _resident(shape, col_block=0):
    return pl.BlockSpec(shape, lambda b, i: (0,) * (len(shape) - 1) + (col_block,), pipeline_mode=pl.Buffered(1))


def _mod_kernel(c_ref, w_ref, b_ref, o_ref):
    c = c_ref[...]
    cond = c * _sigmoid(c)
    pad_rows = o_ref.shape[0] - cond.shape[0]
    if pad_rows:
        cond = jnp.concatenate([cond, jnp.zeros((pad_rows, cond.shape[1]), F32)], axis=0)
    bias = b_ref[pl.ds(pl.program_id(0), 1), :]
    o_ref[...] = jnp.dot(cond.astype(BF16), w_ref[...].astype(BF16), preferred_element_type=F32) + bias


def _modulation(c, ada_w, ada_b):
    batch = c.shape[0]
    rows = -(-batch // SUBLANES) * SUBLANES
    n_col = 3 * D_MODEL // D_MODEL
    return pl.pallas_call(
        _mod_kernel,
        grid=(DEPTH, n_col),
        in_specs=[
            pl.BlockSpec((batch, D_MODEL), lambda l, j: (0, 0)),
            pl.BlockSpec((None, D_MODEL, D_MODEL), lambda l, j: (l, 0, j)),
            pl.BlockSpec((DEPTH, D_MODEL), lambda l, j: (0, j)),
        ],
        out_specs=pl.BlockSpec((None, rows, D_MODEL), lambda l, j: (l, 0, j)),
        out_shape=jax.ShapeDtypeStruct((DEPTH, rows, 3 * D_MODEL), F32),
        compiler_params=_params("parallel", "parallel"),
        name="adaln_mod",
    )(c, ada_w, ada_b)


MOD_SHIFT, MOD_SCALE, MOD_GATE = range(3)
EVEN_LAYER, ODD_LAYER = 0, 1


def _mod_spec(mod, layer, part):
    return pl.BlockSpec((None, mod.shape[1], D_MODEL), lambda b, i: (layer, 0, part))


def _batch_row(ref):
    return ref[pl.ds(pl.program_id(0), 1), :]


def _even_in_kernel(x_ref, shift_ref, scale_ref, pos_ref, freq_ref, sign_ref, w_f32_ref,
                    q_ref, k_ref, v_ref, su_ref, sv_ref, g_ref, w_ref):
    _cast_weight_once(w_f32_ref, w_ref)
    lane = lax.broadcasted_iota(jnp.int32, (1, LANES), 1)
    first_half = (lane % HEAD_DIM) < ROPE_HALF
    left_head = lane < HEAD_DIM
    diagonal = lax.broadcasted_iota(jnp.int32, (LANES, LANES), 0) == lax.broadcasted_iota(jnp.int32, (LANES, LANES), 1)
    batch_ids = lax.broadcasted_iota(jnp.int32, (pos_ref.shape[0], LANES), 0)
    q_scale = HEAD_DIM ** -0.5 * LOG2_E

    half_rows = x_ref.shape[0] // 2
    for r in range(2):
        rows = slice(r * half_rows, (r + 1) * half_rows)
        h = (x_ref[rows, :] * (1.0 + _batch_row(scale_ref)) + _batch_row(shift_ref)).astype(BF16)
        p = jnp.dot(h, w_ref[...], preferred_element_type=F32)

        pos_cols = []
        for j in range(r * half_rows // LANES, (r + 1) * half_rows // LANES):
            pos_all = pos_ref[:, j * LANES:(j + 1) * LANES].astype(F32)
            pos_row = jnp.sum(jnp.where(batch_ids == pl.program_id(0), pos_all, 0.0), axis=0, keepdims=True)
            pos_cols.append(jnp.sum(jnp.where(diagonal, pos_row, 0.0), axis=1, keepdims=True))
        ang = jnp.concatenate(pos_cols, axis=0) * freq_ref[...]
        cs = jnp.cos(ang)
        sn = jnp.sin(ang) * sign_ref[...]

        def rotary(t):
            partner = jnp.where(first_half, pltpu.roll(t, LANES - ROPE_HALF, 1), pltpu.roll(t, ROPE_HALF, 1))
            return t * cs + partner * sn

        for j in range(ATTN_WIDTH // LANES):
            q_ref[rows, j * LANES:(j + 1) * LANES] = (
                rotary(p[:, j * LANES:(j + 1) * LANES]) * q_scale).astype(BF16)
        o = ATTN_WIDTH
        k = rotary(p[:, o:o + KV_WIDTH])
        k_sw = pltpu.roll(k, HEAD_DIM, 1)
        k_ref[rows, 0:LANES] = jnp.where(left_head, k, k_sw).astype(BF16)
        k_ref[rows, LANES:2 * LANES] = jnp.where(left_head, k_sw, k).astype(BF16)
        o += KV_WIDTH
        v = p[:, o:o + KV_WIDTH]
        v_sw = pltpu.roll(v, HEAD_DIM, 1)
        v_ref[rows, 0 * LANES:1 * LANES] = jnp.where(left_head, v, 0.0).astype(BF16)
        v_ref[rows, 1 * LANES:2 * LANES] = jnp.where(left_head, 0.0, v_sw).astype(BF16)
        v_ref[rows, 2 * LANES:3 * LANES] = jnp.where(left_head, v_sw, 0.0).astype(BF16)
        v_ref[rows, 3 * LANES:4 * LANES] = jnp.where(left_head, 0.0, v).astype(BF16)
        o += KV_WIDTH
        su_ref[rows, :] = p[:, o:o + SG_WIDTH].astype(BF16)
        o += SG_WIDTH
        sv_ref[rows, :] = p[:, o:o + SG_WIDTH].astype(BF16)
        o += SG_WIDTH
        g_ref[rows, :] = p[:, o:o + EVEN_GATE_WIDTH].astype(BF16)


def _even_in_proj(x, mod, positions, w_in):
    batch, seq, _ = x.shape
    d = np.arange(LANES) % HEAD_DIM
    inv_freq = np.power(np.float64(ROPE_THETA), -np.arange(ROPE_HALF) / ROPE_HALF)
    freq_lane = jnp.asarray(np.where(d < ROPE_DIM, inv_freq[d % ROPE_HALF], 0.0), dtype=F32).reshape(1, LANES)
    sign_lane = jnp.asarray(np.where(d < ROPE_HALF, -1.0, np.where(d < ROPE_DIM, 1.0, 0.0)),
                            dtype=F32).reshape(1, LANES)
    row = lambda width: pl.BlockSpec((None, ROW_TILE, width), lambda b, i: (b, i, 0))
    const = lambda shape: pl.BlockSpec(shape, lambda b, i: (0,) * len(shape))
    widths = (ATTN_WIDTH, 2 * KV_WIDTH, 4 * KV_WIDTH, SG_WIDTH, SG_WIDTH, EVEN_GATE_WIDTH)
    return pl.pallas_call(
        _even_in_kernel,
        grid=(batch, seq // ROW_TILE),
        in_specs=[row(D_MODEL), _mod_spec(mod, EVEN_LAYER, MOD_SHIFT), _mod_spec(mod, EVEN_LAYER, MOD_SCALE),
                  pl.BlockSpec((batch, ROW_TILE), lambda b, i: (0, i)),
                  const((1, LANES)), const((1, LANES)), _resident((D_MODEL, EVEN_IN_WIDTH))],
        out_specs=[row(w) for w in widths],
        out_shape=[jax.ShapeDtypeStruct((batch, seq, w), BF16) for w in widths],
        scratch_shapes=[pltpu.VMEM((D_MODEL, EVEN_IN_WIDTH), BF16)],
        compiler_params=_params("arbitrary", "arbitrary"),
        name="even_in_proj",
    )(x, mod, mod, positions, freq_lane, sign_lane, w_in)


def _even_mix_kernel(sink_ref, q_ref, k_ref, v_ref, su_ref, sv_ref, g_ref, x_ref, gate_ref,
                     avg_ref, avg2_ref, sgg_ref, sgb_ref, sgw_ref, sgbias_ref, wout_f32_ref, lng_ref, lnb_ref,
                     o_ref, y_scr, wout_ref):
    _cast_weight_once(wout_f32_ref, wout_ref)
    seq = k_ref.shape[0]
    n_blocks = seq // ATTN_BLOCK
    sub_blocks = MIX_TILE // ATTN_BLOCK
    tile = pl.program_id(1)
    lane = lax.broadcasted_iota(jnp.int32, (1, LANES), 1)
    left = lane < HEAD_DIM

    qi = lax.broadcasted_iota(jnp.int32, (ATTN_BLOCK, ATTN_BLOCK), 0)
    kj = lax.broadcasted_iota(jnp.int32, (ATTN_BLOCK, ATTN_BLOCK), 1)
    ones_left = jnp.broadcast_to(jnp.where(left, 1.0, 0.0).astype(BF16), (3 * ATTN_BLOCK, LANES))
    ones_right = jnp.broadcast_to(jnp.where(left, 0.0, 1.0).astype(BF16), (3 * ATTN_BLOCK, LANES))
    avg = avg_ref[...]
    sink_cols = [jnp.concatenate([jnp.full((ATTN_BLOCK, LANES), sink_ref[h] * LOG2_E, F32)
                                  for h in (4 * hk, 4 * hk + 2, 4 * hk + 1, 4 * hk + 3)], axis=0)
                 for hk in range(N_KV_HEADS)]

    def sub_block(j, carry):
        blk = tile * sub_blocks + j
        prev = jnp.maximum(blk - 1, 0)
        nxt = jnp.minimum(blk + 1, n_blocks - 1)
        rows = pl.ds(pl.multiple_of(j * ATTN_BLOCK, ATTN_BLOCK), ATTN_BLOCK)
        bias_prev = jnp.where((kj >= qi) & (blk > 0), 0.0, NEG_INF)
        bias_next = jnp.where((kj <= qi) & (blk < n_blocks - 1), 0.0, NEG_INF)
        bias_prev = jnp.concatenate([bias_prev] * Q_PER_KV, axis=0)
        bias_next = jnp.concatenate([bias_next] * Q_PER_KV, axis=0)

        def band(ref):
            return jnp.concatenate(
                [ref[pl.ds(pl.multiple_of(b * ATTN_BLOCK, ATTN_BLOCK), ATTN_BLOCK), :] for b in (prev, blk, nxt)],
                axis=0)

        k_band = band(k_ref)
        v_band = band(v_ref)
        n_sg = SG_WIDTH // LANES
        sg_cols = [slice(c * LANES, (c + 1) * LANES) for c in range(n_sg)]

        def silu_gate(cols):
            g_c = g_ref[rows, cols].astype(F32)
            return g_c * _sigmoid(g_c)

        scores = []
        for hk in range(N_KV_HEADS):
            q_cols = [q_ref[rows, (2 * hk + c) * LANES:(2 * hk + c + 1) * LANES] for c in range(2)]
            zero = jnp.zeros_like(q_cols[0])
            lhs = jnp.concatenate([jnp.where(left, qc, zero) for qc in q_cols]
                                  + [jnp.where(left, zero, qc) for qc in q_cols], axis=0)
            scores.append(lax.dot_general(lhs, k_band[:, hk * LANES:(hk + 1) * LANES], (((1,), (1,)), ((), ())),
                                          preferred_element_type=F32))
        sv_cols = [sv_ref[rows, cols] for cols in sg_cols]
        means = [jnp.dot(v_c, avg, preferred_element_type=F32) for v_c in sv_cols]

        probs, row_max = [], []
        for hk in range(N_KV_HEADS):
            s = scores[hk]
            s0 = s[:, 0:ATTN_BLOCK] + bias_prev
            s1 = s[:, ATTN_BLOCK:2 * ATTN_BLOCK]
            s2 = s[:, 2 * ATTN_BLOCK:3 * ATTN_BLOCK] + bias_next
            m = jnp.max(jnp.maximum(jnp.maximum(s0, s1), s2), axis=-1, keepdims=True)
            m = jnp.maximum(jnp.broadcast_to(m, sink_cols[hk].shape), sink_cols[hk])
            probs.append(jnp.concatenate([jnp.exp2(t - m).astype(BF16) for t in (s0, s1, s2)], axis=1))
            row_max.append(m)
        devs, sq_parts = [], []
        for v_c, mean in zip(sv_cols, means):
            dev = v_c.astype(F32) - mean
            sq = dev * dev
            sq_hi = sq.astype(BF16)
            sq_lo = (sq - sq_hi.astype(F32)).astype(BF16)
            devs.append(dev)
            sq_parts.append(jnp.concatenate([sq_hi, sq_lo], axis=1))

        pv = []
        for hk in range(N_KV_HEADS):
            p = probs[hk]
            lhs = jnp.concatenate(
                [jnp.concatenate([p[c * ATTN_BLOCK:(c + 1) * ATTN_BLOCK], p[(2 + c) * ATTN_BLOCK:(3 + c) * ATTN_BLOCK]],
                                 axis=1) for c in range(2)], axis=0)
            rhs = jnp.concatenate(
                [jnp.concatenate([v_band[:, (2 * hk) * LANES:(2 * hk + 1) * LANES], ones_left], axis=1),
                 jnp.concatenate([v_band[:, (2 * hk + 1) * LANES:(2 * hk + 2) * LANES], ones_right], axis=1)],
                axis=0)
            pv.append(jnp.dot(lhs, rhs, preferred_element_type=F32))
        variances = [jnp.dot(sq, avg2_ref[...], preferred_element_type=F32) for sq in sq_parts]

        for hk in range(N_KV_HEADS):
            for c in range(2):
                even, odd = slice(c * ATTN_BLOCK, (c + 1) * ATTN_BLOCK), slice((2 + c) * ATTN_BLOCK, (3 + c) * ATTN_BLOCK)
                sink_term = jnp.exp2(jnp.where(left, sink_cols[hk][even] - row_max[hk][even],
                                               sink_cols[hk][odd] - row_max[hk][odd]))
                o = pv[hk][c * ATTN_BLOCK:(c + 1) * ATTN_BLOCK]
                pair = o[:, 0:LANES] * (1.0 / (o[:, LANES:2 * LANES] + sink_term))
                cols = slice((2 * hk + c) * LANES, (2 * hk + c + 1) * LANES)
                y_scr[rows, cols] = (pair * silu_gate(cols)).astype(BF16)

        for c, cols in enumerate(sg_cols):
            vn = devs[c] * lax.rsqrt(variances[c] + LN_EPS) * sgg_ref[:, cols] + sgb_ref[:, cols]
            stacked = jnp.concatenate([jnp.where(left, vn, 0.0), jnp.where(left, 0.0, vn)],
                                      axis=0).astype(BF16)
            mixed = jnp.dot(sgw_ref[c], stacked, preferred_element_type=F32) + sgbias_ref[:, cols]
            y_sg = su_ref[rows, cols].astype(F32) * mixed
            gc = slice(ATTN_WIDTH + c * LANES, ATTN_WIDTH + (c + 1) * LANES)
            y_scr[rows, gc] = (y_sg * silu_gate(gc)).astype(BF16)
        return carry

    lax.fori_loop(0, sub_blocks, sub_block, 0, unroll=True)

    for r in range(2):
        half_rows = slice(r * (MIX_TILE // 2), (r + 1) * (MIX_TILE // 2))
        out = jnp.dot(y_scr[half_rows, :], wout_ref[...], preferred_element_type=F32)
        o_ref[half_rows, :] = _deepnorm_rows(x_ref[half_rows, :], _batch_row(gate_ref), out,
                                             lng_ref[EVEN_LAYER:EVEN_LAYER + 1, :], lnb_ref[EVEN_LAYER:EVEN_LAYER + 1, :])


def _even_mixer(q, k, v, su, sv, g, x, mod, sink, sg_ln_g, sg_ln_b, sg_w, sg_b, w_out, ln_g, ln_b):
    batch, seq, _ = x.shape
    lane = np.arange(LANES)
    avg = jnp.asarray((lane[:, None] // SG_GROUP_DIM == lane[None, :] // SG_GROUP_DIM) / SG_GROUP_DIM, dtype=BF16)
    w_pairs = sg_w.reshape(N_SG_GROUPS // 2, 2, SG_CHUNK, SG_CHUNK).transpose(0, 2, 1, 3)
    w_pairs = w_pairs.reshape(N_SG_GROUPS // 2, SG_CHUNK, 2 * SG_CHUNK).astype(BF16)
    bias_full = jnp.repeat(sg_b.T, SG_GROUP_DIM, axis=1)
    row = lambda width: pl.BlockSpec((None, MIX_TILE, width), lambda b, i: (b, i, 0))
    full_seq = lambda arr: pl.BlockSpec((None, seq, arr.shape[-1]), lambda b, i: (b, 0, 0))
    const = lambda shape: pl.BlockSpec(shape, lambda b, i: (0,) * len(shape))
    return pl.pallas_call(
        _even_mix_kernel,
        grid=(batch, seq // MIX_TILE),
        in_specs=[
            pl.BlockSpec(memory_space=pltpu.SMEM),
            row(ATTN_WIDTH), full_seq(k), full_seq(v), row(SG_WIDTH), row(SG_WIDTH), row(EVEN_GATE_WIDTH),
            row(D_MODEL), _mod_spec(mod, EVEN_LAYER, MOD_GATE),
            const((LANES, LANES)), const((2 * LANES, LANES)), const((1, SG_WIDTH)), const((1, SG_WIDTH)),
            const((N_SG_GROUPS // 2, SG_CHUNK, 2 * SG_CHUNK)), const((SG_CHUNK, SG_WIDTH)),
            _resident((EVEN_GATE_WIDTH, D_MODEL)), const(ln_g.shape), const(ln_b.shape),
        ],
        out_specs=row(D_MODEL),
        out_shape=jax.ShapeDtypeStruct((batch, seq, D_MODEL), F32),
        scratch_shapes=[pltpu.VMEM((MIX_TILE, EVEN_GATE_WIDTH), BF16),
                        pltpu.VMEM((EVEN_GATE_WIDTH, D_MODEL), BF16)],
        compiler_params=_params("arbitrary", "arbitrary"),
        name="even_mixer",
    )(sink, q, k, v, su, sv, g, x, mod, avg, jnp.concatenate([avg, avg], axis=0), sg_ln_g.reshape(1, -1), sg_ln_b.reshape(1, -1),
      w_pairs, bias_full, w_out, ln_g, ln_b)


def _odd_in_kernel(x_ref, shift_ref, scale_ref, w_f32_ref, xr_ref, w_ref):
    _cast_weight_once(w_f32_ref, w_ref)
    seg_len = xr_ref.shape[1]
    head_rows = seg_len * SEGS_PER_TILE
    xr_rows = xr_ref.reshape(RNN_HEADS * head_rows, LANES)
    halves = 2
    for r in range(halves):
        first = r * (SEGS_PER_TILE // halves)
        rows = slice(first * seg_len, (first + SEGS_PER_TILE // halves) * seg_len)
        h = (x_ref[rows, :] * (1.0 + _batch_row(scale_ref)) + _batch_row(shift_ref)).astype(BF16)
        p = jnp.dot(h, w_ref[...], preferred_element_type=F32)
        for s in range(SEGS_PER_TILE // halves):
            for hd in range(RNN_HEADS):
                xr_rows[pl.ds(hd * head_rows + first + s, seg_len, stride=SEGS_PER_TILE), :] = (
                    p[s * seg_len:(s + 1) * seg_len, hd * LANES:(hd + 1) * LANES])


def _odd_in_proj(x, mod, w_in):
    batch, seq, _ = x.shape
    seg_len = seq // N_SEG
    tile_rows = SEGS_PER_TILE * seg_len
    out_spec = pl.BlockSpec((None, RNN_HEADS, seg_len, SEGS_PER_TILE, RNN_HEAD_DIM), lambda b, i: (b, 0, 0, i, 0))
    out_shape = jax.ShapeDtypeStruct((batch, RNN_HEADS, seg_len, N_SEG, RNN_HEAD_DIM), F32)
    return pl.pallas_call(
        _odd_in_kernel,
        grid=(batch, seq // tile_rows),
        in_specs=[pl.BlockSpec((None, tile_rows, D_MODEL), lambda b, i: (b, i, 0)),
                  _mod_spec(mod, ODD_LAYER, MOD_SHIFT), _mod_spec(mod, ODD_LAYER, MOD_SCALE),
                  _resident((D_MODEL, RNN_WIDTH), col_block=0)],
        out_specs=out_spec,
        out_shape=out_shape,
        scratch_shapes=[pltpu.VMEM((D_MODEL, RNN_WIDTH), BF16)],
        compiler_params=_params("arbitrary", "arbitrary"),
        name="odd_in_proj",
    )(x, mod, mod, w_in)


def _rglru_kernel(x_ref, convw_ref, convb_ref, waug_ref, lam_ref, o_ref,
                  halo, a_f, b_f, a_b, b_b, edge):
    n_rows = x_ref.shape[0]
    seg_len = n_rows // N_SEG
    n_chunks = n_rows // SCAN_CHUNK
    left_rows = CONV_LEFT * N_SEG
    right_rows = (CONV_WIDTH - 1 - CONV_LEFT) * N_SEG
    seg = lax.broadcasted_iota(jnp.int32, (N_SEG, LANES), 0)

    def from_prev_segment(blk):
        return jnp.where(seg == 0, 0.0, pltpu.roll(blk, 1, 0))

    def from_next_segment(blk):
        return jnp.where(seg == N_SEG - 1, 0.0, pltpu.roll(blk, N_SEG - 1, 0))

    for t in range(CONV_LEFT):
        halo[0, pl.ds(t * N_SEG, N_SEG), :] = from_prev_segment(
            x_ref[pl.ds((seg_len - CONV_LEFT + t) * N_SEG, N_SEG), :])
    halo[0, pl.ds(left_rows, SCAN_CHUNK + right_rows), :] = x_ref[pl.ds(0, SCAN_CHUNK + right_rows), :]
    halo[1, pl.ds(0, left_rows + SCAN_CHUNK), :] = x_ref[pl.ds(n_rows - SCAN_CHUNK - left_rows, left_rows + SCAN_CHUNK), :]
    for t in range(CONV_WIDTH - 1 - CONV_LEFT):
        halo[1, pl.ds(left_rows + SCAN_CHUNK + t * N_SEG, N_SEG), :] = from_next_segment(
            x_ref[pl.ds(t * N_SEG, N_SEG), :])

    lam = lam_ref[...]
    softplus_neg = jnp.maximum(-lam, 0.0) + jnp.log1p(jnp.exp(-jnp.abs(lam)))
    half_rate = (0.5 * RG_LRU_C) * softplus_neg
    bias_lanes = jnp.broadcast_to(
        jnp.where(lax.broadcasted_iota(jnp.int32, (1, LANES), 1) < BIAS_ROWS, 1.0, 0.0).astype(BF16),
        (SCAN_CHUNK, LANES))

    def gates(base, tap):
        half_x = convb_ref[...] + sum(convw_ref[k:k + 1, :] * tap(k) for k in range(CONV_WIDTH))
        th = jnp.tanh(jnp.dot(jnp.concatenate([half_x.astype(BF16), bias_lanes], axis=1), waug_ref[...],
                              preferred_element_type=F32))
        for d, (a_ref, b_ref) in enumerate(((a_f, b_f), (a_b, b_b))):
            o = 2 * d * LANES
            neg_log_a = (th[:, o:o + LANES] + 1.0) * half_rate[d:d + 1, :]
            a = jnp.exp2(neg_log_a * (-LOG2_E))
            one_minus_a2 = jnp.tanh(neg_log_a) * (a * a + 1.0)
            root = one_minus_a2 * lax.rsqrt(jnp.maximum(one_minus_a2, F32_TINY))
            a_ref[pl.ds(base, SCAN_CHUNK), :] = a
            b_ref[pl.ds(base, SCAN_CHUNK), :] = root * ((th[:, o + LANES:o + 2 * LANES] + 1.0) * half_x)

    def inner_chunk(c, carry):
        base = pl.multiple_of(c * SCAN_CHUNK, SCAN_CHUNK)
        gates(base, lambda k: x_ref[pl.ds(base + (k - CONV_LEFT) * N_SEG, SCAN_CHUNK), :])
        return carry

    gates(0, lambda k: halo[0, pl.ds(k * N_SEG, SCAN_CHUNK), :])
    gates(n_rows - SCAN_CHUNK, lambda k: halo[1, pl.ds(k * N_SEG, SCAN_CHUNK), :])
    lax.fori_loop(1, n_chunks - 1, inner_chunk, 0, unroll=7)

    def scan_step(t, carry):
        h_f, p_f, h_b, p_b = carry
        rf = pl.ds(pl.multiple_of(t * N_SEG, N_SEG), N_SEG)
        rb = pl.ds(pl.multiple_of((seg_len - 1 - t) * N_SEG, N_SEG), N_SEG)
        a = a_f[rf, :]
        h_f = a * h_f + b_f[rf, :]
        p_f = a * p_f
        b_f[rf, :] = h_f
        a_f[rf, :] = p_f
        a = a_b[rb, :]
        h_b = a * h_b + b_b[rb, :]
        p_b = a * p_b
        b_b[rb, :] = h_b
        a_b[rb, :] = p_b
        return h_f, p_f, h_b, p_b

    zeros = jnp.zeros((N_SEG, LANES), F32)
    ones = jnp.ones((N_SEG, LANES), F32)
    h_f, p_f, h_b, p_b = lax.fori_loop(0, seg_len, scan_step, (zeros, ones, zeros, ones), unroll=4)

    edge[0] = h_f
    edge[1] = p_f
    edge[2] = h_b
    edge[3] = p_b
    enter_f = enter_b = jnp.zeros((1, LANES), F32)
    edge[4, 0:1, :] = enter_f
    edge[5, N_SEG - 1:N_SEG, :] = enter_b
    for s in range(1, N_SEG):
        enter_f = edge[0, s - 1:s, :] + edge[1, s - 1:s, :] * enter_f
        edge[4, s:s + 1, :] = enter_f
        r = N_SEG - 1 - s
        enter_b = edge[2, r + 1:r + 2, :] + edge[3, r + 1:r + 2, :] * enter_b
        edge[5, r:r + 1, :] = enter_b
    blocks = SCAN_CHUNK // N_SEG
    in_f = jnp.concatenate([edge[4]] * blocks, axis=0)
    in_b = jnp.concatenate([edge[5]] * blocks, axis=0)

    def out_chunk(c, carry):
        rows = pl.ds(pl.multiple_of(c * SCAN_CHUNK, SCAN_CHUNK), SCAN_CHUNK)
        o_ref[rows, :] = (b_f[rows, :] + a_f[rows, :] * in_f) + (b_b[rows, :] + a_b[rows, :] * in_b)
        return carry

    lax.fori_loop(0, n_rows // SCAN_CHUNK, out_chunk, 0, unroll=2)


def _rglru(xr, conv_w, conv_b, w_a, b_a, w_x, b_x, lam):
    batch, _, n_rows, _ = xr.shape
    w_cat = jnp.concatenate([w_a[0], w_x[0], w_a[1], w_x[1]], axis=-1).astype(BF16)
    per_head = lambda v: v.reshape(RNN_HEADS, 1, RNN_HEAD_DIM)
    half_b = 0.5 * jnp.concatenate([per_head(b_a[0]), per_head(b_x[0]), per_head(b_a[1]), per_head(b_x[1])],
                                   axis=-1)
    b_hi = half_b.astype(BF16)
    b_lo = (half_b - b_hi.astype(F32)).astype(BF16)
    w_aug = jnp.concatenate([w_cat, b_hi, b_lo,
                             jnp.zeros((RNN_HEADS, RNN_HEAD_DIM - BIAS_ROWS, 4 * RNN_HEAD_DIM), BF16)], axis=1)
    col = lambda rows: pl.BlockSpec((rows, RNN_HEAD_DIM), lambda b, h: (0, h))
    seq_col = pl.BlockSpec((None, None, n_rows, RNN_HEAD_DIM), lambda b, h: (b, h, 0, 0))
    halo_rows = SCAN_CHUNK + (CONV_WIDTH - 1) * N_SEG
    return pl.pallas_call(
        _rglru_kernel,
        grid=(batch, RNN_HEADS),
        in_specs=[seq_col, col(CONV_WIDTH), col(1),
                  pl.BlockSpec((None, 2 * RNN_HEAD_DIM, 4 * RNN_HEAD_DIM), lambda b, h: (h, 0, 0)),
                  col(2)],
        out_specs=seq_col,
        out_shape=jax.ShapeDtypeStruct((batch, RNN_HEADS, n_rows, RNN_HEAD_DIM), F32),
        scratch_shapes=[pltpu.VMEM((2, halo_rows, LANES), F32)]
                       + [pltpu.VMEM((n_rows, LANES), F32)] * 4
                       + [pltpu.VMEM((6, N_SEG, LANES), F32)],
        compiler_params=_params("parallel", "parallel"),
        name="rglru",
    )(xr, 0.5 * conv_w, 0.5 * conv_b.reshape(1, -1), w_aug, lam)


def _odd_out_kernel(y_ref, x_ref, shift_ref, scale_ref, gate_ref, wg_f32_ref, w_f32_ref, lng_ref, lnb_ref,
                    o_ref, wg_ref, w_ref):
    _cast_weight_once(wg_f32_ref, wg_ref)
    _cast_weight_once(w_f32_ref, w_ref)
    seg_len = y_ref.shape[1]
    head_rows = seg_len * SEGS_PER_TILE
    y_rows = y_ref.reshape(RNN_HEADS * head_rows, LANES)
    x = x_ref[...]
    h = (x * (1.0 + _batch_row(scale_ref)) + _batch_row(shift_ref)).astype(BF16)
    g_all = jnp.dot(h, wg_ref[...], preferred_element_type=F32)
    gated = []
    for s in range(SEGS_PER_TILE):
        y = jnp.concatenate([y_rows[pl.ds(hd * head_rows + s, seg_len, stride=SEGS_PER_TILE), :]
                             for hd in range(RNN_HEADS)], axis=1)
        g = g_all[s * seg_len:(s + 1) * seg_len, :]
        gated.append((y * (g * _sigmoid(g))).astype(BF16))
    out = jnp.dot(jnp.concatenate(gated, axis=0), w_ref[...], preferred_element_type=F32)
    o_ref[...] = _deepnorm_rows(x, _batch_row(gate_ref), out,
                                lng_ref[ODD_LAYER:ODD_LAYER + 1, :], lnb_ref[ODD_LAYER:ODD_LAYER + 1, :])


def _odd_out_proj(y, x, mod, w_in, w_out, ln_g, ln_b):
    batch, seq, _ = x.shape
    seg_len = seq // N_SEG
    tile_rows = SEGS_PER_TILE * seg_len
    const = lambda shape: pl.BlockSpec(shape, lambda b, i: (0,) * len(shape))
    return pl.pallas_call(
        _odd_out_kernel,
        grid=(batch, seq // tile_rows),
        in_specs=[pl.BlockSpec((None, RNN_HEADS, seg_len, SEGS_PER_TILE, RNN_HEAD_DIM),
                               lambda b, i: (b, 0, 0, i, 0)),
                  pl.BlockSpec((None, tile_rows, D_MODEL), lambda b, i: (b, i, 0)),
                  _mod_spec(mod, ODD_LAYER, MOD_SHIFT), _mod_spec(mod, ODD_LAYER, MOD_SCALE),
                  _mod_spec(mod, ODD_LAYER, MOD_GATE),
                  _resident((D_MODEL, RNN_WIDTH), col_block=1), _resident((RNN_WIDTH, D_MODEL)),
                  const(ln_g.shape), const(ln_b.shape)],
        out_specs=pl.BlockSpec((None, tile_rows, D_MODEL), lambda b, i: (b, i, 0)),
        out_shape=jax.ShapeDtypeStruct((batch, seq, D_MODEL), F32),
        scratch_shapes=[pltpu.VMEM((D_MODEL, RNN_WIDTH), BF16), pltpu.VMEM((RNN_WIDTH, D_MODEL), BF16)],
        compiler_params=_params("arbitrary", "arbitrary"),
        name="odd_out_proj",
    )(y, x, mod, mod, mod, w_in, w_out, ln_g, ln_b)


def kernel(x, c, positions, ada_w, ada_b, ln_g, ln_b, ev_w_in, ev_w_out, ev_sink, ev_sg_ln_g, ev_sg_ln_b,
           ev_sg_w, ev_sg_b, od_w_in, od_conv_w, od_conv_b, od_w_a, od_b_a, od_w_x, od_b_x, od_lam, od_w_out):
    batch, seq, d_model = x.shape
    seg_len = seq // N_SEG
    assert d_model == D_MODEL and seq % ROW_TILE == 0 and seq % MIX_TILE == 0 and seq == N_SEG * seg_len
    assert N_SEG % SEGS_PER_TILE == 0 and seq % SCAN_CHUNK == 0 and SCAN_CHUNK % N_SEG == 0
    assert ada_w.shape[0] == DEPTH == 2

    mod = _modulation(c, ada_w, ada_b)

    q, k, v, su, sv, g = _even_in_proj(x, mod, positions, ev_w_in[0])
    x1 = _even_mixer(q, k, v, su, sv, g, x, mod, ev_sink[0], ev_sg_ln_g[0], ev_sg_ln_b[0],
                     ev_sg_w[0], ev_sg_b[0], ev_w_out[0], ln_g, ln_b)

    xr = _odd_in_proj(x1, mod, od_w_in[0])
    y = _rglru(xr.reshape(batch, RNN_HEADS, seq, RNN_HEAD_DIM), od_conv_w[0], od_conv_b[0], od_w_a[0],
               od_b_a[0], od_w_x[0], od_b_x[0], od_lam[0])
    return _odd_out_proj(y.reshape(batch, RNN_HEADS, seg_len, N_SEG, RNN_HEAD_DIM), x1, mod, od_w_in[0],
                         od_w_out[0], ln_g, ln_b)
```

```python
import math

import numpy as np
import jax
import jax.numpy as jnp
from jax import lax
from jax.experimental import pallas as pl
from jax.experimental.pallas import tpu as pltpu

D_MODEL = 1024
DEPTH = 2
HEAD_DIM = 64
N_Q_HEADS = 8
N_KV_HEADS = 2
Q_PER_KV = N_Q_HEADS // N_KV_HEADS
ATTN_WIDTH = N_Q_HEADS * HEAD_DIM
KV_WIDTH = N_KV_HEADS * HEAD_DIM
ATTN_BLOCK = 128
ROPE_DIM = HEAD_DIM // 4
ROPE_HALF = ROPE_DIM // 2
ROPE_THETA = 500000.0
N_SG_GROUPS = 8
SG_GROUP_DIM = 64
SG_WIDTH = N_SG_GROUPS * SG_GROUP_DIM
SG_CHUNK = 128
EVEN_GATE_WIDTH = ATTN_WIDTH + SG_WIDTH
EVEN_IN_WIDTH = ATTN_WIDTH + 2 * KV_WIDTH + 2 * SG_WIDTH + EVEN_GATE_WIDTH
RNN_WIDTH = D_MODEL
RNN_HEADS = 8
RNN_HEAD_DIM = RNN_WIDTH // RNN_HEADS
CONV_WIDTH = 4
CONV_LEFT = 2
RG_LRU_C = 8.0
DEEPNORM_ALPHA = (2 * DEPTH) ** 0.25
LN_EPS = 1e-5
NEG_INF = -1e30
LOG2_E = math.log2(math.e)
F32_TINY = float(np.finfo(np.float32).tiny)

LANES = 128
SUBLANES = 8
VMEM_LIMIT = 56 * 1024 * 1024

ROW_TILE = 1024
MIX_TILE = 512
N_SEG = 32
SEGS_PER_TILE = SUBLANES
SCAN_CHUNK = 512
WEIGHT_CAST_ROWS = 128
BIAS_ROWS = 2

BF16 = jnp.bfloat16
F32 = jnp.float32


def _sigmoid(x):
    return 0.5 * jnp.tanh(0.5 * x) + 0.5


def _layer_norm_rows(z, g, b, eps=LN_EPS):
    mu = jnp.mean(z, axis=-1, keepdims=True)
    d = z - mu
    var = jnp.mean(d * d, axis=-1, keepdims=True)
    return d * lax.rsqrt(var + eps) * g + b


def _deepnorm_rows(x, gate, y, g, b):
    return _layer_norm_rows(x + (gate * (1.0 / DEEPNORM_ALPHA)) * y, g, b, LN_EPS / DEEPNORM_ALPHA ** 2)


def _params(*semantics):
    return pltpu.CompilerParams(dimension_semantics=semantics, vmem_limit_bytes=VMEM_LIMIT)


def _cast_weight_once(w_ref, w_bf16):
    @pl.when((pl.program_id(0) == 0) & (pl.program_id(1) == 0))
    def _():
        for r in range(0, w_ref.shape[0], WEIGHT_CAST_ROWS):
            w_bf16[r:r + WEIGHT_CAST_ROWS, :] = w_ref[r:r + WEIGHT_CAST_ROWS, :].astype(BF16)


def _resident(shape, col_block=0):
    return pl.BlockSpec(shape, lambda b, i: (0,) * (len(shape) - 1) + (col_block,), pipeline_mode=pl.Buffered(1))


def _mod_kernel(c_ref, w_ref, b_ref, o_ref):
    c = c_ref[...]
    cond = c * _sigmoid(c)
    pad_rows = o_ref.shape[0] - cond.shape[0]
    if pad_rows:
        cond = jnp.concatenate([cond, jnp.zeros((pad_rows, cond.shape[1]), F32)], axis=0)
    bias = b_ref[pl.ds(pl.program_id(0), 1), :]
    o_ref[...] = jnp.dot(cond.astype(BF16), w_ref[...].astype(BF16), preferred_element_type=F32) + bias


def _modulation(c, ada_w, ada_b):
    batch = c.shape[0]
    rows = -(-batch // SUBLANES) * SUBLANES
    n_col = 3 * D_MODEL // D_MODEL
    return pl.pallas_call(
        _mod_kernel,
        grid=(DEPTH, n_col),
        in_specs=[
            pl.BlockSpec((batch, D_MODEL), lambda l, j: (0, 0)),
            pl.BlockSpec((None, D_MODEL, D_MODEL), lambda l, j: (l, 0, j)),
            pl.BlockSpec((DEPTH, D_MODEL), lambda l, j: (0, j)),
        ],
        out_specs=pl.BlockSpec((None, rows, D_MODEL), lambda l, j: (l, 0, j)),
        out_shape=jax.ShapeDtypeStruct((DEPTH, rows, 3 * D_MODEL), F32),
        compiler_params=_params("parallel", "parallel"),
        name="adaln_mod",
    )(c, ada_w, ada_b)


MOD_SHIFT, MOD_SCALE, MOD_GATE = range(3)
EVEN_LAYER, ODD_LAYER = 0, 1


def _mod_spec(mod, layer, part):
    return pl.BlockSpec((None, mod.shape[1], D_MODEL), lambda b, i: (layer, 0, part))


def _batch_row(ref):
    return ref[pl.ds(pl.program_id(0), 1), :]


def _even_in_kernel(x_ref, shift_ref, scale_ref, pos_ref, freq_ref, sign_ref, w_f32_ref,
                    q_ref, k_ref, v_ref, su_ref, sv_ref, g_ref, w_ref):
    _cast_weight_once(w_f32_ref, w_ref)
    lane = lax.broadcasted_iota(jnp.int32, (1, LANES), 1)
    first_half = (lane % HEAD_DIM) < ROPE_HALF
    left_head = lane < HEAD_DIM
    diagonal = lax.broadcasted_iota(jnp.int32, (LANES, LANES), 0) == lax.broadcasted_iota(jnp.int32, (LANES, LANES), 1)
    batch_ids = lax.broadcasted_iota(jnp.int32, (pos_ref.shape[0], LANES), 0)
    q_scale = HEAD_DIM ** -0.5 * LOG2_E

    half_rows = x_ref.shape[0] // 2
    for r in range(2):
        rows = slice(r * half_rows, (r + 1) * half_rows)
        h = (x_ref[rows, :] * (1.0 + _batch_row(scale_ref)) + _batch_row(shift_ref)).astype(BF16)
        p = jnp.dot(h, w_ref[...], preferred_element_type=F32)

        pos_cols = []
        for j in range(r * half_rows // LANES, (r + 1) * half_rows // LANES):
            pos_all = pos_ref[:, j * LANES:(j + 1) * LANES].astype(F32)
            pos_row = jnp.sum(jnp.where(batch_ids == pl.program_id(0), pos_all, 0.0), axis=0, keepdims=True)
            pos_cols.append(jnp.sum(jnp.where(diagonal, pos_row, 0.0), axis=1, keepdims=True))
        ang = jnp.concatenate(pos_cols, axis=0) * freq_ref[...]
        cs = jnp.cos(ang)
        sn = jnp.sin(ang) * sign_ref[...]

        def rotary(t):
            partner = jnp.where(first_half, pltpu.roll(t, LANES - ROPE_HALF, 1), pltpu.roll(t, ROPE_HALF, 1))
            return t * cs + partner * sn

        for j in range(ATTN_WIDTH // LANES):
            q_ref[rows, j * LANES:(j + 1) * LANES] = (
                rotary(p[:, j * LANES:(j + 1) * LANES]) * q_scale).astype(BF16)
        o = ATTN_WIDTH
        k = rotary(p[:, o:o + KV_WIDTH])
        k_sw = pltpu.roll(k, HEAD_DIM, 1)
        k_ref[rows, 0:LANES] = jnp.where(left_head, k, k_sw).astype(BF16)
        k_ref[rows, LANES:2 * LANES] = jnp.where(left_head, k_sw, k).astype(BF16)
        o += KV_WIDTH
        v = p[:, o:o + KV_WIDTH]
        v_sw = pltpu.roll(v, HEAD_DIM, 1)
        v_ref[rows, 0 * LANES:1 * LANES] = jnp.where(left_head, v, 0.0).astype(BF16)
        v_ref[rows, 1 * LANES:2 * LANES] = jnp.where(left_head, 0.0, v_sw).astype(BF16)
        v_ref[rows, 2 * LANES:3 * LANES] = jnp.where(left_head, v_sw, 0.0).astype(BF16)
        v_ref[rows, 3 * LANES:4 * LANES] = jnp.where(left_head, 0.0, v).astype(BF16)
        o += KV_WIDTH
        su_ref[rows, :] = p[:, o:o + SG_WIDTH].astype(BF16)
        o += SG_WIDTH
        sv_ref[rows, :] = p[:, o:o + SG_WIDTH].astype(BF16)
        o += SG_WIDTH
        g_ref[rows, :] = p[:, o:o + EVEN_GATE_WIDTH].astype(BF16)


def _even_in_proj(x, mod, positions, w_in):
    batch, seq, _ = x.shape
    d = np.arange(LANES) % HEAD_DIM
    inv_freq = np.power(np.float64(ROPE_THETA), -np.arange(ROPE_HALF) / ROPE_HALF)
    freq_lane = jnp.asarray(np.where(d < ROPE_DIM, inv_freq[d % ROPE_HALF], 0.0), dtype=F32).reshape(1, LANES)
    sign_lane = jnp.asarray(np.where(d < ROPE_HALF, -1.0, np.where(d < ROPE_DIM, 1.0, 0.0)),
                            dtype=F32).reshape(1, LANES)
    row = lambda width: pl.BlockSpec((None, ROW_TILE, width), lambda b, i: (b, i, 0))
    const = lambda shape: pl.BlockSpec(shape, lambda b, i: (0,) * len(shape))
    widths = (ATTN_WIDTH, 2 * KV_WIDTH, 4 * KV_WIDTH, SG_WIDTH, SG_WIDTH, EVEN_GATE_WIDTH)
    return pl.pallas_call(
        _even_in_kernel,
        grid=(batch, seq // ROW_TILE),
        in_specs=[row(D_MODEL), _mod_spec(mod, EVEN_LAYER, MOD_SHIFT), _mod_spec(mod, EVEN_LAYER, MOD_SCALE),
                  pl.BlockSpec((batch, ROW_TILE), lambda b, i: (0, i)),
                  const((1, LANES)), const((1, LANES)), _resident((D_MODEL, EVEN_IN_WIDTH))],
        out_specs=[row(w) for w in widths],
        out_shape=[jax.ShapeDtypeStruct((batch, seq, w), BF16) for w in widths],
        scratch_shapes=[pltpu.VMEM((D_MODEL, EVEN_IN_WIDTH), BF16)],
        compiler_params=_params("arbitrary", "arbitrary"),
        name="even_in_proj",
    )(x, mod, mod, positions, freq_lane, sign_lane, w_in)


def _even_mix_kernel(sink_ref, q_ref, k_ref, v_ref, su_ref, sv_ref, g_ref, x_ref, gate_ref,
                     avg_ref, avg2_ref, sgg_ref, sgb_ref, sgw_ref, sgbias_ref, wout_f32_ref, lng_ref, lnb_ref,
                     o_ref, y_scr, wout_ref):
    _cast_weight_once(wout_f32_ref, wout_ref)
    seq = k_ref.shape[0]
    n_blocks = seq // ATTN_BLOCK
    sub_blocks = MIX_TILE // ATTN_BLOCK
    tile = pl.program_id(1)
    lane = lax.broadcasted_iota(jnp.int32, (1, LANES), 1)
    left = lane < HEAD_DIM

    qi = lax.broadcasted_iota(jnp.int32, (ATTN_BLOCK, ATTN_BLOCK), 0)
    kj = lax.broadcasted_iota(jnp.int32, (ATTN_BLOCK, ATTN_BLOCK), 1)
    ones_left = jnp.broadcast_to(jnp.where(left, 1.0, 0.0).astype(BF16), (3 * ATTN_BLOCK, LANES))
    ones_right = jnp.broadcast_to(jnp.where(left, 0.0, 1.0).astype(BF16), (3 * ATTN_BLOCK, LANES))
    avg = avg_ref[...]
    sink_cols = [jnp.concatenate([jnp.full((ATTN_BLOCK, LANES), sink_ref[h] * LOG2_E, F32)
                                  for h in (4 * hk, 4 * hk + 2, 4 * hk + 1, 4 * hk + 3)], axis=0)
                 for hk in range(N_KV_HEADS)]

    def sub_block(j, carry):
        blk = tile * sub_blocks + j
        prev = jnp.maximum(blk - 1, 0)
        nxt = jnp.minimum(blk + 1, n_blocks - 1)
        rows = pl.ds(pl.multiple_of(j * ATTN_BLOCK, ATTN_BLOCK), ATTN_BLOCK)
        bias_prev = jnp.where((kj >= qi) & (blk > 0), 0.0, NEG_INF)
        bias_next = jnp.where((kj <= qi) & (blk < n_blocks - 1), 0.0, NEG_INF)
        bias_prev = jnp.concatenate([bias_prev] * Q_PER_KV, axis=0)
        bias_next = jnp.concatenate([bias_next] * Q_PER_KV, axis=0)

        def band(ref):
            return jnp.concatenate(
                [ref[pl.ds(pl.multiple_of(b * ATTN_BLOCK, ATTN_BLOCK), ATTN_BLOCK), :] for b in (prev, blk, nxt)],
                axis=0)

        k_band = band(k_ref)
        v_band = band(v_ref)
        n_sg = SG_WIDTH // LANES
        sg_cols = [slice(c * LANES, (c + 1) * LANES) for c in range(n_sg)]

        def silu_gate(cols):
            g_c = g_ref[rows, cols].astype(F32)
            return g_c * _sigmoid(g_c)

        scores = []
        for hk in range(N_KV_HEADS):
            q_cols = [q_ref[rows, (2 * hk + c) * LANES:(2 * hk + c + 1) * LANES] for c in range(2)]
            zero = jnp.zeros_like(q_cols[0])
            lhs = jnp.concatenate([jnp.where(left, qc, zero) for qc in q_cols]
                                  + [jnp.where(left, zero, qc) for qc in q_cols], axis=0)
            scores.append(lax.dot_general(lhs, k_band[:, hk * LANES:(hk + 1) * LANES], (((1,), (1,)), ((), ())),
                                          preferred_element_type=F32))
        sv_cols = [sv_ref[rows, cols] for cols in sg_cols]
        means = [jnp.dot(v_c, avg, preferred_element_type=F32) for v_c in sv_cols]

        probs, row_max = [], []
        for hk in range(N_KV_HEADS):
            s = scores[hk]
            s0 = s[:, 0:ATTN_BLOCK] + bias_prev
            s1 = s[:, ATTN_BLOCK:2 * ATTN_BLOCK]
            s2 = s[:, 2 * ATTN_BLOCK:3 * ATTN_BLOCK] + bias_next
            m = jnp.max(jnp.maximum(jnp.maximum(s0, s1), s2), axis=-1, keepdims=True)
            m = jnp.maximum(jnp.broadcast_to(m, sink_cols[hk].shape), sink_cols[hk])
            probs.append(jnp.concatenate([jnp.exp2(t - m).astype(BF16) for t in (s0, s1, s2)], axis=1))
            row_max.append(m)
        devs, sq_parts = [], []
        for v_c, mean in zip(sv_cols, means):
            dev = v_c.astype(F32) - mean
            sq = dev * dev
            sq_hi = sq.astype(BF16)
            sq_lo = (sq - sq_hi.astype(F32)).astype(BF16)
            devs.append(dev)
            sq_parts.append(jnp.concatenate([sq_hi, sq_lo], axis=1))

        pv = []
        for hk in range(N_KV_HEADS):
            p = probs[hk]
            lhs = jnp.concatenate(
                [jnp.concatenate([p[c * ATTN_BLOCK:(c + 1) * ATTN_BLOCK], p[(2 + c) * ATTN_BLOCK:(3 + c) * ATTN_BLOCK]],
                                 axis=1) for c in range(2)], axis=0)
            rhs = jnp.concatenate(
                [jnp.concatenate([v_band[:, (2 * hk) * LANES:(2 * hk + 1) * LANES], ones_left], axis=1),
                 jnp.concatenate([v_band[:, (2 * hk + 1) * LANES:(2 * hk + 2) * LANES], ones_right], axis=1)],
                axis=0)
            pv.append(jnp.dot(lhs, rhs, preferred_element_type=F32))
        variances = [jnp.dot(sq, avg2_ref[...], preferred_element_type=F32) for sq in sq_parts]

        for hk in range(N_KV_HEADS):
            for c in range(2):
                even, odd = slice(c * ATTN_BLOCK, (c + 1) * ATTN_BLOCK), slice((2 + c) * ATTN_BLOCK, (3 + c) * ATTN_BLOCK)
                sink_term = jnp.exp2(jnp.where(left, sink_cols[hk][even] - row_max[hk][even],
                                               sink_cols[hk][odd] - row_max[hk][odd]))
                o = pv[hk][c * ATTN_BLOCK:(c + 1) * ATTN_BLOCK]
                pair = o[:, 0:LANES] * (1.0 / (o[:, LANES:2 * LANES] + sink_term))
                cols = slice((2 * hk + c) * LANES, (2 * hk + c + 1) * LANES)
                y_scr[rows, cols] = (pair * silu_gate(cols)).astype(BF16)

        for c, cols in enumerate(sg_cols):
            vn = devs[c] * lax.rsqrt(variances[c] + LN_EPS) * sgg_ref[:, cols] + sgb_ref[:, cols]
            stacked = jnp.concatenate([jnp.where(left, vn, 0.0), jnp.where(left, 0.0, vn)],
                                      axis=0).astype(BF16)
            mixed = jnp.dot(sgw_ref[c], stacked, preferred_element_type=F32) + sgbias_ref[:, cols]
            y_sg = su_ref[rows, cols].astype(F32) * mixed
            gc = slice(ATTN_WIDTH + c * LANES, ATTN_WIDTH + (c + 1) * LANES)
            y_scr[rows, gc] = (y_sg * silu_gate(gc)).astype(BF16)
        return carry

    lax.fori_loop(0, sub_blocks, sub_block, 0, unroll=True)

    for r in range(2):
        half_rows = slice(r * (MIX_TILE // 2), (r + 1) * (MIX_TILE // 2))
        out = jnp.dot(y_scr[half_rows, :], wout_ref[...], preferred_element_type=F32)
        o_ref[half_rows, :] = _deepnorm_rows(x_ref[half_rows, :], _batch_row(gate_ref), out,
                                             lng_ref[EVEN_LAYER:EVEN_LAYER + 1, :], lnb_ref[EVEN_LAYER:EVEN_LAYER + 1, :])


def _even_mixer(q, k, v, su, sv, g, x, mod, sink, sg_ln_g, sg_ln_b, sg_w, sg_b, w_out, ln_g, ln_b):
    batch, seq, _ = x.shape
    lane = np.arange(LANES)
    avg = jnp.asarray((lane[:, None] // SG_GROUP_DIM == lane[None, :] // SG_GROUP_DIM) / SG_GROUP_DIM, dtype=BF16)
    w_pairs = sg_w.reshape(N_SG_GROUPS // 2, 2, SG_CHUNK, SG_CHUNK).transpose(0, 2, 1, 3)
    w_pairs = w_pairs.reshape(N_SG_GROUPS // 2, SG_CHUNK, 2 * SG_CHUNK).astype(BF16)
    bias_full = jnp.repeat(sg_b.T, SG_GROUP_DIM, axis=1)
    row = lambda width: pl.BlockSpec((None, MIX_TILE, width), lambda b, i: (b, i, 0))
    full_seq = lambda arr: pl.BlockSpec((None, seq, arr.shape[-1]), lambda b, i: (b, 0, 0))
    const = lambda shape: pl.BlockSpec(shape, lambda b, i: (0,) * len(shape))
    return pl.pallas_call(
        _even_mix_kernel,
        grid=(batch, seq // MIX_TILE),
        in_specs=[
            pl.BlockSpec(memory_space=pltpu.SMEM),
            row(ATTN_WIDTH), full_seq(k), full_seq(v), row(SG_WIDTH), row(SG_WIDTH), row(EVEN_GATE_WIDTH),
            row(D_MODEL), _mod_spec(mod, EVEN_LAYER, MOD_GATE),
            const((LANES, LANES)), const((2 * LANES, LANES)), const((1, SG_WIDTH)), const((1, SG_WIDTH)),
            const((N_SG_GROUPS // 2, SG_CHUNK, 2 * SG_CHUNK)), const((SG_CHUNK, SG_WIDTH)),
            _resident((EVEN_GATE_WIDTH, D_MODEL)), const(ln_g.shape), const(ln_b.shape),
        ],
        out_specs=row(D_MODEL),
        out_shape=jax.ShapeDtypeStruct((batch, seq, D_MODEL), F32),
        scratch_shapes=[pltpu.VMEM((MIX_TILE, EVEN_GATE_WIDTH), BF16),
                        pltpu.VMEM((EVEN_GATE_WIDTH, D_MODEL), BF16)],
        compiler_params=_params("arbitrary", "arbitrary"),
        name="even_mixer",
    )(sink, q, k, v, su, sv, g, x, mod, avg, jnp.concatenate([avg, avg], axis=0), sg_ln_g.reshape(1, -1), sg_ln_b.reshape(1, -1),
      w_pairs, bias_full, w_out, ln_g, ln_b)


def _odd_in_kernel(x_ref, shift_ref, scale_ref, w_f32_ref, xr_ref, w_ref):
    _cast_weight_once(w_f32_ref, w_ref)
    seg_len = xr_ref.shape[1]
    head_rows = seg_len * SEGS_PER_TILE
    xr_rows = xr_ref.reshape(RNN_HEADS * head_rows, LANES)
    halves = 2
    for r in range(halves):
        first = r * (SEGS_PER_TILE // halves)
        rows = slice(first * seg_len, (first + SEGS_PER_TILE // halves) * seg_len)
        h = (x_ref[rows, :] * (1.0 + _batch_row(scale_ref)) + _batch_row(shift_ref)).astype(BF16)
        p = jnp.dot(h, w_ref[...], preferred_element_type=F32)
        for s in range(SEGS_PER_TILE // halves):
            for hd in range(RNN_HEADS):
                xr_rows[pl.ds(hd * head_rows + first + s, seg_len, stride=SEGS_PER_TILE), :] = (
                    p[s * seg_len:(s + 1) * seg_len, hd * LANES:(hd + 1) * LANES])


def _odd_in_proj(x, mod, w_in):
    batch, seq, _ = x.shape
    seg_len = seq // N_SEG
    tile_rows = SEGS_PER_TILE * seg_len
    out_spec = pl.BlockSpec((None, RNN_HEADS, seg_len, SEGS_PER_TILE, RNN_HEAD_DIM), lambda b, i: (b, 0, 0, i, 0))
    out_shape = jax.ShapeDtypeStruct((batch, RNN_HEADS, seg_len, N_SEG, RNN_HEAD_DIM), F32)
    return pl.pallas_call(
        _odd_in_kernel,
        grid=(batch, seq // tile_rows),
        in_specs=[pl.BlockSpec((None, tile_rows, D_MODEL), lambda b, i: (b, i, 0)),
                  _mod_spec(mod, ODD_LAYER, MOD_SHIFT), _mod_spec(mod, ODD_LAYER, MOD_SCALE),
                  _resident((D_MODEL, RNN_WIDTH), col_block=0)],
        out_specs=out_spec,
        out_shape=out_shape,
        scratch_shapes=[pltpu.VMEM((D_MODEL, RNN_WIDTH), BF16)],
        compiler_params=_params("arbitrary", "arbitrary"),
        name="odd_in_proj",
    )(x, mod, mod, w_in)


def _rglru_kernel(x_ref, convw_ref, convb_ref, waug_ref, lam_ref, o_ref,
                  halo, a_f, b_f, a_b, b_b, edge):
    n_rows = x_ref.shape[0]
    seg_len = n_rows // N_SEG
    n_chunks = n_rows // SCAN_CHUNK
    left_rows = CONV_LEFT * N_SEG
    right_rows = (CONV_WIDTH - 1 - CONV_LEFT) * N_SEG
    seg = lax.broadcasted_iota(jnp.int32, (N_SEG, LANES), 0)

    def from_prev_segment(blk):
        return jnp.where(seg == 0, 0.0, pltpu.roll(blk, 1, 0))

    def from_next_segment(blk):
        return jnp.where(seg == N_SEG - 1, 0.0, pltpu.roll(blk, N_SEG - 1, 0))

    for t in range(CONV_LEFT):
        halo[0, pl.ds(t * N_SEG, N_SEG), :] = from_prev_segment(
            x_ref[pl.ds((seg_len - CONV_LEFT + t) * N_SEG, N_SEG), :])
    halo[0, pl.ds(left_rows, SCAN_CHUNK + right_rows), :] = x_ref[pl.ds(0, SCAN_CHUNK + right_rows), :]
    halo[1, pl.ds(0, left_rows + SCAN_CHUNK), :] = x_ref[pl.ds(n_rows - SCAN_CHUNK - left_rows, left_rows + SCAN_CHUNK), :]
    for t in range(CONV_WIDTH - 1 - CONV_LEFT):
        halo[1, pl.ds(left_rows + SCAN_CHUNK + t * N_SEG, N_SEG), :] = from_next_segment(
            x_ref[pl.ds(t * N_SEG, N_SEG), :])

    lam = lam_ref[...]
    softplus_neg = jnp.maximum(-lam, 0.0) + jnp.log1p(jnp.exp(-jnp.abs(lam)))
    half_rate = (0.5 * RG_LRU_C) * softplus_neg
    bias_lanes = jnp.broadcast_to(
        jnp.where(lax.broadcasted_iota(jnp.int32, (1, LANES), 1) < BIAS_ROWS, 1.0, 0.0).astype(BF16),
        (SCAN_CHUNK, LANES))

    def gates(base, tap):
        half_x = convb_ref[...] + sum(convw_ref[k:k + 1, :] * tap(k) for k in range(CONV_WIDTH))
        th = jnp.tanh(jnp.dot(jnp.concatenate([half_x.astype(BF16), bias_lanes], axis=1), waug_ref[...],
                              preferred_element_type=F32))
        for d, (a_ref, b_ref) in enumerate(((a_f, b_f), (a_b, b_b))):
            o = 2 * d * LANES
            neg_log_a = (th[:, o:o + LANES] + 1.0) * half_rate[d:d + 1, :]
            a = jnp.exp2(neg_log_a * (-LOG2_E))
            one_minus_a2 = jnp.tanh(neg_log_a) * (a * a + 1.0)
            root = one_minus_a2 * lax.rsqrt(jnp.maximum(one_minus_a2, F32_TINY))
            a_ref[pl.ds(base, SCAN_CHUNK), :] = a
            b_ref[pl.ds(base, SCAN_CHUNK), :] = root * ((th[:, o + LANES:o + 2 * LANES] + 1.0) * half_x)

    def inner_chunk(c, carry):
        base = pl.multiple_of(c * SCAN_CHUNK, SCAN_CHUNK)
        gates(base, lambda k: x_ref[pl.ds(base + (k - CONV_LEFT) * N_SEG, SCAN_CHUNK), :])
        return carry

    gates(0, lambda k: halo[0, pl.ds(k * N_SEG, SCAN_CHUNK), :])
    gates(n_rows - SCAN_CHUNK, lambda k: halo[1, pl.ds(k * N_SEG, SCAN_CHUNK), :])
    lax.fori_loop(1, n_chunks - 1, inner_chunk, 0, unroll=3)

    def scan_step(t, carry):
        h_f, p_f, h_b, p_b = carry
        rf = pl.ds(pl.multiple_of(t * N_SEG, N_SEG), N_SEG)
        rb = pl.ds(pl.multiple_of((seg_len - 1 - t) * N_SEG, N_SEG), N_SEG)
        a = a_f[rf, :]
        h_f = a * h_f + b_f[rf, :]
        p_f = a * p_f
        b_f[rf, :] = h_f
        a_f[rf, :] = p_f
        a = a_b[rb, :]
        h_b = a * h_b + b_b[rb, :]
        p_b = a * p_b
        b_b[rb, :] = h_b
        a_b[rb, :] = p_b
        return h_f, p_f, h_b, p_b

    zeros = jnp.zeros((N_SEG, LANES), F32)
    ones = jnp.ones((N_SEG, LANES), F32)
    h_f, p_f, h_b, p_b = lax.fori_loop(0, seg_len, scan_step, (zeros, ones, zeros, ones), unroll=4)

    edge[0] = h_f
    edge[1] = p_f
    edge[2] = h_b
    edge[3] = p_b
    enter_f = enter_b = jnp.zeros((1, LANES), F32)
    edge[4, 0:1, :] = enter_f
    edge[5, N_SEG - 1:N_SEG, :] = enter_b
    for s in range(1, N_SEG):
        enter_f = edge[0, s - 1:s, :] + edge[1, s - 1:s, :] * enter_f
        edge[4, s:s + 1, :] = enter_f
        r = N_SEG - 1 - s
        enter_b = edge[2, r + 1:r + 2, :] + edge[3, r + 1:r + 2, :] * enter_b
        edge[5, r:r + 1, :] = enter_b
    blocks = SCAN_CHUNK // N_SEG
    in_f = jnp.concatenate([edge[4]] * blocks, axis=0)
    in_b = jnp.concatenate([edge[5]] * blocks, axis=0)

    def out_chunk(c, carry):
        rows = pl.ds(pl.multiple_of(c * SCAN_CHUNK, SCAN_CHUNK), SCAN_CHUNK)
        o_ref[rows, :] = (b_f[rows, :] + a_f[rows, :] * in_f) + (b_b[rows, :] + a_b[rows, :] * in_b)
        return carry

    lax.fori_loop(0, n_rows // SCAN_CHUNK, out_chunk, 0, unroll=2)


def _rglru(xr, conv_w, conv_b, w_a, b_a, w_x, b_x, lam):
    batch, _, n_rows, _ = xr.shape
    w_cat = jnp.concatenate([w_a[0], w_x[0], w_a[1], w_x[1]], axis=-1).astype(BF16)
    per_head = lambda v: v.reshape(RNN_HEADS, 1, RNN_HEAD_DIM)
    half_b = 0.5 * jnp.concatenate([per_head(b_a[0]), per_head(b_x[0]), per_head(b_a[1]), per_head(b_x[1])],
                                   axis=-1)
    b_hi = half_b.astype(BF16)
    b_lo = (half_b - b_hi.astype(F32)).astype(BF16)
    w_aug = jnp.concatenate([w_cat, b_hi, b_lo,
                             jnp.zeros((RNN_HEADS, RNN_HEAD_DIM - BIAS_ROWS, 4 * RNN_HEAD_DIM), BF16)], axis=1)
    col = lambda rows: pl.BlockSpec((rows, RNN_HEAD_DIM), lambda b, h: (0, h))
    seq_col = pl.BlockSpec((None, None, n_rows, RNN_HEAD_DIM), lambda b, h: (b, h, 0, 0))
    halo_rows = SCAN_CHUNK + (CONV_WIDTH - 1) * N_SEG
    return pl.pallas_call(
        _rglru_kernel,
        grid=(batch, RNN_HEADS),
        in_specs=[seq_col, col(CONV_WIDTH), col(1),
                  pl.BlockSpec((None, 2 * RNN_HEAD_DIM, 4 * RNN_HEAD_DIM), lambda b, h: (h, 0, 0)),
                  col(2)],
        out_specs=seq_col,
        out_shape=jax.ShapeDtypeStruct((batch, RNN_HEADS, n_rows, RNN_HEAD_DIM), F32),
        scratch_shapes=[pltpu.VMEM((2, halo_rows, LANES), F32)]
                       + [pltpu.VMEM((n_rows, LANES), F32)] * 4
                       + [pltpu.VMEM((6, N_SEG, LANES), F32)],
        compiler_params=_params("parallel", "parallel"),
        name="rglru",
    )(xr, 0.5 * conv_w, 0.5 * conv_b.reshape(1, -1), w_aug, lam)


def _odd_out_kernel(y_ref, x_ref, shift_ref, scale_ref, gate_ref, wg_f32_ref, w_f32_ref, lng_ref, lnb_ref,
                    o_ref, wg_ref, w_ref):
    _cast_weight_once(wg_f32_ref, wg_ref)
    _cast_weight_once(w_f32_ref, w_ref)
    seg_len = y_ref.shape[1]
    head_rows = seg_len * SEGS_PER_TILE
    y_rows = y_ref.reshape(RNN_HEADS * head_rows, LANES)
    x = x_ref[...]
    h = (x * (1.0 + _batch_row(scale_ref)) + _batch_row(shift_ref)).astype(BF16)
    g_all = jnp.dot(h, wg_ref[...], preferred_element_type=F32)
    gated = []
    for s in range(SEGS_PER_TILE):
        y = jnp.concatenate([y_rows[pl.ds(hd * head_rows + s, seg_len, stride=SEGS_PER_TILE), :]
                             for hd in range(RNN_HEADS)], axis=1)
        g = g_all[s * seg_len:(s + 1) * seg_len, :]
        gated.append((y * (g * _sigmoid(g))).astype(BF16))
    out = jnp.dot(jnp.concatenate(gated, axis=0), w_ref[...], preferred_element_type=F32)
    o_ref[...] = _deepnorm_rows(x, _batch_row(gate_ref), out,
                                lng_ref[ODD_LAYER:ODD_LAYER + 1, :], lnb_ref[ODD_LAYER:ODD_LAYER + 1, :])


def _odd_out_proj(y, x, mod, w_in, w_out, ln_g, ln_b):
    batch, seq, _ = x.shape
    seg_len = seq // N_SEG
    tile_rows = SEGS_PER_TILE * seg_len
    const = lambda shape: pl.BlockSpec(shape, lambda b, i: (0,) * len(shape))
    return pl.pallas_call(
        _odd_out_kernel,
        grid=(batch, seq // tile_rows),
        in_specs=[pl.BlockSpec((None, RNN_HEADS, seg_len, SEGS_PER_TILE, RNN_HEAD_DIM),
                               lambda b, i: (b, 0, 0, i, 0)),
                  pl.BlockSpec((None, tile_rows, D_MODEL), lambda b, i: (b, i, 0)),
                  _mod_spec(mod, ODD_LAYER, MOD_SHIFT), _mod_spec(mod, ODD_LAYER, MOD_SCALE),
                  _mod_spec(mod, ODD_LAYER, MOD_GATE),
                  _resident((D_MODEL, RNN_WIDTH), col_block=1), _resident((RNN_WIDTH, D_MODEL)),
                  const(ln_g.shape), const(ln_b.shape)],
        out_specs=pl.BlockSpec((None, tile_rows, D_MODEL), lambda b, i: (b, i, 0)),
        out_shape=jax.ShapeDtypeStruct((batch, seq, D_MODEL), F32),
        scratch_shapes=[pltpu.VMEM((D_MODEL, RNN_WIDTH), BF16), pltpu.VMEM((RNN_WIDTH, D_MODEL), BF16)],
        compiler_params=_params("arbitrary", "arbitrary"),
        name="odd_out_proj",
    )(y, x, mod, mod, mod, w_in, w_out, ln_g, ln_b)


def kernel(x, c, positions, ada_w, ada_b, ln_g, ln_b, ev_w_in, ev_w_out, ev_sink, ev_sg_ln_g, ev_sg_ln_b,
           ev_sg_w, ev_sg_b, od_w_in, od_conv_w, od_conv_b, od_w_a, od_b_a, od_w_x, od_b_x, od_lam, od_w_out):
    batch, seq, d_model = x.shape
    seg_len = seq // N_SEG
    assert d_model == D_MODEL and seq % ROW_TILE == 0 and seq % MIX_TILE == 0 and seq == N_SEG * seg_len
    assert N_SEG % SEGS_PER_TILE == 0 and seq % SCAN_CHUNK == 0 and SCAN_CHUNK % N_SEG == 0
    assert ada_w.shape[0] == DEPTH == 2

    mod = _modulation(c, ada_w, ada_b)

    q, k, v, su, sv, g = _even_in_proj(x, mod, positions, ev_w_in[0])
    x1 = _even_mixer(q, k, v, su, sv, g, x, mod, ev_sink[0], ev_sg_ln_g[0], ev_sg_ln_b[0],
                     ev_sg_w[0], ev_sg_b[0], ev_w_out[0], ln_g, ln_b)

    xr = _odd_in_proj(x1, mod, od_w_in[0])
    y = _rglru(xr.reshape(batch, RNN_HEADS, seq, RNN_HEAD_DIM), od_conv_w[0], od_conv_b[0], od_w_a[0],
               od_b_a[0], od_w_x[0], od_b_x[0], od_lam[0])
    return _odd_out_proj(y.reshape(batch, RNN_HEADS, seg_len, N_SEG, RNN_HEAD_DIM), x1, mod, od_w_in[0],
                         od_w_out[0], ln_g, ln_b)
```

```python
import math

import numpy as np
import jax
import jax.numpy as jnp
from jax import lax
from jax.experimental import pallas as pl
from jax.experimental.pallas import tpu as pltpu

D_MODEL = 1024
DEPTH = 2
HEAD_DIM = 64
N_Q_HEADS = 8
N_KV_HEADS = 2
Q_PER_KV = N_Q_HEADS // N_KV_HEADS
ATTN_WIDTH = N_Q_HEADS * HEAD_DIM
KV_WIDTH = N_KV_HEADS * HEAD_DIM
ATTN_BLOCK = 128
ROPE_DIM = HEAD_DIM // 4
ROPE_HALF = ROPE_DIM // 2
ROPE_THETA = 500000.0
N_SG_GROUPS = 8
SG_GROUP_DIM = 64
SG_WIDTH = N_SG_GROUPS * SG_GROUP_DIM
SG_CHUNK = 128
EVEN_GATE_WIDTH = ATTN_WIDTH + SG_WIDTH
EVEN_IN_WIDTH = ATTN_WIDTH + 2 * KV_WIDTH + 2 * SG_WIDTH + EVEN_GATE_WIDTH
RNN_WIDTH = D_MODEL
RNN_HEADS = 8
RNN_HEAD_DIM = RNN_WIDTH // RNN_HEADS
CONV_WIDTH = 4
CONV_LEFT = 2
RG_LRU_C = 8.0
DEEPNORM_ALPHA = (2 * DEPTH) ** 0.25
LN_EPS = 1e-5
NEG_INF = -1e30
LOG2_E = math.log2(math.e)
F32_TINY = float(np.finfo(np.float32).tiny)

LANES = 128
SUBLANES = 8
VMEM_LIMIT = 56 * 1024 * 1024

ROW_TILE = 1024
MIX_TILE = 512
N_SEG = 32
SEGS_PER_TILE = SUBLANES
SCAN_CHUNK = 1024
WEIGHT_CAST_ROWS = 128
BIAS_ROWS = 2

BF16 = jnp.bfloat16
F32 = jnp.float32


def _sigmoid(x):
    return 0.5 * jnp.tanh(0.5 * x) + 0.5


def _layer_norm_rows(z, g, b, eps=LN_EPS):
    mu = jnp.mean(z, axis=-1, keepdims=True)
    d = z - mu
    var = jnp.mean(d * d, axis=-1, keepdims=True)
    return d * lax.rsqrt(var + eps) * g + b


def _deepnorm_rows(x, gate, y, g, b):
    return _layer_norm_rows(x + (gate * (1.0 / DEEPNORM_ALPHA)) * y, g, b, LN_EPS / DEEPNORM_ALPHA ** 2)


def _params(*semantics):
    return pltpu.CompilerParams(dimension_semantics=semantics, vmem_limit_bytes=VMEM_LIMIT)


def _cast_weight_once(w_ref, w_bf16):
    @pl.when((pl.program_id(0) == 0) & (pl.program_id(1) == 0))
    def _():
        for r in range(0, w_ref.shape[0], WEIGHT_CAST_ROWS):
            w_bf16[r:r + WEIGHT_CAST_ROWS, :] = w_ref[r:r + WEIGHT_CAST_ROWS, :].astype(BF16)


def _resident(shape, col_block=0):
    return pl.BlockSpec(shape, lambda b, i: (0,) * (len(shape) - 1) + (col_block,), pipeline_mode=pl.Buffered(1))


def _mod_kernel(c_ref, w_ref, b_ref, o_ref):
    c = c_ref[...]
    cond = c * _sigmoid(c)
    pad_rows = o_ref.shape[0] - cond.shape[0]
    if pad_rows:
        cond = jnp.concatenate([cond, jnp.zeros((pad_rows, cond.shape[1]), F32)], axis=0)
    bias = b_ref[pl.ds(pl.program_id(0), 1), :]
    o_ref[...] = jnp.dot(cond.astype(BF16), w_ref[...].astype(BF16), preferred_element_type=F32) + bias


def _modulation(c, ada_w, ada_b):
    batch = c.shape[0]
    rows = -(-batch // SUBLANES) * SUBLANES
    n_col = 3 * D_MODEL // D_MODEL
    return pl.pallas_call(
        _mod_kernel,
        grid=(DEPTH, n_col),
        in_specs=[
            pl.BlockSpec((batch, D_MODEL), lambda l, j: (0, 0)),
            pl.BlockSpec((None, D_MODEL, D_MODEL), lambda l, j: (l, 0, j)),
            pl.BlockSpec((DEPTH, D_MODEL), lambda l, j: (0, j)),
        ],
        out_specs=pl.BlockSpec((None, rows, D_MODEL), lambda l, j: (l, 0, j)),
        out_shape=jax.ShapeDtypeStruct((DEPTH, rows, 3 * D_MODEL), F32),
        compiler_params=_params("parallel", "parallel"),
        name="adaln_mod",
    )(c, ada_w, ada_b)


MOD_SHIFT, MOD_SCALE, MOD_GATE = range(3)
EVEN_LAYER, ODD_LAYER = 0, 1


def _mod_spec(mod, layer, part):
    return pl.BlockSpec((None, mod.shape[1], D_MODEL), lambda b, i: (layer, 0, part))


def _batch_row(ref):
    return ref[pl.ds(pl.program_id(0), 1), :]


def _even_in_kernel(x_ref, shift_ref, scale_ref, pos_ref, freq_ref, sign_ref, w_f32_ref,
                    q_ref, k_ref, v_ref, su_ref, sv_ref, g_ref, w_ref):
    _cast_weight_once(w_f32_ref, w_ref)
    lane = lax.broadcasted_iota(jnp.int32, (1, LANES), 1)
    first_half = (lane % HEAD_DIM) < ROPE_HALF
    left_head = lane < HEAD_DIM
    diagonal = lax.broadcasted_iota(jnp.int32, (LANES, LANES), 0) == lax.broadcasted_iota(jnp.int32, (LANES, LANES), 1)
    batch_ids = lax.broadcasted_iota(jnp.int32, (pos_ref.shape[0], LANES), 0)
    q_scale = HEAD_DIM ** -0.5 * LOG2_E

    half_rows = x_ref.shape[0] // 2
    for r in range(2):
        rows = slice(r * half_rows, (r + 1) * half_rows)
        h = (x_ref[rows, :] * (1.0 + _batch_row(scale_ref)) + _batch_row(shift_ref)).astype(BF16)
        p = jnp.dot(h, w_ref[...], preferred_element_type=F32)

        pos_cols = []
        for j in range(r * half_rows // LANES, (r + 1) * half_rows // LANES):
            pos_all = pos_ref[:, j * LANES:(j + 1) * LANES].astype(F32)
            pos_row = jnp.sum(jnp.where(batch_ids == pl.program_id(0), pos_all, 0.0), axis=0, keepdims=True)
            pos_cols.append(jnp.sum(jnp.where(diagonal, pos_row, 0.0), axis=1, keepdims=True))
        ang = jnp.concatenate(pos_cols, axis=0) * freq_ref[...]
        cs = jnp.cos(ang)
        sn = jnp.sin(ang) * sign_ref[...]

        def rotary(t):
            partner = jnp.where(first_half, pltpu.roll(t, LANES - ROPE_HALF, 1), pltpu.roll(t, ROPE_HALF, 1))
            return t * cs + partner * sn

        for j in range(ATTN_WIDTH // LANES):
            q_ref[rows, j * LANES:(j + 1) * LANES] = (
                rotary(p[:, j * LANES:(j + 1) * LANES]) * q_scale).astype(BF16)
        o = ATTN_WIDTH
        k = rotary(p[:, o:o + KV_WIDTH])
        k_sw = pltpu.roll(k, HEAD_DIM, 1)
        k_ref[rows, 0:LANES] = jnp.where(left_head, k, k_sw).astype(BF16)
        k_ref[rows, LANES:2 * LANES] = jnp.where(left_head, k_sw, k).astype(BF16)
        o += KV_WIDTH
        v = p[:, o:o + KV_WIDTH]
        v_sw = pltpu.roll(v, HEAD_DIM, 1)
        v_ref[rows, 0 * LANES:1 * LANES] = jnp.where(left_head, v, 0.0).astype(BF16)
        v_ref[rows, 1 * LANES:2 * LANES] = jnp.where(left_head, 0.0, v_sw).astype(BF16)
        v_ref[rows, 2 * LANES:3 * LANES] = jnp.where(left_head, v_sw, 0.0).astype(BF16)
        v_ref[rows, 3 * LANES:4 * LANES] = jnp.where(left_head, 0.0, v).astype(BF16)
        o += KV_WIDTH
        su_ref[rows, :] = p[:, o:o + SG_WIDTH].astype(BF16)
        o += SG_WIDTH
        sv_ref[rows, :] = p[:, o:o + SG_WIDTH].astype(BF16)
        o += SG_WIDTH
        g_ref[rows, :] = p[:, o:o + EVEN_GATE_WIDTH].astype(BF16)


def _even_in_proj(x, mod, positions, w_in):
    batch, seq, _ = x.shape
    d = np.arange(LANES) % HEAD_DIM
    inv_freq = np.power(np.float64(ROPE_THETA), -np.arange(ROPE_HALF) / ROPE_HALF)
    freq_lane = jnp.asarray(np.where(d < ROPE_DIM, inv_freq[d % ROPE_HALF], 0.0), dtype=F32).reshape(1, LANES)
    sign_lane = jnp.asarray(np.where(d < ROPE_HALF, -1.0, np.where(d < ROPE_DIM, 1.0, 0.0)),
                            dtype=F32).reshape(1, LANES)
    row = lambda width: pl.BlockSpec((None, ROW_TILE, width), lambda b, i: (b, i, 0))
    const = lambda shape: pl.BlockSpec(shape, lambda b, i: (0,) * len(shape))
    widths = (ATTN_WIDTH, 2 * KV_WIDTH, 4 * KV_WIDTH, SG_WIDTH, SG_WIDTH, EVEN_GATE_WIDTH)
    return pl.pallas_call(
        _even_in_kernel,
        grid=(batch, seq // ROW_TILE),
        in_specs=[row(D_MODEL), _mod_spec(mod, EVEN_LAYER, MOD_SHIFT), _mod_spec(mod, EVEN_LAYER, MOD_SCALE),
                  pl.BlockSpec((batch, ROW_TILE), lambda b, i: (0, i)),
                  const((1, LANES)), const((1, LANES)), _resident((D_MODEL, EVEN_IN_WIDTH))],
        out_specs=[row(w) for w in widths],
        out_shape=[jax.ShapeDtypeStruct((batch, seq, w), BF16) for w in widths],
        scratch_shapes=[pltpu.VMEM((D_MODEL, EVEN_IN_WIDTH), BF16)],
        compiler_params=_params("arbitrary", "arbitrary"),
        name="even_in_proj",
    )(x, mod, mod, positions, freq_lane, sign_lane, w_in)


def _even_mix_kernel(sink_ref, q_ref, k_ref, v_ref, su_ref, sv_ref, g_ref, x_ref, gate_ref,
                     avg_ref, avg2_ref, sgg_ref, sgb_ref, sgw_ref, sgbias_ref, wout_f32_ref, lng_ref, lnb_ref,
                     o_ref, y_scr, wout_ref):
    _cast_weight_once(wout_f32_ref, wout_ref)
    seq = k_ref.shape[0]
    n_blocks = seq // ATTN_BLOCK
    sub_blocks = MIX_TILE // ATTN_BLOCK
    tile = pl.program_id(1)
    lane = lax.broadcasted_iota(jnp.int32, (1, LANES), 1)
    left = lane < HEAD_DIM

    qi = lax.broadcasted_iota(jnp.int32, (ATTN_BLOCK, ATTN_BLOCK), 0)
    kj = lax.broadcasted_iota(jnp.int32, (ATTN_BLOCK, ATTN_BLOCK), 1)
    ones_left = jnp.broadcast_to(jnp.where(left, 1.0, 0.0).astype(BF16), (3 * ATTN_BLOCK, LANES))
    ones_right = jnp.broadcast_to(jnp.where(left, 0.0, 1.0).astype(BF16), (3 * ATTN_BLOCK, LANES))
    avg = avg_ref[...]
    sink_cols = [jnp.concatenate([jnp.full((ATTN_BLOCK, LANES), sink_ref[h] * LOG2_E, F32)
                                  for h in (4 * hk, 4 * hk + 2, 4 * hk + 1, 4 * hk + 3)], axis=0)
                 for hk in range(N_KV_HEADS)]

    def sub_block(j, carry):
        blk = tile * sub_blocks + j
        prev = jnp.maximum(blk - 1, 0)
        nxt = jnp.minimum(blk + 1, n_blocks - 1)
        rows = pl.ds(pl.multiple_of(j * ATTN_BLOCK, ATTN_BLOCK), ATTN_BLOCK)
        bias_prev = jnp.where((kj >= qi) & (blk > 0), 0.0, NEG_INF)
        bias_next = jnp.where((kj <= qi) & (blk < n_blocks - 1), 0.0, NEG_INF)
        bias_prev = jnp.concatenate([bias_prev] * Q_PER_KV, axis=0)
        bias_next = jnp.concatenate([bias_next] * Q_PER_KV, axis=0)

        def band(ref):
            return jnp.concatenate(
                [ref[pl.ds(pl.multiple_of(b * ATTN_BLOCK, ATTN_BLOCK), ATTN_BLOCK), :] for b in (prev, blk, nxt)],
                axis=0)

        k_band = band(k_ref)
        v_band = band(v_ref)
        n_sg = SG_WIDTH // LANES
        sg_cols = [slice(c * LANES, (c + 1) * LANES) for c in range(n_sg)]

        def silu_gate(cols):
            g_c = g_ref[rows, cols].astype(F32)
            return g_c * _sigmoid(g_c)

        scores = []
        for hk in range(N_KV_HEADS):
            q_cols = [q_ref[rows, (2 * hk + c) * LANES:(2 * hk + c + 1) * LANES] for c in range(2)]
            zero = jnp.zeros_like(q_cols[0])
            lhs = jnp.concatenate([jnp.where(left, qc, zero) for qc in q_cols]
                                  + [jnp.where(left, zero, qc) for qc in q_cols], axis=0)
            scores.append(lax.dot_general(lhs, k_band[:, hk * LANES:(hk + 1) * LANES], (((1,), (1,)), ((), ())),
                                          preferred_element_type=F32))
        sv_cols = [sv_ref[rows, cols] for cols in sg_cols]
        means = [jnp.dot(v_c, avg, preferred_element_type=F32) for v_c in sv_cols]

        probs, row_max = [], []
        for hk in range(N_KV_HEADS):
            s = scores[hk]
            s0 = s[:, 0:ATTN_BLOCK] + bias_prev
            s1 = s[:, ATTN_BLOCK:2 * ATTN_BLOCK]
            s2 = s[:, 2 * ATTN_BLOCK:3 * ATTN_BLOCK] + bias_next
            m = jnp.max(jnp.maximum(jnp.maximum(s0, s1), s2), axis=-1, keepdims=True)
            m = jnp.maximum(jnp.broadcast_to(m, sink_cols[hk].shape), sink_cols[hk])
            probs.append(jnp.concatenate([jnp.exp2(t - m).astype(BF16) for t in (s0, s1, s2)], axis=1))
            row_max.append(m)
        devs, sq_parts = [], []
        for v_c, mean in zip(sv_cols, means):
            dev = v_c.astype(F32) - mean
            sq = dev * dev
            sq_hi = sq.astype(BF16)
            sq_lo = (sq - sq_hi.astype(F32)).astype(BF16)
            devs.append(dev)
            sq_parts.append(jnp.concatenate([sq_hi, sq_lo], axis=1))

        pv = []
        for hk in range(N_KV_HEADS):
            p = probs[hk]
            lhs = jnp.concatenate(
                [jnp.concatenate([p[c * ATTN_BLOCK:(c + 1) * ATTN_BLOCK], p[(2 + c) * ATTN_BLOCK:(3 + c) * ATTN_BLOCK]],
                                 axis=1) for c in range(2)], axis=0)
            rhs = jnp.concatenate(
                [jnp.concatenate([v_band[:, (2 * hk) * LANES:(2 * hk + 1) * LANES], ones_left], axis=1),
                 jnp.concatenate([v_band[:, (2 * hk + 1) * LANES:(2 * hk + 2) * LANES], ones_right], axis=1)],
                axis=0)
            pv.append(jnp.dot(lhs, rhs, preferred_element_type=F32))
        variances = [jnp.dot(sq, avg2_ref[...], preferred_element_type=F32) for sq in sq_parts]

        for hk in range(N_KV_HEADS):
            for c in range(2):
                even, odd = slice(c * ATTN_BLOCK, (c + 1) * ATTN_BLOCK), slice((2 + c) * ATTN_BLOCK, (3 + c) * ATTN_BLOCK)
                sink_term = jnp.exp2(jnp.where(left, sink_cols[hk][even] - row_max[hk][even],
                                               sink_cols[hk][odd] - row_max[hk][odd]))
                o = pv[hk][c * ATTN_BLOCK:(c + 1) * ATTN_BLOCK]
                pair = o[:, 0:LANES] * (1.0 / (o[:, LANES:2 * LANES] + sink_term))
                cols = slice((2 * hk + c) * LANES, (2 * hk + c + 1) * LANES)
                y_scr[rows, cols] = (pair * silu_gate(cols)).astype(BF16)

        for c, cols in enumerate(sg_cols):
            vn = devs[c] * lax.rsqrt(variances[c] + LN_EPS) * sgg_ref[:, cols] + sgb_ref[:, cols]
            stacked = jnp.concatenate([jnp.where(left, vn, 0.0), jnp.where(left, 0.0, vn)],
                                      axis=0).astype(BF16)
            mixed = jnp.dot(sgw_ref[c], stacked, preferred_element_type=F32) + sgbias_ref[:, cols]
            y_sg = su_ref[rows, cols].astype(F32) * mixed
            gc = slice(ATTN_WIDTH + c * LANES, ATTN_WIDTH + (c + 1) * LANES)
            y_scr[rows, gc] = (y_sg * silu_gate(gc)).astype(BF16)
        return carry

    lax.fori_loop(0, sub_blocks, sub_block, 0, unroll=True)

    for r in range(2):
        half_rows = slice(r * (MIX_TILE // 2), (r + 1) * (MIX_TILE // 2))
        out = jnp.dot(y_scr[half_rows, :], wout_ref[...], preferred_element_type=F32)
        o_ref[half_rows, :] = _deepnorm_rows(x_ref[half_rows, :], _batch_row(gate_ref), out,
                                             lng_ref[EVEN_LAYER:EVEN_LAYER + 1, :], lnb_ref[EVEN_LAYER:EVEN_LAYER + 1, :])


def _even_mixer(q, k, v, su, sv, g, x, mod, sink, sg_ln_g, sg_ln_b, sg_w, sg_b, w_out, ln_g, ln_b):
    batch, seq, _ = x.shape
    lane = np.arange(LANES)
    avg = jnp.asarray((lane[:, None] // SG_GROUP_DIM == lane[None, :] // SG_GROUP_DIM) / SG_GROUP_DIM, dtype=BF16)
    w_pairs = sg_w.reshape(N_SG_GROUPS // 2, 2, SG_CHUNK, SG_CHUNK).transpose(0, 2, 1, 3)
    w_pairs = w_pairs.reshape(N_SG_GROUPS // 2, SG_CHUNK, 2 * SG_CHUNK).astype(BF16)
    bias_full = jnp.repeat(sg_b.T, SG_GROUP_DIM, axis=1)
    row = lambda width: pl.BlockSpec((None, MIX_TILE, width), lambda b, i: (b, i, 0))
    full_seq = lambda arr: pl.BlockSpec((None, seq, arr.shape[-1]), lambda b, i: (b, 0, 0))
    const = lambda shape: pl.BlockSpec(shape, lambda b, i: (0,) * len(shape))
    return pl.pallas_call(
        _even_mix_kernel,
        grid=(batch, seq // MIX_TILE),
        in_specs=[
            pl.BlockSpec(memory_space=pltpu.SMEM),
            row(ATTN_WIDTH), full_seq(k), full_seq(v), row(SG_WIDTH), row(SG_WIDTH), row(EVEN_GATE_WIDTH),
            row(D_MODEL), _mod_spec(mod, EVEN_LAYER, MOD_GATE),
            const((LANES, LANES)), const((2 * LANES, LANES)), const((1, SG_WIDTH)), const((1, SG_WIDTH)),
            const((N_SG_GROUPS // 2, SG_CHUNK, 2 * SG_CHUNK)), const((SG_CHUNK, SG_WIDTH)),
            _resident((EVEN_GATE_WIDTH, D_MODEL)), const(ln_g.shape), const(ln_b.shape),
        ],
        out_specs=row(D_MODEL),
        out_shape=jax.ShapeDtypeStruct((batch, seq, D_MODEL), F32),
        scratch_shapes=[pltpu.VMEM((MIX_TILE, EVEN_GATE_WIDTH), BF16),
                        pltpu.VMEM((EVEN_GATE_WIDTH, D_MODEL), BF16)],
        compiler_params=_params("arbitrary", "arbitrary"),
        name="even_mixer",
    )(sink, q, k, v, su, sv, g, x, mod, avg, jnp.concatenate([avg, avg], axis=0), sg_ln_g.reshape(1, -1), sg_ln_b.reshape(1, -1),
      w_pairs, bias_full, w_out, ln_g, ln_b)


def _odd_in_kernel(x_ref, shift_ref, scale_ref, w_f32_ref, xr_ref, w_ref):
    _cast_weight_once(w_f32_ref, w_ref)
    seg_len = xr_ref.shape[1]
    head_rows = seg_len * SEGS_PER_TILE
    xr_rows = xr_ref.reshape(RNN_HEADS * head_rows, LANES)
    halves = 2
    for r in range(halves):
        first = r * (SEGS_PER_TILE // halves)
        rows = slice(first * seg_len, (first + SEGS_PER_TILE // halves) * seg_len)
        h = (x_ref[rows, :] * (1.0 + _batch_row(scale_ref)) + _batch_row(shift_ref)).astype(BF16)
        p = jnp.dot(h, w_ref[...], preferred_element_type=F32)
        for s in range(SEGS_PER_TILE // halves):
            for hd in range(RNN_HEADS):
                xr_rows[pl.ds(hd * head_rows + first + s, seg_len, stride=SEGS_PER_TILE), :] = (
                    p[s * seg_len:(s + 1) * seg_len, hd * LANES:(hd + 1) * LANES])


def _odd_in_proj(x, mod, w_in):
    batch, seq, _ = x.shape
    seg_len = seq // N_SEG
    tile_rows = SEGS_PER_TILE * seg_len
    out_spec = pl.BlockSpec((None, RNN_HEADS, seg_len, SEGS_PER_TILE, RNN_HEAD_DIM), lambda b, i: (b, 0, 0, i, 0))
    out_shape = jax.ShapeDtypeStruct((batch, RNN_HEADS, seg_len, N_SEG, RNN_HEAD_DIM), F32)
    return pl.pallas_call(
        _odd_in_kernel,
        grid=(batch, seq // tile_rows),
        in_specs=[pl.BlockSpec((None, tile_rows, D_MODEL), lambda b, i: (b, i, 0)),
                  _mod_spec(mod, ODD_LAYER, MOD_SHIFT), _mod_spec(mod, ODD_LAYER, MOD_SCALE),
                  _resident((D_MODEL, RNN_WIDTH), col_block=0)],
        out_specs=out_spec,
        out_shape=out_shape,
        scratch_shapes=[pltpu.VMEM((D_MODEL, RNN_WIDTH), BF16)],
        compiler_params=_params("arbitrary", "arbitrary"),
        name="odd_in_proj",
    )(x, mod, mod, w_in)


def _rglru_kernel(x_ref, convw_ref, convb_ref, waug_ref, lam_ref, o_ref,
                  halo, a_f, b_f, a_b, b_b, edge):
    n_rows = x_ref.shape[0]
    seg_len = n_rows // N_SEG
    n_chunks = n_rows // SCAN_CHUNK
    left_rows = CONV_LEFT * N_SEG
    right_rows = (CONV_WIDTH - 1 - CONV_LEFT) * N_SEG
    seg = lax.broadcasted_iota(jnp.int32, (N_SEG, LANES), 0)

    def from_prev_segment(blk):
        return jnp.where(seg == 0, 0.0, pltpu.roll(blk, 1, 0))

    def from_next_segment(blk):
        return jnp.where(seg == N_SEG - 1, 0.0, pltpu.roll(blk, N_SEG - 1, 0))

    for t in range(CONV_LEFT):
        halo[0, pl.ds(t * N_SEG, N_SEG), :] = from_prev_segment(
            x_ref[pl.ds((seg_len - CONV_LEFT + t) * N_SEG, N_SEG), :])
    halo[0, pl.ds(left_rows, SCAN_CHUNK + right_rows), :] = x_ref[pl.ds(0, SCAN_CHUNK + right_rows), :]
    halo[1, pl.ds(0, left_rows + SCAN_CHUNK), :] = x_ref[pl.ds(n_rows - SCAN_CHUNK - left_rows, left_rows + SCAN_CHUNK), :]
    for t in range(CONV_WIDTH - 1 - CONV_LEFT):
        halo[1, pl.ds(left_rows + SCAN_CHUNK + t * N_SEG, N_SEG), :] = from_next_segment(
            x_ref[pl.ds(t * N_SEG, N_SEG), :])

    lam = lam_ref[...]
    softplus_neg = jnp.maximum(-lam, 0.0) + jnp.log1p(jnp.exp(-jnp.abs(lam)))
    half_rate = (0.5 * RG_LRU_C) * softplus_neg
    bias_lanes = jnp.broadcast_to(
        jnp.where(lax.broadcasted_iota(jnp.int32, (1, LANES), 1) < BIAS_ROWS, 1.0, 0.0).astype(BF16),
        (SCAN_CHUNK, LANES))

    def gates(base, tap):
        half_x = convb_ref[...] + sum(convw_ref[k:k + 1, :] * tap(k) for k in range(CONV_WIDTH))
        th = jnp.tanh(jnp.dot(jnp.concatenate([half_x.astype(BF16), bias_lanes], axis=1), waug_ref[...],
                              preferred_element_type=F32))
        for d, (a_ref, b_ref) in enumerate(((a_f, b_f), (a_b, b_b))):
            o = 2 * d * LANES
            neg_log_a = (th[:, o:o + LANES] + 1.0) * half_rate[d:d + 1, :]
            a = jnp.exp2(neg_log_a * (-LOG2_E))
            one_minus_a2 = jnp.tanh(neg_log_a) * (a * a + 1.0)
            root = one_minus_a2 * lax.rsqrt(jnp.maximum(one_minus_a2, F32_TINY))
            a_ref[pl.ds(base, SCAN_CHUNK), :] = a
            b_ref[pl.ds(base, SCAN_CHUNK), :] = root * ((th[:, o + LANES:o + 2 * LANES] + 1.0) * half_x)

    def inner_chunk(c, carry):
        base = pl.multiple_of(c * SCAN_CHUNK, SCAN_CHUNK)
        gates(base, lambda k: x_ref[pl.ds(base + (k - CONV_LEFT) * N_SEG, SCAN_CHUNK), :])
        return carry

    gates(0, lambda k: halo[0, pl.ds(k * N_SEG, SCAN_CHUNK), :])
    gates(n_rows - SCAN_CHUNK, lambda k: halo[1, pl.ds(k * N_SEG, SCAN_CHUNK), :])
    lax.fori_loop(1, n_chunks - 1, inner_chunk, 0, unroll=2)

    def scan_step(t, carry):
        h_f, p_f, h_b, p_b = carry
        rf = pl.ds(pl.multiple_of(t * N_SEG, N_SEG), N_SEG)
        rb = pl.ds(pl.multiple_of((seg_len - 1 - t) * N_SEG, N_SEG), N_SEG)
        a = a_f[rf, :]
        h_f = a * h_f + b_f[rf, :]
        p_f = a * p_f
        b_f[rf, :] = h_f
        a_f[rf, :] = p_f
        a = a_b[rb, :]
        h_b = a * h_b + b_b[rb, :]
        p_b = a * p_b
        b_b[rb, :] = h_b
        a_b[rb, :] = p_b
        return h_f, p_f, h_b, p_b

    zeros = jnp.zeros((N_SEG, LANES), F32)
    ones = jnp.ones((N_SEG, LANES), F32)
    h_f, p_f, h_b, p_b = lax.fori_loop(0, seg_len, scan_step, (zeros, ones, zeros, ones), unroll=4)

    edge[0] = h_f
    edge[1] = p_f
    edge[2] = h_b
    edge[3] = p_b
    enter_f = enter_b = jnp.zeros((1, LANES), F32)
    edge[4, 0:1, :] = enter_f
    edge[5, N_SEG - 1:N_SEG, :] = enter_b
    for s in range(1, N_SEG):
        enter_f = edge[0, s - 1:s, :] + edge[1, s - 1:s, :] * enter_f
        edge[4, s:s + 1, :] = enter_f
        r = N_SEG - 1 - s
        enter_b = edge[2, r + 1:r + 2, :] + edge[3, r + 1:r + 2, :] * enter_b
        edge[5, r:r + 1, :] = enter_b
    blocks = SCAN_CHUNK // N_SEG
    in_f = jnp.concatenate([edge[4]] * blocks, axis=0)
    in_b = jnp.concatenate([edge[5]] * blocks, axis=0)

    def out_chunk(c, carry):
        rows = pl.ds(pl.multiple_of(c * SCAN_CHUNK, SCAN_CHUNK), SCAN_CHUNK)
        o_ref[rows, :] = (b_f[rows, :] + a_f[rows, :] * in_f) + (b_b[rows, :] + a_b[rows, :] * in_b)
        return carry

    lax.fori_loop(0, n_rows // SCAN_CHUNK, out_chunk, 0, unroll=2)


def _rglru(xr, conv_w, conv_b, w_a, b_a, w_x, b_x, lam):
    batch, _, n_rows, _ = xr.shape
    w_cat = jnp.concatenate([w_a[0], w_x[0], w_a[1], w_x[1]], axis=-1).astype(BF16)
    per_head = lambda v: v.reshape(RNN_HEADS, 1, RNN_HEAD_DIM)
    half_b = 0.5 * jnp.concatenate([per_head(b_a[0]), per_head(b_x[0]), per_head(b_a[1]), per_head(b_x[1])],
                                   axis=-1)
    b_hi = half_b.astype(BF16)
    b_lo = (half_b - b_hi.astype(F32)).astype(BF16)
    w_aug = jnp.concatenate([w_cat, b_hi, b_lo,
                             jnp.zeros((RNN_HEADS, RNN_HEAD_DIM - BIAS_ROWS, 4 * RNN_HEAD_DIM), BF16)], axis=1)
    col = lambda rows: pl.BlockSpec((rows, RNN_HEAD_DIM), lambda b, h: (0, h))
    seq_col = pl.BlockSpec((None, None, n_rows, RNN_HEAD_DIM), lambda b, h: (b, h, 0, 0))
    halo_rows = SCAN_CHUNK + (CONV_WIDTH - 1) * N_SEG
    return pl.pallas_call(
        _rglru_kernel,
        grid=(batch, RNN_HEADS),
        in_specs=[seq_col, col(CONV_WIDTH), col(1),
                  pl.BlockSpec((None, 2 * RNN_HEAD_DIM, 4 * RNN_HEAD_DIM), lambda b, h: (h, 0, 0)),
                  col(2)],
        out_specs=seq_col,
        out_shape=jax.ShapeDtypeStruct((batch, RNN_HEADS, n_rows, RNN_HEAD_DIM), F32),
        scratch_shapes=[pltpu.VMEM((2, halo_rows, LANES), F32)]
                       + [pltpu.VMEM((n_rows, LANES), F32)] * 4
                       + [pltpu.VMEM((6, N_SEG, LANES), F32)],
        compiler_params=_params("parallel", "parallel"),
        name="rglru",
    )(xr, 0.5 * conv_w, 0.5 * conv_b.reshape(1, -1), w_aug, lam)


def _odd_out_kernel(y_ref, x_ref, shift_ref, scale_ref, gate_ref, wg_f32_ref, w_f32_ref, lng_ref, lnb_ref,
                    o_ref, wg_ref, w_ref):
    _cast_weight_once(wg_f32_ref, wg_ref)
    _cast_weight_once(w_f32_ref, w_ref)
    seg_len = y_ref.shape[1]
    head_rows = seg_len * SEGS_PER_TILE
    y_rows = y_ref.reshape(RNN_HEADS * head_rows, LANES)
    x = x_ref[...]
    h = (x * (1.0 + _batch_row(scale_ref)) + _batch_row(shift_ref)).astype(BF16)
    g_all = jnp.dot(h, wg_ref[...], preferred_element_type=F32)
    gated = []
    for s in range(SEGS_PER_TILE):
        y = jnp.concatenate([y_rows[pl.ds(hd * head_rows + s, seg_len, stride=SEGS_PER_TILE), :]
                             for hd in range(RNN_HEADS)], axis=1)
        g = g_all[s * seg_len:(s + 1) * seg_len, :]
        gated.append((y * (g * _sigmoid(g))).astype(BF16))
    out = jnp.dot(jnp.concatenate(gated, axis=0), w_ref[...], preferred_element_type=F32)
    o_ref[...] = _deepnorm_rows(x, _batch_row(gate_ref), out,
                                lng_ref[ODD_LAYER:ODD_LAYER + 1, :], lnb_ref[ODD_LAYER:ODD_LAYER + 1, :])


def _odd_out_proj(y, x, mod, w_in, w_out, ln_g, ln_b):
    batch, seq, _ = x.shape
    seg_len = seq // N_SEG
    tile_rows = SEGS_PER_TILE * seg_len
    const = lambda shape: pl.BlockSpec(shape, lambda b, i: (0,) * len(shape))
    return pl.pallas_call(
        _odd_out_kernel,
        grid=(batch, seq // tile_rows),
        in_specs=[pl.BlockSpec((None, RNN_HEADS, seg_len, SEGS_PER_TILE, RNN_HEAD_DIM),
                               lambda b, i: (b, 0, 0, i, 0)),
                  pl.BlockSpec((None, tile_rows, D_MODEL), lambda b, i: (b, i, 0)),
                  _mod_spec(mod, ODD_LAYER, MOD_SHIFT), _mod_spec(mod, ODD_LAYER, MOD_SCALE),
                  _mod_spec(mod, ODD_LAYER, MOD_GATE),
                  _resident((D_MODEL, RNN_WIDTH), col_block=1), _resident((RNN_WIDTH, D_MODEL)),
                  const(ln_g.shape), const(ln_b.shape)],
        out_specs=pl.BlockSpec((None, tile_rows, D_MODEL), lambda b, i: (b, i, 0)),
        out_shape=jax.ShapeDtypeStruct((batch, seq, D_MODEL), F32),
        scratch_shapes=[pltpu.VMEM((D_MODEL, RNN_WIDTH), BF16), pltpu.VMEM((RNN_WIDTH, D_MODEL), BF16)],
        compiler_params=_params("arbitrary", "arbitrary"),
        name="odd_out_proj",
    )(y, x, mod, mod, mod, w_in, w_out, ln_g, ln_b)


def kernel(x, c, positions, ada_w, ada_b, ln_g, ln_b, ev_w_in, ev_w_out, ev_sink, ev_sg_ln_g, ev_sg_ln_b,
           ev_sg_w, ev_sg_b, od_w_in, od_conv_w, od_conv_b, od_w_a, od_b_a, od_w_x, od_b_x, od_lam, od_w_out):
    batch, seq, d_model = x.shape
    seg_len = seq // N_SEG
    assert d_model == D_MODEL and seq % ROW_TILE == 0 and seq % MIX_TILE == 0 and seq == N_SEG * seg_len
    assert N_SEG % SEGS_PER_TILE == 0 and seq % SCAN_CHUNK == 0 and SCAN_CHUNK % N_SEG == 0
    assert ada_w.shape[0] == DEPTH == 2

    mod = _modulation(c, ada_w, ada_b)

    q, k, v, su, sv, g = _even_in_proj(x, mod, positions, ev_w_in[0])
    x1 = _even_mixer(q, k, v, su, sv, g, x, mod, ev_sink[0], ev_sg_ln_g[0], ev_sg_ln_b[0],
                     ev_sg_w[0], ev_sg_b[0], ev_w_out[0], ln_g, ln_b)

    xr = _odd_in_proj(x1, mod, od_w_in[0])
    y = _rglru(xr.reshape(batch, RNN_HEADS, seq, RNN_HEAD_DIM), od_conv_w[0], od_conv_b[0], od_w_a[0],
               od_b_a[0], od_w_x[0], od_b_x[0], od_lam[0])
    return _odd_out_proj(y.reshape(batch, RNN_HEADS, seg_len, N_SEG, RNN_HEAD_DIM), x1, mod, od_w_in[0],
                         od_w_out[0], ln_g, ln_b)
```

```python
import math

import numpy as np
import jax
import jax.numpy as jnp
from jax import lax
from jax.experimental import pallas as pl
from jax.experimental.pallas import tpu as pltpu

D_MODEL = 1024
DEPTH = 2
HEAD_DIM = 64
N_Q_HEADS = 8
N_KV_HEADS = 2
Q_PER_KV = N_Q_HEADS // N_KV_HEADS
ATTN_WIDTH = N_Q_HEADS * HEAD_DIM
KV_WIDTH = N_KV_HEADS * HEAD_DIM
ATTN_BLOCK = 128
ROPE_DIM = HEAD_DIM // 4
ROPE_HALF = ROPE_DIM // 2
ROPE_THETA = 500000.0
N_SG_GROUPS = 8
SG_GROUP_DIM = 64
SG_WIDTH = N_SG_GROUPS * SG_GROUP_DIM
SG_CHUNK = 128
EVEN_GATE_WIDTH = ATTN_WIDTH + SG_WIDTH
EVEN_IN_WIDTH = ATTN_WIDTH + 2 * KV_WIDTH + 2 * SG_WIDTH + EVEN_GATE_WIDTH
RNN_WIDTH = D_MODEL
RNN_HEADS = 8
RNN_HEAD_DIM = RNN_WIDTH // RNN_HEADS
CONV_WIDTH = 4
CONV_LEFT = 2
RG_LRU_C = 8.0
DEEPNORM_ALPHA = (2 * DEPTH) ** 0.25
LN_EPS = 1e-5
NEG_INF = -1e30
LOG2_E = math.log2(math.e)
F32_TINY = float(np.finfo(np.float32).tiny)

LANES = 128
SUBLANES = 8
VMEM_LIMIT = 56 * 1024 * 1024

ROW_TILE = 1024
MIX_TILE = 512
N_SEG = 32
SEGS_PER_TILE = SUBLANES
SCAN_CHUNK = 1024
WEIGHT_CAST_ROWS = 128
BIAS_ROWS = 2

BF16 = jnp.bfloat16
F32 = jnp.float32


def _sigmoid(x):
    return 0.5 * jnp.tanh(0.5 * x) + 0.5


def _layer_norm_rows(z, g, b, eps=LN_EPS):
    mu = jnp.mean(z, axis=-1, keepdims=True)
    d = z - mu
    var = jnp.mean(d * d, axis=-1, keepdims=True)
    return d * lax.rsqrt(var + eps) * g + b


def _deepnorm_rows(x, gate, y, g, b):
    return _layer_norm_rows(x + (gate * (1.0 / DEEPNORM_ALPHA)) * y, g, b, LN_EPS / DEEPNORM_ALPHA ** 2)


def _params(*semantics):
    return pltpu.CompilerParams(dimension_semantics=semantics, vmem_limit_bytes=VMEM_LIMIT)


def _cast_weight_once(w_ref, w_bf16):
    @pl.when((pl.program_id(0) == 0) & (pl.program_id(1) == 0))
    def _():
        for r in range(0, w_ref.shape[0], WEIGHT_CAST_ROWS):
            w_bf16[r:r + WEIGHT_CAST_ROWS, :] = w_ref[r:r + WEIGHT_CAST_ROWS, :].astype(BF16)


def _resident(shape, col_block=0):
    return pl.BlockSpec(shape, lambda b, i: (0,) * (len(shape) - 1) + (col_block,), pipeline_mode=pl.Buffered(1))


def _mod_kernel(c_ref, w_ref, b_ref, o_ref):
    c = c_ref[...]
    cond = c * _sigmoid(c)
    pad_rows = o_ref.shape[0] - cond.shape[0]
    if pad_rows:
        cond = jnp.concatenate([cond, jnp.zeros((pad_rows, cond.shape[1]), F32)], axis=0)
    bias = b_ref[pl.ds(pl.program_id(0), 1), :]
    o_ref[...] = jnp.dot(cond.astype(BF16), w_ref[...].astype(BF16), preferred_element_type=F32) + bias


def _modulation(c, ada_w, ada_b):
    batch = c.shape[0]
    rows = -(-batch // SUBLANES) * SUBLANES
    n_col = 3 * D_MODEL // D_MODEL
    return pl.pallas_call(
        _mod_kernel,
        grid=(DEPTH, n_col),
        in_specs=[
            pl.BlockSpec((batch, D_MODEL), lambda l, j: (0, 0)),
            pl.BlockSpec((None, D_MODEL, D_MODEL), lambda l, j: (l, 0, j)),
            pl.BlockSpec((DEPTH, D_MODEL), lambda l, j: (0, j)),
        ],
        out_specs=pl.BlockSpec((None, rows, D_MODEL), lambda l, j: (l, 0, j)),
        out_shape=jax.ShapeDtypeStruct((DEPTH, rows, 3 * D_MODEL), F32),
        compiler_params=_params("parallel", "parallel"),
        name="adaln_mod",
    )(c, ada_w, ada_b)


MOD_SHIFT, MOD_SCALE, MOD_GATE = range(3)
EVEN_LAYER, ODD_LAYER = 0, 1


def _mod_spec(mod, layer, part):
    return pl.BlockSpec((None, mod.shape[1], D_MODEL), lambda b, i: (layer, 0, part))


def _batch_row(ref):
    return ref[pl.ds(pl.program_id(0), 1), :]


def _even_in_kernel(x_ref, shift_ref, scale_ref, pos_ref, freq_ref, sign_ref, w_f32_ref,
                    q_ref, k_ref, v_ref, su_ref, sv_ref, g_ref, w_ref):
    _cast_weight_once(w_f32_ref, w_ref)
    lane = lax.broadcasted_iota(jnp.int32, (1, LANES), 1)
    first_half = (lane % HEAD_DIM) < ROPE_HALF
    left_head = lane < HEAD_DIM
    diagonal = lax.broadcasted_iota(jnp.int32, (LANES, LANES), 0) == lax.broadcasted_iota(jnp.int32, (LANES, LANES), 1)
    batch_ids = lax.broadcasted_iota(jnp.int32, (pos_ref.shape[0], LANES), 0)
    q_scale = HEAD_DIM ** -0.5 * LOG2_E

    half_rows = x_ref.shape[0] // 2
    for r in range(2):
        rows = slice(r * half_rows, (r + 1) * half_rows)
        h = (x_ref[rows, :] * (1.0 + _batch_row(scale_ref)) + _batch_row(shift_ref)).astype(BF16)
        p = jnp.dot(h, w_ref[...], preferred_element_type=F32)

        pos_cols = []
        for j in range(r * half_rows // LANES, (r + 1) * half_rows // LANES):
            pos_all = pos_ref[:, j * LANES:(j + 1) * LANES].astype(F32)
            pos_row = jnp.sum(jnp.where(batch_ids == pl.program_id(0), pos_all, 0.0), axis=0, keepdims=True)
            pos_cols.append(jnp.sum(jnp.where(diagonal, pos_row, 0.0), axis=1, keepdims=True))
        ang = jnp.concatenate(pos_cols, axis=0) * freq_ref[...]
        cs = jnp.cos(ang)
        sn = jnp.sin(ang) * sign_ref[...]

        def rotary(t):
            partner = jnp.where(first_half, pltpu.roll(t, LANES - ROPE_HALF, 1), pltpu.roll(t, ROPE_HALF, 1))
            return t * cs + partner * sn

        for j in range(ATTN_WIDTH // LANES):
            q_ref[rows, j * LANES:(j + 1) * LANES] = (
                rotary(p[:, j * LANES:(j + 1) * LANES]) * q_scale).astype(BF16)
        o = ATTN_WIDTH
        k = rotary(p[:, o:o + KV_WIDTH])
        k_sw = pltpu.roll(k, HEAD_DIM, 1)
        k_ref[rows, 0:LANES] = jnp.where(left_head, k, k_sw).astype(BF16)
        k_ref[rows, LANES:2 * LANES] = jnp.where(left_head, k_sw, k).astype(BF16)
        o += KV_WIDTH
        v = p[:, o:o + KV_WIDTH]
        v_sw = pltpu.roll(v, HEAD_DIM, 1)
        v_ref[rows, 0 * LANES:1 * LANES] = jnp.where(left_head, v, 0.0).astype(BF16)
        v_ref[rows, 1 * LANES:2 * LANES] = jnp.where(left_head, 0.0, v_sw).astype(BF16)
        v_ref[rows, 2 * LANES:3 * LANES] = jnp.where(left_head, v_sw, 0.0).astype(BF16)
        v_ref[rows, 3 * LANES:4 * LANES] = jnp.where(left_head, 0.0, v).astype(BF16)
        o += KV_WIDTH
        su_ref[rows, :] = p[:, o:o + SG_WIDTH].astype(BF16)
        o += SG_WIDTH
        sv_ref[rows, :] = p[:, o:o + SG_WIDTH].astype(BF16)
        o += SG_WIDTH
        g_ref[rows, :] = p[:, o:o + EVEN_GATE_WIDTH].astype(BF16)


def _even_in_proj(x, mod, positions, w_in):
    batch, seq, _ = x.shape
    d = np.arange(LANES) % HEAD_DIM
    inv_freq = np.power(np.float64(ROPE_THETA), -np.arange(ROPE_HALF) / ROPE_HALF)
    freq_lane = jnp.asarray(np.where(d < ROPE_DIM, inv_freq[d % ROPE_HALF], 0.0), dtype=F32).reshape(1, LANES)
    sign_lane = jnp.asarray(np.where(d < ROPE_HALF, -1.0, np.where(d < ROPE_DIM, 1.0, 0.0)),
                            dtype=F32).reshape(1, LANES)
    row = lambda width: pl.BlockSpec((None, ROW_TILE, width), lambda b, i: (b, i, 0))
    const = lambda shape: pl.BlockSpec(shape, lambda b, i: (0,) * len(shape))
    widths = (ATTN_WIDTH, 2 * KV_WIDTH, 4 * KV_WIDTH, SG_WIDTH, SG_WIDTH, EVEN_GATE_WIDTH)
    return pl.pallas_call(
        _even_in_kernel,
        grid=(batch, seq // ROW_TILE),
        in_specs=[row(D_MODEL), _mod_spec(mod, EVEN_LAYER, MOD_SHIFT), _mod_spec(mod, EVEN_LAYER, MOD_SCALE),
                  pl.BlockSpec((batch, ROW_TILE), lambda b, i: (0, i)),
                  const((1, LANES)), const((1, LANES)), _resident((D_MODEL, EVEN_IN_WIDTH))],
        out_specs=[row(w) for w in widths],
        out_shape=[jax.ShapeDtypeStruct((batch, seq, w), BF16) for w in widths],
        scratch_shapes=[pltpu.VMEM((D_MODEL, EVEN_IN_WIDTH), BF16)],
        compiler_params=_params("arbitrary", "arbitrary"),
        name="even_in_proj",
    )(x, mod, mod, positions, freq_lane, sign_lane, w_in)


def _even_mix_kernel(sink_ref, q_ref, k_ref, v_ref, su_ref, sv_ref, g_ref, x_ref, gate_ref,
                     avg_ref, avg2_ref, sgg_ref, sgb_ref, sgw_ref, sgbias_ref, wout_f32_ref, lng_ref, lnb_ref,
                     o_ref, y_scr, wout_ref):
    _cast_weight_once(wout_f32_ref, wout_ref)
    seq = k_ref.shape[0]
    n_blocks = seq // ATTN_BLOCK
    sub_blocks = MIX_TILE // ATTN_BLOCK
    tile = pl.program_id(1)
    lane = lax.broadcasted_iota(jnp.int32, (1, LANES), 1)
    left = lane < HEAD_DIM

    qi = lax.broadcasted_iota(jnp.int32, (ATTN_BLOCK, ATTN_BLOCK), 0)
    kj = lax.broadcasted_iota(jnp.int32, (ATTN_BLOCK, ATTN_BLOCK), 1)
    ones_left = jnp.broadcast_to(jnp.where(left, 1.0, 0.0).astype(BF16), (3 * ATTN_BLOCK, LANES))
    ones_right = jnp.broadcast_to(jnp.where(left, 0.0, 1.0).astype(BF16), (3 * ATTN_BLOCK, LANES))
    avg = avg_ref[...]
    sink_cols = [jnp.concatenate([jnp.full((ATTN_BLOCK, LANES), sink_ref[h] * LOG2_E, F32)
                                  for h in (4 * hk, 4 * hk + 2, 4 * hk + 1, 4 * hk + 3)], axis=0)
                 for hk in range(N_KV_HEADS)]

    def sub_block(j, carry):
        blk = tile * sub_blocks + j
        prev = jnp.maximum(blk - 1, 0)
        nxt = jnp.minimum(blk + 1, n_blocks - 1)
        rows = pl.ds(pl.multiple_of(j * ATTN_BLOCK, ATTN_BLOCK), ATTN_BLOCK)
        bias_prev = jnp.where((kj >= qi) & (blk > 0), 0.0, NEG_INF)
        bias_next = jnp.where((kj <= qi) & (blk < n_blocks - 1), 0.0, NEG_INF)
        bias_prev = jnp.concatenate([bias_prev] * Q_PER_KV, axis=0)
        bias_next = jnp.concatenate([bias_next] * Q_PER_KV, axis=0)

        def band(ref):
            return jnp.concatenate(
                [ref[pl.ds(pl.multiple_of(b * ATTN_BLOCK, ATTN_BLOCK), ATTN_BLOCK), :] for b in (prev, blk, nxt)],
                axis=0)

        k_band = band(k_ref)
        v_band = band(v_ref)
        n_sg = SG_WIDTH // LANES
        sg_cols = [slice(c * LANES, (c + 1) * LANES) for c in range(n_sg)]

        def silu_gate(cols):
            g_c = g_ref[rows, cols].astype(F32)
            return g_c * _sigmoid(g_c)

        scores = []
        for hk in range(N_KV_HEADS):
            q_cols = [q_ref[rows, (2 * hk + c) * LANES:(2 * hk + c + 1) * LANES] for c in range(2)]
            zero = jnp.zeros_like(q_cols[0])
            lhs = jnp.concatenate([jnp.where(left, qc, zero) for qc in q_cols]
                                  + [jnp.where(left, zero, qc) for qc in q_cols], axis=0)
            scores.append(lax.dot_general(lhs, k_band[:, hk * LANES:(hk + 1) * LANES], (((1,), (1,)), ((), ())),
                                          preferred_element_type=F32))
        sv_cols = [sv_ref[rows, cols] for cols in sg_cols]
        means = [jnp.dot(v_c, avg, preferred_element_type=F32) for v_c in sv_cols]

        probs, row_max = [], []
        for hk in range(N_KV_HEADS):
            s = scores[hk]
            s0 = s[:, 0:ATTN_BLOCK] + bias_prev
            s1 = s[:, ATTN_BLOCK:2 * ATTN_BLOCK]
            s2 = s[:, 2 * ATTN_BLOCK:3 * ATTN_BLOCK] + bias_next
            m = jnp.max(jnp.maximum(jnp.maximum(s0, s1), s2), axis=-1, keepdims=True)
            m = jnp.maximum(jnp.broadcast_to(m, sink_cols[hk].shape), sink_cols[hk])
            probs.append(jnp.concatenate([jnp.exp2(t - m).astype(BF16) for t in (s0, s1, s2)], axis=1))
            row_max.append(m)
        devs, sq_parts = [], []
        for v_c, mean in zip(sv_cols, means):
            dev = v_c.astype(F32) - mean
            sq = dev * dev
            sq_hi = sq.astype(BF16)
            sq_lo = (sq - sq_hi.astype(F32)).astype(BF16)
            devs.append(dev)
            sq_parts.append(jnp.concatenate([sq_hi, sq_lo], axis=1))

        pv = []
        for hk in range(N_KV_HEADS):
            p = probs[hk]
            lhs = jnp.concatenate(
                [jnp.concatenate([p[c * ATTN_BLOCK:(c + 1) * ATTN_BLOCK], p[(2 + c) * ATTN_BLOCK:(3 + c) * ATTN_BLOCK]],
                                 axis=1) for c in range(2)], axis=0)
            rhs = jnp.concatenate(
                [jnp.concatenate([v_band[:, (2 * hk) * LANES:(2 * hk + 1) * LANES], ones_left], axis=1),
                 jnp.concatenate([v_band[:, (2 * hk + 1) * LANES:(2 * hk + 2) * LANES], ones_right], axis=1)],
                axis=0)
            pv.append(jnp.dot(lhs, rhs, preferred_element_type=F32))
        variances = [jnp.dot(sq, avg2_ref[...], preferred_element_type=F32) for sq in sq_parts]

        for hk in range(N_KV_HEADS):
            for c in range(2):
                even, odd = slice(c * ATTN_BLOCK, (c + 1) * ATTN_BLOCK), slice((2 + c) * ATTN_BLOCK, (3 + c) * ATTN_BLOCK)
                sink_term = jnp.exp2(jnp.where(left, sink_cols[hk][even] - row_max[hk][even],
                                               sink_cols[hk][odd] - row_max[hk][odd]))
                o = pv[hk][c * ATTN_BLOCK:(c + 1) * ATTN_BLOCK]
                pair = o[:, 0:LANES] * (1.0 / (o[:, LANES:2 * LANES] + sink_term))
                cols = slice((2 * hk + c) * LANES, (2 * hk + c + 1) * LANES)
                y_scr[rows, cols] = (pair * silu_gate(cols)).astype(BF16)

        for c, cols in enumerate(sg_cols):
            vn = devs[c] * lax.rsqrt(variances[c] + LN_EPS) * sgg_ref[:, cols] + sgb_ref[:, cols]
            stacked = jnp.concatenate([jnp.where(left, vn, 0.0), jnp.where(left, 0.0, vn)],
                                      axis=0).astype(BF16)
            mixed = jnp.dot(sgw_ref[c], stacked, preferred_element_type=F32) + sgbias_ref[:, cols]
            y_sg = su_ref[rows, cols].astype(F32) * mixed
            gc = slice(ATTN_WIDTH + c * LANES, ATTN_WIDTH + (c + 1) * LANES)
            y_scr[rows, gc] = (y_sg * silu_gate(gc)).astype(BF16)
        return carry

    lax.fori_loop(0, sub_blocks, sub_block, 0, unroll=True)

    for r in range(2):
        half_rows = slice(r * (MIX_TILE // 2), (r + 1) * (MIX_TILE // 2))
        out = jnp.dot(y_scr[half_rows, :], wout_ref[...], preferred_element_type=F32)
        o_ref[half_rows, :] = _deepnorm_rows(x_ref[half_rows, :], _batch_row(gate_ref), out,
                                             lng_ref[EVEN_LAYER:EVEN_LAYER + 1, :], lnb_ref[EVEN_LAYER:EVEN_LAYER + 1, :])


def _even_mixer(q, k, v, su, sv, g, x, mod, sink, sg_ln_g, sg_ln_b, sg_w, sg_b, w_out, ln_g, ln_b):
    batch, seq, _ = x.shape
    lane = np.arange(LANES)
    avg = jnp.asarray((lane[:, None] // SG_GROUP_DIM == lane[None, :] // SG_GROUP_DIM) / SG_GROUP_DIM, dtype=BF16)
    w_pairs = sg_w.reshape(N_SG_GROUPS // 2, 2, SG_CHUNK, SG_CHUNK).transpose(0, 2, 1, 3)
    w_pairs = w_pairs.reshape(N_SG_GROUPS // 2, SG_CHUNK, 2 * SG_CHUNK).astype(BF16)
    bias_full = jnp.repeat(sg_b.T, SG_GROUP_DIM, axis=1)
    row = lambda width: pl.BlockSpec((None, MIX_TILE, width), lambda b, i: (b, i, 0))
    full_seq = lambda arr: pl.BlockSpec((None, seq, arr.shape[-1]), lambda b, i: (b, 0, 0))
    const = lambda shape: pl.BlockSpec(shape, lambda b, i: (0,) * len(shape))
    return pl.pallas_call(
        _even_mix_kernel,
        grid=(batch, seq // MIX_TILE),
        in_specs=[
            pl.BlockSpec(memory_space=pltpu.SMEM),
            row(ATTN_WIDTH), full_seq(k), full_seq(v), row(SG_WIDTH), row(SG_WIDTH), row(EVEN_GATE_WIDTH),
            row(D_MODEL), _mod_spec(mod, EVEN_LAYER, MOD_GATE),
            const((LANES, LANES)), const((2 * LANES, LANES)), const((1, SG_WIDTH)), const((1, SG_WIDTH)),
            const((N_SG_GROUPS // 2, SG_CHUNK, 2 * SG_CHUNK)), const((SG_CHUNK, SG_WIDTH)),
            _resident((EVEN_GATE_WIDTH, D_MODEL)), const(ln_g.shape), const(ln_b.shape),
        ],
        out_specs=row(D_MODEL),
        out_shape=jax.ShapeDtypeStruct((batch, seq, D_MODEL), F32),
        scratch_shapes=[pltpu.VMEM((MIX_TILE, EVEN_GATE_WIDTH), BF16),
                        pltpu.VMEM((EVEN_GATE_WIDTH, D_MODEL), BF16)],
        compiler_params=_params("arbitrary", "arbitrary"),
        name="even_mixer",
    )(sink, q, k, v, su, sv, g, x, mod, avg, jnp.concatenate([avg, avg], axis=0), sg_ln_g.reshape(1, -1), sg_ln_b.reshape(1, -1),
      w_pairs, bias_full, w_out, ln_g, ln_b)


def _odd_in_kernel(x_ref, shift_ref, scale_ref, w_f32_ref, xr_ref, w_ref):
    _cast_weight_once(w_f32_ref, w_ref)
    seg_len = xr_ref.shape[1]
    head_rows = seg_len * SEGS_PER_TILE
    xr_rows = xr_ref.reshape(RNN_HEADS * head_rows, LANES)
    halves = 2
    for r in range(halves):
        first = r * (SEGS_PER_TILE // halves)
        rows = slice(first * seg_len, (first + SEGS_PER_TILE // halves) * seg_len)
        h = (x_ref[rows, :] * (1.0 + _batch_row(scale_ref)) + _batch_row(shift_ref)).astype(BF16)
        p = jnp.dot(h, w_ref[...], preferred_element_type=F32)
        for s in range(SEGS_PER_TILE // halves):
            for hd in range(RNN_HEADS):
                xr_rows[pl.ds(hd * head_rows + first + s, seg_len, stride=SEGS_PER_TILE), :] = (
                    p[s * seg_len:(s + 1) * seg_len, hd * LANES:(hd + 1) * LANES])


def _odd_in_proj(x, mod, w_in):
    batch, seq, _ = x.shape
    seg_len = seq // N_SEG
    tile_rows = SEGS_PER_TILE * seg_len
    out_spec = pl.BlockSpec((None, RNN_HEADS, seg_len, SEGS_PER_TILE, RNN_HEAD_DIM), lambda b, i: (b, 0, 0, i, 0))
    out_shape = jax.ShapeDtypeStruct((batch, RNN_HEADS, seg_len, N_SEG, RNN_HEAD_DIM), F32)
    return pl.pallas_call(
        _odd_in_kernel,
        grid=(batch, seq // tile_rows),
        in_specs=[pl.BlockSpec((None, tile_rows, D_MODEL), lambda b, i: (b, i, 0)),
                  _mod_spec(mod, ODD_LAYER, MOD_SHIFT), _mod_spec(mod, ODD_LAYER, MOD_SCALE),
                  _resident((D_MODEL, RNN_WIDTH), col_block=0)],
        out_specs=out_spec,
        out_shape=out_shape,
        scratch_shapes=[pltpu.VMEM((D_MODEL, RNN_WIDTH), BF16)],
        compiler_params=_params("arbitrary", "arbitrary"),
        name="odd_in_proj",
    )(x, mod, mod, w_in)


def _rglru_kernel(x_ref, convw_ref, convb_ref, waug_ref, lam_ref, o_ref,
                  halo, a_f, b_f, a_b, b_b, edge):
    n_rows = x_ref.shape[0]
    seg_len = n_rows // N_SEG
    n_chunks = n_rows // SCAN_CHUNK
    left_rows = CONV_LEFT * N_SEG
    right_rows = (CONV_WIDTH - 1 - CONV_LEFT) * N_SEG
    seg = lax.broadcasted_iota(jnp.int32, (N_SEG, LANES), 0)

    def from_prev_segment(blk):
        return jnp.where(seg == 0, 0.0, pltpu.roll(blk, 1, 0))

    def from_next_segment(blk):
        return jnp.where(seg == N_SEG - 1, 0.0, pltpu.roll(blk, N_SEG - 1, 0))

    for t in range(CONV_LEFT):
        halo[0, pl.ds(t * N_SEG, N_SEG), :] = from_prev_segment(
            x_ref[pl.ds((seg_len - CONV_LEFT + t) * N_SEG, N_SEG), :])
    halo[0, pl.ds(left_rows, SCAN_CHUNK + right_rows), :] = x_ref[pl.ds(0, SCAN_CHUNK + right_rows), :]
    halo[1, pl.ds(0, left_rows + SCAN_CHUNK), :] = x_ref[pl.ds(n_rows - SCAN_CHUNK - left_rows, left_rows + SCAN_CHUNK), :]
    for t in range(CONV_WIDTH - 1 - CONV_LEFT):
        halo[1, pl.ds(left_rows + SCAN_CHUNK + t * N_SEG, N_SEG), :] = from_next_segment(
            x_ref[pl.ds(t * N_SEG, N_SEG), :])

    lam = lam_ref[...]
    softplus_neg = jnp.maximum(-lam, 0.0) + jnp.log1p(jnp.exp(-jnp.abs(lam)))
    half_rate = (0.5 * RG_LRU_C) * softplus_neg
    bias_lanes = jnp.broadcast_to(
        jnp.where(lax.broadcasted_iota(jnp.int32, (1, LANES), 1) < BIAS_ROWS, 1.0, 0.0).astype(BF16),
        (SCAN_CHUNK, LANES))

    def gates(base, tap):
        half_x = convb_ref[...] + sum(convw_ref[k:k + 1, :] * tap(k) for k in range(CONV_WIDTH))
        th = jnp.tanh(jnp.dot(jnp.concatenate([half_x.astype(BF16), bias_lanes], axis=1), waug_ref[...],
                              preferred_element_type=F32))
        for d, (a_ref, b_ref) in enumerate(((a_f, b_f), (a_b, b_b))):
            o = 2 * d * LANES
            neg_log_a = (th[:, o:o + LANES] + 1.0) * half_rate[d:d + 1, :]
            a = jnp.exp2(neg_log_a * (-LOG2_E))
            one_minus_a2 = jnp.tanh(neg_log_a) * (a * a + 1.0)
            root = one_minus_a2 * lax.rsqrt(jnp.maximum(one_minus_a2, F32_TINY))
            a_ref[pl.ds(base, SCAN_CHUNK), :] = a
            b_ref[pl.ds(base, SCAN_CHUNK), :] = root * ((th[:, o + LANES:o + 2 * LANES] + 1.0) * half_x)

    def inner_chunk(c, carry):
        base = pl.multiple_of(c * SCAN_CHUNK, SCAN_CHUNK)
        gates(base, lambda k: x_ref[pl.ds(base + (k - CONV_LEFT) * N_SEG, SCAN_CHUNK), :])
        return carry

    gates(0, lambda k: halo[0, pl.ds(k * N_SEG, SCAN_CHUNK), :])
    gates(n_rows - SCAN_CHUNK, lambda k: halo[1, pl.ds(k * N_SEG, SCAN_CHUNK), :])
    lax.fori_loop(1, n_chunks - 1, inner_chunk, 0, unroll=2)

    def scan_step(t, carry):
        h_f, p_f, h_b, p_b = carry
        rf = pl.ds(pl.multiple_of(t * N_SEG, N_SEG), N_SEG)
        rb = pl.ds(pl.multiple_of((seg_len - 1 - t) * N_SEG, N_SEG), N_SEG)
        a = a_f[rf, :]
        h_f = a * h_f + b_f[rf, :]
        p_f = a * p_f
        b_f[rf, :] = h_f
        a_f[rf, :] = p_f
        a = a_b[rb, :]
        h_b = a * h_b + b_b[rb, :]
        p_b = a * p_b
        b_b[rb, :] = h_b
        a_b[rb, :] = p_b
        return h_f, p_f, h_b, p_b

    zeros = jnp.zeros((N_SEG, LANES), F32)
    ones = jnp.ones((N_SEG, LANES), F32)
    h_f, p_f, h_b, p_b = lax.fori_loop(0, seg_len, scan_step, (zeros, ones, zeros, ones), unroll=True)

    edge[0] = h_f
    edge[1] = p_f
    edge[2] = h_b
    edge[3] = p_b
    enter_f = enter_b = jnp.zeros((1, LANES), F32)
    edge[4, 0:1, :] = enter_f
    edge[5, N_SEG - 1:N_SEG, :] = enter_b
    for s in range(1, N_SEG):
        enter_f = edge[0, s - 1:s, :] + edge[1, s - 1:s, :] * enter_f
        edge[4, s:s + 1, :] = enter_f
        r = N_SEG - 1 - s
        enter_b = edge[2, r + 1:r + 2, :] + edge[3, r + 1:r + 2, :] * enter_b
        edge[5, r:r + 1, :] = enter_b
    blocks = SCAN_CHUNK // N_SEG
    in_f = jnp.concatenate([edge[4]] * blocks, axis=0)
    in_b = jnp.concatenate([edge[5]] * blocks, axis=0)

    def out_chunk(c, carry):
        rows = pl.ds(pl.multiple_of(c * SCAN_CHUNK, SCAN_CHUNK), SCAN_CHUNK)
        o_ref[rows, :] = (b_f[rows, :] + a_f[rows, :] * in_f) + (b_b[rows, :] + a_b[rows, :] * in_b)
        return carry

    lax.fori_loop(0, n_rows // SCAN_CHUNK, out_chunk, 0, unroll=2)


def _rglru(xr, conv_w, conv_b, w_a, b_a, w_x, b_x, lam):
    batch, _, n_rows, _ = xr.shape
    w_cat = jnp.concatenate([w_a[0], w_x[0], w_a[1], w_x[1]], axis=-1).astype(BF16)
    per_head = lambda v: v.reshape(RNN_HEADS, 1, RNN_HEAD_DIM)
    half_b = 0.5 * jnp.concatenate([per_head(b_a[0]), per_head(b_x[0]), per_head(b_a[1]), per_head(b_x[1])],
                                   axis=-1)
    b_hi = half_b.astype(BF16)
    b_lo = (half_b - b_hi.astype(F32)).astype(BF16)
    w_aug = jnp.concatenate([w_cat, b_hi, b_lo,
                             jnp.zeros((RNN_HEADS, RNN_HEAD_DIM - BIAS_ROWS, 4 * RNN_HEAD_DIM), BF16)], axis=1)
    col = lambda rows: pl.BlockSpec((rows, RNN_HEAD_DIM), lambda b, h: (0, h))
    seq_col = pl.BlockSpec((None, None, n_rows, RNN_HEAD_DIM), lambda b, h: (b, h, 0, 0))
    halo_rows = SCAN_CHUNK + (CONV_WIDTH - 1) * N_SEG
    return pl.pallas_call(
        _rglru_kernel,
        grid=(batch, RNN_HEADS),
        in_specs=[seq_col, col(CONV_WIDTH), col(1),
                  pl.BlockSpec((None, 2 * RNN_HEAD_DIM, 4 * RNN_HEAD_DIM), lambda b, h: (h, 0, 0)),
                  col(2)],
        out_specs=seq_col,
        out_shape=jax.ShapeDtypeStruct((batch, RNN_HEADS, n_rows, RNN_HEAD_DIM), F32),
        scratch_shapes=[pltpu.VMEM((2, halo_rows, LANES), F32)]
                       + [pltpu.VMEM((n_rows, LANES), F32)] * 4
                       + [pltpu.VMEM((6, N_SEG, LANES), F32)],
        compiler_params=_params("parallel", "parallel"),
        name="rglru",
    )(xr, 0.5 * conv_w, 0.5 * conv_b.reshape(1, -1), w_aug, lam)


def _odd_out_kernel(y_ref, x_ref, shift_ref, scale_ref, gate_ref, wg_f32_ref, w_f32_ref, lng_ref, lnb_ref,
                    o_ref, wg_ref, w_ref):
    _cast_weight_once(wg_f32_ref, wg_ref)
    _cast_weight_once(w_f32_ref, w_ref)
    seg_len = y_ref.shape[1]
    head_rows = seg_len * SEGS_PER_TILE
    y_rows = y_ref.reshape(RNN_HEADS * head_rows, LANES)
    x = x_ref[...]
    h = (x * (1.0 + _batch_row(scale_ref)) + _batch_row(shift_ref)).astype(BF16)
    g_all = jnp.dot(h, wg_ref[...], preferred_element_type=F32)
    gated = []
    for s in range(SEGS_PER_TILE):
        y = jnp.concatenate([y_rows[pl.ds(hd * head_rows + s, seg_len, stride=SEGS_PER_TILE), :]
                             for hd in range(RNN_HEADS)], axis=1)
        g = g_all[s * seg_len:(s + 1) * seg_len, :]
        gated.append((y * (g * _sigmoid(g))).astype(BF16))
    out = jnp.dot(jnp.concatenate(gated, axis=0), w_ref[...], preferred_element_type=F32)
    o_ref[...] = _deepnorm_rows(x, _batch_row(gate_ref), out,
                                lng_ref[ODD_LAYER:ODD_LAYER + 1, :], lnb_ref[ODD_LAYER:ODD_LAYER + 1, :])


def _odd_out_proj(y, x, mod, w_in, w_out, ln_g, ln_b):
    batch, seq, _ = x.shape
    seg_len = seq // N_SEG
    tile_rows = SEGS_PER_TILE * seg_len
    const = lambda shape: pl.BlockSpec(shape, lambda b, i: (0,) * len(shape))
    return pl.pallas_call(
        _odd_out_kernel,
        grid=(batch, seq // tile_rows),
        in_specs=[pl.BlockSpec((None, RNN_HEADS, seg_len, SEGS_PER_TILE, RNN_HEAD_DIM),
                               lambda b, i: (b, 0, 0, i, 0)),
                  pl.BlockSpec((None, tile_rows, D_MODEL), lambda b, i: (b, i, 0)),
                  _mod_spec(mod, ODD_LAYER, MOD_SHIFT), _mod_spec(mod, ODD_LAYER, MOD_SCALE),
                  _mod_spec(mod, ODD_LAYER, MOD_GATE),
                  _resident((D_MODEL, RNN_WIDTH), col_block=1), _resident((RNN_WIDTH, D_MODEL)),
                  const(ln_g.shape), const(ln_b.shape)],
        out_specs=pl.BlockSpec((None, tile_rows, D_MODEL), lambda b, i: (b, i, 0)),
        out_shape=jax.ShapeDtypeStruct((batch, seq, D_MODEL), F32),
        scratch_shapes=[pltpu.VMEM((D_MODEL, RNN_WIDTH), BF16), pltpu.VMEM((RNN_WIDTH, D_MODEL), BF16)],
        compiler_params=_params("arbitrary", "arbitrary"),
        name="odd_out_proj",
    )(y, x, mod, mod, mod, w_in, w_out, ln_g, ln_b)


def kernel(x, c, positions, ada_w, ada_b, ln_g, ln_b, ev_w_in, ev_w_out, ev_sink, ev_sg_ln_g, ev_sg_ln_b,
           ev_sg_w, ev_sg_b, od_w_in, od_conv_w, od_conv_b, od_w_a, od_b_a, od_w_x, od_b_x, od_lam, od_w_out):
    batch, seq, d_model = x.shape
    seg_len = seq // N_SEG
    assert d_model == D_MODEL and seq % ROW_TILE == 0 and seq % MIX_TILE == 0 and seq == N_SEG * seg_len
    assert N_SEG % SEGS_PER_TILE == 0 and seq % SCAN_CHUNK == 0 and SCAN_CHUNK % N_SEG == 0
    assert ada_w.shape[0] == DEPTH == 2

    mod = _modulation(c, ada_w, ada_b)

    q, k, v, su, sv, g = _even_in_proj(x, mod, positions, ev_w_in[0])
    x1 = _even_mixer(q, k, v, su, sv, g, x, mod, ev_sink[0], ev_sg_ln_g[0], ev_sg_ln_b[0],
                     ev_sg_w[0], ev_sg_b[0], ev_w_out[0], ln_g, ln_b)

    xr = _odd_in_proj(x1, mod, od_w_in[0])
    y = _rglru(xr.reshape(batch, RNN_HEADS, seq, RNN_HEAD_DIM), od_conv_w[0], od_conv_b[0], od_w_a[0],
               od_b_a[0], od_w_x[0], od_b_x[0], od_lam[0])
    return _odd_out_proj(y.reshape(batch, RNN_HEADS, seg_len, N_SEG, RNN_HEAD_DIM), x1, mod, od_w_in[0],
                         od_w_out[0], ln_g, ln_b)
```

```python
import math

import numpy as np
import jax
import jax.numpy as jnp
from jax import lax
from jax.experimental import pallas as pl
from jax.experimental.pallas import tpu as pltpu

D_MODEL = 1024
DEPTH = 2
HEAD_DIM = 64
N_Q_HEADS = 8
N_KV_HEADS = 2
Q_PER_KV = N_Q_HEADS // N_KV_HEADS
ATTN_WIDTH = N_Q_HEADS * HEAD_DIM
KV_WIDTH = N_KV_HEADS * HEAD_DIM
ATTN_BLOCK = 128
ROPE_DIM = HEAD_DIM // 4
ROPE_HALF = ROPE_DIM // 2
ROPE_THETA = 500000.0
N_SG_GROUPS = 8
SG_GROUP_DIM = 64
SG_WIDTH = N_SG_GROUPS * SG_GROUP_DIM
SG_CHUNK = 128
EVEN_GATE_WIDTH = ATTN_WIDTH + SG_WIDTH
EVEN_IN_WIDTH = ATTN_WIDTH + 2 * KV_WIDTH + 2 * SG_WIDTH + EVEN_GATE_WIDTH
RNN_WIDTH = D_MODEL
RNN_HEADS = 8
RNN_HEAD_DIM = RNN_WIDTH // RNN_HEADS
CONV_WIDTH = 4
CONV_LEFT = 2
RG_LRU_C = 8.0
DEEPNORM_ALPHA = (2 * DEPTH) ** 0.25
LN_EPS = 1e-5
NEG_INF = -1e30
LOG2_E = math.log2(math.e)
F32_TINY = float(np.finfo(np.float32).tiny)

LANES = 128
SUBLANES = 8
VMEM_LIMIT = 56 * 1024 * 1024

ROW_TILE = 1024
MIX_TILE = 512
N_SEG = 32
SEGS_PER_TILE = SUBLANES
SCAN_CHUNK = 1024
WEIGHT_CAST_ROWS = 128
BIAS_ROWS = 2

BF16 = jnp.bfloat16
F32 = jnp.float32


def _sigmoid(x):
    return 0.5 * jnp.tanh(0.5 * x) + 0.5


def _layer_norm_rows(z, g, b, eps=LN_EPS):
    mu = jnp.mean(z, axis=-1, keepdims=True)
    d = z - mu
    var = jnp.mean(d * d, axis=-1, keepdims=True)
    return d * lax.rsqrt(var + eps) * g + b


def _deepnorm_rows(x, gate, y, g, b):
    return _layer_norm_rows(x + (gate * (1.0 / DEEPNORM_ALPHA)) * y, g, b, LN_EPS / DEEPNORM_ALPHA ** 2)


def _params(*semantics):
    return pltpu.CompilerParams(dimension_semantics=semantics, vmem_limit_bytes=VMEM_LIMIT)


def _cast_weight_once(w_ref, w_bf16):
    @pl.when((pl.program_id(0) == 0) & (pl.program_id(1) == 0))
    def _():
        for r in range(0, w_ref.shape[0], WEIGHT_CAST_ROWS):
            w_bf16[r:r + WEIGHT_CAST_ROWS, :] = w_ref[r:r + WEIGHT_CAST_ROWS, :].astype(BF16)


def _resident(shape, col_block=0):
    return pl.BlockSpec(shape, lambda b, i: (0,) * (len(shape) - 1) + (col_block,), pipeline_mode=pl.Buffered(1))


def _mod_kernel(c_ref, w_ref, b_ref, o_ref):
    c = c_ref[...]
    cond = c * _sigmoid(c)
    pad_rows = o_ref.shape[0] - cond.shape[0]
    if pad_rows:
        cond = jnp.concatenate([cond, jnp.zeros((pad_rows, cond.shape[1]), F32)], axis=0)
    bias = b_ref[pl.ds(pl.program_id(0), 1), :]
    o_ref[...] = jnp.dot(cond.astype(BF16), w_ref[...].astype(BF16), preferred_element_type=F32) + bias


def _modulation(c, ada_w, ada_b):
    batch = c.shape[0]
    rows = -(-batch // SUBLANES) * SUBLANES
    n_col = 3 * D_MODEL // D_MODEL
    return pl.pallas_call(
        _mod_kernel,
        grid=(DEPTH, n_col),
        in_specs=[
            pl.BlockSpec((batch, D_MODEL), lambda l, j: (0, 0)),
            pl.BlockSpec((None, D_MODEL, D_MODEL), lambda l, j: (l, 0, j)),
            pl.BlockSpec((DEPTH, D_MODEL), lambda l, j: (0, j)),
        ],
        out_specs=pl.BlockSpec((None, rows, D_MODEL), lambda l, j: (l, 0, j)),
        out_shape=jax.ShapeDtypeStruct((DEPTH, rows, 3 * D_MODEL), F32),
        compiler_params=_params("parallel", "parallel"),
        name="adaln_mod",
    )(c, ada_w, ada_b)


MOD_SHIFT, MOD_SCALE, MOD_GATE = range(3)
EVEN_LAYER, ODD_LAYER = 0, 1


def _mod_spec(mod, layer, part):
    return pl.BlockSpec((None, mod.shape[1], D_MODEL), lambda b, i: (layer, 0, part))


def _batch_row(ref):
    return ref[pl.ds(pl.program_id(0), 1), :]


def _even_in_kernel(x_ref, shift_ref, scale_ref, pos_ref, freq_ref, sign_ref, w_f32_ref,
                    q_ref, k_ref, v_ref, su_ref, sv_ref, g_ref, w_ref):
    _cast_weight_once(w_f32_ref, w_ref)
    lane = lax.broadcasted_iota(jnp.int32, (1, LANES), 1)
    first_half = (lane % HEAD_DIM) < ROPE_HALF
    left_head = lane < HEAD_DIM
    diagonal = lax.broadcasted_iota(jnp.int32, (LANES, LANES), 0) == lax.broadcasted_iota(jnp.int32, (LANES, LANES), 1)
    batch_ids = lax.broadcasted_iota(jnp.int32, (pos_ref.shape[0], LANES), 0)
    q_scale = HEAD_DIM ** -0.5 * LOG2_E

    half_rows = x_ref.shape[0] // 2
    for r in range(2):
        rows = slice(r * half_rows, (r + 1) * half_rows)
        h = (x_ref[rows, :] * (1.0 + _batch_row(scale_ref)) + _batch_row(shift_ref)).astype(BF16)
        p = jnp.dot(h, w_ref[...], preferred_element_type=F32)

        pos_cols = []
        for j in range(r * half_rows // LANES, (r + 1) * half_rows // LANES):
            pos_all = pos_ref[:, j * LANES:(j + 1) * LANES].astype(F32)
            pos_row = jnp.sum(jnp.where(batch_ids == pl.program_id(0), pos_all, 0.0), axis=0, keepdims=True)
            pos_cols.append(jnp.sum(jnp.where(diagonal, pos_row, 0.0), axis=1, keepdims=True))
        ang = jnp.concatenate(pos_cols, axis=0) * freq_ref[...]
        cs = jnp.cos(ang)
        sn = jnp.sin(ang) * sign_ref[...]

        def rotary(t):
            partner = jnp.where(first_half, pltpu.roll(t, LANES - ROPE_HALF, 1), pltpu.roll(t, ROPE_HALF, 1))
            return t * cs + partner * sn

        for j in range(ATTN_WIDTH // LANES):
            q_ref[rows, j * LANES:(j + 1) * LANES] = (
                rotary(p[:, j * LANES:(j + 1) * LANES]) * q_scale).astype(BF16)
        o = ATTN_WIDTH
        k = rotary(p[:, o:o + KV_WIDTH])
        k_sw = pltpu.roll(k, HEAD_DIM, 1)
        k_ref[rows, 0:LANES] = jnp.where(left_head, k, k_sw).astype(BF16)
        k_ref[rows, LANES:2 * LANES] = jnp.where(left_head, k_sw, k).astype(BF16)
        o += KV_WIDTH
        v = p[:, o:o + KV_WIDTH]
        v_sw = pltpu.roll(v, HEAD_DIM, 1)
        v_ref[rows, 0 * LANES:1 * LANES] = jnp.where(left_head, v, 0.0).astype(BF16)
        v_ref[rows, 1 * LANES:2 * LANES] = jnp.where(left_head, 0.0, v_sw).astype(BF16)
        v_ref[rows, 2 * LANES:3 * LANES] = jnp.where(left_head, v_sw, 0.0).astype(BF16)
        v_ref[rows, 3 * LANES:4 * LANES] = jnp.where(left_head, 0.0, v).astype(BF16)
        o += KV_WIDTH
        su_ref[rows, :] = p[:, o:o + SG_WIDTH].astype(BF16)
        o += SG_WIDTH
        sv_ref[rows, :] = p[:, o:o + SG_WIDTH].astype(BF16)
        o += SG_WIDTH
        g_ref[rows, :] = p[:, o:o + EVEN_GATE_WIDTH].astype(BF16)


def _even_in_proj(x, mod, positions, w_in):
    batch, seq, _ = x.shape
    d = np.arange(LANES) % HEAD_DIM
    inv_freq = np.power(np.float64(ROPE_THETA), -np.arange(ROPE_HALF) / ROPE_HALF)
    freq_lane = jnp.asarray(np.where(d < ROPE_DIM, inv_freq[d % ROPE_HALF], 0.0), dtype=F32).reshape(1, LANES)
    sign_lane = jnp.asarray(np.where(d < ROPE_HALF, -1.0, np.where(d < ROPE_DIM, 1.0, 0.0)),
                            dtype=F32).reshape(1, LANES)
    row = lambda width: pl.BlockSpec((None, ROW_TILE, width), lambda b, i: (b, i, 0))
    const = lambda shape: pl.BlockSpec(shape, lambda b, i: (0,) * len(shape))
    widths = (ATTN_WIDTH, 2 * KV_WIDTH, 4 * KV_WIDTH, SG_WIDTH, SG_WIDTH, EVEN_GATE_WIDTH)
    return pl.pallas_call(
        _even_in_kernel,
        grid=(batch, seq // ROW_TILE),
        in_specs=[row(D_MODEL), _mod_spec(mod, EVEN_LAYER, MOD_SHIFT), _mod_spec(mod, EVEN_LAYER, MOD_SCALE),
                  pl.BlockSpec((batch, ROW_TILE), lambda b, i: (0, i)),
                  const((1, LANES)), const((1, LANES)), _resident((D_MODEL, EVEN_IN_WIDTH))],
        out_specs=[row(w) for w in widths],
        out_shape=[jax.ShapeDtypeStruct((batch, seq, w), BF16) for w in widths],
        scratch_shapes=[pltpu.VMEM((D_MODEL, EVEN_IN_WIDTH), BF16)],
        compiler_params=_params("arbitrary", "arbitrary"),
        name="even_in_proj",
    )(x, mod, mod, positions, freq_lane, sign_lane, w_in)


def _even_mix_kernel(sink_ref, q_ref, k_ref, v_ref, su_ref, sv_ref, g_ref, x_ref, gate_ref,
                     avg_ref, avg2_ref, sgg_ref, sgb_ref, sgw_ref, sgbias_ref, wout_f32_ref, lng_ref, lnb_ref,
                     o_ref, y_scr, wout_ref):
    _cast_weight_once(wout_f32_ref, wout_ref)
    seq = k_ref.shape[0]
    n_blocks = seq // ATTN_BLOCK
    sub_blocks = MIX_TILE // ATTN_BLOCK
    tile = pl.program_id(1)
    lane = lax.broadcasted_iota(jnp.int32, (1, LANES), 1)
    left = lane < HEAD_DIM

    qi = lax.broadcasted_iota(jnp.int32, (ATTN_BLOCK, ATTN_BLOCK), 0)
    kj = lax.broadcasted_iota(jnp.int32, (ATTN_BLOCK, ATTN_BLOCK), 1)
    ones_left = jnp.broadcast_to(jnp.where(left, 1.0, 0.0).astype(BF16), (3 * ATTN_BLOCK, LANES))
    ones_right = jnp.broadcast_to(jnp.where(left, 0.0, 1.0).astype(BF16), (3 * ATTN_BLOCK, LANES))
    avg = avg_ref[...]
    sink_cols = [jnp.concatenate([jnp.full((ATTN_BLOCK, LANES), sink_ref[h] * LOG2_E, F32)
                                  for h in (4 * hk, 4 * hk + 2, 4 * hk + 1, 4 * hk + 3)], axis=0)
                 for hk in range(N_KV_HEADS)]

    def sub_block(j, carry):
        blk = tile * sub_blocks + j
        prev = jnp.maximum(blk - 1, 0)
        nxt = jnp.minimum(blk + 1, n_blocks - 1)
        rows = pl.ds(pl.multiple_of(j * ATTN_BLOCK, ATTN_BLOCK), ATTN_BLOCK)
        bias_prev = jnp.where((kj >= qi) & (blk > 0), 0.0, NEG_INF)
        bias_next = jnp.where((kj <= qi) & (blk < n_blocks - 1), 0.0, NEG_INF)
        bias_prev = jnp.concatenate([bias_prev] * Q_PER_KV, axis=0)
        bias_next = jnp.concatenate([bias_next] * Q_PER_KV, axis=0)

        def band(ref):
            return jnp.concatenate(
                [ref[pl.ds(pl.multiple_of(b * ATTN_BLOCK, ATTN_BLOCK), ATTN_BLOCK), :] for b in (prev, blk, nxt)],
                axis=0)

        k_band = band(k_ref)
        v_band = band(v_ref)
        n_sg = SG_WIDTH // LANES
        sg_cols = [slice(c * LANES, (c + 1) * LANES) for c in range(n_sg)]

        def silu_gate(cols):
            g_c = g_ref[rows, cols].astype(F32)
            return g_c * _sigmoid(g_c)

        scores = []
        for hk in range(N_KV_HEADS):
            q_cols = [q_ref[rows, (2 * hk + c) * LANES:(2 * hk + c + 1) * LANES] for c in range(2)]
            zero = jnp.zeros_like(q_cols[0])
            lhs = jnp.concatenate([jnp.where(left, qc, zero) for qc in q_cols]
                                  + [jnp.where(left, zero, qc) for qc in q_cols], axis=0)
            scores.append(lax.dot_general(lhs, k_band[:, hk * LANES:(hk + 1) * LANES], (((1,), (1,)), ((), ())),
                                          preferred_element_type=F32))
        sv_cols = [sv_ref[rows, cols] for cols in sg_cols]
        means = [jnp.dot(v_c, avg, preferred_element_type=F32) for v_c in sv_cols]

        probs, row_max = [], []
        for hk in range(N_KV_HEADS):
            s = scores[hk]
            s0 = s[:, 0:ATTN_BLOCK] + bias_prev
            s1 = s[:, ATTN_BLOCK:2 * ATTN_BLOCK]
            s2 = s[:, 2 * ATTN_BLOCK:3 * ATTN_BLOCK] + bias_next
            m = jnp.max(jnp.maximum(jnp.maximum(s0, s1), s2), axis=-1, keepdims=True)
            m = jnp.maximum(jnp.broadcast_to(m, sink_cols[hk].shape), sink_cols[hk])
            probs.append(jnp.concatenate([jnp.exp2(t - m).astype(BF16) for t in (s0, s1, s2)], axis=1))
            row_max.append(m)
        devs, sq_parts = [], []
        for v_c, mean in zip(sv_cols, means):
            dev = v_c.astype(F32) - mean
            sq = dev * dev
            sq_hi = sq.astype(BF16)
            sq_lo = (sq - sq_hi.astype(F32)).astype(BF16)
            devs.append(dev)
            sq_parts.append(jnp.concatenate([sq_hi, sq_lo], axis=1))

        pv = []
        for hk in range(N_KV_HEADS):
            p = probs[hk]
            lhs = jnp.concatenate(
                [jnp.concatenate([p[c * ATTN_BLOCK:(c + 1) * ATTN_BLOCK], p[(2 + c) * ATTN_BLOCK:(3 + c) * ATTN_BLOCK]],
                                 axis=1) for c in range(2)], axis=0)
            rhs = jnp.concatenate(
                [jnp.concatenate([v_band[:, (2 * hk) * LANES:(2 * hk + 1) * LANES], ones_left], axis=1),
                 jnp.concatenate([v_band[:, (2 * hk + 1) * LANES:(2 * hk + 2) * LANES], ones_right], axis=1)],
                axis=0)
            pv.append(jnp.dot(lhs, rhs, preferred_element_type=F32))
        variances = [jnp.dot(sq, avg2_ref[...], preferred_element_type=F32) for sq in sq_parts]

        for hk in range(N_KV_HEADS):
            for c in range(2):
                even, odd = slice(c * ATTN_BLOCK, (c + 1) * ATTN_BLOCK), slice((2 + c) * ATTN_BLOCK, (3 + c) * ATTN_BLOCK)
                sink_term = jnp.exp2(jnp.where(left, sink_cols[hk][even] - row_max[hk][even],
                                               sink_cols[hk][odd] - row_max[hk][odd]))
                o = pv[hk][c * ATTN_BLOCK:(c + 1) * ATTN_BLOCK]
                pair = o[:, 0:LANES] * (1.0 / (o[:, LANES:2 * LANES] + sink_term))
                cols = slice((2 * hk + c) * LANES, (2 * hk + c + 1) * LANES)
                y_scr[rows, cols] = (pair * silu_gate(cols)).astype(BF16)

        for c, cols in enumerate(sg_cols):
            vn = devs[c] * lax.rsqrt(variances[c] + LN_EPS) * sgg_ref[:, cols] + sgb_ref[:, cols]
            stacked = jnp.concatenate([jnp.where(left, vn, 0.0), jnp.where(left, 0.0, vn)],
                                      axis=0).astype(BF16)
            mixed = jnp.dot(sgw_ref[c], stacked, preferred_element_type=F32) + sgbias_ref[:, cols]
            y_sg = su_ref[rows, cols].astype(F32) * mixed
            gc = slice(ATTN_WIDTH + c * LANES, ATTN_WIDTH + (c + 1) * LANES)
            y_scr[rows, gc] = (y_sg * silu_gate(gc)).astype(BF16)
        return carry

    lax.fori_loop(0, sub_blocks, sub_block, 0, unroll=True)

    for r in range(2):
        half_rows = slice(r * (MIX_TILE // 2), (r + 1) * (MIX_TILE // 2))
        out = jnp.dot(y_scr[half_rows, :], wout_ref[...], preferred_element_type=F32)
        o_ref[half_rows, :] = _deepnorm_rows(x_ref[half_rows, :], _batch_row(gate_ref), out,
                                             lng_ref[EVEN_LAYER:EVEN_LAYER + 1, :], lnb_ref[EVEN_LAYER:EVEN_LAYER + 1, :])


def _even_mixer(q, k, v, su, sv, g, x, mod, sink, sg_ln_g, sg_ln_b, sg_w, sg_b, w_out, ln_g, ln_b):
    batch, seq, _ = x.shape
    lane = np.arange(LANES)
    avg = jnp.asarray((lane[:, None] // SG_GROUP_DIM == lane[None, :] // SG_GROUP_DIM) / SG_GROUP_DIM, dtype=BF16)
    w_pairs = sg_w.reshape(N_SG_GROUPS // 2, 2, SG_CHUNK, SG_CHUNK).transpose(0, 2, 1, 3)
    w_pairs = w_pairs.reshape(N_SG_GROUPS // 2, SG_CHUNK, 2 * SG_CHUNK).astype(BF16)
    bias_full = jnp.repeat(sg_b.T, SG_GROUP_DIM, axis=1)
    row = lambda width: pl.BlockSpec((None, MIX_TILE, width), lambda b, i: (b, i, 0))
    full_seq = lambda arr: pl.BlockSpec((None, seq, arr.shape[-1]), lambda b, i: (b, 0, 0))
    const = lambda shape: pl.BlockSpec(shape, lambda b, i: (0,) * len(shape))
    return pl.pallas_call(
        _even_mix_kernel,
        grid=(batch, seq // MIX_TILE),
        in_specs=[
            pl.BlockSpec(memory_space=pltpu.SMEM),
            row(ATTN_WIDTH), full_seq(k), full_seq(v), row(SG_WIDTH), row(SG_WIDTH), row(EVEN_GATE_WIDTH),
            row(D_MODEL), _mod_spec(mod, EVEN_LAYER, MOD_GATE),
            const((LANES, LANES)), const((2 * LANES, LANES)), const((1, SG_WIDTH)), const((1, SG_WIDTH)),
            const((N_SG_GROUPS // 2, SG_CHUNK, 2 * SG_CHUNK)), const((SG_CHUNK, SG_WIDTH)),
            _resident((EVEN_GATE_WIDTH, D_MODEL)), const(ln_g.shape), const(ln_b.shape),
        ],
        out_specs=row(D_MODEL),
        out_shape=jax.ShapeDtypeStruct((batch, seq, D_MODEL), F32),
        scratch_shapes=[pltpu.VMEM((MIX_TILE, EVEN_GATE_WIDTH), BF16),
                        pltpu.VMEM((EVEN_GATE_WIDTH, D_MODEL), BF16)],
        compiler_params=_params("arbitrary", "arbitrary"),
        name="even_mixer",
    )(sink, q, k, v, su, sv, g, x, mod, avg, jnp.concatenate([avg, avg], axis=0), sg_ln_g.reshape(1, -1), sg_ln_b.reshape(1, -1),
      w_pairs, bias_full, w_out, ln_g, ln_b)


def _odd_in_kernel(x_ref, shift_ref, scale_ref, w_f32_ref, xr_ref, w_ref):
    _cast_weight_once(w_f32_ref, w_ref)
    seg_len = xr_ref.shape[1]
    head_rows = seg_len * SEGS_PER_TILE
    xr_rows = xr_ref.reshape(RNN_HEADS * head_rows, LANES)
    halves = 2
    for r in range(halves):
        first = r * (SEGS_PER_TILE // halves)
        rows = slice(first * seg_len, (first + SEGS_PER_TILE // halves) * seg_len)
        h = (x_ref[rows, :] * (1.0 + _batch_row(scale_ref)) + _batch_row(shift_ref)).astype(BF16)
        p = jnp.dot(h, w_ref[...], preferred_element_type=F32)
        for s in range(SEGS_PER_TILE // halves):
            for hd in range(RNN_HEADS):
                xr_rows[pl.ds(hd * head_rows + first + s, seg_len, stride=SEGS_PER_TILE), :] = (
                    p[s * seg_len:(s + 1) * seg_len, hd * LANES:(hd + 1) * LANES])


def _odd_in_proj(x, mod, w_in):
    batch, seq, _ = x.shape
    seg_len = seq // N_SEG
    tile_rows = SEGS_PER_TILE * seg_len
    out_spec = pl.BlockSpec((None, RNN_HEADS, seg_len, SEGS_PER_TILE, RNN_HEAD_DIM), lambda b, i: (b, 0, 0, i, 0))
    out_shape = jax.ShapeDtypeStruct((batch, RNN_HEADS, seg_len, N_SEG, RNN_HEAD_DIM), F32)
    return pl.pallas_call(
        _odd_in_kernel,
        grid=(batch, seq // tile_rows),
        in_specs=[pl.BlockSpec((None, tile_rows, D_MODEL), lambda b, i: (b, i, 0)),
                  _mod_spec(mod, ODD_LAYER, MOD_SHIFT), _mod_spec(mod, ODD_LAYER, MOD_SCALE),
                  _resident((D_MODEL, RNN_WIDTH), col_block=0)],
        out_specs=out_spec,
        out_shape=out_shape,
        scratch_shapes=[pltpu.VMEM((D_MODEL, RNN_WIDTH), BF16)],
        compiler_params=_params("arbitrary", "arbitrary"),
        name="odd_in_proj",
    )(x, mod, mod, w_in)


def _rglru_kernel(x_ref, convw_ref, convb_ref, waug_ref, lam_ref, o_ref,
                  halo, a_f, b_f, a_b, b_b, edge):
    n_rows = x_ref.shape[0]
    seg_len = n_rows // N_SEG
    n_chunks = n_rows // SCAN_CHUNK
    left_rows = CONV_LEFT * N_SEG
    right_rows = (CONV_WIDTH - 1 - CONV_LEFT) * N_SEG
    seg = lax.broadcasted_iota(jnp.int32, (N_SEG, LANES), 0)

    def from_prev_segment(blk):
        return jnp.where(seg == 0, 0.0, pltpu.roll(blk, 1, 0))

    def from_next_segment(blk):
        return jnp.where(seg == N_SEG - 1, 0.0, pltpu.roll(blk, N_SEG - 1, 0))

    for t in range(CONV_LEFT):
        halo[0, pl.ds(t * N_SEG, N_SEG), :] = from_prev_segment(
            x_ref[pl.ds((seg_len - CONV_LEFT + t) * N_SEG, N_SEG), :])
    halo[0, pl.ds(left_rows, SCAN_CHUNK + right_rows), :] = x_ref[pl.ds(0, SCAN_CHUNK + right_rows), :]
    halo[1, pl.ds(0, left_rows + SCAN_CHUNK), :] = x_ref[pl.ds(n_rows - SCAN_CHUNK - left_rows, left_rows + SCAN_CHUNK), :]
    for t in range(CONV_WIDTH - 1 - CONV_LEFT):
        halo[1, pl.ds(left_rows + SCAN_CHUNK + t * N_SEG, N_SEG), :] = from_next_segment(
            x_ref[pl.ds(t * N_SEG, N_SEG), :])

    lam = lam_ref[...]
    softplus_neg = jnp.maximum(-lam, 0.0) + jnp.log1p(jnp.exp(-jnp.abs(lam)))
    half_rate = (0.5 * RG_LRU_C) * softplus_neg
    bias_lanes = jnp.broadcast_to(
        jnp.where(lax.broadcasted_iota(jnp.int32, (1, LANES), 1) < BIAS_ROWS, 1.0, 0.0).astype(BF16),
        (SCAN_CHUNK, LANES))

    def gates(base, tap):
        half_x = convb_ref[...] + sum(convw_ref[k:k + 1, :] * tap(k) for k in range(CONV_WIDTH))
        th = jnp.tanh(jnp.dot(jnp.concatenate([half_x.astype(BF16), bias_lanes], axis=1), waug_ref[...],
                              preferred_element_type=F32))
        for d, (a_ref, b_ref) in enumerate(((a_f, b_f), (a_b, b_b))):
            o = 2 * d * LANES
            neg_log_a = (th[:, o:o + LANES] + 1.0) * half_rate[d:d + 1, :]
            a = jnp.exp2(neg_log_a * (-LOG2_E))
            one_minus_a2 = jnp.tanh(neg_log_a) * (a * a + 1.0)
            root = one_minus_a2 * lax.rsqrt(jnp.maximum(one_minus_a2, F32_TINY))
            a_ref[pl.ds(base, SCAN_CHUNK), :] = a
            b_ref[pl.ds(base, SCAN_CHUNK), :] = root * ((th[:, o + LANES:o + 2 * LANES] + 1.0) * half_x)

    def inner_chunk(c, carry):
        base = pl.multiple_of(c * SCAN_CHUNK, SCAN_CHUNK)
        gates(base, lambda k: x_ref[pl.ds(base + (k - CONV_LEFT) * N_SEG, SCAN_CHUNK), :])
        return carry

    gates(0, lambda k: halo[0, pl.ds(k * N_SEG, SCAN_CHUNK), :])
    gates(n_rows - SCAN_CHUNK, lambda k: halo[1, pl.ds(k * N_SEG, SCAN_CHUNK), :])
    lax.fori_loop(1, n_chunks - 1, inner_chunk, 0, unroll=2)

    def scan_step(t, carry):
        h_f, p_f, h_b, p_b = carry
        rf = pl.ds(pl.multiple_of(t * N_SEG, N_SEG), N_SEG)
        rb = pl.ds(pl.multiple_of((seg_len - 1 - t) * N_SEG, N_SEG), N_SEG)
        a = a_f[rf, :]
        h_f = a * h_f + b_f[rf, :]
        p_f = a * p_f
        b_f[rf, :] = h_f
        a_f[rf, :] = p_f
        a = a_b[rb, :]
        h_b = a * h_b + b_b[rb, :]
        p_b = a * p_b
        b_b[rb, :] = h_b
        a_b[rb, :] = p_b
        return h_f, p_f, h_b, p_b

    zeros = jnp.zeros((N_SEG, LANES), F32)
    ones = jnp.ones((N_SEG, LANES), F32)
    h_f, p_f, h_b, p_b = lax.fori_loop(0, seg_len, scan_step, (zeros, ones, zeros, ones), unroll=True)

    edge[0] = h_f
    edge[1] = p_f
    edge[2] = h_b
    edge[3] = p_b
    enter_f = enter_b = jnp.zeros((1, LANES), F32)
    edge[4, 0:1, :] = enter_f
    edge[5, N_SEG - 1:N_SEG, :] = enter_b
    for s in range(1, N_SEG):
        enter_f = edge[0, s - 1:s, :] + edge[1, s - 1:s, :] * enter_f
        edge[4, s:s + 1, :] = enter_f
        r = N_SEG - 1 - s
        enter_b = edge[2, r + 1:r + 2, :] + edge[3, r + 1:r + 2, :] * enter_b
        edge[5, r:r + 1, :] = enter_b
    blocks = SCAN_CHUNK // N_SEG
    in_f = jnp.concatenate([edge[4]] * blocks, axis=0)
    in_b = jnp.concatenate([edge[5]] * blocks, axis=0)

    def out_chunk(c, carry):
        rows = pl.ds(pl.multiple_of(c * SCAN_CHUNK, SCAN_CHUNK), SCAN_CHUNK)
        o_ref[rows, :] = (b_f[rows, :] + a_f[rows, :] * in_f) + (b_b[rows, :] + a_b[rows, :] * in_b)
        return carry

    lax.fori_loop(0, n_rows // SCAN_CHUNK, out_chunk, 0, unroll=2)


def _rglru(xr, conv_w, conv_b, w_a, b_a, w_x, b_x, lam):
    batch, _, n_rows, _ = xr.shape
    w_cat = jnp.concatenate([w_a[0], w_x[0], w_a[1], w_x[1]], axis=-1).astype(BF16)
    per_head = lambda v: v.reshape(RNN_HEADS, 1, RNN_HEAD_DIM)
    half_b = 0.5 * jnp.concatenate([per_head(b_a[0]), per_head(b_x[0]), per_head(b_a[1]), per_head(b_x[1])],
                                   axis=-1)
    b_hi = half_b.astype(BF16)
    b_lo = (half_b - b_hi.astype(F32)).astype(BF16)
    w_aug = jnp.concatenate([w_cat, b_hi, b_lo,
                             jnp.zeros((RNN_HEADS, RNN_HEAD_DIM - BIAS_ROWS, 4 * RNN_HEAD_DIM), BF16)], axis=1)
    col = lambda rows: pl.BlockSpec((rows, RNN_HEAD_DIM), lambda b, h: (0, h))
    seq_col = pl.BlockSpec((None, None, n_rows, RNN_HEAD_DIM), lambda b, h: (b, h, 0, 0))
    halo_rows = SCAN_CHUNK + (CONV_WIDTH - 1) * N_SEG
    return pl.pallas_call(
        _rglru_kernel,
        grid=(batch, RNN_HEADS),
        in_specs=[seq_col, col(CONV_WIDTH), col(1),
                  pl.BlockSpec((None, 2 * RNN_HEAD_DIM, 4 * RNN_HEAD_DIM), lambda b, h: (h, 0, 0)),
                  col(2)],
        out_specs=seq_col,
        out_shape=jax.ShapeDtypeStruct((batch, RNN_HEADS, n_rows, RNN_HEAD_DIM), F32),
        scratch_shapes=[pltpu.VMEM((2, halo_rows, LANES), F32)]
                       + [pltpu.VMEM((n_rows, LANES), F32)] * 4
                       + [pltpu.VMEM((6, N_SEG, LANES), F32)],
        compiler_params=_params("parallel", "parallel"),
        name="rglru",
    )(xr, 0.5 * conv_w, 0.5 * conv_b.reshape(1, -1), w_aug, lam)


def _odd_out_kernel(y_ref, x_ref, shift_ref, scale_ref, gate_ref, wg_f32_ref, w_f32_ref, lng_ref, lnb_ref,
                    o_ref, wg_ref, w_ref):
    _cast_weight_once(wg_f32_ref, wg_ref)
    _cast_weight_once(w_f32_ref, w_ref)
    seg_len = y_ref.shape[1]
    head_rows = seg_len * SEGS_PER_TILE
    y_rows = y_ref.reshape(RNN_HEADS * head_rows, LANES)
    x = x_ref[...]
    h = (x * (1.0 + _batch_row(scale_ref)) + _batch_row(shift_ref)).astype(BF16)
    g_all = jnp.dot(h, wg_ref[...], preferred_element_type=F32)
    gated = []
    for s in range(SEGS_PER_TILE):
        y = jnp.concatenate([y_rows[pl.ds(hd * head_rows + s, seg_len, stride=SEGS_PER_TILE), :]
                             for hd in range(RNN_HEADS)], axis=1)
        g = g_all[s * seg_len:(s + 1) * seg_len, :]
        gated.append((y * (g * _sigmoid(g))).astype(BF16))
    half_segs = SEGS_PER_TILE // 2
    half_rows = half_segs * seg_len
    for half in range(2):
        rows = slice(half * half_rows, (half + 1) * half_rows)
        out = jnp.dot(jnp.concatenate(gated[half * half_segs:(half + 1) * half_segs], axis=0), w_ref[...],
                      preferred_element_type=F32)
        o_ref[rows, :] = _deepnorm_rows(x[rows, :], _batch_row(gate_ref), out,
                                        lng_ref[ODD_LAYER:ODD_LAYER + 1, :], lnb_ref[ODD_LAYER:ODD_LAYER + 1, :])


def _odd_out_proj(y, x, mod, w_in, w_out, ln_g, ln_b):
    batch, seq, _ = x.shape
    seg_len = seq // N_SEG
    tile_rows = SEGS_PER_TILE * seg_len
    const = lambda shape: pl.BlockSpec(shape, lambda b, i: (0,) * len(shape))
    return pl.pallas_call(
        _odd_out_kernel,
        grid=(batch, seq // tile_rows),
        in_specs=[pl.BlockSpec((None, RNN_HEADS, seg_len, SEGS_PER_TILE, RNN_HEAD_DIM),
                               lambda b, i: (b, 0, 0, i, 0)),
                  pl.BlockSpec((None, tile_rows, D_MODEL), lambda b, i: (b, i, 0)),
                  _mod_spec(mod, ODD_LAYER, MOD_SHIFT), _mod_spec(mod, ODD_LAYER, MOD_SCALE),
                  _mod_spec(mod, ODD_LAYER, MOD_GATE),
                  _resident((D_MODEL, RNN_WIDTH), col_block=1), _resident((RNN_WIDTH, D_MODEL)),
                  const(ln_g.shape), const(ln_b.shape)],
        out_specs=pl.BlockSpec((None, tile_rows, D_MODEL), lambda b, i: (b, i, 0)),
        out_shape=jax.ShapeDtypeStruct((batch, seq, D_MODEL), F32),
        scratch_shapes=[pltpu.VMEM((D_MODEL, RNN_WIDTH), BF16), pltpu.VMEM((RNN_WIDTH, D_MODEL), BF16)],
        compiler_params=_params("arbitrary", "arbitrary"),
        name="odd_out_proj",
    )(y, x, mod, mod, mod, w_in, w_out, ln_g, ln_b)


def kernel(x, c, positions, ada_w, ada_b, ln_g, ln_b, ev_w_in, ev_w_out, ev_sink, ev_sg_ln_g, ev_sg_ln_b,
           ev_sg_w, ev_sg_b, od_w_in, od_conv_w, od_conv_b, od_w_a, od_b_a, od_w_x, od_b_x, od_lam, od_w_out):
    batch, seq, d_model = x.shape
    seg_len = seq // N_SEG
    assert d_model == D_MODEL and seq % ROW_TILE == 0 and seq % MIX_TILE == 0 and seq == N_SEG * seg_len
    assert N_SEG % SEGS_PER_TILE == 0 and seq % SCAN_CHUNK == 0 and SCAN_CHUNK % N_SEG == 0
    assert ada_w.shape[0] == DEPTH == 2

    mod = _modulation(c, ada_w, ada_b)

    q, k, v, su, sv, g = _even_in_proj(x, mod, positions, ev_w_in[0])
    x1 = _even_mixer(q, k, v, su, sv, g, x, mod, ev_sink[0], ev_sg_ln_g[0], ev_sg_ln_b[0],
                     ev_sg_w[0], ev_sg_b[0], ev_w_out[0], ln_g, ln_b)

    xr = _odd_in_proj(x1, mod, od_w_in[0])
    y = _rglru(xr.reshape(batch, RNN_HEADS, seq, RNN_HEAD_DIM), od_conv_w[0], od_conv_b[0], od_w_a[0],
               od_b_a[0], od_w_x[0], od_b_x[0], od_lam[0])
    return _odd_out_proj(y.reshape(batch, RNN_HEADS, seg_len, N_SEG, RNN_HEAD_DIM), x1, mod, od_w_in[0],
                         od_w_out[0], ln_g, ln_b)
```

```python
import math

import numpy as np
import jax
import jax.numpy as jnp
from jax import lax
from jax.experimental import pallas as pl
from jax.experimental.pallas import tpu as pltpu

D_MODEL = 1024
DEPTH = 2
HEAD_DIM = 64
N_Q_HEADS = 8
N_KV_HEADS = 2
Q_PER_KV = N_Q_HEADS // N_KV_HEADS
ATTN_WIDTH = N_Q_HEADS * HEAD_DIM
KV_WIDTH = N_KV_HEADS * HEAD_DIM
ATTN_BLOCK = 128
ROPE_DIM = HEAD_DIM // 4
ROPE_HALF = ROPE_DIM // 2
ROPE_THETA = 500000.0
N_SG_GROUPS = 8
SG_GROUP_DIM = 64
SG_WIDTH = N_SG_GROUPS * SG_GROUP_DIM
SG_CHUNK = 128
EVEN_GATE_WIDTH = ATTN_WIDTH + SG_WIDTH
EVEN_IN_WIDTH = ATTN_WIDTH + 2 * KV_WIDTH + 2 * SG_WIDTH + EVEN_GATE_WIDTH
RNN_WIDTH = D_MODEL
RNN_HEADS = 8
RNN_HEAD_DIM = RNN_WIDTH // RNN_HEADS
CONV_WIDTH = 4
CONV_LEFT = 2
RG_LRU_C = 8.0
DEEPNORM_ALPHA = (2 * DEPTH) ** 0.25
LN_EPS = 1e-5
NEG_INF = -1e30
LOG2_E = math.log2(math.e)
F32_TINY = float(np.finfo(np.float32).tiny)

LANES = 128
SUBLANES = 8
VMEM_LIMIT = 56 * 1024 * 1024

ROW_TILE = 1024
MIX_TILE = 512
N_SEG = 32
SEGS_PER_TILE = SUBLANES
SCAN_CHUNK = 1024
WEIGHT_CAST_ROWS = 128
BIAS_ROWS = 2

BF16 = jnp.bfloat16
F32 = jnp.float32


def _sigmoid(x):
    return 0.5 * jnp.tanh(0.5 * x) + 0.5


def _layer_norm_rows(z, g, b, eps=LN_EPS):
    mu = jnp.mean(z, axis=-1, keepdims=True)
    d = z - mu
    var = jnp.mean(d * d, axis=-1, keepdims=True)
    return d * lax.rsqrt(var + eps) * g + b


def _deepnorm_rows(x, gate, y, g, b):
    return _layer_norm_rows(x + (gate * (1.0 / DEEPNORM_ALPHA)) * y, g, b, LN_EPS / DEEPNORM_ALPHA ** 2)


def _params(*semantics):
    return pltpu.CompilerParams(dimension_semantics=semantics, vmem_limit_bytes=VMEM_LIMIT)


def _cast_weight_once(w_ref, w_bf16):
    @pl.when((pl.program_id(0) == 0) & (pl.program_id(1) == 0))
    def _():
        for r in range(0, w_ref.shape[0], WEIGHT_CAST_ROWS):
            w_bf16[r:r + WEIGHT_CAST_ROWS, :] = w_ref[r:r + WEIGHT_CAST_ROWS, :].astype(BF16)


def _resident(shape, col_block=0):
    return pl.BlockSpec(shape, lambda b, i: (0,) * (len(shape) - 1) + (col_block,), pipeline_mode=pl.Buffered(1))


def _mod_kernel(c_ref, w_ref, b_ref, o_ref):
    c = c_ref[...]
    cond = c * _sigmoid(c)
    pad_rows = o_ref.shape[0] - cond.shape[0]
    if pad_rows:
        cond = jnp.concatenate([cond, jnp.zeros((pad_rows, cond.shape[1]), F32)], axis=0)
    bias = b_ref[pl.ds(pl.program_id(0), 1), :]
    o_ref[...] = jnp.dot(cond.astype(BF16), w_ref[...].astype(BF16), preferred_element_type=F32) + bias


def _modulation(c, ada_w, ada_b):
    batch = c.shape[0]
    rows = -(-batch // SUBLANES) * SUBLANES
    n_col = 3 * D_MODEL // D_MODEL
    return pl.pallas_call(
        _mod_kernel,
        grid=(DEPTH, n_col),
        in_specs=[
            pl.BlockSpec((batch, D_MODEL), lambda l, j: (0, 0)),
            pl.BlockSpec((None, D_MODEL, D_MODEL), lambda l, j: (l, 0, j)),
            pl.BlockSpec((DEPTH, D_MODEL), lambda l, j: (0, j)),
        ],
        out_specs=pl.BlockSpec((None, rows, D_MODEL), lambda l, j: (l, 0, j)),
        out_shape=jax.ShapeDtypeStruct((DEPTH, rows, 3 * D_MODEL), F32),
        compiler_params=_params("parallel", "parallel"),
        name="adaln_mod",
    )(c, ada_w, ada_b)


MOD_SHIFT, MOD_SCALE, MOD_GATE = range(3)
EVEN_LAYER, ODD_LAYER = 0, 1


def _mod_spec(mod, layer, part):
    return pl.BlockSpec((None, mod.shape[1], D_MODEL), lambda b, i: (layer, 0, part))


def _batch_row(ref):
    return ref[pl.ds(pl.program_id(0), 1), :]


def _even_in_kernel(x_ref, shift_ref, scale_ref, pos_ref, freq_ref, sign_ref, w_f32_ref,
                    q_ref, k_ref, v_ref, su_ref, sv_ref, g_ref, w_ref):
    _cast_weight_once(w_f32_ref, w_ref)
    lane = lax.broadcasted_iota(jnp.int32, (1, LANES), 1)
    first_half = (lane % HEAD_DIM) < ROPE_HALF
    left_head = lane < HEAD_DIM
    diagonal = lax.broadcasted_iota(jnp.int32, (LANES, LANES), 0) == lax.broadcasted_iota(jnp.int32, (LANES, LANES), 1)
    batch_ids = lax.broadcasted_iota(jnp.int32, (pos_ref.shape[0], LANES), 0)
    q_scale = HEAD_DIM ** -0.5 * LOG2_E

    half_rows = x_ref.shape[0] // 2
    for r in range(2):
        rows = slice(r * half_rows, (r + 1) * half_rows)
        h = (x_ref[rows, :] * (1.0 + _batch_row(scale_ref)) + _batch_row(shift_ref)).astype(BF16)
        p = jnp.dot(h, w_ref[...], preferred_element_type=F32)

        pos_cols = []
        for j in range(r * half_rows // LANES, (r + 1) * half_rows // LANES):
            pos_all = pos_ref[:, j * LANES:(j + 1) * LANES].astype(F32)
            pos_row = jnp.sum(jnp.where(batch_ids == pl.program_id(0), pos_all, 0.0), axis=0, keepdims=True)
            pos_cols.append(jnp.sum(jnp.where(diagonal, pos_row, 0.0), axis=1, keepdims=True))
        ang = jnp.concatenate(pos_cols, axis=0) * freq_ref[...]
        cs = jnp.cos(ang)
        sn = jnp.sin(ang) * sign_ref[...]

        def rotary(t):
            partner = jnp.where(first_half, pltpu.roll(t, LANES - ROPE_HALF, 1), pltpu.roll(t, ROPE_HALF, 1))
            return t * cs + partner * sn

        for j in range(ATTN_WIDTH // LANES):
            q_ref[rows, j * LANES:(j + 1) * LANES] = (
                rotary(p[:, j * LANES:(j + 1) * LANES]) * q_scale).astype(BF16)
        o = ATTN_WIDTH
        k = rotary(p[:, o:o + KV_WIDTH])
        k_sw = pltpu.roll(k, HEAD_DIM, 1)
        k_ref[rows, 0:LANES] = jnp.where(left_head, k, k_sw).astype(BF16)
        k_ref[rows, LANES:2 * LANES] = jnp.where(left_head, k_sw, k).astype(BF16)
        o += KV_WIDTH
        v = p[:, o:o + KV_WIDTH]
        v_sw = pltpu.roll(v, HEAD_DIM, 1)
        v_ref[rows, 0 * LANES:1 * LANES] = jnp.where(left_head, v, 0.0).astype(BF16)
        v_ref[rows, 1 * LANES:2 * LANES] = jnp.where(left_head, 0.0, v_sw).astype(BF16)
        v_ref[rows, 2 * LANES:3 * LANES] = jnp.where(left_head, v_sw, 0.0).astype(BF16)
        v_ref[rows, 3 * LANES:4 * LANES] = jnp.where(left_head, 0.0, v).astype(BF16)
        o += KV_WIDTH
        su_ref[rows, :] = p[:, o:o + SG_WIDTH].astype(BF16)
        o += SG_WIDTH
        sv_ref[rows, :] = p[:, o:o + SG_WIDTH].astype(BF16)
        o += SG_WIDTH
        g_ref[rows, :] = p[:, o:o + EVEN_GATE_WIDTH].astype(BF16)


def _even_in_proj(x, mod, positions, w_in):
    batch, seq, _ = x.shape
    d = np.arange(LANES) % HEAD_DIM
    inv_freq = np.power(np.float64(ROPE_THETA), -np.arange(ROPE_HALF) / ROPE_HALF)
    freq_lane = jnp.asarray(np.where(d < ROPE_DIM, inv_freq[d % ROPE_HALF], 0.0), dtype=F32).reshape(1, LANES)
    sign_lane = jnp.asarray(np.where(d < ROPE_HALF, -1.0, np.where(d < ROPE_DIM, 1.0, 0.0)),
                            dtype=F32).reshape(1, LANES)
    row = lambda width: pl.BlockSpec((None, ROW_TILE, width), lambda b, i: (b, i, 0))
    const = lambda shape: pl.BlockSpec(shape, lambda b, i: (0,) * len(shape))
    widths = (ATTN_WIDTH, 2 * KV_WIDTH, 4 * KV_WIDTH, SG_WIDTH, SG_WIDTH, EVEN_GATE_WIDTH)
    return pl.pallas_call(
        _even_in_kernel,
        grid=(batch, seq // ROW_TILE),
        in_specs=[row(D_MODEL), _mod_spec(mod, EVEN_LAYER, MOD_SHIFT), _mod_spec(mod, EVEN_LAYER, MOD_SCALE),
                  pl.BlockSpec((batch, ROW_TILE), lambda b, i: (0, i)),
                  const((1, LANES)), const((1, LANES)), _resident((D_MODEL, EVEN_IN_WIDTH))],
        out_specs=[row(w) for w in widths],
        out_shape=[jax.ShapeDtypeStruct((batch, seq, w), BF16) for w in widths],
        scratch_shapes=[pltpu.VMEM((D_MODEL, EVEN_IN_WIDTH), BF16)],
        compiler_params=_params("arbitrary", "arbitrary"),
        name="even_in_proj",
    )(x, mod, mod, positions, freq_lane, sign_lane, w_in)


def _even_mix_kernel(sink_ref, q_ref, k_ref, v_ref, su_ref, sv_ref, g_ref, x_ref, gate_ref,
                     avg_ref, avg2_ref, sgg_ref, sgb_ref, sgw_ref, sgbias_ref, wout_f32_ref, lng_ref, lnb_ref,
                     o_ref, y_scr, wout_ref):
    _cast_weight_once(wout_f32_ref, wout_ref)
    seq = k_ref.shape[0]
    n_blocks = seq // ATTN_BLOCK
    sub_blocks = MIX_TILE // ATTN_BLOCK
    tile = pl.program_id(1)
    lane = lax.broadcasted_iota(jnp.int32, (1, LANES), 1)
    left = lane < HEAD_DIM

    qi = lax.broadcasted_iota(jnp.int32, (ATTN_BLOCK, ATTN_BLOCK), 0)
    kj = lax.broadcasted_iota(jnp.int32, (ATTN_BLOCK, ATTN_BLOCK), 1)
    ones_left = jnp.broadcast_to(jnp.where(left, 1.0, 0.0).astype(BF16), (3 * ATTN_BLOCK, LANES))
    ones_right = jnp.broadcast_to(jnp.where(left, 0.0, 1.0).astype(BF16), (3 * ATTN_BLOCK, LANES))
    avg = avg_ref[...]
    sink_cols = [jnp.concatenate([jnp.full((ATTN_BLOCK, LANES), sink_ref[h] * LOG2_E, F32)
                                  for h in (4 * hk, 4 * hk + 2, 4 * hk + 1, 4 * hk + 3)], axis=0)
                 for hk in range(N_KV_HEADS)]

    def sub_block(j, carry):
        blk = tile * sub_blocks + j
        prev = jnp.maximum(blk - 1, 0)
        nxt = jnp.minimum(blk + 1, n_blocks - 1)
        rows = pl.ds(pl.multiple_of(j * ATTN_BLOCK, ATTN_BLOCK), ATTN_BLOCK)
        bias_prev = jnp.where((kj >= qi) & (blk > 0), 0.0, NEG_INF)
        bias_next = jnp.where((kj <= qi) & (blk < n_blocks - 1), 0.0, NEG_INF)
        bias_prev = jnp.concatenate([bias_prev] * Q_PER_KV, axis=0)
        bias_next = jnp.concatenate([bias_next] * Q_PER_KV, axis=0)

        def band(ref):
            return jnp.concatenate(
                [ref[pl.ds(pl.multiple_of(b * ATTN_BLOCK, ATTN_BLOCK), ATTN_BLOCK), :] for b in (prev, blk, nxt)],
                axis=0)

        k_band = band(k_ref)
        v_band = band(v_ref)
        n_sg = SG_WIDTH // LANES
        sg_cols = [slice(c * LANES, (c + 1) * LANES) for c in range(n_sg)]

        def silu_gate(cols):
            g_c = g_ref[rows, cols].astype(F32)
            return g_c * _sigmoid(g_c)

        scores = []
        for hk in range(N_KV_HEADS):
            q_cols = [q_ref[rows, (2 * hk + c) * LANES:(2 * hk + c + 1) * LANES] for c in range(2)]
            zero = jnp.zeros_like(q_cols[0])
            lhs = jnp.concatenate([jnp.where(left, qc, zero) for qc in q_cols]
                                  + [jnp.where(left, zero, qc) for qc in q_cols], axis=0)
            scores.append(lax.dot_general(lhs, k_band[:, hk * LANES:(hk + 1) * LANES], (((1,), (1,)), ((), ())),
                                          preferred_element_type=F32))
        sv_cols = [sv_ref[rows, cols] for cols in sg_cols]
        means = [jnp.dot(v_c, avg, preferred_element_type=F32) for v_c in sv_cols]

        probs, row_max = [], []
        for hk in range(N_KV_HEADS):
            s = scores[hk]
            s0 = s[:, 0:ATTN_BLOCK] + bias_prev
            s1 = s[:, ATTN_BLOCK:2 * ATTN_BLOCK]
            s2 = s[:, 2 * ATTN_BLOCK:3 * ATTN_BLOCK] + bias_next
            m = jnp.max(jnp.maximum(jnp.maximum(s0, s1), s2), axis=-1, keepdims=True)
            m = jnp.maximum(jnp.broadcast_to(m, sink_cols[hk].shape), sink_cols[hk])
            probs.append(jnp.concatenate([jnp.exp2(t - m).astype(BF16) for t in (s0, s1, s2)], axis=1))
            row_max.append(m)
        devs, sq_parts = [], []
        for v_c, mean in zip(sv_cols, means):
            dev = v_c.astype(F32) - mean
            sq = dev * dev
            sq_hi = sq.astype(BF16)
            sq_lo = (sq - sq_hi.astype(F32)).astype(BF16)
            devs.append(dev)
            sq_parts.append(jnp.concatenate([sq_hi, sq_lo], axis=1))

        pv = []
        for hk in range(N_KV_HEADS):
            p = probs[hk]
            lhs = jnp.concatenate(
                [jnp.concatenate([p[c * ATTN_BLOCK:(c + 1) * ATTN_BLOCK], p[(2 + c) * ATTN_BLOCK:(3 + c) * ATTN_BLOCK]],
                                 axis=1) for c in range(2)], axis=0)
            rhs = jnp.concatenate(
                [jnp.concatenate([v_band[:, (2 * hk) * LANES:(2 * hk + 1) * LANES], ones_left], axis=1),
                 jnp.concatenate([v_band[:, (2 * hk + 1) * LANES:(2 * hk + 2) * LANES], ones_right], axis=1)],
                axis=0)
            pv.append(jnp.dot(lhs, rhs, preferred_element_type=F32))
        variances = [jnp.dot(sq, avg2_ref[...], preferred_element_type=F32) for sq in sq_parts]

        for hk in range(N_KV_HEADS):
            for c in range(2):
                even, odd = slice(c * ATTN_BLOCK, (c + 1) * ATTN_BLOCK), slice((2 + c) * ATTN_BLOCK, (3 + c) * ATTN_BLOCK)
                sink_term = jnp.exp2(jnp.where(left, sink_cols[hk][even] - row_max[hk][even],
                                               sink_cols[hk][odd] - row_max[hk][odd]))
                o = pv[hk][c * ATTN_BLOCK:(c + 1) * ATTN_BLOCK]
                pair = o[:, 0:LANES] * (1.0 / (o[:, LANES:2 * LANES] + sink_term))
                cols = slice((2 * hk + c) * LANES, (2 * hk + c + 1) * LANES)
                y_scr[rows, cols] = (pair * silu_gate(cols)).astype(BF16)

        for c, cols in enumerate(sg_cols):
            vn = devs[c] * lax.rsqrt(variances[c] + LN_EPS) * sgg_ref[:, cols] + sgb_ref[:, cols]
            stacked = jnp.concatenate([jnp.where(left, vn, 0.0), jnp.where(left, 0.0, vn)],
                                      axis=0).astype(BF16)
            mixed = jnp.dot(sgw_ref[c], stacked, preferred_element_type=F32) + sgbias_ref[:, cols]
            y_sg = su_ref[rows, cols].astype(F32) * mixed
            gc = slice(ATTN_WIDTH + c * LANES, ATTN_WIDTH + (c + 1) * LANES)
            y_scr[rows, gc] = (y_sg * silu_gate(gc)).astype(BF16)
        return carry

    lax.fori_loop(0, sub_blocks, sub_block, 0, unroll=True)

    out = jnp.dot(y_scr[...], wout_ref[...], preferred_element_type=F32)
    o_ref[...] = _deepnorm_rows(x_ref[...], _batch_row(gate_ref), out,
                                lng_ref[EVEN_LAYER:EVEN_LAYER + 1, :], lnb_ref[EVEN_LAYER:EVEN_LAYER + 1, :])


def _even_mixer(q, k, v, su, sv, g, x, mod, sink, sg_ln_g, sg_ln_b, sg_w, sg_b, w_out, ln_g, ln_b):
    batch, seq, _ = x.shape
    lane = np.arange(LANES)
    avg = jnp.asarray((lane[:, None] // SG_GROUP_DIM == lane[None, :] // SG_GROUP_DIM) / SG_GROUP_DIM, dtype=BF16)
    w_pairs = sg_w.reshape(N_SG_GROUPS // 2, 2, SG_CHUNK, SG_CHUNK).transpose(0, 2, 1, 3)
    w_pairs = w_pairs.reshape(N_SG_GROUPS // 2, SG_CHUNK, 2 * SG_CHUNK).astype(BF16)
    bias_full = jnp.repeat(sg_b.T, SG_GROUP_DIM, axis=1)
    row = lambda width: pl.BlockSpec((None, MIX_TILE, width), lambda b, i: (b, i, 0))
    full_seq = lambda arr: pl.BlockSpec((None, seq, arr.shape[-1]), lambda b, i: (b, 0, 0))
    const = lambda shape: pl.BlockSpec(shape, lambda b, i: (0,) * len(shape))
    return pl.pallas_call(
        _even_mix_kernel,
        grid=(batch, seq // MIX_TILE),
        in_specs=[
            pl.BlockSpec(memory_space=pltpu.SMEM),
            row(ATTN_WIDTH), full_seq(k), full_seq(v), row(SG_WIDTH), row(SG_WIDTH), row(EVEN_GATE_WIDTH),
            row(D_MODEL), _mod_spec(mod, EVEN_LAYER, MOD_GATE),
            const((LANES, LANES)), const((2 * LANES, LANES)), const((1, SG_WIDTH)), const((1, SG_WIDTH)),
            const((N_SG_GROUPS // 2, SG_CHUNK, 2 * SG_CHUNK)), const((SG_CHUNK, SG_WIDTH)),
            _resident((EVEN_GATE_WIDTH, D_MODEL)), const(ln_g.shape), const(ln_b.shape),
        ],
        out_specs=row(D_MODEL),
        out_shape=jax.ShapeDtypeStruct((batch, seq, D_MODEL), F32),
        scratch_shapes=[pltpu.VMEM((MIX_TILE, EVEN_GATE_WIDTH), BF16),
                        pltpu.VMEM((EVEN_GATE_WIDTH, D_MODEL), BF16)],
        compiler_params=_params("arbitrary", "arbitrary"),
        name="even_mixer",
    )(sink, q, k, v, su, sv, g, x, mod, avg, jnp.concatenate([avg, avg], axis=0), sg_ln_g.reshape(1, -1), sg_ln_b.reshape(1, -1),
      w_pairs, bias_full, w_out, ln_g, ln_b)


def _odd_in_kernel(x_ref, shift_ref, scale_ref, w_f32_ref, xr_ref, w_ref):
    _cast_weight_once(w_f32_ref, w_ref)
    seg_len = xr_ref.shape[1]
    head_rows = seg_len * SEGS_PER_TILE
    xr_rows = xr_ref.reshape(RNN_HEADS * head_rows, LANES)
    halves = 2
    for r in range(halves):
        first = r * (SEGS_PER_TILE // halves)
        rows = slice(first * seg_len, (first + SEGS_PER_TILE // halves) * seg_len)
        h = (x_ref[rows, :] * (1.0 + _batch_row(scale_ref)) + _batch_row(shift_ref)).astype(BF16)
        p = jnp.dot(h, w_ref[...], preferred_element_type=F32)
        for s in range(SEGS_PER_TILE // halves):
            for hd in range(RNN_HEADS):
                xr_rows[pl.ds(hd * head_rows + first + s, seg_len, stride=SEGS_PER_TILE), :] = (
                    p[s * seg_len:(s + 1) * seg_len, hd * LANES:(hd + 1) * LANES])


def _odd_in_proj(x, mod, w_in):
    batch, seq, _ = x.shape
    seg_len = seq // N_SEG
    tile_rows = SEGS_PER_TILE * seg_len
    out_spec = pl.BlockSpec((None, RNN_HEADS, seg_len, SEGS_PER_TILE, RNN_HEAD_DIM), lambda b, i: (b, 0, 0, i, 0))
    out_shape = jax.ShapeDtypeStruct((batch, RNN_HEADS, seg_len, N_SEG, RNN_HEAD_DIM), F32)
    return pl.pallas_call(
        _odd_in_kernel,
        grid=(batch, seq // tile_rows),
        in_specs=[pl.BlockSpec((None, tile_rows, D_MODEL), lambda b, i: (b, i, 0)),
                  _mod_spec(mod, ODD_LAYER, MOD_SHIFT), _mod_spec(mod, ODD_LAYER, MOD_SCALE),
                  _resident((D_MODEL, RNN_WIDTH), col_block=0)],
        out_specs=out_spec,
        out_shape=out_shape,
        scratch_shapes=[pltpu.VMEM((D_MODEL, RNN_WIDTH), BF16)],
        compiler_params=_params("arbitrary", "arbitrary"),
        name="odd_in_proj",
    )(x, mod, mod, w_in)


def _rglru_kernel(x_ref, convw_ref, convb_ref, waug_ref, lam_ref, o_ref,
                  halo, a_f, b_f, a_b, b_b, edge):
    n_rows = x_ref.shape[0]
    seg_len = n_rows // N_SEG
    n_chunks = n_rows // SCAN_CHUNK
    left_rows = CONV_LEFT * N_SEG
    right_rows = (CONV_WIDTH - 1 - CONV_LEFT) * N_SEG
    seg = lax.broadcasted_iota(jnp.int32, (N_SEG, LANES), 0)

    def from_prev_segment(blk):
        return jnp.where(seg == 0, 0.0, pltpu.roll(blk, 1, 0))

    def from_next_segment(blk):
        return jnp.where(seg == N_SEG - 1, 0.0, pltpu.roll(blk, N_SEG - 1, 0))

    for t in range(CONV_LEFT):
        halo[0, pl.ds(t * N_SEG, N_SEG), :] = from_prev_segment(
            x_ref[pl.ds((seg_len - CONV_LEFT + t) * N_SEG, N_SEG), :])
    halo[0, pl.ds(left_rows, SCAN_CHUNK + right_rows), :] = x_ref[pl.ds(0, SCAN_CHUNK + right_rows), :]
    halo[1, pl.ds(0, left_rows + SCAN_CHUNK), :] = x_ref[pl.ds(n_rows - SCAN_CHUNK - left_rows, left_rows + SCAN_CHUNK), :]
    for t in range(CONV_WIDTH - 1 - CONV_LEFT):
        halo[1, pl.ds(left_rows + SCAN_CHUNK + t * N_SEG, N_SEG), :] = from_next_segment(
            x_ref[pl.ds(t * N_SEG, N_SEG), :])

    lam = lam_ref[...]
    softplus_neg = jnp.maximum(-lam, 0.0) + jnp.log1p(jnp.exp(-jnp.abs(lam)))
    half_rate = (0.5 * RG_LRU_C) * softplus_neg
    bias_lanes = jnp.broadcast_to(
        jnp.where(lax.broadcasted_iota(jnp.int32, (1, LANES), 1) < BIAS_ROWS, 1.0, 0.0).astype(BF16),
        (SCAN_CHUNK, LANES))

    def gates(base, tap):
        half_x = convb_ref[...] + sum(convw_ref[k:k + 1, :] * tap(k) for k in range(CONV_WIDTH))
        th = jnp.tanh(jnp.dot(jnp.concatenate([half_x.astype(BF16), bias_lanes], axis=1), waug_ref[...],
                              preferred_element_type=F32))
        for d, (a_ref, b_ref) in enumerate(((a_f, b_f), (a_b, b_b))):
            o = 2 * d * LANES
            neg_log_a = (th[:, o:o + LANES] + 1.0) * half_rate[d:d + 1, :]
            a = jnp.exp2(neg_log_a * (-LOG2_E))
            one_minus_a2 = jnp.tanh(neg_log_a) * (a * a + 1.0)
            root = one_minus_a2 * lax.rsqrt(jnp.maximum(one_minus_a2, F32_TINY))
            a_ref[pl.ds(base, SCAN_CHUNK), :] = a
            b_ref[pl.ds(base, SCAN_CHUNK), :] = root * ((th[:, o + LANES:o + 2 * LANES] + 1.0) * half_x)

    def inner_chunk(c, carry):
        base = pl.multiple_of(c * SCAN_CHUNK, SCAN_CHUNK)
        gates(base, lambda k: x_ref[pl.ds(base + (k - CONV_LEFT) * N_SEG, SCAN_CHUNK), :])
        return carry

    gates(0, lambda k: halo[0, pl.ds(k * N_SEG, SCAN_CHUNK), :])
    gates(n_rows - SCAN_CHUNK, lambda k: halo[1, pl.ds(k * N_SEG, SCAN_CHUNK), :])
    lax.fori_loop(1, n_chunks - 1, inner_chunk, 0, unroll=2)

    def scan_step(t, carry):
        h_f, p_f, h_b, p_b = carry
        rf = pl.ds(pl.multiple_of(t * N_SEG, N_SEG), N_SEG)
        rb = pl.ds(pl.multiple_of((seg_len - 1 - t) * N_SEG, N_SEG), N_SEG)
        a = a_f[rf, :]
        h_f = a * h_f + b_f[rf, :]
        p_f = a * p_f
        b_f[rf, :] = h_f
        a_f[rf, :] = p_f
        a = a_b[rb, :]
        h_b = a * h_b + b_b[rb, :]
        p_b = a * p_b
        b_b[rb, :] = h_b
        a_b[rb, :] = p_b
        return h_f, p_f, h_b, p_b

    zeros = jnp.zeros((N_SEG, LANES), F32)
    ones = jnp.ones((N_SEG, LANES), F32)
    h_f, p_f, h_b, p_b = lax.fori_loop(0, seg_len, scan_step, (zeros, ones, zeros, ones), unroll=True)

    edge[0] = h_f
    edge[1] = p_f
    edge[2] = h_b
    edge[3] = p_b
    enter_f = enter_b = jnp.zeros((1, LANES), F32)
    edge[4, 0:1, :] = enter_f
    edge[5, N_SEG - 1:N_SEG, :] = enter_b
    for s in range(1, N_SEG):
        enter_f = edge[0, s - 1:s, :] + edge[1, s - 1:s, :] * enter_f
        edge[4, s:s + 1, :] = enter_f
        r = N_SEG - 1 - s
        enter_b = edge[2, r + 1:r + 2, :] + edge[3, r + 1:r + 2, :] * enter_b
        edge[5, r:r + 1, :] = enter_b
    blocks = SCAN_CHUNK // N_SEG
    in_f = jnp.concatenate([edge[4]] * blocks, axis=0)
    in_b = jnp.concatenate([edge[5]] * blocks, axis=0)

    def out_chunk(c, carry):
        rows = pl.ds(pl.multiple_of(c * SCAN_CHUNK, SCAN_CHUNK), SCAN_CHUNK)
        o_ref[rows, :] = (b_f[rows, :] + a_f[rows, :] * in_f) + (b_b[rows, :] + a_b[rows, :] * in_b)
        return carry

    lax.fori_loop(0, n_rows // SCAN_CHUNK, out_chunk, 0, unroll=2)


def _rglru(xr, conv_w, conv_b, w_a, b_a, w_x, b_x, lam):
    batch, _, n_rows, _ = xr.shape
    w_cat = jnp.concatenate([w_a[0], w_x[0], w_a[1], w_x[1]], axis=-1).astype(BF16)
    per_head = lambda v: v.reshape(RNN_HEADS, 1, RNN_HEAD_DIM)
    half_b = 0.5 * jnp.concatenate([per_head(b_a[0]), per_head(b_x[0]), per_head(b_a[1]), per_head(b_x[1])],
                                   axis=-1)
    b_hi = half_b.astype(BF16)
    b_lo = (half_b - b_hi.astype(F32)).astype(BF16)
    w_aug = jnp.concatenate([w_cat, b_hi, b_lo,
                             jnp.zeros((RNN_HEADS, RNN_HEAD_DIM - BIAS_ROWS, 4 * RNN_HEAD_DIM), BF16)], axis=1)
    col = lambda rows: pl.BlockSpec((rows, RNN_HEAD_DIM), lambda b, h: (0, h))
    seq_col = pl.BlockSpec((None, None, n_rows, RNN_HEAD_DIM), lambda b, h: (b, h, 0, 0))
    halo_rows = SCAN_CHUNK + (CONV_WIDTH - 1) * N_SEG
    return pl.pallas_call(
        _rglru_kernel,
        grid=(batch, RNN_HEADS),
        in_specs=[seq_col, col(CONV_WIDTH), col(1),
                  pl.BlockSpec((None, 2 * RNN_HEAD_DIM, 4 * RNN_HEAD_DIM), lambda b, h: (h, 0, 0)),
                  col(2)],
        out_specs=seq_col,
        out_shape=jax.ShapeDtypeStruct((batch, RNN_HEADS, n_rows, RNN_HEAD_DIM), F32),
        scratch_shapes=[pltpu.VMEM((2, halo_rows, LANES), F32)]
                       + [pltpu.VMEM((n_rows, LANES), F32)] * 4
                       + [pltpu.VMEM((6, N_SEG, LANES), F32)],
        compiler_params=_params("parallel", "parallel"),
        name="rglru",
    )(xr, 0.5 * conv_w, 0.5 * conv_b.reshape(1, -1), w_aug, lam)


def _odd_out_kernel(y_ref, x_ref, shift_ref, scale_ref, gate_ref, wg_f32_ref, w_f32_ref, lng_ref, lnb_ref,
                    o_ref, wg_ref, w_ref):
    _cast_weight_once(wg_f32_ref, wg_ref)
    _cast_weight_once(w_f32_ref, w_ref)
    seg_len = y_ref.shape[1]
    head_rows = seg_len * SEGS_PER_TILE
    y_rows = y_ref.reshape(RNN_HEADS * head_rows, LANES)
    x = x_ref[...]
    h = (x * (1.0 + _batch_row(scale_ref)) + _batch_row(shift_ref)).astype(BF16)
    g_all = jnp.dot(h, wg_ref[...], preferred_element_type=F32)
    gated = []
    for s in range(SEGS_PER_TILE):
        y = jnp.concatenate([y_rows[pl.ds(hd * head_rows + s, seg_len, stride=SEGS_PER_TILE), :]
                             for hd in range(RNN_HEADS)], axis=1)
        g = g_all[s * seg_len:(s + 1) * seg_len, :]
        gated.append((y * (g * _sigmoid(g))).astype(BF16))
    out = jnp.dot(jnp.concatenate(gated, axis=0), w_ref[...], preferred_element_type=F32)
    o_ref[...] = _deepnorm_rows(x, _batch_row(gate_ref), out,
                                lng_ref[ODD_LAYER:ODD_LAYER + 1, :], lnb_ref[ODD_LAYER:ODD_LAYER + 1, :])


def _odd_out_proj(y, x, mod, w_in, w_out, ln_g, ln_b):
    batch, seq, _ = x.shape
    seg_len = seq // N_SEG
    tile_rows = SEGS_PER_TILE * seg_len
    const = lambda shape: pl.BlockSpec(shape, lambda b, i: (0,) * len(shape))
    return pl.pallas_call(
        _odd_out_kernel,
        grid=(batch, seq // tile_rows),
        in_specs=[pl.BlockSpec((None, RNN_HEADS, seg_len, SEGS_PER_TILE, RNN_HEAD_DIM),
                               lambda b, i: (b, 0, 0, i, 0)),
                  pl.BlockSpec((None, tile_rows, D_MODEL), lambda b, i: (b, i, 0)),
                  _mod_spec(mod, ODD_LAYER, MOD_SHIFT), _mod_spec(mod, ODD_LAYER, MOD_SCALE),
                  _mod_spec(mod, ODD_LAYER, MOD_GATE),
                  _resident((D_MODEL, RNN_WIDTH), col_block=1), _resident((RNN_WIDTH, D_MODEL)),
                  const(ln_g.shape), const(ln_b.shape)],
        out_specs=pl.BlockSpec((None, tile_rows, D_MODEL), lambda b, i: (b, i, 0)),
        out_shape=jax.ShapeDtypeStruct((batch, seq, D_MODEL), F32),
        scratch_shapes=[pltpu.VMEM((D_MODEL, RNN_WIDTH), BF16), pltpu.VMEM((RNN_WIDTH, D_MODEL), BF16)],
        compiler_params=_params("arbitrary", "arbitrary"),
        name="odd_out_proj",
    )(y, x, mod, mod, mod, w_in, w_out, ln_g, ln_b)


def kernel(x, c, positions, ada_w, ada_b, ln_g, ln_b, ev_w_in, ev_w_out, ev_sink, ev_sg_ln_g, ev_sg_ln_b,
           ev_sg_w, ev_sg_b, od_w_in, od_conv_w, od_conv_b, od_w_a, od_b_a, od_w_x, od_b_x, od_lam, od_w_out):
    batch, seq, d_model = x.shape
    seg_len = seq // N_SEG
    assert d_model == D_MODEL and seq % ROW_TILE == 0 and seq % MIX_TILE == 0 and seq == N_SEG * seg_len
    assert N_SEG % SEGS_PER_TILE == 0 and seq % SCAN_CHUNK == 0 and SCAN_CHUNK % N_SEG == 0
    assert ada_w.shape[0] == DEPTH == 2

    mod = _modulation(c, ada_w, ada_b)

    q, k, v, su, sv, g = _even_in_proj(x, mod, positions, ev_w_in[0])
    x1 = _even_mixer(q, k, v, su, sv, g, x, mod, ev_sink[0], ev_sg_ln_g[0], ev_sg_ln_b[0],
                     ev_sg_w[0], ev_sg_b[0], ev_w_out[0], ln_g, ln_b)

    xr = _odd_in_proj(x1, mod, od_w_in[0])
    y = _rglru(xr.reshape(batch, RNN_HEADS, seq, RNN_HEAD_DIM), od_conv_w[0], od_conv_b[0], od_w_a[0],
               od_b_a[0], od_w_x[0], od_b_x[0], od_lam[0])
    return _odd_out_proj(y.reshape(batch, RNN_HEADS, seg_len, N_SEG, RNN_HEAD_DIM), x1, mod, od_w_in[0],
                         od_w_out[0], ln_g, ln_b)
```

```python
import math

import numpy as np
import jax
import jax.numpy as jnp
from jax import lax
from jax.experimental import pallas as pl
from jax.experimental.pallas import tpu as pltpu

D_MODEL = 1024
DEPTH = 2
HEAD_DIM = 64
N_Q_HEADS = 8
N_KV_HEADS = 2
Q_PER_KV = N_Q_HEADS // N_KV_HEADS
ATTN_WIDTH = N_Q_HEADS * HEAD_DIM
KV_WIDTH = N_KV_HEADS * HEAD_DIM
ATTN_BLOCK = 128
ROPE_DIM = HEAD_DIM // 4
ROPE_HALF = ROPE_DIM // 2
ROPE_THETA = 500000.0
N_SG_GROUPS = 8
SG_GROUP_DIM = 64
SG_WIDTH = N_SG_GROUPS * SG_GROUP_DIM
SG_CHUNK = 128
EVEN_GATE_WIDTH = ATTN_WIDTH + SG_WIDTH
EVEN_IN_WIDTH = ATTN_WIDTH + 2 * KV_WIDTH + 2 * SG_WIDTH + EVEN_GATE_WIDTH
RNN_WIDTH = D_MODEL
RNN_HEADS = 8
RNN_HEAD_DIM = RNN_WIDTH // RNN_HEADS
CONV_WIDTH = 4
CONV_LEFT = 2
RG_LRU_C = 8.0
DEEPNORM_ALPHA = (2 * DEPTH) ** 0.25
LN_EPS = 1e-5
NEG_INF = -1e30
LOG2_E = math.log2(math.e)
F32_TINY = float(np.finfo(np.float32).tiny)

LANES = 128
SUBLANES = 8
VMEM_LIMIT = 56 * 1024 * 1024

ROW_TILE = 1024
MIX_TILE = 512
N_SEG = 32
SEGS_PER_TILE = SUBLANES
SCAN_CHUNK = 1024
WEIGHT_CAST_ROWS = 128
BIAS_ROWS = 2

BF16 = jnp.bfloat16
F32 = jnp.float32


def _silu(x):
    h = 0.5 * x
    return h + h * jnp.tanh(h)


def _layer_norm_rows(z, g, b, eps=LN_EPS):
    mu = jnp.mean(z, axis=-1, keepdims=True)
    d = z - mu
    var = jnp.mean(d * d, axis=-1, keepdims=True)
    return d * lax.rsqrt(var + eps) * g + b


def _deepnorm_rows(x, gate, y, g, b):
    return _layer_norm_rows(x + (gate * (1.0 / DEEPNORM_ALPHA)) * y, g, b, LN_EPS / DEEPNORM_ALPHA ** 2)


def _params(*semantics):
    return pltpu.CompilerParams(dimension_semantics=semantics, vmem_limit_bytes=VMEM_LIMIT)


def _cast_weight_once(w_ref, w_bf16):
    @pl.when((pl.program_id(0) == 0) & (pl.program_id(1) == 0))
    def _():
        for r in range(0, w_ref.shape[0], WEIGHT_CAST_ROWS):
            w_bf16[r:r + WEIGHT_CAST_ROWS, :] = w_ref[r:r + WEIGHT_CAST_ROWS, :].astype(BF16)


def _resident(shape, col_block=0):
    return pl.BlockSpec(shape, lambda b, i: (0,) * (len(shape) - 1) + (col_block,), pipeline_mode=pl.Buffered(1))


def _mod_kernel(c_ref, w_ref, b_ref, o_ref):
    c = c_ref[...]
    cond = _silu(c)
    pad_rows = o_ref.shape[0] - cond.shape[0]
    if pad_rows:
        cond = jnp.concatenate([cond, jnp.zeros((pad_rows, cond.shape[1]), F32)], axis=0)
    bias = b_ref[pl.ds(pl.program_id(0), 1), :]
    o_ref[...] = jnp.dot(cond.astype(BF16), w_ref[...].astype(BF16), preferred_element_type=F32) + bias


def _modulation(c, ada_w, ada_b):
    batch = c.shape[0]
    rows = -(-batch // SUBLANES) * SUBLANES
    n_col = 3 * D_MODEL // D_MODEL
    return pl.pallas_call(
        _mod_kernel,
        grid=(DEPTH, n_col),
        in_specs=[
            pl.BlockSpec((batch, D_MODEL), lambda l, j: (0, 0)),
            pl.BlockSpec((None, D_MODEL, D_MODEL), lambda l, j: (l, 0, j)),
            pl.BlockSpec((DEPTH, D_MODEL), lambda l, j: (0, j)),
        ],
        out_specs=pl.BlockSpec((None, rows, D_MODEL), lambda l, j: (l, 0, j)),
        out_shape=jax.ShapeDtypeStruct((DEPTH, rows, 3 * D_MODEL), F32),
        compiler_params=_params("parallel", "parallel"),
        name="adaln_mod",
    )(c, ada_w, ada_b)


MOD_SHIFT, MOD_SCALE, MOD_GATE = range(3)
EVEN_LAYER, ODD_LAYER = 0, 1


def _mod_spec(mod, layer, part):
    return pl.BlockSpec((None, mod.shape[1], D_MODEL), lambda b, i: (layer, 0, part))


def _batch_row(ref):
    return ref[pl.ds(pl.program_id(0), 1), :]


def _even_in_kernel(x_ref, shift_ref, scale_ref, pos_ref, freq_ref, sign_ref, w_f32_ref,
                    q_ref, k_ref, v_ref, su_ref, sv_ref, g_ref, w_ref):
    _cast_weight_once(w_f32_ref, w_ref)
    lane = lax.broadcasted_iota(jnp.int32, (1, LANES), 1)
    first_half = (lane % HEAD_DIM) < ROPE_HALF
    left_head = lane < HEAD_DIM
    diagonal = lax.broadcasted_iota(jnp.int32, (LANES, LANES), 0) == lax.broadcasted_iota(jnp.int32, (LANES, LANES), 1)
    batch_ids = lax.broadcasted_iota(jnp.int32, (pos_ref.shape[0], LANES), 0)
    q_scale = HEAD_DIM ** -0.5 * LOG2_E

    half_rows = x_ref.shape[0] // 2
    for r in range(2):
        rows = slice(r * half_rows, (r + 1) * half_rows)
        h = (x_ref[rows, :] * (1.0 + _batch_row(scale_ref)) + _batch_row(shift_ref)).astype(BF16)
        p = jnp.dot(h, w_ref[...], preferred_element_type=F32)

        pos_cols = []
        for j in range(r * half_rows // LANES, (r + 1) * half_rows // LANES):
            pos_all = pos_ref[:, j * LANES:(j + 1) * LANES].astype(F32)
            pos_row = jnp.sum(jnp.where(batch_ids == pl.program_id(0), pos_all, 0.0), axis=0, keepdims=True)
            pos_cols.append(jnp.sum(jnp.where(diagonal, pos_row, 0.0), axis=1, keepdims=True))
        ang = jnp.concatenate(pos_cols, axis=0) * freq_ref[...]
        cs = jnp.cos(ang)
        sn = jnp.sin(ang) * sign_ref[...]

        def rotary(t):
            partner = jnp.where(first_half, pltpu.roll(t, LANES - ROPE_HALF, 1), pltpu.roll(t, ROPE_HALF, 1))
            return t * cs + partner * sn

        for j in range(ATTN_WIDTH // LANES):
            q_ref[rows, j * LANES:(j + 1) * LANES] = (
                rotary(p[:, j * LANES:(j + 1) * LANES]) * q_scale).astype(BF16)
        o = ATTN_WIDTH
        k = rotary(p[:, o:o + KV_WIDTH])
        k_sw = pltpu.roll(k, HEAD_DIM, 1)
        k_ref[rows, 0:LANES] = jnp.where(left_head, k, k_sw).astype(BF16)
        k_ref[rows, LANES:2 * LANES] = jnp.where(left_head, k_sw, k).astype(BF16)
        o += KV_WIDTH
        v = p[:, o:o + KV_WIDTH]
        v_sw = pltpu.roll(v, HEAD_DIM, 1)
        v_ref[rows, 0 * LANES:1 * LANES] = jnp.where(left_head, v, 0.0).astype(BF16)
        v_ref[rows, 1 * LANES:2 * LANES] = jnp.where(left_head, 0.0, v_sw).astype(BF16)
        v_ref[rows, 2 * LANES:3 * LANES] = jnp.where(left_head, v_sw, 0.0).astype(BF16)
        v_ref[rows, 3 * LANES:4 * LANES] = jnp.where(left_head, 0.0, v).astype(BF16)
        o += KV_WIDTH
        su_ref[rows, :] = p[:, o:o + SG_WIDTH].astype(BF16)
        o += SG_WIDTH
        sv_ref[rows, :] = p[:, o:o + SG_WIDTH].astype(BF16)
        o += SG_WIDTH
        g_ref[rows, :] = p[:, o:o + EVEN_GATE_WIDTH].astype(BF16)


def _even_in_proj(x, mod, positions, w_in):
    batch, seq, _ = x.shape
    d = np.arange(LANES) % HEAD_DIM
    inv_freq = np.power(np.float64(ROPE_THETA), -np.arange(ROPE_HALF) / ROPE_HALF)
    freq_lane = jnp.asarray(np.where(d < ROPE_DIM, inv_freq[d % ROPE_HALF], 0.0), dtype=F32).reshape(1, LANES)
    sign_lane = jnp.asarray(np.where(d < ROPE_HALF, -1.0, np.where(d < ROPE_DIM, 1.0, 0.0)),
                            dtype=F32).reshape(1, LANES)
    row = lambda width: pl.BlockSpec((None, ROW_TILE, width), lambda b, i: (b, i, 0))
    const = lambda shape: pl.BlockSpec(shape, lambda b, i: (0,) * len(shape))
    widths = (ATTN_WIDTH, 2 * KV_WIDTH, 4 * KV_WIDTH, SG_WIDTH, SG_WIDTH, EVEN_GATE_WIDTH)
    return pl.pallas_call(
        _even_in_kernel,
        grid=(batch, seq // ROW_TILE),
        in_specs=[row(D_MODEL), _mod_spec(mod, EVEN_LAYER, MOD_SHIFT), _mod_spec(mod, EVEN_LAYER, MOD_SCALE),
                  pl.BlockSpec((batch, ROW_TILE), lambda b, i: (0, i)),
                  const((1, LANES)), const((1, LANES)), _resident((D_MODEL, EVEN_IN_WIDTH))],
        out_specs=[row(w) for w in widths],
        out_shape=[jax.ShapeDtypeStruct((batch, seq, w), BF16) for w in widths],
        scratch_shapes=[pltpu.VMEM((D_MODEL, EVEN_IN_WIDTH), BF16)],
        compiler_params=_params("arbitrary", "arbitrary"),
        name="even_in_proj",
    )(x, mod, mod, positions, freq_lane, sign_lane, w_in)


def _even_mix_kernel(sink_ref, q_ref, k_ref, v_ref, su_ref, sv_ref, g_ref, x_ref, gate_ref,
                     avg_ref, avg2_ref, sgg_ref, sgb_ref, sgw_ref, sgbias_ref, wout_f32_ref, lng_ref, lnb_ref,
                     o_ref, y_scr, wout_ref):
    _cast_weight_once(wout_f32_ref, wout_ref)
    seq = k_ref.shape[0]
    n_blocks = seq // ATTN_BLOCK
    sub_blocks = MIX_TILE // ATTN_BLOCK
    tile = pl.program_id(1)
    lane = lax.broadcasted_iota(jnp.int32, (1, LANES), 1)
    left = lane < HEAD_DIM

    qi = lax.broadcasted_iota(jnp.int32, (ATTN_BLOCK, ATTN_BLOCK), 0)
    kj = lax.broadcasted_iota(jnp.int32, (ATTN_BLOCK, ATTN_BLOCK), 1)
    ones_left = jnp.broadcast_to(jnp.where(left, 1.0, 0.0).astype(BF16), (3 * ATTN_BLOCK, LANES))
    ones_right = jnp.broadcast_to(jnp.where(left, 0.0, 1.0).astype(BF16), (3 * ATTN_BLOCK, LANES))
    avg = avg_ref[...]
    sink_cols = [jnp.concatenate([jnp.full((ATTN_BLOCK, LANES), sink_ref[h] * LOG2_E, F32)
                                  for h in (4 * hk, 4 * hk + 2, 4 * hk + 1, 4 * hk + 3)], axis=0)
                 for hk in range(N_KV_HEADS)]

    def sub_block(j, carry):
        blk = tile * sub_blocks + j
        prev = jnp.maximum(blk - 1, 0)
        nxt = jnp.minimum(blk + 1, n_blocks - 1)
        rows = pl.ds(pl.multiple_of(j * ATTN_BLOCK, ATTN_BLOCK), ATTN_BLOCK)
        bias_prev = jnp.where((kj >= qi) & (blk > 0), 0.0, NEG_INF)
        bias_next = jnp.where((kj <= qi) & (blk < n_blocks - 1), 0.0, NEG_INF)
        bias_prev = jnp.concatenate([bias_prev] * Q_PER_KV, axis=0)
        bias_next = jnp.concatenate([bias_next] * Q_PER_KV, axis=0)

        def band(ref):
            return jnp.concatenate(
                [ref[pl.ds(pl.multiple_of(b * ATTN_BLOCK, ATTN_BLOCK), ATTN_BLOCK), :] for b in (prev, blk, nxt)],
                axis=0)

        k_band = band(k_ref)
        v_band = band(v_ref)
        n_sg = SG_WIDTH // LANES
        sg_cols = [slice(c * LANES, (c + 1) * LANES) for c in range(n_sg)]

        def silu_gate(cols):
            return _silu(g_ref[rows, cols].astype(F32))

        scores = []
        for hk in range(N_KV_HEADS):
            q_cols = [q_ref[rows, (2 * hk + c) * LANES:(2 * hk + c + 1) * LANES] for c in range(2)]
            zero = jnp.zeros_like(q_cols[0])
            lhs = jnp.concatenate([jnp.where(left, qc, zero) for qc in q_cols]
                                  + [jnp.where(left, zero, qc) for qc in q_cols], axis=0)
            scores.append(lax.dot_general(lhs, k_band[:, hk * LANES:(hk + 1) * LANES], (((1,), (1,)), ((), ())),
                                          preferred_element_type=F32))
        sv_cols = [sv_ref[rows, cols] for cols in sg_cols]
        means = [jnp.dot(v_c, avg, preferred_element_type=F32) for v_c in sv_cols]

        probs, row_max = [], []
        for hk in range(N_KV_HEADS):
            s = scores[hk]
            s0 = s[:, 0:ATTN_BLOCK] + bias_prev
            s1 = s[:, ATTN_BLOCK:2 * ATTN_BLOCK]
            s2 = s[:, 2 * ATTN_BLOCK:3 * ATTN_BLOCK] + bias_next
            m = jnp.max(jnp.maximum(jnp.maximum(s0, s1), s2), axis=-1, keepdims=True)
            m = jnp.maximum(jnp.broadcast_to(m, sink_cols[hk].shape), sink_cols[hk])
            probs.append(jnp.concatenate([jnp.exp2(t - m).astype(BF16) for t in (s0, s1, s2)], axis=1))
            row_max.append(m)
        devs, sq_parts = [], []
        for v_c, mean in zip(sv_cols, means):
            dev = v_c.astype(F32) - mean
            sq = dev * dev
            sq_hi = sq.astype(BF16)
            sq_lo = (sq - sq_hi.astype(F32)).astype(BF16)
            devs.append(dev)
            sq_parts.append(jnp.concatenate([sq_hi, sq_lo], axis=1))

        pv = []
        for hk in range(N_KV_HEADS):
            p = probs[hk]
            lhs = jnp.concatenate(
                [jnp.concatenate([p[c * ATTN_BLOCK:(c + 1) * ATTN_BLOCK], p[(2 + c) * ATTN_BLOCK:(3 + c) * ATTN_BLOCK]],
                                 axis=1) for c in range(2)], axis=0)
            rhs = jnp.concatenate(
                [jnp.concatenate([v_band[:, (2 * hk) * LANES:(2 * hk + 1) * LANES], ones_left], axis=1),
                 jnp.concatenate([v_band[:, (2 * hk + 1) * LANES:(2 * hk + 2) * LANES], ones_right], axis=1)],
                axis=0)
            pv.append(jnp.dot(lhs, rhs, preferred_element_type=F32))
        variances = [jnp.dot(sq, avg2_ref[...], preferred_element_type=F32) for sq in sq_parts]

        for hk in range(N_KV_HEADS):
            for c in range(2):
                even, odd = slice(c * ATTN_BLOCK, (c + 1) * ATTN_BLOCK), slice((2 + c) * ATTN_BLOCK, (3 + c) * ATTN_BLOCK)
                sink_term = jnp.exp2(jnp.where(left, sink_cols[hk][even] - row_max[hk][even],
                                               sink_cols[hk][odd] - row_max[hk][odd]))
                o = pv[hk][c * ATTN_BLOCK:(c + 1) * ATTN_BLOCK]
                pair = o[:, 0:LANES] * (1.0 / (o[:, LANES:2 * LANES] + sink_term))
                cols = slice((2 * hk + c) * LANES, (2 * hk + c + 1) * LANES)
                y_scr[rows, cols] = (pair * silu_gate(cols)).astype(BF16)

        for c, cols in enumerate(sg_cols):
            vn = devs[c] * lax.rsqrt(variances[c] + LN_EPS) * sgg_ref[:, cols] + sgb_ref[:, cols]
            stacked = jnp.concatenate([jnp.where(left, vn, 0.0), jnp.where(left, 0.0, vn)],
                                      axis=0).astype(BF16)
            mixed = jnp.dot(sgw_ref[c], stacked, preferred_element_type=F32) + sgbias_ref[:, cols]
            y_sg = su_ref[rows, cols].astype(F32) * mixed
            gc = slice(ATTN_WIDTH + c * LANES, ATTN_WIDTH + (c + 1) * LANES)
            y_scr[rows, gc] = (y_sg * silu_gate(gc)).astype(BF16)
        return carry

    lax.fori_loop(0, sub_blocks, sub_block, 0, unroll=True)

    for r in range(2):
        half_rows = slice(r * (MIX_TILE // 2), (r + 1) * (MIX_TILE // 2))
        out = jnp.dot(y_scr[half_rows, :], wout_ref[...], preferred_element_type=F32)
        o_ref[half_rows, :] = _deepnorm_rows(x_ref[half_rows, :], _batch_row(gate_ref), out,
                                             lng_ref[EVEN_LAYER:EVEN_LAYER + 1, :], lnb_ref[EVEN_LAYER:EVEN_LAYER + 1, :])


def _even_mixer(q, k, v, su, sv, g, x, mod, sink, sg_ln_g, sg_ln_b, sg_w, sg_b, w_out, ln_g, ln_b):
    batch, seq, _ = x.shape
    lane = np.arange(LANES)
    avg = jnp.asarray((lane[:, None] // SG_GROUP_DIM == lane[None, :] // SG_GROUP_DIM) / SG_GROUP_DIM, dtype=BF16)
    w_pairs = sg_w.reshape(N_SG_GROUPS // 2, 2, SG_CHUNK, SG_CHUNK).transpose(0, 2, 1, 3)
    w_pairs = w_pairs.reshape(N_SG_GROUPS // 2, SG_CHUNK, 2 * SG_CHUNK).astype(BF16)
    bias_full = jnp.repeat(sg_b.T, SG_GROUP_DIM, axis=1)
    row = lambda width: pl.BlockSpec((None, MIX_TILE, width), lambda b, i: (b, i, 0))
    full_seq = lambda arr: pl.BlockSpec((None, seq, arr.shape[-1]), lambda b, i: (b, 0, 0))
    const = lambda shape: pl.BlockSpec(shape, lambda b, i: (0,) * len(shape))
    return pl.pallas_call(
        _even_mix_kernel,
        grid=(batch, seq // MIX_TILE),
        in_specs=[
            pl.BlockSpec(memory_space=pltpu.SMEM),
            row(ATTN_WIDTH), full_seq(k), full_seq(v), row(SG_WIDTH), row(SG_WIDTH), row(EVEN_GATE_WIDTH),
            row(D_MODEL), _mod_spec(mod, EVEN_LAYER, MOD_GATE),
            const((LANES, LANES)), const((2 * LANES, LANES)), const((1, SG_WIDTH)), const((1, SG_WIDTH)),
            const((N_SG_GROUPS // 2, SG_CHUNK, 2 * SG_CHUNK)), const((SG_CHUNK, SG_WIDTH)),
            _resident((EVEN_GATE_WIDTH, D_MODEL)), const(ln_g.shape), const(ln_b.shape),
        ],
        out_specs=row(D_MODEL),
        out_shape=jax.ShapeDtypeStruct((batch, seq, D_MODEL), F32),
        scratch_shapes=[pltpu.VMEM((MIX_TILE, EVEN_GATE_WIDTH), BF16),
                        pltpu.VMEM((EVEN_GATE_WIDTH, D_MODEL), BF16)],
        compiler_params=_params("arbitrary", "arbitrary"),
        name="even_mixer",
    )(sink, q, k, v, su, sv, g, x, mod, avg, jnp.concatenate([avg, avg], axis=0), sg_ln_g.reshape(1, -1), sg_ln_b.reshape(1, -1),
      w_pairs, bias_full, w_out, ln_g, ln_b)


def _odd_in_kernel(x_ref, shift_ref, scale_ref, w_f32_ref, xr_ref, w_ref):
    _cast_weight_once(w_f32_ref, w_ref)
    seg_len = xr_ref.shape[1]
    head_rows = seg_len * SEGS_PER_TILE
    xr_rows = xr_ref.reshape(RNN_HEADS * head_rows, LANES)
    halves = 2
    for r in range(halves):
        first = r * (SEGS_PER_TILE // halves)
        rows = slice(first * seg_len, (first + SEGS_PER_TILE // halves) * seg_len)
        h = (x_ref[rows, :] * (1.0 + _batch_row(scale_ref)) + _batch_row(shift_ref)).astype(BF16)
        p = jnp.dot(h, w_ref[...], preferred_element_type=F32)
        for s in range(SEGS_PER_TILE // halves):
            for hd in range(RNN_HEADS):
                xr_rows[pl.ds(hd * head_rows + first + s, seg_len, stride=SEGS_PER_TILE), :] = (
                    p[s * seg_len:(s + 1) * seg_len, hd * LANES:(hd + 1) * LANES])


def _odd_in_proj(x, mod, w_in):
    batch, seq, _ = x.shape
    seg_len = seq // N_SEG
    tile_rows = SEGS_PER_TILE * seg_len
    out_spec = pl.BlockSpec((None, RNN_HEADS, seg_len, SEGS_PER_TILE, RNN_HEAD_DIM), lambda b, i: (b, 0, 0, i, 0))
    out_shape = jax.ShapeDtypeStruct((batch, RNN_HEADS, seg_len, N_SEG, RNN_HEAD_DIM), F32)
    return pl.pallas_call(
        _odd_in_kernel,
        grid=(batch, seq // tile_rows),
        in_specs=[pl.BlockSpec((None, tile_rows, D_MODEL), lambda b, i: (b, i, 0)),
                  _mod_spec(mod, ODD_LAYER, MOD_SHIFT), _mod_spec(mod, ODD_LAYER, MOD_SCALE),
                  _resident((D_MODEL, RNN_WIDTH), col_block=0)],
        out_specs=out_spec,
        out_shape=out_shape,
        scratch_shapes=[pltpu.VMEM((D_MODEL, RNN_WIDTH), BF16)],
        compiler_params=_params("arbitrary", "arbitrary"),
        name="odd_in_proj",
    )(x, mod, mod, w_in)


def _rglru_kernel(x_ref, convw_ref, convb_ref, waug_ref, lam_ref, o_ref,
                  halo, a_f, b_f, a_b, b_b, edge):
    n_rows = x_ref.shape[0]
    seg_len = n_rows // N_SEG
    n_chunks = n_rows // SCAN_CHUNK
    left_rows = CONV_LEFT * N_SEG
    right_rows = (CONV_WIDTH - 1 - CONV_LEFT) * N_SEG
    seg = lax.broadcasted_iota(jnp.int32, (N_SEG, LANES), 0)

    def from_prev_segment(blk):
        return jnp.where(seg == 0, 0.0, pltpu.roll(blk, 1, 0))

    def from_next_segment(blk):
        return jnp.where(seg == N_SEG - 1, 0.0, pltpu.roll(blk, N_SEG - 1, 0))

    for t in range(CONV_LEFT):
        halo[0, pl.ds(t * N_SEG, N_SEG), :] = from_prev_segment(
            x_ref[pl.ds((seg_len - CONV_LEFT + t) * N_SEG, N_SEG), :])
    halo[0, pl.ds(left_rows, SCAN_CHUNK + right_rows), :] = x_ref[pl.ds(0, SCAN_CHUNK + right_rows), :]
    halo[1, pl.ds(0, left_rows + SCAN_CHUNK), :] = x_ref[pl.ds(n_rows - SCAN_CHUNK - left_rows, left_rows + SCAN_CHUNK), :]
    for t in range(CONV_WIDTH - 1 - CONV_LEFT):
        halo[1, pl.ds(left_rows + SCAN_CHUNK + t * N_SEG, N_SEG), :] = from_next_segment(
            x_ref[pl.ds(t * N_SEG, N_SEG), :])

    lam = lam_ref[...]
    softplus_neg = jnp.maximum(-lam, 0.0) + jnp.log1p(jnp.exp(-jnp.abs(lam)))
    half_rate = (0.5 * RG_LRU_C) * softplus_neg
    bias_lanes = jnp.broadcast_to(
        jnp.where(lax.broadcasted_iota(jnp.int32, (1, LANES), 1) < BIAS_ROWS, 1.0, 0.0).astype(BF16),
        (SCAN_CHUNK, LANES))

    def gates(base, tap):
        half_x = convb_ref[...] + sum(convw_ref[k:k + 1, :] * tap(k) for k in range(CONV_WIDTH))
        th = jnp.tanh(jnp.dot(jnp.concatenate([half_x.astype(BF16), bias_lanes], axis=1), waug_ref[...],
                              preferred_element_type=F32))
        for d, (a_ref, b_ref) in enumerate(((a_f, b_f), (a_b, b_b))):
            o = 2 * d * LANES
            neg_log_a = (th[:, o:o + LANES] + 1.0) * half_rate[d:d + 1, :]
            a = jnp.exp2(neg_log_a * (-LOG2_E))
            one_minus_a2 = jnp.tanh(neg_log_a) * (a * a + 1.0)
            root = one_minus_a2 * lax.rsqrt(jnp.maximum(one_minus_a2, F32_TINY))
            a_ref[pl.ds(base, SCAN_CHUNK), :] = a
            b_ref[pl.ds(base, SCAN_CHUNK), :] = root * ((th[:, o + LANES:o + 2 * LANES] + 1.0) * half_x)

    def inner_chunk(c, carry):
        base = pl.multiple_of(c * SCAN_CHUNK, SCAN_CHUNK)
        gates(base, lambda k: x_ref[pl.ds(base + (k - CONV_LEFT) * N_SEG, SCAN_CHUNK), :])
        return carry

    gates(0, lambda k: halo[0, pl.ds(k * N_SEG, SCAN_CHUNK), :])
    gates(n_rows - SCAN_CHUNK, lambda k: halo[1, pl.ds(k * N_SEG, SCAN_CHUNK), :])
    lax.fori_loop(1, n_chunks - 1, inner_chunk, 0, unroll=2)

    def scan_step(t, carry):
        h_f, p_f, h_b, p_b = carry
        rf = pl.ds(pl.multiple_of(t * N_SEG, N_SEG), N_SEG)
        rb = pl.ds(pl.multiple_of((seg_len - 1 - t) * N_SEG, N_SEG), N_SEG)
        a = a_f[rf, :]
        h_f = a * h_f + b_f[rf, :]
        p_f = a * p_f
        b_f[rf, :] = h_f
        a_f[rf, :] = p_f
        a = a_b[rb, :]
        h_b = a * h_b + b_b[rb, :]
        p_b = a * p_b
        b_b[rb, :] = h_b
        a_b[rb, :] = p_b
        return h_f, p_f, h_b, p_b

    zeros = jnp.zeros((N_SEG, LANES), F32)
    ones = jnp.ones((N_SEG, LANES), F32)
    h_f, p_f, h_b, p_b = lax.fori_loop(0, seg_len, scan_step, (zeros, ones, zeros, ones), unroll=True)

    edge[0] = h_f
    edge[1] = p_f
    edge[2] = h_b
    edge[3] = p_b
    enter_f = enter_b = jnp.zeros((1, LANES), F32)
    edge[4, 0:1, :] = enter_f
    edge[5, N_SEG - 1:N_SEG, :] = enter_b
    for s in range(1, N_SEG):
        enter_f = edge[0, s - 1:s, :] + edge[1, s - 1:s, :] * enter_f
        edge[4, s:s + 1, :] = enter_f
        r = N_SEG - 1 - s
        enter_b = edge[2, r + 1:r + 2, :] + edge[3, r + 1:r + 2, :] * enter_b
        edge[5, r:r + 1, :] = enter_b
    blocks = SCAN_CHUNK // N_SEG
    in_f = jnp.concatenate([edge[4]] * blocks, axis=0)
    in_b = jnp.concatenate([edge[5]] * blocks, axis=0)

    def out_chunk(c, carry):
        rows = pl.ds(pl.multiple_of(c * SCAN_CHUNK, SCAN_CHUNK), SCAN_CHUNK)
        o_ref[rows, :] = (b_f[rows, :] + a_f[rows, :] * in_f) + (b_b[rows, :] + a_b[rows, :] * in_b)
        return carry

    lax.fori_loop(0, n_rows // SCAN_CHUNK, out_chunk, 0, unroll=2)


def _rglru(xr, conv_w, conv_b, w_a, b_a, w_x, b_x, lam):
    batch, _, n_rows, _ = xr.shape
    w_cat = jnp.concatenate([w_a[0], w_x[0], w_a[1], w_x[1]], axis=-1).astype(BF16)
    per_head = lambda v: v.reshape(RNN_HEADS, 1, RNN_HEAD_DIM)
    half_b = 0.5 * jnp.concatenate([per_head(b_a[0]), per_head(b_x[0]), per_head(b_a[1]), per_head(b_x[1])],
                                   axis=-1)
    b_hi = half_b.astype(BF16)
    b_lo = (half_b - b_hi.astype(F32)).astype(BF16)
    w_aug = jnp.concatenate([w_cat, b_hi, b_lo,
                             jnp.zeros((RNN_HEADS, RNN_HEAD_DIM - BIAS_ROWS, 4 * RNN_HEAD_DIM), BF16)], axis=1)
    col = lambda rows: pl.BlockSpec((rows, RNN_HEAD_DIM), lambda b, h: (0, h))
    seq_col = pl.BlockSpec((None, None, n_rows, RNN_HEAD_DIM), lambda b, h: (b, h, 0, 0))
    halo_rows = SCAN_CHUNK + (CONV_WIDTH - 1) * N_SEG
    return pl.pallas_call(
        _rglru_kernel,
        grid=(batch, RNN_HEADS),
        in_specs=[seq_col, col(CONV_WIDTH), col(1),
                  pl.BlockSpec((None, 2 * RNN_HEAD_DIM, 4 * RNN_HEAD_DIM), lambda b, h: (h, 0, 0)),
                  col(2)],
        out_specs=seq_col,
        out_shape=jax.ShapeDtypeStruct((batch, RNN_HEADS, n_rows, RNN_HEAD_DIM), F32),
        scratch_shapes=[pltpu.VMEM((2, halo_rows, LANES), F32)]
                       + [pltpu.VMEM((n_rows, LANES), F32)] * 4
                       + [pltpu.VMEM((6, N_SEG, LANES), F32)],
        compiler_params=_params("parallel", "parallel"),
        name="rglru",
    )(xr, 0.5 * conv_w, 0.5 * conv_b.reshape(1, -1), w_aug, lam)


def _odd_out_kernel(y_ref, x_ref, shift_ref, scale_ref, gate_ref, wg_f32_ref, w_f32_ref, lng_ref, lnb_ref,
                    o_ref, wg_ref, w_ref):
    _cast_weight_once(wg_f32_ref, wg_ref)
    _cast_weight_once(w_f32_ref, w_ref)
    seg_len = y_ref.shape[1]
    head_rows = seg_len * SEGS_PER_TILE
    y_rows = y_ref.reshape(RNN_HEADS * head_rows, LANES)
    x = x_ref[...]
    h = (x * (1.0 + _batch_row(scale_ref)) + _batch_row(shift_ref)).astype(BF16)
    g_all = jnp.dot(h, wg_ref[...], preferred_element_type=F32)
    gated = []
    for s in range(SEGS_PER_TILE):
        y = jnp.concatenate([y_rows[pl.ds(hd * head_rows + s, seg_len, stride=SEGS_PER_TILE), :]
                             for hd in range(RNN_HEADS)], axis=1)
        g = g_all[s * seg_len:(s + 1) * seg_len, :]
        gated.append((y * _silu(g)).astype(BF16))
    out = jnp.dot(jnp.concatenate(gated, axis=0), w_ref[...], preferred_element_type=F32)
    o_ref[...] = _deepnorm_rows(x, _batch_row(gate_ref), out,
                                lng_ref[ODD_LAYER:ODD_LAYER + 1, :], lnb_ref[ODD_LAYER:ODD_LAYER + 1, :])


def _odd_out_proj(y, x, mod, w_in, w_out, ln_g, ln_b):
    batch, seq, _ = x.shape
    seg_len = seq // N_SEG
    tile_rows = SEGS_PER_TILE * seg_len
    const = lambda shape: pl.BlockSpec(shape, lambda b, i: (0,) * len(shape))
    return pl.pallas_call(
        _odd_out_kernel,
        grid=(batch, seq // tile_rows),
        in_specs=[pl.BlockSpec((None, RNN_HEADS, seg_len, SEGS_PER_TILE, RNN_HEAD_DIM),
                               lambda b, i: (b, 0, 0, i, 0)),
                  pl.BlockSpec((None, tile_rows, D_MODEL), lambda b, i: (b, i, 0)),
                  _mod_spec(mod, ODD_LAYER, MOD_SHIFT), _mod_spec(mod, ODD_LAYER, MOD_SCALE),
                  _mod_spec(mod, ODD_LAYER, MOD_GATE),
                  _resident((D_MODEL, RNN_WIDTH), col_block=1), _resident((RNN_WIDTH, D_MODEL)),
                  const(ln_g.shape), const(ln_b.shape)],
        out_specs=pl.BlockSpec((None, tile_rows, D_MODEL), lambda b, i: (b, i, 0)),
        out_shape=jax.ShapeDtypeStruct((batch, seq, D_MODEL), F32),
        scratch_shapes=[pltpu.VMEM((D_MODEL, RNN_WIDTH), BF16), pltpu.VMEM((RNN_WIDTH, D_MODEL), BF16)],
        compiler_params=_params("arbitrary", "arbitrary"),
        name="odd_out_proj",
    )(y, x, mod, mod, mod, w_in, w_out, ln_g, ln_b)


def kernel(x, c, positions, ada_w, ada_b, ln_g, ln_b, ev_w_in, ev_w_out, ev_sink, ev_sg_ln_g, ev_sg_ln_b,
           ev_sg_w, ev_sg_b, od_w_in, od_conv_w, od_conv_b, od_w_a, od_b_a, od_w_x, od_b_x, od_lam, od_w_out):
    batch, seq, d_model = x.shape
    seg_len = seq // N_SEG
    assert d_model == D_MODEL and seq % ROW_TILE == 0 and seq % MIX_TILE == 0 and seq == N_SEG * seg_len
    assert N_SEG % SEGS_PER_TILE == 0 and seq % SCAN_CHUNK == 0 and SCAN_CHUNK % N_SEG == 0
    assert ada_w.shape[0] == DEPTH == 2

    mod = _modulation(c, ada_w, ada_b)

    q, k, v, su, sv, g = _even_in_proj(x, mod, positions, ev_w_in[0])
    x1 = _even_mixer(q, k, v, su, sv, g, x, mod, ev_sink[0], ev_sg_ln_g[0], ev_sg_ln_b[0],
                     ev_sg_w[0], ev_sg_b[0], ev_w_out[0], ln_g, ln_b)

    xr = _odd_in_proj(x1, mod, od_w_in[0])
    y = _rglru(xr.reshape(batch, RNN_HEADS, seq, RNN_HEAD_DIM), od_conv_w[0], od_conv_b[0], od_w_a[0],
               od_b_a[0], od_w_x[0], od_b_x[0], od_lam[0])
    return _odd_out_proj(y.reshape(batch, RNN_HEADS, seg_len, N_SEG, RNN_HEAD_DIM), x1, mod, od_w_in[0],
                         od_w_out[0], ln_g, ln_b)
```

```python
import math

import numpy as np
import jax
import jax.numpy as jnp
from jax import lax
from jax.experimental import pallas as pl
from jax.experimental.pallas import tpu as pltpu

D_MODEL = 1024
DEPTH = 2
HEAD_DIM = 64
N_Q_HEADS = 8
N_KV_HEADS = 2
Q_PER_KV = N_Q_HEADS // N_KV_HEADS
ATTN_WIDTH = N_Q_HEADS * HEAD_DIM
KV_WIDTH = N_KV_HEADS * HEAD_DIM
ATTN_BLOCK = 128
ROPE_DIM = HEAD_DIM // 4
ROPE_HALF = ROPE_DIM // 2
ROPE_THETA = 500000.0
N_SG_GROUPS = 8
SG_GROUP_DIM = 64
SG_WIDTH = N_SG_GROUPS * SG_GROUP_DIM
SG_CHUNK = 128
EVEN_GATE_WIDTH = ATTN_WIDTH + SG_WIDTH
EVEN_IN_WIDTH = ATTN_WIDTH + 2 * KV_WIDTH + 2 * SG_WIDTH + EVEN_GATE_WIDTH
RNN_WIDTH = D_MODEL
RNN_HEADS = 8
RNN_HEAD_DIM = RNN_WIDTH // RNN_HEADS
CONV_WIDTH = 4
CONV_LEFT = 2
RG_LRU_C = 8.0
DEEPNORM_ALPHA = (2 * DEPTH) ** 0.25
LN_EPS = 1e-5
NEG_INF = -1e30
LOG2_E = math.log2(math.e)
F32_TINY = float(np.finfo(np.float32).tiny)

LANES = 128
SUBLANES = 8
VMEM_LIMIT = 56 * 1024 * 1024

ROW_TILE = 1024
MIX_TILE = 512
N_SEG = 32
SEGS_PER_TILE = SUBLANES
SCAN_CHUNK = 1024
X_RING = 3
WEIGHT_CAST_ROWS = 128
BIAS_ROWS = 2

BF16 = jnp.bfloat16
F32 = jnp.float32


def _silu(x):
    h = 0.5 * x
    return h + h * jnp.tanh(h)


def _layer_norm_rows(z, g, b, eps=LN_EPS):
    mu = jnp.mean(z, axis=-1, keepdims=True)
    d = z - mu
    var = jnp.mean(d * d, axis=-1, keepdims=True)
    return d * lax.rsqrt(var + eps) * g + b


def _deepnorm_rows(x, gate, y, g, b):
    return _layer_norm_rows(x + (gate * (1.0 / DEEPNORM_ALPHA)) * y, g, b, LN_EPS / DEEPNORM_ALPHA ** 2)


def _params(*semantics):
    return pltpu.CompilerParams(dimension_semantics=semantics, vmem_limit_bytes=VMEM_LIMIT)


def _cast_weight_once(w_ref, w_bf16):
    @pl.when((pl.program_id(0) == 0) & (pl.program_id(1) == 0))
    def _():
        for r in range(0, w_ref.shape[0], WEIGHT_CAST_ROWS):
            w_bf16[r:r + WEIGHT_CAST_ROWS, :] = w_ref[r:r + WEIGHT_CAST_ROWS, :].astype(BF16)


def _resident(shape, col_block=0):
    return pl.BlockSpec(shape, lambda b, i: (0,) * (len(shape) - 1) + (col_block,), pipeline_mode=pl.Buffered(1))


def _mod_kernel(c_ref, w_ref, b_ref, o_ref):
    c = c_ref[...]
    cond = _silu(c)
    pad_rows = o_ref.shape[0] - cond.shape[0]
    if pad_rows:
        cond = jnp.concatenate([cond, jnp.zeros((pad_rows, cond.shape[1]), F32)], axis=0)
    bias = b_ref[pl.ds(pl.program_id(0), 1), :]
    o_ref[...] = jnp.dot(cond.astype(BF16), w_ref[...].astype(BF16), preferred_element_type=F32) + bias


def _modulation(c, ada_w, ada_b):
    batch = c.shape[0]
    rows = -(-batch // SUBLANES) * SUBLANES
    n_col = 3 * D_MODEL // D_MODEL
    return pl.pallas_call(
        _mod_kernel,
        grid=(DEPTH, n_col),
        in_specs=[
            pl.BlockSpec((batch, D_MODEL), lambda l, j: (0, 0)),
            pl.BlockSpec((None, D_MODEL, D_MODEL), lambda l, j: (l, 0, j)),
            pl.BlockSpec((DEPTH, D_MODEL), lambda l, j: (0, j)),
        ],
        out_specs=pl.BlockSpec((None, rows, D_MODEL), lambda l, j: (l, 0, j)),
        out_shape=jax.ShapeDtypeStruct((DEPTH, rows, 3 * D_MODEL), F32),
        compiler_params=_params("parallel", "parallel"),
        name="adaln_mod",
    )(c, ada_w, ada_b)


MOD_SHIFT, MOD_SCALE, MOD_GATE = range(3)
EVEN_LAYER, ODD_LAYER = 0, 1


def _mod_spec(mod, layer, part):
    return pl.BlockSpec((None, mod.shape[1], D_MODEL), lambda b, i: (layer, 0, part))


def _batch_row(ref):
    return ref[pl.ds(pl.program_id(0), 1), :]


def _even_in_kernel(x_ref, shift_ref, scale_ref, pos_ref, freq_ref, sign_ref, w_f32_ref,
                    q_ref, k_ref, v_ref, su_ref, sv_ref, g_ref, w_ref):
    _cast_weight_once(w_f32_ref, w_ref)
    lane = lax.broadcasted_iota(jnp.int32, (1, LANES), 1)
    first_half = (lane % HEAD_DIM) < ROPE_HALF
    left_head = lane < HEAD_DIM
    diagonal = lax.broadcasted_iota(jnp.int32, (LANES, LANES), 0) == lax.broadcasted_iota(jnp.int32, (LANES, LANES), 1)
    batch_ids = lax.broadcasted_iota(jnp.int32, (pos_ref.shape[0], LANES), 0)
    q_scale = HEAD_DIM ** -0.5 * LOG2_E

    half_rows = x_ref.shape[0] // 2
    for r in range(2):
        rows = slice(r * half_rows, (r + 1) * half_rows)
        h = (x_ref[rows, :] * (1.0 + _batch_row(scale_ref)) + _batch_row(shift_ref)).astype(BF16)
        p = jnp.dot(h, w_ref[...], preferred_element_type=F32)

        pos_cols = []
        for j in range(r * half_rows // LANES, (r + 1) * half_rows // LANES):
            pos_all = pos_ref[:, j * LANES:(j + 1) * LANES].astype(F32)
            pos_row = jnp.sum(jnp.where(batch_ids == pl.program_id(0), pos_all, 0.0), axis=0, keepdims=True)
            pos_cols.append(jnp.sum(jnp.where(diagonal, pos_row, 0.0), axis=1, keepdims=True))
        ang = jnp.concatenate(pos_cols, axis=0) * freq_ref[...]
        cs = jnp.cos(ang)
        sn = jnp.sin(ang) * sign_ref[...]

        def rotary(t):
            partner = jnp.where(first_half, pltpu.roll(t, LANES - ROPE_HALF, 1), pltpu.roll(t, ROPE_HALF, 1))
            return t * cs + partner * sn

        for j in range(ATTN_WIDTH // LANES):
            q_ref[rows, j * LANES:(j + 1) * LANES] = (
                rotary(p[:, j * LANES:(j + 1) * LANES]) * q_scale).astype(BF16)
        o = ATTN_WIDTH
        k = rotary(p[:, o:o + KV_WIDTH])
        k_sw = pltpu.roll(k, HEAD_DIM, 1)
        k_ref[rows, 0:LANES] = jnp.where(left_head, k, k_sw).astype(BF16)
        k_ref[rows, LANES:2 * LANES] = jnp.where(left_head, k_sw, k).astype(BF16)
        o += KV_WIDTH
        v = p[:, o:o + KV_WIDTH]
        v_sw = pltpu.roll(v, HEAD_DIM, 1)
        v_ref[rows, 0 * LANES:1 * LANES] = jnp.where(left_head, v, 0.0).astype(BF16)
        v_ref[rows, 1 * LANES:2 * LANES] = jnp.where(left_head, 0.0, v_sw).astype(BF16)
        v_ref[rows, 2 * LANES:3 * LANES] = jnp.where(left_head, v_sw, 0.0).astype(BF16)
        v_ref[rows, 3 * LANES:4 * LANES] = jnp.where(left_head, 0.0, v).astype(BF16)
        o += KV_WIDTH
        su_ref[rows, :] = p[:, o:o + SG_WIDTH].astype(BF16)
        o += SG_WIDTH
        sv_ref[rows, :] = p[:, o:o + SG_WIDTH].astype(BF16)
        o += SG_WIDTH
        g_ref[rows, :] = p[:, o:o + EVEN_GATE_WIDTH].astype(BF16)


def _even_in_proj(x, mod, positions, w_in):
    batch, seq, _ = x.shape
    d = np.arange(LANES) % HEAD_DIM
    inv_freq = np.power(np.float64(ROPE_THETA), -np.arange(ROPE_HALF) / ROPE_HALF)
    freq_lane = jnp.asarray(np.where(d < ROPE_DIM, inv_freq[d % ROPE_HALF], 0.0), dtype=F32).reshape(1, LANES)
    sign_lane = jnp.asarray(np.where(d < ROPE_HALF, -1.0, np.where(d < ROPE_DIM, 1.0, 0.0)),
                            dtype=F32).reshape(1, LANES)
    row = lambda width: pl.BlockSpec((None, ROW_TILE, width), lambda b, i: (b, i, 0))
    const = lambda shape: pl.BlockSpec(shape, lambda b, i: (0,) * len(shape))
    widths = (ATTN_WIDTH, 2 * KV_WIDTH, 4 * KV_WIDTH, SG_WIDTH, SG_WIDTH, EVEN_GATE_WIDTH)
    return pl.pallas_call(
        _even_in_kernel,
        grid=(batch, seq // ROW_TILE),
        in_specs=[row(D_MODEL), _mod_spec(mod, EVEN_LAYER, MOD_SHIFT), _mod_spec(mod, EVEN_LAYER, MOD_SCALE),
                  pl.BlockSpec((batch, ROW_TILE), lambda b, i: (0, i)),
                  const((1, LANES)), const((1, LANES)), _resident((D_MODEL, EVEN_IN_WIDTH))],
        out_specs=[row(w) for w in widths],
        out_shape=[jax.ShapeDtypeStruct((batch, seq, w), BF16) for w in widths],
        scratch_shapes=[pltpu.VMEM((D_MODEL, EVEN_IN_WIDTH), BF16)],
        compiler_params=_params("arbitrary", "arbitrary"),
        name="even_in_proj",
    )(x, mod, mod, positions, freq_lane, sign_lane, w_in)


def _even_mix_kernel(sink_ref, q_ref, k_ref, v_ref, su_ref, sv_ref, g_ref, x_ref, gate_ref,
                     avg_ref, avg2_ref, sgg_ref, sgb_ref, sgw_ref, sgbias_ref, wout_f32_ref, lng_ref, lnb_ref,
                     o_ref, y_scr, wout_ref):
    _cast_weight_once(wout_f32_ref, wout_ref)
    seq = k_ref.shape[0]
    n_blocks = seq // ATTN_BLOCK
    sub_blocks = MIX_TILE // ATTN_BLOCK
    tile = pl.program_id(1)
    lane = lax.broadcasted_iota(jnp.int32, (1, LANES), 1)
    left = lane < HEAD_DIM

    qi = lax.broadcasted_iota(jnp.int32, (ATTN_BLOCK, ATTN_BLOCK), 0)
    kj = lax.broadcasted_iota(jnp.int32, (ATTN_BLOCK, ATTN_BLOCK), 1)
    ones_left = jnp.broadcast_to(jnp.where(left, 1.0, 0.0).astype(BF16), (3 * ATTN_BLOCK, LANES))
    ones_right = jnp.broadcast_to(jnp.where(left, 0.0, 1.0).astype(BF16), (3 * ATTN_BLOCK, LANES))
    avg = avg_ref[...]
    sink_cols = [jnp.concatenate([jnp.full((ATTN_BLOCK, LANES), sink_ref[h] * LOG2_E, F32)
                                  for h in (4 * hk, 4 * hk + 2, 4 * hk + 1, 4 * hk + 3)], axis=0)
                 for hk in range(N_KV_HEADS)]

    def sub_block(j, carry):
        blk = tile * sub_blocks + j
        prev = jnp.maximum(blk - 1, 0)
        nxt = jnp.minimum(blk + 1, n_blocks - 1)
        rows = pl.ds(pl.multiple_of(j * ATTN_BLOCK, ATTN_BLOCK), ATTN_BLOCK)
        bias_prev = jnp.where((kj >= qi) & (blk > 0), 0.0, NEG_INF)
        bias_next = jnp.where((kj <= qi) & (blk < n_blocks - 1), 0.0, NEG_INF)
        bias_prev = jnp.concatenate([bias_prev] * Q_PER_KV, axis=0)
        bias_next = jnp.concatenate([bias_next] * Q_PER_KV, axis=0)

        def band(ref):
            return jnp.concatenate(
                [ref[pl.ds(pl.multiple_of(b * ATTN_BLOCK, ATTN_BLOCK), ATTN_BLOCK), :] for b in (prev, blk, nxt)],
                axis=0)

        k_band = band(k_ref)
        v_band = band(v_ref)
        n_sg = SG_WIDTH // LANES
        sg_cols = [slice(c * LANES, (c + 1) * LANES) for c in range(n_sg)]

        def silu_gate(cols):
            return _silu(g_ref[rows, cols].astype(F32))

        scores = []
        for hk in range(N_KV_HEADS):
            q_cols = [q_ref[rows, (2 * hk + c) * LANES:(2 * hk + c + 1) * LANES] for c in range(2)]
            zero = jnp.zeros_like(q_cols[0])
            lhs = jnp.concatenate([jnp.where(left, qc, zero) for qc in q_cols]
                                  + [jnp.where(left, zero, qc) for qc in q_cols], axis=0)
            scores.append(lax.dot_general(lhs, k_band[:, hk * LANES:(hk + 1) * LANES], (((1,), (1,)), ((), ())),
                                          preferred_element_type=F32))
        sv_cols = [sv_ref[rows, cols] for cols in sg_cols]
        means = [jnp.dot(v_c, avg, preferred_element_type=F32) for v_c in sv_cols]

        probs, row_max = [], []
        for hk in range(N_KV_HEADS):
            s = scores[hk]
            s0 = s[:, 0:ATTN_BLOCK] + bias_prev
            s1 = s[:, ATTN_BLOCK:2 * ATTN_BLOCK]
            s2 = s[:, 2 * ATTN_BLOCK:3 * ATTN_BLOCK] + bias_next
            m = jnp.max(jnp.maximum(jnp.maximum(s0, s1), s2), axis=-1, keepdims=True)
            m = jnp.maximum(jnp.broadcast_to(m, sink_cols[hk].shape), sink_cols[hk])
            probs.append(jnp.concatenate([jnp.exp2(t - m).astype(BF16) for t in (s0, s1, s2)], axis=1))
            row_max.append(m)
        devs, sq_parts = [], []
        for v_c, mean in zip(sv_cols, means):
            dev = v_c.astype(F32) - mean
            sq = dev * dev
            sq_hi = sq.astype(BF16)
            sq_lo = (sq - sq_hi.astype(F32)).astype(BF16)
            devs.append(dev)
            sq_parts.append(jnp.concatenate([sq_hi, sq_lo], axis=1))

        pv = []
        for hk in range(N_KV_HEADS):
            p = probs[hk]
            lhs = jnp.concatenate(
                [jnp.concatenate([p[c * ATTN_BLOCK:(c + 1) * ATTN_BLOCK], p[(2 + c) * ATTN_BLOCK:(3 + c) * ATTN_BLOCK]],
                                 axis=1) for c in range(2)], axis=0)
            rhs = jnp.concatenate(
                [jnp.concatenate([v_band[:, (2 * hk) * LANES:(2 * hk + 1) * LANES], ones_left], axis=1),
                 jnp.concatenate([v_band[:, (2 * hk + 1) * LANES:(2 * hk + 2) * LANES], ones_right], axis=1)],
                axis=0)
            pv.append(jnp.dot(lhs, rhs, preferred_element_type=F32))
        variances = [jnp.dot(sq, avg2_ref[...], preferred_element_type=F32) for sq in sq_parts]

        for hk in range(N_KV_HEADS):
            for c in range(2):
                even, odd = slice(c * ATTN_BLOCK, (c + 1) * ATTN_BLOCK), slice((2 + c) * ATTN_BLOCK, (3 + c) * ATTN_BLOCK)
                sink_term = jnp.exp2(jnp.where(left, sink_cols[hk][even] - row_max[hk][even],
                                               sink_cols[hk][odd] - row_max[hk][odd]))
                o = pv[hk][c * ATTN_BLOCK:(c + 1) * ATTN_BLOCK]
                pair = o[:, 0:LANES] * (1.0 / (o[:, LANES:2 * LANES] + sink_term))
                cols = slice((2 * hk + c) * LANES, (2 * hk + c + 1) * LANES)
                y_scr[rows, cols] = (pair * silu_gate(cols)).astype(BF16)

        for c, cols in enumerate(sg_cols):
            vn = devs[c] * lax.rsqrt(variances[c] + LN_EPS) * sgg_ref[:, cols] + sgb_ref[:, cols]
            stacked = jnp.concatenate([jnp.where(left, vn, 0.0), jnp.where(left, 0.0, vn)],
                                      axis=0).astype(BF16)
            mixed = jnp.dot(sgw_ref[c], stacked, preferred_element_type=F32) + sgbias_ref[:, cols]
            y_sg = su_ref[rows, cols].astype(F32) * mixed
            gc = slice(ATTN_WIDTH + c * LANES, ATTN_WIDTH + (c + 1) * LANES)
            y_scr[rows, gc] = (y_sg * silu_gate(gc)).astype(BF16)
        return carry

    lax.fori_loop(0, sub_blocks, sub_block, 0, unroll=True)

    for r in range(2):
        half_rows = slice(r * (MIX_TILE // 2), (r + 1) * (MIX_TILE // 2))
        out = jnp.dot(y_scr[half_rows, :], wout_ref[...], preferred_element_type=F32)
        o_ref[half_rows, :] = _deepnorm_rows(x_ref[half_rows, :], _batch_row(gate_ref), out,
                                             lng_ref[EVEN_LAYER:EVEN_LAYER + 1, :], lnb_ref[EVEN_LAYER:EVEN_LAYER + 1, :])


def _even_mixer(q, k, v, su, sv, g, x, mod, sink, sg_ln_g, sg_ln_b, sg_w, sg_b, w_out, ln_g, ln_b):
    batch, seq, _ = x.shape
    lane = np.arange(LANES)
    avg = jnp.asarray((lane[:, None] // SG_GROUP_DIM == lane[None, :] // SG_GROUP_DIM) / SG_GROUP_DIM, dtype=BF16)
    w_pairs = sg_w.reshape(N_SG_GROUPS // 2, 2, SG_CHUNK, SG_CHUNK).transpose(0, 2, 1, 3)
    w_pairs = w_pairs.reshape(N_SG_GROUPS // 2, SG_CHUNK, 2 * SG_CHUNK).astype(BF16)
    bias_full = jnp.repeat(sg_b.T, SG_GROUP_DIM, axis=1)
    row = lambda width: pl.BlockSpec((None, MIX_TILE, width), lambda b, i: (b, i, 0))
    full_seq = lambda arr: pl.BlockSpec((None, seq, arr.shape[-1]), lambda b, i: (b, 0, 0))
    const = lambda shape: pl.BlockSpec(shape, lambda b, i: (0,) * len(shape))
    return pl.pallas_call(
        _even_mix_kernel,
        grid=(batch, seq // MIX_TILE),
        in_specs=[
            pl.BlockSpec(memory_space=pltpu.SMEM),
            row(ATTN_WIDTH), full_seq(k), full_seq(v), row(SG_WIDTH), row(SG_WIDTH), row(EVEN_GATE_WIDTH),
            row(D_MODEL), _mod_spec(mod, EVEN_LAYER, MOD_GATE),
            const((LANES, LANES)), const((2 * LANES, LANES)), const((1, SG_WIDTH)), const((1, SG_WIDTH)),
            const((N_SG_GROUPS // 2, SG_CHUNK, 2 * SG_CHUNK)), const((SG_CHUNK, SG_WIDTH)),
            _resident((EVEN_GATE_WIDTH, D_MODEL)), const(ln_g.shape), const(ln_b.shape),
        ],
        out_specs=row(D_MODEL),
        out_shape=jax.ShapeDtypeStruct((batch, seq, D_MODEL), F32),
        scratch_shapes=[pltpu.VMEM((MIX_TILE, EVEN_GATE_WIDTH), BF16),
                        pltpu.VMEM((EVEN_GATE_WIDTH, D_MODEL), BF16)],
        compiler_params=_params("arbitrary", "arbitrary"),
        name="even_mixer",
    )(sink, q, k, v, su, sv, g, x, mod, avg, jnp.concatenate([avg, avg], axis=0), sg_ln_g.reshape(1, -1), sg_ln_b.reshape(1, -1),
      w_pairs, bias_full, w_out, ln_g, ln_b)


def _odd_in_kernel(x_hbm, shift_ref, scale_ref, w_f32_ref, xr_ref, w_ref, x_ring, x_sem):
    tile_rows = x_ring.shape[1]
    tiles_per_batch = x_hbm.shape[1] // tile_rows
    n_steps = x_hbm.shape[0] * tiles_per_batch
    step = pl.program_id(0) * tiles_per_batch + pl.program_id(1)

    def tile_copy(s):
        src = x_hbm.at[s // tiles_per_batch, pl.ds((s % tiles_per_batch) * tile_rows, tile_rows), :]
        return pltpu.make_async_copy(src, x_ring.at[s % X_RING], x_sem.at[s % X_RING])

    @pl.when(step == 0)
    def _():
        for s in range(X_RING - 1):
            tile_copy(s).start()

    @pl.when(step + (X_RING - 1) < n_steps)
    def _():
        tile_copy(step + (X_RING - 1)).start()

    tile_copy(step).wait()
    x_ref = x_ring.at[step % X_RING]
    _cast_weight_once(w_f32_ref, w_ref)
    seg_len = xr_ref.shape[1]
    head_rows = seg_len * SEGS_PER_TILE
    xr_rows = xr_ref.reshape(RNN_HEADS * head_rows, LANES)
    halves = 2
    for r in range(halves):
        first = r * (SEGS_PER_TILE // halves)
        rows = slice(first * seg_len, (first + SEGS_PER_TILE // halves) * seg_len)
        h = (x_ref[rows, :] * (1.0 + _batch_row(scale_ref)) + _batch_row(shift_ref)).astype(BF16)
        p = jnp.dot(h, w_ref[...], preferred_element_type=F32)
        for s in range(SEGS_PER_TILE // halves):
            for hd in range(RNN_HEADS):
                xr_rows[pl.ds(hd * head_rows + first + s, seg_len, stride=SEGS_PER_TILE), :] = (
                    p[s * seg_len:(s + 1) * seg_len, hd * LANES:(hd + 1) * LANES])


def _odd_in_proj(x, mod, w_in):
    batch, seq, _ = x.shape
    seg_len = seq // N_SEG
    tile_rows = SEGS_PER_TILE * seg_len
    assert batch * (seq // tile_rows) >= X_RING - 1
    out_spec = pl.BlockSpec((None, RNN_HEADS, seg_len, SEGS_PER_TILE, RNN_HEAD_DIM), lambda b, i: (b, 0, 0, i, 0))
    out_shape = jax.ShapeDtypeStruct((batch, RNN_HEADS, seg_len, N_SEG, RNN_HEAD_DIM), F32)
    return pl.pallas_call(
        _odd_in_kernel,
        grid=(batch, seq // tile_rows),
        in_specs=[pl.BlockSpec(memory_space=pl.ANY),
                  _mod_spec(mod, ODD_LAYER, MOD_SHIFT), _mod_spec(mod, ODD_LAYER, MOD_SCALE),
                  _resident((D_MODEL, RNN_WIDTH), col_block=0)],
        out_specs=out_spec,
        out_shape=out_shape,
        scratch_shapes=[pltpu.VMEM((D_MODEL, RNN_WIDTH), BF16), pltpu.VMEM((X_RING, tile_rows, D_MODEL), F32),
                        pltpu.SemaphoreType.DMA((X_RING,))],
        compiler_params=_params("arbitrary", "arbitrary"),
        name="odd_in_proj",
    )(x, mod, mod, w_in)


def _rglru_kernel(x_ref, convw_ref, convb_ref, waug_ref, lam_ref, o_ref,
                  halo, a_f, b_f, a_b, b_b, edge):
    n_rows = x_ref.shape[0]
    seg_len = n_rows // N_SEG
    n_chunks = n_rows // SCAN_CHUNK
    left_rows = CONV_LEFT * N_SEG
    right_rows = (CONV_WIDTH - 1 - CONV_LEFT) * N_SEG
    seg = lax.broadcasted_iota(jnp.int32, (N_SEG, LANES), 0)

    def from_prev_segment(blk):
        return jnp.where(seg == 0, 0.0, pltpu.roll(blk, 1, 0))

    def from_next_segment(blk):
        return jnp.where(seg == N_SEG - 1, 0.0, pltpu.roll(blk, N_SEG - 1, 0))

    for t in range(CONV_LEFT):
        halo[0, pl.ds(t * N_SEG, N_SEG), :] = from_prev_segment(
            x_ref[pl.ds((seg_len - CONV_LEFT + t) * N_SEG, N_SEG), :])
    halo[0, pl.ds(left_rows, SCAN_CHUNK + right_rows), :] = x_ref[pl.ds(0, SCAN_CHUNK + right_rows), :]
    halo[1, pl.ds(0, left_rows + SCAN_CHUNK), :] = x_ref[pl.ds(n_rows - SCAN_CHUNK - left_rows, left_rows + SCAN_CHUNK), :]
    for t in range(CONV_WIDTH - 1 - CONV_LEFT):
        halo[1, pl.ds(left_rows + SCAN_CHUNK + t * N_SEG, N_SEG), :] = from_next_segment(
            x_ref[pl.ds(t * N_SEG, N_SEG), :])

    lam = lam_ref[...]
    softplus_neg = jnp.maximum(-lam, 0.0) + jnp.log1p(jnp.exp(-jnp.abs(lam)))
    half_rate = (0.5 * RG_LRU_C) * softplus_neg
    bias_lanes = jnp.broadcast_to(
        jnp.where(lax.broadcasted_iota(jnp.int32, (1, LANES), 1) < BIAS_ROWS, 1.0, 0.0).astype(BF16),
        (SCAN_CHUNK, LANES))

    def gates(base, tap):
        half_x = convb_ref[...] + sum(convw_ref[k:k + 1, :] * tap(k) for k in range(CONV_WIDTH))
        th = jnp.tanh(jnp.dot(jnp.concatenate([half_x.astype(BF16), bias_lanes], axis=1), waug_ref[...],
                              preferred_element_type=F32))
        for d, (a_ref, b_ref) in enumerate(((a_f, b_f), (a_b, b_b))):
            o = 2 * d * LANES
            neg_log_a = (th[:, o:o + LANES] + 1.0) * half_rate[d:d + 1, :]
            a = jnp.exp2(neg_log_a * (-LOG2_E))
            one_minus_a2 = jnp.tanh(neg_log_a) * (a * a + 1.0)
            root = one_minus_a2 * lax.rsqrt(jnp.maximum(one_minus_a2, F32_TINY))
            a_ref[pl.ds(base, SCAN_CHUNK), :] = a
            b_ref[pl.ds(base, SCAN_CHUNK), :] = root * ((th[:, o + LANES:o + 2 * LANES] + 1.0) * half_x)

    def inner_chunk(c, carry):
        base = pl.multiple_of(c * SCAN_CHUNK, SCAN_CHUNK)
        gates(base, lambda k: x_ref[pl.ds(base + (k - CONV_LEFT) * N_SEG, SCAN_CHUNK), :])
        return carry

    gates(0, lambda k: halo[0, pl.ds(k * N_SEG, SCAN_CHUNK), :])
    gates(n_rows - SCAN_CHUNK, lambda k: halo[1, pl.ds(k * N_SEG, SCAN_CHUNK), :])
    lax.fori_loop(1, n_chunks - 1, inner_chunk, 0, unroll=2)

    def scan_step(t, carry):
        h_f, p_f, h_b, p_b = carry
        rf = pl.ds(pl.multiple_of(t * N_SEG, N_SEG), N_SEG)
        rb = pl.ds(pl.multiple_of((seg_len - 1 - t) * N_SEG, N_SEG), N_SEG)
        a = a_f[rf, :]
        h_f = a * h_f + b_f[rf, :]
        p_f = a * p_f
        b_f[rf, :] = h_f
        a_f[rf, :] = p_f
        a = a_b[rb, :]
        h_b = a * h_b + b_b[rb, :]
        p_b = a * p_b
        b_b[rb, :] = h_b
        a_b[rb, :] = p_b
        return h_f, p_f, h_b, p_b

    zeros = jnp.zeros((N_SEG, LANES), F32)
    ones = jnp.ones((N_SEG, LANES), F32)
    h_f, p_f, h_b, p_b = lax.fori_loop(0, seg_len, scan_step, (zeros, ones, zeros, ones), unroll=True)

    edge[0] = h_f
    edge[1] = p_f
    edge[2] = h_b
    edge[3] = p_b
    enter_f = enter_b = jnp.zeros((1, LANES), F32)
    edge[4, 0:1, :] = enter_f
    edge[5, N_SEG - 1:N_SEG, :] = enter_b
    for s in range(1, N_SEG):
        enter_f = edge[0, s - 1:s, :] + edge[1, s - 1:s, :] * enter_f
        edge[4, s:s + 1, :] = enter_f
        r = N_SEG - 1 - s
        enter_b = edge[2, r + 1:r + 2, :] + edge[3, r + 1:r + 2, :] * enter_b
        edge[5, r:r + 1, :] = enter_b
    blocks = SCAN_CHUNK // N_SEG
    in_f = jnp.concatenate([edge[4]] * blocks, axis=0)
    in_b = jnp.concatenate([edge[5]] * blocks, axis=0)

    def out_chunk(c, carry):
        rows = pl.ds(pl.multiple_of(c * SCAN_CHUNK, SCAN_CHUNK), SCAN_CHUNK)
        o_ref[rows, :] = (b_f[rows, :] + a_f[rows, :] * in_f) + (b_b[rows, :] + a_b[rows, :] * in_b)
        return carry

    lax.fori_loop(0, n_rows // SCAN_CHUNK, out_chunk, 0, unroll=2)


def _rglru(xr, conv_w, conv_b, w_a, b_a, w_x, b_x, lam):
    batch, _, n_rows, _ = xr.shape
    w_cat = jnp.concatenate([w_a[0], w_x[0], w_a[1], w_x[1]], axis=-1).astype(BF16)
    per_head = lambda v: v.reshape(RNN_HEADS, 1, RNN_HEAD_DIM)
    half_b = 0.5 * jnp.concatenate([per_head(b_a[0]), per_head(b_x[0]), per_head(b_a[1]), per_head(b_x[1])],
                                   axis=-1)
    b_hi = half_b.astype(BF16)
    b_lo = (half_b - b_hi.astype(F32)).astype(BF16)
    w_aug = jnp.concatenate([w_cat, b_hi, b_lo,
                             jnp.zeros((RNN_HEADS, RNN_HEAD_DIM - BIAS_ROWS, 4 * RNN_HEAD_DIM), BF16)], axis=1)
    col = lambda rows: pl.BlockSpec((rows, RNN_HEAD_DIM), lambda b, h: (0, h))
    seq_col = pl.BlockSpec((None, None, n_rows, RNN_HEAD_DIM), lambda b, h: (b, h, 0, 0))
    halo_rows = SCAN_CHUNK + (CONV_WIDTH - 1) * N_SEG
    return pl.pallas_call(
        _rglru_kernel,
        grid=(batch, RNN_HEADS),
        in_specs=[seq_col, col(CONV_WIDTH), col(1),
                  pl.BlockSpec((None, 2 * RNN_HEAD_DIM, 4 * RNN_HEAD_DIM), lambda b, h: (h, 0, 0)),
                  col(2)],
        out_specs=seq_col,
        out_shape=jax.ShapeDtypeStruct((batch, RNN_HEADS, n_rows, RNN_HEAD_DIM), F32),
        scratch_shapes=[pltpu.VMEM((2, halo_rows, LANES), F32)]
                       + [pltpu.VMEM((n_rows, LANES), F32)] * 4
                       + [pltpu.VMEM((6, N_SEG, LANES), F32)],
        compiler_params=_params("parallel", "parallel"),
        name="rglru",
    )(xr, 0.5 * conv_w, 0.5 * conv_b.reshape(1, -1), w_aug, lam)


def _odd_out_kernel(y_ref, x_ref, shift_ref, scale_ref, gate_ref, wg_f32_ref, w_f32_ref, lng_ref, lnb_ref,
                    o_ref, wg_ref, w_ref):
    _cast_weight_once(wg_f32_ref, wg_ref)
    _cast_weight_once(w_f32_ref, w_ref)
    seg_len = y_ref.shape[1]
    head_rows = seg_len * SEGS_PER_TILE
    y_rows = y_ref.reshape(RNN_HEADS * head_rows, LANES)
    x = x_ref[...]
    h = (x * (1.0 + _batch_row(scale_ref)) + _batch_row(shift_ref)).astype(BF16)
    g_all = jnp.dot(h, wg_ref[...], preferred_element_type=F32)
    gated = []
    for s in range(SEGS_PER_TILE):
        y = jnp.concatenate([y_rows[pl.ds(hd * head_rows + s, seg_len, stride=SEGS_PER_TILE), :]
                             for hd in range(RNN_HEADS)], axis=1)
        g = g_all[s * seg_len:(s + 1) * seg_len, :]
        gated.append((y * _silu(g)).astype(BF16))
    out = jnp.dot(jnp.concatenate(gated, axis=0), w_ref[...], preferred_element_type=F32)
    o_ref[...] = _deepnorm_rows(x, _batch_row(gate_ref), out,
                                lng_ref[ODD_LAYER:ODD_LAYER + 1, :], lnb_ref[ODD_LAYER:ODD_LAYER + 1, :])


def _odd_out_proj(y, x, mod, w_in, w_out, ln_g, ln_b):
    batch, seq, _ = x.shape
    seg_len = seq // N_SEG
    tile_rows = SEGS_PER_TILE * seg_len
    const = lambda shape: pl.BlockSpec(shape, lambda b, i: (0,) * len(shape))
    return pl.pallas_call(
        _odd_out_kernel,
        grid=(batch, seq // tile_rows),
        in_specs=[pl.BlockSpec((None, RNN_HEADS, seg_len, SEGS_PER_TILE, RNN_HEAD_DIM),
                               lambda b, i: (b, 0, 0, i, 0)),
                  pl.BlockSpec((None, tile_rows, D_MODEL), lambda b, i: (b, i, 0)),
                  _mod_spec(mod, ODD_LAYER, MOD_SHIFT), _mod_spec(mod, ODD_LAYER, MOD_SCALE),
                  _mod_spec(mod, ODD_LAYER, MOD_GATE),
                  _resident((D_MODEL, RNN_WIDTH), col_block=1), _resident((RNN_WIDTH, D_MODEL)),
                  const(ln_g.shape), const(ln_b.shape)],
        out_specs=pl.BlockSpec((None, tile_rows, D_MODEL), lambda b, i: (b, i, 0)),
        out_shape=jax.ShapeDtypeStruct((batch, seq, D_MODEL), F32),
        scratch_shapes=[pltpu.VMEM((D_MODEL, RNN_WIDTH), BF16), pltpu.VMEM((RNN_WIDTH, D_MODEL), BF16)],
        compiler_params=_params("arbitrary", "arbitrary"),
        name="odd_out_proj",
    )(y, x, mod, mod, mod, w_in, w_out, ln_g, ln_b)


def kernel(x, c, positions, ada_w, ada_b, ln_g, ln_b, ev_w_in, ev_w_out, ev_sink, ev_sg_ln_g, ev_sg_ln_b,
           ev_sg_w, ev_sg_b, od_w_in, od_conv_w, od_conv_b, od_w_a, od_b_a, od_w_x, od_b_x, od_lam, od_w_out):
    batch, seq, d_model = x.shape
    seg_len = seq // N_SEG
    assert d_model == D_MODEL and seq % ROW_TILE == 0 and seq % MIX_TILE == 0 and seq == N_SEG * seg_len
    assert N_SEG % SEGS_PER_TILE == 0 and seq % SCAN_CHUNK == 0 and SCAN_CHUNK % N_SEG == 0
    assert ada_w.shape[0] == DEPTH == 2

    mod = _modulation(c, ada_w, ada_b)

    q, k, v, su, sv, g = _even_in_proj(x, mod, positions, ev_w_in[0])
    x1 = _even_mixer(q, k, v, su, sv, g, x, mod, ev_sink[0], ev_sg_ln_g[0], ev_sg_ln_b[0],
                     ev_sg_w[0], ev_sg_b[0], ev_w_out[0], ln_g, ln_b)

    xr = _odd_in_proj(x1, mod, od_w_in[0])
    y = _rglru(xr.reshape(batch, RNN_HEADS, seq, RNN_HEAD_DIM), od_conv_w[0], od_conv_b[0], od_w_a[0],
               od_b_a[0], od_w_x[0], od_b_x[0], od_lam[0])
    return _odd_out_proj(y.reshape(batch, RNN_HEADS, seg_len, N_SEG, RNN_HEAD_DIM), x1, mod, od_w_in[0],
                         od_w_out[0], ln_g, ln_b)
```
